```python
import jax, jax.numpy as jnp
from jax import lax
import numpy as np

D_MODEL = 1024
BATCH = 8
SEQ = 8192
DEPTH = 2

EPS = 1e-6
LN_EPS = 1e-5
CONV_WIDTH = 512
CONV_K = 31
N_HEADS = 8
QK_NOPE_DIM = 64
QK_ROPE_DIM = 32
V_HEAD_DIM = 64
Q_LORA_RANK = 256
KV_LORA_RANK = 128
ROPE_BASE = 10000.0
Q_BLOCK = 128
ATTN_WIDTH = N_HEADS * V_HEAD_DIM
IN_EVEN = 2 * CONV_WIDTH + Q_LORA_RANK + KV_LORA_RANK + QK_ROPE_DIM
SPLIT_EVEN = [CONV_WIDTH, 2 * CONV_WIDTH, 2 * CONV_WIDTH + Q_LORA_RANK,
              2 * CONV_WIDTH + Q_LORA_RANK + KV_LORA_RANK]
MIX_EVEN = CONV_WIDTH + ATTN_WIDTH
SSM_WIDTH = 512
SSM_GROUP = 16
SSM_GROUPS = SSM_WIDTH // SSM_GROUP
SSM_STATE = 64
DT_MIN = 0.001
DT_MAX = 0.1
D_FF = 2816
FFN_K = 3

kernel_name = "hybrid_conv_mla_s5_convffn"


def rms_norm(x, g):
    xf = x.astype(jnp.float32)
    y = xf * lax.rsqrt(jnp.mean(xf * xf, axis=-1, keepdims=True) + EPS)
    return (y * g.astype(jnp.float32)).astype(x.dtype)


def layer_norm(x, g, b):
    xf = x.astype(jnp.float32)
    mu = jnp.mean(xf, axis=-1, keepdims=True)
    var = jnp.mean(jnp.square(xf - mu), axis=-1, keepdims=True)
    y = (xf - mu) * lax.rsqrt(var + LN_EPS)
    return (y * g.astype(jnp.float32) + b.astype(jnp.float32)).astype(x.dtype)


def causal_dwconv(u, w, b):
    k, c = w.shape
    y = lax.conv_general_dilated(u, w[:, None, :].astype(u.dtype), window_strides=(1,),
                                 padding=[(k - 1, 0)],
                                 dimension_numbers=('NWC', 'WIO', 'NWC'),
                                 feature_group_count=c)
    return y + b.astype(u.dtype)


def rope(t, pos):
    half = QK_ROPE_DIM // 2
    inv = ROPE_BASE ** (-jnp.arange(half, dtype=jnp.float32) / half)
    ang = pos.astype(jnp.float32)[:, None] * inv[None, :]
    cos, sin = jnp.cos(ang)[:, None, :], jnp.sin(ang)[:, None, :]
    tf = t.astype(jnp.float32)
    t1, t2 = tf[..., :half], tf[..., half:]
    return jnp.concatenate([t1 * cos - t2 * sin, t1 * sin + t2 * cos], axis=-1).astype(t.dtype)


def mla_attention(q_nope, q_rope, k_nope, k_rope, v):
    b, s, h, _ = q_nope.shape
    nb = s // Q_BLOCK
    scale = (QK_NOPE_DIM + QK_ROPE_DIM) ** -0.5
    k_pos = jnp.arange(s)

    def blocks(t):
        return jnp.moveaxis(t.reshape((b, nb, Q_BLOCK) + t.shape[2:]), 1, 0)

    def one_block(args):
        qn, qr, i = args
        scores = (jnp.einsum('bqhd,bkhd->bhqk', qn, k_nope).astype(jnp.float32)
                  + jnp.einsum('bqhr,bkr->bhqk', qr, k_rope).astype(jnp.float32)) * scale
        q_pos = i * Q_BLOCK + jnp.arange(Q_BLOCK)
        mask = k_pos[None, :] <= q_pos[:, None]
        scores = jnp.where(mask[None, None], scores, -jnp.inf)
        p = jax.nn.softmax(scores, axis=-1).astype(v.dtype)
        return jnp.einsum('bhqk,bkhd->bqhd', p, v)

    out = lax.map(one_block, (blocks(q_nope), blocks(q_rope), jnp.arange(nb)))
    return jnp.moveaxis(out, 0, 1).reshape(b, s, h * V_HEAD_DIM)


def conv_attn_mixer(xn, w_in, conv_w, conv_b, conv_ln_g, conv_ln_b,
                    q_norm, kv_norm, w_uq, w_ukv, w_out):
    b, s, _ = xn.shape
    h = xn @ w_in
    glu_a, glu_g, c_q, c_kv, k_r = jnp.split(h, SPLIT_EVEN, axis=-1)
    u = glu_a * jax.nn.sigmoid(glu_g)
    u = causal_dwconv(u, conv_w, conv_b)
    u = jax.nn.silu(layer_norm(u, conv_ln_g, conv_ln_b))
    pos = jnp.arange(s)
    q = (rms_norm(c_q, q_norm) @ w_uq).reshape(b, s, N_HEADS, QK_NOPE_DIM + QK_ROPE_DIM)
    q_nope, q_rope = q[..., :QK_NOPE_DIM], rope(q[..., QK_NOPE_DIM:], pos)
    kv = (rms_norm(c_kv, kv_norm) @ w_ukv).reshape(b, s, N_HEADS, QK_NOPE_DIM + V_HEAD_DIM)
    k_nope, v = kv[..., :QK_NOPE_DIM], kv[..., QK_NOPE_DIM:]
    k_rope = rope(k_r[:, :, None, :], pos)[:, :, 0, :]
    attn = mla_attention(q_nope, q_rope, k_nope, k_rope, v)
    return jnp.concatenate([u, attn], axis=-1) @ w_out


def s5_mixer(xn, w_in, log_dt, a_re, a_im, b_re, b_im, c_re, c_im, d_skip, w_glu, b_glu):
    f32 = jnp.float32
    bsz, s, _ = xn.shape
    u = (xn @ w_in).astype(f32).reshape(bsz, s, SSM_GROUPS, SSM_GROUP)
    dt = jnp.exp(log_dt.astype(f32))[:, None]
    ar, ai = a_re.astype(f32), a_im.astype(f32)
    mag = jnp.exp(ar * dt)
    lb_re, lb_im = mag * jnp.cos(ai * dt), mag * jnp.sin(ai * dt)
    den = ar * ar + ai * ai
    nr, ni = lb_re - 1.0, lb_im
    f_re = (nr * ar + ni * ai) / den
    f_im = (ni * ar - nr * ai) / den
    br, bi = b_re.astype(f32), b_im.astype(f32)
    bb_re = f_re[..., None] * br - f_im[..., None] * bi
    bb_im = f_re[..., None] * bi + f_im[..., None] * br
    bu_re = jnp.einsum('bsgc,gpc->bsgp', u, bb_re)
    bu_im = jnp.einsum('bsgc,gpc->bsgp', u, bb_im)
    lam_re = jnp.broadcast_to(lb_re, bu_re.shape)
    lam_im = jnp.broadcast_to(lb_im, bu_re.shape)

    def combine(e1, e2):
        a1r, a1i, b1r, b1i = e1
        a2r, a2i, b2r, b2i = e2
        return (a2r * a1r - a2i * a1i, a2r * a1i + a2i * a1r,
                a2r * b1r - a2i * b1i + b2r, a2r * b1i + a2i * b1r + b2i)

    _, _, x_re, x_im = lax.associative_scan(combine, (lam_re, lam_im, bu_re, bu_im), axis=1)
    y = (jnp.einsum('gcp,bsgp->bsgc', c_re.astype(f32), x_re)
         - jnp.einsum('gcp,bsgp->bsgc', c_im.astype(f32), x_im)
         + d_skip.astype(f32).reshape(SSM_GROUPS, SSM_GROUP) * u)
    y = jax.nn.gelu(y.reshape(bsz, s, SSM_WIDTH)).astype(xn.dtype)
    z = y @ w_glu + b_glu
    return z[..., :D_MODEL] * jax.nn.sigmoid(z[..., D_MODEL:])


def conv_ffn(xn, w_up, conv_w, conv_b, w_down):
    h = causal_dwconv(xn @ w_up, conv_w, conv_b)
    return (jax.nn.silu(h[..., :D_FF]) * h[..., D_FF:]) @ w_down


def _fwd_setup_inputs(seed: int = 0) -> dict:
    key = jax.random.key(seed)
    keys = iter(jax.random.split(key, 64))
    f32 = jnp.float32

    def nrm(shape, scale):
        return jax.random.normal(next(keys), shape, f32) * scale

    def gain(n):
        return 1.0 + nrm((n,), 0.01)

    d = D_MODEL
    inp = {}
    inp["x"] = nrm((BATCH, SEQ, d), 1.0)
    inp["l0_mix_norm"] = gain(d)
    inp["l0_w_in"] = nrm((d, IN_EVEN), d ** -0.5)
    inp["l0_conv_w"] = nrm((CONV_K, CONV_WIDTH), CONV_K ** -0.5)
    inp["l0_conv_b"] = nrm((CONV_WIDTH,), 0.01)
    inp["l0_conv_ln_g"] = gain(CONV_WIDTH)
    inp["l0_conv_ln_b"] = nrm((CONV_WIDTH,), 0.01)
    inp["l0_q_norm"] = gain(Q_LORA_RANK)
    inp["l0_kv_norm"] = gain(KV_LORA_RANK)
    inp["l0_w_uq"] = nrm((Q_LORA_RANK, N_HEADS * (QK_NOPE_DIM + QK_ROPE_DIM)), Q_LORA_RANK ** -0.5)
    inp["l0_w_ukv"] = nrm((KV_LORA_RANK, N_HEADS * (QK_NOPE_DIM + V_HEAD_DIM)), KV_LORA_RANK ** -0.5)
    inp["l0_w_out"] = nrm((MIX_EVEN, d), MIX_EVEN ** -0.5)
    inp["l0_ffn_norm"] = gain(d)
    inp["l0_w_up"] = nrm((d, 2 * D_FF), d ** -0.5)
    inp["l0_ffn_conv_w"] = nrm((FFN_K, 2 * D_FF), FFN_K ** -0.5)
    inp["l0_ffn_conv_b"] = nrm((2 * D_FF,), 0.01)
    inp["l0_w_down"] = nrm((D_FF, d), D_FF ** -0.5)
    inp["l1_mix_norm"] = gain(d)
    inp["l1_w_in"] = nrm((d, SSM_WIDTH), d ** -0.5)
    inp["l1_log_dt"] = jax.random.uniform(next(keys), (SSM_GROUPS,), f32,
                                          float(np.log(DT_MIN)), float(np.log(DT_MAX)))
    inp["l1_a_re"] = -0.5 + nrm((SSM_GROUPS, SSM_STATE), 0.01)
    inp["l1_a_im"] = (jnp.pi * jnp.arange(SSM_STATE, dtype=f32))[None, :] + nrm((SSM_GROUPS, SSM_STATE), 0.01)
    inp["l1_b_re"] = nrm((SSM_GROUPS, SSM_STATE, SSM_GROUP), (2 * SSM_GROUP) ** -0.5)
    inp["l1_b_im"] = nrm((SSM_GROUPS, SSM_STATE, SSM_GROUP), (2 * SSM_GROUP) ** -0.5)
    inp["l1_c_re"] = nrm((SSM_GROUPS, SSM_GROUP, SSM_STATE), SSM_STATE ** -0.5)
    inp["l1_c_im"] = nrm((SSM_GROUPS, SSM_GROUP, SSM_STATE), SSM_STATE ** -0.5)
    inp["l1_d"] = nrm((SSM_WIDTH,), 1.0)
    inp["l1_w_glu"] = nrm((SSM_WIDTH, 2 * d), SSM_WIDTH ** -0.5)
    inp["l1_b_glu"] = nrm((2 * d,), 0.01)
    inp["l1_ffn_norm"] = gain(d)
    inp["l1_w_up"] = nrm((d, 2 * D_FF), d ** -0.5)
    inp["l1_ffn_conv_w"] = nrm((FFN_K, 2 * D_FF), FFN_K ** -0.5)
    inp["l1_ffn_conv_b"] = nrm((2 * D_FF,), 0.01)
    inp["l1_w_down"] = nrm((D_FF, d), D_FF ** -0.5)
    inp["final_norm"] = gain(d)
    return inp


def _fwd_reference(x,
              l0_mix_norm, l0_w_in, l0_conv_w, l0_conv_b, l0_conv_ln_g, l0_conv_ln_b,
              l0_q_norm, l0_kv_norm, l0_w_uq, l0_w_ukv, l0_w_out,
              l0_ffn_norm, l0_w_up, l0_ffn_conv_w, l0_ffn_conv_b, l0_w_down,
              l1_mix_norm, l1_w_in, l1_log_dt, l1_a_re, l1_a_im, l1_b_re, l1_b_im,
              l1_c_re, l1_c_im, l1_d, l1_w_glu, l1_b_glu,
              l1_ffn_norm, l1_w_up, l1_ffn_conv_w, l1_ffn_conv_b, l1_w_down,
              final_norm):
    layers = [
        (conv_attn_mixer,
         (l0_w_in, l0_conv_w, l0_conv_b, l0_conv_ln_g, l0_conv_ln_b,
          l0_q_norm, l0_kv_norm, l0_w_uq, l0_w_ukv, l0_w_out),
         l0_mix_norm, (l0_w_up, l0_ffn_conv_w, l0_ffn_conv_b, l0_w_down), l0_ffn_norm),
        (s5_mixer,
         (l1_w_in, l1_log_dt, l1_a_re, l1_a_im, l1_b_re, l1_b_im,
          l1_c_re, l1_c_im, l1_d, l1_w_glu, l1_b_glu),
         l1_mix_norm, (l1_w_up, l1_ffn_conv_w, l1_ffn_conv_b, l1_w_down), l1_ffn_norm),
    ]
    for i in range(DEPTH):
        mixer, mix_params, mix_g, ffn_params, ffn_g = layers[i]
        x = x + mixer(rms_norm(x, mix_g), *mix_params)
        x = x + conv_ffn(rms_norm(x, ffn_g), *ffn_params)
    return rms_norm(x, final_norm)


import jax as _jax
import jax.numpy as _jnp

TWIN_FORMAT = 'train_step'
FWD_PARAMS = ['x', 'l0_mix_norm', 'l0_w_in', 'l0_conv_w', 'l0_conv_b', 'l0_conv_ln_g', 'l0_conv_ln_b', 'l0_q_norm', 'l0_kv_norm', 'l0_w_uq', 'l0_w_ukv', 'l0_w_out', 'l0_ffn_norm', 'l0_w_up', 'l0_ffn_conv_w', 'l0_ffn_conv_b', 'l0_w_down', 'l1_mix_norm', 'l1_w_in', 'l1_log_dt', 'l1_a_re', 'l1_a_im', 'l1_b_re', 'l1_b_im', 'l1_c_re', 'l1_c_im', 'l1_d', 'l1_w_glu', 'l1_b_glu', 'l1_ffn_norm', 'l1_w_up', 'l1_ffn_conv_w', 'l1_ffn_conv_b', 'l1_w_down', 'final_norm']
TWIN_WEIGHTS = ['l0_mix_norm', 'l0_w_in', 'l0_conv_w', 'l0_conv_b', 'l0_conv_ln_g', 'l0_conv_ln_b', 'l0_q_norm', 'l0_kv_norm', 'l0_w_uq', 'l0_w_ukv', 'l0_w_out', 'l0_ffn_norm', 'l0_w_up', 'l0_ffn_conv_w', 'l0_ffn_conv_b', 'l0_w_down', 'l1_mix_norm', 'l1_w_in', 'l1_log_dt', 'l1_a_re', 'l1_a_im', 'l1_b_re', 'l1_b_im', 'l1_c_re', 'l1_c_im', 'l1_d', 'l1_w_glu', 'l1_b_glu', 'l1_ffn_norm', 'l1_w_up', 'l1_ffn_conv_w', 'l1_ffn_conv_b', 'l1_w_down', 'final_norm']
TWIN_DIFF_INPUT = 'x'
TWIN_INPUTS = ['x', 'l0_mix_norm', 'l0_w_in', 'l0_conv_w', 'l0_conv_b', 'l0_conv_ln_g', 'l0_conv_ln_b', 'l0_q_norm', 'l0_kv_norm', 'l0_w_uq', 'l0_w_ukv', 'l0_w_out', 'l0_ffn_norm', 'l0_w_up', 'l0_ffn_conv_w', 'l0_ffn_conv_b', 'l0_w_down', 'l1_mix_norm', 'l1_w_in', 'l1_log_dt', 'l1_a_re', 'l1_a_im', 'l1_b_re', 'l1_b_im', 'l1_c_re', 'l1_c_im', 'l1_d', 'l1_w_glu', 'l1_b_glu', 'l1_ffn_norm', 'l1_w_up', 'l1_ffn_conv_w', 'l1_ffn_conv_b', 'l1_w_down', 'final_norm', 'loss_target', 'm_l0_mix_norm', 'm_l0_w_in', 'm_l0_conv_w', 'm_l0_conv_b', 'm_l0_conv_ln_g', 'm_l0_conv_ln_b', 'm_l0_q_norm', 'm_l0_kv_norm', 'm_l0_w_uq', 'm_l0_w_ukv', 'm_l0_w_out', 'm_l0_ffn_norm', 'm_l0_w_up', 'm_l0_ffn_conv_w', 'm_l0_ffn_conv_b', 'm_l0_w_down', 'm_l1_mix_norm', 'm_l1_w_in', 'm_l1_log_dt', 'm_l1_a_re', 'm_l1_a_im', 'm_l1_b_re', 'm_l1_b_im', 'm_l1_c_re', 'm_l1_c_im', 'm_l1_d', 'm_l1_w_glu', 'm_l1_b_glu', 'm_l1_ffn_norm', 'm_l1_w_up', 'm_l1_ffn_conv_w', 'm_l1_ffn_conv_b', 'm_l1_w_down', 'm_final_norm', 'v_l0_mix_norm', 'v_l0_w_in', 'v_l0_conv_w', 'v_l0_conv_b', 'v_l0_conv_ln_g', 'v_l0_conv_ln_b', 'v_l0_q_norm', 'v_l0_kv_norm', 'v_l0_w_uq', 'v_l0_w_ukv', 'v_l0_w_out', 'v_l0_ffn_norm', 'v_l0_w_up', 'v_l0_ffn_conv_w', 'v_l0_ffn_conv_b', 'v_l0_w_down', 'v_l1_mix_norm', 'v_l1_w_in', 'v_l1_log_dt', 'v_l1_a_re', 'v_l1_a_im', 'v_l1_b_re', 'v_l1_b_im', 'v_l1_c_re', 'v_l1_c_im', 'v_l1_d', 'v_l1_w_glu', 'v_l1_b_glu', 'v_l1_ffn_norm', 'v_l1_w_up', 'v_l1_ffn_conv_w', 'v_l1_ffn_conv_b', 'v_l1_w_down', 'v_final_norm']
TWIN_OUTPUTS = ['loss', 'grad_x', 'grad_l0_mix_norm', 'grad_l0_w_in', 'grad_l0_conv_w', 'grad_l0_conv_b', 'grad_l0_conv_ln_g', 'grad_l0_conv_ln_b', 'grad_l0_q_norm', 'grad_l0_kv_norm', 'grad_l0_w_uq', 'grad_l0_w_ukv', 'grad_l0_w_out', 'grad_l0_ffn_norm', 'grad_l0_w_up', 'grad_l0_ffn_conv_w', 'grad_l0_ffn_conv_b', 'grad_l0_w_down', 'grad_l1_mix_norm', 'grad_l1_w_in', 'grad_l1_log_dt', 'grad_l1_a_re', 'grad_l1_a_im', 'grad_l1_b_re', 'grad_l1_b_im', 'grad_l1_c_re', 'grad_l1_c_im', 'grad_l1_d', 'grad_l1_w_glu', 'grad_l1_b_glu', 'grad_l1_ffn_norm', 'grad_l1_w_up', 'grad_l1_ffn_conv_w', 'grad_l1_ffn_conv_b', 'grad_l1_w_down', 'grad_final_norm', 'delta_l0_mix_norm', 'delta_l0_w_in', 'delta_l0_conv_w', 'delta_l0_conv_b', 'delta_l0_conv_ln_g', 'delta_l0_conv_ln_b', 'delta_l0_q_norm', 'delta_l0_kv_norm', 'delta_l0_w_uq', 'delta_l0_w_ukv', 'delta_l0_w_out', 'delta_l0_ffn_norm', 'delta_l0_w_up', 'delta_l0_ffn_conv_w', 'delta_l0_ffn_conv_b', 'delta_l0_w_down', 'delta_l1_mix_norm', 'delta_l1_w_in', 'delta_l1_log_dt', 'delta_l1_a_re', 'delta_l1_a_im', 'delta_l1_b_re', 'delta_l1_b_im', 'delta_l1_c_re', 'delta_l1_c_im', 'delta_l1_d', 'delta_l1_w_glu', 'delta_l1_b_glu', 'delta_l1_ffn_norm', 'delta_l1_w_up', 'delta_l1_ffn_conv_w', 'delta_l1_ffn_conv_b', 'delta_l1_w_down', 'delta_final_norm', 'new_m_l0_mix_norm', 'new_m_l0_w_in', 'new_m_l0_conv_w', 'new_m_l0_conv_b', 'new_m_l0_conv_ln_g', 'new_m_l0_conv_ln_b', 'new_m_l0_q_norm', 'new_m_l0_kv_norm', 'new_m_l0_w_uq', 'new_m_l0_w_ukv', 'new_m_l0_w_out', 'new_m_l0_ffn_norm', 'new_m_l0_w_up', 'new_m_l0_ffn_conv_w', 'new_m_l0_ffn_conv_b', 'new_m_l0_w_down', 'new_m_l1_mix_norm', 'new_m_l1_w_in', 'new_m_l1_log_dt', 'new_m_l1_a_re', 'new_m_l1_a_im', 'new_m_l1_b_re', 'new_m_l1_b_im', 'new_m_l1_c_re', 'new_m_l1_c_im', 'new_m_l1_d', 'new_m_l1_w_glu', 'new_m_l1_b_glu', 'new_m_l1_ffn_norm', 'new_m_l1_w_up', 'new_m_l1_ffn_conv_w', 'new_m_l1_ffn_conv_b', 'new_m_l1_w_down', 'new_m_final_norm', 'new_v_l0_mix_norm', 'new_v_l0_w_in', 'new_v_l0_conv_w', 'new_v_l0_conv_b', 'new_v_l0_conv_ln_g', 'new_v_l0_conv_ln_b', 'new_v_l0_q_norm', 'new_v_l0_kv_norm', 'new_v_l0_w_uq', 'new_v_l0_w_ukv', 'new_v_l0_w_out', 'new_v_l0_ffn_norm', 'new_v_l0_w_up', 'new_v_l0_ffn_conv_w', 'new_v_l0_ffn_conv_b', 'new_v_l0_w_down', 'new_v_l1_mix_norm', 'new_v_l1_w_in', 'new_v_l1_log_dt', 'new_v_l1_a_re', 'new_v_l1_a_im', 'new_v_l1_b_re', 'new_v_l1_b_im', 'new_v_l1_c_re', 'new_v_l1_c_im', 'new_v_l1_d', 'new_v_l1_w_glu', 'new_v_l1_b_glu', 'new_v_l1_ffn_norm', 'new_v_l1_w_up', 'new_v_l1_ffn_conv_w', 'new_v_l1_ffn_conv_b', 'new_v_l1_w_down', 'new_v_final_norm']
TWIN_LEAF_KINDS = {'loss': 'loss', 'grad_x': 'grad_x', 'grad_l0_mix_norm': 'grad_w', 'grad_l0_w_in': 'grad_w', 'grad_l0_conv_w': 'grad_w', 'grad_l0_conv_b': 'grad_w', 'grad_l0_conv_ln_g': 'grad_w', 'grad_l0_conv_ln_b': 'grad_w', 'grad_l0_q_norm': 'grad_w', 'grad_l0_kv_norm': 'grad_w', 'grad_l0_w_uq': 'grad_w', 'grad_l0_w_ukv': 'grad_w', 'grad_l0_w_out': 'grad_w', 'grad_l0_ffn_norm': 'grad_w', 'grad_l0_w_up': 'grad_w', 'grad_l0_ffn_conv_w': 'grad_w', 'grad_l0_ffn_conv_b': 'grad_w', 'grad_l0_w_down': 'grad_w', 'grad_l1_mix_norm': 'grad_w', 'grad_l1_w_in': 'grad_w', 'grad_l1_log_dt': 'grad_w', 'grad_l1_a_re': 'grad_w', 'grad_l1_a_im': 'grad_w', 'grad_l1_b_re': 'grad_w', 'grad_l1_b_im': 'grad_w', 'grad_l1_c_re': 'grad_w', 'grad_l1_c_im': 'grad_w', 'grad_l1_d': 'grad_w', 'grad_l1_w_glu': 'grad_w', 'grad_l1_b_glu': 'grad_w', 'grad_l1_ffn_norm': 'grad_w', 'grad_l1_w_up': 'grad_w', 'grad_l1_ffn_conv_w': 'grad_w', 'grad_l1_ffn_conv_b': 'grad_w', 'grad_l1_w_down': 'grad_w', 'grad_final_norm': 'grad_w', 'delta_l0_mix_norm': 'delta_w', 'delta_l0_w_in': 'delta_w', 'delta_l0_conv_w': 'delta_w', 'delta_l0_conv_b': 'delta_w', 'delta_l0_conv_ln_g': 'delta_w', 'delta_l0_conv_ln_b': 'delta_w', 'delta_l0_q_norm': 'delta_w', 'delta_l0_kv_norm': 'delta_w', 'delta_l0_w_uq': 'delta_w', 'delta_l0_w_ukv': 'delta_w', 'delta_l0_w_out': 'delta_w', 'delta_l0_ffn_norm': 'delta_w', 'delta_l0_w_up': 'delta_w', 'delta_l0_ffn_conv_w': 'delta_w', 'delta_l0_ffn_conv_b': 'delta_w', 'delta_l0_w_down': 'delta_w', 'delta_l1_mix_norm': 'delta_w', 'delta_l1_w_in': 'delta_w', 'delta_l1_log_dt': 'delta_w', 'delta_l1_a_re': 'delta_w', 'delta_l1_a_im': 'delta_w', 'delta_l1_b_re': 'delta_w', 'delta_l1_b_im': 'delta_w', 'delta_l1_c_re': 'delta_w', 'delta_l1_c_im': 'delta_w', 'delta_l1_d': 'delta_w', 'delta_l1_w_glu': 'delta_w', 'delta_l1_b_glu': 'delta_w', 'delta_l1_ffn_norm': 'delta_w', 'delta_l1_w_up': 'delta_w', 'delta_l1_ffn_conv_w': 'delta_w', 'delta_l1_ffn_conv_b': 'delta_w', 'delta_l1_w_down': 'delta_w', 'delta_final_norm': 'delta_w', 'new_m_l0_mix_norm': 'new_m', 'new_m_l0_w_in': 'new_m', 'new_m_l0_conv_w': 'new_m', 'new_m_l0_conv_b': 'new_m', 'new_m_l0_conv_ln_g': 'new_m', 'new_m_l0_conv_ln_b': 'new_m', 'new_m_l0_q_norm': 'new_m', 'new_m_l0_kv_norm': 'new_m', 'new_m_l0_w_uq': 'new_m', 'new_m_l0_w_ukv': 'new_m', 'new_m_l0_w_out': 'new_m', 'new_m_l0_ffn_norm': 'new_m', 'new_m_l0_w_up': 'new_m', 'new_m_l0_ffn_conv_w': 'new_m', 'new_m_l0_ffn_conv_b': 'new_m', 'new_m_l0_w_down': 'new_m', 'new_m_l1_mix_norm': 'new_m', 'new_m_l1_w_in': 'new_m', 'new_m_l1_log_dt': 'new_m', 'new_m_l1_a_re': 'new_m', 'new_m_l1_a_im': 'new_m', 'new_m_l1_b_re': 'new_m', 'new_m_l1_b_im': 'new_m', 'new_m_l1_c_re': 'new_m', 'new_m_l1_c_im': 'new_m', 'new_m_l1_d': 'new_m', 'new_m_l1_w_glu': 'new_m', 'new_m_l1_b_glu': 'new_m', 'new_m_l1_ffn_norm': 'new_m', 'new_m_l1_w_up': 'new_m', 'new_m_l1_ffn_conv_w': 'new_m', 'new_m_l1_ffn_conv_b': 'new_m', 'new_m_l1_w_down': 'new_m', 'new_m_final_norm': 'new_m', 'new_v_l0_mix_norm': 'new_v', 'new_v_l0_w_in': 'new_v', 'new_v_l0_conv_w': 'new_v', 'new_v_l0_conv_b': 'new_v', 'new_v_l0_conv_ln_g': 'new_v', 'new_v_l0_conv_ln_b': 'new_v', 'new_v_l0_q_norm': 'new_v', 'new_v_l0_kv_norm': 'new_v', 'new_v_l0_w_uq': 'new_v', 'new_v_l0_w_ukv': 'new_v', 'new_v_l0_w_out': 'new_v', 'new_v_l0_ffn_norm': 'new_v', 'new_v_l0_w_up': 'new_v', 'new_v_l0_ffn_conv_w': 'new_v', 'new_v_l0_ffn_conv_b': 'new_v', 'new_v_l0_w_down': 'new_v', 'new_v_l1_mix_norm': 'new_v', 'new_v_l1_w_in': 'new_v', 'new_v_l1_log_dt': 'new_v', 'new_v_l1_a_re': 'new_v', 'new_v_l1_a_im': 'new_v', 'new_v_l1_b_re': 'new_v', 'new_v_l1_b_im': 'new_v', 'new_v_l1_c_re': 'new_v', 'new_v_l1_c_im': 'new_v', 'new_v_l1_d': 'new_v', 'new_v_l1_w_glu': 'new_v', 'new_v_l1_b_glu': 'new_v', 'new_v_l1_ffn_norm': 'new_v', 'new_v_l1_w_up': 'new_v', 'new_v_l1_ffn_conv_w': 'new_v', 'new_v_l1_ffn_conv_b': 'new_v', 'new_v_l1_w_down': 'new_v', 'new_v_final_norm': 'new_v'}


def _forward(args):
    return _fwd_reference(*[args[k] for k in FWD_PARAMS])


def _output_shape():
    def fwd():
        inp = _fwd_setup_inputs(0)
        return _fwd_reference(*[inp[k] for k in FWD_PARAMS])
    out = _jax.eval_shape(fwd)
    return out.shape, out.dtype

N_MICROBATCH = 1
ADAM_LR = 0.001
ADAM_B1 = 0.9
ADAM_B2 = 0.999
ADAM_EPS = 1e-08
ADAM_WD = 0.01
ADAM_STEP = 10
PER_EXAMPLE_BATCH_AXIS = {'x': 0, 'loss_target': 0}
SHARED_INPUTS = []
_WEIGHT_DTYPES = {'l0_mix_norm': _jnp.float32, 'l0_w_in': _jnp.float32, 'l0_conv_w': _jnp.float32, 'l0_conv_b': _jnp.float32, 'l0_conv_ln_g': _jnp.float32, 'l0_conv_ln_b': _jnp.float32, 'l0_q_norm': _jnp.float32, 'l0_kv_norm': _jnp.float32, 'l0_w_uq': _jnp.float32, 'l0_w_ukv': _jnp.float32, 'l0_w_out': _jnp.float32, 'l0_ffn_norm': _jnp.float32, 'l0_w_up': _jnp.float32, 'l0_ffn_conv_w': _jnp.float32, 'l0_ffn_conv_b': _jnp.float32, 'l0_w_down': _jnp.float32, 'l1_mix_norm': _jnp.float32, 'l1_w_in': _jnp.float32, 'l1_log_dt': _jnp.float32, 'l1_a_re': _jnp.float32, 'l1_a_im': _jnp.float32, 'l1_b_re': _jnp.float32, 'l1_b_im': _jnp.float32, 'l1_c_re': _jnp.float32, 'l1_c_im': _jnp.float32, 'l1_d': _jnp.float32, 'l1_w_glu': _jnp.float32, 'l1_b_glu': _jnp.float32, 'l1_ffn_norm': _jnp.float32, 'l1_w_up': _jnp.float32, 'l1_ffn_conv_w': _jnp.float32, 'l1_ffn_conv_b': _jnp.float32, 'l1_w_down': _jnp.float32, 'final_norm': _jnp.float32}
MOMENT_SCALE = {'l0_mix_norm': 1.492044e-01, 'l0_w_in': 1.243546e-01, 'l0_conv_w': 1.765494e-01, 'l0_conv_b': 3.716635e-01, 'l0_conv_ln_g': 2.121399e-01, 'l0_conv_ln_b': 1.787636e-01, 'l0_q_norm': 7.209490e-02, 'l0_kv_norm': 1.559768e-01, 'l0_w_uq': 4.213037e-02, 'l0_w_ukv': 5.325034e-02, 'l0_w_out': 1.263720e-01, 'l0_ffn_norm': 2.081045e-01, 'l0_w_up': 8.267490e-02, 'l0_ffn_conv_w': 8.375524e-02, 'l0_ffn_conv_b': 8.289507e-02, 'l0_w_down': 1.361272e-01, 'l1_mix_norm': 8.313139e-02, 'l1_w_in': 1.126495e-01, 'l1_log_dt': 1.270366e+01, 'l1_a_re': 1.072710e-02, 'l1_a_im': 8.756847e-03, 'l1_b_re': 5.699295e-03, 'l1_b_im': 5.816674e-03, 'l1_c_re': 8.286954e-03, 'l1_c_im': 8.277600e-03, 'l1_d': 1.123296e-01, 'l1_w_glu': 5.652927e-02, 'l1_b_glu': 9.747325e-02, 'l1_ffn_norm': 1.584542e-01, 'l1_w_up': 6.649100e-02, 'l1_ffn_conv_w': 6.587838e-02, 'l1_ffn_conv_b': 6.556284e-02, 'l1_w_down': 1.080489e-01, 'final_norm': 6.392462e+01}


def _to_microbatches(a, axis):
    t = _jnp.moveaxis(a, axis, 0)
    t = t.reshape((N_MICROBATCH, t.shape[0] // N_MICROBATCH) + t.shape[1:])
    return _jnp.moveaxis(t, 1, axis + 1)


def setup_inputs(seed: int = 0) -> dict:
    inp = _fwd_setup_inputs(seed)
    key = _jax.random.fold_in(_jax.random.key(seed), 7919)
    shape, _ = _output_shape()
    out = dict(inp)
    out["loss_target"] = _jax.random.normal(_jax.random.fold_in(key, 0), shape, _jnp.float32)
    for i, name in enumerate(TWIN_WEIGHTS):
        w = inp[name].astype(_jnp.float32)
        if MOMENT_SCALE is None:
            s = _jnp.sqrt(_jnp.mean(_jnp.square(w)) + 1e-30)
        else:
            s = MOMENT_SCALE[name]
        km, kv = _jax.random.split(_jax.random.fold_in(key, i + 1))
        out[name] = w
        out["m_" + name] = s * _jax.random.normal(km, w.shape, _jnp.float32)
        out["v_" + name] = (s * s) * _jax.random.uniform(kv, w.shape, _jnp.float32, 0.5, 1.5)
    if N_MICROBATCH > 1:
        for name, axis in PER_EXAMPLE_BATCH_AXIS.items():
            out[name] = _to_microbatches(out[name], axis)
    return {'x': out['x'], 'l0_mix_norm': out['l0_mix_norm'], 'l0_w_in': out['l0_w_in'], 'l0_conv_w': out['l0_conv_w'], 'l0_conv_b': out['l0_conv_b'], 'l0_conv_ln_g': out['l0_conv_ln_g'], 'l0_conv_ln_b': out['l0_conv_ln_b'], 'l0_q_norm': out['l0_q_norm'], 'l0_kv_norm': out['l0_kv_norm'], 'l0_w_uq': out['l0_w_uq'], 'l0_w_ukv': out['l0_w_ukv'], 'l0_w_out': out['l0_w_out'], 'l0_ffn_norm': out['l0_ffn_norm'], 'l0_w_up': out['l0_w_up'], 'l0_ffn_conv_w': out['l0_ffn_conv_w'], 'l0_ffn_conv_b': out['l0_ffn_conv_b'], 'l0_w_down': out['l0_w_down'], 'l1_mix_norm': out['l1_mix_norm'], 'l1_w_in': out['l1_w_in'], 'l1_log_dt': out['l1_log_dt'], 'l1_a_re': out['l1_a_re'], 'l1_a_im': out['l1_a_im'], 'l1_b_re': out['l1_b_re'], 'l1_b_im': out['l1_b_im'], 'l1_c_re': out['l1_c_re'], 'l1_c_im': out['l1_c_im'], 'l1_d': out['l1_d'], 'l1_w_glu': out['l1_w_glu'], 'l1_b_glu': out['l1_b_glu'], 'l1_ffn_norm': out['l1_ffn_norm'], 'l1_w_up': out['l1_w_up'], 'l1_ffn_conv_w': out['l1_ffn_conv_w'], 'l1_ffn_conv_b': out['l1_ffn_conv_b'], 'l1_w_down': out['l1_w_down'], 'final_norm': out['final_norm'], 'loss_target': out['loss_target'], 'm_l0_mix_norm': out['m_l0_mix_norm'], 'm_l0_w_in': out['m_l0_w_in'], 'm_l0_conv_w': out['m_l0_conv_w'], 'm_l0_conv_b': out['m_l0_conv_b'], 'm_l0_conv_ln_g': out['m_l0_conv_ln_g'], 'm_l0_conv_ln_b': out['m_l0_conv_ln_b'], 'm_l0_q_norm': out['m_l0_q_norm'], 'm_l0_kv_norm': out['m_l0_kv_norm'], 'm_l0_w_uq': out['m_l0_w_uq'], 'm_l0_w_ukv': out['m_l0_w_ukv'], 'm_l0_w_out': out['m_l0_w_out'], 'm_l0_ffn_norm': out['m_l0_ffn_norm'], 'm_l0_w_up': out['m_l0_w_up'], 'm_l0_ffn_conv_w': out['m_l0_ffn_conv_w'], 'm_l0_ffn_conv_b': out['m_l0_ffn_conv_b'], 'm_l0_w_down': out['m_l0_w_down'], 'm_l1_mix_norm': out['m_l1_mix_norm'], 'm_l1_w_in': out['m_l1_w_in'], 'm_l1_log_dt': out['m_l1_log_dt'], 'm_l1_a_re': out['m_l1_a_re'], 'm_l1_a_im': out['m_l1_a_im'], 'm_l1_b_re': out['m_l1_b_re'], 'm_l1_b_im': out['m_l1_b_im'], 'm_l1_c_re': out['m_l1_c_re'], 'm_l1_c_im': out['m_l1_c_im'], 'm_l1_d': out['m_l1_d'], 'm_l1_w_glu': out['m_l1_w_glu'], 'm_l1_b_glu': out['m_l1_b_glu'], 'm_l1_ffn_norm': out['m_l1_ffn_norm'], 'm_l1_w_up': out['m_l1_w_up'], 'm_l1_ffn_conv_w': out['m_l1_ffn_conv_w'], 'm_l1_ffn_conv_b': out['m_l1_ffn_conv_b'], 'm_l1_w_down': out['m_l1_w_down'], 'm_final_norm': out['m_final_norm'], 'v_l0_mix_norm': out['v_l0_mix_norm'], 'v_l0_w_in': out['v_l0_w_in'], 'v_l0_conv_w': out['v_l0_conv_w'], 'v_l0_conv_b': out['v_l0_conv_b'], 'v_l0_conv_ln_g': out['v_l0_conv_ln_g'], 'v_l0_conv_ln_b': out['v_l0_conv_ln_b'], 'v_l0_q_norm': out['v_l0_q_norm'], 'v_l0_kv_norm': out['v_l0_kv_norm'], 'v_l0_w_uq': out['v_l0_w_uq'], 'v_l0_w_ukv': out['v_l0_w_ukv'], 'v_l0_w_out': out['v_l0_w_out'], 'v_l0_ffn_norm': out['v_l0_ffn_norm'], 'v_l0_w_up': out['v_l0_w_up'], 'v_l0_ffn_conv_w': out['v_l0_ffn_conv_w'], 'v_l0_ffn_conv_b': out['v_l0_ffn_conv_b'], 'v_l0_w_down': out['v_l0_w_down'], 'v_l1_mix_norm': out['v_l1_mix_norm'], 'v_l1_w_in': out['v_l1_w_in'], 'v_l1_log_dt': out['v_l1_log_dt'], 'v_l1_a_re': out['v_l1_a_re'], 'v_l1_a_im': out['v_l1_a_im'], 'v_l1_b_re': out['v_l1_b_re'], 'v_l1_b_im': out['v_l1_b_im'], 'v_l1_c_re': out['v_l1_c_re'], 'v_l1_c_im': out['v_l1_c_im'], 'v_l1_d': out['v_l1_d'], 'v_l1_w_glu': out['v_l1_w_glu'], 'v_l1_b_glu': out['v_l1_b_glu'], 'v_l1_ffn_norm': out['v_l1_ffn_norm'], 'v_l1_w_up': out['v_l1_w_up'], 'v_l1_ffn_conv_w': out['v_l1_ffn_conv_w'], 'v_l1_ffn_conv_b': out['v_l1_ffn_conv_b'], 'v_l1_w_down': out['v_l1_w_down'], 'v_final_norm': out['v_final_norm']}


def _loss(weights, diff, rest, loss_target):
    with _jax.named_scope("forward"):
        args = {**rest, TWIN_DIFF_INPUT: diff, **{k: w.astype(_WEIGHT_DTYPES[k]) for k, w in weights.items()}}
        y = _forward(args)
    with _jax.named_scope("loss_head"):
        err = _jnp.square(y.astype(_jnp.float32) - loss_target)
        return 0.5 * _jnp.sum(_jnp.mean(err, axis=-1)) if err.ndim else 0.5 * err


def _adamw(w, g, m, v):
    m = ADAM_B1 * m + (1.0 - ADAM_B1) * g
    v = ADAM_B2 * v + (1.0 - ADAM_B2) * _jnp.square(g)
    m_hat = m / (1.0 - ADAM_B1 ** ADAM_STEP)
    v_hat = v / (1.0 - ADAM_B2 ** ADAM_STEP)
    delta = -ADAM_LR * (m_hat / (_jnp.sqrt(v_hat) + ADAM_EPS) + ADAM_WD * w)
    return delta, m, v


def reference(x, l0_mix_norm, l0_w_in, l0_conv_w, l0_conv_b, l0_conv_ln_g, l0_conv_ln_b, l0_q_norm, l0_kv_norm, l0_w_uq, l0_w_ukv, l0_w_out, l0_ffn_norm, l0_w_up, l0_ffn_conv_w, l0_ffn_conv_b, l0_w_down, l1_mix_norm, l1_w_in, l1_log_dt, l1_a_re, l1_a_im, l1_b_re, l1_b_im, l1_c_re, l1_c_im, l1_d, l1_w_glu, l1_b_glu, l1_ffn_norm, l1_w_up, l1_ffn_conv_w, l1_ffn_conv_b, l1_w_down, final_norm, loss_target, m_l0_mix_norm, m_l0_w_in, m_l0_conv_w, m_l0_conv_b, m_l0_conv_ln_g, m_l0_conv_ln_b, m_l0_q_norm, m_l0_kv_norm, m_l0_w_uq, m_l0_w_ukv, m_l0_w_out, m_l0_ffn_norm, m_l0_w_up, m_l0_ffn_conv_w, m_l0_ffn_conv_b, m_l0_w_down, m_l1_mix_norm, m_l1_w_in, m_l1_log_dt, m_l1_a_re, m_l1_a_im, m_l1_b_re, m_l1_b_im, m_l1_c_re, m_l1_c_im, m_l1_d, m_l1_w_glu, m_l1_b_glu, m_l1_ffn_norm, m_l1_w_up, m_l1_ffn_conv_w, m_l1_ffn_conv_b, m_l1_w_down, m_final_norm, v_l0_mix_norm, v_l0_w_in, v_l0_conv_w, v_l0_conv_b, v_l0_conv_ln_g, v_l0_conv_ln_b, v_l0_q_norm, v_l0_kv_norm, v_l0_w_uq, v_l0_w_ukv, v_l0_w_out, v_l0_ffn_norm, v_l0_w_up, v_l0_ffn_conv_w, v_l0_ffn_conv_b, v_l0_w_down, v_l1_mix_norm, v_l1_w_in, v_l1_log_dt, v_l1_a_re, v_l1_a_im, v_l1_b_re, v_l1_b_im, v_l1_c_re, v_l1_c_im, v_l1_d, v_l1_w_glu, v_l1_b_glu, v_l1_ffn_norm, v_l1_w_up, v_l1_ffn_conv_w, v_l1_ffn_conv_b, v_l1_w_down, v_final_norm):
    given = dict(x=x, l0_mix_norm=l0_mix_norm, l0_w_in=l0_w_in, l0_conv_w=l0_conv_w, l0_conv_b=l0_conv_b, l0_conv_ln_g=l0_conv_ln_g, l0_conv_ln_b=l0_conv_ln_b, l0_q_norm=l0_q_norm, l0_kv_norm=l0_kv_norm, l0_w_uq=l0_w_uq, l0_w_ukv=l0_w_ukv, l0_w_out=l0_w_out, l0_ffn_norm=l0_ffn_norm, l0_w_up=l0_w_up, l0_ffn_conv_w=l0_ffn_conv_w, l0_ffn_conv_b=l0_ffn_conv_b, l0_w_down=l0_w_down, l1_mix_norm=l1_mix_norm, l1_w_in=l1_w_in, l1_log_dt=l1_log_dt, l1_a_re=l1_a_re, l1_a_im=l1_a_im, l1_b_re=l1_b_re, l1_b_im=l1_b_im, l1_c_re=l1_c_re, l1_c_im=l1_c_im, l1_d=l1_d, l1_w_glu=l1_w_glu, l1_b_glu=l1_b_glu, l1_ffn_norm=l1_ffn_norm, l1_w_up=l1_w_up, l1_ffn_conv_w=l1_ffn_conv_w, l1_ffn_conv_b=l1_ffn_conv_b, l1_w_down=l1_w_down, final_norm=final_norm, loss_target=loss_target, m_l0_mix_norm=m_l0_mix_norm, m_l0_w_in=m_l0_w_in, m_l0_conv_w=m_l0_conv_w, m_l0_conv_b=m_l0_conv_b, m_l0_conv_ln_g=m_l0_conv_ln_g, m_l0_conv_ln_b=m_l0_conv_ln_b, m_l0_q_norm=m_l0_q_norm, m_l0_kv_norm=m_l0_kv_norm, m_l0_w_uq=m_l0_w_uq, m_l0_w_ukv=m_l0_w_ukv, m_l0_w_out=m_l0_w_out, m_l0_ffn_norm=m_l0_ffn_norm, m_l0_w_up=m_l0_w_up, m_l0_ffn_conv_w=m_l0_ffn_conv_w, m_l0_ffn_conv_b=m_l0_ffn_conv_b, m_l0_w_down=m_l0_w_down, m_l1_mix_norm=m_l1_mix_norm, m_l1_w_in=m_l1_w_in, m_l1_log_dt=m_l1_log_dt, m_l1_a_re=m_l1_a_re, m_l1_a_im=m_l1_a_im, m_l1_b_re=m_l1_b_re, m_l1_b_im=m_l1_b_im, m_l1_c_re=m_l1_c_re, m_l1_c_im=m_l1_c_im, m_l1_d=m_l1_d, m_l1_w_glu=m_l1_w_glu, m_l1_b_glu=m_l1_b_glu, m_l1_ffn_norm=m_l1_ffn_norm, m_l1_w_up=m_l1_w_up, m_l1_ffn_conv_w=m_l1_ffn_conv_w, m_l1_ffn_conv_b=m_l1_ffn_conv_b, m_l1_w_down=m_l1_w_down, m_final_norm=m_final_norm, v_l0_mix_norm=v_l0_mix_norm, v_l0_w_in=v_l0_w_in, v_l0_conv_w=v_l0_conv_w, v_l0_conv_b=v_l0_conv_b, v_l0_conv_ln_g=v_l0_conv_ln_g, v_l0_conv_ln_b=v_l0_conv_ln_b, v_l0_q_norm=v_l0_q_norm, v_l0_kv_norm=v_l0_kv_norm, v_l0_w_uq=v_l0_w_uq, v_l0_w_ukv=v_l0_w_ukv, v_l0_w_out=v_l0_w_out, v_l0_ffn_norm=v_l0_ffn_norm, v_l0_w_up=v_l0_w_up, v_l0_ffn_conv_w=v_l0_ffn_conv_w, v_l0_ffn_conv_b=v_l0_ffn_conv_b, v_l0_w_down=v_l0_w_down, v_l1_mix_norm=v_l1_mix_norm, v_l1_w_in=v_l1_w_in, v_l1_log_dt=v_l1_log_dt, v_l1_a_re=v_l1_a_re, v_l1_a_im=v_l1_a_im, v_l1_b_re=v_l1_b_re, v_l1_b_im=v_l1_b_im, v_l1_c_re=v_l1_c_re, v_l1_c_im=v_l1_c_im, v_l1_d=v_l1_d, v_l1_w_glu=v_l1_w_glu, v_l1_b_glu=v_l1_b_glu, v_l1_ffn_norm=v_l1_ffn_norm, v_l1_w_up=v_l1_w_up, v_l1_ffn_conv_w=v_l1_ffn_conv_w, v_l1_ffn_conv_b=v_l1_ffn_conv_b, v_l1_w_down=v_l1_w_down, v_final_norm=v_final_norm)
    weights = {n: given[n] for n in TWIN_WEIGHTS}
    shared = {n: given[n] for n in SHARED_INPUTS}
    per_example = {n: given[n] for n in ['x']}
    grad_fn = _jax.value_and_grad(_loss, argnums=(0, 1))

    def one_microbatch(ex, loss_target):
        ex = dict(ex)
        diff = ex.pop(TWIN_DIFF_INPUT)
        return grad_fn(weights, diff, {**shared, **ex}, loss_target)

    if N_MICROBATCH == 1:
        loss, (grad_w, grad_x) = one_microbatch(per_example, given["loss_target"])
    else:
        def body(carry, xs):
            loss_sum, grad_sum = carry
            l_k, (gw_k, gx_k) = one_microbatch(xs[0], xs[1])
            with _jax.named_scope("update"):
                return (loss_sum + l_k, _jax.tree.map(_jnp.add, grad_sum, gw_k)), gx_k

        init = (_jnp.zeros((), _jnp.float32), _jax.tree.map(_jnp.zeros_like, weights))
        (loss, grad_w), grad_x = _jax.lax.scan(body, init, (per_example, given["loss_target"]))
    with _jax.named_scope("update"):
        delta_w, new_m, new_v = {}, {}, {}
        for n in TWIN_WEIGHTS:
            delta_w[n], new_m[n], new_v[n] = _adamw(weights[n], grad_w[n], given["m_" + n], given["v_" + n])
    return (loss, grad_x, *[grad_w[n] for n in TWIN_WEIGHTS], *[delta_w[n] for n in TWIN_WEIGHTS],
            *[new_m[n] for n in TWIN_WEIGHTS], *[new_v[n] for n in TWIN_WEIGHTS])
```

```python
import functools
import math

import jax
import jax.numpy as jnp
import numpy as np
from jax import lax
from jax.experimental import pallas as pl
from jax.experimental.pallas import tpu as pltpu

F32 = jnp.float32
BF16 = jnp.bfloat16
MESH = pl.DeviceIdType.MESH

D_MODEL = 1024
EPS = 1e-6
LN_EPS = 1e-5
CONV_WIDTH = 512
CONV_K = 31
N_HEADS = 8
QK_NOPE = 64
QK_ROPE = 32
V_DIM = 64
Q_LORA = 256
KV_LORA = 128
ROPE_BASE = 10000.0
ATT_SCALE = (QK_NOPE + QK_ROPE) ** -0.5
SSM_WIDTH = 512
SSM_GROUP = 16
SSM_GROUPS = 32
SSM_STATE = 64
D_FF = 2816
FFN_K = 3
ADAM_LR = 0.001
ADAM_B1 = 0.9
ADAM_B2 = 0.999
ADAM_EPS = 1e-08
ADAM_WD = 0.01
ADAM_STEP = 10

N_CHIPS = 4
LANES = 128
HEAD_PAD = 256
CONV_HALO = 32
FFN_HALO = 16
VMEM_LIMIT = 56 * 1024 * 1024

ROW_TILE = 512
FFN_ROW_TILE = 1024
FFN_COL_TILE = 256
ATT_TILE = 512
SCAN_TILE = 256


def _cparams(*sem):
    return pltpu.CompilerParams(dimension_semantics=tuple(sem), vmem_limit_bytes=VMEM_LIMIT)


def _pick(n, cands):
    for c in cands:
        if n % c == 0:
            return c
    return n


def matmul(a, b, *, ta=False, tb=False, res=None, bias=None, out_dtype=F32, name):
    if ta:
        kdim, m = a.shape
    else:
        m, kdim = a.shape
    if tb:
        n, k2 = b.shape
    else:
        k2, n = b.shape
    assert kdim == k2, (a.shape, b.shape, ta, tb)
    tm = _pick(m, (512, 256, 128))
    tn = _pick(n, (512, 256, 128))
    tk = _pick(kdim, (1024, 512, 256, 128))
    nk = kdim // tk
    has_res, has_bias = res is not None, bias is not None
    dims = (((0,) if ta else (1,), (1,) if tb else (0,)), ((), ()))

    def body(*refs):
        a_ref, b_ref = refs[0], refs[1]
        pos = 2
        res_ref = bias_ref = None
        if has_res:
            res_ref = refs[pos]
            pos += 1
        if has_bias:
            bias_ref = refs[pos]
            pos += 1
        o_ref, acc_ref = refs[pos], refs[pos + 1]
        k = pl.program_id(2)

        @pl.when(k == 0)
        def _():
            acc_ref[...] = jnp.zeros_like(acc_ref)

        acc_ref[...] += lax.dot_general(a_ref[...].astype(BF16), b_ref[...].astype(BF16), dims,
                                        preferred_element_type=F32)

        @pl.when(k == nk - 1)
        def _():
            r = acc_ref[...]
            if has_bias:
                r = r + bias_ref[...]
            if has_res:
                r = r + res_ref[...].astype(F32)
            o_ref[...] = r.astype(o_ref.dtype)

    a_spec = pl.BlockSpec((tk, tm), lambda i, j, k: (k, i)) if ta else pl.BlockSpec((tm, tk), lambda i, j, k: (i, k))
    b_spec = pl.BlockSpec((tn, tk), lambda i, j, k: (j, k)) if tb else pl.BlockSpec((tk, tn), lambda i, j, k: (k, j))
    in_specs = [a_spec, b_spec]
    args = [a, b]
    if has_res:
        in_specs.append(pl.BlockSpec((tm, tn), lambda i, j, k: (i, j)))
        args.append(res)
    if has_bias:
        in_specs.append(pl.BlockSpec((1, tn), lambda i, j, k: (0, j)))
        args.append(bias)
    return pl.pallas_call(
        body, name=name, grid=(m // tm, n // tn, nk),
        in_specs=in_specs, out_specs=pl.BlockSpec((tm, tn), lambda i, j, k: (i, j)),
        out_shape=jax.ShapeDtypeStruct((m, n), out_dtype),
        scratch_shapes=[pltpu.VMEM((tm, tn), F32)],
        compiler_params=_cparams("parallel", "parallel", "arbitrary"),
    )(*args)


def rowcall(body, *, rows, ts, ins, outs, name, scratch=()):
    nt = rows // ts
    in_specs, args = [], []
    for arr, kind in ins:
        if kind == "row":
            in_specs.append(pl.BlockSpec((ts, arr.shape[1]), lambda i: (i, 0)))
        elif kind == "rev":
            in_specs.append(pl.BlockSpec((ts, arr.shape[1]), lambda i: (nt - 1 - i, 0)))
        elif kind == "full":
            nd = arr.ndim
            in_specs.append(pl.BlockSpec(arr.shape, lambda i, nd=nd: (0,) * nd))
        elif kind.startswith("prev:"):
            h = int(kind[5:])
            r = ts // h
            in_specs.append(pl.BlockSpec((h, arr.shape[1]), lambda i, r=r: (jnp.maximum(i * r - 1, 0), 0)))
        elif kind.startswith("next:"):
            h = int(kind[5:])
            r = ts // h
            last = rows // h - 1
            in_specs.append(pl.BlockSpec((h, arr.shape[1]), lambda i, r=r, last=last: (jnp.minimum((i + 1) * r, last), 0)))
        elif kind.startswith("revprev:"):
            h = int(kind[8:])
            r = ts // h
            in_specs.append(pl.BlockSpec((h, arr.shape[1]), lambda i, r=r: (jnp.maximum((nt - 1 - i) * r - 1, 0), 0)))
        else:
            raise ValueError(kind)
        args.append(arr)
    out_specs, out_shapes = [], []
    for shape, dtype, kind in outs:
        if kind == "row":
            out_specs.append(pl.BlockSpec((ts, shape[1]), lambda i: (i, 0)))
        elif kind == "rev":
            out_specs.append(pl.BlockSpec((ts, shape[1]), lambda i: (nt - 1 - i, 0)))
        else:
            nd = len(shape)
            out_specs.append(pl.BlockSpec(tuple(shape), lambda i, nd=nd: (0,) * nd))
        out_shapes.append(jax.ShapeDtypeStruct(tuple(shape), dtype))
    return pl.pallas_call(
        functools.partial(body, nt), name=name, grid=(nt,),
        in_specs=in_specs, out_specs=tuple(out_specs), out_shape=tuple(out_shapes),
        scratch_shapes=list(scratch),
        compiler_params=_cparams("arbitrary"),
    )(*args)


def _rms(x, g):
    return x * lax.rsqrt(jnp.mean(x * x, axis=-1, keepdims=True) + EPS) * g


def _layer_norm(x, g, b):
    mu = jnp.mean(x, axis=-1, keepdims=True)
    xc = x - mu
    var = jnp.mean(xc * xc, axis=-1, keepdims=True)
    return xc * lax.rsqrt(var + LN_EPS) * g + b


def _sigmoid(x):
    return 1.0 / (1.0 + jnp.exp(-x))


def _silu(x):
    return x * _sigmoid(x)


def _gelu(x):
    return 0.5 * x * (1.0 + jnp.tanh(math.sqrt(2.0 / math.pi) * (x + 0.044715 * (x * x * x))))


def _acc(ref, i, val):
    s = jnp.sum(val, axis=0, keepdims=True)

    @pl.when(i == 0)
    def _():
        ref[...] = jnp.zeros_like(ref)

    ref[...] += jnp.broadcast_to(s, ref.shape)


def rms_fwd(x, g, name):
    s, c = x.shape

    def body(nt, x_ref, g_ref, o_ref):
        o_ref[...] = _rms(x_ref[...], g_ref[...]).astype(BF16)

    return rowcall(body, rows=s, ts=min(ROW_TILE, s), ins=[(x, "row"), (g, "full")],
                   outs=[((s, c), BF16, "row")], name=name)[0]


def rms_bwd(x, g, dxn, dres, name):
    s, c = x.shape

    def body(nt, x_ref, g_ref, d_ref, r_ref, dx_ref, dg_ref):
        i = pl.program_id(0)
        _, vjp = jax.vjp(_rms, x_ref[...], g_ref[...])
        dx, dg = vjp(d_ref[...].astype(F32))
        dx_ref[...] = dx + r_ref[...]
        _acc(dg_ref, i, dg)

    return rowcall(body, rows=s, ts=min(ROW_TILE, s),
                   ins=[(x, "row"), (g, "full"), (dxn, "row"), (dres, "row")],
                   outs=[((s, c), F32, "row"), ((8, c), F32, "acc")], name=name)


def _partner(t):
    lane = lax.broadcasted_iota(jnp.int32, t.shape, 1)
    half = QK_ROPE // 2
    return jnp.where(lane % QK_ROPE < half, pltpu.roll(t, LANES - half, 1), pltpu.roll(t, half, 1))


def _rope(t, cos, sin):
    return t * cos + _partner(t) * sin


def _rope_t(d, cos, sin):
    return d * cos + _partner(d * sin)


def rope_tables(s):
    half = QK_ROPE // 2
    inv = ROPE_BASE ** (-jnp.arange(half, dtype=F32) / half)
    ang = jnp.arange(s).astype(F32)[:, None] * inv[None, :]
    cos, sin = jnp.cos(ang), jnp.sin(ang)
    z = jnp.zeros((s, LANES - QK_ROPE), F32)
    return jnp.concatenate([cos, cos, z], axis=1), jnp.concatenate([-sin, sin, z], axis=1)


H0_A, H0_G, H0_Q, H0_KV, H0_KR, H0_W = 0, 512, 1024, 1280, 1408, 1536


def _mixpre_fn(a, g, q, kv, qn, kvn):
    return a * _sigmoid(g), _rms(q, qn), _rms(kv, kvn)


def _h0_parts(h_ref):
    return (h_ref[:, H0_A:H0_G], h_ref[:, H0_G:H0_Q], h_ref[:, H0_Q:H0_KV], h_ref[:, H0_KV:H0_KR])


def mixpre_fwd(h0, qn, kvn, cos, sin):
    s = h0.shape[0]

    def body(nt, h_ref, qn_ref, kvn_ref, cos_ref, sin_ref, u0_ref, cq_ref, ckv_ref, kr_ref):
        u0, cq, ckv = _mixpre_fn(*_h0_parts(h_ref), qn_ref[...], kvn_ref[...])
        u0_ref[...] = u0
        cq_ref[...] = cq.astype(BF16)
        ckv_ref[...] = ckv.astype(BF16)
        kr_ref[...] = _rope(h_ref[:, H0_KR:H0_W], cos_ref[...], sin_ref[...]).astype(BF16)

    return rowcall(body, rows=s, ts=min(ROW_TILE, s),
                   ins=[(h0, "row"), (qn, "full"), (kvn, "full"), (cos, "row"), (sin, "row")],
                   outs=[((s, CONV_WIDTH), F32, "row"), ((s, Q_LORA), BF16, "row"),
                         ((s, KV_LORA), BF16, "row"), ((s, LANES), BF16, "row")], name="mixpre_fwd")


def mixpre_bwd(h0, qn, kvn, cos, sin, du0, dcq, dckv, dkr):
    s = h0.shape[0]

    def body(nt, h_ref, qn_ref, kvn_ref, cos_ref, sin_ref, du0_ref, dcq_ref, dckv_ref, dkr_ref,
             dh_ref, dqn_ref, dkvn_ref):
        i = pl.program_id(0)
        _, vjp = jax.vjp(_mixpre_fn, *_h0_parts(h_ref), qn_ref[...], kvn_ref[...])
        da, dg, dq, dkv, dqn, dkvn = vjp((du0_ref[...], dcq_ref[...], dckv_ref[...]))
        dh_ref[:, H0_A:H0_G] = da.astype(BF16)
        dh_ref[:, H0_G:H0_Q] = dg.astype(BF16)
        dh_ref[:, H0_Q:H0_KV] = dq.astype(BF16)
        dh_ref[:, H0_KV:H0_KR] = dkv.astype(BF16)
        dh_ref[:, H0_KR:H0_W] = _rope_t(dkr_ref[...], cos_ref[...], sin_ref[...]).astype(BF16)
        _acc(dqn_ref, i, dqn)
        _acc(dkvn_ref, i, dkvn)

    return rowcall(body, rows=s, ts=min(ROW_TILE, s),
                   ins=[(h0, "row"), (qn, "full"), (kvn, "full"), (cos, "row"), (sin, "row"),
                        (du0, "row"), (dcq, "row"), (dckv, "row"), (dkr, "row")],
                   outs=[((s, H0_W), BF16, "row"), ((8, Q_LORA), F32, "acc"), ((8, KV_LORA), F32, "acc")],
                   name="mixpre_bwd")


def _conv_taps(ext_ref, w_ref, ts, first, ntaps):
    acc = None
    for k in range(ntaps):
        term = w_ref[pl.ds(k, 1), :] * ext_ref[pl.ds(first + k, ts), :]
        acc = term if acc is None else acc + term
    return acc


def _ln_silu(u1, g, b):
    return _silu(_layer_norm(u1, g, b))


def convln_fwd(u0, w, b, lg, lb):
    s, c = u0.shape
    ts = min(ROW_TILE, s)

    def body(nt, cur_ref, prev_ref, w_ref, b_ref, lg_ref, lb_ref, o_ref, ext_ref):
        i = pl.program_id(0)
        ext_ref[pl.ds(0, CONV_HALO), :] = jnp.where(i > 0, prev_ref[...], 0.0)
        ext_ref[pl.ds(CONV_HALO, ts), :] = cur_ref[...]
        u1 = _conv_taps(ext_ref, w_ref, ts, CONV_HALO - (CONV_K - 1), CONV_K) + b_ref[...]
        o_ref[...] = _ln_silu(u1, lg_ref[...], lb_ref[...]).astype(BF16)

    return rowcall(body, rows=s, ts=ts,
                   ins=[(u0, "row"), (u0, f"prev:{CONV_HALO}"), (w, "full"), (b, "full"), (lg, "full"), (lb, "full")],
                   outs=[((s, c), BF16, "row")], name="convln_fwd",
                   scratch=[pltpu.VMEM((ts + CONV_HALO, c), F32)])[0]


def convln_bwd1(u0, w, b, lg, lb, du):
    s, c = u0.shape
    ts = min(ROW_TILE, s)

    def body(nt, cur_ref, prev_ref, w_ref, b_ref, lg_ref, lb_ref, du_ref, du1_ref, dlg_ref, dlb_ref, dcb_ref, ext_ref):
        i = pl.program_id(0)
        ext_ref[pl.ds(0, CONV_HALO), :] = jnp.where(i > 0, prev_ref[...], 0.0)
        ext_ref[pl.ds(CONV_HALO, ts), :] = cur_ref[...]
        u1 = _conv_taps(ext_ref, w_ref, ts, CONV_HALO - (CONV_K - 1), CONV_K) + b_ref[...]
        _, vjp = jax.vjp(_ln_silu, u1, lg_ref[...], lb_ref[...])
        du1, dlg, dlb = vjp(du_ref[...].astype(F32))
        du1_ref[...] = du1
        _acc(dlg_ref, i, dlg)
        _acc(dlb_ref, i, dlb)
        _acc(dcb_ref, i, du1)

    return rowcall(body, rows=s, ts=ts,
                   ins=[(u0, "row"), (u0, f"prev:{CONV_HALO}"), (w, "full"), (b, "full"), (lg, "full"), (lb, "full"),
                        (du, "row")],
                   outs=[((s, c), F32, "row"), ((8, c), F32, "acc"), ((8, c), F32, "acc"), ((8, c), F32, "acc")],
                   name="convln_bwd1", scratch=[pltpu.VMEM((ts + CONV_HALO, c), F32)])


def convln_bwd2(u0, wrev, du1):
    s, c = u0.shape
    ts = min(ROW_TILE, s)

    def body(nt, cur_ref, prev_ref, d_ref, dnext_ref, wrev_ref, du0_ref, dw_ref, ext_ref, dext_ref):
        i = pl.program_id(0)
        ext_ref[pl.ds(0, CONV_HALO), :] = jnp.where(i > 0, prev_ref[...], 0.0)
        ext_ref[pl.ds(CONV_HALO, ts), :] = cur_ref[...]
        d = d_ref[...]
        dext_ref[pl.ds(0, ts), :] = d
        dext_ref[pl.ds(ts, CONV_HALO), :] = jnp.where(i < nt - 1, dnext_ref[...], 0.0)
        du0_ref[...] = _conv_taps(dext_ref, wrev_ref, ts, 0, CONV_K)

        @pl.when(i == 0)
        def _():
            dw_ref[...] = jnp.zeros_like(dw_ref)

        first = CONV_HALO - (CONV_K - 1)
        for k in range(CONV_K):
            dw_ref[pl.ds(k, 1), :] += jnp.sum(d * ext_ref[pl.ds(first + k, ts), :], axis=0, keepdims=True)

    return rowcall(body, rows=s, ts=ts,
                   ins=[(u0, "row"), (u0, f"prev:{CONV_HALO}"), (du1, "row"), (du1, f"next:{CONV_HALO}"), (wrev, "full")],
                   outs=[((s, c), F32, "row"), ((CONV_HALO, c), F32, "acc")], name="convln_bwd2",
                   scratch=[pltpu.VMEM((ts + CONV_HALO, c), F32), pltpu.VMEM((ts + CONV_HALO, c), F32)])


def qrope_fwd(qraw, cos, sin):
    s = qraw.shape[0]

    def body(nt, q_ref, cos_ref, sin_ref, o_ref):
        cos_v, sin_v = cos_ref[...], sin_ref[...]
        for h in range(N_HEADS):
            o_ref[:, h * HEAD_PAD:h * HEAD_PAD + LANES] = q_ref[:, h * HEAD_PAD:h * HEAD_PAD + LANES].astype(BF16)
            r = q_ref[:, h * HEAD_PAD + LANES:(h + 1) * HEAD_PAD]
            o_ref[:, h * HEAD_PAD + LANES:(h + 1) * HEAD_PAD] = _rope(r, cos_v, sin_v).astype(BF16)

    return rowcall(body, rows=s, ts=min(ROW_TILE, s), ins=[(qraw, "row"), (cos, "row"), (sin, "row")],
                   outs=[((s, N_HEADS * HEAD_PAD), BF16, "row")], name="qrope_fwd")[0]


def qrope_bwd(dq, cos, sin):
    s = dq.shape[0]

    def body(nt, d_ref, cos_ref, sin_ref, o_ref):
        cos_v, sin_v = cos_ref[...], sin_ref[...]
        for h in range(N_HEADS):
            o_ref[:, h * HEAD_PAD:h * HEAD_PAD + LANES] = d_ref[:, h * HEAD_PAD:h * HEAD_PAD + LANES].astype(BF16)
            r = d_ref[:, h * HEAD_PAD + LANES:(h + 1) * HEAD_PAD].astype(F32)
            o_ref[:, h * HEAD_PAD + LANES:(h + 1) * HEAD_PAD] = _rope_t(r, cos_v, sin_v).astype(BF16)

    return rowcall(body, rows=s, ts=min(ROW_TILE, s), ins=[(dq, "row"), (cos, "row"), (sin, "row")],
                   outs=[((s, N_HEADS * HEAD_PAD), BF16, "row")], name="qrope_bwd")[0]


_NT = (((1,), (1,)), ((), ()))
_TN = (((0,), (0,)), ((), ()))


def _scores(q, kv, kr, i, j, t):
    s = lax.dot_general(q[:, :LANES], kv, _NT, preferred_element_type=F32)
    s = s + lax.dot_general(q[:, LANES:], kr, _NT, preferred_element_type=F32)
    s = s * ATT_SCALE
    row = lax.broadcasted_iota(jnp.int32, s.shape, 0) + i * t
    col = lax.broadcasted_iota(jnp.int32, s.shape, 1) + j * t
    return jnp.where(col <= row, s, -jnp.inf)


def attn_fwd(q, kv, kr):
    s = q.shape[0]
    t = min(ATT_TILE, s)
    n = s // t

    def body(q_ref, kv_ref, kr_ref, o_ref, lse_ref, m_ref, l_ref, acc_ref):
        i, j = pl.program_id(1), pl.program_id(2)

        @pl.when(j == 0)
        def _():
            m_ref[...] = jnp.full_like(m_ref, -jnp.inf)
            l_ref[...] = jnp.zeros_like(l_ref)
            acc_ref[...] = jnp.zeros_like(acc_ref)

        @pl.when(j <= i)
        def _():
            kvv = kv_ref[...]
            sc = _scores(q_ref[...], kvv, kr_ref[...], i, j, t)
            m_prev = m_ref[...]
            m_new = jnp.maximum(m_prev, jnp.max(sc, axis=-1, keepdims=True))
            alpha = jnp.exp(m_prev - m_new)
            p = jnp.exp(sc - m_new)
            l_ref[...] = alpha * l_ref[...] + jnp.sum(p, axis=-1, keepdims=True)
            acc_ref[...] = alpha * acc_ref[...] + jnp.dot(p.astype(BF16), kvv, preferred_element_type=F32)
            m_ref[...] = m_new

        @pl.when(j == i)
        def _():
            l = l_ref[...]
            o_ref[...] = (acc_ref[...] / l).astype(BF16)
            lse_ref[...] = jnp.broadcast_to(m_ref[...] + jnp.log(l), lse_ref.shape)

    return pl.pallas_call(
        body, name="attn_fwd", grid=(N_HEADS, n, n),
        in_specs=[pl.BlockSpec((t, HEAD_PAD), lambda h, i, j: (i, h)),
                  pl.BlockSpec((t, LANES), lambda h, i, j: (jnp.minimum(j, i), h)),
                  pl.BlockSpec((t, LANES), lambda h, i, j: (jnp.minimum(j, i), 0))],
        out_specs=(pl.BlockSpec((t, LANES), lambda h, i, j: (i, h)),
                   pl.BlockSpec((t, LANES), lambda h, i, j: (i, h))),
        out_shape=(jax.ShapeDtypeStruct((s, N_HEADS * LANES), BF16),
                   jax.ShapeDtypeStruct((s, N_HEADS * LANES), F32)),
        scratch_shapes=[pltpu.VMEM((t, 1), F32), pltpu.VMEM((t, 1), F32), pltpu.VMEM((t, LANES), F32)],
        compiler_params=_cparams("parallel", "parallel", "arbitrary"),
    )(q, kv, kr)


def _attn_bwd_common(q, kv, kr, o, do, lse, i, j, t):
    sc = _scores(q, kv, kr, i, j, t)
    p = jnp.exp(sc - lse[:, :1])
    dp = lax.dot_general(do, kv, _NT, preferred_element_type=F32)
    delta = jnp.sum(do.astype(F32) * o.astype(F32), axis=-1, keepdims=True)
    ds = p * (dp - delta) * ATT_SCALE
    return p.astype(BF16), ds.astype(BF16)


def attn_bwd_kv(q, kv, kr, o, do, lse):
    s = q.shape[0]
    t = min(ATT_TILE, s)
    n = s // t

    def body(q_ref, kv_ref, kr_ref, o_ref, do_ref, lse_ref, dkv_ref, dkr_ref):
        j, h, i = pl.program_id(0), pl.program_id(1), pl.program_id(2)

        @pl.when(i == 0)
        def _():
            dkv_ref[...] = jnp.zeros_like(dkv_ref)

        @pl.when((i == 0) & (h == 0))
        def _():
            dkr_ref[...] = jnp.zeros_like(dkr_ref)

        @pl.when(i >= j)
        def _():
            qv, dov = q_ref[...], do_ref[...]
            p, ds = _attn_bwd_common(qv, kv_ref[...], kr_ref[...], o_ref[...], dov, lse_ref[...], i, j, t)
            dkv_ref[...] += (lax.dot_general(p, dov, _TN, preferred_element_type=F32)
                             + lax.dot_general(ds, qv[:, :LANES], _TN, preferred_element_type=F32))
            dkr_ref[...] += lax.dot_general(ds, qv[:, LANES:], _TN, preferred_element_type=F32)

    qi = lambda j, h, i: (jnp.maximum(i, j), h)
    return pl.pallas_call(
        body, name="attn_bwd_kv", grid=(n, N_HEADS, n),
        in_specs=[pl.BlockSpec((t, HEAD_PAD), qi),
                  pl.BlockSpec((t, LANES), lambda j, h, i: (j, h)),
                  pl.BlockSpec((t, LANES), lambda j, h, i: (j, 0)),
                  pl.BlockSpec((t, LANES), qi), pl.BlockSpec((t, LANES), qi), pl.BlockSpec((t, LANES), qi)],
        out_specs=(pl.BlockSpec((t, LANES), lambda j, h, i: (j, h)),
                   pl.BlockSpec((t, LANES), lambda j, h, i: (j, 0))),
        out_shape=(jax.ShapeDtypeStruct((s, N_HEADS * LANES), F32), jax.ShapeDtypeStruct((s, LANES), F32)),
        compiler_params=_cparams("parallel", "arbitrary", "arbitrary"),
    )(q, kv, kr, o, do, lse)


def attn_bwd_q(q, kv, kr, o, do, lse):
    s = q.shape[0]
    t = min(ATT_TILE, s)
    n = s // t

    def body(q_ref, kv_ref, kr_ref, o_ref, do_ref, lse_ref, dq_ref, acc_ref):
        i, j = pl.program_id(1), pl.program_id(2)

        @pl.when(j == 0)
        def _():
            acc_ref[...] = jnp.zeros_like(acc_ref)

        @pl.when(j <= i)
        def _():
            kvv, krv = kv_ref[...], kr_ref[...]
            _, ds = _attn_bwd_common(q_ref[...], kvv, krv, o_ref[...], do_ref[...], lse_ref[...], i, j, t)
            acc_ref[:, :LANES] += jnp.dot(ds, kvv, preferred_element_type=F32)
            acc_ref[:, LANES:] += jnp.dot(ds, krv, preferred_element_type=F32)

        @pl.when(j == i)
        def _():
            dq_ref[...] = acc_ref[...].astype(BF16)

    qi = lambda h, i, j: (i, h)
    return pl.pallas_call(
        body, name="attn_bwd_q", grid=(N_HEADS, n, n),
        in_specs=[pl.BlockSpec((t, HEAD_PAD), qi),
                  pl.BlockSpec((t, LANES), lambda h, i, j: (jnp.minimum(j, i), h)),
                  pl.BlockSpec((t, LANES), lambda h, i, j: (jnp.minimum(j, i), 0)),
                  pl.BlockSpec((t, LANES), qi), pl.BlockSpec((t, LANES), qi), pl.BlockSpec((t, LANES), qi)],
        out_specs=pl.BlockSpec((t, HEAD_PAD), qi),
        out_shape=jax.ShapeDtypeStruct((s, N_HEADS * HEAD_PAD), BF16),
        scratch_shapes=[pltpu.VMEM((t, HEAD_PAD), F32)],
        compiler_params=_cparams("parallel", "parallel", "arbitrary"),
    )(q, kv, kr, o, do, lse)


def _gate(ha, hb):
    return _silu(ha) * hb


def ffn_act_fwd(hpa, hpb, wa, wb, ba, bb):
    s, f = hpa.shape
    ts, tf = min(FFN_ROW_TILE, s), FFN_COL_TILE
    hal = FFN_HALO

    def body(a_ref, ap_ref, b_ref, bp_ref, wa_ref, wb_ref, ba_ref, bb_ref, o_ref, ea_ref, eb_ref):
        i = pl.program_id(0)
        for cur, prev, ext in ((a_ref, ap_ref, ea_ref), (b_ref, bp_ref, eb_ref)):
            ext[pl.ds(0, hal), :] = jnp.where(i > 0, prev[...].astype(F32), 0.0)
            ext[pl.ds(hal, ts), :] = cur[...].astype(F32)
        first = hal - (FFN_K - 1)
        ha = _conv_taps(ea_ref, wa_ref, ts, first, FFN_K) + ba_ref[...]
        hb = _conv_taps(eb_ref, wb_ref, ts, first, FFN_K) + bb_ref[...]
        o_ref[...] = _gate(ha, hb).astype(BF16)

    r = ts // hal
    cur = pl.BlockSpec((ts, tf), lambda i, j: (i, j))
    prev = pl.BlockSpec((hal, tf), lambda i, j: (jnp.maximum(i * r - 1, 0), j))
    wsp = pl.BlockSpec((8, tf), lambda i, j: (0, j))
    bsp = pl.BlockSpec((1, tf), lambda i, j: (0, j))
    return pl.pallas_call(
        body, name="ffn_act_fwd", grid=(s // ts, f // tf),
        in_specs=[cur, prev, cur, prev, wsp, wsp, bsp, bsp], out_specs=cur,
        out_shape=jax.ShapeDtypeStruct((s, f), BF16),
        scratch_shapes=[pltpu.VMEM((ts + hal, tf), F32), pltpu.VMEM((ts + hal, tf), F32)],
        compiler_params=_cparams("parallel", "parallel"),
    )(hpa, hpa, hpb, hpb, wa, wb, ba, bb)


def ffn_act_bwd(hpa, hpb, dact, wa, wb, wra, wrb, ba, bb):
    s, f = hpa.shape
    ts, tf = min(FFN_ROW_TILE, s), FFN_COL_TILE
    hal = FFN_HALO
    nt = s // ts
    te = ts + hal

    def body(a_ref, ap_ref, an_ref, b_ref, bp_ref, bn_ref, d_ref, dn_ref, wa_ref, wb_ref, wra_ref, wrb_ref,
             ba_ref, bb_ref, dpa_ref, dpb_ref, dwa_ref, dwb_ref, dba_ref, dbb_ref, ea_ref, eb_ref, da_ref, db_ref):
        i = pl.program_id(1)
        last = i == nt - 1
        for cur, prev, nxt, ext in ((a_ref, ap_ref, an_ref, ea_ref), (b_ref, bp_ref, bn_ref, eb_ref)):
            ext[pl.ds(0, hal), :] = jnp.where(i > 0, prev[...].astype(F32), 0.0)
            ext[pl.ds(hal, ts), :] = cur[...].astype(F32)
            ext[pl.ds(hal + ts, hal), :] = jnp.where(last, 0.0, nxt[...].astype(F32))
        first = hal - (FFN_K - 1)
        ha = _conv_taps(ea_ref, wa_ref, te, first, FFN_K) + ba_ref[...]
        hb = _conv_taps(eb_ref, wb_ref, te, first, FFN_K) + bb_ref[...]
        dact_e = jnp.concatenate([d_ref[...].astype(F32), jnp.where(last, 0.0, dn_ref[...].astype(F32))], axis=0)
        _, vjp = jax.vjp(_gate, ha, hb)
        dha, dhb = vjp(dact_e)
        da_ref[...] = dha
        db_ref[...] = dhb
        dpa_ref[...] = _conv_taps(da_ref, wra_ref, ts, 0, FFN_K).astype(BF16)
        dpb_ref[...] = _conv_taps(db_ref, wrb_ref, ts, 0, FFN_K).astype(BF16)

        @pl.when(i == 0)
        def _():
            for r in (dwa_ref, dwb_ref, dba_ref, dbb_ref):
                r[...] = jnp.zeros_like(r)

        for dh, ext, dw_ref, dbias_ref in ((dha, ea_ref, dwa_ref, dba_ref), (dhb, eb_ref, dwb_ref, dbb_ref)):
            dcur = dh[:ts]
            dbias_ref[...] += jnp.broadcast_to(jnp.sum(dcur, axis=0, keepdims=True), dbias_ref.shape)
            for k in range(FFN_K):
                dw_ref[pl.ds(k, 1), :] += jnp.sum(dcur * ext[pl.ds(first + k, ts), :], axis=0, keepdims=True)

    r = ts // hal
    lastblk = s // hal - 1
    cur = pl.BlockSpec((ts, tf), lambda j, i: (i, j))
    prev = pl.BlockSpec((hal, tf), lambda j, i: (jnp.maximum(i * r - 1, 0), j))
    nxt = pl.BlockSpec((hal, tf), lambda j, i: (jnp.minimum((i + 1) * r, lastblk), j))
    wsp = pl.BlockSpec((8, tf), lambda j, i: (0, j))
    bsp = pl.BlockSpec((1, tf), lambda j, i: (0, j))
    return pl.pallas_call(
        body, name="ffn_act_bwd", grid=(f // tf, nt),
        in_specs=[cur, prev, nxt, cur, prev, nxt, cur, nxt, wsp, wsp, wsp, wsp, bsp, bsp],
        out_specs=(cur, cur, wsp, wsp, wsp, wsp),
        out_shape=(jax.ShapeDtypeStruct((s, f), BF16), jax.ShapeDtypeStruct((s, f), BF16),
                   jax.ShapeDtypeStruct((8, f), F32), jax.ShapeDtypeStruct((8, f), F32),
                   jax.ShapeDtypeStruct((8, f), F32), jax.ShapeDtypeStruct((8, f), F32)),
        scratch_shapes=[pltpu.VMEM((ts + 2 * hal, tf), F32), pltpu.VMEM((ts + 2 * hal, tf), F32),
                        pltpu.VMEM((te, tf), F32), pltpu.VMEM((te, tf), F32)],
        compiler_params=_cparams("parallel", "arbitrary"),
    )(hpa, hpa, hpa, hpb, hpb, hpb, dact, dact, wa, wb, wra, wrb, ba, bb)


NQ = 4
SQ = SSM_STATE * 8
NS = SSM_GROUPS * SSM_STATE


def _s5_disc(log_dt, a_re, a_im, b_re, b_im, expand):
    dt = jnp.exp(log_dt)
    mag = jnp.exp(a_re * dt)
    lb_re, lb_im = mag * jnp.cos(a_im * dt), mag * jnp.sin(a_im * dt)
    den = a_re * a_re + a_im * a_im
    nr, ni = lb_re - 1.0, lb_im
    f_re = (nr * a_re + ni * a_im) / den
    f_im = (ni * a_re - nr * a_im) / den
    fe_re = jnp.dot(f_re, expand, precision=lax.Precision.HIGHEST, preferred_element_type=F32)
    fe_im = jnp.dot(f_im, expand, precision=lax.Precision.HIGHEST, preferred_element_type=F32)
    return lb_re, lb_im, fe_re * b_re - fe_im * b_im, fe_re * b_im + fe_im * b_re


def _expand_matrix():
    e = np.zeros((SSM_STATE, SSM_STATE * SSM_GROUP), np.float32)
    for p in range(SSM_STATE):
        e[p, p * SSM_GROUP:(p + 1) * SSM_GROUP] = 1.0
    return jnp.asarray(e)


def s5_params_fwd(log_dt, a_re, a_im, b_re, b_im):
    expand = _expand_matrix()

    def body(ld_ref, ar_ref, ai_ref, br_ref, bi_ref, e_ref, lr_ref, li_ref, bbr_ref, bbi_ref):
        lr, li, bbr, bbi = _s5_disc(ld_ref[...], ar_ref[...], ai_ref[...], br_ref[...], bi_ref[...], e_ref[...])
        lr_ref[...] = lr
        li_ref[...] = li
        bbr_ref[...] = bbr
        bbi_ref[...] = bbi

    g, p, pc = SSM_GROUPS, SSM_STATE, SSM_STATE * SSM_GROUP
    return pl.pallas_call(
        body, name="s5_params_fwd",
        out_shape=(jax.ShapeDtypeStruct((g, p), F32), jax.ShapeDtypeStruct((g, p), F32),
                   jax.ShapeDtypeStruct((g, pc), F32), jax.ShapeDtypeStruct((g, pc), F32)),
    )(log_dt, a_re, a_im, b_re, b_im, expand)


def s5_params_bwd(log_dt, a_re, a_im, b_re, b_im, dlr, dli, dbbr, dbbi):
    expand = _expand_matrix()

    def body(ld_ref, ar_ref, ai_ref, br_ref, bi_ref, e_ref, dlr_ref, dli_ref, dbbr_ref, dbbi_ref,
             dld_ref, dar_ref, dai_ref, dbr_ref, dbi_ref):
        e = e_ref[...]
        f = lambda ld, ar, ai, br, bi: _s5_disc(ld, ar, ai, br, bi, e)
        _, vjp = jax.vjp(f, ld_ref[...], ar_ref[...], ai_ref[...], br_ref[...], bi_ref[...])
        dld, dar, dai, dbr, dbi = vjp((dlr_ref[...], dli_ref[...], dbbr_ref[...], dbbi_ref[...]))
        dld_ref[...] = dld
        dar_ref[...] = dar
        dai_ref[...] = dai
        dbr_ref[...] = dbr
        dbi_ref[...] = dbi

    g, p, pc = SSM_GROUPS, SSM_STATE, SSM_STATE * SSM_GROUP
    return pl.pallas_call(
        body, name="s5_params_bwd",
        out_shape=(jax.ShapeDtypeStruct((g, 1), F32), jax.ShapeDtypeStruct((g, p), F32),
                   jax.ShapeDtypeStruct((g, p), F32), jax.ShapeDtypeStruct((g, pc), F32),
                   jax.ShapeDtypeStruct((g, pc), F32)),
    )(log_dt, a_re, a_im, b_re, b_im, expand, dlr, dli, dbbr, dbbi)


def _cmul(ar, ai, br, bi):
    return ar * br - ai * bi, ar * bi + ai * br


def _power_rows(lr, li, conj_rev):
    row = lax.broadcasted_iota(jnp.int32, (8, NS), 0)
    tr = jnp.zeros((8, NS), F32)
    ti = jnp.zeros((8, NS), F32)
    pr, pi = lr, li
    for r in range(8):
        dst = 7 - r if conj_rev else r
        tr = jnp.where(row == dst, pr, tr)
        ti = jnp.where(row == dst, -pi if conj_rev else pi, ti)
        if r < 7:
            pr, pi = _cmul(pr, pi, lr, li)
    return tr, ti


def _scan8(xr, xi, tr_ref, ti_ref, cr, ci, reverse):
    row = lax.broadcasted_iota(jnp.int32, xr.shape, 0)
    for d in (1, 2, 4):
        if reverse:
            sr, si = pltpu.roll(xr, 8 - d, 0), pltpu.roll(xi, 8 - d, 0)
            keep = row < 8 - d
            pw = 8 - d
        else:
            sr, si = pltpu.roll(xr, d, 0), pltpu.roll(xi, d, 0)
            keep = row >= d
            pw = d - 1
        mr, mi = _cmul(tr_ref[pl.ds(pw, 1), :], ti_ref[pl.ds(pw, 1), :], sr, si)
        xr = xr + jnp.where(keep, mr, 0.0)
        xi = xi + jnp.where(keep, mi, 0.0)
    mr, mi = _cmul(tr_ref[...], ti_ref[...], cr, ci)
    return xr + mr, xi + mi


def _row_of(x, r):
    row = lax.broadcasted_iota(jnp.int32, x.shape, 0)
    return jnp.sum(jnp.where(row == r, x, 0.0), axis=0, keepdims=True)


def s5_scan_fwd(u, lam_r, lam_i, bre, bim, cre, cim, dskip):
    s = u.shape[0]
    tt = min(SCAN_TILE, s)
    nb = tt // 8

    def body(nt, u_ref, lr_ref, li_ref, bre_ref, bim_ref, cre_ref, cim_ref, d_ref,
             xr_ref, xi_ref, y_ref, yg_ref, tr_ref, ti_ref, cr_ref, ci_ref):
        i = pl.program_id(0)

        @pl.when(i == 0)
        def _():
            tr, ti = _power_rows(lr_ref[...], li_ref[...], False)
            tr_ref[...] = tr
            ti_ref[...] = ti
            cr_ref[...] = jnp.zeros_like(cr_ref)
            ci_ref[...] = jnp.zeros_like(ci_ref)

        uv = u_ref[...]
        ub = uv.astype(BF16)
        for q in range(NQ):
            uq = ub[:, q * LANES:(q + 1) * LANES]
            xr_ref[:, q * SQ:(q + 1) * SQ] = jnp.dot(uq, bre_ref[q], preferred_element_type=F32)
            xi_ref[:, q * SQ:(q + 1) * SQ] = jnp.dot(uq, bim_ref[q], preferred_element_type=F32)

        def step(b, carry):
            cr, ci = carry
            rows = pl.ds(pl.multiple_of(b * 8, 8), 8)
            xr, xi = _scan8(xr_ref[rows, :], xi_ref[rows, :], tr_ref, ti_ref, cr, ci, False)
            xr_ref[rows, :] = xr
            xi_ref[rows, :] = xi
            return _row_of(xr, 7), _row_of(xi, 7)

        cr, ci = lax.fori_loop(0, nb, step, (cr_ref[...], ci_ref[...]))
        cr_ref[...] = cr
        ci_ref[...] = ci
        y = d_ref[...] * uv
        for q in range(NQ):
            yq = (jnp.dot(xr_ref[:, q * SQ:(q + 1) * SQ].astype(BF16), cre_ref[q], preferred_element_type=F32)
                  - jnp.dot(xi_ref[:, q * SQ:(q + 1) * SQ].astype(BF16), cim_ref[q], preferred_element_type=F32))
            y_ref[:, q * LANES:(q + 1) * LANES] = yq + y[:, q * LANES:(q + 1) * LANES]
        yg_ref[...] = _gelu(y_ref[...]).astype(BF16)

    return rowcall(body, rows=s, ts=tt,
                   ins=[(u, "row"), (lam_r, "full"), (lam_i, "full"), (bre, "full"), (bim, "full"),
                        (cre, "full"), (cim, "full"), (dskip, "full")],
                   outs=[((s, NS), F32, "row"), ((s, NS), F32, "row"), ((s, SSM_WIDTH), F32, "row"),
                         ((s, SSM_WIDTH), BF16, "row")], name="s5_scan_fwd",
                   scratch=[pltpu.VMEM((8, NS), F32), pltpu.VMEM((8, NS), F32),
                            pltpu.VMEM((1, NS), F32), pltpu.VMEM((1, NS), F32)])


def s5_scan_bwd(dyg, y, lam_r, lam_i, cre, cim):
    s = y.shape[0]
    tt = min(SCAN_TILE, s)
    nb = tt // 8

    def body(nt, dyg_ref, y_ref, lr_ref, li_ref, cre_ref, cim_ref,
             ar_ref, ai_ref, dy_ref, tr_ref, ti_ref, cr_ref, ci_ref):
        i = pl.program_id(0)

        @pl.when(i == 0)
        def _():
            tr, ti = _power_rows(lr_ref[...], li_ref[...], True)
            tr_ref[...] = tr
            ti_ref[...] = ti
            cr_ref[...] = jnp.zeros_like(cr_ref)
            ci_ref[...] = jnp.zeros_like(ci_ref)

        _, vjp = jax.vjp(_gelu, y_ref[...])
        dy = vjp(dyg_ref[...])[0]
        dyb = dy.astype(BF16)
        dy_ref[...] = dyb
        for q in range(NQ):
            dq = dyb[:, q * LANES:(q + 1) * LANES]
            ar_ref[:, q * SQ:(q + 1) * SQ] = lax.dot_general(dq, cre_ref[q], _NT, preferred_element_type=F32)
            ai_ref[:, q * SQ:(q + 1) * SQ] = -lax.dot_general(dq, cim_ref[q], _NT, preferred_element_type=F32)

        def step(b, carry):
            cr, ci = carry
            rows = pl.ds(pl.multiple_of((nb - 1 - b) * 8, 8), 8)
            xr, xi = _scan8(ar_ref[rows, :], ai_ref[rows, :], tr_ref, ti_ref, cr, ci, True)
            ar_ref[rows, :] = xr
            ai_ref[rows, :] = xi
            return _row_of(xr, 0), _row_of(xi, 0)

        cr, ci = lax.fori_loop(0, nb, step, (cr_ref[...], ci_ref[...]))
        cr_ref[...] = cr
        ci_ref[...] = ci

    return rowcall(body, rows=s, ts=tt,
                   ins=[(dyg, "rev"), (y, "rev"), (lam_r, "full"), (lam_i, "full"), (cre, "full"), (cim, "full")],
                   outs=[((s, NS), F32, "rev"), ((s, NS), F32, "rev"), ((s, SSM_WIDTH), BF16, "rev")],
                   name="s5_scan_bwd",
                   scratch=[pltpu.VMEM((8, NS), F32), pltpu.VMEM((8, NS), F32),
                            pltpu.VMEM((1, NS), F32), pltpu.VMEM((1, NS), F32)])


def s5_grads(u, dy, xr, xi, ar, ai, bre, bim, dskip):
    s = u.shape[0]
    tt = min(SCAN_TILE, s)

    def body(nt, u_ref, dy_ref, xr_ref, xrp_ref, xi_ref, xip_ref, ar_ref, ai_ref, bre_ref, bim_ref, d_ref,
             du_ref, dlr_ref, dli_ref, dbr_ref, dbi_ref, dcr_ref, dci_ref, dd_ref, er_ref, ei_ref):
        i = pl.program_id(0)

        @pl.when(i == 0)
        def _():
            for r in (dbr_ref, dbi_ref, dcr_ref, dci_ref):
                r[...] = jnp.zeros_like(r)

        uv, dyb = u_ref[...], dy_ref[...]
        dyf = dyb.astype(F32)
        av_r, av_i, xv_r, xv_i = ar_ref[...], ai_ref[...], xr_ref[...], xi_ref[...]
        er_ref[pl.ds(0, 8), :] = jnp.where(i > 0, xrp_ref[...], 0.0)
        ei_ref[pl.ds(0, 8), :] = jnp.where(i > 0, xip_ref[...], 0.0)
        er_ref[pl.ds(8, tt), :] = xv_r
        ei_ref[pl.ds(8, tt), :] = xv_i
        sr, si = er_ref[pl.ds(7, tt), :], ei_ref[pl.ds(7, tt), :]
        _acc(dlr_ref, i, av_r * sr + av_i * si)
        _acc(dli_ref, i, av_i * sr - av_r * si)
        _acc(dd_ref, i, dyf * uv)
        ub = uv.astype(BF16)
        ab_r, ab_i = av_r.astype(BF16), av_i.astype(BF16)
        xb_r, xb_i = xv_r.astype(BF16), xv_i.astype(BF16)
        du = d_ref[...] * dyf
        for q in range(NQ):
            cs, ss = slice(q * LANES, (q + 1) * LANES), slice(q * SQ, (q + 1) * SQ)
            dbr_ref[q] += lax.dot_general(ub[:, cs], ab_r[:, ss], _TN, preferred_element_type=F32)
            dbi_ref[q] += lax.dot_general(ub[:, cs], ab_i[:, ss], _TN, preferred_element_type=F32)
            dcr_ref[q] += lax.dot_general(xb_r[:, ss], dyb[:, cs], _TN, preferred_element_type=F32)
            dci_ref[q] -= lax.dot_general(xb_i[:, ss], dyb[:, cs], _TN, preferred_element_type=F32)
            du_ref[:, cs] = (du[:, cs]
                             + lax.dot_general(ab_r[:, ss], bre_ref[q], _NT, preferred_element_type=F32)
                             + lax.dot_general(ab_i[:, ss], bim_ref[q], _NT, preferred_element_type=F32))

    return rowcall(body, rows=s, ts=tt,
                   ins=[(u, "row"), (dy, "row"), (xr, "row"), (xr, "prev:8"), (xi, "row"), (xi, "prev:8"),
                        (ar, "row"), (ai, "row"), (bre, "full"), (bim, "full"), (dskip, "full")],
                   outs=[((s, SSM_WIDTH), F32, "row"), ((8, NS), F32, "acc"), ((8, NS), F32, "acc"),
                         ((NQ, LANES, SQ), F32, "acc"), ((NQ, LANES, SQ), F32, "acc"),
                         ((NQ, SQ, LANES), F32, "acc"), ((NQ, SQ, LANES), F32, "acc"),
                         ((8, SSM_WIDTH), F32, "acc")], name="s5_grads",
                   scratch=[pltpu.VMEM((tt + 8, NS), F32), pltpu.VMEM((tt + 8, NS), F32)])


def _glu_fn(za, zb):
    return za * _sigmoid(zb)


def glu_res_fwd(z, xres):
    s = z.shape[0]

    def body(nt, z_ref, x_ref, o_ref):
        o_ref[...] = x_ref[...] + _glu_fn(z_ref[:, :D_MODEL].astype(F32), z_ref[:, D_MODEL:].astype(F32))

    return rowcall(body, rows=s, ts=min(ROW_TILE, s), ins=[(z, "row"), (xres, "row")],
                   outs=[((s, D_MODEL), F32, "row")], name="glu_res_fwd")[0]


def glu_bwd(z, dout):
    s, c = z.shape

    def body(nt, z_ref, d_ref, dz_ref, dba_ref, dbb_ref):
        i = pl.program_id(0)
        _, vjp = jax.vjp(_glu_fn, z_ref[:, :D_MODEL].astype(F32), z_ref[:, D_MODEL:].astype(F32))
        dza, dzb = vjp(d_ref[...])
        dz_ref[:, :D_MODEL] = dza.astype(BF16)
        dz_ref[:, D_MODEL:] = dzb.astype(BF16)
        _acc(dba_ref, i, dza)
        _acc(dbb_ref, i, dzb)

    return rowcall(body, rows=s, ts=min(ROW_TILE, s), ins=[(z, "row"), (dout, "row")],
                   outs=[((s, c), BF16, "row"), ((8, D_MODEL), F32, "acc"), ((8, D_MODEL), F32, "acc")],
                   name="glu_bwd")


def loss_head(x, g, target):
    s, c = x.shape

    def body(nt, x_ref, g_ref, t_ref, loss_ref, dx_ref, dg_ref):
        i = pl.program_id(0)
        y, vjp = jax.vjp(_rms, x_ref[...], g_ref[...])
        err = y - t_ref[...]
        dx, dg = vjp(err * (1.0 / c))
        dx_ref[...] = dx
        _acc(dg_ref, i, dg)
        part = jnp.sum(jnp.sum(err * err, axis=-1, keepdims=True), axis=0, keepdims=True) * (0.5 / c)

        @pl.when(i == 0)
        def _():
            loss_ref[...] = jnp.zeros_like(loss_ref)

        loss_ref[...] += jnp.broadcast_to(part, loss_ref.shape)

    return rowcall(body, rows=s, ts=min(ROW_TILE, s), ins=[(x, "row"), (g, "full"), (target, "row")],
                   outs=[((8, LANES), F32, "acc"), ((s, c), F32, "row"), ((8, c), F32, "acc")], name="loss_head")


def _tile_rows(r, cands=(512, 256, 128, 64, 32, 16, 8)):
    return _pick(r, cands)


def add_to_bf16(a, b, name):
    n, r, c = a.shape
    tr = _tile_rows(r)

    def body(a_ref, b_ref, o_ref):
        o_ref[...] = (a_ref[...] + b_ref[...]).astype(BF16)

    spec = pl.BlockSpec((1, tr, c), lambda j, i: (j, i, 0))
    return pl.pallas_call(body, name=name, grid=(n, r // tr), in_specs=[spec, spec], out_specs=spec,
                          out_shape=jax.ShapeDtypeStruct((n, r, c), BF16),
                          compiler_params=_cparams("parallel", "parallel"))(a, b)


def sum_leading(a, name):
    n, r, c = a.shape
    tr = _tile_rows(r)

    def body(a_ref, o_ref):
        acc = a_ref[0].astype(F32)
        for k in range(1, n):
            acc = acc + a_ref[k].astype(F32)
        o_ref[...] = acc

    return pl.pallas_call(body, name=name, grid=(r // tr,),
                          in_specs=[pl.BlockSpec((n, tr, c), lambda i: (0, i, 0))],
                          out_specs=pl.BlockSpec((tr, c), lambda i: (i, 0)),
                          out_shape=jax.ShapeDtypeStruct((r, c), F32),
                          compiler_params=_cparams("parallel"))(a)


def adamw(w, g, m, v, name):
    r, c = w.shape
    tr = _tile_rows(r, (256, 128, 64, 32, 16, 8))
    c1 = 1.0 - ADAM_B1 ** ADAM_STEP
    c2 = 1.0 - ADAM_B2 ** ADAM_STEP

    def body(w_ref, g_ref, m_ref, v_ref, d_ref, nm_ref, nv_ref):
        gv = g_ref[...]
        mn = ADAM_B1 * m_ref[...] + (1.0 - ADAM_B1) * gv
        vn = ADAM_B2 * v_ref[...] + (1.0 - ADAM_B2) * (gv * gv)
        d_ref[...] = -ADAM_LR * ((mn / c1) / (jnp.sqrt(vn / c2) + ADAM_EPS) + ADAM_WD * w_ref[...])
        nm_ref[...] = mn
        nv_ref[...] = vn

    spec = pl.BlockSpec((tr, c), lambda i: (i, 0))
    shp = jax.ShapeDtypeStruct((r, c), F32)
    return pl.pallas_call(body, name=name, grid=(r // tr,), in_specs=[spec] * 4, out_specs=(spec,) * 3,
                          out_shape=(shp,) * 3, compiler_params=_cparams("parallel"))(w, g, m, v)


_ANY = pl.BlockSpec(memory_space=pl.ANY)


def all_gather8(block, name):
    r, c = block.shape

    def body(x_ref, out_ref, send_sems, recv_sems, local_sem):
        x, y, cc = lax.axis_index("x"), lax.axis_index("y"), lax.axis_index("c")
        me, sibling = (x, y, cc), (x, y, 1 - cc)
        chips = [(1 - x, y), (x, 1 - y), (1 - x, 1 - y)]

        def slot(px, py, pc):
            return out_ref.at[4 * px + 2 * py + pc]

        def copy(k, blk, to, src=None):
            return pltpu.make_async_remote_copy(
                src_ref=slot(*blk) if src is None else src, dst_ref=slot(*blk),
                send_sem=send_sems.at[k], recv_sem=recv_sems.at[k], device_id=to, device_id_type=MESH)

        mine = pltpu.make_async_copy(x_ref, slot(*me), local_sem)
        mine.start()
        first = [copy(0, me, sibling, src=x_ref)]
        first += [copy(1 + j, me, (*chip, cc), src=x_ref) for j, chip in enumerate(chips)]
        for cp in first:
            cp.start()
        passed = [copy(4 + j, (*chip, cc), sibling) for j, chip in enumerate(chips)]
        for j, chip in enumerate(chips):
            copy(1 + j, (*chip, cc), me).wait_recv()
            passed[j].start()
        copy(0, sibling, me).wait_recv()
        for j, chip in enumerate(chips):
            copy(4 + j, (*chip, 1 - cc), me).wait_recv()
        for cp in first + passed:
            cp.wait_send()
        mine.wait()

    return pl.pallas_call(
        body, name=name, in_specs=[_ANY], out_specs=_ANY,
        out_shape=jax.ShapeDtypeStruct((8, r, c), block.dtype),
        scratch_shapes=[pltpu.SemaphoreType.DMA((7,)), pltpu.SemaphoreType.DMA((7,)), pltpu.SemaphoreType.DMA],
    )(block)


def sibling_swap(block, name):
    def body(x_ref, out_ref, send_sem, recv_sem):
        x, y, cc = lax.axis_index("x"), lax.axis_index("y"), lax.axis_index("c")
        cp = pltpu.make_async_remote_copy(src_ref=x_ref, dst_ref=out_ref, send_sem=send_sem, recv_sem=recv_sem,
                                          device_id=(x, y, 1 - cc), device_id_type=MESH)
        cp.start()
        cp.wait()

    return pl.pallas_call(
        body, name=name, in_specs=[_ANY], out_specs=_ANY,
        out_shape=jax.ShapeDtypeStruct(block.shape, block.dtype),
        scratch_shapes=[pltpu.SemaphoreType.DMA, pltpu.SemaphoreType.DMA],
    )(block)


def chip_exchange(parts, name):
    def body(p_ref, out_ref, send_sems, recv_sems, local_sem):
        x, y, cc = lax.axis_index("x"), lax.axis_index("y"), lax.axis_index("c")
        me = 2 * x + y
        chips = [(1 - x, y), (x, 1 - y), (1 - x, 1 - y)]
        mine = pltpu.make_async_copy(p_ref.at[me], out_ref.at[me], local_sem)
        mine.start()
        sends = []
        for k, (px, py) in enumerate(chips):
            sends.append(pltpu.make_async_remote_copy(
                src_ref=p_ref.at[2 * px + py], dst_ref=out_ref.at[me],
                send_sem=send_sems.at[k], recv_sem=recv_sems.at[k], device_id=(px, py, cc), device_id_type=MESH))
        for cp in sends:
            cp.start()
        for k, (px, py) in enumerate(chips):
            pltpu.make_async_remote_copy(
                src_ref=p_ref.at[me], dst_ref=out_ref.at[2 * px + py],
                send_sem=send_sems.at[k], recv_sem=recv_sems.at[k], device_id=(px, py, cc),
                device_id_type=MESH).wait_recv()
        for cp in sends:
            cp.wait_send()
        mine.wait()

    return pl.pallas_call(
        body, name=name, in_specs=[_ANY], out_specs=_ANY,
        out_shape=jax.ShapeDtypeStruct(parts.shape, parts.dtype),
        scratch_shapes=[pltpu.SemaphoreType.DMA((3,)), pltpu.SemaphoreType.DMA((3,)), pltpu.SemaphoreType.DMA],
    )(parts)


PACK_COLS = 1024
SHARDED = (("l0_w_in", 1), ("l0_w_uq", 1), ("l0_w_ukv", 1), ("l0_w_out", 0), ("l0_w_up", 1), ("l0_w_down", 0),
           ("l1_w_in", 0), ("l1_w_glu", 1), ("l1_w_up", 1), ("l1_w_down", 0),
           ("l0_conv_w", 1), ("l0_ffn_conv_w", 1), ("l1_ffn_conv_w", 1))
REPLICATED = ("l0_mix_norm", "l0_conv_b", "l0_conv_ln_g", "l0_conv_ln_b", "l0_q_norm", "l0_kv_norm", "l0_ffn_norm",
              "l0_ffn_conv_b", "l1_mix_norm", "l1_log_dt", "l1_a_re", "l1_a_im", "l1_b_re", "l1_b_im", "l1_c_re",
              "l1_c_im", "l1_d", "l1_b_glu", "l1_ffn_norm", "l1_ffn_conv_b", "final_norm")


def _pack(arrs, dtype, mult):
    flat = jnp.concatenate([a.reshape(-1).astype(dtype) for a in arrs])
    n = flat.shape[0]
    total = -(-n // mult) * mult
    return jnp.pad(flat, (0, total - n))


def _unpack(flat, shapes):
    out, pos = [], 0
    for shp in shapes:
        n = int(np.prod(shp))
        out.append(flat[pos:pos + n].reshape(shp))
        pos += n
    return out


def _shard(full, axis, j):
    n = full.shape[axis] // N_CHIPS
    return lax.slice_in_dim(full, j * n, (j + 1) * n, axis=axis)


def _block_diag(t):
    q, g, a, b = t.shape
    eye = jnp.eye(g, dtype=t.dtype)
    return jnp.einsum("qgab,gh->qgahb", t, eye).reshape(q, g * a, g * b)


def _block_diag_t(d, a, b):
    q = d.shape[0]
    d5 = d.reshape(q, 8, a, 8, b)
    eye = jnp.eye(8, dtype=d.dtype)
    return jnp.einsum("qgahb,gh->qgab", d5, eye)


def kernel(x, l0_mix_norm, l0_w_in, l0_conv_w, l0_conv_b, l0_conv_ln_g, l0_conv_ln_b, l0_q_norm, l0_kv_norm, l0_w_uq, l0_w_ukv, l0_w_out, l0_ffn_norm, l0_w_up, l0_ffn_conv_w, l0_ffn_conv_b, l0_w_down, l1_mix_norm, l1_w_in, l1_log_dt, l1_a_re, l1_a_im, l1_b_re, l1_b_im, l1_c_re, l1_c_im, l1_d, l1_w_glu, l1_b_glu, l1_ffn_norm, l1_w_up, l1_ffn_conv_w, l1_ffn_conv_b, l1_w_down, final_norm, loss_target, m_l0_mix_norm, m_l0_w_in, m_l0_conv_w, m_l0_conv_b, m_l0_conv_ln_g, m_l0_conv_ln_b, m_l0_q_norm, m_l0_kv_norm, m_l0_w_uq, m_l0_w_ukv, m_l0_w_out, m_l0_ffn_norm, m_l0_w_up, m_l0_ffn_conv_w, m_l0_ffn_conv_b, m_l0_w_down, m_l1_mix_norm, m_l1_w_in, m_l1_log_dt, m_l1_a_re, m_l1_a_im, m_l1_b_re, m_l1_b_im, m_l1_c_re, m_l1_c_im, m_l1_d, m_l1_w_glu, m_l1_b_glu, m_l1_ffn_norm, m_l1_w_up, m_l1_ffn_conv_w, m_l1_ffn_conv_b, m_l1_w_down, m_final_norm, v_l0_mix_norm, v_l0_w_in, v_l0_conv_w, v_l0_conv_b, v_l0_conv_ln_g, v_l0_conv_ln_b, v_l0_q_norm, v_l0_kv_norm, v_l0_w_uq, v_l0_w_ukv, v_l0_w_out, v_l0_ffn_norm, v_l0_w_up, v_l0_ffn_conv_w, v_l0_ffn_conv_b, v_l0_w_down, v_l1_mix_norm, v_l1_w_in, v_l1_log_dt, v_l1_a_re, v_l1_a_im, v_l1_b_re, v_l1_b_im, v_l1_c_re, v_l1_c_im, v_l1_d, v_l1_w_glu, v_l1_b_glu, v_l1_ffn_norm, v_l1_w_up, v_l1_ffn_conv_w, v_l1_ffn_conv_b, v_l1_w_down, v_final_norm):
    a = dict(locals())
    w = {n: a[n] for n in [s for s, _ in SHARDED] + list(REPLICATED)}
    mom = {n: a["m_" + n] for n in w}
    var = {n: a["v_" + n] for n in w}
    return _step(a["x"][0], a["loss_target"][0], w, mom, var)


def _gather_weights(w):
    cc = lax.axis_index("c")
    big = [n for n, _ in SHARDED[:10]]
    small = [n for n, _ in SHARDED[10:]]
    full = {}
    for names, dtype, mult in ((big, BF16, 2 * 256 * PACK_COLS), (small, F32, 2 * 8 * LANES)):
        cols = PACK_COLS if dtype == BF16 else LANES
        flat = _pack([w[n] for n in names], dtype, mult)
        half = lax.dynamic_index_in_dim(flat.reshape(2, -1, cols), cc, axis=0, keepdims=False)
        got = all_gather8(half, "gather_" + ("matrices" if dtype == BF16 else "conv_weights"))
        got = got.reshape(N_CHIPS, -1)
        shapes = [w[n].shape for n in names]
        per_chip = [_unpack(got[j], shapes) for j in range(N_CHIPS)]
        for k, n in enumerate(names):
            axis = dict(SHARDED)[n]
            full[n] = jnp.concatenate([per_chip[j][k] for j in range(N_CHIPS)], axis=axis)
    return full


def _reduce_sharded(grads):
    cc = lax.axis_index("c")
    names = [n for n, _ in SHARDED]
    axes = dict(SHARDED)
    mult = 2 * 256 * PACK_COLS
    packs = [_pack([_shard(grads[n], axes[n], j) for n in names], F32, mult) for j in range(N_CHIPS)]
    g = jnp.stack(packs).reshape(N_CHIPS, 2, -1, PACK_COLS)
    keep = lax.dynamic_index_in_dim(g, cc, axis=1, keepdims=False)
    give = lax.dynamic_index_in_dim(g, 1 - cc, axis=1, keepdims=False)
    got = sibling_swap(give, "grad_swap_halves")
    parts = add_to_bf16(keep, got, "grad_add_sibling")
    landed = chip_exchange(parts, "grad_chip_exchange")
    mine = sum_leading(landed, "grad_sum_chips")
    theirs = sibling_swap(mine, "grad_swap_sums")
    lo = jnp.where(cc == 0, mine, theirs)
    hi = jnp.where(cc == 0, theirs, mine)
    flat = jnp.concatenate([lo.reshape(-1), hi.reshape(-1)])
    shapes = [_shard(grads[n], axes[n], 0).shape for n in names]
    return dict(zip(names, _unpack(flat, shapes)))


def _reduce_replicated(grads):
    names = list(REPLICATED)
    flat = _pack([grads[n] for n in names], F32, 8 * LANES).reshape(-1, LANES)
    got = all_gather8(flat, "gather_small_grads")
    tot = sum_leading(got, "sum_small_grads").reshape(-1)
    return dict(zip(names, _unpack(tot, [grads[n].shape for n in names]))), flat.shape


def _row(v):
    return v.reshape(1, -1).astype(F32)


def _pad_rows(wt, rows):
    return jnp.pad(wt.astype(F32), ((0, rows - wt.shape[0]), (0, 0)))


def _ffn_fwd(xin, g, wa, wb, cw, cb, wd, tag):
    xn = rms_fwd(xin, _row(g), f"{tag}_rms")
    hpa = matmul(xn, wa, out_dtype=BF16, name=f"{tag}_up_a")
    hpb = matmul(xn, wb, out_dtype=BF16, name=f"{tag}_up_b")
    cwa, cwb = _pad_rows(cw[:, :D_FF], 8), _pad_rows(cw[:, D_FF:], 8)
    act = ffn_act_fwd(hpa, hpb, cwa, cwb, _row(cb[:D_FF]), _row(cb[D_FF:]))
    xout = matmul(act, wd, res=xin, name=f"{tag}_down")
    return xout, (xin, xn, hpa, hpb, act)


def _ffn_bwd(dxout, saved, g, wa, wb, cw, cb, wd, tag):
    xin, xn, hpa, hpb, act = saved
    dact = matmul(dxout, wd, tb=True, out_dtype=BF16, name=f"{tag}_d_act")
    d_wd = matmul(act, dxout, ta=True, name=f"{tag}_d_wdown")
    cwa, cwb = _pad_rows(cw[:, :D_FF], 8), _pad_rows(cw[:, D_FF:], 8)
    wra, wrb = _pad_rows(cw[::-1, :D_FF], 8), _pad_rows(cw[::-1, D_FF:], 8)
    dpa, dpb, dwa, dwb, dba, dbb = ffn_act_bwd(hpa, hpb, dact, cwa, cwb, wra, wrb, _row(cb[:D_FF]), _row(cb[D_FF:]))
    dxn = matmul(dpa, wa, tb=True, name=f"{tag}_d_xn_a")
    dxn = matmul(dpb, wb, tb=True, res=dxn, name=f"{tag}_d_xn_b")
    d_wu = jnp.concatenate([matmul(xn, dpa, ta=True, name=f"{tag}_d_wup_a"),
                            matmul(xn, dpb, ta=True, name=f"{tag}_d_wup_b")], axis=1)
    dxin, dg = rms_bwd(xin, _row(g), dxn, dxout, f"{tag}_rms_bwd")
    d_cw = jnp.concatenate([dwa[:FFN_K], dwb[:FFN_K]], axis=1)
    d_cb = jnp.concatenate([dba[0], dbb[0]])
    return dxin, dg[0], d_wu, d_cw, d_cb, d_wd


def _step(x, target, w, mom, var):
    s = x.shape[0]
    full = _gather_weights(w)
    cos, sin = rope_tables(s)

    w_in0 = full["l0_w_in"]
    w_in0p = jnp.concatenate([w_in0, jnp.zeros((D_MODEL, H0_W - w_in0.shape[1]), BF16)], axis=1)
    wq = full["l0_w_uq"].reshape(Q_LORA, N_HEADS, QK_NOPE + QK_ROPE)
    zq = lambda n: jnp.zeros((Q_LORA, N_HEADS, n), BF16)
    w_uqp = jnp.concatenate([wq[..., :QK_NOPE], zq(LANES - QK_NOPE), wq[..., QK_NOPE:], zq(LANES - QK_ROPE)],
                            axis=-1).reshape(Q_LORA, N_HEADS * HEAD_PAD)
    w_ukv = full["l0_w_ukv"]
    w_out = full["l0_w_out"]
    w_out_u = w_out[:CONV_WIDTH]
    wo = w_out[CONV_WIDTH:].reshape(N_HEADS, V_DIM, D_MODEL)
    w_out_a = jnp.concatenate([jnp.zeros_like(wo), wo], axis=1).reshape(N_HEADS * LANES, D_MODEL)
    conv_w = _pad_rows(full["l0_conv_w"], CONV_HALO)
    conv_wrev = _pad_rows(full["l0_conv_w"][::-1], CONV_HALO)
    w_up0a, w_up0b = full["l0_w_up"][:, :D_FF], full["l0_w_up"][:, D_FF:]
    w_up1a, w_up1b = full["l1_w_up"][:, :D_FF], full["l1_w_up"][:, D_FF:]

    xn0 = rms_fwd(x, _row(w["l0_mix_norm"]), "l0_mix_rms")
    h0 = matmul(xn0, w_in0p, name="l0_in_proj")
    qn_g, kvn_g = _row(w["l0_q_norm"]), _row(w["l0_kv_norm"])
    u0, cq, ckv, kr = mixpre_fwd(h0, qn_g, kvn_g, cos, sin)
    cb, lg, lb = _row(w["l0_conv_b"]), _row(w["l0_conv_ln_g"]), _row(w["l0_conv_ln_b"])
    u = convln_fwd(u0, conv_w, cb, lg, lb)
    qraw = matmul(cq, w_uqp, name="l0_q_up")
    q = qrope_fwd(qraw, cos, sin)
    kv = matmul(ckv, w_ukv, out_dtype=BF16, name="l0_kv_up")
    o, lse = attn_fwd(q, kv, kr)
    x1 = matmul(u, w_out_u, res=x, name="l0_out_conv")
    x1 = matmul(o, w_out_a, res=x1, name="l0_out_attn")

    x2, ffn0 = _ffn_fwd(x1, w["l0_ffn_norm"], w_up0a, w_up0b, full["l0_ffn_conv_w"], w["l0_ffn_conv_b"],
                        full["l0_w_down"], "l0_ffn")

    g_, p_, c_ = SSM_GROUPS, SSM_STATE, SSM_GROUP
    s5_in = (w["l1_log_dt"].reshape(g_, 1), w["l1_a_re"], w["l1_a_im"],
             w["l1_b_re"].reshape(g_, p_ * c_), w["l1_b_im"].reshape(g_, p_ * c_))
    lam_r, lam_i, bb_r, bb_i = s5_params_fwd(*s5_in)
    lam_rf, lam_if = lam_r.reshape(1, NS), lam_i.reshape(1, NS)

    def b_blocks(bb):
        t = bb.reshape(NQ, 8, p_, c_).transpose(0, 1, 3, 2)
        return _block_diag(t).astype(BF16)

    def c_blocks(cm):
        t = cm.reshape(NQ, 8, c_, p_).transpose(0, 1, 3, 2)
        return _block_diag(t).astype(BF16)

    bre, bim = b_blocks(bb_r), b_blocks(bb_i)
    cre, cim = c_blocks(w["l1_c_re"]), c_blocks(w["l1_c_im"])
    dskip = _row(w["l1_d"])
    xn2 = rms_fwd(x2, _row(w["l1_mix_norm"]), "l1_mix_rms")
    u1 = matmul(xn2, full["l1_w_in"], name="l1_in_proj")
    xs_r, xs_i, y1, yg = s5_scan_fwd(u1, lam_rf, lam_if, bre, bim, cre, cim, dskip)
    z = matmul(yg, full["l1_w_glu"], bias=_row(w["l1_b_glu"]), out_dtype=BF16, name="l1_glu_proj")
    x3 = glu_res_fwd(z, x2)

    x4, ffn1 = _ffn_fwd(x3, w["l1_ffn_norm"], w_up1a, w_up1b, full["l1_ffn_conv_w"], w["l1_ffn_conv_b"],
                        full["l1_w_down"], "l1_ffn")
    loss_part, dx4, dgf = loss_head(x4, _row(w["final_norm"]), target)
    loss = lax.psum(loss_part[0, 0], ("x", "y", "c"))

    gr = {"final_norm": dgf[0]}

    dx3, gr["l1_ffn_norm"], gr["l1_w_up"], gr["l1_ffn_conv_w"], gr["l1_ffn_conv_b"], gr["l1_w_down"] = _ffn_bwd(
        dx4, ffn1, w["l1_ffn_norm"], w_up1a, w_up1b, full["l1_ffn_conv_w"], w["l1_ffn_conv_b"], full["l1_w_down"],
        "l1_ffn")

    dz, dbga, dbgb = glu_bwd(z, dx3)
    gr["l1_b_glu"] = jnp.concatenate([dbga[0], dbgb[0]])
    dyg = matmul(dz, full["l1_w_glu"], tb=True, name="l1_d_yg")
    gr["l1_w_glu"] = matmul(yg, dz, ta=True, name="l1_d_wglu")
    a_r, a_i, dy1 = s5_scan_bwd(dyg, y1, lam_rf, lam_if, cre, cim)
    du1, dlr, dli, dbr, dbi, dcr, dci, dd = s5_grads(u1, dy1, xs_r, xs_i, a_r, a_i, bre, bim, dskip)
    gr["l1_d"] = dd[0]

    def b_unblock(d):
        return _block_diag_t(d, c_, p_).transpose(0, 1, 3, 2).reshape(g_, p_ * c_)

    def c_unblock(d):
        return _block_diag_t(d, p_, c_).transpose(0, 1, 3, 2).reshape(g_, c_, p_)

    gr["l1_c_re"], gr["l1_c_im"] = c_unblock(dcr), c_unblock(dci)
    dld, dar, dai, dbre, dbim = s5_params_bwd(*s5_in, dlr[0].reshape(g_, p_), dli[0].reshape(g_, p_),
                                              b_unblock(dbr), b_unblock(dbi))
    gr["l1_log_dt"], gr["l1_a_re"], gr["l1_a_im"] = dld.reshape(g_), dar, dai
    gr["l1_b_re"], gr["l1_b_im"] = dbre.reshape(g_, p_, c_), dbim.reshape(g_, p_, c_)
    dxn2 = matmul(du1, full["l1_w_in"], tb=True, name="l1_d_xn")
    gr["l1_w_in"] = matmul(xn2, du1, ta=True, name="l1_d_win")
    dx2, dg = rms_bwd(x2, _row(w["l1_mix_norm"]), dxn2, dx3, "l1_mix_rms_bwd")
    gr["l1_mix_norm"] = dg[0]

    dx1, gr["l0_ffn_norm"], gr["l0_w_up"], gr["l0_ffn_conv_w"], gr["l0_ffn_conv_b"], gr["l0_w_down"] = _ffn_bwd(
        dx2, ffn0, w["l0_ffn_norm"], w_up0a, w_up0b, full["l0_ffn_conv_w"], w["l0_ffn_conv_b"], full["l0_w_down"],
        "l0_ffn")

    du = matmul(dx1, w_out_u, tb=True, out_dtype=BF16, name="l0_d_u")
    do = matmul(dx1, w_out_a, tb=True, out_dtype=BF16, name="l0_d_o")
    d_wout_u = matmul(u, dx1, ta=True, name="l0_d_wout_u")
    d_wout_a = matmul(o, dx1, ta=True, name="l0_d_wout_a")
    gr["l0_w_out"] = jnp.concatenate(
        [d_wout_u, d_wout_a.reshape(N_HEADS, LANES, D_MODEL)[:, LANES - V_DIM:].reshape(N_HEADS * V_DIM, D_MODEL)])
    dkv, dkr = attn_bwd_kv(q, kv, kr, o, do, lse)
    dq = attn_bwd_q(q, kv, kr, o, do, lse)
    dqraw = qrope_bwd(dq, cos, sin)
    dcq = matmul(dqraw, w_uqp, tb=True, name="l0_d_cq")
    d_wuqp = matmul(cq, dqraw, ta=True, name="l0_d_wuq").reshape(Q_LORA, N_HEADS, HEAD_PAD)
    gr["l0_w_uq"] = jnp.concatenate([d_wuqp[..., :QK_NOPE], d_wuqp[..., LANES:LANES + QK_ROPE]],
                                    axis=-1).reshape(Q_LORA, -1)
    dckv = matmul(dkv, w_ukv, tb=True, name="l0_d_ckv")
    gr["l0_w_ukv"] = matmul(ckv, dkv, ta=True, name="l0_d_wukv")
    du1c, dlg, dlb, dcb = convln_bwd1(u0, conv_w, cb, lg, lb, du)
    gr["l0_conv_ln_g"], gr["l0_conv_ln_b"], gr["l0_conv_b"] = dlg[0], dlb[0], dcb[0]
    du0, dcw = convln_bwd2(u0, conv_wrev, du1c)
    gr["l0_conv_w"] = dcw[:CONV_K]
    dh0, dqn, dkvn = mixpre_bwd(h0, qn_g, kvn_g, cos, sin, du0, dcq, dckv, dkr)
    gr["l0_q_norm"], gr["l0_kv_norm"] = dqn[0], dkvn[0]
    dxn0 = matmul(dh0, w_in0p, tb=True, name="l0_d_xn")
    gr["l0_w_in"] = matmul(xn0, dh0, ta=True, name="l0_d_win")[:, :w_in0.shape[1]]
    grad_x, dg = rms_bwd(x, _row(w["l0_mix_norm"]), dxn0, dx1, "l0_mix_rms_bwd")
    gr["l0_mix_norm"] = dg[0]

    g_sh = _reduce_sharded(gr)
    g_rep, pack_shape = _reduce_replicated(gr)
    grad, delta, new_m, new_v = {}, {}, {}, {}
    for n, _ in SHARDED:
        shp = w[n].shape
        two_d = (lambda t: t.reshape(shp[0], -1))
        grad[n] = g_sh[n]
        delta[n], new_m[n], new_v[n] = adamw(two_d(w[n]), two_d(g_sh[n]), two_d(mom[n]), two_d(var[n]), f"adamw_{n}")
    names = list(REPLICATED)
    pk = lambda d: _pack([d[n] for n in names], F32, 8 * LANES).reshape(pack_shape)
    dl, nm, nv = adamw(pk(w), pk(g_rep), pk(mom), pk(var), "adamw_small")
    shapes = [w[n].shape for n in names]
    for n, d_, m_, v_ in zip(names, _unpack(dl.reshape(-1), shapes), _unpack(nm.reshape(-1), shapes),
                             _unpack(nv.reshape(-1), shapes)):
        grad[n], delta[n], new_m[n], new_v[n] = g_rep[n], d_, m_, v_

    order = ["l0_mix_norm", "l0_w_in", "l0_conv_w", "l0_conv_b", "l0_conv_ln_g", "l0_conv_ln_b", "l0_q_norm",
             "l0_kv_norm", "l0_w_uq", "l0_w_ukv", "l0_w_out", "l0_ffn_norm", "l0_w_up", "l0_ffn_conv_w",
             "l0_ffn_conv_b", "l0_w_down", "l1_mix_norm", "l1_w_in", "l1_log_dt", "l1_a_re", "l1_a_im", "l1_b_re",
             "l1_b_im", "l1_c_re", "l1_c_im", "l1_d", "l1_w_glu", "l1_b_glu", "l1_ffn_norm", "l1_w_up",
             "l1_ffn_conv_w", "l1_ffn_conv_b", "l1_w_down", "final_norm"]
    return (loss, grad_x[None], *[grad[n] for n in order], *[delta[n] for n in order],
            *[new_m[n] for n in order], *[new_v[n] for n in order])
```

```python
import functools
import math

import jax
import jax.numpy as jnp
import numpy as np
from jax import lax
from jax.experimental import pallas as pl
from jax.experimental.pallas import tpu as pltpu

F32 = jnp.float32
BF16 = jnp.bfloat16
MESH = pl.DeviceIdType.MESH

D_MODEL = 1024
EPS = 1e-6
LN_EPS = 1e-5
CONV_WIDTH = 512
CONV_K = 31
N_HEADS = 8
QK_NOPE = 64
QK_ROPE = 32
V_DIM = 64
Q_LORA = 256
KV_LORA = 128
ROPE_BASE = 10000.0
ATT_SCALE = (QK_NOPE + QK_ROPE) ** -0.5
SSM_WIDTH = 512
SSM_GROUP = 16
SSM_GROUPS = 32
SSM_STATE = 64
D_FF = 2816
FFN_K = 3
ADAM_LR = 0.001
ADAM_B1 = 0.9
ADAM_B2 = 0.999
ADAM_EPS = 1e-08
ADAM_WD = 0.01
ADAM_STEP = 10

N_CHIPS = 4
LANES = 128
HEAD_PAD = 256
CONV_HALO = 32
FFN_HALO = 16
VMEM_LIMIT = 56 * 1024 * 1024

ROW_TILE = 512
FFN_ROW_TILE = 1024
FFN_COL_TILE = 256
FFN_ROW_CHUNK = 64
ATT_TILE = 1024
SCAN_TILE = 256


def _cparams(*sem):
    return pltpu.CompilerParams(dimension_semantics=tuple(sem), vmem_limit_bytes=VMEM_LIMIT)


def _pick(n, cands):
    for c in cands:
        if n % c == 0:
            return c
    return n


def matmul(a, b, *, ta=False, tb=False, res=None, bias=None, out_dtype=None, name):
    if out_dtype is None:
        out_dtype = BF16 if ta else F32
    if ta:
        kdim, m = a.shape
    else:
        m, kdim = a.shape
    if tb:
        n, k2 = b.shape
    else:
        k2, n = b.shape
    assert kdim == k2, (a.shape, b.shape, ta, tb)
    tn = _pick(n, (1408, 1024, 768, 512, 384, 256, 128))
    if ta:
        tm = _pick(m, (1408, 1024, 512, 256, 128))
        tk = _pick(kdim, (512, 256, 128))
    else:
        tm = _pick(m, (1024, 512, 256, 128))
        tk = kdim
        if kdim > 1024:
            tn = _pick(n, (512, 256, 128))
        if tm * tn > 1024 * 1024 and out_dtype == F32:
            tm = _pick(m, (512, 256, 128))
    nk = kdim // tk
    has_res, has_bias = res is not None, bias is not None
    dims = (((0,) if ta else (1,), (1,) if tb else (0,)), ((), ()))

    def body(*refs):
        a_ref, b_ref = refs[0], refs[1]
        pos = 2
        res_ref = bias_ref = None
        if has_res:
            res_ref = refs[pos]
            pos += 1
        if has_bias:
            bias_ref = refs[pos]
            pos += 1
        o_ref = refs[pos]

        def finish(r):
            if has_bias:
                r = r + bias_ref[...]
            if has_res:
                r = r + res_ref[...].astype(F32)
            o_ref[...] = r.astype(o_ref.dtype)

        prod = lax.dot_general(a_ref[...].astype(BF16), b_ref[...].astype(BF16), dims, preferred_element_type=F32)
        if nk == 1:
            finish(prod)
            return
        acc_ref = refs[pos + 1]
        k = pl.program_id(2)

        @pl.when(k == 0)
        def _():
            acc_ref[...] = prod

        @pl.when(k > 0)
        def _():
            acc_ref[...] += prod

        @pl.when(k == nk - 1)
        def _():
            finish(acc_ref[...])

    a_spec = pl.BlockSpec((tk, tm), lambda i, j, k: (k, i)) if ta else pl.BlockSpec((tm, tk), lambda i, j, k: (i, k))
    b_spec = pl.BlockSpec((tn, tk), lambda i, j, k: (j, k)) if tb else pl.BlockSpec((tk, tn), lambda i, j, k: (k, j))
    in_specs = [a_spec, b_spec]
    args = [a, b]
    if has_res:
        in_specs.append(pl.BlockSpec((tm, tn), lambda i, j, k: (i, j)))
        args.append(res)
    if has_bias:
        in_specs.append(pl.BlockSpec((1, tn), lambda i, j, k: (0, j)))
        args.append(bias)
    return pl.pallas_call(
        body, name=name, grid=(m // tm, n // tn, nk),
        in_specs=in_specs, out_specs=pl.BlockSpec((tm, tn), lambda i, j, k: (i, j)),
        out_shape=jax.ShapeDtypeStruct((m, n), out_dtype),
        scratch_shapes=[pltpu.VMEM((tm, tn), F32)] if nk > 1 else [],
        compiler_params=_cparams("parallel", "parallel", "arbitrary"),
    )(*args)


def rowcall(body, *, rows, ts, ins, outs, name, scratch=()):
    nt = rows // ts
    in_specs, args = [], []
    for arr, kind in ins:
        if kind == "row":
            in_specs.append(pl.BlockSpec((ts, arr.shape[1]), lambda i: (i, 0)))
        elif kind == "rev":
            in_specs.append(pl.BlockSpec((ts, arr.shape[1]), lambda i: (nt - 1 - i, 0)))
        elif kind == "full":
            nd = arr.ndim
            in_specs.append(pl.BlockSpec(arr.shape, lambda i, nd=nd: (0,) * nd))
        elif kind.startswith("prev:"):
            h = int(kind[5:])
            r = ts // h
            in_specs.append(pl.BlockSpec((h, arr.shape[1]), lambda i, r=r: (jnp.maximum(i * r - 1, 0), 0)))
        elif kind.startswith("next:"):
            h = int(kind[5:])
            r = ts // h
            last = rows // h - 1
            in_specs.append(pl.BlockSpec((h, arr.shape[1]), lambda i, r=r, last=last: (jnp.minimum((i + 1) * r, last), 0)))
        elif kind.startswith("revprev:"):
            h = int(kind[8:])
            r = ts // h
            in_specs.append(pl.BlockSpec((h, arr.shape[1]), lambda i, r=r: (jnp.maximum((nt - 1 - i) * r - 1, 0), 0)))
        else:
            raise ValueError(kind)
        args.append(arr)
    out_specs, out_shapes = [], []
    for shape, dtype, kind in outs:
        if kind == "row":
            out_specs.append(pl.BlockSpec((ts, shape[1]), lambda i: (i, 0)))
        elif kind == "rev":
            out_specs.append(pl.BlockSpec((ts, shape[1]), lambda i: (nt - 1 - i, 0)))
        else:
            nd = len(shape)
            out_specs.append(pl.BlockSpec(tuple(shape), lambda i, nd=nd: (0,) * nd))
        out_shapes.append(jax.ShapeDtypeStruct(tuple(shape), dtype))
    return pl.pallas_call(
        functools.partial(body, nt), name=name, grid=(nt,),
        in_specs=in_specs, out_specs=tuple(out_specs), out_shape=tuple(out_shapes),
        scratch_shapes=list(scratch),
        compiler_params=_cparams("arbitrary"),
    )(*args)


def _rms(x, g):
    return x * lax.rsqrt(jnp.mean(x * x, axis=-1, keepdims=True) + EPS) * g


def _layer_norm(x, g, b):
    mu = jnp.mean(x, axis=-1, keepdims=True)
    xc = x - mu
    var = jnp.mean(xc * xc, axis=-1, keepdims=True)
    return xc * lax.rsqrt(var + LN_EPS) * g + b


def _sigmoid(x):
    return 1.0 / (1.0 + jnp.exp(-x))


def _silu(x):
    return x * _sigmoid(x)


def _gelu(x):
    return 0.5 * x * (1.0 + jnp.tanh(math.sqrt(2.0 / math.pi) * (x + 0.044715 * (x * x * x))))


def _acc(ref, i, val):
    s = jnp.sum(val, axis=0, keepdims=True)

    @pl.when(i == 0)
    def _():
        ref[...] = jnp.zeros_like(ref)

    ref[...] += jnp.broadcast_to(s, ref.shape)


def rms_fwd(x, g, name):
    s, c = x.shape

    def body(nt, x_ref, g_ref, o_ref):
        o_ref[...] = _rms(x_ref[...], g_ref[...]).astype(BF16)

    return rowcall(body, rows=s, ts=min(ROW_TILE, s), ins=[(x, "row"), (g, "full")],
                   outs=[((s, c), BF16, "row")], name=name)[0]


def rms_bwd(x, g, dxn, dres, name):
    s, c = x.shape

    def body(nt, x_ref, g_ref, d_ref, r_ref, dx_ref, dg_ref):
        i = pl.program_id(0)
        _, vjp = jax.vjp(_rms, x_ref[...], g_ref[...])
        dx, dg = vjp(d_ref[...].astype(F32))
        dx_ref[...] = dx + r_ref[...]
        _acc(dg_ref, i, dg)

    return rowcall(body, rows=s, ts=min(ROW_TILE, s),
                   ins=[(x, "row"), (g, "full"), (dxn, "row"), (dres, "row")],
                   outs=[((s, c), F32, "row"), ((8, c), F32, "acc")], name=name)


def _partner(t):
    lane = lax.broadcasted_iota(jnp.int32, t.shape, 1)
    half = QK_ROPE // 2
    return jnp.where(lane % QK_ROPE < half, pltpu.roll(t, LANES - half, 1), pltpu.roll(t, half, 1))


def _rope(t, cos, sin):
    return t * cos + _partner(t) * sin


def _rope_t(d, cos, sin):
    return d * cos + _partner(d * sin)


def rope_tables(s):
    half = QK_ROPE // 2
    inv = ROPE_BASE ** (-jnp.arange(half, dtype=F32) / half)
    ang = jnp.arange(s).astype(F32)[:, None] * inv[None, :]
    cos, sin = jnp.cos(ang), jnp.sin(ang)
    z = jnp.zeros((s, LANES - QK_ROPE), F32)
    return jnp.concatenate([cos, cos, z], axis=1), jnp.concatenate([-sin, sin, z], axis=1)


H0_A, H0_G, H0_Q, H0_KV, H0_KR, H0_W = 0, 512, 1024, 1280, 1408, 1536


def _mixpre_fn(a, g, q, kv, qn, kvn):
    return a * _sigmoid(g), _rms(q, qn), _rms(kv, kvn)


def _h0_parts(h_ref):
    return (h_ref[:, H0_A:H0_G], h_ref[:, H0_G:H0_Q], h_ref[:, H0_Q:H0_KV], h_ref[:, H0_KV:H0_KR])


def mixpre_fwd(h0, qn, kvn, cos, sin):
    s = h0.shape[0]

    def body(nt, h_ref, qn_ref, kvn_ref, cos_ref, sin_ref, u0_ref, cq_ref, ckv_ref, kr_ref):
        u0, cq, ckv = _mixpre_fn(*_h0_parts(h_ref), qn_ref[...], kvn_ref[...])
        u0_ref[...] = u0
        cq_ref[...] = cq.astype(BF16)
        ckv_ref[...] = ckv.astype(BF16)
        kr_ref[...] = _rope(h_ref[:, H0_KR:H0_W], cos_ref[...], sin_ref[...]).astype(BF16)

    return rowcall(body, rows=s, ts=min(ROW_TILE, s),
                   ins=[(h0, "row"), (qn, "full"), (kvn, "full"), (cos, "row"), (sin, "row")],
                   outs=[((s, CONV_WIDTH), F32, "row"), ((s, Q_LORA), BF16, "row"),
                         ((s, KV_LORA), BF16, "row"), ((s, LANES), BF16, "row")], name="mixpre_fwd")


def mixpre_bwd(h0, qn, kvn, cos, sin, du0, dcq, dckv, dkr):
    s = h0.shape[0]

    def body(nt, h_ref, qn_ref, kvn_ref, cos_ref, sin_ref, du0_ref, dcq_ref, dckv_ref, dkr_ref,
             dh_ref, dqn_ref, dkvn_ref):
        i = pl.program_id(0)
        _, vjp = jax.vjp(_mixpre_fn, *_h0_parts(h_ref), qn_ref[...], kvn_ref[...])
        da, dg, dq, dkv, dqn, dkvn = vjp((du0_ref[...], dcq_ref[...], dckv_ref[...]))
        dh_ref[:, H0_A:H0_G] = da.astype(BF16)
        dh_ref[:, H0_G:H0_Q] = dg.astype(BF16)
        dh_ref[:, H0_Q:H0_KV] = dq.astype(BF16)
        dh_ref[:, H0_KV:H0_KR] = dkv.astype(BF16)
        dkr = dkr_ref[:, :LANES]
        for h in range(1, N_HEADS):
            dkr = dkr + dkr_ref[:, h * LANES:(h + 1) * LANES]
        dh_ref[:, H0_KR:H0_W] = _rope_t(dkr, cos_ref[...], sin_ref[...]).astype(BF16)
        _acc(dqn_ref, i, dqn)
        _acc(dkvn_ref, i, dkvn)

    return rowcall(body, rows=s, ts=min(ROW_TILE, s),
                   ins=[(h0, "row"), (qn, "full"), (kvn, "full"), (cos, "row"), (sin, "row"),
                        (du0, "row"), (dcq, "row"), (dckv, "row"), (dkr, "row")],
                   outs=[((s, H0_W), BF16, "row"), ((8, Q_LORA), F32, "acc"), ((8, KV_LORA), F32, "acc")],
                   name="mixpre_bwd")


def _conv_taps(ext_ref, w_ref, ts, first, ntaps):
    acc = None
    for k in range(ntaps):
        term = w_ref[pl.ds(k, 1), :] * ext_ref[pl.ds(first + k, ts), :]
        acc = term if acc is None else acc + term
    return acc


def _ln_silu(u1, g, b):
    return _silu(_layer_norm(u1, g, b))


def convln_fwd(u0, w, b, lg, lb):
    s, c = u0.shape
    ts = min(ROW_TILE, s)

    def body(nt, cur_ref, prev_ref, w_ref, b_ref, lg_ref, lb_ref, o_ref, ext_ref):
        i = pl.program_id(0)
        ext_ref[pl.ds(0, CONV_HALO), :] = jnp.where(i > 0, prev_ref[...], 0.0)
        ext_ref[pl.ds(CONV_HALO, ts), :] = cur_ref[...]
        u1 = _conv_taps(ext_ref, w_ref, ts, CONV_HALO - (CONV_K - 1), CONV_K) + b_ref[...]
        o_ref[...] = _ln_silu(u1, lg_ref[...], lb_ref[...]).astype(BF16)

    return rowcall(body, rows=s, ts=ts,
                   ins=[(u0, "row"), (u0, f"prev:{CONV_HALO}"), (w, "full"), (b, "full"), (lg, "full"), (lb, "full")],
                   outs=[((s, c), BF16, "row")], name="convln_fwd",
                   scratch=[pltpu.VMEM((ts + CONV_HALO, c), F32)])[0]


def convln_bwd1(u0, w, b, lg, lb, du):
    s, c = u0.shape
    ts = min(ROW_TILE, s)

    def body(nt, cur_ref, prev_ref, w_ref, b_ref, lg_ref, lb_ref, du_ref, du1_ref, dlg_ref, dlb_ref, dcb_ref, ext_ref):
        i = pl.program_id(0)
        ext_ref[pl.ds(0, CONV_HALO), :] = jnp.where(i > 0, prev_ref[...], 0.0)
        ext_ref[pl.ds(CONV_HALO, ts), :] = cur_ref[...]
        u1 = _conv_taps(ext_ref, w_ref, ts, CONV_HALO - (CONV_K - 1), CONV_K) + b_ref[...]
        _, vjp = jax.vjp(_ln_silu, u1, lg_ref[...], lb_ref[...])
        du1, dlg, dlb = vjp(du_ref[...].astype(F32))
        du1_ref[...] = du1
        _acc(dlg_ref, i, dlg)
        _acc(dlb_ref, i, dlb)
        _acc(dcb_ref, i, du1)

    return rowcall(body, rows=s, ts=ts,
                   ins=[(u0, "row"), (u0, f"prev:{CONV_HALO}"), (w, "full"), (b, "full"), (lg, "full"), (lb, "full"),
                        (du, "row")],
                   outs=[((s, c), F32, "row"), ((8, c), F32, "acc"), ((8, c), F32, "acc"), ((8, c), F32, "acc")],
                   name="convln_bwd1", scratch=[pltpu.VMEM((ts + CONV_HALO, c), F32)])


def convln_bwd2(u0, wrev, du1):
    s, c = u0.shape
    ts = min(ROW_TILE, s)

    def body(nt, cur_ref, prev_ref, d_ref, dnext_ref, wrev_ref, du0_ref, dw_ref, ext_ref, dext_ref):
        i = pl.program_id(0)
        ext_ref[pl.ds(0, CONV_HALO), :] = jnp.where(i > 0, prev_ref[...], 0.0)
        ext_ref[pl.ds(CONV_HALO, ts), :] = cur_ref[...]
        d = d_ref[...]
        dext_ref[pl.ds(0, ts), :] = d
        dext_ref[pl.ds(ts, CONV_HALO), :] = jnp.where(i < nt - 1, dnext_ref[...], 0.0)
        du0_ref[...] = _conv_taps(dext_ref, wrev_ref, ts, 0, CONV_K)

        @pl.when(i == 0)
        def _():
            dw_ref[...] = jnp.zeros_like(dw_ref)

        first = CONV_HALO - (CONV_K - 1)
        for k in range(CONV_K):
            dw_ref[pl.ds(k, 1), :] += jnp.sum(d * ext_ref[pl.ds(first + k, ts), :], axis=0, keepdims=True)

    return rowcall(body, rows=s, ts=ts,
                   ins=[(u0, "row"), (u0, f"prev:{CONV_HALO}"), (du1, "row"), (du1, f"next:{CONV_HALO}"), (wrev, "full")],
                   outs=[((s, c), F32, "row"), ((CONV_HALO, c), F32, "acc")], name="convln_bwd2",
                   scratch=[pltpu.VMEM((ts + CONV_HALO, c), F32), pltpu.VMEM((ts + CONV_HALO, c), F32)])


def qrope_fwd(qraw, cos, sin):
    s = qraw.shape[0]

    def body(nt, q_ref, cos_ref, sin_ref, o_ref):
        cos_v, sin_v = cos_ref[...], sin_ref[...]
        for h in range(N_HEADS):
            o_ref[:, h * HEAD_PAD:h * HEAD_PAD + LANES] = q_ref[:, h * HEAD_PAD:h * HEAD_PAD + LANES].astype(BF16)
            r = q_ref[:, h * HEAD_PAD + LANES:(h + 1) * HEAD_PAD]
            o_ref[:, h * HEAD_PAD + LANES:(h + 1) * HEAD_PAD] = _rope(r, cos_v, sin_v).astype(BF16)

    return rowcall(body, rows=s, ts=min(ROW_TILE, s), ins=[(qraw, "row"), (cos, "row"), (sin, "row")],
                   outs=[((s, N_HEADS * HEAD_PAD), BF16, "row")], name="qrope_fwd")[0]


def qrope_bwd(dq, cos, sin):
    s = dq.shape[0]

    def body(nt, d_ref, cos_ref, sin_ref, o_ref):
        cos_v, sin_v = cos_ref[...], sin_ref[...]
        for h in range(N_HEADS):
            o_ref[:, h * HEAD_PAD:h * HEAD_PAD + LANES] = d_ref[:, h * HEAD_PAD:h * HEAD_PAD + LANES].astype(BF16)
            r = d_ref[:, h * HEAD_PAD + LANES:(h + 1) * HEAD_PAD].astype(F32)
            o_ref[:, h * HEAD_PAD + LANES:(h + 1) * HEAD_PAD] = _rope_t(r, cos_v, sin_v).astype(BF16)

    return rowcall(body, rows=s, ts=min(ROW_TILE, s), ins=[(dq, "row"), (cos, "row"), (sin, "row")],
                   outs=[((s, N_HEADS * HEAD_PAD), BF16, "row")], name="qrope_bwd")[0]


_NT = (((1,), (1,)), ((), ()))
_TN = (((0,), (0,)), ((), ()))


def _scores(q, kv, kr, i, j, t):
    s = lax.dot_general(q[:, :LANES], kv, _NT, preferred_element_type=F32)
    s = s + lax.dot_general(q[:, LANES:], kr, _NT, preferred_element_type=F32)
    s = s * ATT_SCALE
    row = lax.broadcasted_iota(jnp.int32, s.shape, 0) + i * t
    col = lax.broadcasted_iota(jnp.int32, s.shape, 1) + j * t
    return jnp.where(col <= row, s, -jnp.inf)


def attn_fwd(q, kv, kr):
    s = q.shape[0]
    t = min(ATT_TILE, s)
    n = s // t

    def body(q_ref, kv_ref, kr_ref, o_ref, lse_ref, m_ref, l_ref, acc_ref):
        i, j = pl.program_id(1), pl.program_id(2)

        @pl.when(j == 0)
        def _():
            m_ref[...] = jnp.full_like(m_ref, -jnp.inf)
            l_ref[...] = jnp.zeros_like(l_ref)
            acc_ref[...] = jnp.zeros_like(acc_ref)

        @pl.when(j <= i)
        def _():
            kvv = kv_ref[...]
            sc = _scores(q_ref[...], kvv, kr_ref[...], i, j, t)
            m_prev = m_ref[...]
            m_new = jnp.maximum(m_prev, jnp.max(sc, axis=-1, keepdims=True))
            alpha = jnp.exp(m_prev - m_new)
            p = jnp.exp(sc - m_new)
            l_ref[...] = alpha * l_ref[...] + jnp.sum(p, axis=-1, keepdims=True)
            acc_ref[...] = alpha * acc_ref[...] + jnp.dot(p.astype(BF16), kvv, preferred_element_type=F32)
            m_ref[...] = m_new

        @pl.when(j == i)
        def _():
            l = l_ref[...]
            o_ref[...] = (acc_ref[...] / l).astype(BF16)
            lse_ref[...] = jnp.broadcast_to(m_ref[...] + jnp.log(l), lse_ref.shape)

    return pl.pallas_call(
        body, name="attn_fwd", grid=(N_HEADS, n, n),
        in_specs=[pl.BlockSpec((t, HEAD_PAD), lambda h, i, j: (i, h)),
                  pl.BlockSpec((t, LANES), lambda h, i, j: (jnp.minimum(j, i), h)),
                  pl.BlockSpec((t, LANES), lambda h, i, j: (jnp.minimum(j, i), 0))],
        out_specs=(pl.BlockSpec((t, LANES), lambda h, i, j: (i, h)),
                   pl.BlockSpec((t, LANES), lambda h, i, j: (i, h))),
        out_shape=(jax.ShapeDtypeStruct((s, N_HEADS * LANES), BF16),
                   jax.ShapeDtypeStruct((s, N_HEADS * LANES), F32)),
        scratch_shapes=[pltpu.VMEM((t, 1), F32), pltpu.VMEM((t, 1), F32), pltpu.VMEM((t, LANES), F32)],
        compiler_params=_cparams("parallel", "parallel", "arbitrary"),
    )(q, kv, kr)


def _attn_bwd_common(q, kv, kr, o, do, lse, i, j, t):
    sc = _scores(q, kv, kr, i, j, t)
    p = jnp.exp(sc - lse[:, :1])
    dp = lax.dot_general(do, kv, _NT, preferred_element_type=F32)
    delta = jnp.sum(do.astype(F32) * o.astype(F32), axis=-1, keepdims=True)
    ds = p * (dp - delta) * ATT_SCALE
    return p.astype(BF16), ds.astype(BF16)


def attn_bwd(q, kv, kr, o, do, lse):
    s = q.shape[0]
    t = min(ATT_TILE, s)
    n = s // t

    def body(q_ref, kv_ref, kr_ref, o_ref, do_ref, lse_ref, dq_ref, dkv_ref, dkr_ref):
        j, i = pl.program_id(1), pl.program_id(2)

        @pl.when((i == 0) & (j == 0))
        def _():
            dq_ref[...] = jnp.zeros_like(dq_ref)

        @pl.when(i == 0)
        def _():
            dkv_ref[...] = jnp.zeros_like(dkv_ref)
            dkr_ref[...] = jnp.zeros_like(dkr_ref)

        @pl.when(i >= j)
        def _():
            qv, dov, kvv, krv = q_ref[...], do_ref[...], kv_ref[...], kr_ref[...]
            p, ds = _attn_bwd_common(qv, kvv, krv, o_ref[...], dov, lse_ref[...], i, j, t)
            dkv_ref[...] += (lax.dot_general(p, dov, _TN, preferred_element_type=F32)
                             + lax.dot_general(ds, qv[:, :LANES], _TN, preferred_element_type=F32))
            dkr_ref[...] += lax.dot_general(ds, qv[:, LANES:], _TN, preferred_element_type=F32)
            rows = pl.ds(pl.multiple_of(i * t, t), t)
            dq_ref[rows, :LANES] += jnp.dot(ds, kvv, preferred_element_type=F32)
            dq_ref[rows, LANES:] += jnp.dot(ds, krv, preferred_element_type=F32)

    qi = lambda h, j, i: (jnp.maximum(i, j), h)
    kj = lambda h, j, i: (j, h)
    return pl.pallas_call(
        body, name="attn_bwd", grid=(N_HEADS, n, n),
        in_specs=[pl.BlockSpec((t, HEAD_PAD), qi), pl.BlockSpec((t, LANES), kj),
                  pl.BlockSpec((t, LANES), lambda h, j, i: (j, 0)),
                  pl.BlockSpec((t, LANES), qi), pl.BlockSpec((t, LANES), qi), pl.BlockSpec((t, LANES), qi)],
        out_specs=(pl.BlockSpec((s, HEAD_PAD), lambda h, j, i: (0, h)),
                   pl.BlockSpec((t, LANES), kj), pl.BlockSpec((t, LANES), kj)),
        out_shape=(jax.ShapeDtypeStruct((s, N_HEADS * HEAD_PAD), F32),
                   jax.ShapeDtypeStruct((s, N_HEADS * LANES), F32), jax.ShapeDtypeStruct((s, N_HEADS * LANES), F32)),
        compiler_params=_cparams("parallel", "arbitrary", "arbitrary"),
    )(q, kv, kr, o, do, lse)


def _gate(ha, hb):
    return _silu(ha) * hb


def ffn_act_fwd(hpa, hpb, wa, wb, ba, bb):
    s, f = hpa.shape
    ts, tf = min(FFN_ROW_TILE, s), FFN_COL_TILE
    hal = FFN_HALO

    def body(a_ref, ap_ref, b_ref, bp_ref, wa_ref, wb_ref, ba_ref, bb_ref, o_ref, ea_ref, eb_ref):
        i = pl.program_id(0)
        for cur, prev, ext in ((a_ref, ap_ref, ea_ref), (b_ref, bp_ref, eb_ref)):
            ext[pl.ds(0, hal), :] = jnp.where(i > 0, prev[...].astype(F32), 0.0)
            ext[pl.ds(hal, ts), :] = cur[...].astype(F32)
        first = hal - (FFN_K - 1)
        rc = min(FFN_ROW_CHUNK, ts)
        for r0 in range(0, ts, rc):
            ha = _conv_taps(ea_ref, wa_ref, rc, first + r0, FFN_K) + ba_ref[...]
            hb = _conv_taps(eb_ref, wb_ref, rc, first + r0, FFN_K) + bb_ref[...]
            o_ref[pl.ds(r0, rc), :] = _gate(ha, hb).astype(BF16)

    r = ts // hal
    cur = pl.BlockSpec((ts, tf), lambda i, j: (i, j))
    prev = pl.BlockSpec((hal, tf), lambda i, j: (jnp.maximum(i * r - 1, 0), j))
    wsp = pl.BlockSpec((8, tf), lambda i, j: (0, j))
    bsp = pl.BlockSpec((1, tf), lambda i, j: (0, j))
    return pl.pallas_call(
        body, name="ffn_act_fwd", grid=(s // ts, f // tf),
        in_specs=[cur, prev, cur, prev, wsp, wsp, bsp, bsp], out_specs=cur,
        out_shape=jax.ShapeDtypeStruct((s, f), BF16),
        scratch_shapes=[pltpu.VMEM((ts + hal, tf), F32), pltpu.VMEM((ts + hal, tf), F32)],
        compiler_params=_cparams("parallel", "parallel"),
    )(hpa, hpa, hpb, hpb, wa, wb, ba, bb)


def ffn_act_bwd(hpa, hpb, dact, wa, wb, wra, wrb, ba, bb):
    s, f = hpa.shape
    ts, tf = min(FFN_ROW_TILE, s), FFN_COL_TILE
    hal = FFN_HALO
    nt = s // ts
    te = ts + hal

    def body(a_ref, ap_ref, an_ref, b_ref, bp_ref, bn_ref, d_ref, dn_ref, wa_ref, wb_ref, wra_ref, wrb_ref,
             ba_ref, bb_ref, dpa_ref, dpb_ref, dwa_ref, dwb_ref, dba_ref, dbb_ref, ea_ref, eb_ref, da_ref, db_ref):
        i = pl.program_id(1)
        last = i == nt - 1
        for cur, prev, nxt, ext in ((a_ref, ap_ref, an_ref, ea_ref), (b_ref, bp_ref, bn_ref, eb_ref)):
            ext[pl.ds(0, hal), :] = jnp.where(i > 0, prev[...].astype(F32), 0.0)
            ext[pl.ds(hal, ts), :] = cur[...].astype(F32)
            ext[pl.ds(hal + ts, hal), :] = jnp.where(last, 0.0, nxt[...].astype(F32))
        first = hal - (FFN_K - 1)
        rc = min(FFN_ROW_CHUNK, ts)
        zero = jnp.zeros((1, tf), F32)
        sums = {"ba": zero, "bb": zero, **{("a", k): zero for k in range(FFN_K)}, **{("b", k): zero for k in range(FFN_K)}}
        for r0 in list(range(0, ts, rc)) + [ts]:
            n = rc if r0 < ts else hal
            win_a = [ea_ref[pl.ds(first + r0 + k, n), :] for k in range(FFN_K)]
            win_b = [eb_ref[pl.ds(first + r0 + k, n), :] for k in range(FFN_K)]
            ha = sum(wa_ref[pl.ds(k, 1), :] * win_a[k] for k in range(FFN_K)) + ba_ref[...]
            hb = sum(wb_ref[pl.ds(k, 1), :] * win_b[k] for k in range(FFN_K)) + bb_ref[...]
            if r0 < ts:
                dact = d_ref[pl.ds(r0, n), :].astype(F32)
            else:
                dact = jnp.where(last, 0.0, dn_ref[...].astype(F32))
            _, vjp = jax.vjp(_gate, ha, hb)
            dha, dhb = vjp(dact)
            da_ref[pl.ds(r0, n), :] = dha
            db_ref[pl.ds(r0, n), :] = dhb
            if r0 < ts:
                sums["ba"] = sums["ba"] + jnp.sum(dha, axis=0, keepdims=True)
                sums["bb"] = sums["bb"] + jnp.sum(dhb, axis=0, keepdims=True)
                for k in range(FFN_K):
                    sums["a", k] = sums["a", k] + jnp.sum(dha * win_a[k], axis=0, keepdims=True)
                    sums["b", k] = sums["b", k] + jnp.sum(dhb * win_b[k], axis=0, keepdims=True)
        for r0 in range(0, ts, rc):
            dpa_ref[pl.ds(r0, rc), :] = _conv_taps(da_ref, wra_ref, rc, r0, FFN_K).astype(BF16)
            dpb_ref[pl.ds(r0, rc), :] = _conv_taps(db_ref, wrb_ref, rc, r0, FFN_K).astype(BF16)

        @pl.when(i == 0)
        def _():
            for r in (dwa_ref, dwb_ref, dba_ref, dbb_ref):
                r[...] = jnp.zeros_like(r)

        dba_ref[...] += jnp.broadcast_to(sums["ba"], dba_ref.shape)
        dbb_ref[...] += jnp.broadcast_to(sums["bb"], dbb_ref.shape)
        for k in range(FFN_K):
            dwa_ref[pl.ds(k, 1), :] += sums["a", k]
            dwb_ref[pl.ds(k, 1), :] += sums["b", k]

    r = ts // hal
    lastblk = s // hal - 1
    cur = pl.BlockSpec((ts, tf), lambda j, i: (i, j))
    prev = pl.BlockSpec((hal, tf), lambda j, i: (jnp.maximum(i * r - 1, 0), j))
    nxt = pl.BlockSpec((hal, tf), lambda j, i: (jnp.minimum((i + 1) * r, lastblk), j))
    wsp = pl.BlockSpec((8, tf), lambda j, i: (0, j))
    bsp = pl.BlockSpec((1, tf), lambda j, i: (0, j))
    return pl.pallas_call(
        body, name="ffn_act_bwd", grid=(f // tf, nt),
        in_specs=[cur, prev, nxt, cur, prev, nxt, cur, nxt, wsp, wsp, wsp, wsp, bsp, bsp],
        out_specs=(cur, cur, wsp, wsp, wsp, wsp),
        out_shape=(jax.ShapeDtypeStruct((s, f), BF16), jax.ShapeDtypeStruct((s, f), BF16),
                   jax.ShapeDtypeStruct((8, f), F32), jax.ShapeDtypeStruct((8, f), F32),
                   jax.ShapeDtypeStruct((8, f), F32), jax.ShapeDtypeStruct((8, f), F32)),
        scratch_shapes=[pltpu.VMEM((ts + 2 * hal, tf), F32), pltpu.VMEM((ts + 2 * hal, tf), F32),
                        pltpu.VMEM((te, tf), F32), pltpu.VMEM((te, tf), F32)],
        compiler_params=_cparams("parallel", "arbitrary"),
    )(hpa, hpa, hpa, hpb, hpb, hpb, dact, dact, wa, wb, wra, wrb, ba, bb)


NQ = 4
SQ = SSM_STATE * 8
NS = SSM_GROUPS * SSM_STATE


def _s5_disc(log_dt, a_re, a_im, b_re, b_im, expand):
    dt = jnp.exp(log_dt)
    mag = jnp.exp(a_re * dt)
    lb_re, lb_im = mag * jnp.cos(a_im * dt), mag * jnp.sin(a_im * dt)
    den = a_re * a_re + a_im * a_im
    nr, ni = lb_re - 1.0, lb_im
    f_re = (nr * a_re + ni * a_im) / den
    f_im = (ni * a_re - nr * a_im) / den
    fe_re = jnp.dot(f_re, expand, precision=lax.Precision.HIGHEST, preferred_element_type=F32)
    fe_im = jnp.dot(f_im, expand, precision=lax.Precision.HIGHEST, preferred_element_type=F32)
    return lb_re, lb_im, fe_re * b_re - fe_im * b_im, fe_re * b_im + fe_im * b_re


def _expand_matrix():
    e = np.zeros((SSM_STATE, SSM_STATE * SSM_GROUP), np.float32)
    for p in range(SSM_STATE):
        e[p, p * SSM_GROUP:(p + 1) * SSM_GROUP] = 1.0
    return jnp.asarray(e)


def s5_params_fwd(log_dt, a_re, a_im, b_re, b_im):
    expand = _expand_matrix()

    def body(ld_ref, ar_ref, ai_ref, br_ref, bi_ref, e_ref, lr_ref, li_ref, bbr_ref, bbi_ref):
        lr, li, bbr, bbi = _s5_disc(ld_ref[...], ar_ref[...], ai_ref[...], br_ref[...], bi_ref[...], e_ref[...])
        lr_ref[...] = lr
        li_ref[...] = li
        bbr_ref[...] = bbr
        bbi_ref[...] = bbi

    g, p, pc = SSM_GROUPS, SSM_STATE, SSM_STATE * SSM_GROUP
    return pl.pallas_call(
        body, name="s5_params_fwd",
        out_shape=(jax.ShapeDtypeStruct((g, p), F32), jax.ShapeDtypeStruct((g, p), F32),
                   jax.ShapeDtypeStruct((g, pc), F32), jax.ShapeDtypeStruct((g, pc), F32)),
    )(log_dt, a_re, a_im, b_re, b_im, expand)


def s5_params_bwd(log_dt, a_re, a_im, b_re, b_im, dlr, dli, dbbr, dbbi):
    expand = _expand_matrix()

    def body(ld_ref, ar_ref, ai_ref, br_ref, bi_ref, e_ref, dlr_ref, dli_ref, dbbr_ref, dbbi_ref,
             dld_ref, dar_ref, dai_ref, dbr_ref, dbi_ref):
        e = e_ref[...]
        f = lambda ld, ar, ai, br, bi: _s5_disc(ld, ar, ai, br, bi, e)
        _, vjp = jax.vjp(f, ld_ref[...], ar_ref[...], ai_ref[...], br_ref[...], bi_ref[...])
        dld, dar, dai, dbr, dbi = vjp((dlr_ref[...], dli_ref[...], dbbr_ref[...], dbbi_ref[...]))
        dld_ref[...] = dld
        dar_ref[...] = dar
        dai_ref[...] = dai
        dbr_ref[...] = dbr
        dbi_ref[...] = dbi

    g, p, pc = SSM_GROUPS, SSM_STATE, SSM_STATE * SSM_GROUP
    return pl.pallas_call(
        body, name="s5_params_bwd",
        out_shape=(jax.ShapeDtypeStruct((g, 1), F32), jax.ShapeDtypeStruct((g, p), F32),
                   jax.ShapeDtypeStruct((g, p), F32), jax.ShapeDtypeStruct((g, pc), F32),
                   jax.ShapeDtypeStruct((g, pc), F32)),
    )(log_dt, a_re, a_im, b_re, b_im, expand, dlr, dli, dbbr, dbbi)


def _cmul(ar, ai, br, bi):
    return ar * br - ai * bi, ar * bi + ai * br


def _power_rows(lr, li, conj_rev):
    row = lax.broadcasted_iota(jnp.int32, (8, NS), 0)
    tr = jnp.zeros((8, NS), F32)
    ti = jnp.zeros((8, NS), F32)
    pr, pi = lr, li
    for r in range(8):
        dst = 7 - r if conj_rev else r
        tr = jnp.where(row == dst, pr, tr)
        ti = jnp.where(row == dst, -pi if conj_rev else pi, ti)
        if r < 7:
            pr, pi = _cmul(pr, pi, lr, li)
    return tr, ti


def _scan8(xr, xi, tr_ref, ti_ref, cr, ci, reverse):
    row = lax.broadcasted_iota(jnp.int32, xr.shape, 0)
    for d in (1, 2, 4):
        if reverse:
            sr, si = pltpu.roll(xr, 8 - d, 0), pltpu.roll(xi, 8 - d, 0)
            keep = row < 8 - d
            pw = 8 - d
        else:
            sr, si = pltpu.roll(xr, d, 0), pltpu.roll(xi, d, 0)
            keep = row >= d
            pw = d - 1
        mr, mi = _cmul(tr_ref[pl.ds(pw, 1), :], ti_ref[pl.ds(pw, 1), :], sr, si)
        xr = xr + jnp.where(keep, mr, 0.0)
        xi = xi + jnp.where(keep, mi, 0.0)
    mr, mi = _cmul(tr_ref[...], ti_ref[...], cr, ci)
    return xr + mr, xi + mi


def _row_of(x, r):
    row = lax.broadcasted_iota(jnp.int32, x.shape, 0)
    return jnp.sum(jnp.where(row == r, x, 0.0), axis=0, keepdims=True)


def s5_scan_fwd(u, lam_r, lam_i, bre, bim, cre, cim, dskip):
    s = u.shape[0]
    tt = min(SCAN_TILE, s)
    nb = tt // 8

    def body(nt, u_ref, lr_ref, li_ref, bre_ref, bim_ref, cre_ref, cim_ref, d_ref,
             xr_ref, xi_ref, y_ref, yg_ref, tr_ref, ti_ref, cr_ref, ci_ref):
        i = pl.program_id(0)

        @pl.when(i == 0)
        def _():
            tr, ti = _power_rows(lr_ref[...], li_ref[...], False)
            tr_ref[...] = tr
            ti_ref[...] = ti
            cr_ref[...] = jnp.zeros_like(cr_ref)
            ci_ref[...] = jnp.zeros_like(ci_ref)

        uv = u_ref[...]
        ub = uv.astype(BF16)
        for q in range(NQ):
            uq = ub[:, q * LANES:(q + 1) * LANES]
            xr_ref[:, q * SQ:(q + 1) * SQ] = jnp.dot(uq, bre_ref[q], preferred_element_type=F32)
            xi_ref[:, q * SQ:(q + 1) * SQ] = jnp.dot(uq, bim_ref[q], preferred_element_type=F32)

        def step(b, carry):
            cr, ci = carry
            rows = pl.ds(pl.multiple_of(b * 8, 8), 8)
            xr, xi = _scan8(xr_ref[rows, :], xi_ref[rows, :], tr_ref, ti_ref, cr, ci, False)
            xr_ref[rows, :] = xr
            xi_ref[rows, :] = xi
            return _row_of(xr, 7), _row_of(xi, 7)

        cr, ci = lax.fori_loop(0, nb, step, (cr_ref[...], ci_ref[...]))
        cr_ref[...] = cr
        ci_ref[...] = ci
        y = d_ref[...] * uv
        for q in range(NQ):
            yq = (jnp.dot(xr_ref[:, q * SQ:(q + 1) * SQ].astype(BF16), cre_ref[q], preferred_element_type=F32)
                  - jnp.dot(xi_ref[:, q * SQ:(q + 1) * SQ].astype(BF16), cim_ref[q], preferred_element_type=F32))
            y_ref[:, q * LANES:(q + 1) * LANES] = yq + y[:, q * LANES:(q + 1) * LANES]
        yg_ref[...] = _gelu(y_ref[...]).astype(BF16)

    return rowcall(body, rows=s, ts=tt,
                   ins=[(u, "row"), (lam_r, "full"), (lam_i, "full"), (bre, "full"), (bim, "full"),
                        (cre, "full"), (cim, "full"), (dskip, "full")],
                   outs=[((s, NS), F32, "row"), ((s, NS), F32, "row"), ((s, SSM_WIDTH), F32, "row"),
                         ((s, SSM_WIDTH), BF16, "row")], name="s5_scan_fwd",
                   scratch=[pltpu.VMEM((8, NS), F32), pltpu.VMEM((8, NS), F32),
                            pltpu.VMEM((1, NS), F32), pltpu.VMEM((1, NS), F32)])


def s5_scan_bwd(dyg, y, lam_r, lam_i, cre, cim):
    s = y.shape[0]
    tt = min(SCAN_TILE, s)
    nb = tt // 8

    def body(nt, dyg_ref, y_ref, lr_ref, li_ref, cre_ref, cim_ref,
             ar_ref, ai_ref, dy_ref, tr_ref, ti_ref, cr_ref, ci_ref):
        i = pl.program_id(0)

        @pl.when(i == 0)
        def _():
            tr, ti = _power_rows(lr_ref[...], li_ref[...], True)
            tr_ref[...] = tr
            ti_ref[...] = ti
            cr_ref[...] = jnp.zeros_like(cr_ref)
            ci_ref[...] = jnp.zeros_like(ci_ref)

        _, vjp = jax.vjp(_gelu, y_ref[...])
        dy = vjp(dyg_ref[...])[0]
        dyb = dy.astype(BF16)
        dy_ref[...] = dyb
        for q in range(NQ):
            dq = dyb[:, q * LANES:(q + 1) * LANES]
            ar_ref[:, q * SQ:(q + 1) * SQ] = lax.dot_general(dq, cre_ref[q], _NT, preferred_element_type=F32)
            ai_ref[:, q * SQ:(q + 1) * SQ] = -lax.dot_general(dq, cim_ref[q], _NT, preferred_element_type=F32)

        def step(b, carry):
            cr, ci = carry
            rows = pl.ds(pl.multiple_of((nb - 1 - b) * 8, 8), 8)
            xr, xi = _scan8(ar_ref[rows, :], ai_ref[rows, :], tr_ref, ti_ref, cr, ci, True)
            ar_ref[rows, :] = xr
            ai_ref[rows, :] = xi
            return _row_of(xr, 0), _row_of(xi, 0)

        cr, ci = lax.fori_loop(0, nb, step, (cr_ref[...], ci_ref[...]))
        cr_ref[...] = cr
        ci_ref[...] = ci

    return rowcall(body, rows=s, ts=tt,
                   ins=[(dyg, "rev"), (y, "rev"), (lam_r, "full"), (lam_i, "full"), (cre, "full"), (cim, "full")],
                   outs=[((s, NS), F32, "rev"), ((s, NS), F32, "rev"), ((s, SSM_WIDTH), BF16, "rev")],
                   name="s5_scan_bwd",
                   scratch=[pltpu.VMEM((8, NS), F32), pltpu.VMEM((8, NS), F32),
                            pltpu.VMEM((1, NS), F32), pltpu.VMEM((1, NS), F32)])


def s5_grads(u, dy, xr, xi, ar, ai, bre, bim, dskip):
    s = u.shape[0]
    tt = min(SCAN_TILE, s)

    def body(nt, u_ref, dy_ref, xr_ref, xrp_ref, xi_ref, xip_ref, ar_ref, ai_ref, bre_ref, bim_ref, d_ref,
             du_ref, dlr_ref, dli_ref, dbr_ref, dbi_ref, dcr_ref, dci_ref, dd_ref, er_ref, ei_ref):
        i = pl.program_id(0)

        @pl.when(i == 0)
        def _():
            for r in (dbr_ref, dbi_ref, dcr_ref, dci_ref):
                r[...] = jnp.zeros_like(r)

        uv, dyb = u_ref[...], dy_ref[...]
        dyf = dyb.astype(F32)
        av_r, av_i, xv_r, xv_i = ar_ref[...], ai_ref[...], xr_ref[...], xi_ref[...]
        er_ref[pl.ds(0, 8), :] = jnp.where(i > 0, xrp_ref[...], 0.0)
        ei_ref[pl.ds(0, 8), :] = jnp.where(i > 0, xip_ref[...], 0.0)
        er_ref[pl.ds(8, tt), :] = xv_r
        ei_ref[pl.ds(8, tt), :] = xv_i
        sr, si = er_ref[pl.ds(7, tt), :], ei_ref[pl.ds(7, tt), :]
        _acc(dlr_ref, i, av_r * sr + av_i * si)
        _acc(dli_ref, i, av_i * sr - av_r * si)
        _acc(dd_ref, i, dyf * uv)
        ub = uv.astype(BF16)
        ab_r, ab_i = av_r.astype(BF16), av_i.astype(BF16)
        xb_r, xb_i = xv_r.astype(BF16), xv_i.astype(BF16)
        du = d_ref[...] * dyf
        for q in range(NQ):
            cs, ss = slice(q * LANES, (q + 1) * LANES), slice(q * SQ, (q + 1) * SQ)
            dbr_ref[q] += lax.dot_general(ub[:, cs], ab_r[:, ss], _TN, preferred_element_type=F32)
            dbi_ref[q] += lax.dot_general(ub[:, cs], ab_i[:, ss], _TN, preferred_element_type=F32)
            dcr_ref[q] += lax.dot_general(xb_r[:, ss], dyb[:, cs], _TN, preferred_element_type=F32)
            dci_ref[q] -= lax.dot_general(xb_i[:, ss], dyb[:, cs], _TN, preferred_element_type=F32)
            du_ref[:, cs] = (du[:, cs]
                             + lax.dot_general(ab_r[:, ss], bre_ref[q], _NT, preferred_element_type=F32)
                             + lax.dot_general(ab_i[:, ss], bim_ref[q], _NT, preferred_element_type=F32))

    return rowcall(body, rows=s, ts=tt,
                   ins=[(u, "row"), (dy, "row"), (xr, "row"), (xr, "prev:8"), (xi, "row"), (xi, "prev:8"),
                        (ar, "row"), (ai, "row"), (bre, "full"), (bim, "full"), (dskip, "full")],
                   outs=[((s, SSM_WIDTH), F32, "row"), ((8, NS), F32, "acc"), ((8, NS), F32, "acc"),
                         ((NQ, LANES, SQ), F32, "acc"), ((NQ, LANES, SQ), F32, "acc"),
                         ((NQ, SQ, LANES), F32, "acc"), ((NQ, SQ, LANES), F32, "acc"),
                         ((8, SSM_WIDTH), F32, "acc")], name="s5_grads",
                   scratch=[pltpu.VMEM((tt + 8, NS), F32), pltpu.VMEM((tt + 8, NS), F32)])


def _glu_fn(za, zb):
    return za * _sigmoid(zb)


def glu_res_fwd(z, xres):
    s = z.shape[0]

    def body(nt, z_ref, x_ref, o_ref):
        o_ref[...] = x_ref[...] + _glu_fn(z_ref[:, :D_MODEL].astype(F32), z_ref[:, D_MODEL:].astype(F32))

    return rowcall(body, rows=s, ts=min(ROW_TILE, s), ins=[(z, "row"), (xres, "row")],
                   outs=[((s, D_MODEL), F32, "row")], name="glu_res_fwd")[0]


def glu_bwd(z, dout):
    s, c = z.shape

    def body(nt, z_ref, d_ref, dz_ref, dba_ref, dbb_ref):
        i = pl.program_id(0)
        _, vjp = jax.vjp(_glu_fn, z_ref[:, :D_MODEL].astype(F32), z_ref[:, D_MODEL:].astype(F32))
        dza, dzb = vjp(d_ref[...])
        dz_ref[:, :D_MODEL] = dza.astype(BF16)
        dz_ref[:, D_MODEL:] = dzb.astype(BF16)
        _acc(dba_ref, i, dza)
        _acc(dbb_ref, i, dzb)

    return rowcall(body, rows=s, ts=min(ROW_TILE, s), ins=[(z, "row"), (dout, "row")],
                   outs=[((s, c), BF16, "row"), ((8, D_MODEL), F32, "acc"), ((8, D_MODEL), F32, "acc")],
                   name="glu_bwd")


def loss_head(x, g, target):
    s, c = x.shape

    def body(nt, x_ref, g_ref, t_ref, loss_ref, dx_ref, dg_ref):
        i = pl.program_id(0)
        y, vjp = jax.vjp(_rms, x_ref[...], g_ref[...])
        err = y - t_ref[...]
        dx, dg = vjp(err * (1.0 / c))
        dx_ref[...] = dx
        _acc(dg_ref, i, dg)
        part = jnp.sum(jnp.sum(err * err, axis=-1, keepdims=True), axis=0, keepdims=True) * (0.5 / c)

        @pl.when(i == 0)
        def _():
            loss_ref[...] = jnp.zeros_like(loss_ref)

        loss_ref[...] += jnp.broadcast_to(part, loss_ref.shape)

    return rowcall(body, rows=s, ts=min(ROW_TILE, s), ins=[(x, "row"), (g, "full"), (target, "row")],
                   outs=[((8, LANES), F32, "acc"), ((s, c), F32, "row"), ((8, c), F32, "acc")], name="loss_head")


def _tile_rows(r, cands=(512, 256, 128, 64, 32, 16, 8)):
    return _pick(r, cands)


def add_to_bf16(a, b, name):
    n, r, c = a.shape
    tr = _tile_rows(r)

    def body(a_ref, b_ref, o_ref):
        o_ref[...] = (a_ref[...].astype(F32) + b_ref[...].astype(F32)).astype(BF16)

    spec = pl.BlockSpec((1, tr, c), lambda j, i: (j, i, 0))
    return pl.pallas_call(body, name=name, grid=(n, r // tr), in_specs=[spec, spec], out_specs=spec,
                          out_shape=jax.ShapeDtypeStruct((n, r, c), BF16),
                          compiler_params=_cparams("parallel", "parallel"))(a, b)


def sum_leading(a, name):
    n, r, c = a.shape
    tr = _tile_rows(r)

    def body(a_ref, o_ref):
        acc = a_ref[0].astype(F32)
        for k in range(1, n):
            acc = acc + a_ref[k].astype(F32)
        o_ref[...] = acc

    return pl.pallas_call(body, name=name, grid=(r // tr,),
                          in_specs=[pl.BlockSpec((n, tr, c), lambda i: (0, i, 0))],
                          out_specs=pl.BlockSpec((tr, c), lambda i: (i, 0)),
                          out_shape=jax.ShapeDtypeStruct((r, c), F32),
                          compiler_params=_cparams("parallel"))(a)


def adamw(w, g, m, v, name):
    r, c = w.shape
    tr = _tile_rows(r, (256, 128, 64, 32, 16, 8))
    c1 = 1.0 - ADAM_B1 ** ADAM_STEP
    c2 = 1.0 - ADAM_B2 ** ADAM_STEP

    def body(w_ref, g_ref, m_ref, v_ref, d_ref, nm_ref, nv_ref):
        gv = g_ref[...]
        mn = ADAM_B1 * m_ref[...] + (1.0 - ADAM_B1) * gv
        vn = ADAM_B2 * v_ref[...] + (1.0 - ADAM_B2) * (gv * gv)
        d_ref[...] = -ADAM_LR * ((mn / c1) / (jnp.sqrt(vn / c2) + ADAM_EPS) + ADAM_WD * w_ref[...])
        nm_ref[...] = mn
        nv_ref[...] = vn

    spec = pl.BlockSpec((tr, c), lambda i: (i, 0))
    shp = jax.ShapeDtypeStruct((r, c), F32)
    return pl.pallas_call(body, name=name, grid=(r // tr,), in_specs=[spec] * 4, out_specs=(spec,) * 3,
                          out_shape=(shp,) * 3, compiler_params=_cparams("parallel"))(w, g, m, v)


_ANY = pl.BlockSpec(memory_space=pl.ANY)


def all_gather8(block, name):
    r, c = block.shape

    def body(x_ref, out_ref, send_sems, recv_sems, local_sem):
        x, y, cc = lax.axis_index("x"), lax.axis_index("y"), lax.axis_index("c")
        me, sibling = (x, y, cc), (x, y, 1 - cc)
        chips = [(1 - x, y), (x, 1 - y), (1 - x, 1 - y)]

        def slot(px, py, pc):
            return out_ref.at[4 * px + 2 * py + pc]

        def copy(k, blk, to, src=None):
            return pltpu.make_async_remote_copy(
                src_ref=slot(*blk) if src is None else src, dst_ref=slot(*blk),
                send_sem=send_sems.at[k], recv_sem=recv_sems.at[k], device_id=to, device_id_type=MESH)

        mine = pltpu.make_async_copy(x_ref, slot(*me), local_sem)
        mine.start()
        first = [copy(0, me, sibling, src=x_ref)]
        first += [copy(1 + j, me, (*chip, cc), src=x_ref) for j, chip in enumerate(chips)]
        for cp in first:
            cp.start()
        passed = [copy(4 + j, (*chip, cc), sibling) for j, chip in enumerate(chips)]
        for j, chip in enumerate(chips):
            copy(1 + j, (*chip, cc), me).wait_recv()
            passed[j].start()
        copy(0, sibling, me).wait_recv()
        for j, chip in enumerate(chips):
            copy(4 + j, (*chip, 1 - cc), me).wait_recv()
        for cp in first + passed:
            cp.wait_send()
        mine.wait()

    return pl.pallas_call(
        body, name=name, in_specs=[_ANY], out_specs=_ANY,
        out_shape=jax.ShapeDtypeStruct((8, r, c), block.dtype),
        scratch_shapes=[pltpu.SemaphoreType.DMA((7,)), pltpu.SemaphoreType.DMA((7,)), pltpu.SemaphoreType.DMA],
    )(block)


def sibling_swap(block, name):
    def body(x_ref, out_ref, send_sem, recv_sem):
        x, y, cc = lax.axis_index("x"), lax.axis_index("y"), lax.axis_index("c")
        cp = pltpu.make_async_remote_copy(src_ref=x_ref, dst_ref=out_ref, send_sem=send_sem, recv_sem=recv_sem,
                                          device_id=(x, y, 1 - cc), device_id_type=MESH)
        cp.start()
        cp.wait()

    return pl.pallas_call(
        body, name=name, in_specs=[_ANY], out_specs=_ANY,
        out_shape=jax.ShapeDtypeStruct(block.shape, block.dtype),
        scratch_shapes=[pltpu.SemaphoreType.DMA, pltpu.SemaphoreType.DMA],
    )(block)


def chip_exchange(parts, name):
    def body(p_ref, out_ref, send_sems, recv_sems, local_sem):
        x, y, cc = lax.axis_index("x"), lax.axis_index("y"), lax.axis_index("c")
        me = 2 * x + y
        chips = [(1 - x, y), (x, 1 - y), (1 - x, 1 - y)]
        mine = pltpu.make_async_copy(p_ref.at[me], out_ref.at[me], local_sem)
        mine.start()
        sends = []
        for k, (px, py) in enumerate(chips):
            sends.append(pltpu.make_async_remote_copy(
                src_ref=p_ref.at[2 * px + py], dst_ref=out_ref.at[me],
                send_sem=send_sems.at[k], recv_sem=recv_sems.at[k], device_id=(px, py, cc), device_id_type=MESH))
        for cp in sends:
            cp.start()
        for k, (px, py) in enumerate(chips):
            pltpu.make_async_remote_copy(
                src_ref=p_ref.at[me], dst_ref=out_ref.at[2 * px + py],
                send_sem=send_sems.at[k], recv_sem=recv_sems.at[k], device_id=(px, py, cc),
                device_id_type=MESH).wait_recv()
        for cp in sends:
            cp.wait_send()
        mine.wait()

    return pl.pallas_call(
        body, name=name, in_specs=[_ANY], out_specs=_ANY,
        out_shape=jax.ShapeDtypeStruct(parts.shape, parts.dtype),
        scratch_shapes=[pltpu.SemaphoreType.DMA((3,)), pltpu.SemaphoreType.DMA((3,)), pltpu.SemaphoreType.DMA],
    )(parts)


PACK_COLS = 1024
SHARDED = (("l0_w_in", 1), ("l0_w_uq", 1), ("l0_w_ukv", 1), ("l0_w_out", 0), ("l0_w_up", 1), ("l0_w_down", 0),
           ("l1_w_in", 0), ("l1_w_glu", 1), ("l1_w_up", 1), ("l1_w_down", 0),
           ("l0_conv_w", 1), ("l0_ffn_conv_w", 1), ("l1_ffn_conv_w", 1))
REPLICATED = ("l0_mix_norm", "l0_conv_b", "l0_conv_ln_g", "l0_conv_ln_b", "l0_q_norm", "l0_kv_norm", "l0_ffn_norm",
              "l0_ffn_conv_b", "l1_mix_norm", "l1_log_dt", "l1_a_re", "l1_a_im", "l1_b_re", "l1_b_im", "l1_c_re",
              "l1_c_im", "l1_d", "l1_b_glu", "l1_ffn_norm", "l1_ffn_conv_b", "final_norm")


def _pack(arrs, dtype, mult):
    flat = jnp.concatenate([a.reshape(-1).astype(dtype) for a in arrs])
    n = flat.shape[0]
    total = -(-n // mult) * mult
    return jnp.pad(flat, (0, total - n))


def _unpack(flat, shapes):
    out, pos = [], 0
    for shp in shapes:
        n = int(np.prod(shp))
        out.append(flat[pos:pos + n].reshape(shp))
        pos += n
    return out


def _shard(full, axis, j):
    n = full.shape[axis] // N_CHIPS
    return lax.slice_in_dim(full, j * n, (j + 1) * n, axis=axis)


def _block_diag(t):
    q, g, a, b = t.shape
    eye = jnp.eye(g, dtype=t.dtype)
    return jnp.einsum("qgab,gh->qgahb", t, eye).reshape(q, g * a, g * b)


def _block_diag_t(d, a, b):
    q = d.shape[0]
    d5 = d.reshape(q, 8, a, 8, b)
    eye = jnp.eye(8, dtype=d.dtype)
    return jnp.einsum("qgahb,gh->qgab", d5, eye)


def kernel(x, l0_mix_norm, l0_w_in, l0_conv_w, l0_conv_b, l0_conv_ln_g, l0_conv_ln_b, l0_q_norm, l0_kv_norm, l0_w_uq, l0_w_ukv, l0_w_out, l0_ffn_norm, l0_w_up, l0_ffn_conv_w, l0_ffn_conv_b, l0_w_down, l1_mix_norm, l1_w_in, l1_log_dt, l1_a_re, l1_a_im, l1_b_re, l1_b_im, l1_c_re, l1_c_im, l1_d, l1_w_glu, l1_b_glu, l1_ffn_norm, l1_w_up, l1_ffn_conv_w, l1_ffn_conv_b, l1_w_down, final_norm, loss_target, m_l0_mix_norm, m_l0_w_in, m_l0_conv_w, m_l0_conv_b, m_l0_conv_ln_g, m_l0_conv_ln_b, m_l0_q_norm, m_l0_kv_norm, m_l0_w_uq, m_l0_w_ukv, m_l0_w_out, m_l0_ffn_norm, m_l0_w_up, m_l0_ffn_conv_w, m_l0_ffn_conv_b, m_l0_w_down, m_l1_mix_norm, m_l1_w_in, m_l1_log_dt, m_l1_a_re, m_l1_a_im, m_l1_b_re, m_l1_b_im, m_l1_c_re, m_l1_c_im, m_l1_d, m_l1_w_glu, m_l1_b_glu, m_l1_ffn_norm, m_l1_w_up, m_l1_ffn_conv_w, m_l1_ffn_conv_b, m_l1_w_down, m_final_norm, v_l0_mix_norm, v_l0_w_in, v_l0_conv_w, v_l0_conv_b, v_l0_conv_ln_g, v_l0_conv_ln_b, v_l0_q_norm, v_l0_kv_norm, v_l0_w_uq, v_l0_w_ukv, v_l0_w_out, v_l0_ffn_norm, v_l0_w_up, v_l0_ffn_conv_w, v_l0_ffn_conv_b, v_l0_w_down, v_l1_mix_norm, v_l1_w_in, v_l1_log_dt, v_l1_a_re, v_l1_a_im, v_l1_b_re, v_l1_b_im, v_l1_c_re, v_l1_c_im, v_l1_d, v_l1_w_glu, v_l1_b_glu, v_l1_ffn_norm, v_l1_w_up, v_l1_ffn_conv_w, v_l1_ffn_conv_b, v_l1_w_down, v_final_norm):
    a = dict(locals())
    w = {n: a[n] for n in [s for s, _ in SHARDED] + list(REPLICATED)}
    mom = {n: a["m_" + n] for n in w}
    var = {n: a["v_" + n] for n in w}
    return _step(a["x"][0], a["loss_target"][0], w, mom, var)


def _gather_weights(w):
    cc = lax.axis_index("c")
    big = [n for n, _ in SHARDED[:10]]
    small = [n for n, _ in SHARDED[10:]]
    full = {}
    for names, dtype, mult in ((big, BF16, 2 * 256 * PACK_COLS), (small, F32, 2 * 8 * LANES)):
        cols = PACK_COLS if dtype == BF16 else LANES
        flat = _pack([w[n] for n in names], dtype, mult)
        half = lax.dynamic_index_in_dim(flat.reshape(2, -1, cols), cc, axis=0, keepdims=False)
        got = all_gather8(half, "gather_" + ("matrices" if dtype == BF16 else "conv_weights"))
        got = got.reshape(N_CHIPS, -1)
        shapes = [w[n].shape for n in names]
        per_chip = [_unpack(got[j], shapes) for j in range(N_CHIPS)]
        for k, n in enumerate(names):
            axis = dict(SHARDED)[n]
            full[n] = jnp.concatenate([per_chip[j][k] for j in range(N_CHIPS)], axis=axis)
    return full


def _reduce_sharded(grads):
    cc = lax.axis_index("c")
    names = [n for n, _ in SHARDED]
    axes = dict(SHARDED)
    mult = 2 * 256 * PACK_COLS
    packs = [_pack([_shard(grads[n], axes[n], j) for n in names], BF16, mult) for j in range(N_CHIPS)]
    g = jnp.stack(packs).reshape(N_CHIPS, 2, -1, PACK_COLS)
    keep = lax.dynamic_index_in_dim(g, cc, axis=1, keepdims=False)
    give = lax.dynamic_index_in_dim(g, 1 - cc, axis=1, keepdims=False)
    got = sibling_swap(give, "grad_swap_halves")
    parts = add_to_bf16(keep, got, "grad_add_sibling")
    landed = chip_exchange(parts, "grad_chip_exchange")
    mine = sum_leading(landed, "grad_sum_chips")
    theirs = sibling_swap(mine, "grad_swap_sums")
    lo = jnp.where(cc == 0, mine, theirs)
    hi = jnp.where(cc == 0, theirs, mine)
    flat = jnp.concatenate([lo.reshape(-1), hi.reshape(-1)])
    shapes = [_shard(grads[n], axes[n], 0).shape for n in names]
    return dict(zip(names, _unpack(flat, shapes)))


def _reduce_replicated(grads):
    names = list(REPLICATED)
    flat = _pack([grads[n] for n in names], F32, 256 * LANES).reshape(-1, LANES)
    got = all_gather8(flat, "gather_small_grads")
    tot = sum_leading(got, "sum_small_grads").reshape(-1)
    return dict(zip(names, _unpack(tot, [grads[n].shape for n in names]))), flat.shape


def _row(v):
    return v.reshape(1, -1).astype(F32)


def _pad_rows(wt, rows):
    return jnp.pad(wt.astype(F32), ((0, rows - wt.shape[0]), (0, 0)))


def _ffn_fwd(xin, g, wa, wb, cw, cb, wd, tag):
    xn = rms_fwd(xin, _row(g), f"{tag}_rms")
    hpa = matmul(xn, wa, out_dtype=BF16, name=f"{tag}_up_a")
    hpb = matmul(xn, wb, out_dtype=BF16, name=f"{tag}_up_b")
    cwa, cwb = _pad_rows(cw[:, :D_FF], 8), _pad_rows(cw[:, D_FF:], 8)
    act = ffn_act_fwd(hpa, hpb, cwa, cwb, _row(cb[:D_FF]), _row(cb[D_FF:]))
    xout = matmul(act, wd, res=xin, name=f"{tag}_down")
    return xout, (xin, xn, hpa, hpb, act)


def _ffn_bwd(dxout, saved, g, wa, wb, cw, cb, wd, tag):
    xin, xn, hpa, hpb, act = saved
    dact = matmul(dxout, wd, tb=True, out_dtype=BF16, name=f"{tag}_d_act")
    d_wd = matmul(act, dxout, ta=True, name=f"{tag}_d_wdown")
    cwa, cwb = _pad_rows(cw[:, :D_FF], 8), _pad_rows(cw[:, D_FF:], 8)
    wra, wrb = _pad_rows(cw[::-1, :D_FF], 8), _pad_rows(cw[::-1, D_FF:], 8)
    dpa, dpb, dwa, dwb, dba, dbb = ffn_act_bwd(hpa, hpb, dact, cwa, cwb, wra, wrb, _row(cb[:D_FF]), _row(cb[D_FF:]))
    dxn = matmul(dpa, wa, tb=True, name=f"{tag}_d_xn_a")
    dxn = matmul(dpb, wb, tb=True, res=dxn, name=f"{tag}_d_xn_b")
    d_wu = jnp.concatenate([matmul(xn, dpa, ta=True, name=f"{tag}_d_wup_a"),
                            matmul(xn, dpb, ta=True, name=f"{tag}_d_wup_b")], axis=1)
    dxin, dg = rms_bwd(xin, _row(g), dxn, dxout, f"{tag}_rms_bwd")
    d_cw = jnp.concatenate([dwa[:FFN_K], dwb[:FFN_K]], axis=1)
    d_cb = jnp.concatenate([dba[0], dbb[0]])
    return dxin, dg[0], d_wu, d_cw, d_cb, d_wd


def _step(x, target, w, mom, var):
    s = x.shape[0]
    full = _gather_weights(w)
    cos, sin = rope_tables(s)

    w_in0 = full["l0_w_in"]
    w_in0p = jnp.concatenate([w_in0, jnp.zeros((D_MODEL, H0_W - w_in0.shape[1]), BF16)], axis=1)
    wq = full["l0_w_uq"].reshape(Q_LORA, N_HEADS, QK_NOPE + QK_ROPE)
    zq = lambda n: jnp.zeros((Q_LORA, N_HEADS, n), BF16)
    w_uqp = jnp.concatenate([wq[..., :QK_NOPE], zq(LANES - QK_NOPE), wq[..., QK_NOPE:], zq(LANES - QK_ROPE)],
                            axis=-1).reshape(Q_LORA, N_HEADS * HEAD_PAD)
    w_ukv = full["l0_w_ukv"]
    w_out = full["l0_w_out"]
    w_out_u = w_out[:CONV_WIDTH]
    wo = w_out[CONV_WIDTH:].reshape(N_HEADS, V_DIM, D_MODEL)
    w_out_a = jnp.concatenate([jnp.zeros_like(wo), wo], axis=1).reshape(N_HEADS * LANES, D_MODEL)
    conv_w = _pad_rows(full["l0_conv_w"], CONV_HALO)
    conv_wrev = _pad_rows(full["l0_conv_w"][::-1], CONV_HALO)
    w_up0a, w_up0b = full["l0_w_up"][:, :D_FF], full["l0_w_up"][:, D_FF:]
    w_up1a, w_up1b = full["l1_w_up"][:, :D_FF], full["l1_w_up"][:, D_FF:]

    xn0 = rms_fwd(x, _row(w["l0_mix_norm"]), "l0_mix_rms")
    h0 = matmul(xn0, w_in0p, name="l0_in_proj")
    qn_g, kvn_g = _row(w["l0_q_norm"]), _row(w["l0_kv_norm"])
    u0, cq, ckv, kr = mixpre_fwd(h0, qn_g, kvn_g, cos, sin)
    cb, lg, lb = _row(w["l0_conv_b"]), _row(w["l0_conv_ln_g"]), _row(w["l0_conv_ln_b"])
    u = convln_fwd(u0, conv_w, cb, lg, lb)
    qraw = matmul(cq, w_uqp, name="l0_q_up")
    q = qrope_fwd(qraw, cos, sin)
    kv = matmul(ckv, w_ukv, out_dtype=BF16, name="l0_kv_up")
    o, lse = attn_fwd(q, kv, kr)
    x1 = matmul(u, w_out_u, res=x, name="l0_out_conv")
    x1 = matmul(o, w_out_a, res=x1, name="l0_out_attn")

    x2, ffn0 = _ffn_fwd(x1, w["l0_ffn_norm"], w_up0a, w_up0b, full["l0_ffn_conv_w"], w["l0_ffn_conv_b"],
                        full["l0_w_down"], "l0_ffn")

    g_, p_, c_ = SSM_GROUPS, SSM_STATE, SSM_GROUP
    s5_in = (w["l1_log_dt"].reshape(g_, 1), w["l1_a_re"], w["l1_a_im"],
             w["l1_b_re"].reshape(g_, p_ * c_), w["l1_b_im"].reshape(g_, p_ * c_))
    lam_r, lam_i, bb_r, bb_i = s5_params_fwd(*s5_in)
    lam_rf, lam_if = lam_r.reshape(1, NS), lam_i.reshape(1, NS)

    def b_blocks(bb):
        t = bb.reshape(NQ, 8, p_, c_).transpose(0, 1, 3, 2)
        return _block_diag(t).astype(BF16)

    def c_blocks(cm):
        t = cm.reshape(NQ, 8, c_, p_).transpose(0, 1, 3, 2)
        return _block_diag(t).astype(BF16)

    bre, bim = b_blocks(bb_r), b_blocks(bb_i)
    cre, cim = c_blocks(w["l1_c_re"]), c_blocks(w["l1_c_im"])
    dskip = _row(w["l1_d"])
    xn2 = rms_fwd(x2, _row(w["l1_mix_norm"]), "l1_mix_rms")
    u1 = matmul(xn2, full["l1_w_in"], name="l1_in_proj")
    xs_r, xs_i, y1, yg = s5_scan_fwd(u1, lam_rf, lam_if, bre, bim, cre, cim, dskip)
    z = matmul(yg, full["l1_w_glu"], bias=_row(w["l1_b_glu"]), out_dtype=BF16, name="l1_glu_proj")
    x3 = glu_res_fwd(z, x2)

    x4, ffn1 = _ffn_fwd(x3, w["l1_ffn_norm"], w_up1a, w_up1b, full["l1_ffn_conv_w"], w["l1_ffn_conv_b"],
                        full["l1_w_down"], "l1_ffn")
    loss_part, dx4, dgf = loss_head(x4, _row(w["final_norm"]), target)
    loss = lax.psum(loss_part[0, 0], ("x", "y", "c"))

    gr = {"final_norm": dgf[0]}

    dx3, gr["l1_ffn_norm"], gr["l1_w_up"], gr["l1_ffn_conv_w"], gr["l1_ffn_conv_b"], gr["l1_w_down"] = _ffn_bwd(
        dx4, ffn1, w["l1_ffn_norm"], w_up1a, w_up1b, full["l1_ffn_conv_w"], w["l1_ffn_conv_b"], full["l1_w_down"],
        "l1_ffn")

    dz, dbga, dbgb = glu_bwd(z, dx3)
    gr["l1_b_glu"] = jnp.concatenate([dbga[0], dbgb[0]])
    dyg = matmul(dz, full["l1_w_glu"], tb=True, name="l1_d_yg")
    gr["l1_w_glu"] = matmul(yg, dz, ta=True, name="l1_d_wglu")
    a_r, a_i, dy1 = s5_scan_bwd(dyg, y1, lam_rf, lam_if, cre, cim)
    du1, dlr, dli, dbr, dbi, dcr, dci, dd = s5_grads(u1, dy1, xs_r, xs_i, a_r, a_i, bre, bim, dskip)
    gr["l1_d"] = dd[0]

    def b_unblock(d):
        return _block_diag_t(d, c_, p_).transpose(0, 1, 3, 2).reshape(g_, p_ * c_)

    def c_unblock(d):
        return _block_diag_t(d, p_, c_).transpose(0, 1, 3, 2).reshape(g_, c_, p_)

    gr["l1_c_re"], gr["l1_c_im"] = c_unblock(dcr), c_unblock(dci)
    dld, dar, dai, dbre, dbim = s5_params_bwd(*s5_in, dlr[0].reshape(g_, p_), dli[0].reshape(g_, p_),
                                              b_unblock(dbr), b_unblock(dbi))
    gr["l1_log_dt"], gr["l1_a_re"], gr["l1_a_im"] = dld.reshape(g_), dar, dai
    gr["l1_b_re"], gr["l1_b_im"] = dbre.reshape(g_, p_, c_), dbim.reshape(g_, p_, c_)
    dxn2 = matmul(du1, full["l1_w_in"], tb=True, name="l1_d_xn")
    gr["l1_w_in"] = matmul(xn2, du1, ta=True, name="l1_d_win")
    dx2, dg = rms_bwd(x2, _row(w["l1_mix_norm"]), dxn2, dx3, "l1_mix_rms_bwd")
    gr["l1_mix_norm"] = dg[0]

    dx1, gr["l0_ffn_norm"], gr["l0_w_up"], gr["l0_ffn_conv_w"], gr["l0_ffn_conv_b"], gr["l0_w_down"] = _ffn_bwd(
        dx2, ffn0, w["l0_ffn_norm"], w_up0a, w_up0b, full["l0_ffn_conv_w"], w["l0_ffn_conv_b"], full["l0_w_down"],
        "l0_ffn")

    du = matmul(dx1, w_out_u, tb=True, out_dtype=BF16, name="l0_d_u")
    do = matmul(dx1, w_out_a, tb=True, out_dtype=BF16, name="l0_d_o")
    d_wout_u = matmul(u, dx1, ta=True, name="l0_d_wout_u")
    d_wout_a = matmul(o, dx1, ta=True, name="l0_d_wout_a")
    gr["l0_w_out"] = jnp.concatenate(
        [d_wout_u, d_wout_a.reshape(N_HEADS, LANES, D_MODEL)[:, LANES - V_DIM:].reshape(N_HEADS * V_DIM, D_MODEL)])
    dq, dkv, dkr = attn_bwd(q, kv, kr, o, do, lse)
    dqraw = qrope_bwd(dq, cos, sin)
    dcq = matmul(dqraw, w_uqp, tb=True, name="l0_d_cq")
    d_wuqp = matmul(cq, dqraw, ta=True, name="l0_d_wuq").reshape(Q_LORA, N_HEADS, HEAD_PAD)
    gr["l0_w_uq"] = jnp.concatenate([d_wuqp[..., :QK_NOPE], d_wuqp[..., LANES:LANES + QK_ROPE]],
                                    axis=-1).reshape(Q_LORA, -1)
    dckv = matmul(dkv, w_ukv, tb=True, name="l0_d_ckv")
    gr["l0_w_ukv"] = matmul(ckv, dkv, ta=True, name="l0_d_wukv")
    du1c, dlg, dlb, dcb = convln_bwd1(u0, conv_w, cb, lg, lb, du)
    gr["l0_conv_ln_g"], gr["l0_conv_ln_b"], gr["l0_conv_b"] = dlg[0], dlb[0], dcb[0]
    du0, dcw = convln_bwd2(u0, conv_wrev, du1c)
    gr["l0_conv_w"] = dcw[:CONV_K]
    dh0, dqn, dkvn = mixpre_bwd(h0, qn_g, kvn_g, cos, sin, du0, dcq, dckv, dkr)
    gr["l0_q_norm"], gr["l0_kv_norm"] = dqn[0], dkvn[0]
    dxn0 = matmul(dh0, w_in0p, tb=True, name="l0_d_xn")
    gr["l0_w_in"] = matmul(xn0, dh0, ta=True, name="l0_d_win")[:, :w_in0.shape[1]]
    grad_x, dg = rms_bwd(x, _row(w["l0_mix_norm"]), dxn0, dx1, "l0_mix_rms_bwd")
    gr["l0_mix_norm"] = dg[0]

    g_sh = _reduce_sharded(gr)
    g_rep, pack_shape = _reduce_replicated(gr)
    grad, delta, new_m, new_v = {}, {}, {}, {}
    for n, _ in SHARDED:
        shp = w[n].shape
        two_d = (lambda t: t.reshape(shp[0], -1))
        grad[n] = g_sh[n]
        delta[n], new_m[n], new_v[n] = adamw(two_d(w[n]), two_d(g_sh[n]), two_d(mom[n]), two_d(var[n]), f"adamw_{n}")
    names = list(REPLICATED)
    pk = lambda d: _pack([d[n] for n in names], F32, 256 * LANES).reshape(pack_shape)
    dl, nm, nv = adamw(pk(w), pk(g_rep), pk(mom), pk(var), "adamw_small")
    shapes = [w[n].shape for n in names]
    for n, d_, m_, v_ in zip(names, _unpack(dl.reshape(-1), shapes), _unpack(nm.reshape(-1), shapes),
                             _unpack(nv.reshape(-1), shapes)):
        grad[n], delta[n], new_m[n], new_v[n] = g_rep[n], d_, m_, v_

    order = ["l0_mix_norm", "l0_w_in", "l0_conv_w", "l0_conv_b", "l0_conv_ln_g", "l0_conv_ln_b", "l0_q_norm",
             "l0_kv_norm", "l0_w_uq", "l0_w_ukv", "l0_w_out", "l0_ffn_norm", "l0_w_up", "l0_ffn_conv_w",
             "l0_ffn_conv_b", "l0_w_down", "l1_mix_norm", "l1_w_in", "l1_log_dt", "l1_a_re", "l1_a_im", "l1_b_re",
             "l1_b_im", "l1_c_re", "l1_c_im", "l1_d", "l1_w_glu", "l1_b_glu", "l1_ffn_norm", "l1_w_up",
             "l1_ffn_conv_w", "l1_ffn_conv_b", "l1_w_down", "final_norm"]
    return (loss, grad_x[None], *[grad[n] for n in order], *[delta[n] for n in order],
            *[new_m[n] for n in order], *[new_v[n] for n in order])
```

```python
import functools
import math

import jax
import jax.numpy as jnp
import numpy as np
from jax import lax
from jax.experimental import pallas as pl
from jax.experimental.pallas import tpu as pltpu

F32 = jnp.float32
BF16 = jnp.bfloat16
MESH = pl.DeviceIdType.MESH

D_MODEL = 1024
EPS = 1e-6
LN_EPS = 1e-5
CONV_WIDTH = 512
CONV_K = 31
N_HEADS = 8
QK_NOPE = 64
QK_ROPE = 32
V_DIM = 64
Q_LORA = 256
KV_LORA = 128
ROPE_BASE = 10000.0
ATT_SCALE = (QK_NOPE + QK_ROPE) ** -0.5
SSM_WIDTH = 512
SSM_GROUP = 16
SSM_GROUPS = 32
SSM_STATE = 64
D_FF = 2816
FFN_K = 3
ADAM_LR = 0.001
ADAM_B1 = 0.9
ADAM_B2 = 0.999
ADAM_EPS = 1e-08
ADAM_WD = 0.01
ADAM_STEP = 10

N_CHIPS = 4
LANES = 128
HEAD_PAD = 256
CONV_HALO = 32
FFN_HALO = 16
VMEM_LIMIT = 56 * 1024 * 1024

ROW_TILE = 512
FFN_ROW_TILE = 1024
FFN_COL_TILE = 256
FFN_ROW_CHUNK = 64
ATT_TILE = 1024
SCAN_TILE = 256


def _cparams(*sem):
    return pltpu.CompilerParams(dimension_semantics=tuple(sem), vmem_limit_bytes=VMEM_LIMIT)


def _pick(n, cands):
    for c in cands:
        if n % c == 0:
            return c
    return n


def matmul(a, b, *, ta=False, tb=False, res=None, bias=None, out_dtype=None, name):
    if out_dtype is None:
        out_dtype = BF16 if ta else F32
    if ta:
        kdim, m = a.shape
    else:
        m, kdim = a.shape
    if tb:
        n, k2 = b.shape
    else:
        k2, n = b.shape
    assert kdim == k2, (a.shape, b.shape, ta, tb)
    tn = _pick(n, (1408, 1024, 768, 512, 384, 256, 128))
    if ta:
        tm = _pick(m, (1408, 1024, 512, 256, 128))
        tk = _pick(kdim, (512, 256, 128))
    else:
        tm = _pick(m, (1024, 512, 256, 128))
        tk = kdim
        if kdim > 1024:
            tn = _pick(n, (512, 256, 128))
        if tm * tn > 1024 * 1024 and out_dtype == F32:
            tm = _pick(m, (512, 256, 128))
    nk = kdim // tk
    has_res, has_bias = res is not None, bias is not None
    dims = (((0,) if ta else (1,), (1,) if tb else (0,)), ((), ()))

    def body(*refs):
        a_ref, b_ref = refs[0], refs[1]
        pos = 2
        res_ref = bias_ref = None
        if has_res:
            res_ref = refs[pos]
            pos += 1
        if has_bias:
            bias_ref = refs[pos]
            pos += 1
        o_ref = refs[pos]

        def finish(r):
            if has_bias:
                r = r + bias_ref[...]
            if has_res:
                r = r + res_ref[...].astype(F32)
            o_ref[...] = r.astype(o_ref.dtype)

        prod = lax.dot_general(a_ref[...].astype(BF16), b_ref[...].astype(BF16), dims, preferred_element_type=F32)
        if nk == 1:
            finish(prod)
            return
        acc_ref = refs[pos + 1]
        k = pl.program_id(2)

        @pl.when(k == 0)
        def _():
            acc_ref[...] = prod

        @pl.when(k > 0)
        def _():
            acc_ref[...] += prod

        @pl.when(k == nk - 1)
        def _():
            finish(acc_ref[...])

    a_spec = pl.BlockSpec((tk, tm), lambda i, j, k: (k, i)) if ta else pl.BlockSpec((tm, tk), lambda i, j, k: (i, k))
    b_spec = pl.BlockSpec((tn, tk), lambda i, j, k: (j, k)) if tb else pl.BlockSpec((tk, tn), lambda i, j, k: (k, j))
    in_specs = [a_spec, b_spec]
    args = [a, b]
    if has_res:
        in_specs.append(pl.BlockSpec((tm, tn), lambda i, j, k: (i, j)))
        args.append(res)
    if has_bias:
        in_specs.append(pl.BlockSpec((1, tn), lambda i, j, k: (0, j)))
        args.append(bias)
    return pl.pallas_call(
        body, name=name, grid=(m // tm, n // tn, nk),
        in_specs=in_specs, out_specs=pl.BlockSpec((tm, tn), lambda i, j, k: (i, j)),
        out_shape=jax.ShapeDtypeStruct((m, n), out_dtype),
        scratch_shapes=[pltpu.VMEM((tm, tn), F32)] if nk > 1 else [],
        compiler_params=_cparams("parallel", "parallel", "arbitrary"),
    )(*args)


def rowcall(body, *, rows, ts, ins, outs, name, scratch=()):
    nt = rows // ts
    in_specs, args = [], []
    for arr, kind in ins:
        if kind == "row":
            in_specs.append(pl.BlockSpec((ts, arr.shape[1]), lambda i: (i, 0)))
        elif kind == "rev":
            in_specs.append(pl.BlockSpec((ts, arr.shape[1]), lambda i: (nt - 1 - i, 0)))
        elif kind == "full":
            nd = arr.ndim
            in_specs.append(pl.BlockSpec(arr.shape, lambda i, nd=nd: (0,) * nd))
        elif kind.startswith("prev:"):
            h = int(kind[5:])
            r = ts // h
            in_specs.append(pl.BlockSpec((h, arr.shape[1]), lambda i, r=r: (jnp.maximum(i * r - 1, 0), 0)))
        elif kind.startswith("next:"):
            h = int(kind[5:])
            r = ts // h
            last = rows // h - 1
            in_specs.append(pl.BlockSpec((h, arr.shape[1]), lambda i, r=r, last=last: (jnp.minimum((i + 1) * r, last), 0)))
        elif kind.startswith("revprev:"):
            h = int(kind[8:])
            r = ts // h
            in_specs.append(pl.BlockSpec((h, arr.shape[1]), lambda i, r=r: (jnp.maximum((nt - 1 - i) * r - 1, 0), 0)))
        else:
            raise ValueError(kind)
        args.append(arr)
    out_specs, out_shapes = [], []
    for shape, dtype, kind in outs:
        if kind == "row":
            out_specs.append(pl.BlockSpec((ts, shape[1]), lambda i: (i, 0)))
        elif kind == "rev":
            out_specs.append(pl.BlockSpec((ts, shape[1]), lambda i: (nt - 1 - i, 0)))
        else:
            nd = len(shape)
            out_specs.append(pl.BlockSpec(tuple(shape), lambda i, nd=nd: (0,) * nd))
        out_shapes.append(jax.ShapeDtypeStruct(tuple(shape), dtype))
    return pl.pallas_call(
        functools.partial(body, nt), name=name, grid=(nt,),
        in_specs=in_specs, out_specs=tuple(out_specs), out_shape=tuple(out_shapes),
        scratch_shapes=list(scratch),
        compiler_params=_cparams("arbitrary"),
    )(*args)


def _rms(x, g):
    return x * lax.rsqrt(jnp.mean(x * x, axis=-1, keepdims=True) + EPS) * g


def _layer_norm(x, g, b):
    mu = jnp.mean(x, axis=-1, keepdims=True)
    xc = x - mu
    var = jnp.mean(xc * xc, axis=-1, keepdims=True)
    return xc * lax.rsqrt(var + LN_EPS) * g + b


def _sigmoid(x):
    return 1.0 / (1.0 + jnp.exp(-x))


def _silu(x):
    return x * _sigmoid(x)


def _gelu(x):
    return 0.5 * x * (1.0 + jnp.tanh(math.sqrt(2.0 / math.pi) * (x + 0.044715 * (x * x * x))))


def _acc(ref, i, val):
    s = jnp.sum(val, axis=0, keepdims=True)

    @pl.when(i == 0)
    def _():
        ref[...] = jnp.zeros_like(ref)

    ref[...] += jnp.broadcast_to(s, ref.shape)


def rms_fwd(x, g, name):
    s, c = x.shape

    def body(nt, x_ref, g_ref, o_ref):
        o_ref[...] = _rms(x_ref[...], g_ref[...]).astype(BF16)

    return rowcall(body, rows=s, ts=min(ROW_TILE, s), ins=[(x, "row"), (g, "full")],
                   outs=[((s, c), BF16, "row")], name=name)[0]


def rms_bwd(x, g, dxn, dres, name):
    s, c = x.shape

    def body(nt, x_ref, g_ref, d_ref, r_ref, dx_ref, dg_ref):
        i = pl.program_id(0)
        _, vjp = jax.vjp(_rms, x_ref[...], g_ref[...])
        dx, dg = vjp(d_ref[...].astype(F32))
        dx_ref[...] = dx + r_ref[...]
        _acc(dg_ref, i, dg)

    return rowcall(body, rows=s, ts=min(ROW_TILE, s),
                   ins=[(x, "row"), (g, "full"), (dxn, "row"), (dres, "row")],
                   outs=[((s, c), F32, "row"), ((8, c), F32, "acc")], name=name)


def _partner(t):
    lane = lax.broadcasted_iota(jnp.int32, t.shape, 1)
    half = QK_ROPE // 2
    return jnp.where(lane % QK_ROPE < half, pltpu.roll(t, LANES - half, 1), pltpu.roll(t, half, 1))


def _rope(t, cos, sin):
    return t * cos + _partner(t) * sin


def _rope_t(d, cos, sin):
    return d * cos + _partner(d * sin)


def rope_tables(s):
    half = QK_ROPE // 2
    inv = ROPE_BASE ** (-jnp.arange(half, dtype=F32) / half)
    ang = jnp.arange(s).astype(F32)[:, None] * inv[None, :]
    cos, sin = jnp.cos(ang), jnp.sin(ang)
    z = jnp.zeros((s, LANES - QK_ROPE), F32)
    return jnp.concatenate([cos, cos, z], axis=1), jnp.concatenate([-sin, sin, z], axis=1)


H0_A, H0_G, H0_Q, H0_KV, H0_KR, H0_W = 0, 512, 1024, 1280, 1408, 1536


def _mixpre_fn(a, g, q, kv, qn, kvn):
    return a * _sigmoid(g), _rms(q, qn), _rms(kv, kvn)


def _h0_parts(h_ref):
    return (h_ref[:, H0_A:H0_G], h_ref[:, H0_G:H0_Q], h_ref[:, H0_Q:H0_KV], h_ref[:, H0_KV:H0_KR])


def mixpre_fwd(h0, qn, kvn, cos, sin):
    s = h0.shape[0]

    def body(nt, h_ref, qn_ref, kvn_ref, cos_ref, sin_ref, u0_ref, cq_ref, ckv_ref, kr_ref):
        u0, cq, ckv = _mixpre_fn(*_h0_parts(h_ref), qn_ref[...], kvn_ref[...])
        u0_ref[...] = u0
        cq_ref[...] = cq.astype(BF16)
        ckv_ref[...] = ckv.astype(BF16)
        kr_ref[...] = _rope(h_ref[:, H0_KR:H0_W], cos_ref[...], sin_ref[...]).astype(BF16)

    return rowcall(body, rows=s, ts=min(ROW_TILE, s),
                   ins=[(h0, "row"), (qn, "full"), (kvn, "full"), (cos, "row"), (sin, "row")],
                   outs=[((s, CONV_WIDTH), F32, "row"), ((s, Q_LORA), BF16, "row"),
                         ((s, KV_LORA), BF16, "row"), ((s, LANES), BF16, "row")], name="mixpre_fwd")


def mixpre_bwd(h0, qn, kvn, cos, sin, du0, dcq, dckv, dkr):
    s = h0.shape[0]

    def body(nt, h_ref, qn_ref, kvn_ref, cos_ref, sin_ref, du0_ref, dcq_ref, dckv_ref, dkr_ref,
             dh_ref, dqn_ref, dkvn_ref):
        i = pl.program_id(0)
        _, vjp = jax.vjp(_mixpre_fn, *_h0_parts(h_ref), qn_ref[...], kvn_ref[...])
        da, dg, dq, dkv, dqn, dkvn = vjp((du0_ref[...], dcq_ref[...], dckv_ref[...]))
        dh_ref[:, H0_A:H0_G] = da.astype(BF16)
        dh_ref[:, H0_G:H0_Q] = dg.astype(BF16)
        dh_ref[:, H0_Q:H0_KV] = dq.astype(BF16)
        dh_ref[:, H0_KV:H0_KR] = dkv.astype(BF16)
        dkr = dkr_ref[:, :LANES]
        for h in range(1, N_HEADS):
            dkr = dkr + dkr_ref[:, h * LANES:(h + 1) * LANES]
        dh_ref[:, H0_KR:H0_W] = _rope_t(dkr, cos_ref[...], sin_ref[...]).astype(BF16)
        _acc(dqn_ref, i, dqn)
        _acc(dkvn_ref, i, dkvn)

    return rowcall(body, rows=s, ts=min(ROW_TILE, s),
                   ins=[(h0, "row"), (qn, "full"), (kvn, "full"), (cos, "row"), (sin, "row"),
                        (du0, "row"), (dcq, "row"), (dckv, "row"), (dkr, "row")],
                   outs=[((s, H0_W), BF16, "row"), ((8, Q_LORA), F32, "acc"), ((8, KV_LORA), F32, "acc")],
                   name="mixpre_bwd")


def _conv_taps(ext_ref, w_ref, ts, first, ntaps):
    acc = None
    for k in range(ntaps):
        term = w_ref[pl.ds(k, 1), :] * ext_ref[pl.ds(first + k, ts), :]
        acc = term if acc is None else acc + term
    return acc


def _ln_silu(u1, g, b):
    return _silu(_layer_norm(u1, g, b))


def convln_fwd(u0, w, b, lg, lb):
    s, c = u0.shape
    ts = min(ROW_TILE, s)

    def body(nt, cur_ref, prev_ref, w_ref, b_ref, lg_ref, lb_ref, o_ref, ext_ref):
        i = pl.program_id(0)
        ext_ref[pl.ds(0, CONV_HALO), :] = jnp.where(i > 0, prev_ref[...], 0.0)
        ext_ref[pl.ds(CONV_HALO, ts), :] = cur_ref[...]
        u1 = _conv_taps(ext_ref, w_ref, ts, CONV_HALO - (CONV_K - 1), CONV_K) + b_ref[...]
        o_ref[...] = _ln_silu(u1, lg_ref[...], lb_ref[...]).astype(BF16)

    return rowcall(body, rows=s, ts=ts,
                   ins=[(u0, "row"), (u0, f"prev:{CONV_HALO}"), (w, "full"), (b, "full"), (lg, "full"), (lb, "full")],
                   outs=[((s, c), BF16, "row")], name="convln_fwd",
                   scratch=[pltpu.VMEM((ts + CONV_HALO, c), F32)])[0]


def convln_bwd1(u0, w, b, lg, lb, du):
    s, c = u0.shape
    ts = min(ROW_TILE, s)

    def body(nt, cur_ref, prev_ref, w_ref, b_ref, lg_ref, lb_ref, du_ref, du1_ref, dlg_ref, dlb_ref, dcb_ref, ext_ref):
        i = pl.program_id(0)
        ext_ref[pl.ds(0, CONV_HALO), :] = jnp.where(i > 0, prev_ref[...], 0.0)
        ext_ref[pl.ds(CONV_HALO, ts), :] = cur_ref[...]
        u1 = _conv_taps(ext_ref, w_ref, ts, CONV_HALO - (CONV_K - 1), CONV_K) + b_ref[...]
        _, vjp = jax.vjp(_ln_silu, u1, lg_ref[...], lb_ref[...])
        du1, dlg, dlb = vjp(du_ref[...].astype(F32))
        du1_ref[...] = du1
        _acc(dlg_ref, i, dlg)
        _acc(dlb_ref, i, dlb)
        _acc(dcb_ref, i, du1)

    return rowcall(body, rows=s, ts=ts,
                   ins=[(u0, "row"), (u0, f"prev:{CONV_HALO}"), (w, "full"), (b, "full"), (lg, "full"), (lb, "full"),
                        (du, "row")],
                   outs=[((s, c), F32, "row"), ((8, c), F32, "acc"), ((8, c), F32, "acc"), ((8, c), F32, "acc")],
                   name="convln_bwd1", scratch=[pltpu.VMEM((ts + CONV_HALO, c), F32)])


def convln_bwd2(u0, wrev, du1):
    s, c = u0.shape
    ts = min(ROW_TILE, s)

    def body(nt, cur_ref, prev_ref, d_ref, dnext_ref, wrev_ref, du0_ref, dw_ref, ext_ref, dext_ref):
        i = pl.program_id(0)
        ext_ref[pl.ds(0, CONV_HALO), :] = jnp.where(i > 0, prev_ref[...], 0.0)
        ext_ref[pl.ds(CONV_HALO, ts), :] = cur_ref[...]
        d = d_ref[...]
        dext_ref[pl.ds(0, ts), :] = d
        dext_ref[pl.ds(ts, CONV_HALO), :] = jnp.where(i < nt - 1, dnext_ref[...], 0.0)
        du0_ref[...] = _conv_taps(dext_ref, wrev_ref, ts, 0, CONV_K)

        @pl.when(i == 0)
        def _():
            dw_ref[...] = jnp.zeros_like(dw_ref)

        first = CONV_HALO - (CONV_K - 1)
        for k in range(CONV_K):
            dw_ref[pl.ds(k, 1), :] += jnp.sum(d * ext_ref[pl.ds(first + k, ts), :], axis=0, keepdims=True)

    return rowcall(body, rows=s, ts=ts,
                   ins=[(u0, "row"), (u0, f"prev:{CONV_HALO}"), (du1, "row"), (du1, f"next:{CONV_HALO}"), (wrev, "full")],
                   outs=[((s, c), F32, "row"), ((CONV_HALO, c), F32, "acc")], name="convln_bwd2",
                   scratch=[pltpu.VMEM((ts + CONV_HALO, c), F32), pltpu.VMEM((ts + CONV_HALO, c), F32)])


def qrope_fwd(qraw, cos, sin):
    s = qraw.shape[0]

    def body(nt, q_ref, cos_ref, sin_ref, o_ref):
        cos_v, sin_v = cos_ref[...] * ATT_SCALE, sin_ref[...] * ATT_SCALE
        for h in range(N_HEADS):
            nope = q_ref[:, h * HEAD_PAD:h * HEAD_PAD + LANES] * ATT_SCALE
            o_ref[:, h * HEAD_PAD:h * HEAD_PAD + LANES] = nope.astype(BF16)
            r = q_ref[:, h * HEAD_PAD + LANES:(h + 1) * HEAD_PAD]
            o_ref[:, h * HEAD_PAD + LANES:(h + 1) * HEAD_PAD] = _rope(r, cos_v, sin_v).astype(BF16)

    return rowcall(body, rows=s, ts=min(ROW_TILE, s), ins=[(qraw, "row"), (cos, "row"), (sin, "row")],
                   outs=[((s, N_HEADS * HEAD_PAD), BF16, "row")], name="qrope_fwd")[0]


def qrope_bwd(dq, cos, sin):
    s = dq.shape[0]

    def body(nt, d_ref, cos_ref, sin_ref, o_ref):
        cos_v, sin_v = cos_ref[...] * ATT_SCALE, sin_ref[...] * ATT_SCALE
        for h in range(N_HEADS):
            nope = d_ref[:, h * HEAD_PAD:h * HEAD_PAD + LANES] * ATT_SCALE
            o_ref[:, h * HEAD_PAD:h * HEAD_PAD + LANES] = nope.astype(BF16)
            r = d_ref[:, h * HEAD_PAD + LANES:(h + 1) * HEAD_PAD].astype(F32)
            o_ref[:, h * HEAD_PAD + LANES:(h + 1) * HEAD_PAD] = _rope_t(r, cos_v, sin_v).astype(BF16)

    return rowcall(body, rows=s, ts=min(ROW_TILE, s), ins=[(dq, "row"), (cos, "row"), (sin, "row")],
                   outs=[((s, N_HEADS * HEAD_PAD), BF16, "row")], name="qrope_bwd")[0]


_NT = (((1,), (1,)), ((), ()))
_TN = (((0,), (0,)), ((), ()))


def _scores(q, kvr, diagonal):
    s = lax.dot_general(q, kvr, _NT, preferred_element_type=F32)
    if not diagonal:
        return s
    row = lax.broadcasted_iota(jnp.int32, s.shape, 0)
    col = lax.broadcasted_iota(jnp.int32, s.shape, 1)
    return jnp.where(col <= row, s, -jnp.inf)


def _on_causal_blocks(q_blk, k_blk, fn):
    @pl.when(k_blk < q_blk)
    def _():
        fn(False)

    @pl.when(k_blk == q_blk)
    def _():
        fn(True)


def attn_fwd(q, kv, kr):
    s = q.shape[0]
    t = min(ATT_TILE, s)
    n = s // t

    def body(q_ref, kv_ref, kr_ref, o_ref, lse_ref, m_ref, l_ref, acc_ref):
        i, j = pl.program_id(1), pl.program_id(2)

        @pl.when(j == 0)
        def _():
            m_ref[...] = jnp.full_like(m_ref, -jnp.inf)
            l_ref[...] = jnp.zeros_like(l_ref)
            acc_ref[...] = jnp.zeros_like(acc_ref)

        def block(diagonal):
            kvv = kv_ref[...]
            sc = _scores(q_ref[...], jnp.concatenate([kvv, kr_ref[...]], axis=1), diagonal)
            m_prev = m_ref[...]
            m_new = jnp.maximum(m_prev, jnp.max(sc, axis=-1, keepdims=True))
            alpha = jnp.exp(m_prev - m_new)
            p = jnp.exp(sc - m_new)
            l_ref[...] = alpha * l_ref[...] + jnp.sum(p, axis=-1, keepdims=True)
            acc_ref[...] = alpha * acc_ref[...] + jnp.dot(p.astype(BF16), kvv, preferred_element_type=F32)
            m_ref[...] = m_new

        _on_causal_blocks(i, j, block)

        @pl.when(j == i)
        def _():
            l = l_ref[...]
            o_ref[...] = (acc_ref[...] / l).astype(BF16)
            lse_ref[...] = jnp.broadcast_to(m_ref[...] + jnp.log(l), lse_ref.shape)

    return pl.pallas_call(
        body, name="attn_fwd", grid=(N_HEADS, n, n),
        in_specs=[pl.BlockSpec((t, HEAD_PAD), lambda h, i, j: (i, h)),
                  pl.BlockSpec((t, LANES), lambda h, i, j: (jnp.minimum(j, i), h)),
                  pl.BlockSpec((t, LANES), lambda h, i, j: (jnp.minimum(j, i), 0))],
        out_specs=(pl.BlockSpec((t, LANES), lambda h, i, j: (i, h)),
                   pl.BlockSpec((t, LANES), lambda h, i, j: (i, h))),
        out_shape=(jax.ShapeDtypeStruct((s, N_HEADS * LANES), BF16),
                   jax.ShapeDtypeStruct((s, N_HEADS * LANES), F32)),
        scratch_shapes=[pltpu.VMEM((t, 1), F32), pltpu.VMEM((t, 1), F32), pltpu.VMEM((t, LANES), F32)],
        compiler_params=_cparams("parallel", "parallel", "arbitrary"),
    )(q, kv, kr)


def attn_bwd(q, kv, kr, o, do, lse):
    s = q.shape[0]
    t = min(ATT_TILE, s)
    n = s // t

    def body(q_ref, kv_ref, kr_ref, o_ref, do_ref, lse_ref, dq_ref, dkv_ref, dkr_ref):
        j, i = pl.program_id(1), pl.program_id(2)

        @pl.when((i == 0) & (j == 0))
        def _():
            dq_ref[...] = jnp.zeros_like(dq_ref)

        @pl.when(i == 0)
        def _():
            dkv_ref[...] = jnp.zeros_like(dkv_ref)
            dkr_ref[...] = jnp.zeros_like(dkr_ref)

        def block(diagonal):
            qv, dov, kvv = q_ref[...], do_ref[...], kv_ref[...]
            kvr = jnp.concatenate([kvv, kr_ref[...]], axis=1)
            p = jnp.exp(_scores(qv, kvr, diagonal) - lse_ref[:, :1])
            dp = lax.dot_general(dov, kvv, _NT, preferred_element_type=F32)
            delta = jnp.sum(dov.astype(F32) * o_ref[...].astype(F32), axis=-1, keepdims=True)
            ds = (p * (dp - delta)).astype(BF16)
            dk = lax.dot_general(ds, qv, _TN, preferred_element_type=F32)
            dkv_ref[...] += lax.dot_general(p.astype(BF16), dov, _TN, preferred_element_type=F32) + dk[:, :LANES]
            dkr_ref[...] += dk[:, LANES:]
            rows = pl.ds(pl.multiple_of(i * t, t), t)
            dq_ref[rows, :] += jnp.dot(ds, kvr, preferred_element_type=F32)

        _on_causal_blocks(i, j, block)

    qi = lambda h, j, i: (jnp.maximum(i, j), h)
    kj = lambda h, j, i: (j, h)
    return pl.pallas_call(
        body, name="attn_bwd", grid=(N_HEADS, n, n),
        in_specs=[pl.BlockSpec((t, HEAD_PAD), qi), pl.BlockSpec((t, LANES), kj),
                  pl.BlockSpec((t, LANES), lambda h, j, i: (j, 0)),
                  pl.BlockSpec((t, LANES), qi), pl.BlockSpec((t, LANES), qi), pl.BlockSpec((t, LANES), qi)],
        out_specs=(pl.BlockSpec((s, HEAD_PAD), lambda h, j, i: (0, h)),
                   pl.BlockSpec((t, LANES), kj), pl.BlockSpec((t, LANES), kj)),
        out_shape=(jax.ShapeDtypeStruct((s, N_HEADS * HEAD_PAD), F32),
                   jax.ShapeDtypeStruct((s, N_HEADS * LANES), F32), jax.ShapeDtypeStruct((s, N_HEADS * LANES), F32)),
        compiler_params=_cparams("parallel", "arbitrary", "arbitrary"),
    )(q, kv, kr, o, do, lse)


def ffn_fwd(x, g, wa, wb, cwa, cwb, ba, bb, wd, name):
    s, d = x.shape
    f = wa.shape[1]
    ts, tf = min(FFN_ROW_TILE, s), FFN_COL_TILE
    hal = FFN_HALO
    nj = f // tf
    first = hal - (FFN_K - 1)
    rc = min(FFN_ROW_CHUNK, ts)

    def body(x_ref, xp_ref, g_ref, wa_ref, wb_ref, cwa_ref, cwb_ref, ba_ref, bb_ref, wd_ref,
             xo_ref, xn_ref, hpa_ref, hpb_ref, act_ref, xe_ref, ea_ref, eb_ref):
        i, j = pl.program_id(0), pl.program_id(1)

        @pl.when(j == 0)
        def _():
            xn = _rms(x_ref[...], g_ref[...]).astype(BF16)
            xn_ref[...] = xn
            xe_ref[pl.ds(hal, ts), :] = xn
            xe_ref[pl.ds(0, hal), :] = jnp.where(i > 0, _rms(xp_ref[...], g_ref[...]), 0.0).astype(BF16)

        xe = xe_ref[...]
        ea_ref[...] = jnp.dot(xe, wa_ref[...], preferred_element_type=F32)
        eb_ref[...] = jnp.dot(xe, wb_ref[...], preferred_element_type=F32)
        hpa_ref[...] = ea_ref[pl.ds(hal, ts), :].astype(BF16)
        hpb_ref[...] = eb_ref[pl.ds(hal, ts), :].astype(BF16)
        for r0 in range(0, ts, rc):
            ha = _conv_taps(ea_ref, cwa_ref, rc, first + r0, FFN_K) + ba_ref[...]
            hb = _conv_taps(eb_ref, cwb_ref, rc, first + r0, FFN_K) + bb_ref[...]
            act_ref[pl.ds(r0, rc), :] = (_silu(ha) * hb).astype(BF16)
        y = jnp.dot(act_ref[...], wd_ref[...], preferred_element_type=F32)

        @pl.when(j == 0)
        def _():
            xo_ref[...] = x_ref[...] + y

        @pl.when(j > 0)
        def _():
            xo_ref[...] += y

    r = ts // hal
    row = pl.BlockSpec((ts, d), lambda i, j: (i, 0))
    prev = pl.BlockSpec((hal, d), lambda i, j: (jnp.maximum(i * r - 1, 0), 0))
    gsp = pl.BlockSpec((1, d), lambda i, j: (0, 0))
    wup = pl.BlockSpec((d, tf), lambda i, j: (0, j))
    cwsp = pl.BlockSpec((8, tf), lambda i, j: (0, j))
    bsp = pl.BlockSpec((1, tf), lambda i, j: (0, j))
    wdn = pl.BlockSpec((tf, d), lambda i, j: (j, 0))
    hid = pl.BlockSpec((ts, tf), lambda i, j: (i, j))
    return pl.pallas_call(
        body, name=name, grid=(s // ts, nj),
        in_specs=[row, prev, gsp, wup, wup, cwsp, cwsp, bsp, bsp, wdn],
        out_specs=(row, row, hid, hid, hid),
        out_shape=(jax.ShapeDtypeStruct((s, d), F32), jax.ShapeDtypeStruct((s, d), BF16),
                   jax.ShapeDtypeStruct((s, f), BF16), jax.ShapeDtypeStruct((s, f), BF16),
                   jax.ShapeDtypeStruct((s, f), BF16)),
        scratch_shapes=[pltpu.VMEM((ts + hal, d), BF16), pltpu.VMEM((ts + hal, tf), F32),
                        pltpu.VMEM((ts + hal, tf), F32)],
        compiler_params=_cparams("parallel", "arbitrary"),
    )(x, x, g, wa, wb, cwa, cwb, ba, bb, wd)


def ffn_bwd(dy, hpa, hpb, wa, wb, cwa, cwb, cwra, cwrb, ba, bb, wd, name):
    s, d = dy.shape
    f = hpa.shape[1]
    ts, tf = min(FFN_ROW_TILE, s), FFN_COL_TILE
    hal = FFN_HALO
    nt, nj = s // ts, f // tf
    te = ts + hal
    first = hal - (FFN_K - 1)
    rc = min(FFN_ROW_CHUNK, ts)

    def body(dy_ref, dyn_ref, a_ref, ap_ref, an_ref, b_ref, bp_ref, bn_ref, wa_ref, wb_ref, cwa_ref, cwb_ref,
             cwra_ref, cwrb_ref, ba_ref, bb_ref, wd_ref,
             dxn_ref, dpa_ref, dpb_ref, dwa_ref, dwb_ref, dba_ref, dbb_ref,
             dye_ref, ea_ref, eb_ref, dact_ref, da_ref, db_ref):
        i, j = pl.program_id(0), pl.program_id(1)
        last = i == nt - 1

        @pl.when(j == 0)
        def _():
            dye_ref[pl.ds(0, ts), :] = dy_ref[...].astype(BF16)
            dye_ref[pl.ds(ts, hal), :] = jnp.where(last, 0.0, dyn_ref[...]).astype(BF16)

        @pl.when((i == 0) & (j == 0))
        def _():
            for r in (dwa_ref, dwb_ref, dba_ref, dbb_ref):
                r[...] = jnp.zeros_like(r)

        dact_ref[...] = lax.dot_general(dye_ref[...], wd_ref[...], _NT, preferred_element_type=F32)
        for cur, prev, nxt, ext in ((a_ref, ap_ref, an_ref, ea_ref), (b_ref, bp_ref, bn_ref, eb_ref)):
            ext[pl.ds(0, hal), :] = jnp.where(i > 0, prev[...].astype(F32), 0.0)
            ext[pl.ds(hal, ts), :] = cur[...].astype(F32)
            ext[pl.ds(hal + ts, hal), :] = jnp.where(last, 0.0, nxt[...].astype(F32))
        zero = jnp.zeros((1, tf), F32)
        sums = {"ba": zero, "bb": zero, **{("a", k): zero for k in range(FFN_K)}, **{("b", k): zero for k in range(FFN_K)}}
        for r0 in list(range(0, ts, rc)) + [ts]:
            n = rc if r0 < ts else hal
            win_a = [ea_ref[pl.ds(first + r0 + k, n), :] for k in range(FFN_K)]
            win_b = [eb_ref[pl.ds(first + r0 + k, n), :] for k in range(FFN_K)]
            ha = sum(cwa_ref[pl.ds(k, 1), :] * win_a[k] for k in range(FFN_K)) + ba_ref[...]
            hb = sum(cwb_ref[pl.ds(k, 1), :] * win_b[k] for k in range(FFN_K)) + bb_ref[...]
            sig = _sigmoid(ha)
            gs = dact_ref[pl.ds(r0, n), :] * sig
            dha = gs * hb * (1.0 + ha * (1.0 - sig))
            dhb = gs * ha
            da_ref[pl.ds(r0, n), :] = dha
            db_ref[pl.ds(r0, n), :] = dhb
            if r0 < ts:
                sums["ba"] = sums["ba"] + jnp.sum(dha, axis=0, keepdims=True)
                sums["bb"] = sums["bb"] + jnp.sum(dhb, axis=0, keepdims=True)
                for k in range(FFN_K):
                    sums["a", k] = sums["a", k] + jnp.sum(dha * win_a[k], axis=0, keepdims=True)
                    sums["b", k] = sums["b", k] + jnp.sum(dhb * win_b[k], axis=0, keepdims=True)
        for r0 in range(0, ts, rc):
            dpa_ref[pl.ds(r0, rc), :] = _conv_taps(da_ref, cwra_ref, rc, r0, FFN_K).astype(BF16)
            dpb_ref[pl.ds(r0, rc), :] = _conv_taps(db_ref, cwrb_ref, rc, r0, FFN_K).astype(BF16)
        dba_ref[j] += jnp.broadcast_to(sums["ba"], (8, tf))
        dbb_ref[j] += jnp.broadcast_to(sums["bb"], (8, tf))
        row = lax.broadcasted_iota(jnp.int32, (8, tf), 0)
        dwa_ref[j] += sum(jnp.where(row == k, sums["a", k], 0.0) for k in range(FFN_K))
        dwb_ref[j] += sum(jnp.where(row == k, sums["b", k], 0.0) for k in range(FFN_K))
        dxn = (lax.dot_general(dpa_ref[...], wa_ref[...], _NT, preferred_element_type=F32)
               + lax.dot_general(dpb_ref[...], wb_ref[...], _NT, preferred_element_type=F32))

        @pl.when(j == 0)
        def _():
            dxn_ref[...] = dxn

        @pl.when(j > 0)
        def _():
            dxn_ref[...] += dxn

    r = ts // hal
    lastblk = s // hal - 1
    row = pl.BlockSpec((ts, d), lambda i, j: (i, 0))
    rown = pl.BlockSpec((hal, d), lambda i, j: (jnp.minimum((i + 1) * r, lastblk), 0))
    cur = pl.BlockSpec((ts, tf), lambda i, j: (i, j))
    prev = pl.BlockSpec((hal, tf), lambda i, j: (jnp.maximum(i * r - 1, 0), j))
    nxt = pl.BlockSpec((hal, tf), lambda i, j: (jnp.minimum((i + 1) * r, lastblk), j))
    wup = pl.BlockSpec((d, tf), lambda i, j: (0, j))
    cwsp = pl.BlockSpec((8, tf), lambda i, j: (0, j))
    bsp = pl.BlockSpec((1, tf), lambda i, j: (0, j))
    wdn = pl.BlockSpec((tf, d), lambda i, j: (j, 0))
    accsp = pl.BlockSpec((nj, 8, tf), lambda i, j: (0, 0, 0))
    accshape = jax.ShapeDtypeStruct((nj, 8, tf), F32)
    return pl.pallas_call(
        body, name=name, grid=(nt, nj),
        in_specs=[row, rown, cur, prev, nxt, cur, prev, nxt, wup, wup, cwsp, cwsp, cwsp, cwsp, bsp, bsp, wdn],
        out_specs=(row, cur, cur, accsp, accsp, accsp, accsp),
        out_shape=(jax.ShapeDtypeStruct((s, d), F32), jax.ShapeDtypeStruct((s, f), BF16),
                   jax.ShapeDtypeStruct((s, f), BF16), accshape, accshape, accshape, accshape),
        scratch_shapes=[pltpu.VMEM((te, d), BF16), pltpu.VMEM((ts + 2 * hal, tf), F32),
                        pltpu.VMEM((ts + 2 * hal, tf), F32), pltpu.VMEM((te, tf), F32),
                        pltpu.VMEM((te, tf), F32), pltpu.VMEM((te, tf), F32)],
        compiler_params=_cparams("arbitrary", "arbitrary"),
    )(dy, dy, hpa, hpa, hpa, hpb, hpb, hpb, wa, wb, cwa, cwb, cwra, cwrb, ba, bb, wd)


NQ = 4
SQ = SSM_STATE * 8
NS = SSM_GROUPS * SSM_STATE


def _s5_disc(log_dt, a_re, a_im, b_re, b_im, expand):
    dt = jnp.exp(log_dt)
    mag = jnp.exp(a_re * dt)
    lb_re, lb_im = mag * jnp.cos(a_im * dt), mag * jnp.sin(a_im * dt)
    den = a_re * a_re + a_im * a_im
    nr, ni = lb_re - 1.0, lb_im
    f_re = (nr * a_re + ni * a_im) / den
    f_im = (ni * a_re - nr * a_im) / den
    fe_re = jnp.dot(f_re, expand, precision=lax.Precision.HIGHEST, preferred_element_type=F32)
    fe_im = jnp.dot(f_im, expand, precision=lax.Precision.HIGHEST, preferred_element_type=F32)
    return lb_re, lb_im, fe_re * b_re - fe_im * b_im, fe_re * b_im + fe_im * b_re


def _expand_matrix():
    e = np.zeros((SSM_STATE, SSM_STATE * SSM_GROUP), np.float32)
    for p in range(SSM_STATE):
        e[p, p * SSM_GROUP:(p + 1) * SSM_GROUP] = 1.0
    return jnp.asarray(e)


def s5_params_fwd(log_dt, a_re, a_im, b_re, b_im):
    expand = _expand_matrix()

    def body(ld_ref, ar_ref, ai_ref, br_ref, bi_ref, e_ref, lr_ref, li_ref, bbr_ref, bbi_ref):
        lr, li, bbr, bbi = _s5_disc(ld_ref[...], ar_ref[...], ai_ref[...], br_ref[...], bi_ref[...], e_ref[...])
        lr_ref[...] = lr
        li_ref[...] = li
        bbr_ref[...] = bbr
        bbi_ref[...] = bbi

    g, p, pc = SSM_GROUPS, SSM_STATE, SSM_STATE * SSM_GROUP
    return pl.pallas_call(
        body, name="s5_params_fwd",
        out_shape=(jax.ShapeDtypeStruct((g, p), F32), jax.ShapeDtypeStruct((g, p), F32),
                   jax.ShapeDtypeStruct((g, pc), F32), jax.ShapeDtypeStruct((g, pc), F32)),
    )(log_dt, a_re, a_im, b_re, b_im, expand)


def s5_params_bwd(log_dt, a_re, a_im, b_re, b_im, dlr, dli, dbbr, dbbi):
    expand = _expand_matrix()

    def body(ld_ref, ar_ref, ai_ref, br_ref, bi_ref, e_ref, dlr_ref, dli_ref, dbbr_ref, dbbi_ref,
             dld_ref, dar_ref, dai_ref, dbr_ref, dbi_ref):
        e = e_ref[...]
        f = lambda ld, ar, ai, br, bi: _s5_disc(ld, ar, ai, br, bi, e)
        _, vjp = jax.vjp(f, ld_ref[...], ar_ref[...], ai_ref[...], br_ref[...], bi_ref[...])
        dld, dar, dai, dbr, dbi = vjp((dlr_ref[...], dli_ref[...], dbbr_ref[...], dbbi_ref[...]))
        dld_ref[...] = dld
        dar_ref[...] = dar
        dai_ref[...] = dai
        dbr_ref[...] = dbr
        dbi_ref[...] = dbi

    g, p, pc = SSM_GROUPS, SSM_STATE, SSM_STATE * SSM_GROUP
    return pl.pallas_call(
        body, name="s5_params_bwd",
        out_shape=(jax.ShapeDtypeStruct((g, 1), F32), jax.ShapeDtypeStruct((g, p), F32),
                   jax.ShapeDtypeStruct((g, p), F32), jax.ShapeDtypeStruct((g, pc), F32),
                   jax.ShapeDtypeStruct((g, pc), F32)),
    )(log_dt, a_re, a_im, b_re, b_im, expand, dlr, dli, dbbr, dbbi)


def _cmul(ar, ai, br, bi):
    return ar * br - ai * bi, ar * bi + ai * br


def _power_rows(lr, li, conj_rev):
    row = lax.broadcasted_iota(jnp.int32, (8, NS), 0)
    tr = jnp.zeros((8, NS), F32)
    ti = jnp.zeros((8, NS), F32)
    pr, pi = lr, li
    for r in range(8):
        dst = 7 - r if conj_rev else r
        tr = jnp.where(row == dst, pr, tr)
        ti = jnp.where(row == dst, -pi if conj_rev else pi, ti)
        if r < 7:
            pr, pi = _cmul(pr, pi, lr, li)
    return tr, ti


def _scan8(xr, xi, tr_ref, ti_ref, cr, ci, reverse):
    row = lax.broadcasted_iota(jnp.int32, xr.shape, 0)
    for d in (1, 2, 4):
        if reverse:
            sr, si = pltpu.roll(xr, 8 - d, 0), pltpu.roll(xi, 8 - d, 0)
            keep = row < 8 - d
            pw = 8 - d
        else:
            sr, si = pltpu.roll(xr, d, 0), pltpu.roll(xi, d, 0)
            keep = row >= d
            pw = d - 1
        mr, mi = _cmul(tr_ref[pl.ds(pw, 1), :], ti_ref[pl.ds(pw, 1), :], sr, si)
        xr = xr + jnp.where(keep, mr, 0.0)
        xi = xi + jnp.where(keep, mi, 0.0)
    mr, mi = _cmul(tr_ref[...], ti_ref[...], cr, ci)
    return xr + mr, xi + mi


def _row_of(x, r):
    row = lax.broadcasted_iota(jnp.int32, x.shape, 0)
    return jnp.sum(jnp.where(row == r, x, 0.0), axis=0, keepdims=True)


def s5_scan_fwd(u, lam_r, lam_i, bre, bim, cre, cim, dskip):
    s = u.shape[0]
    tt = min(SCAN_TILE, s)
    nb = tt // 8

    def body(nt, u_ref, lr_ref, li_ref, bre_ref, bim_ref, cre_ref, cim_ref, d_ref,
             xr_ref, xi_ref, y_ref, yg_ref, tr_ref, ti_ref, cr_ref, ci_ref):
        i = pl.program_id(0)

        @pl.when(i == 0)
        def _():
            tr, ti = _power_rows(lr_ref[...], li_ref[...], False)
            tr_ref[...] = tr
            ti_ref[...] = ti
            cr_ref[...] = jnp.zeros_like(cr_ref)
            ci_ref[...] = jnp.zeros_like(ci_ref)

        uv = u_ref[...]
        ub = uv.astype(BF16)
        for q in range(NQ):
            uq = ub[:, q * LANES:(q + 1) * LANES]
            xr_ref[:, q * SQ:(q + 1) * SQ] = jnp.dot(uq, bre_ref[q], preferred_element_type=F32)
            xi_ref[:, q * SQ:(q + 1) * SQ] = jnp.dot(uq, bim_ref[q], preferred_element_type=F32)

        def step(b, carry):
            cr, ci = carry
            rows = pl.ds(pl.multiple_of(b * 8, 8), 8)
            xr, xi = _scan8(xr_ref[rows, :], xi_ref[rows, :], tr_ref, ti_ref, cr, ci, False)
            xr_ref[rows, :] = xr
            xi_ref[rows, :] = xi
            return _row_of(xr, 7), _row_of(xi, 7)

        cr, ci = lax.fori_loop(0, nb, step, (cr_ref[...], ci_ref[...]))
        cr_ref[...] = cr
        ci_ref[...] = ci
        y = d_ref[...] * uv
        for q in range(NQ):
            yq = (jnp.dot(xr_ref[:, q * SQ:(q + 1) * SQ].astype(BF16), cre_ref[q], preferred_element_type=F32)
                  - jnp.dot(xi_ref[:, q * SQ:(q + 1) * SQ].astype(BF16), cim_ref[q], preferred_element_type=F32))
            y_ref[:, q * LANES:(q + 1) * LANES] = yq + y[:, q * LANES:(q + 1) * LANES]
        yg_ref[...] = _gelu(y_ref[...]).astype(BF16)

    return rowcall(body, rows=s, ts=tt,
                   ins=[(u, "row"), (lam_r, "full"), (lam_i, "full"), (bre, "full"), (bim, "full"),
                        (cre, "full"), (cim, "full"), (dskip, "full")],
                   outs=[((s, NS), F32, "row"), ((s, NS), F32, "row"), ((s, SSM_WIDTH), F32, "row"),
                         ((s, SSM_WIDTH), BF16, "row")], name="s5_scan_fwd",
                   scratch=[pltpu.VMEM((8, NS), F32), pltpu.VMEM((8, NS), F32),
                            pltpu.VMEM((1, NS), F32), pltpu.VMEM((1, NS), F32)])


def s5_scan_bwd(dyg, y, lam_r, lam_i, cre, cim):
    s = y.shape[0]
    tt = min(SCAN_TILE, s)
    nb = tt // 8

    def body(nt, dyg_ref, y_ref, lr_ref, li_ref, cre_ref, cim_ref,
             ar_ref, ai_ref, dy_ref, tr_ref, ti_ref, cr_ref, ci_ref):
        i = pl.program_id(0)

        @pl.when(i == 0)
        def _():
            tr, ti = _power_rows(lr_ref[...], li_ref[...], True)
            tr_ref[...] = tr
            ti_ref[...] = ti
            cr_ref[...] = jnp.zeros_like(cr_ref)
            ci_ref[...] = jnp.zeros_like(ci_ref)

        _, vjp = jax.vjp(_gelu, y_ref[...])
        dy = vjp(dyg_ref[...])[0]
        dyb = dy.astype(BF16)
        dy_ref[...] = dyb
        for q in range(NQ):
            dq = dyb[:, q * LANES:(q + 1) * LANES]
            ar_ref[:, q * SQ:(q + 1) * SQ] = lax.dot_general(dq, cre_ref[q], _NT, preferred_element_type=F32)
            ai_ref[:, q * SQ:(q + 1) * SQ] = -lax.dot_general(dq, cim_ref[q], _NT, preferred_element_type=F32)

        def step(b, carry):
            cr, ci = carry
            rows = pl.ds(pl.multiple_of((nb - 1 - b) * 8, 8), 8)
            xr, xi = _scan8(ar_ref[rows, :], ai_ref[rows, :], tr_ref, ti_ref, cr, ci, True)
            ar_ref[rows, :] = xr
            ai_ref[rows, :] = xi
            return _row_of(xr, 0), _row_of(xi, 0)

        cr, ci = lax.fori_loop(0, nb, step, (cr_ref[...], ci_ref[...]))
        cr_ref[...] = cr
        ci_ref[...] = ci

    return rowcall(body, rows=s, ts=tt,
                   ins=[(dyg, "rev"), (y, "rev"), (lam_r, "full"), (lam_i, "full"), (cre, "full"), (cim, "full")],
                   outs=[((s, NS), F32, "rev"), ((s, NS), F32, "rev"), ((s, SSM_WIDTH), BF16, "rev")],
                   name="s5_scan_bwd",
                   scratch=[pltpu.VMEM((8, NS), F32), pltpu.VMEM((8, NS), F32),
                            pltpu.VMEM((1, NS), F32), pltpu.VMEM((1, NS), F32)])


def s5_grads(u, dy, xr, xi, ar, ai, bre, bim, dskip):
    s = u.shape[0]
    tt = min(SCAN_TILE, s)

    def body(nt, u_ref, dy_ref, xr_ref, xrp_ref, xi_ref, xip_ref, ar_ref, ai_ref, bre_ref, bim_ref, d_ref,
             du_ref, dlr_ref, dli_ref, dbr_ref, dbi_ref, dcr_ref, dci_ref, dd_ref, er_ref, ei_ref):
        i = pl.program_id(0)

        @pl.when(i == 0)
        def _():
            for r in (dbr_ref, dbi_ref, dcr_ref, dci_ref):
                r[...] = jnp.zeros_like(r)

        uv, dyb = u_ref[...], dy_ref[...]
        dyf = dyb.astype(F32)
        av_r, av_i, xv_r, xv_i = ar_ref[...], ai_ref[...], xr_ref[...], xi_ref[...]
        er_ref[pl.ds(0, 8), :] = jnp.where(i > 0, xrp_ref[...], 0.0)
        ei_ref[pl.ds(0, 8), :] = jnp.where(i > 0, xip_ref[...], 0.0)
        er_ref[pl.ds(8, tt), :] = xv_r
        ei_ref[pl.ds(8, tt), :] = xv_i
        sr, si = er_ref[pl.ds(7, tt), :], ei_ref[pl.ds(7, tt), :]
        _acc(dlr_ref, i, av_r * sr + av_i * si)
        _acc(dli_ref, i, av_i * sr - av_r * si)
        _acc(dd_ref, i, dyf * uv)
        ub = uv.astype(BF16)
        ab_r, ab_i = av_r.astype(BF16), av_i.astype(BF16)
        xb_r, xb_i = xv_r.astype(BF16), xv_i.astype(BF16)
        du = d_ref[...] * dyf
        for q in range(NQ):
            cs, ss = slice(q * LANES, (q + 1) * LANES), slice(q * SQ, (q + 1) * SQ)
            dbr_ref[q] += lax.dot_general(ub[:, cs], ab_r[:, ss], _TN, preferred_element_type=F32)
            dbi_ref[q] += lax.dot_general(ub[:, cs], ab_i[:, ss], _TN, preferred_element_type=F32)
            dcr_ref[q] += lax.dot_general(xb_r[:, ss], dyb[:, cs], _TN, preferred_element_type=F32)
            dci_ref[q] -= lax.dot_general(xb_i[:, ss], dyb[:, cs], _TN, preferred_element_type=F32)
            du_ref[:, cs] = (du[:, cs]
                             + lax.dot_general(ab_r[:, ss], bre_ref[q], _NT, preferred_element_type=F32)
                             + lax.dot_general(ab_i[:, ss], bim_ref[q], _NT, preferred_element_type=F32))

    return rowcall(body, rows=s, ts=tt,
                   ins=[(u, "row"), (dy, "row"), (xr, "row"), (xr, "prev:8"), (xi, "row"), (xi, "prev:8"),
                        (ar, "row"), (ai, "row"), (bre, "full"), (bim, "full"), (dskip, "full")],
                   outs=[((s, SSM_WIDTH), F32, "row"), ((8, NS), F32, "acc"), ((8, NS), F32, "acc"),
                         ((NQ, LANES, SQ), F32, "acc"), ((NQ, LANES, SQ), F32, "acc"),
                         ((NQ, SQ, LANES), F32, "acc"), ((NQ, SQ, LANES), F32, "acc"),
                         ((8, SSM_WIDTH), F32, "acc")], name="s5_grads",
                   scratch=[pltpu.VMEM((tt + 8, NS), F32), pltpu.VMEM((tt + 8, NS), F32)])


def _glu_fn(za, zb):
    return za * _sigmoid(zb)


def glu_res_fwd(z, xres):
    s = z.shape[0]

    def body(nt, z_ref, x_ref, o_ref):
        o_ref[...] = x_ref[...] + _glu_fn(z_ref[:, :D_MODEL].astype(F32), z_ref[:, D_MODEL:].astype(F32))

    return rowcall(body, rows=s, ts=min(ROW_TILE, s), ins=[(z, "row"), (xres, "row")],
                   outs=[((s, D_MODEL), F32, "row")], name="glu_res_fwd")[0]


def glu_bwd(z, dout):
    s, c = z.shape

    def body(nt, z_ref, d_ref, dz_ref, dba_ref, dbb_ref):
        i = pl.program_id(0)
        _, vjp = jax.vjp(_glu_fn, z_ref[:, :D_MODEL].astype(F32), z_ref[:, D_MODEL:].astype(F32))
        dza, dzb = vjp(d_ref[...])
        dz_ref[:, :D_MODEL] = dza.astype(BF16)
        dz_ref[:, D_MODEL:] = dzb.astype(BF16)
        _acc(dba_ref, i, dza)
        _acc(dbb_ref, i, dzb)

    return rowcall(body, rows=s, ts=min(ROW_TILE, s), ins=[(z, "row"), (dout, "row")],
                   outs=[((s, c), BF16, "row"), ((8, D_MODEL), F32, "acc"), ((8, D_MODEL), F32, "acc")],
                   name="glu_bwd")


def loss_head(x, g, target):
    s, c = x.shape

    def body(nt, x_ref, g_ref, t_ref, loss_ref, dx_ref, dg_ref):
        i = pl.program_id(0)
        y, vjp = jax.vjp(_rms, x_ref[...], g_ref[...])
        err = y - t_ref[...]
        dx, dg = vjp(err * (1.0 / c))
        dx_ref[...] = dx
        _acc(dg_ref, i, dg)
        part = jnp.sum(jnp.sum(err * err, axis=-1, keepdims=True), axis=0, keepdims=True) * (0.5 / c)

        @pl.when(i == 0)
        def _():
            loss_ref[...] = jnp.zeros_like(loss_ref)

        loss_ref[...] += jnp.broadcast_to(part, loss_ref.shape)

    return rowcall(body, rows=s, ts=min(ROW_TILE, s), ins=[(x, "row"), (g, "full"), (target, "row")],
                   outs=[((8, LANES), F32, "acc"), ((s, c), F32, "row"), ((8, c), F32, "acc")], name="loss_head")


def _tile_rows(r, cands=(512, 256, 128, 64, 32, 16, 8)):
    return _pick(r, cands)


def add_to_bf16(a, b, name):
    n, r, c = a.shape
    tr = _tile_rows(r)

    def body(a_ref, b_ref, o_ref):
        o_ref[...] = (a_ref[...].astype(F32) + b_ref[...].astype(F32)).astype(BF16)

    spec = pl.BlockSpec((1, tr, c), lambda j, i: (j, i, 0))
    return pl.pallas_call(body, name=name, grid=(n, r // tr), in_specs=[spec, spec], out_specs=spec,
                          out_shape=jax.ShapeDtypeStruct((n, r, c), BF16),
                          compiler_params=_cparams("parallel", "parallel"))(a, b)


def sum_leading(a, name):
    n, r, c = a.shape
    tr = _tile_rows(r)

    def body(a_ref, o_ref):
        acc = a_ref[0].astype(F32)
        for k in range(1, n):
            acc = acc + a_ref[k].astype(F32)
        o_ref[...] = acc

    return pl.pallas_call(body, name=name, grid=(r // tr,),
                          in_specs=[pl.BlockSpec((n, tr, c), lambda i: (0, i, 0))],
                          out_specs=pl.BlockSpec((tr, c), lambda i: (i, 0)),
                          out_shape=jax.ShapeDtypeStruct((r, c), F32),
                          compiler_params=_cparams("parallel"))(a)


def adamw(w, g, m, v, name):
    r, c = w.shape
    tr = _tile_rows(r, (256, 128, 64, 32, 16, 8))
    c1 = 1.0 - ADAM_B1 ** ADAM_STEP
    c2 = 1.0 - ADAM_B2 ** ADAM_STEP

    def body(w_ref, g_ref, m_ref, v_ref, d_ref, nm_ref, nv_ref):
        gv = g_ref[...]
        mn = ADAM_B1 * m_ref[...] + (1.0 - ADAM_B1) * gv
        vn = ADAM_B2 * v_ref[...] + (1.0 - ADAM_B2) * (gv * gv)
        d_ref[...] = -ADAM_LR * ((mn / c1) / (jnp.sqrt(vn / c2) + ADAM_EPS) + ADAM_WD * w_ref[...])
        nm_ref[...] = mn
        nv_ref[...] = vn

    spec = pl.BlockSpec((tr, c), lambda i: (i, 0))
    shp = jax.ShapeDtypeStruct((r, c), F32)
    return pl.pallas_call(body, name=name, grid=(r // tr,), in_specs=[spec] * 4, out_specs=(spec,) * 3,
                          out_shape=(shp,) * 3, compiler_params=_cparams("parallel"))(w, g, m, v)


_ANY = pl.BlockSpec(memory_space=pl.ANY)


def all_gather8(block, name):
    r, c = block.shape

    def body(x_ref, out_ref, send_sems, recv_sems, local_sem):
        x, y, cc = lax.axis_index("x"), lax.axis_index("y"), lax.axis_index("c")
        me, sibling = (x, y, cc), (x, y, 1 - cc)
        chips = [(1 - x, y), (x, 1 - y), (1 - x, 1 - y)]

        def slot(px, py, pc):
            return out_ref.at[4 * px + 2 * py + pc]

        def copy(k, blk, to, src=None):
            return pltpu.make_async_remote_copy(
                src_ref=slot(*blk) if src is None else src, dst_ref=slot(*blk),
                send_sem=send_sems.at[k], recv_sem=recv_sems.at[k], device_id=to, device_id_type=MESH)

        mine = pltpu.make_async_copy(x_ref, slot(*me), local_sem)
        mine.start()
        first = [copy(0, me, sibling, src=x_ref)]
        first += [copy(1 + j, me, (*chip, cc), src=x_ref) for j, chip in enumerate(chips)]
        for cp in first:
            cp.start()
        passed = [copy(4 + j, (*chip, cc), sibling) for j, chip in enumerate(chips)]
        for j, chip in enumerate(chips):
            copy(1 + j, (*chip, cc), me).wait_recv()
            passed[j].start()
        copy(0, sibling, me).wait_recv()
        for j, chip in enumerate(chips):
            copy(4 + j, (*chip, 1 - cc), me).wait_recv()
        for cp in first + passed:
            cp.wait_send()
        mine.wait()

    return pl.pallas_call(
        body, name=name, in_specs=[_ANY], out_specs=_ANY,
        out_shape=jax.ShapeDtypeStruct((8, r, c), block.dtype),
        scratch_shapes=[pltpu.SemaphoreType.DMA((7,)), pltpu.SemaphoreType.DMA((7,)), pltpu.SemaphoreType.DMA],
    )(block)


def sibling_swap(block, name):
    def body(x_ref, out_ref, send_sem, recv_sem):
        x, y, cc = lax.axis_index("x"), lax.axis_index("y"), lax.axis_index("c")
        cp = pltpu.make_async_remote_copy(src_ref=x_ref, dst_ref=out_ref, send_sem=send_sem, recv_sem=recv_sem,
                                          device_id=(x, y, 1 - cc), device_id_type=MESH)
        cp.start()
        cp.wait()

    return pl.pallas_call(
        body, name=name, in_specs=[_ANY], out_specs=_ANY,
        out_shape=jax.ShapeDtypeStruct(block.shape, block.dtype),
        scratch_shapes=[pltpu.SemaphoreType.DMA, pltpu.SemaphoreType.DMA],
    )(block)


def chip_exchange(parts, name):
    def body(p_ref, out_ref, send_sems, recv_sems, local_sem):
        x, y, cc = lax.axis_index("x"), lax.axis_index("y"), lax.axis_index("c")
        me = 2 * x + y
        chips = [(1 - x, y), (x, 1 - y), (1 - x, 1 - y)]
        mine = pltpu.make_async_copy(p_ref.at[me], out_ref.at[me], local_sem)
        mine.start()
        sends = []
        for k, (px, py) in enumerate(chips):
            sends.append(pltpu.make_async_remote_copy(
                src_ref=p_ref.at[2 * px + py], dst_ref=out_ref.at[me],
                send_sem=send_sems.at[k], recv_sem=recv_sems.at[k], device_id=(px, py, cc), device_id_type=MESH))
        for cp in sends:
            cp.start()
        for k, (px, py) in enumerate(chips):
            pltpu.make_async_remote_copy(
                src_ref=p_ref.at[me], dst_ref=out_ref.at[2 * px + py],
                send_sem=send_sems.at[k], recv_sem=recv_sems.at[k], device_id=(px, py, cc),
                device_id_type=MESH).wait_recv()
        for cp in sends:
            cp.wait_send()
        mine.wait()

    return pl.pallas_call(
        body, name=name, in_specs=[_ANY], out_specs=_ANY,
        out_shape=jax.ShapeDtypeStruct(parts.shape, parts.dtype),
        scratch_shapes=[pltpu.SemaphoreType.DMA((3,)), pltpu.SemaphoreType.DMA((3,)), pltpu.SemaphoreType.DMA],
    )(parts)


PACK_COLS = 1024
SHARDED = (("l0_w_in", 1), ("l0_w_uq", 1), ("l0_w_ukv", 1), ("l0_w_out", 0), ("l0_w_up", 1), ("l0_w_down", 0),
           ("l1_w_in", 0), ("l1_w_glu", 1), ("l1_w_up", 1), ("l1_w_down", 0),
           ("l0_conv_w", 1), ("l0_ffn_conv_w", 1), ("l1_ffn_conv_w", 1))
REPLICATED = ("l0_mix_norm", "l0_conv_b", "l0_conv_ln_g", "l0_conv_ln_b", "l0_q_norm", "l0_kv_norm", "l0_ffn_norm",
              "l0_ffn_conv_b", "l1_mix_norm", "l1_log_dt", "l1_a_re", "l1_a_im", "l1_b_re", "l1_b_im", "l1_c_re",
              "l1_c_im", "l1_d", "l1_b_glu", "l1_ffn_norm", "l1_ffn_conv_b", "final_norm")


def _pack(arrs, dtype, mult):
    flat = jnp.concatenate([a.reshape(-1).astype(dtype) for a in arrs])
    n = flat.shape[0]
    total = -(-n // mult) * mult
    return jnp.pad(flat, (0, total - n))


def _unpack(flat, shapes):
    out, pos = [], 0
    for shp in shapes:
        n = int(np.prod(shp))
        out.append(flat[pos:pos + n].reshape(shp))
        pos += n
    return out


def _shard(full, axis, j):
    n = full.shape[axis] // N_CHIPS
    return lax.slice_in_dim(full, j * n, (j + 1) * n, axis=axis)


def _block_diag(t):
    q, g, a, b = t.shape
    eye = jnp.eye(g, dtype=t.dtype)
    return jnp.einsum("qgab,gh->qgahb", t, eye).reshape(q, g * a, g * b)


def _block_diag_t(d, a, b):
    q = d.shape[0]
    d5 = d.reshape(q, 8, a, 8, b)
    eye = jnp.eye(8, dtype=d.dtype)
    return jnp.einsum("qgahb,gh->qgab", d5, eye)


def kernel(x, l0_mix_norm, l0_w_in, l0_conv_w, l0_conv_b, l0_conv_ln_g, l0_conv_ln_b, l0_q_norm, l0_kv_norm, l0_w_uq, l0_w_ukv, l0_w_out, l0_ffn_norm, l0_w_up, l0_ffn_conv_w, l0_ffn_conv_b, l0_w_down, l1_mix_norm, l1_w_in, l1_log_dt, l1_a_re, l1_a_im, l1_b_re, l1_b_im, l1_c_re, l1_c_im, l1_d, l1_w_glu, l1_b_glu, l1_ffn_norm, l1_w_up, l1_ffn_conv_w, l1_ffn_conv_b, l1_w_down, final_norm, loss_target, m_l0_mix_norm, m_l0_w_in, m_l0_conv_w, m_l0_conv_b, m_l0_conv_ln_g, m_l0_conv_ln_b, m_l0_q_norm, m_l0_kv_norm, m_l0_w_uq, m_l0_w_ukv, m_l0_w_out, m_l0_ffn_norm, m_l0_w_up, m_l0_ffn_conv_w, m_l0_ffn_conv_b, m_l0_w_down, m_l1_mix_norm, m_l1_w_in, m_l1_log_dt, m_l1_a_re, m_l1_a_im, m_l1_b_re, m_l1_b_im, m_l1_c_re, m_l1_c_im, m_l1_d, m_l1_w_glu, m_l1_b_glu, m_l1_ffn_norm, m_l1_w_up, m_l1_ffn_conv_w, m_l1_ffn_conv_b, m_l1_w_down, m_final_norm, v_l0_mix_norm, v_l0_w_in, v_l0_conv_w, v_l0_conv_b, v_l0_conv_ln_g, v_l0_conv_ln_b, v_l0_q_norm, v_l0_kv_norm, v_l0_w_uq, v_l0_w_ukv, v_l0_w_out, v_l0_ffn_norm, v_l0_w_up, v_l0_ffn_conv_w, v_l0_ffn_conv_b, v_l0_w_down, v_l1_mix_norm, v_l1_w_in, v_l1_log_dt, v_l1_a_re, v_l1_a_im, v_l1_b_re, v_l1_b_im, v_l1_c_re, v_l1_c_im, v_l1_d, v_l1_w_glu, v_l1_b_glu, v_l1_ffn_norm, v_l1_w_up, v_l1_ffn_conv_w, v_l1_ffn_conv_b, v_l1_w_down, v_final_norm):
    a = dict(locals())
    w = {n: a[n] for n in [s for s, _ in SHARDED] + list(REPLICATED)}
    mom = {n: a["m_" + n] for n in w}
    var = {n: a["v_" + n] for n in w}
    return _step(a["x"][0], a["loss_target"][0], w, mom, var)


def _gather_weights(w):
    cc = lax.axis_index("c")
    big = [n for n, _ in SHARDED[:10]]
    small = [n for n, _ in SHARDED[10:]]
    full = {}
    for names, dtype, mult in ((big, BF16, 2 * 256 * PACK_COLS), (small, F32, 2 * 8 * LANES)):
        cols = PACK_COLS if dtype == BF16 else LANES
        flat = _pack([w[n] for n in names], dtype, mult)
        half = lax.dynamic_index_in_dim(flat.reshape(2, -1, cols), cc, axis=0, keepdims=False)
        got = all_gather8(half, "gather_" + ("matrices" if dtype == BF16 else "conv_weights"))
        got = got.reshape(N_CHIPS, -1)
        shapes = [w[n].shape for n in names]
        per_chip = [_unpack(got[j], shapes) for j in range(N_CHIPS)]
        for k, n in enumerate(names):
            axis = dict(SHARDED)[n]
            full[n] = jnp.concatenate([per_chip[j][k] for j in range(N_CHIPS)], axis=axis)
    return full


def _reduce_sharded(grads):
    cc = lax.axis_index("c")
    names = [n for n, _ in SHARDED]
    axes = dict(SHARDED)
    mult = 2 * 256 * PACK_COLS
    packs = [_pack([_shard(grads[n], axes[n], j) for n in names], BF16, mult) for j in range(N_CHIPS)]
    g = jnp.stack(packs).reshape(N_CHIPS, 2, -1, PACK_COLS)
    keep = lax.dynamic_index_in_dim(g, cc, axis=1, keepdims=False)
    give = lax.dynamic_index_in_dim(g, 1 - cc, axis=1, keepdims=False)
    got = sibling_swap(give, "grad_swap_halves")
    parts = add_to_bf16(keep, got, "grad_add_sibling")
    landed = chip_exchange(parts, "grad_chip_exchange")
    mine = sum_leading(landed, "grad_sum_chips")
    theirs = sibling_swap(mine, "grad_swap_sums")
    lo = jnp.where(cc == 0, mine, theirs)
    hi = jnp.where(cc == 0, theirs, mine)
    flat = jnp.concatenate([lo.reshape(-1), hi.reshape(-1)])
    shapes = [_shard(grads[n], axes[n], 0).shape for n in names]
    return dict(zip(names, _unpack(flat, shapes)))


def _reduce_replicated(grads):
    names = list(REPLICATED)
    flat = _pack([grads[n] for n in names], F32, 256 * LANES).reshape(-1, LANES)
    got = all_gather8(flat, "gather_small_grads")
    tot = sum_leading(got, "sum_small_grads").reshape(-1)
    return dict(zip(names, _unpack(tot, [grads[n].shape for n in names]))), flat.shape


def _row(v):
    return v.reshape(1, -1).astype(F32)


def _pad_rows(wt, rows):
    return jnp.pad(wt.astype(F32), ((0, rows - wt.shape[0]), (0, 0)))


def _ffn_fwd(xin, g, wa, wb, cw, cb, wd, tag):
    cwa, cwb = _pad_rows(cw[:, :D_FF], 8), _pad_rows(cw[:, D_FF:], 8)
    xout, xn, hpa, hpb, act = ffn_fwd(xin, _row(g), wa, wb, cwa, cwb, _row(cb[:D_FF]), _row(cb[D_FF:]), wd, tag)
    return xout, (xin, xn, hpa, hpb, act)


def _ffn_bwd(dxout, saved, g, wa, wb, cw, cb, wd, tag):
    xin, xn, hpa, hpb, act = saved
    d_wd = matmul(act, dxout, ta=True, name=f"{tag}_d_wdown")
    cwa, cwb = _pad_rows(cw[:, :D_FF], 8), _pad_rows(cw[:, D_FF:], 8)
    wra, wrb = _pad_rows(cw[::-1, :D_FF], 8), _pad_rows(cw[::-1, D_FF:], 8)
    dxn, dpa, dpb, dwa, dwb, dba, dbb = ffn_bwd(dxout, hpa, hpb, wa, wb, cwa, cwb, wra, wrb,
                                                _row(cb[:D_FF]), _row(cb[D_FF:]), wd, tag + "_bwd")
    d_wu = jnp.concatenate([matmul(xn, dpa, ta=True, name=f"{tag}_d_wup_a"),
                            matmul(xn, dpb, ta=True, name=f"{tag}_d_wup_b")], axis=1)
    dxin, dg = rms_bwd(xin, _row(g), dxn, dxout, f"{tag}_rms_bwd")
    taps = lambda t: t.transpose(1, 0, 2).reshape(8, -1)
    d_cw = jnp.concatenate([taps(dwa)[:FFN_K], taps(dwb)[:FFN_K]], axis=1)
    d_cb = jnp.concatenate([taps(dba)[0], taps(dbb)[0]])
    return dxin, dg[0], d_wu, d_cw, d_cb, d_wd


def _step(x, target, w, mom, var):
    s = x.shape[0]
    full = _gather_weights(w)
    cos, sin = rope_tables(s)

    w_in0 = full["l0_w_in"]
    w_in0p = jnp.concatenate([w_in0, jnp.zeros((D_MODEL, H0_W - w_in0.shape[1]), BF16)], axis=1)
    wq = full["l0_w_uq"].reshape(Q_LORA, N_HEADS, QK_NOPE + QK_ROPE)
    zq = lambda n: jnp.zeros((Q_LORA, N_HEADS, n), BF16)
    w_uqp = jnp.concatenate([wq[..., :QK_NOPE], zq(LANES - QK_NOPE), wq[..., QK_NOPE:], zq(LANES - QK_ROPE)],
                            axis=-1).reshape(Q_LORA, N_HEADS * HEAD_PAD)
    w_ukv = full["l0_w_ukv"]
    w_out = full["l0_w_out"]
    w_out_u = w_out[:CONV_WIDTH]
    wo = w_out[CONV_WIDTH:].reshape(N_HEADS, V_DIM, D_MODEL)
    w_out_a = jnp.concatenate([jnp.zeros_like(wo), wo], axis=1).reshape(N_HEADS * LANES, D_MODEL)
    conv_w = _pad_rows(full["l0_conv_w"], CONV_HALO)
    conv_wrev = _pad_rows(full["l0_conv_w"][::-1], CONV_HALO)
    w_up0a, w_up0b = full["l0_w_up"][:, :D_FF], full["l0_w_up"][:, D_FF:]
    w_up1a, w_up1b = full["l1_w_up"][:, :D_FF], full["l1_w_up"][:, D_FF:]

    xn0 = rms_fwd(x, _row(w["l0_mix_norm"]), "l0_mix_rms")
    h0 = matmul(xn0, w_in0p, name="l0_in_proj")
    qn_g, kvn_g = _row(w["l0_q_norm"]), _row(w["l0_kv_norm"])
    u0, cq, ckv, kr = mixpre_fwd(h0, qn_g, kvn_g, cos, sin)
    cb, lg, lb = _row(w["l0_conv_b"]), _row(w["l0_conv_ln_g"]), _row(w["l0_conv_ln_b"])
    u = convln_fwd(u0, conv_w, cb, lg, lb)
    qraw = matmul(cq, w_uqp, name="l0_q_up")
    q = qrope_fwd(qraw, cos, sin)
    kv = matmul(ckv, w_ukv, out_dtype=BF16, name="l0_kv_up")
    o, lse = attn_fwd(q, kv, kr)
    x1 = matmul(u, w_out_u, res=x, name="l0_out_conv")
    x1 = matmul(o, w_out_a, res=x1, name="l0_out_attn")

    x2, ffn0 = _ffn_fwd(x1, w["l0_ffn_norm"], w_up0a, w_up0b, full["l0_ffn_conv_w"], w["l0_ffn_conv_b"],
                        full["l0_w_down"], "l0_ffn")

    g_, p_, c_ = SSM_GROUPS, SSM_STATE, SSM_GROUP
    s5_in = (w["l1_log_dt"].reshape(g_, 1), w["l1_a_re"], w["l1_a_im"],
             w["l1_b_re"].reshape(g_, p_ * c_), w["l1_b_im"].reshape(g_, p_ * c_))
    lam_r, lam_i, bb_r, bb_i = s5_params_fwd(*s5_in)
    lam_rf, lam_if = lam_r.reshape(1, NS), lam_i.reshape(1, NS)

    def b_blocks(bb):
        t = bb.reshape(NQ, 8, p_, c_).transpose(0, 1, 3, 2)
        return _block_diag(t).astype(BF16)

    def c_blocks(cm):
        t = cm.reshape(NQ, 8, c_, p_).transpose(0, 1, 3, 2)
        return _block_diag(t).astype(BF16)

    bre, bim = b_blocks(bb_r), b_blocks(bb_i)
    cre, cim = c_blocks(w["l1_c_re"]), c_blocks(w["l1_c_im"])
    dskip = _row(w["l1_d"])
    xn2 = rms_fwd(x2, _row(w["l1_mix_norm"]), "l1_mix_rms")
    u1 = matmul(xn2, full["l1_w_in"], name="l1_in_proj")
    xs_r, xs_i, y1, yg = s5_scan_fwd(u1, lam_rf, lam_if, bre, bim, cre, cim, dskip)
    z = matmul(yg, full["l1_w_glu"], bias=_row(w["l1_b_glu"]), out_dtype=BF16, name="l1_glu_proj")
    x3 = glu_res_fwd(z, x2)

    x4, ffn1 = _ffn_fwd(x3, w["l1_ffn_norm"], w_up1a, w_up1b, full["l1_ffn_conv_w"], w["l1_ffn_conv_b"],
                        full["l1_w_down"], "l1_ffn")
    loss_part, dx4, dgf = loss_head(x4, _row(w["final_norm"]), target)
    loss = lax.psum(loss_part[0, 0], ("x", "y", "c"))

    gr = {"final_norm": dgf[0]}

    dx3, gr["l1_ffn_norm"], gr["l1_w_up"], gr["l1_ffn_conv_w"], gr["l1_ffn_conv_b"], gr["l1_w_down"] = _ffn_bwd(
        dx4, ffn1, w["l1_ffn_norm"], w_up1a, w_up1b, full["l1_ffn_conv_w"], w["l1_ffn_conv_b"], full["l1_w_down"],
        "l1_ffn")

    dz, dbga, dbgb = glu_bwd(z, dx3)
    gr["l1_b_glu"] = jnp.concatenate([dbga[0], dbgb[0]])
    dyg = matmul(dz, full["l1_w_glu"], tb=True, name="l1_d_yg")
    gr["l1_w_glu"] = matmul(yg, dz, ta=True, name="l1_d_wglu")
    a_r, a_i, dy1 = s5_scan_bwd(dyg, y1, lam_rf, lam_if, cre, cim)
    du1, dlr, dli, dbr, dbi, dcr, dci, dd = s5_grads(u1, dy1, xs_r, xs_i, a_r, a_i, bre, bim, dskip)
    gr["l1_d"] = dd[0]

    def b_unblock(d):
        return _block_diag_t(d, c_, p_).transpose(0, 1, 3, 2).reshape(g_, p_ * c_)

    def c_unblock(d):
        return _block_diag_t(d, p_, c_).transpose(0, 1, 3, 2).reshape(g_, c_, p_)

    gr["l1_c_re"], gr["l1_c_im"] = c_unblock(dcr), c_unblock(dci)
    dld, dar, dai, dbre, dbim = s5_params_bwd(*s5_in, dlr[0].reshape(g_, p_), dli[0].reshape(g_, p_),
                                              b_unblock(dbr), b_unblock(dbi))
    gr["l1_log_dt"], gr["l1_a_re"], gr["l1_a_im"] = dld.reshape(g_), dar, dai
    gr["l1_b_re"], gr["l1_b_im"] = dbre.reshape(g_, p_, c_), dbim.reshape(g_, p_, c_)
    dxn2 = matmul(du1, full["l1_w_in"], tb=True, name="l1_d_xn")
    gr["l1_w_in"] = matmul(xn2, du1, ta=True, name="l1_d_win")
    dx2, dg = rms_bwd(x2, _row(w["l1_mix_norm"]), dxn2, dx3, "l1_mix_rms_bwd")
    gr["l1_mix_norm"] = dg[0]

    dx1, gr["l0_ffn_norm"], gr["l0_w_up"], gr["l0_ffn_conv_w"], gr["l0_ffn_conv_b"], gr["l0_w_down"] = _ffn_bwd(
        dx2, ffn0, w["l0_ffn_norm"], w_up0a, w_up0b, full["l0_ffn_conv_w"], w["l0_ffn_conv_b"], full["l0_w_down"],
        "l0_ffn")

    du = matmul(dx1, w_out_u, tb=True, out_dtype=BF16, name="l0_d_u")
    do = matmul(dx1, w_out_a, tb=True, out_dtype=BF16, name="l0_d_o")
    d_wout_u = matmul(u, dx1, ta=True, name="l0_d_wout_u")
    d_wout_a = matmul(o, dx1, ta=True, name="l0_d_wout_a")
    gr["l0_w_out"] = jnp.concatenate(
        [d_wout_u, d_wout_a.reshape(N_HEADS, LANES, D_MODEL)[:, LANES - V_DIM:].reshape(N_HEADS * V_DIM, D_MODEL)])
    dq, dkv, dkr = attn_bwd(q, kv, kr, o, do, lse)
    dqraw = qrope_bwd(dq, cos, sin)
    dcq = matmul(dqraw, w_uqp, tb=True, name="l0_d_cq")
    d_wuqp = matmul(cq, dqraw, ta=True, name="l0_d_wuq").reshape(Q_LORA, N_HEADS, HEAD_PAD)
    gr["l0_w_uq"] = jnp.concatenate([d_wuqp[..., :QK_NOPE], d_wuqp[..., LANES:LANES + QK_ROPE]],
                                    axis=-1).reshape(Q_LORA, -1)
    dckv = matmul(dkv, w_ukv, tb=True, name="l0_d_ckv")
    gr["l0_w_ukv"] = matmul(ckv, dkv, ta=True, name="l0_d_wukv")
    du1c, dlg, dlb, dcb = convln_bwd1(u0, conv_w, cb, lg, lb, du)
    gr["l0_conv_ln_g"], gr["l0_conv_ln_b"], gr["l0_conv_b"] = dlg[0], dlb[0], dcb[0]
    du0, dcw = convln_bwd2(u0, conv_wrev, du1c)
    gr["l0_conv_w"] = dcw[:CONV_K]
    dh0, dqn, dkvn = mixpre_bwd(h0, qn_g, kvn_g, cos, sin, du0, dcq, dckv, dkr)
    gr["l0_q_norm"], gr["l0_kv_norm"] = dqn[0], dkvn[0]
    dxn0 = matmul(dh0, w_in0p, tb=True, name="l0_d_xn")
    gr["l0_w_in"] = matmul(xn0, dh0, ta=True, name="l0_d_win")[:, :w_in0.shape[1]]
    grad_x, dg = rms_bwd(x, _row(w["l0_mix_norm"]), dxn0, dx1, "l0_mix_rms_bwd")
    gr["l0_mix_norm"] = dg[0]

    g_sh = _reduce_sharded(gr)
    g_rep, pack_shape = _reduce_replicated(gr)
    grad, delta, new_m, new_v = {}, {}, {}, {}
    for n, _ in SHARDED:
        shp = w[n].shape
        two_d = (lambda t: t.reshape(shp[0], -1))
        grad[n] = g_sh[n]
        delta[n], new_m[n], new_v[n] = adamw(two_d(w[n]), two_d(g_sh[n]), two_d(mom[n]), two_d(var[n]), f"adamw_{n}")
    names = list(REPLICATED)
    pk = lambda d: _pack([d[n] for n in names], F32, 256 * LANES).reshape(pack_shape)
    dl, nm, nv = adamw(pk(w), pk(g_rep), pk(mom), pk(var), "adamw_small")
    shapes = [w[n].shape for n in names]
    for n, d_, m_, v_ in zip(names, _unpack(dl.reshape(-1), shapes), _unpack(nm.reshape(-1), shapes),
                             _unpack(nv.reshape(-1), shapes)):
        grad[n], delta[n], new_m[n], new_v[n] = g_rep[n], d_, m_, v_

    order = ["l0_mix_norm", "l0_w_in", "l0_conv_w", "l0_conv_b", "l0_conv_ln_g", "l0_conv_ln_b", "l0_q_norm",
             "l0_kv_norm", "l0_w_uq", "l0_w_ukv", "l0_w_out", "l0_ffn_norm", "l0_w_up", "l0_ffn_conv_w",
             "l0_ffn_conv_b", "l0_w_down", "l1_mix_norm", "l1_w_in", "l1_log_dt", "l1_a_re", "l1_a_im", "l1_b_re",
             "l1_b_im", "l1_c_re", "l1_c_im", "l1_d", "l1_w_glu", "l1_b_glu", "l1_ffn_norm", "l1_w_up",
             "l1_ffn_conv_w", "l1_ffn_conv_b", "l1_w_down", "final_norm"]
    return (loss, grad_x[None], *[grad[n] for n in order], *[delta[n] for n in order],
            *[new_m[n] for n in order], *[new_v[n] for n in order])
```

```python
import functools
import math

import jax
import jax.numpy as jnp
import numpy as np
from jax import lax
from jax.experimental import pallas as pl
from jax.experimental.pallas import tpu as pltpu

F32 = jnp.float32
BF16 = jnp.bfloat16
MESH = pl.DeviceIdType.MESH

D_MODEL = 1024
EPS = 1e-6
LN_EPS = 1e-5
CONV_WIDTH = 512
CONV_K = 31
N_HEADS = 8
QK_NOPE = 64
QK_ROPE = 32
V_DIM = 64
Q_LORA = 256
KV_LORA = 128
ROPE_BASE = 10000.0
ATT_SCALE = (QK_NOPE + QK_ROPE) ** -0.5
SSM_WIDTH = 512
SSM_GROUP = 16
SSM_GROUPS = 32
SSM_STATE = 64
D_FF = 2816
FFN_K = 3
ADAM_LR = 0.001
ADAM_B1 = 0.9
ADAM_B2 = 0.999
ADAM_EPS = 1e-08
ADAM_WD = 0.01
ADAM_STEP = 10

N_CHIPS = 4
LANES = 128
HEAD_PAD = 256
CONV_HALO = 32
FFN_HALO = 16
VMEM_LIMIT = 56 * 1024 * 1024

ROW_TILE = 512
FFN_ROW_TILE = 1024
FFN_COL_TILE = 256
FFN_ROW_CHUNK = 64
CONV_ROW_CHUNK = 32
ATT_TILE = 1024
SCAN_TILE = 256
SCAN_UNROLL = 4


def _cparams(*sem):
    return pltpu.CompilerParams(dimension_semantics=tuple(sem), vmem_limit_bytes=VMEM_LIMIT)


def _pick(n, cands):
    for c in cands:
        if n % c == 0:
            return c
    return n


def matmul(a, b, *, ta=False, tb=False, res=None, bias=None, out_dtype=None, name):
    if out_dtype is None:
        out_dtype = BF16 if ta else F32
    if ta:
        kdim, m = a.shape
    else:
        m, kdim = a.shape
    if tb:
        n, k2 = b.shape
    else:
        k2, n = b.shape
    assert kdim == k2, (a.shape, b.shape, ta, tb)
    tn = _pick(n, (1408, 1024, 768, 512, 384, 256, 128))
    if ta:
        tm = _pick(m, (1408, 1024, 512, 256, 128))
        tk = _pick(kdim, (512, 256, 128))
    else:
        tm = _pick(m, (1024, 512, 256, 128))
        tk = kdim
        if kdim > 1024:
            tn = _pick(n, (512, 256, 128))
        if tm * tn > 1024 * 1024 and out_dtype == F32:
            tm = _pick(m, (512, 256, 128))
    nk = kdim // tk
    has_res, has_bias = res is not None, bias is not None
    dims = (((0,) if ta else (1,), (1,) if tb else (0,)), ((), ()))

    def body(*refs):
        a_ref, b_ref = refs[0], refs[1]
        pos = 2
        res_ref = bias_ref = None
        if has_res:
            res_ref = refs[pos]
            pos += 1
        if has_bias:
            bias_ref = refs[pos]
            pos += 1
        o_ref = refs[pos]

        def finish(r):
            if has_bias:
                r = r + bias_ref[...]
            if has_res:
                r = r + res_ref[...].astype(F32)
            o_ref[...] = r.astype(o_ref.dtype)

        prod = lax.dot_general(a_ref[...].astype(BF16), b_ref[...].astype(BF16), dims, preferred_element_type=F32)
        if nk == 1:
            finish(prod)
            return
        acc_ref = refs[pos + 1]
        k = pl.program_id(2)

        @pl.when(k == 0)
        def _():
            acc_ref[...] = prod

        @pl.when(k > 0)
        def _():
            acc_ref[...] += prod

        @pl.when(k == nk - 1)
        def _():
            finish(acc_ref[...])

    a_spec = pl.BlockSpec((tk, tm), lambda i, j, k: (k, i)) if ta else pl.BlockSpec((tm, tk), lambda i, j, k: (i, k))
    b_spec = pl.BlockSpec((tn, tk), lambda i, j, k: (j, k)) if tb else pl.BlockSpec((tk, tn), lambda i, j, k: (k, j))
    in_specs = [a_spec, b_spec]
    args = [a, b]
    if has_res:
        in_specs.append(pl.BlockSpec((tm, tn), lambda i, j, k: (i, j)))
        args.append(res)
    if has_bias:
        in_specs.append(pl.BlockSpec((1, tn), lambda i, j, k: (0, j)))
        args.append(bias)
    return pl.pallas_call(
        body, name=name, grid=(m // tm, n // tn, nk),
        in_specs=in_specs, out_specs=pl.BlockSpec((tm, tn), lambda i, j, k: (i, j)),
        out_shape=jax.ShapeDtypeStruct((m, n), out_dtype),
        scratch_shapes=[pltpu.VMEM((tm, tn), F32)] if nk > 1 else [],
        compiler_params=_cparams("parallel", "parallel", "arbitrary"),
    )(*args)


def rowcall(body, *, rows, ts, ins, outs, name, scratch=()):
    nt = rows // ts
    in_specs, args = [], []
    for arr, kind in ins:
        if kind == "row":
            in_specs.append(pl.BlockSpec((ts, arr.shape[1]), lambda i: (i, 0)))
        elif kind == "rev":
            in_specs.append(pl.BlockSpec((ts, arr.shape[1]), lambda i: (nt - 1 - i, 0)))
        elif kind == "full":
            nd = arr.ndim
            in_specs.append(pl.BlockSpec(arr.shape, lambda i, nd=nd: (0,) * nd))
        elif kind.startswith("prev:"):
            h = int(kind[5:])
            r = ts // h
            in_specs.append(pl.BlockSpec((h, arr.shape[1]), lambda i, r=r: (jnp.maximum(i * r - 1, 0), 0)))
        elif kind.startswith("next:"):
            h = int(kind[5:])
            r = ts // h
            last = rows // h - 1
            in_specs.append(pl.BlockSpec((h, arr.shape[1]), lambda i, r=r, last=last: (jnp.minimum((i + 1) * r, last), 0)))
        elif kind.startswith("revprev:"):
            h = int(kind[8:])
            r = ts // h
            in_specs.append(pl.BlockSpec((h, arr.shape[1]), lambda i, r=r: (jnp.maximum((nt - 1 - i) * r - 1, 0), 0)))
        else:
            raise ValueError(kind)
        args.append(arr)
    out_specs, out_shapes = [], []
    for shape, dtype, kind in outs:
        if kind == "row":
            out_specs.append(pl.BlockSpec((ts, shape[1]), lambda i: (i, 0)))
        elif kind == "rev":
            out_specs.append(pl.BlockSpec((ts, shape[1]), lambda i: (nt - 1 - i, 0)))
        else:
            nd = len(shape)
            out_specs.append(pl.BlockSpec(tuple(shape), lambda i, nd=nd: (0,) * nd))
        out_shapes.append(jax.ShapeDtypeStruct(tuple(shape), dtype))
    return pl.pallas_call(
        functools.partial(body, nt), name=name, grid=(nt,),
        in_specs=in_specs, out_specs=tuple(out_specs), out_shape=tuple(out_shapes),
        scratch_shapes=list(scratch),
        compiler_params=_cparams("arbitrary"),
    )(*args)


def _rms(x, g):
    return x * lax.rsqrt(jnp.mean(x * x, axis=-1, keepdims=True) + EPS) * g


def _layer_norm(x, g, b):
    mu = jnp.mean(x, axis=-1, keepdims=True)
    xc = x - mu
    var = jnp.mean(xc * xc, axis=-1, keepdims=True)
    return xc * lax.rsqrt(var + LN_EPS) * g + b


def _sigmoid(x):
    return 1.0 / (1.0 + jnp.exp(-x))


def _silu(x):
    return x * _sigmoid(x)


def _gelu(x):
    return 0.5 * x * (1.0 + jnp.tanh(math.sqrt(2.0 / math.pi) * (x + 0.044715 * (x * x * x))))


def _acc(ref, i, val):
    s = jnp.sum(val, axis=0, keepdims=True)

    @pl.when(i == 0)
    def _():
        ref[...] = jnp.zeros_like(ref)

    ref[...] += jnp.broadcast_to(s, ref.shape)


def rms_fwd(x, g, name):
    s, c = x.shape

    def body(nt, x_ref, g_ref, o_ref):
        o_ref[...] = _rms(x_ref[...], g_ref[...]).astype(BF16)

    return rowcall(body, rows=s, ts=min(ROW_TILE, s), ins=[(x, "row"), (g, "full")],
                   outs=[((s, c), BF16, "row")], name=name)[0]


def rms_bwd(x, g, dxn, dres, name):
    s, c = x.shape

    def body(nt, x_ref, g_ref, d_ref, r_ref, dx_ref, dg_ref):
        i = pl.program_id(0)
        _, vjp = jax.vjp(_rms, x_ref[...], g_ref[...])
        dx, dg = vjp(d_ref[...].astype(F32))
        dx_ref[...] = dx + r_ref[...]
        _acc(dg_ref, i, dg)

    return rowcall(body, rows=s, ts=min(ROW_TILE, s),
                   ins=[(x, "row"), (g, "full"), (dxn, "row"), (dres, "row")],
                   outs=[((s, c), F32, "row"), ((8, c), F32, "acc")], name=name)


def _partner(t):
    lane = lax.broadcasted_iota(jnp.int32, t.shape, 1)
    half = QK_ROPE // 2
    return jnp.where(lane % QK_ROPE < half, pltpu.roll(t, LANES - half, 1), pltpu.roll(t, half, 1))


def _rope(t, cos, sin):
    return t * cos + _partner(t) * sin


def _rope_t(d, cos, sin):
    return d * cos + _partner(d * sin)


def rope_tables(s):
    half = QK_ROPE // 2
    inv = ROPE_BASE ** (-jnp.arange(half, dtype=F32) / half)
    ang = jnp.arange(s).astype(F32)[:, None] * inv[None, :]
    cos, sin = jnp.cos(ang), jnp.sin(ang)
    z = jnp.zeros((s, LANES - QK_ROPE), F32)
    return jnp.concatenate([cos, cos, z], axis=1), jnp.concatenate([-sin, sin, z], axis=1)


H0_A, H0_G, H0_Q, H0_KV, H0_KR, H0_W = 0, 512, 1024, 1280, 1408, 1536


def _mixpre_fn(a, g, q, kv, qn, kvn):
    return a * _sigmoid(g), _rms(q, qn), _rms(kv, kvn)


def _h0_parts(h_ref):
    return (h_ref[:, H0_A:H0_G], h_ref[:, H0_G:H0_Q], h_ref[:, H0_Q:H0_KV], h_ref[:, H0_KV:H0_KR])


def mixpre_fwd(h0, qn, kvn, cos, sin):
    s = h0.shape[0]

    def body(nt, h_ref, qn_ref, kvn_ref, cos_ref, sin_ref, u0_ref, cq_ref, ckv_ref, kr_ref):
        u0, cq, ckv = _mixpre_fn(*_h0_parts(h_ref), qn_ref[...], kvn_ref[...])
        u0_ref[...] = u0
        cq_ref[...] = cq.astype(BF16)
        ckv_ref[...] = ckv.astype(BF16)
        kr_ref[...] = _rope(h_ref[:, H0_KR:H0_W], cos_ref[...], sin_ref[...]).astype(BF16)

    return rowcall(body, rows=s, ts=min(ROW_TILE, s),
                   ins=[(h0, "row"), (qn, "full"), (kvn, "full"), (cos, "row"), (sin, "row")],
                   outs=[((s, CONV_WIDTH), F32, "row"), ((s, Q_LORA), BF16, "row"),
                         ((s, KV_LORA), BF16, "row"), ((s, LANES), BF16, "row")], name="mixpre_fwd")


def mixpre_bwd(h0, qn, kvn, cos, sin, du0, dcq, dckv, dkr):
    s = h0.shape[0]

    def body(nt, h_ref, qn_ref, kvn_ref, cos_ref, sin_ref, du0_ref, dcq_ref, dckv_ref, dkr_ref,
             dh_ref, dqn_ref, dkvn_ref):
        i = pl.program_id(0)
        _, vjp = jax.vjp(_mixpre_fn, *_h0_parts(h_ref), qn_ref[...], kvn_ref[...])
        da, dg, dq, dkv, dqn, dkvn = vjp((du0_ref[...], dcq_ref[...], dckv_ref[...]))
        dh_ref[:, H0_A:H0_G] = da.astype(BF16)
        dh_ref[:, H0_G:H0_Q] = dg.astype(BF16)
        dh_ref[:, H0_Q:H0_KV] = dq.astype(BF16)
        dh_ref[:, H0_KV:H0_KR] = dkv.astype(BF16)
        dkr = dkr_ref[:, :LANES]
        for h in range(1, N_HEADS):
            dkr = dkr + dkr_ref[:, h * LANES:(h + 1) * LANES]
        dh_ref[:, H0_KR:H0_W] = _rope_t(dkr, cos_ref[...], sin_ref[...]).astype(BF16)
        _acc(dqn_ref, i, dqn)
        _acc(dkvn_ref, i, dkvn)

    return rowcall(body, rows=s, ts=min(ROW_TILE, s),
                   ins=[(h0, "row"), (qn, "full"), (kvn, "full"), (cos, "row"), (sin, "row"),
                        (du0, "row"), (dcq, "row"), (dckv, "row"), (dkr, "row")],
                   outs=[((s, H0_W), BF16, "row"), ((8, Q_LORA), F32, "acc"), ((8, KV_LORA), F32, "acc")],
                   name="mixpre_bwd")


def _conv_taps(ext_ref, w_ref, ts, first, ntaps):
    acc = None
    for k in range(ntaps):
        term = w_ref[pl.ds(k, 1), :] * ext_ref[pl.ds(first + k, ts), :]
        acc = term if acc is None else acc + term
    return acc


def _ln_silu(u1, g, b):
    return _silu(_layer_norm(u1, g, b))


SUBLANES = 8


def _fill_shifted(sh_ref, parts, rows):
    pos = 0
    for p in parts:
        sh_ref[0, pl.ds(pos, p.shape[0]), :] = p
        pos += p.shape[0]
    sh_ref[0, pl.ds(rows, SUBLANES), :] = jnp.zeros((SUBLANES, sh_ref.shape[2]), F32)
    for r in range(1, SUBLANES):
        sh_ref[r, pl.ds(0, rows), :] = sh_ref[0, pl.ds(r, rows), :]


def _window(sh_ref, off, n):
    r = off % SUBLANES
    return sh_ref[r, pl.ds(off - r, n), :]


def _taps_aligned(sh_ref, w_ref, n, first, ntaps):
    acc = None
    for k in range(ntaps):
        term = w_ref[pl.ds(k, 1), :] * _window(sh_ref, first + k, n)
        acc = term if acc is None else acc + term
    return acc


def _conv_scratch(ts, c):
    return pltpu.VMEM((SUBLANES, ts + CONV_HALO + SUBLANES, c), F32)


def convln_fwd(u0, w, b, lg, lb):
    s, c = u0.shape
    ts = min(ROW_TILE, s)
    rc = min(CONV_ROW_CHUNK, ts)
    first = CONV_HALO - (CONV_K - 1)

    def body(nt, cur_ref, prev_ref, w_ref, b_ref, lg_ref, lb_ref, o_ref, sh_ref):
        i = pl.program_id(0)
        _fill_shifted(sh_ref, [jnp.where(i > 0, prev_ref[...], 0.0), cur_ref[...]], ts + CONV_HALO)
        for r0 in range(0, ts, rc):
            u1 = _taps_aligned(sh_ref, w_ref, rc, first + r0, CONV_K) + b_ref[...]
            o_ref[pl.ds(r0, rc), :] = _ln_silu(u1, lg_ref[...], lb_ref[...]).astype(BF16)

    return rowcall(body, rows=s, ts=ts,
                   ins=[(u0, "row"), (u0, f"prev:{CONV_HALO}"), (w, "full"), (b, "full"), (lg, "full"), (lb, "full")],
                   outs=[((s, c), BF16, "row")], name="convln_fwd", scratch=[_conv_scratch(ts, c)])[0]


def convln_bwd1(u0, w, b, lg, lb, du):
    s, c = u0.shape
    ts = min(ROW_TILE, s)
    rc = min(CONV_ROW_CHUNK, ts)
    first = CONV_HALO - (CONV_K - 1)

    def body(nt, cur_ref, prev_ref, w_ref, b_ref, lg_ref, lb_ref, du_ref, du1_ref, dlg_ref, dlb_ref, dcb_ref, sh_ref):
        i = pl.program_id(0)
        _fill_shifted(sh_ref, [jnp.where(i > 0, prev_ref[...], 0.0), cur_ref[...]], ts + CONV_HALO)
        sums = [jnp.zeros((1, c), F32)] * 3
        for r0 in range(0, ts, rc):
            u1 = _taps_aligned(sh_ref, w_ref, rc, first + r0, CONV_K) + b_ref[...]
            _, vjp = jax.vjp(_ln_silu, u1, lg_ref[...], lb_ref[...])
            du1, dlg, dlb = vjp(du_ref[pl.ds(r0, rc), :].astype(F32))
            du1_ref[pl.ds(r0, rc), :] = du1
            parts = (dlg, dlb, jnp.sum(du1, axis=0, keepdims=True))
            sums = [a + jnp.sum(p, axis=0, keepdims=True) for a, p in zip(sums, parts)]
        _acc(dlg_ref, i, sums[0])
        _acc(dlb_ref, i, sums[1])
        _acc(dcb_ref, i, sums[2])

    return rowcall(body, rows=s, ts=ts,
                   ins=[(u0, "row"), (u0, f"prev:{CONV_HALO}"), (w, "full"), (b, "full"), (lg, "full"), (lb, "full"),
                        (du, "row")],
                   outs=[((s, c), F32, "row"), ((8, c), F32, "acc"), ((8, c), F32, "acc"), ((8, c), F32, "acc")],
                   name="convln_bwd1", scratch=[_conv_scratch(ts, c)])


def convln_bwd2(u0, wrev, du1):
    s, c = u0.shape
    ts = min(ROW_TILE, s)
    rc = min(CONV_ROW_CHUNK, ts)
    first = CONV_HALO - (CONV_K - 1)

    def body(nt, cur_ref, prev_ref, d_ref, dnext_ref, wrev_ref, du0_ref, dw_ref, sh_ref, dsh_ref):
        i = pl.program_id(0)
        _fill_shifted(sh_ref, [jnp.where(i > 0, prev_ref[...], 0.0), cur_ref[...]], ts + CONV_HALO)
        _fill_shifted(dsh_ref, [d_ref[...], jnp.where(i < nt - 1, dnext_ref[...], 0.0)], ts + CONV_HALO)
        for r0 in range(0, ts, rc):
            du0_ref[pl.ds(r0, rc), :] = _taps_aligned(dsh_ref, wrev_ref, rc, r0, CONV_K)

        @pl.when(i == 0)
        def _():
            dw_ref[...] = jnp.zeros_like(dw_ref)

        for k in range(CONV_K):
            part = jnp.zeros((SUBLANES, c), F32)
            for r0 in range(0, ts, rc):
                prod = d_ref[pl.ds(r0, rc), :] * _window(sh_ref, first + k + r0, rc)
                for a in range(0, rc, SUBLANES):
                    part = part + prod[a:a + SUBLANES]
            dw_ref[pl.ds(k, 1), :] += jnp.sum(part, axis=0, keepdims=True)

    return rowcall(body, rows=s, ts=ts,
                   ins=[(u0, "row"), (u0, f"prev:{CONV_HALO}"), (du1, "row"), (du1, f"next:{CONV_HALO}"), (wrev, "full")],
                   outs=[((s, c), F32, "row"), ((CONV_HALO, c), F32, "acc")], name="convln_bwd2",
                   scratch=[_conv_scratch(ts, c), _conv_scratch(ts, c)])


def qrope_fwd(qraw, cos, sin):
    s = qraw.shape[0]

    def body(nt, q_ref, cos_ref, sin_ref, o_ref):
        cos_v, sin_v = cos_ref[...] * ATT_SCALE, sin_ref[...] * ATT_SCALE
        for h in range(N_HEADS):
            nope = q_ref[:, h * HEAD_PAD:h * HEAD_PAD + LANES] * ATT_SCALE
            o_ref[:, h * HEAD_PAD:h * HEAD_PAD + LANES] = nope.astype(BF16)
            r = q_ref[:, h * HEAD_PAD + LANES:(h + 1) * HEAD_PAD]
            o_ref[:, h * HEAD_PAD + LANES:(h + 1) * HEAD_PAD] = _rope(r, cos_v, sin_v).astype(BF16)

    return rowcall(body, rows=s, ts=min(ROW_TILE, s), ins=[(qraw, "row"), (cos, "row"), (sin, "row")],
                   outs=[((s, N_HEADS * HEAD_PAD), BF16, "row")], name="qrope_fwd")[0]


def qrope_bwd(dq, cos, sin):
    s = dq.shape[0]

    def body(nt, d_ref, cos_ref, sin_ref, o_ref):
        cos_v, sin_v = cos_ref[...] * ATT_SCALE, sin_ref[...] * ATT_SCALE
        for h in range(N_HEADS):
            nope = d_ref[:, h * HEAD_PAD:h * HEAD_PAD + LANES] * ATT_SCALE
            o_ref[:, h * HEAD_PAD:h * HEAD_PAD + LANES] = nope.astype(BF16)
            r = d_ref[:, h * HEAD_PAD + LANES:(h + 1) * HEAD_PAD].astype(F32)
            o_ref[:, h * HEAD_PAD + LANES:(h + 1) * HEAD_PAD] = _rope_t(r, cos_v, sin_v).astype(BF16)

    return rowcall(body, rows=s, ts=min(ROW_TILE, s), ins=[(dq, "row"), (cos, "row"), (sin, "row")],
                   outs=[((s, N_HEADS * HEAD_PAD), BF16, "row")], name="qrope_bwd")[0]


_NT = (((1,), (1,)), ((), ()))
_TN = (((0,), (0,)), ((), ()))


def _scores(q, kvr, diagonal):
    s = lax.dot_general(q, kvr, _NT, preferred_element_type=F32)
    if not diagonal:
        return s
    row = lax.broadcasted_iota(jnp.int32, s.shape, 0)
    col = lax.broadcasted_iota(jnp.int32, s.shape, 1)
    return jnp.where(col <= row, s, -jnp.inf)


def _on_causal_pairs(pair, k_blk, fn):
    @pl.when(k_blk < 2 * pair)
    def _():
        fn(0, False)
        fn(1, False)

    @pl.when(k_blk == 2 * pair)
    def _():
        fn(0, True)
        fn(1, False)

    @pl.when(k_blk == 2 * pair + 1)
    def _():
        fn(1, True)


def attn_fwd(q, kv, kr):
    s = q.shape[0]
    t = min(ATT_TILE, s // 2)
    n = s // t
    np_ = n // 2

    def body(q_ref, kv_ref, kr_ref, o_ref, lse_ref, m_ref, l_ref, acc_ref):
        i, j = pl.program_id(1), pl.program_id(2)

        @pl.when(j == 0)
        def _():
            m_ref[...] = jnp.full_like(m_ref, -jnp.inf)
            l_ref[...] = jnp.zeros_like(l_ref)
            acc_ref[...] = jnp.zeros_like(acc_ref)

        def block(sub, diagonal):
            kvv = kv_ref[...]
            kvr = jnp.concatenate([kvv, kr_ref[...]], axis=1)
            sc = _scores(q_ref[pl.ds(sub * t, t), :], kvr, diagonal)
            m_prev = m_ref[sub]
            m_new = jnp.maximum(m_prev, jnp.max(sc, axis=-1, keepdims=True))
            alpha = jnp.exp(m_prev - m_new)
            p = jnp.exp(sc - m_new)
            l_ref[sub] = alpha * l_ref[sub] + jnp.sum(p, axis=-1, keepdims=True)
            acc_ref[sub] = alpha * acc_ref[sub] + jnp.dot(p.astype(BF16), kvv, preferred_element_type=F32)
            m_ref[sub] = m_new

        _on_causal_pairs(i, j, block)

        @pl.when(j == 2 * i + 1)
        def _():
            for sub in range(2):
                l = l_ref[sub]
                o_ref[pl.ds(sub * t, t), :] = (acc_ref[sub] / l).astype(BF16)
                lse_ref[pl.ds(sub * t, t), :] = jnp.broadcast_to(m_ref[sub] + jnp.log(l), (t, LANES))

    kj = lambda h, i, j: (jnp.minimum(j, 2 * i + 1), h)
    return pl.pallas_call(
        body, name="attn_fwd", grid=(N_HEADS, np_, n),
        in_specs=[pl.BlockSpec((2 * t, HEAD_PAD), lambda h, i, j: (i, h)),
                  pl.BlockSpec((t, LANES), kj),
                  pl.BlockSpec((t, LANES), lambda h, i, j: (jnp.minimum(j, 2 * i + 1), 0))],
        out_specs=(pl.BlockSpec((2 * t, LANES), lambda h, i, j: (i, h)),
                   pl.BlockSpec((2 * t, LANES), lambda h, i, j: (i, h))),
        out_shape=(jax.ShapeDtypeStruct((s, N_HEADS * LANES), BF16),
                   jax.ShapeDtypeStruct((s, N_HEADS * LANES), F32)),
        scratch_shapes=[pltpu.VMEM((2, t, 1), F32), pltpu.VMEM((2, t, 1), F32), pltpu.VMEM((2, t, LANES), F32)],
        compiler_params=_cparams("parallel", "parallel", "arbitrary"),
    )(q, kv, kr)


def attn_bwd(q, kv, kr, o, do, lse):
    s = q.shape[0]
    t = min(ATT_TILE, s // 2)
    n = s // t
    np_ = n // 2

    def body(q_ref, kv_ref, kr_ref, o_ref, do_ref, lse_ref, dq_ref, dkv_ref, dkr_ref):
        j, i = pl.program_id(1), pl.program_id(2)

        @pl.when((i == 0) & (j == 0))
        def _():
            dq_ref[...] = jnp.zeros_like(dq_ref)

        @pl.when(i == 0)
        def _():
            dkv_ref[...] = jnp.zeros_like(dkv_ref)
            dkr_ref[...] = jnp.zeros_like(dkr_ref)

        def block(sub, diagonal):
            sl = pl.ds(sub * t, t)
            qv, dov, kvv = q_ref[sl, :], do_ref[sl, :], kv_ref[...]
            kvr = jnp.concatenate([kvv, kr_ref[...]], axis=1)
            p = jnp.exp(_scores(qv, kvr, diagonal) - lse_ref[sl, :1])
            dp = lax.dot_general(dov, kvv, _NT, preferred_element_type=F32)
            delta = jnp.sum(dov.astype(F32) * o_ref[sl, :].astype(F32), axis=-1, keepdims=True)
            ds = (p * (dp - delta)).astype(BF16)
            dk = lax.dot_general(ds, qv, _TN, preferred_element_type=F32)
            dkv_ref[...] += lax.dot_general(p.astype(BF16), dov, _TN, preferred_element_type=F32) + dk[:, :LANES]
            dkr_ref[...] += dk[:, LANES:]
            rows = pl.ds(pl.multiple_of((2 * i + sub) * t, t), t)
            dq_ref[rows, :] += jnp.dot(ds, kvr, preferred_element_type=F32)

        _on_causal_pairs(i, j, block)

    qi = lambda h, j, i: (jnp.maximum(i, lax.div(j, 2)), h)
    kj = lambda h, j, i: (j, h)
    return pl.pallas_call(
        body, name="attn_bwd", grid=(N_HEADS, n, np_),
        in_specs=[pl.BlockSpec((2 * t, HEAD_PAD), qi), pl.BlockSpec((t, LANES), kj),
                  pl.BlockSpec((t, LANES), lambda h, j, i: (j, 0)),
                  pl.BlockSpec((2 * t, LANES), qi), pl.BlockSpec((2 * t, LANES), qi), pl.BlockSpec((2 * t, LANES), qi)],
        out_specs=(pl.BlockSpec((s, HEAD_PAD), lambda h, j, i: (0, h)),
                   pl.BlockSpec((t, LANES), kj), pl.BlockSpec((t, LANES), kj)),
        out_shape=(jax.ShapeDtypeStruct((s, N_HEADS * HEAD_PAD), F32),
                   jax.ShapeDtypeStruct((s, N_HEADS * LANES), F32), jax.ShapeDtypeStruct((s, N_HEADS * LANES), F32)),
        compiler_params=_cparams("parallel", "arbitrary", "arbitrary"),
    )(q, kv, kr, o, do, lse)


def ffn_fwd(x, g, wa, wb, cwa, cwb, ba, bb, wd, name):
    s, d = x.shape
    f = wa.shape[1]
    ts, tf = min(FFN_ROW_TILE, s), FFN_COL_TILE
    hal = FFN_HALO
    nj = f // tf
    first = hal - (FFN_K - 1)
    rc = min(FFN_ROW_CHUNK, ts)

    def body(x_ref, xp_ref, g_ref, wa_ref, wb_ref, cwa_ref, cwb_ref, ba_ref, bb_ref, wd_ref,
             xo_ref, xn_ref, hpa_ref, hpb_ref, act_ref, xe_ref, ea_ref, eb_ref):
        i, j = pl.program_id(0), pl.program_id(1)

        @pl.when(j == 0)
        def _():
            xn = _rms(x_ref[...], g_ref[...]).astype(BF16)
            xn_ref[...] = xn
            xe_ref[pl.ds(hal, ts), :] = xn
            xe_ref[pl.ds(0, hal), :] = jnp.where(i > 0, _rms(xp_ref[...], g_ref[...]), 0.0).astype(BF16)

        xe = xe_ref[...]
        ea_ref[...] = jnp.dot(xe, wa_ref[...], preferred_element_type=F32)
        eb_ref[...] = jnp.dot(xe, wb_ref[...], preferred_element_type=F32)
        hpa_ref[...] = ea_ref[pl.ds(hal, ts), :].astype(BF16)
        hpb_ref[...] = eb_ref[pl.ds(hal, ts), :].astype(BF16)
        for r0 in range(0, ts, rc):
            ha = _conv_taps(ea_ref, cwa_ref, rc, first + r0, FFN_K) + ba_ref[...]
            hb = _conv_taps(eb_ref, cwb_ref, rc, first + r0, FFN_K) + bb_ref[...]
            act_ref[pl.ds(r0, rc), :] = (_silu(ha) * hb).astype(BF16)
        y = jnp.dot(act_ref[...], wd_ref[...], preferred_element_type=F32)

        @pl.when(j == 0)
        def _():
            xo_ref[...] = x_ref[...] + y

        @pl.when(j > 0)
        def _():
            xo_ref[...] += y

    r = ts // hal
    row = pl.BlockSpec((ts, d), lambda i, j: (i, 0))
    prev = pl.BlockSpec((hal, d), lambda i, j: (jnp.maximum(i * r - 1, 0), 0))
    gsp = pl.BlockSpec((1, d), lambda i, j: (0, 0))
    wup = pl.BlockSpec((d, tf), lambda i, j: (0, j))
    cwsp = pl.BlockSpec((8, tf), lambda i, j: (0, j))
    bsp = pl.BlockSpec((1, tf), lambda i, j: (0, j))
    wdn = pl.BlockSpec((tf, d), lambda i, j: (j, 0))
    hid = pl.BlockSpec((ts, tf), lambda i, j: (i, j))
    return pl.pallas_call(
        body, name=name, grid=(s // ts, nj),
        in_specs=[row, prev, gsp, wup, wup, cwsp, cwsp, bsp, bsp, wdn],
        out_specs=(row, row, hid, hid, hid),
        out_shape=(jax.ShapeDtypeStruct((s, d), F32), jax.ShapeDtypeStruct((s, d), BF16),
                   jax.ShapeDtypeStruct((s, f), BF16), jax.ShapeDtypeStruct((s, f), BF16),
                   jax.ShapeDtypeStruct((s, f), BF16)),
        scratch_shapes=[pltpu.VMEM((ts + hal, d), BF16), pltpu.VMEM((ts + hal, tf), F32),
                        pltpu.VMEM((ts + hal, tf), F32)],
        compiler_params=_cparams("parallel", "arbitrary"),
    )(x, x, g, wa, wb, cwa, cwb, ba, bb, wd)


def ffn_bwd(dy, hpa, hpb, wa, wb, cwa, cwb, cwra, cwrb, ba, bb, wd, name):
    s, d = dy.shape
    f = hpa.shape[1]
    ts, tf = min(FFN_ROW_TILE, s), FFN_COL_TILE
    hal = FFN_HALO
    nt, nj = s // ts, f // tf
    te = ts + hal
    first = hal - (FFN_K - 1)
    rc = min(FFN_ROW_CHUNK, ts)

    def body(dy_ref, dyn_ref, a_ref, ap_ref, an_ref, b_ref, bp_ref, bn_ref, wa_ref, wb_ref, cwa_ref, cwb_ref,
             cwra_ref, cwrb_ref, ba_ref, bb_ref, wd_ref,
             dxn_ref, dpa_ref, dpb_ref, dwa_ref, dwb_ref, dba_ref, dbb_ref,
             dye_ref, ea_ref, eb_ref, dact_ref, da_ref, db_ref):
        i, j = pl.program_id(0), pl.program_id(1)
        last = i == nt - 1

        @pl.when(j == 0)
        def _():
            dye_ref[pl.ds(0, ts), :] = dy_ref[...].astype(BF16)
            dye_ref[pl.ds(ts, hal), :] = jnp.where(last, 0.0, dyn_ref[...]).astype(BF16)

        @pl.when((i == 0) & (j == 0))
        def _():
            for r in (dwa_ref, dwb_ref, dba_ref, dbb_ref):
                r[...] = jnp.zeros_like(r)

        dact_ref[...] = lax.dot_general(dye_ref[...], wd_ref[...], _NT, preferred_element_type=F32)
        for cur, prev, nxt, ext in ((a_ref, ap_ref, an_ref, ea_ref), (b_ref, bp_ref, bn_ref, eb_ref)):
            ext[pl.ds(0, hal), :] = jnp.where(i > 0, prev[...].astype(F32), 0.0)
            ext[pl.ds(hal, ts), :] = cur[...].astype(F32)
            ext[pl.ds(hal + ts, hal), :] = jnp.where(last, 0.0, nxt[...].astype(F32))
        zero = jnp.zeros((1, tf), F32)
        sums = {"ba": zero, "bb": zero, **{("a", k): zero for k in range(FFN_K)}, **{("b", k): zero for k in range(FFN_K)}}
        for r0 in list(range(0, ts, rc)) + [ts]:
            n = rc if r0 < ts else hal
            win_a = [ea_ref[pl.ds(first + r0 + k, n), :] for k in range(FFN_K)]
            win_b = [eb_ref[pl.ds(first + r0 + k, n), :] for k in range(FFN_K)]
            ha = sum(cwa_ref[pl.ds(k, 1), :] * win_a[k] for k in range(FFN_K)) + ba_ref[...]
            hb = sum(cwb_ref[pl.ds(k, 1), :] * win_b[k] for k in range(FFN_K)) + bb_ref[...]
            sig = _sigmoid(ha)
            gs = dact_ref[pl.ds(r0, n), :] * sig
            dha = gs * hb * (1.0 + ha * (1.0 - sig))
            dhb = gs * ha
            da_ref[pl.ds(r0, n), :] = dha
            db_ref[pl.ds(r0, n), :] = dhb
            if r0 < ts:
                sums["ba"] = sums["ba"] + jnp.sum(dha, axis=0, keepdims=True)
                sums["bb"] = sums["bb"] + jnp.sum(dhb, axis=0, keepdims=True)
                for k in range(FFN_K):
                    sums["a", k] = sums["a", k] + jnp.sum(dha * win_a[k], axis=0, keepdims=True)
                    sums["b", k] = sums["b", k] + jnp.sum(dhb * win_b[k], axis=0, keepdims=True)
        for r0 in range(0, ts, rc):
            dpa_ref[pl.ds(r0, rc), :] = _conv_taps(da_ref, cwra_ref, rc, r0, FFN_K).astype(BF16)
            dpb_ref[pl.ds(r0, rc), :] = _conv_taps(db_ref, cwrb_ref, rc, r0, FFN_K).astype(BF16)
        dba_ref[j] += jnp.broadcast_to(sums["ba"], (8, tf))
        dbb_ref[j] += jnp.broadcast_to(sums["bb"], (8, tf))
        row = lax.broadcasted_iota(jnp.int32, (8, tf), 0)
        dwa_ref[j] += sum(jnp.where(row == k, sums["a", k], 0.0) for k in range(FFN_K))
        dwb_ref[j] += sum(jnp.where(row == k, sums["b", k], 0.0) for k in range(FFN_K))
        dxn = (lax.dot_general(dpa_ref[...], wa_ref[...], _NT, preferred_element_type=F32)
               + lax.dot_general(dpb_ref[...], wb_ref[...], _NT, preferred_element_type=F32))

        @pl.when(j == 0)
        def _():
            dxn_ref[...] = dxn

        @pl.when(j > 0)
        def _():
            dxn_ref[...] += dxn

    r = ts // hal
    lastblk = s // hal - 1
    row = pl.BlockSpec((ts, d), lambda i, j: (i, 0))
    rown = pl.BlockSpec((hal, d), lambda i, j: (jnp.minimum((i + 1) * r, lastblk), 0))
    cur = pl.BlockSpec((ts, tf), lambda i, j: (i, j))
    prev = pl.BlockSpec((hal, tf), lambda i, j: (jnp.maximum(i * r - 1, 0), j))
    nxt = pl.BlockSpec((hal, tf), lambda i, j: (jnp.minimum((i + 1) * r, lastblk), j))
    wup = pl.BlockSpec((d, tf), lambda i, j: (0, j))
    cwsp = pl.BlockSpec((8, tf), lambda i, j: (0, j))
    bsp = pl.BlockSpec((1, tf), lambda i, j: (0, j))
    wdn = pl.BlockSpec((tf, d), lambda i, j: (j, 0))
    accsp = pl.BlockSpec((nj, 8, tf), lambda i, j: (0, 0, 0))
    accshape = jax.ShapeDtypeStruct((nj, 8, tf), F32)
    return pl.pallas_call(
        body, name=name, grid=(nt, nj),
        in_specs=[row, rown, cur, prev, nxt, cur, prev, nxt, wup, wup, cwsp, cwsp, cwsp, cwsp, bsp, bsp, wdn],
        out_specs=(row, cur, cur, accsp, accsp, accsp, accsp),
        out_shape=(jax.ShapeDtypeStruct((s, d), F32), jax.ShapeDtypeStruct((s, f), BF16),
                   jax.ShapeDtypeStruct((s, f), BF16), accshape, accshape, accshape, accshape),
        scratch_shapes=[pltpu.VMEM((te, d), BF16), pltpu.VMEM((ts + 2 * hal, tf), F32),
                        pltpu.VMEM((ts + 2 * hal, tf), F32), pltpu.VMEM((te, tf), F32),
                        pltpu.VMEM((te, tf), F32), pltpu.VMEM((te, tf), F32)],
        compiler_params=_cparams("arbitrary", "arbitrary"),
    )(dy, dy, hpa, hpa, hpa, hpb, hpb, hpb, wa, wb, cwa, cwb, cwra, cwrb, ba, bb, wd)


NQ = 4
SQ = SSM_STATE * 8
NS = SSM_GROUPS * SSM_STATE


def _s5_disc(log_dt, a_re, a_im, b_re, b_im, expand):
    dt = jnp.exp(log_dt)
    mag = jnp.exp(a_re * dt)
    lb_re, lb_im = mag * jnp.cos(a_im * dt), mag * jnp.sin(a_im * dt)
    den = a_re * a_re + a_im * a_im
    nr, ni = lb_re - 1.0, lb_im
    f_re = (nr * a_re + ni * a_im) / den
    f_im = (ni * a_re - nr * a_im) / den
    fe_re = jnp.dot(f_re, expand, precision=lax.Precision.HIGHEST, preferred_element_type=F32)
    fe_im = jnp.dot(f_im, expand, precision=lax.Precision.HIGHEST, preferred_element_type=F32)
    return lb_re, lb_im, fe_re * b_re - fe_im * b_im, fe_re * b_im + fe_im * b_re


def _expand_matrix():
    e = np.zeros((SSM_STATE, SSM_STATE * SSM_GROUP), np.float32)
    for p in range(SSM_STATE):
        e[p, p * SSM_GROUP:(p + 1) * SSM_GROUP] = 1.0
    return jnp.asarray(e)


def s5_params_fwd(log_dt, a_re, a_im, b_re, b_im):
    expand = _expand_matrix()

    def body(ld_ref, ar_ref, ai_ref, br_ref, bi_ref, e_ref, lr_ref, li_ref, bbr_ref, bbi_ref):
        lr, li, bbr, bbi = _s5_disc(ld_ref[...], ar_ref[...], ai_ref[...], br_ref[...], bi_ref[...], e_ref[...])
        lr_ref[...] = lr
        li_ref[...] = li
        bbr_ref[...] = bbr
        bbi_ref[...] = bbi

    g, p, pc = SSM_GROUPS, SSM_STATE, SSM_STATE * SSM_GROUP
    return pl.pallas_call(
        body, name="s5_params_fwd",
        out_shape=(jax.ShapeDtypeStruct((g, p), F32), jax.ShapeDtypeStruct((g, p), F32),
                   jax.ShapeDtypeStruct((g, pc), F32), jax.ShapeDtypeStruct((g, pc), F32)),
    )(log_dt, a_re, a_im, b_re, b_im, expand)


def s5_params_bwd(log_dt, a_re, a_im, b_re, b_im, dlr, dli, dbbr, dbbi):
    expand = _expand_matrix()

    def body(ld_ref, ar_ref, ai_ref, br_ref, bi_ref, e_ref, dlr_ref, dli_ref, dbbr_ref, dbbi_ref,
             dld_ref, dar_ref, dai_ref, dbr_ref, dbi_ref):
        e = e_ref[...]
        f = lambda ld, ar, ai, br, bi: _s5_disc(ld, ar, ai, br, bi, e)
        _, vjp = jax.vjp(f, ld_ref[...], ar_ref[...], ai_ref[...], br_ref[...], bi_ref[...])
        dld, dar, dai, dbr, dbi = vjp((dlr_ref[...], dli_ref[...], dbbr_ref[...], dbbi_ref[...]))
        dld_ref[...] = dld
        dar_ref[...] = dar
        dai_ref[...] = dai
        dbr_ref[...] = dbr
        dbi_ref[...] = dbi

    g, p, pc = SSM_GROUPS, SSM_STATE, SSM_STATE * SSM_GROUP
    return pl.pallas_call(
        body, name="s5_params_bwd",
        out_shape=(jax.ShapeDtypeStruct((g, 1), F32), jax.ShapeDtypeStruct((g, p), F32),
                   jax.ShapeDtypeStruct((g, p), F32), jax.ShapeDtypeStruct((g, pc), F32),
                   jax.ShapeDtypeStruct((g, pc), F32)),
    )(log_dt, a_re, a_im, b_re, b_im, expand, dlr, dli, dbbr, dbbi)


def _cmul(ar, ai, br, bi):
    return ar * br - ai * bi, ar * bi + ai * br


def _power_rows(lr, li, conj_rev):
    row = lax.broadcasted_iota(jnp.int32, (8, NS), 0)
    tr = jnp.zeros((8, NS), F32)
    ti = jnp.zeros((8, NS), F32)
    pr, pi = lr, li
    for r in range(8):
        dst = 7 - r if conj_rev else r
        tr = jnp.where(row == dst, pr, tr)
        ti = jnp.where(row == dst, -pi if conj_rev else pi, ti)
        if r < 7:
            pr, pi = _cmul(pr, pi, lr, li)
    return tr, ti


def _scan8(xr, xi, tr_ref, ti_ref, cr, ci, reverse):
    row = lax.broadcasted_iota(jnp.int32, xr.shape, 0)
    for d in (1, 2, 4):
        if reverse:
            sr, si = pltpu.roll(xr, 8 - d, 0), pltpu.roll(xi, 8 - d, 0)
            keep = row < 8 - d
            pw = 8 - d
        else:
            sr, si = pltpu.roll(xr, d, 0), pltpu.roll(xi, d, 0)
            keep = row >= d
            pw = d - 1
        mr, mi = _cmul(tr_ref[pl.ds(pw, 1), :], ti_ref[pl.ds(pw, 1), :], sr, si)
        xr = xr + jnp.where(keep, mr, 0.0)
        xi = xi + jnp.where(keep, mi, 0.0)
    mr, mi = _cmul(tr_ref[...], ti_ref[...], cr, ci)
    return xr + mr, xi + mi


def _row_of(x, r):
    row = lax.broadcasted_iota(jnp.int32, x.shape, 0)
    return jnp.sum(jnp.where(row == r, x, 0.0), axis=0, keepdims=True)


def s5_scan_fwd(u, lam_r, lam_i, bre, bim, cre, cim, dskip):
    s = u.shape[0]
    tt = min(SCAN_TILE, s)
    nb = tt // 8

    def body(nt, u_ref, lr_ref, li_ref, bre_ref, bim_ref, cre_ref, cim_ref, d_ref,
             xr_ref, xi_ref, y_ref, yg_ref, tr_ref, ti_ref, cr_ref, ci_ref):
        i = pl.program_id(0)

        @pl.when(i == 0)
        def _():
            tr, ti = _power_rows(lr_ref[...], li_ref[...], False)
            tr_ref[...] = tr
            ti_ref[...] = ti
            cr_ref[...] = jnp.zeros_like(cr_ref)
            ci_ref[...] = jnp.zeros_like(ci_ref)

        uv = u_ref[...]
        ub = uv.astype(BF16)
        for q in range(NQ):
            uq = ub[:, q * LANES:(q + 1) * LANES]
            xr_ref[:, q * SQ:(q + 1) * SQ] = jnp.dot(uq, bre_ref[q], preferred_element_type=F32)
            xi_ref[:, q * SQ:(q + 1) * SQ] = jnp.dot(uq, bim_ref[q], preferred_element_type=F32)

        def step(b, carry):
            cr, ci = carry
            rows = pl.ds(pl.multiple_of(b * 8, 8), 8)
            xr, xi = _scan8(xr_ref[rows, :], xi_ref[rows, :], tr_ref, ti_ref, cr, ci, False)
            xr_ref[rows, :] = xr
            xi_ref[rows, :] = xi
            return _row_of(xr, 7), _row_of(xi, 7)

        cr, ci = lax.fori_loop(0, nb, step, (cr_ref[...], ci_ref[...]), unroll=min(SCAN_UNROLL, nb))
        cr_ref[...] = cr
        ci_ref[...] = ci
        y = d_ref[...] * uv
        for q in range(NQ):
            yq = (jnp.dot(xr_ref[:, q * SQ:(q + 1) * SQ].astype(BF16), cre_ref[q], preferred_element_type=F32)
                  - jnp.dot(xi_ref[:, q * SQ:(q + 1) * SQ].astype(BF16), cim_ref[q], preferred_element_type=F32))
            y_ref[:, q * LANES:(q + 1) * LANES] = yq + y[:, q * LANES:(q + 1) * LANES]
        yg_ref[...] = _gelu(y_ref[...]).astype(BF16)

    return rowcall(body, rows=s, ts=tt,
                   ins=[(u, "row"), (lam_r, "full"), (lam_i, "full"), (bre, "full"), (bim, "full"),
                        (cre, "full"), (cim, "full"), (dskip, "full")],
                   outs=[((s, NS), F32, "row"), ((s, NS), F32, "row"), ((s, SSM_WIDTH), F32, "row"),
                         ((s, SSM_WIDTH), BF16, "row")], name="s5_scan_fwd",
                   scratch=[pltpu.VMEM((8, NS), F32), pltpu.VMEM((8, NS), F32),
                            pltpu.VMEM((1, NS), F32), pltpu.VMEM((1, NS), F32)])


def s5_scan_bwd(dyg, y, lam_r, lam_i, cre, cim):
    s = y.shape[0]
    tt = min(SCAN_TILE, s)
    nb = tt // 8

    def body(nt, dyg_ref, y_ref, lr_ref, li_ref, cre_ref, cim_ref,
             ar_ref, ai_ref, dy_ref, tr_ref, ti_ref, cr_ref, ci_ref):
        i = pl.program_id(0)

        @pl.when(i == 0)
        def _():
            tr, ti = _power_rows(lr_ref[...], li_ref[...], True)
            tr_ref[...] = tr
            ti_ref[...] = ti
            cr_ref[...] = jnp.zeros_like(cr_ref)
            ci_ref[...] = jnp.zeros_like(ci_ref)

        _, vjp = jax.vjp(_gelu, y_ref[...])
        dy = vjp(dyg_ref[...])[0]
        dyb = dy.astype(BF16)
        dy_ref[...] = dyb
        for q in range(NQ):
            dq = dyb[:, q * LANES:(q + 1) * LANES]
            ar_ref[:, q * SQ:(q + 1) * SQ] = lax.dot_general(dq, cre_ref[q], _NT, preferred_element_type=F32)
            ai_ref[:, q * SQ:(q + 1) * SQ] = -lax.dot_general(dq, cim_ref[q], _NT, preferred_element_type=F32)

        def step(b, carry):
            cr, ci = carry
            rows = pl.ds(pl.multiple_of((nb - 1 - b) * 8, 8), 8)
            xr, xi = _scan8(ar_ref[rows, :], ai_ref[rows, :], tr_ref, ti_ref, cr, ci, True)
            ar_ref[rows, :] = xr
            ai_ref[rows, :] = xi
            return _row_of(xr, 0), _row_of(xi, 0)

        cr, ci = lax.fori_loop(0, nb, step, (cr_ref[...], ci_ref[...]), unroll=min(SCAN_UNROLL, nb))
        cr_ref[...] = cr
        ci_ref[...] = ci

    return rowcall(body, rows=s, ts=tt,
                   ins=[(dyg, "rev"), (y, "rev"), (lam_r, "full"), (lam_i, "full"), (cre, "full"), (cim, "full")],
                   outs=[((s, NS), F32, "rev"), ((s, NS), F32, "rev"), ((s, SSM_WIDTH), BF16, "rev")],
                   name="s5_scan_bwd",
                   scratch=[pltpu.VMEM((8, NS), F32), pltpu.VMEM((8, NS), F32),
                            pltpu.VMEM((1, NS), F32), pltpu.VMEM((1, NS), F32)])


def s5_grads(u, dy, xr, xi, ar, ai, bre, bim, dskip):
    s = u.shape[0]
    tt = min(SCAN_TILE, s)

    def body(nt, u_ref, dy_ref, xr_ref, xrp_ref, xi_ref, xip_ref, ar_ref, ai_ref, bre_ref, bim_ref, d_ref,
             du_ref, dlr_ref, dli_ref, dbr_ref, dbi_ref, dcr_ref, dci_ref, dd_ref, er_ref, ei_ref):
        i = pl.program_id(0)

        @pl.when(i == 0)
        def _():
            for r in (dbr_ref, dbi_ref, dcr_ref, dci_ref):
                r[...] = jnp.zeros_like(r)

        uv, dyb = u_ref[...], dy_ref[...]
        dyf = dyb.astype(F32)
        av_r, av_i, xv_r, xv_i = ar_ref[...], ai_ref[...], xr_ref[...], xi_ref[...]
        er_ref[pl.ds(0, 8), :] = jnp.where(i > 0, xrp_ref[...], 0.0)
        ei_ref[pl.ds(0, 8), :] = jnp.where(i > 0, xip_ref[...], 0.0)
        er_ref[pl.ds(8, tt), :] = xv_r
        ei_ref[pl.ds(8, tt), :] = xv_i
        sr, si = er_ref[pl.ds(7, tt), :], ei_ref[pl.ds(7, tt), :]
        _acc(dlr_ref, i, av_r * sr + av_i * si)
        _acc(dli_ref, i, av_i * sr - av_r * si)
        _acc(dd_ref, i, dyf * uv)
        ub = uv.astype(BF16)
        ab_r, ab_i = av_r.astype(BF16), av_i.astype(BF16)
        xb_r, xb_i = xv_r.astype(BF16), xv_i.astype(BF16)
        du = d_ref[...] * dyf
        for q in range(NQ):
            cs, ss = slice(q * LANES, (q + 1) * LANES), slice(q * SQ, (q + 1) * SQ)
            dbr_ref[q] += lax.dot_general(ub[:, cs], ab_r[:, ss], _TN, preferred_element_type=F32)
            dbi_ref[q] += lax.dot_general(ub[:, cs], ab_i[:, ss], _TN, preferred_element_type=F32)
            dcr_ref[q] += lax.dot_general(xb_r[:, ss], dyb[:, cs], _TN, preferred_element_type=F32)
            dci_ref[q] -= lax.dot_general(xb_i[:, ss], dyb[:, cs], _TN, preferred_element_type=F32)
            du_ref[:, cs] = (du[:, cs]
                             + lax.dot_general(ab_r[:, ss], bre_ref[q], _NT, preferred_element_type=F32)
                             + lax.dot_general(ab_i[:, ss], bim_ref[q], _NT, preferred_element_type=F32))

    return rowcall(body, rows=s, ts=tt,
                   ins=[(u, "row"), (dy, "row"), (xr, "row"), (xr, "prev:8"), (xi, "row"), (xi, "prev:8"),
                        (ar, "row"), (ai, "row"), (bre, "full"), (bim, "full"), (dskip, "full")],
                   outs=[((s, SSM_WIDTH), F32, "row"), ((8, NS), F32, "acc"), ((8, NS), F32, "acc"),
                         ((NQ, LANES, SQ), F32, "acc"), ((NQ, LANES, SQ), F32, "acc"),
                         ((NQ, SQ, LANES), F32, "acc"), ((NQ, SQ, LANES), F32, "acc"),
                         ((8, SSM_WIDTH), F32, "acc")], name="s5_grads",
                   scratch=[pltpu.VMEM((tt + 8, NS), F32), pltpu.VMEM((tt + 8, NS), F32)])


def _glu_fn(za, zb):
    return za * _sigmoid(zb)


def glu_res_fwd(z, xres):
    s = z.shape[0]

    def body(nt, z_ref, x_ref, o_ref):
        o_ref[...] = x_ref[...] + _glu_fn(z_ref[:, :D_MODEL].astype(F32), z_ref[:, D_MODEL:].astype(F32))

    return rowcall(body, rows=s, ts=min(ROW_TILE, s), ins=[(z, "row"), (xres, "row")],
                   outs=[((s, D_MODEL), F32, "row")], name="glu_res_fwd")[0]


def glu_bwd(z, dout):
    s, c = z.shape

    def body(nt, z_ref, d_ref, dz_ref, dba_ref, dbb_ref):
        i = pl.program_id(0)
        _, vjp = jax.vjp(_glu_fn, z_ref[:, :D_MODEL].astype(F32), z_ref[:, D_MODEL:].astype(F32))
        dza, dzb = vjp(d_ref[...])
        dz_ref[:, :D_MODEL] = dza.astype(BF16)
        dz_ref[:, D_MODEL:] = dzb.astype(BF16)
        _acc(dba_ref, i, dza)
        _acc(dbb_ref, i, dzb)

    return rowcall(body, rows=s, ts=min(ROW_TILE, s), ins=[(z, "row"), (dout, "row")],
                   outs=[((s, c), BF16, "row"), ((8, D_MODEL), F32, "acc"), ((8, D_MODEL), F32, "acc")],
                   name="glu_bwd")


def loss_head(x, g, target):
    s, c = x.shape

    def body(nt, x_ref, g_ref, t_ref, loss_ref, dx_ref, dg_ref):
        i = pl.program_id(0)
        y, vjp = jax.vjp(_rms, x_ref[...], g_ref[...])
        err = y - t_ref[...]
        dx, dg = vjp(err * (1.0 / c))
        dx_ref[...] = dx
        _acc(dg_ref, i, dg)
        part = jnp.sum(jnp.sum(err * err, axis=-1, keepdims=True), axis=0, keepdims=True) * (0.5 / c)

        @pl.when(i == 0)
        def _():
            loss_ref[...] = jnp.zeros_like(loss_ref)

        loss_ref[...] += jnp.broadcast_to(part, loss_ref.shape)

    return rowcall(body, rows=s, ts=min(ROW_TILE, s), ins=[(x, "row"), (g, "full"), (target, "row")],
                   outs=[((8, LANES), F32, "acc"), ((s, c), F32, "row"), ((8, c), F32, "acc")], name="loss_head")


def _tile_rows(r, cands=(512, 256, 128, 64, 32, 16, 8)):
    return _pick(r, cands)


def add_to_bf16(a, b, name):
    n, r, c = a.shape
    tr = _tile_rows(r)

    def body(a_ref, b_ref, o_ref):
        o_ref[...] = (a_ref[...].astype(F32) + b_ref[...].astype(F32)).astype(BF16)

    spec = pl.BlockSpec((1, tr, c), lambda j, i: (j, i, 0))
    return pl.pallas_call(body, name=name, grid=(n, r // tr), in_specs=[spec, spec], out_specs=spec,
                          out_shape=jax.ShapeDtypeStruct((n, r, c), BF16),
                          compiler_params=_cparams("parallel", "parallel"))(a, b)


def sum_leading(a, name):
    n, r, c = a.shape
    tr = _tile_rows(r)

    def body(a_ref, o_ref):
        acc = a_ref[0].astype(F32)
        for k in range(1, n):
            acc = acc + a_ref[k].astype(F32)
        o_ref[...] = acc

    return pl.pallas_call(body, name=name, grid=(r // tr,),
                          in_specs=[pl.BlockSpec((n, tr, c), lambda i: (0, i, 0))],
                          out_specs=pl.BlockSpec((tr, c), lambda i: (i, 0)),
                          out_shape=jax.ShapeDtypeStruct((r, c), F32),
                          compiler_params=_cparams("parallel"))(a)


def adamw(w, g, m, v, name):
    r, c = w.shape
    tr = _tile_rows(r, (256, 128, 64, 32, 16, 8))
    c1 = 1.0 - ADAM_B1 ** ADAM_STEP
    c2 = 1.0 - ADAM_B2 ** ADAM_STEP

    def body(w_ref, g_ref, m_ref, v_ref, d_ref, nm_ref, nv_ref):
        gv = g_ref[...]
        mn = ADAM_B1 * m_ref[...] + (1.0 - ADAM_B1) * gv
        vn = ADAM_B2 * v_ref[...] + (1.0 - ADAM_B2) * (gv * gv)
        d_ref[...] = -ADAM_LR * ((mn / c1) / (jnp.sqrt(vn / c2) + ADAM_EPS) + ADAM_WD * w_ref[...])
        nm_ref[...] = mn
        nv_ref[...] = vn

    spec = pl.BlockSpec((tr, c), lambda i: (i, 0))
    shp = jax.ShapeDtypeStruct((r, c), F32)
    return pl.pallas_call(body, name=name, grid=(r // tr,), in_specs=[spec] * 4, out_specs=(spec,) * 3,
                          out_shape=(shp,) * 3, compiler_params=_cparams("parallel"))(w, g, m, v)


_ANY = pl.BlockSpec(memory_space=pl.ANY)


def all_gather8(block, name):
    r, c = block.shape

    def body(x_ref, out_ref, send_sems, recv_sems, local_sem):
        x, y, cc = lax.axis_index("x"), lax.axis_index("y"), lax.axis_index("c")
        me, sibling = (x, y, cc), (x, y, 1 - cc)
        chips = [(1 - x, y), (x, 1 - y), (1 - x, 1 - y)]

        def slot(px, py, pc):
            return out_ref.at[4 * px + 2 * py + pc]

        def copy(k, blk, to, src=None):
            return pltpu.make_async_remote_copy(
                src_ref=slot(*blk) if src is None else src, dst_ref=slot(*blk),
                send_sem=send_sems.at[k], recv_sem=recv_sems.at[k], device_id=to, device_id_type=MESH)

        mine = pltpu.make_async_copy(x_ref, slot(*me), local_sem)
        mine.start()
        first = [copy(0, me, sibling, src=x_ref)]
        first += [copy(1 + j, me, (*chip, cc), src=x_ref) for j, chip in enumerate(chips)]
        for cp in first:
            cp.start()
        passed = [copy(4 + j, (*chip, cc), sibling) for j, chip in enumerate(chips)]
        for j, chip in enumerate(chips):
            copy(1 + j, (*chip, cc), me).wait_recv()
            passed[j].start()
        copy(0, sibling, me).wait_recv()
        for j, chip in enumerate(chips):
            copy(4 + j, (*chip, 1 - cc), me).wait_recv()
        for cp in first + passed:
            cp.wait_send()
        mine.wait()

    return pl.pallas_call(
        body, name=name, in_specs=[_ANY], out_specs=_ANY,
        out_shape=jax.ShapeDtypeStruct((8, r, c), block.dtype),
        scratch_shapes=[pltpu.SemaphoreType.DMA((7,)), pltpu.SemaphoreType.DMA((7,)), pltpu.SemaphoreType.DMA],
    )(block)


def sibling_swap(block, name):
    def body(x_ref, out_ref, send_sem, recv_sem):
        x, y, cc = lax.axis_index("x"), lax.axis_index("y"), lax.axis_index("c")
        cp = pltpu.make_async_remote_copy(src_ref=x_ref, dst_ref=out_ref, send_sem=send_sem, recv_sem=recv_sem,
                                          device_id=(x, y, 1 - cc), device_id_type=MESH)
        cp.start()
        cp.wait()

    return pl.pallas_call(
        body, name=name, in_specs=[_ANY], out_specs=_ANY,
        out_shape=jax.ShapeDtypeStruct(block.shape, block.dtype),
        scratch_shapes=[pltpu.SemaphoreType.DMA, pltpu.SemaphoreType.DMA],
    )(block)


def chip_exchange(parts, name):
    def body(p_ref, out_ref, send_sems, recv_sems, local_sem):
        x, y, cc = lax.axis_index("x"), lax.axis_index("y"), lax.axis_index("c")
        me = 2 * x + y
        chips = [(1 - x, y), (x, 1 - y), (1 - x, 1 - y)]
        mine = pltpu.make_async_copy(p_ref.at[me], out_ref.at[me], local_sem)
        mine.start()
        sends = []
        for k, (px, py) in enumerate(chips):
            sends.append(pltpu.make_async_remote_copy(
                src_ref=p_ref.at[2 * px + py], dst_ref=out_ref.at[me],
                send_sem=send_sems.at[k], recv_sem=recv_sems.at[k], device_id=(px, py, cc), device_id_type=MESH))
        for cp in sends:
            cp.start()
        for k, (px, py) in enumerate(chips):
            pltpu.make_async_remote_copy(
                src_ref=p_ref.at[me], dst_ref=out_ref.at[2 * px + py],
                send_sem=send_sems.at[k], recv_sem=recv_sems.at[k], device_id=(px, py, cc),
                device_id_type=MESH).wait_recv()
        for cp in sends:
            cp.wait_send()
        mine.wait()

    return pl.pallas_call(
        body, name=name, in_specs=[_ANY], out_specs=_ANY,
        out_shape=jax.ShapeDtypeStruct(parts.shape, parts.dtype),
        scratch_shapes=[pltpu.SemaphoreType.DMA((3,)), pltpu.SemaphoreType.DMA((3,)), pltpu.SemaphoreType.DMA],
    )(parts)


PACK_COLS = 1024
SHARDED = (("l0_w_in", 1), ("l0_w_uq", 1), ("l0_w_ukv", 1), ("l0_w_out", 0), ("l0_w_up", 1), ("l0_w_down", 0),
           ("l1_w_in", 0), ("l1_w_glu", 1), ("l1_w_up", 1), ("l1_w_down", 0),
           ("l0_conv_w", 1), ("l0_ffn_conv_w", 1), ("l1_ffn_conv_w", 1))
REPLICATED = ("l0_mix_norm", "l0_conv_b", "l0_conv_ln_g", "l0_conv_ln_b", "l0_q_norm", "l0_kv_norm", "l0_ffn_norm",
              "l0_ffn_conv_b", "l1_mix_norm", "l1_log_dt", "l1_a_re", "l1_a_im", "l1_b_re", "l1_b_im", "l1_c_re",
              "l1_c_im", "l1_d", "l1_b_glu", "l1_ffn_norm", "l1_ffn_conv_b", "final_norm")


def _pack(arrs, dtype, mult):
    flat = jnp.concatenate([a.reshape(-1).astype(dtype) for a in arrs])
    n = flat.shape[0]
    total = -(-n // mult) * mult
    return jnp.pad(flat, (0, total - n))


def _unpack(flat, shapes):
    out, pos = [], 0
    for shp in shapes:
        n = int(np.prod(shp))
        out.append(flat[pos:pos + n].reshape(shp))
        pos += n
    return out


def _shard(full, axis, j):
    n = full.shape[axis] // N_CHIPS
    return lax.slice_in_dim(full, j * n, (j + 1) * n, axis=axis)


def _block_diag(t):
    q, g, a, b = t.shape
    eye = jnp.eye(g, dtype=t.dtype)
    return jnp.einsum("qgab,gh->qgahb", t, eye).reshape(q, g * a, g * b)


def _block_diag_t(d, a, b):
    q = d.shape[0]
    d5 = d.reshape(q, 8, a, 8, b)
    eye = jnp.eye(8, dtype=d.dtype)
    return jnp.einsum("qgahb,gh->qgab", d5, eye)


def kernel(x, l0_mix_norm, l0_w_in, l0_conv_w, l0_conv_b, l0_conv_ln_g, l0_conv_ln_b, l0_q_norm, l0_kv_norm, l0_w_uq, l0_w_ukv, l0_w_out, l0_ffn_norm, l0_w_up, l0_ffn_conv_w, l0_ffn_conv_b, l0_w_down, l1_mix_norm, l1_w_in, l1_log_dt, l1_a_re, l1_a_im, l1_b_re, l1_b_im, l1_c_re, l1_c_im, l1_d, l1_w_glu, l1_b_glu, l1_ffn_norm, l1_w_up, l1_ffn_conv_w, l1_ffn_conv_b, l1_w_down, final_norm, loss_target, m_l0_mix_norm, m_l0_w_in, m_l0_conv_w, m_l0_conv_b, m_l0_conv_ln_g, m_l0_conv_ln_b, m_l0_q_norm, m_l0_kv_norm, m_l0_w_uq, m_l0_w_ukv, m_l0_w_out, m_l0_ffn_norm, m_l0_w_up, m_l0_ffn_conv_w, m_l0_ffn_conv_b, m_l0_w_down, m_l1_mix_norm, m_l1_w_in, m_l1_log_dt, m_l1_a_re, m_l1_a_im, m_l1_b_re, m_l1_b_im, m_l1_c_re, m_l1_c_im, m_l1_d, m_l1_w_glu, m_l1_b_glu, m_l1_ffn_norm, m_l1_w_up, m_l1_ffn_conv_w, m_l1_ffn_conv_b, m_l1_w_down, m_final_norm, v_l0_mix_norm, v_l0_w_in, v_l0_conv_w, v_l0_conv_b, v_l0_conv_ln_g, v_l0_conv_ln_b, v_l0_q_norm, v_l0_kv_norm, v_l0_w_uq, v_l0_w_ukv, v_l0_w_out, v_l0_ffn_norm, v_l0_w_up, v_l0_ffn_conv_w, v_l0_ffn_conv_b, v_l0_w_down, v_l1_mix_norm, v_l1_w_in, v_l1_log_dt, v_l1_a_re, v_l1_a_im, v_l1_b_re, v_l1_b_im, v_l1_c_re, v_l1_c_im, v_l1_d, v_l1_w_glu, v_l1_b_glu, v_l1_ffn_norm, v_l1_w_up, v_l1_ffn_conv_w, v_l1_ffn_conv_b, v_l1_w_down, v_final_norm):
    a = dict(locals())
    w = {n: a[n] for n in [s for s, _ in SHARDED] + list(REPLICATED)}
    mom = {n: a["m_" + n] for n in w}
    var = {n: a["v_" + n] for n in w}
    return _step(a["x"][0], a["loss_target"][0], w, mom, var)


def _gather_weights(w):
    cc = lax.axis_index("c")
    big = [n for n, _ in SHARDED[:10]]
    small = [n for n, _ in SHARDED[10:]]
    full = {}
    for names, dtype, mult in ((big, BF16, 2 * 256 * PACK_COLS), (small, F32, 2 * 8 * LANES)):
        cols = PACK_COLS if dtype == BF16 else LANES
        flat = _pack([w[n] for n in names], dtype, mult)
        half = lax.dynamic_index_in_dim(flat.reshape(2, -1, cols), cc, axis=0, keepdims=False)
        got = all_gather8(half, "gather_" + ("matrices" if dtype == BF16 else "conv_weights"))
        got = got.reshape(N_CHIPS, -1)
        shapes = [w[n].shape for n in names]
        per_chip = [_unpack(got[j], shapes) for j in range(N_CHIPS)]
        for k, n in enumerate(names):
            axis = dict(SHARDED)[n]
            full[n] = jnp.concatenate([per_chip[j][k] for j in range(N_CHIPS)], axis=axis)
    return full


def _reduce_sharded(grads):
    cc = lax.axis_index("c")
    names = [n for n, _ in SHARDED]
    axes = dict(SHARDED)
    mult = 2 * 256 * PACK_COLS
    packs = [_pack([_shard(grads[n], axes[n], j) for n in names], BF16, mult) for j in range(N_CHIPS)]
    g = jnp.stack(packs).reshape(N_CHIPS, 2, -1, PACK_COLS)
    keep = lax.dynamic_index_in_dim(g, cc, axis=1, keepdims=False)
    give = lax.dynamic_index_in_dim(g, 1 - cc, axis=1, keepdims=False)
    got = sibling_swap(give, "grad_swap_halves")
    parts = add_to_bf16(keep, got, "grad_add_sibling")
    landed = chip_exchange(parts, "grad_chip_exchange")
    mine = sum_leading(landed, "grad_sum_chips")
    theirs = sibling_swap(mine, "grad_swap_sums")
    lo = jnp.where(cc == 0, mine, theirs)
    hi = jnp.where(cc == 0, theirs, mine)
    flat = jnp.concatenate([lo.reshape(-1), hi.reshape(-1)])
    shapes = [_shard(grads[n], axes[n], 0).shape for n in names]
    return dict(zip(names, _unpack(flat, shapes)))


def _reduce_replicated(grads):
    names = list(REPLICATED)
    flat = _pack([grads[n] for n in names], F32, 256 * LANES).reshape(-1, LANES)
    got = all_gather8(flat, "gather_small_grads")
    tot = sum_leading(got, "sum_small_grads").reshape(-1)
    return dict(zip(names, _unpack(tot, [grads[n].shape for n in names]))), flat.shape


def _row(v):
    return v.reshape(1, -1).astype(F32)


def _pad_rows(wt, rows):
    return jnp.pad(wt.astype(F32), ((0, rows - wt.shape[0]), (0, 0)))


def _ffn_fwd(xin, g, wa, wb, cw, cb, wd, tag):
    cwa, cwb = _pad_rows(cw[:, :D_FF], 8), _pad_rows(cw[:, D_FF:], 8)
    xout, xn, hpa, hpb, act = ffn_fwd(xin, _row(g), wa, wb, cwa, cwb, _row(cb[:D_FF]), _row(cb[D_FF:]), wd, tag)
    return xout, (xin, xn, hpa, hpb, act)


def _ffn_bwd(dxout, saved, g, wa, wb, cw, cb, wd, tag):
    xin, xn, hpa, hpb, act = saved
    d_wd = matmul(act, dxout, ta=True, name=f"{tag}_d_wdown")
    cwa, cwb = _pad_rows(cw[:, :D_FF], 8), _pad_rows(cw[:, D_FF:], 8)
    wra, wrb = _pad_rows(cw[::-1, :D_FF], 8), _pad_rows(cw[::-1, D_FF:], 8)
    dxn, dpa, dpb, dwa, dwb, dba, dbb = ffn_bwd(dxout, hpa, hpb, wa, wb, cwa, cwb, wra, wrb,
                                                _row(cb[:D_FF]), _row(cb[D_FF:]), wd, tag + "_bwd")
    d_wu = jnp.concatenate([matmul(xn, dpa, ta=True, name=f"{tag}_d_wup_a"),
                            matmul(xn, dpb, ta=True, name=f"{tag}_d_wup_b")], axis=1)
    dxin, dg = rms_bwd(xin, _row(g), dxn, dxout, f"{tag}_rms_bwd")
    taps = lambda t: t.transpose(1, 0, 2).reshape(8, -1)
    d_cw = jnp.concatenate([taps(dwa)[:FFN_K], taps(dwb)[:FFN_K]], axis=1)
    d_cb = jnp.concatenate([taps(dba)[0], taps(dbb)[0]])
    return dxin, dg[0], d_wu, d_cw, d_cb, d_wd


def _step(x, target, w, mom, var):
    s = x.shape[0]
    full = _gather_weights(w)
    cos, sin = rope_tables(s)

    w_in0 = full["l0_w_in"]
    w_in0p = jnp.concatenate([w_in0, jnp.zeros((D_MODEL, H0_W - w_in0.shape[1]), BF16)], axis=1)
    wq = full["l0_w_uq"].reshape(Q_LORA, N_HEADS, QK_NOPE + QK_ROPE)
    zq = lambda n: jnp.zeros((Q_LORA, N_HEADS, n), BF16)
    w_uqp = jnp.concatenate([wq[..., :QK_NOPE], zq(LANES - QK_NOPE), wq[..., QK_NOPE:], zq(LANES - QK_ROPE)],
                            axis=-1).reshape(Q_LORA, N_HEADS * HEAD_PAD)
    w_ukv = full["l0_w_ukv"]
    w_out = full["l0_w_out"]
    w_out_u = w_out[:CONV_WIDTH]
    wo = w_out[CONV_WIDTH:].reshape(N_HEADS, V_DIM, D_MODEL)
    w_out_a = jnp.concatenate([jnp.zeros_like(wo), wo], axis=1).reshape(N_HEADS * LANES, D_MODEL)
    conv_w = _pad_rows(full["l0_conv_w"], CONV_HALO)
    conv_wrev = _pad_rows(full["l0_conv_w"][::-1], CONV_HALO)
    w_up0a, w_up0b = full["l0_w_up"][:, :D_FF], full["l0_w_up"][:, D_FF:]
    w_up1a, w_up1b = full["l1_w_up"][:, :D_FF], full["l1_w_up"][:, D_FF:]

    xn0 = rms_fwd(x, _row(w["l0_mix_norm"]), "l0_mix_rms")
    h0 = matmul(xn0, w_in0p, name="l0_in_proj")
    qn_g, kvn_g = _row(w["l0_q_norm"]), _row(w["l0_kv_norm"])
    u0, cq, ckv, kr = mixpre_fwd(h0, qn_g, kvn_g, cos, sin)
    cb, lg, lb = _row(w["l0_conv_b"]), _row(w["l0_conv_ln_g"]), _row(w["l0_conv_ln_b"])
    u = convln_fwd(u0, conv_w, cb, lg, lb)
    qraw = matmul(cq, w_uqp, name="l0_q_up")
    q = qrope_fwd(qraw, cos, sin)
    kv = matmul(ckv, w_ukv, out_dtype=BF16, name="l0_kv_up")
    o, lse = attn_fwd(q, kv, kr)
    x1 = matmul(u, w_out_u, res=x, name="l0_out_conv")
    x1 = matmul(o, w_out_a, res=x1, name="l0_out_attn")

    x2, ffn0 = _ffn_fwd(x1, w["l0_ffn_norm"], w_up0a, w_up0b, full["l0_ffn_conv_w"], w["l0_ffn_conv_b"],
                        full["l0_w_down"], "l0_ffn")

    g_, p_, c_ = SSM_GROUPS, SSM_STATE, SSM_GROUP
    s5_in = (w["l1_log_dt"].reshape(g_, 1), w["l1_a_re"], w["l1_a_im"],
             w["l1_b_re"].reshape(g_, p_ * c_), w["l1_b_im"].reshape(g_, p_ * c_))
    lam_r, lam_i, bb_r, bb_i = s5_params_fwd(*s5_in)
    lam_rf, lam_if = lam_r.reshape(1, NS), lam_i.reshape(1, NS)

    def b_blocks(bb):
        t = bb.reshape(NQ, 8, p_, c_).transpose(0, 1, 3, 2)
        return _block_diag(t).astype(BF16)

    def c_blocks(cm):
        t = cm.reshape(NQ, 8, c_, p_).transpose(0, 1, 3, 2)
        return _block_diag(t).astype(BF16)

    bre, bim = b_blocks(bb_r), b_blocks(bb_i)
    cre, cim = c_blocks(w["l1_c_re"]), c_blocks(w["l1_c_im"])
    dskip = _row(w["l1_d"])
    xn2 = rms_fwd(x2, _row(w["l1_mix_norm"]), "l1_mix_rms")
    u1 = matmul(xn2, full["l1_w_in"], name="l1_in_proj")
    xs_r, xs_i, y1, yg = s5_scan_fwd(u1, lam_rf, lam_if, bre, bim, cre, cim, dskip)
    z = matmul(yg, full["l1_w_glu"], bias=_row(w["l1_b_glu"]), out_dtype=BF16, name="l1_glu_proj")
    x3 = glu_res_fwd(z, x2)

    x4, ffn1 = _ffn_fwd(x3, w["l1_ffn_norm"], w_up1a, w_up1b, full["l1_ffn_conv_w"], w["l1_ffn_conv_b"],
                        full["l1_w_down"], "l1_ffn")
    loss_part, dx4, dgf = loss_head(x4, _row(w["final_norm"]), target)
    loss = lax.psum(loss_part[0, 0], ("x", "y", "c"))

    gr = {"final_norm": dgf[0]}

    dx3, gr["l1_ffn_norm"], gr["l1_w_up"], gr["l1_ffn_conv_w"], gr["l1_ffn_conv_b"], gr["l1_w_down"] = _ffn_bwd(
        dx4, ffn1, w["l1_ffn_norm"], w_up1a, w_up1b, full["l1_ffn_conv_w"], w["l1_ffn_conv_b"], full["l1_w_down"],
        "l1_ffn")

    dz, dbga, dbgb = glu_bwd(z, dx3)
    gr["l1_b_glu"] = jnp.concatenate([dbga[0], dbgb[0]])
    dyg = matmul(dz, full["l1_w_glu"], tb=True, name="l1_d_yg")
    gr["l1_w_glu"] = matmul(yg, dz, ta=True, name="l1_d_wglu")
    a_r, a_i, dy1 = s5_scan_bwd(dyg, y1, lam_rf, lam_if, cre, cim)
    du1, dlr, dli, dbr, dbi, dcr, dci, dd = s5_grads(u1, dy1, xs_r, xs_i, a_r, a_i, bre, bim, dskip)
    gr["l1_d"] = dd[0]

    def b_unblock(d):
        return _block_diag_t(d, c_, p_).transpose(0, 1, 3, 2).reshape(g_, p_ * c_)

    def c_unblock(d):
        return _block_diag_t(d, p_, c_).transpose(0, 1, 3, 2).reshape(g_, c_, p_)

    gr["l1_c_re"], gr["l1_c_im"] = c_unblock(dcr), c_unblock(dci)
    dld, dar, dai, dbre, dbim = s5_params_bwd(*s5_in, dlr[0].reshape(g_, p_), dli[0].reshape(g_, p_),
                                              b_unblock(dbr), b_unblock(dbi))
    gr["l1_log_dt"], gr["l1_a_re"], gr["l1_a_im"] = dld.reshape(g_), dar, dai
    gr["l1_b_re"], gr["l1_b_im"] = dbre.reshape(g_, p_, c_), dbim.reshape(g_, p_, c_)
    dxn2 = matmul(du1, full["l1_w_in"], tb=True, name="l1_d_xn")
    gr["l1_w_in"] = matmul(xn2, du1, ta=True, name="l1_d_win")
    dx2, dg = rms_bwd(x2, _row(w["l1_mix_norm"]), dxn2, dx3, "l1_mix_rms_bwd")
    gr["l1_mix_norm"] = dg[0]

    dx1, gr["l0_ffn_norm"], gr["l0_w_up"], gr["l0_ffn_conv_w"], gr["l0_ffn_conv_b"], gr["l0_w_down"] = _ffn_bwd(
        dx2, ffn0, w["l0_ffn_norm"], w_up0a, w_up0b, full["l0_ffn_conv_w"], w["l0_ffn_conv_b"], full["l0_w_down"],
        "l0_ffn")

    du = matmul(dx1, w_out_u, tb=True, out_dtype=BF16, name="l0_d_u")
    do = matmul(dx1, w_out_a, tb=True, out_dtype=BF16, name="l0_d_o")
    d_wout_u = matmul(u, dx1, ta=True, name="l0_d_wout_u")
    d_wout_a = matmul(o, dx1, ta=True, name="l0_d_wout_a")
    gr["l0_w_out"] = jnp.concatenate(
        [d_wout_u, d_wout_a.reshape(N_HEADS, LANES, D_MODEL)[:, LANES - V_DIM:].reshape(N_HEADS * V_DIM, D_MODEL)])
    dq, dkv, dkr = attn_bwd(q, kv, kr, o, do, lse)
    dqraw = qrope_bwd(dq, cos, sin)
    dcq = matmul(dqraw, w_uqp, tb=True, name="l0_d_cq")
    d_wuqp = matmul(cq, dqraw, ta=True, name="l0_d_wuq").reshape(Q_LORA, N_HEADS, HEAD_PAD)
    gr["l0_w_uq"] = jnp.concatenate([d_wuqp[..., :QK_NOPE], d_wuqp[..., LANES:LANES + QK_ROPE]],
                                    axis=-1).reshape(Q_LORA, -1)
    dckv = matmul(dkv, w_ukv, tb=True, name="l0_d_ckv")
    gr["l0_w_ukv"] = matmul(ckv, dkv, ta=True, name="l0_d_wukv")
    du1c, dlg, dlb, dcb = convln_bwd1(u0, conv_w, cb, lg, lb, du)
    gr["l0_conv_ln_g"], gr["l0_conv_ln_b"], gr["l0_conv_b"] = dlg[0], dlb[0], dcb[0]
    du0, dcw = convln_bwd2(u0, conv_wrev, du1c)
    gr["l0_conv_w"] = dcw[:CONV_K]
    dh0, dqn, dkvn = mixpre_bwd(h0, qn_g, kvn_g, cos, sin, du0, dcq, dckv, dkr)
    gr["l0_q_norm"], gr["l0_kv_norm"] = dqn[0], dkvn[0]
    dxn0 = matmul(dh0, w_in0p, tb=True, name="l0_d_xn")
    gr["l0_w_in"] = matmul(xn0, dh0, ta=True, name="l0_d_win")[:, :w_in0.shape[1]]
    grad_x, dg = rms_bwd(x, _row(w["l0_mix_norm"]), dxn0, dx1, "l0_mix_rms_bwd")
    gr["l0_mix_norm"] = dg[0]

    g_sh = _reduce_sharded(gr)
    g_rep, pack_shape = _reduce_replicated(gr)
    grad, delta, new_m, new_v = {}, {}, {}, {}
    for n, _ in SHARDED:
        shp = w[n].shape
        two_d = (lambda t: t.reshape(shp[0], -1))
        grad[n] = g_sh[n]
        delta[n], new_m[n], new_v[n] = adamw(two_d(w[n]), two_d(g_sh[n]), two_d(mom[n]), two_d(var[n]), f"adamw_{n}")
    names = list(REPLICATED)
    pk = lambda d: _pack([d[n] for n in names], F32, 256 * LANES).reshape(pack_shape)
    dl, nm, nv = adamw(pk(w), pk(g_rep), pk(mom), pk(var), "adamw_small")
    shapes = [w[n].shape for n in names]
    for n, d_, m_, v_ in zip(names, _unpack(dl.reshape(-1), shapes), _unpack(nm.reshape(-1), shapes),
                             _unpack(nv.reshape(-1), shapes)):
        grad[n], delta[n], new_m[n], new_v[n] = g_rep[n], d_, m_, v_

    order = ["l0_mix_norm", "l0_w_in", "l0_conv_w", "l0_conv_b", "l0_conv_ln_g", "l0_conv_ln_b", "l0_q_norm",
             "l0_kv_norm", "l0_w_uq", "l0_w_ukv", "l0_w_out", "l0_ffn_norm", "l0_w_up", "l0_ffn_conv_w",
             "l0_ffn_conv_b", "l0_w_down", "l1_mix_norm", "l1_w_in", "l1_log_dt", "l1_a_re", "l1_a_im", "l1_b_re",
             "l1_b_im", "l1_c_re", "l1_c_im", "l1_d", "l1_w_glu", "l1_b_glu", "l1_ffn_norm", "l1_w_up",
             "l1_ffn_conv_w", "l1_ffn_conv_b", "l1_w_down", "final_norm"]
    return (loss, grad_x[None], *[grad[n] for n in order], *[delta[n] for n in order],
            *[new_m[n] for n in order], *[new_v[n] for n in order])
```

```python
import functools
import math

import jax
import jax.numpy as jnp
import numpy as np
from jax import lax
from jax.experimental import pallas as pl
from jax.experimental.pallas import tpu as pltpu

F32 = jnp.float32
BF16 = jnp.bfloat16
MESH = pl.DeviceIdType.MESH

D_MODEL = 1024
EPS = 1e-6
LN_EPS = 1e-5
CONV_WIDTH = 512
CONV_K = 31
N_HEADS = 8
QK_NOPE = 64
QK_ROPE = 32
V_DIM = 64
Q_LORA = 256
KV_LORA = 128
ROPE_BASE = 10000.0
ATT_SCALE = (QK_NOPE + QK_ROPE) ** -0.5
SSM_WIDTH = 512
SSM_GROUP = 16
SSM_GROUPS = 32
SSM_STATE = 64
D_FF = 2816
FFN_K = 3
ADAM_LR = 0.001
ADAM_B1 = 0.9
ADAM_B2 = 0.999
ADAM_EPS = 1e-08
ADAM_WD = 0.01
ADAM_STEP = 10

N_CHIPS = 4
LANES = 128
HEAD_PAD = 256
CONV_HALO = 32
FFN_HALO = 16
VMEM_LIMIT = 56 * 1024 * 1024

ROW_TILE = 512
FFN_ROW_TILE = 1024
FFN_COL_TILE = 256
FFN_ROW_CHUNK = 64
CONV_ROW_CHUNK = 32
ATT_TILE = 1024
SCAN_TILE = 256
SCAN_UNROLL = 4


def _cparams(*sem):
    return pltpu.CompilerParams(dimension_semantics=tuple(sem), vmem_limit_bytes=VMEM_LIMIT)


def _pick(n, cands):
    for c in cands:
        if n % c == 0:
            return c
    return n


def matmul(a, b, *, ta=False, tb=False, res=None, bias=None, out_dtype=None, name):
    if out_dtype is None:
        out_dtype = BF16 if ta else F32
    if ta:
        kdim, m = a.shape
    else:
        m, kdim = a.shape
    if tb:
        n, k2 = b.shape
    else:
        k2, n = b.shape
    assert kdim == k2, (a.shape, b.shape, ta, tb)
    tn = _pick(n, (1408, 1024, 768, 512, 384, 256, 128))
    if ta:
        tm = _pick(m, (1408, 1024, 512, 256, 128))
        tk = _pick(kdim, (512, 256, 128))
    else:
        tm = _pick(m, (1024, 512, 256, 128))
        tk = kdim
        if kdim > 1024:
            tn = _pick(n, (512, 256, 128))
        if tm * tn > 1024 * 1024 and out_dtype == F32:
            tm = _pick(m, (512, 256, 128))
    nk = kdim // tk
    has_res, has_bias = res is not None, bias is not None
    dims = (((0,) if ta else (1,), (1,) if tb else (0,)), ((), ()))

    def body(*refs):
        a_ref, b_ref = refs[0], refs[1]
        pos = 2
        res_ref = bias_ref = None
        if has_res:
            res_ref = refs[pos]
            pos += 1
        if has_bias:
            bias_ref = refs[pos]
            pos += 1
        o_ref = refs[pos]

        def finish(r):
            if has_bias:
                r = r + bias_ref[...]
            if has_res:
                r = r + res_ref[...].astype(F32)
            o_ref[...] = r.astype(o_ref.dtype)

        prod = lax.dot_general(a_ref[...].astype(BF16), b_ref[...].astype(BF16), dims, preferred_element_type=F32)
        if nk == 1:
            finish(prod)
            return
        acc_ref = refs[pos + 1]
        k = pl.program_id(2)

        @pl.when(k == 0)
        def _():
            acc_ref[...] = prod

        @pl.when(k > 0)
        def _():
            acc_ref[...] += prod

        @pl.when(k == nk - 1)
        def _():
            finish(acc_ref[...])

    a_spec = pl.BlockSpec((tk, tm), lambda i, j, k: (k, i)) if ta else pl.BlockSpec((tm, tk), lambda i, j, k: (i, k))
    b_spec = pl.BlockSpec((tn, tk), lambda i, j, k: (j, k)) if tb else pl.BlockSpec((tk, tn), lambda i, j, k: (k, j))
    in_specs = [a_spec, b_spec]
    args = [a, b]
    if has_res:
        in_specs.append(pl.BlockSpec((tm, tn), lambda i, j, k: (i, j)))
        args.append(res)
    if has_bias:
        in_specs.append(pl.BlockSpec((1, tn), lambda i, j, k: (0, j)))
        args.append(bias)
    return pl.pallas_call(
        body, name=name, grid=(m // tm, n // tn, nk),
        in_specs=in_specs, out_specs=pl.BlockSpec((tm, tn), lambda i, j, k: (i, j)),
        out_shape=jax.ShapeDtypeStruct((m, n), out_dtype),
        scratch_shapes=[pltpu.VMEM((tm, tn), F32)] if nk > 1 else [],
        compiler_params=_cparams("parallel", "parallel", "arbitrary"),
    )(*args)


def rowcall(body, *, rows, ts, ins, outs, name, scratch=()):
    nt = rows // ts
    in_specs, args = [], []
    for arr, kind in ins:
        if kind == "row":
            in_specs.append(pl.BlockSpec((ts, arr.shape[1]), lambda i: (i, 0)))
        elif kind == "rev":
            in_specs.append(pl.BlockSpec((ts, arr.shape[1]), lambda i: (nt - 1 - i, 0)))
        elif kind == "full":
            nd = arr.ndim
            in_specs.append(pl.BlockSpec(arr.shape, lambda i, nd=nd: (0,) * nd))
        elif kind.startswith("prev:"):
            h = int(kind[5:])
            r = ts // h
            in_specs.append(pl.BlockSpec((h, arr.shape[1]), lambda i, r=r: (jnp.maximum(i * r - 1, 0), 0)))
        elif kind.startswith("next:"):
            h = int(kind[5:])
            r = ts // h
            last = rows // h - 1
            in_specs.append(pl.BlockSpec((h, arr.shape[1]), lambda i, r=r, last=last: (jnp.minimum((i + 1) * r, last), 0)))
        elif kind.startswith("revprev:"):
            h = int(kind[8:])
            r = ts // h
            in_specs.append(pl.BlockSpec((h, arr.shape[1]), lambda i, r=r: (jnp.maximum((nt - 1 - i) * r - 1, 0), 0)))
        else:
            raise ValueError(kind)
        args.append(arr)
    out_specs, out_shapes = [], []
    for shape, dtype, kind in outs:
        if kind == "row":
            out_specs.append(pl.BlockSpec((ts, shape[1]), lambda i: (i, 0)))
        elif kind == "rev":
            out_specs.append(pl.BlockSpec((ts, shape[1]), lambda i: (nt - 1 - i, 0)))
        else:
            nd = len(shape)
            out_specs.append(pl.BlockSpec(tuple(shape), lambda i, nd=nd: (0,) * nd))
        out_shapes.append(jax.ShapeDtypeStruct(tuple(shape), dtype))
    return pl.pallas_call(
        functools.partial(body, nt), name=name, grid=(nt,),
        in_specs=in_specs, out_specs=tuple(out_specs), out_shape=tuple(out_shapes),
        scratch_shapes=list(scratch),
        compiler_params=_cparams("arbitrary"),
    )(*args)


def _rms(x, g):
    return x * lax.rsqrt(jnp.mean(x * x, axis=-1, keepdims=True) + EPS) * g


def _layer_norm(x, g, b):
    mu = jnp.mean(x, axis=-1, keepdims=True)
    xc = x - mu
    var = jnp.mean(xc * xc, axis=-1, keepdims=True)
    return xc * lax.rsqrt(var + LN_EPS) * g + b


def _sigmoid(x):
    return 1.0 / (1.0 + jnp.exp(-x))


def _silu(x):
    return x * _sigmoid(x)


def _gelu(x):
    return 0.5 * x * (1.0 + jnp.tanh(math.sqrt(2.0 / math.pi) * (x + 0.044715 * (x * x * x))))


def _acc(ref, i, val):
    s = jnp.sum(val, axis=0, keepdims=True)

    @pl.when(i == 0)
    def _():
        ref[...] = jnp.zeros_like(ref)

    ref[...] += jnp.broadcast_to(s, ref.shape)


def rms_fwd(x, g, name):
    s, c = x.shape

    def body(nt, x_ref, g_ref, o_ref):
        o_ref[...] = _rms(x_ref[...], g_ref[...]).astype(BF16)

    return rowcall(body, rows=s, ts=min(ROW_TILE, s), ins=[(x, "row"), (g, "full")],
                   outs=[((s, c), BF16, "row")], name=name)[0]


def rms_bwd(x, g, dxn, dres, name):
    s, c = x.shape

    def body(nt, x_ref, g_ref, d_ref, r_ref, dx_ref, dg_ref):
        i = pl.program_id(0)
        _, vjp = jax.vjp(_rms, x_ref[...], g_ref[...])
        dx, dg = vjp(d_ref[...].astype(F32))
        dx_ref[...] = dx + r_ref[...]
        _acc(dg_ref, i, dg)

    return rowcall(body, rows=s, ts=min(ROW_TILE, s),
                   ins=[(x, "row"), (g, "full"), (dxn, "row"), (dres, "row")],
                   outs=[((s, c), F32, "row"), ((8, c), F32, "acc")], name=name)


def _partner(t):
    lane = lax.broadcasted_iota(jnp.int32, t.shape, 1)
    half = QK_ROPE // 2
    return jnp.where(lane % QK_ROPE < half, pltpu.roll(t, LANES - half, 1), pltpu.roll(t, half, 1))


def _rope(t, cos, sin):
    return t * cos + _partner(t) * sin


def _rope_t(d, cos, sin):
    return d * cos + _partner(d * sin)


def rope_tables(s):
    half = QK_ROPE // 2
    inv = ROPE_BASE ** (-jnp.arange(half, dtype=F32) / half)
    ang = jnp.arange(s).astype(F32)[:, None] * inv[None, :]
    cos, sin = jnp.cos(ang), jnp.sin(ang)
    z = jnp.zeros((s, LANES - QK_ROPE), F32)
    return jnp.concatenate([cos, cos, z], axis=1), jnp.concatenate([-sin, sin, z], axis=1)


H0_A, H0_G, H0_Q, H0_KV, H0_KR, H0_W = 0, 512, 1024, 1280, 1408, 1536


def _mixpre_fn(a, g, q, kv, qn, kvn):
    return a * _sigmoid(g), _rms(q, qn), _rms(kv, kvn)


def _h0_parts(h_ref):
    return (h_ref[:, H0_A:H0_G], h_ref[:, H0_G:H0_Q], h_ref[:, H0_Q:H0_KV], h_ref[:, H0_KV:H0_KR])


def mixpre_fwd(h0, qn, kvn, cos, sin):
    s = h0.shape[0]

    def body(nt, h_ref, qn_ref, kvn_ref, cos_ref, sin_ref, u0_ref, cq_ref, ckv_ref, kr_ref):
        u0, cq, ckv = _mixpre_fn(*_h0_parts(h_ref), qn_ref[...], kvn_ref[...])
        u0_ref[...] = u0
        cq_ref[...] = cq.astype(BF16)
        ckv_ref[...] = ckv.astype(BF16)
        kr_ref[...] = _rope(h_ref[:, H0_KR:H0_W], cos_ref[...], sin_ref[...]).astype(BF16)

    return rowcall(body, rows=s, ts=min(ROW_TILE, s),
                   ins=[(h0, "row"), (qn, "full"), (kvn, "full"), (cos, "row"), (sin, "row")],
                   outs=[((s, CONV_WIDTH), F32, "row"), ((s, Q_LORA), BF16, "row"),
                         ((s, KV_LORA), BF16, "row"), ((s, LANES), BF16, "row")], name="mixpre_fwd")


def mixpre_bwd(h0, qn, kvn, cos, sin, du0, dcq, dckv, dkr):
    s = h0.shape[0]

    def body(nt, h_ref, qn_ref, kvn_ref, cos_ref, sin_ref, du0_ref, dcq_ref, dckv_ref, dkr_ref,
             dh_ref, dqn_ref, dkvn_ref):
        i = pl.program_id(0)
        _, vjp = jax.vjp(_mixpre_fn, *_h0_parts(h_ref), qn_ref[...], kvn_ref[...])
        da, dg, dq, dkv, dqn, dkvn = vjp((du0_ref[...], dcq_ref[...], dckv_ref[...]))
        dh_ref[:, H0_A:H0_G] = da.astype(BF16)
        dh_ref[:, H0_G:H0_Q] = dg.astype(BF16)
        dh_ref[:, H0_Q:H0_KV] = dq.astype(BF16)
        dh_ref[:, H0_KV:H0_KR] = dkv.astype(BF16)
        dkr = dkr_ref[:, :LANES]
        for h in range(1, N_HEADS):
            dkr = dkr + dkr_ref[:, h * LANES:(h + 1) * LANES]
        dh_ref[:, H0_KR:H0_W] = _rope_t(dkr, cos_ref[...], sin_ref[...]).astype(BF16)
        _acc(dqn_ref, i, dqn)
        _acc(dkvn_ref, i, dkvn)

    return rowcall(body, rows=s, ts=min(ROW_TILE, s),
                   ins=[(h0, "row"), (qn, "full"), (kvn, "full"), (cos, "row"), (sin, "row"),
                        (du0, "row"), (dcq, "row"), (dckv, "row"), (dkr, "row")],
                   outs=[((s, H0_W), BF16, "row"), ((8, Q_LORA), F32, "acc"), ((8, KV_LORA), F32, "acc")],
                   name="mixpre_bwd")


def _conv_taps(ext_ref, w_ref, ts, first, ntaps, flip=False):
    acc = None
    for k in range(ntaps):
        term = w_ref[pl.ds(ntaps - 1 - k if flip else k, 1), :] * ext_ref[pl.ds(first + k, ts), :]
        acc = term if acc is None else acc + term
    return acc


def _ln_silu(u1, g, b):
    return _silu(_layer_norm(u1, g, b))


SUBLANES = 8


def _fill_shifted(sh_ref, parts, rows):
    pos = 0
    for p in parts:
        sh_ref[0, pl.ds(pos, p.shape[0]), :] = p
        pos += p.shape[0]
    sh_ref[0, pl.ds(rows, SUBLANES), :] = jnp.zeros((SUBLANES, sh_ref.shape[2]), F32)
    for r in range(1, SUBLANES):
        sh_ref[r, pl.ds(0, rows), :] = sh_ref[0, pl.ds(r, rows), :]


def _window(sh_ref, off, n):
    r = off % SUBLANES
    return sh_ref[r, pl.ds(off - r, n), :]


def _taps_aligned(sh_ref, w_ref, n, first, ntaps, flip=False):
    acc = None
    for k in range(ntaps):
        term = w_ref[pl.ds(ntaps - 1 - k if flip else k, 1), :] * _window(sh_ref, first + k, n)
        acc = term if acc is None else acc + term
    return acc


def _conv_scratch(ts, c):
    return pltpu.VMEM((SUBLANES, ts + CONV_HALO + SUBLANES, c), F32)


def convln_fwd(u0, w, b, lg, lb):
    s, c = u0.shape
    ts = min(ROW_TILE, s)
    rc = min(CONV_ROW_CHUNK, ts)
    first = CONV_HALO - (CONV_K - 1)

    def body(nt, cur_ref, prev_ref, w_ref, b_ref, lg_ref, lb_ref, o_ref, sh_ref):
        i = pl.program_id(0)
        _fill_shifted(sh_ref, [jnp.where(i > 0, prev_ref[...], 0.0), cur_ref[...]], ts + CONV_HALO)
        for r0 in range(0, ts, rc):
            u1 = _taps_aligned(sh_ref, w_ref, rc, first + r0, CONV_K) + b_ref[...]
            o_ref[pl.ds(r0, rc), :] = _ln_silu(u1, lg_ref[...], lb_ref[...]).astype(BF16)

    return rowcall(body, rows=s, ts=ts,
                   ins=[(u0, "row"), (u0, f"prev:{CONV_HALO}"), (w, "full"), (b, "full"), (lg, "full"), (lb, "full")],
                   outs=[((s, c), BF16, "row")], name="convln_fwd", scratch=[_conv_scratch(ts, c)])[0]


def convln_bwd1(u0, w, b, lg, lb, du):
    s, c = u0.shape
    ts = min(ROW_TILE, s)
    rc = min(CONV_ROW_CHUNK, ts)
    first = CONV_HALO - (CONV_K - 1)

    def body(nt, cur_ref, prev_ref, w_ref, b_ref, lg_ref, lb_ref, du_ref, du1_ref, dlg_ref, dlb_ref, dcb_ref, sh_ref):
        i = pl.program_id(0)
        _fill_shifted(sh_ref, [jnp.where(i > 0, prev_ref[...], 0.0), cur_ref[...]], ts + CONV_HALO)
        sums = [jnp.zeros((1, c), F32)] * 3
        for r0 in range(0, ts, rc):
            u1 = _taps_aligned(sh_ref, w_ref, rc, first + r0, CONV_K) + b_ref[...]
            _, vjp = jax.vjp(_ln_silu, u1, lg_ref[...], lb_ref[...])
            du1, dlg, dlb = vjp(du_ref[pl.ds(r0, rc), :].astype(F32))
            du1_ref[pl.ds(r0, rc), :] = du1
            parts = (dlg, dlb, jnp.sum(du1, axis=0, keepdims=True))
            sums = [a + jnp.sum(p, axis=0, keepdims=True) for a, p in zip(sums, parts)]
        _acc(dlg_ref, i, sums[0])
        _acc(dlb_ref, i, sums[1])
        _acc(dcb_ref, i, sums[2])

    return rowcall(body, rows=s, ts=ts,
                   ins=[(u0, "row"), (u0, f"prev:{CONV_HALO}"), (w, "full"), (b, "full"), (lg, "full"), (lb, "full"),
                        (du, "row")],
                   outs=[((s, c), F32, "row"), ((8, c), F32, "acc"), ((8, c), F32, "acc"), ((8, c), F32, "acc")],
                   name="convln_bwd1", scratch=[_conv_scratch(ts, c)])


def convln_bwd2(u0, w, du1):
    s, c = u0.shape
    ts = min(ROW_TILE, s)
    rc = min(CONV_ROW_CHUNK, ts)
    first = CONV_HALO - (CONV_K - 1)

    def body(nt, cur_ref, prev_ref, d_ref, dnext_ref, w_ref, du0_ref, dw_ref, sh_ref, dsh_ref):
        i = pl.program_id(0)
        _fill_shifted(sh_ref, [jnp.where(i > 0, prev_ref[...], 0.0), cur_ref[...]], ts + CONV_HALO)
        _fill_shifted(dsh_ref, [d_ref[...], jnp.where(i < nt - 1, dnext_ref[...], 0.0)], ts + CONV_HALO)
        for r0 in range(0, ts, rc):
            du0_ref[pl.ds(r0, rc), :] = _taps_aligned(dsh_ref, w_ref, rc, r0, CONV_K, flip=True)

        @pl.when(i == 0)
        def _():
            dw_ref[...] = jnp.zeros_like(dw_ref)

        for k in range(CONV_K):
            part = jnp.zeros((SUBLANES, c), F32)
            for r0 in range(0, ts, rc):
                prod = d_ref[pl.ds(r0, rc), :] * _window(sh_ref, first + k + r0, rc)
                for a in range(0, rc, SUBLANES):
                    part = part + prod[a:a + SUBLANES]
            dw_ref[pl.ds(k, 1), :] += jnp.sum(part, axis=0, keepdims=True)

    return rowcall(body, rows=s, ts=ts,
                   ins=[(u0, "row"), (u0, f"prev:{CONV_HALO}"), (du1, "row"), (du1, f"next:{CONV_HALO}"), (w, "full")],
                   outs=[((s, c), F32, "row"), ((CONV_HALO, c), F32, "acc")], name="convln_bwd2",
                   scratch=[_conv_scratch(ts, c), _conv_scratch(ts, c)])


def qrope_fwd(qraw, cos, sin):
    s = qraw.shape[0]

    def body(nt, q_ref, cos_ref, sin_ref, o_ref):
        cos_v, sin_v = cos_ref[...] * ATT_SCALE, sin_ref[...] * ATT_SCALE
        for h in range(N_HEADS):
            nope = q_ref[:, h * HEAD_PAD:h * HEAD_PAD + LANES] * ATT_SCALE
            o_ref[:, h * HEAD_PAD:h * HEAD_PAD + LANES] = nope.astype(BF16)
            r = q_ref[:, h * HEAD_PAD + LANES:(h + 1) * HEAD_PAD]
            o_ref[:, h * HEAD_PAD + LANES:(h + 1) * HEAD_PAD] = _rope(r, cos_v, sin_v).astype(BF16)

    return rowcall(body, rows=s, ts=min(ROW_TILE, s), ins=[(qraw, "row"), (cos, "row"), (sin, "row")],
                   outs=[((s, N_HEADS * HEAD_PAD), BF16, "row")], name="qrope_fwd")[0]


def qrope_bwd(dq, cos, sin):
    s = dq.shape[0]

    def body(nt, d_ref, cos_ref, sin_ref, o_ref):
        cos_v, sin_v = cos_ref[...] * ATT_SCALE, sin_ref[...] * ATT_SCALE
        for h in range(N_HEADS):
            nope = d_ref[:, h * HEAD_PAD:h * HEAD_PAD + LANES] * ATT_SCALE
            o_ref[:, h * HEAD_PAD:h * HEAD_PAD + LANES] = nope.astype(BF16)
            r = d_ref[:, h * HEAD_PAD + LANES:(h + 1) * HEAD_PAD].astype(F32)
            o_ref[:, h * HEAD_PAD + LANES:(h + 1) * HEAD_PAD] = _rope_t(r, cos_v, sin_v).astype(BF16)

    return rowcall(body, rows=s, ts=min(ROW_TILE, s), ins=[(dq, "row"), (cos, "row"), (sin, "row")],
                   outs=[((s, N_HEADS * HEAD_PAD), BF16, "row")], name="qrope_bwd")[0]


_NT = (((1,), (1,)), ((), ()))
_TN = (((0,), (0,)), ((), ()))


def _scores(q, kvr, diagonal):
    s = lax.dot_general(q, kvr, _NT, preferred_element_type=F32)
    if not diagonal:
        return s
    row = lax.broadcasted_iota(jnp.int32, s.shape, 0)
    col = lax.broadcasted_iota(jnp.int32, s.shape, 1)
    return jnp.where(col <= row, s, -jnp.inf)


def _on_causal_pairs(pair, k_blk, fn):
    @pl.when(k_blk < 2 * pair)
    def _():
        fn(0, False)
        fn(1, False)

    @pl.when(k_blk == 2 * pair)
    def _():
        fn(0, True)
        fn(1, False)

    @pl.when(k_blk == 2 * pair + 1)
    def _():
        fn(1, True)


def attn_fwd(q, kv, kr):
    s = q.shape[0]
    t = min(ATT_TILE, s // 2)
    n = s // t
    np_ = n // 2

    def body(q_ref, kv_ref, kr_ref, o_ref, lse_ref, m_ref, l_ref, acc_ref):
        i, j = pl.program_id(1), pl.program_id(2)

        @pl.when(j == 0)
        def _():
            m_ref[...] = jnp.full_like(m_ref, -jnp.inf)
            l_ref[...] = jnp.zeros_like(l_ref)
            acc_ref[...] = jnp.zeros_like(acc_ref)

        def block(sub, diagonal):
            kvv = kv_ref[...]
            kvr = jnp.concatenate([kvv, kr_ref[...]], axis=1)
            sc = _scores(q_ref[pl.ds(sub * t, t), :], kvr, diagonal)
            m_prev = m_ref[sub]
            m_new = jnp.maximum(m_prev, jnp.max(sc, axis=-1, keepdims=True))
            alpha = jnp.exp(m_prev - m_new)
            p = jnp.exp(sc - m_new)
            l_ref[sub] = alpha * l_ref[sub] + jnp.sum(p, axis=-1, keepdims=True)
            acc_ref[sub] = alpha * acc_ref[sub] + jnp.dot(p.astype(BF16), kvv, preferred_element_type=F32)
            m_ref[sub] = m_new

        _on_causal_pairs(i, j, block)

        @pl.when(j == 2 * i + 1)
        def _():
            for sub in range(2):
                l = l_ref[sub]
                o_ref[pl.ds(sub * t, t), :] = (acc_ref[sub] / l).astype(BF16)
                lse_ref[pl.ds(sub * t, t), :] = jnp.broadcast_to(m_ref[sub] + jnp.log(l), (t, LANES))

    kj = lambda h, i, j: (jnp.minimum(j, 2 * i + 1), h)
    return pl.pallas_call(
        body, name="attn_fwd", grid=(N_HEADS, np_, n),
        in_specs=[pl.BlockSpec((2 * t, HEAD_PAD), lambda h, i, j: (i, h)),
                  pl.BlockSpec((t, LANES), kj),
                  pl.BlockSpec((t, LANES), lambda h, i, j: (jnp.minimum(j, 2 * i + 1), 0))],
        out_specs=(pl.BlockSpec((2 * t, LANES), lambda h, i, j: (i, h)),
                   pl.BlockSpec((2 * t, LANES), lambda h, i, j: (i, h))),
        out_shape=(jax.ShapeDtypeStruct((s, N_HEADS * LANES), BF16),
                   jax.ShapeDtypeStruct((s, N_HEADS * LANES), F32)),
        scratch_shapes=[pltpu.VMEM((2, t, 1), F32), pltpu.VMEM((2, t, 1), F32), pltpu.VMEM((2, t, LANES), F32)],
        compiler_params=_cparams("parallel", "parallel", "arbitrary"),
    )(q, kv, kr)


def attn_bwd(q, kv, kr, o, do, lse):
    s = q.shape[0]
    t = min(ATT_TILE, s // 2)
    n = s // t
    np_ = n // 2

    def body(q_ref, kv_ref, kr_ref, o_ref, do_ref, lse_ref, dq_ref, dkv_ref, dkr_ref):
        j, i = pl.program_id(1), pl.program_id(2)

        @pl.when((i == 0) & (j == 0))
        def _():
            dq_ref[...] = jnp.zeros_like(dq_ref)

        @pl.when(i == 0)
        def _():
            dkv_ref[...] = jnp.zeros_like(dkv_ref)
            dkr_ref[...] = jnp.zeros_like(dkr_ref)

        def block(sub, diagonal):
            sl = pl.ds(sub * t, t)
            qv, dov, kvv = q_ref[sl, :], do_ref[sl, :], kv_ref[...]
            kvr = jnp.concatenate([kvv, kr_ref[...]], axis=1)
            p = jnp.exp(_scores(qv, kvr, diagonal) - lse_ref[sl, :1])
            dp = lax.dot_general(dov, kvv, _NT, preferred_element_type=F32)
            delta = jnp.sum(dov.astype(F32) * o_ref[sl, :].astype(F32), axis=-1, keepdims=True)
            ds = (p * (dp - delta)).astype(BF16)
            dk = lax.dot_general(ds, qv, _TN, preferred_element_type=F32)
            dkv_ref[...] += lax.dot_general(p.astype(BF16), dov, _TN, preferred_element_type=F32) + dk[:, :LANES]
            dkr_ref[...] += dk[:, LANES:]
            rows = pl.ds(pl.multiple_of((2 * i + sub) * t, t), t)
            dq_ref[rows, :] += jnp.dot(ds, kvr, preferred_element_type=F32)

        _on_causal_pairs(i, j, block)

    qi = lambda h, j, i: (jnp.maximum(i, lax.div(j, 2)), h)
    kj = lambda h, j, i: (j, h)
    return pl.pallas_call(
        body, name="attn_bwd", grid=(N_HEADS, n, np_),
        in_specs=[pl.BlockSpec((2 * t, HEAD_PAD), qi), pl.BlockSpec((t, LANES), kj),
                  pl.BlockSpec((t, LANES), lambda h, j, i: (j, 0)),
                  pl.BlockSpec((2 * t, LANES), qi), pl.BlockSpec((2 * t, LANES), qi), pl.BlockSpec((2 * t, LANES), qi)],
        out_specs=(pl.BlockSpec((s, HEAD_PAD), lambda h, j, i: (0, h)),
                   pl.BlockSpec((t, LANES), kj), pl.BlockSpec((t, LANES), kj)),
        out_shape=(jax.ShapeDtypeStruct((s, N_HEADS * HEAD_PAD), F32),
                   jax.ShapeDtypeStruct((s, N_HEADS * LANES), F32), jax.ShapeDtypeStruct((s, N_HEADS * LANES), F32)),
        compiler_params=_cparams("parallel", "arbitrary", "arbitrary"),
    )(q, kv, kr, o, do, lse)


def ffn_fwd(x, g, wa, wb, cwa, cwb, ba, bb, wd, name):
    s, d = x.shape
    f = wa.shape[1]
    ts, tf = min(FFN_ROW_TILE, s), FFN_COL_TILE
    hal = FFN_HALO
    nj = f // tf
    first = hal - (FFN_K - 1)
    rc = min(FFN_ROW_CHUNK, ts)

    def body(x_ref, xp_ref, g_ref, wa_ref, wb_ref, cwa_ref, cwb_ref, ba_ref, bb_ref, wd_ref,
             xo_ref, xn_ref, hpa_ref, hpb_ref, act_ref, xe_ref, ea_ref, eb_ref):
        i, j = pl.program_id(0), pl.program_id(1)

        @pl.when(j == 0)
        def _():
            xn = _rms(x_ref[...], g_ref[...]).astype(BF16)
            xn_ref[...] = xn
            xe_ref[pl.ds(hal, ts), :] = xn
            xe_ref[pl.ds(0, hal), :] = jnp.where(i > 0, _rms(xp_ref[...], g_ref[...]), 0.0).astype(BF16)

        xe = xe_ref[...]
        ea_ref[...] = jnp.dot(xe, wa_ref[...], preferred_element_type=F32)
        eb_ref[...] = jnp.dot(xe, wb_ref[...], preferred_element_type=F32)
        hpa_ref[...] = ea_ref[pl.ds(hal, ts), :].astype(BF16)
        hpb_ref[...] = eb_ref[pl.ds(hal, ts), :].astype(BF16)
        for r0 in range(0, ts, rc):
            ha = _conv_taps(ea_ref, cwa_ref, rc, first + r0, FFN_K) + ba_ref[...]
            hb = _conv_taps(eb_ref, cwb_ref, rc, first + r0, FFN_K) + bb_ref[...]
            act_ref[pl.ds(r0, rc), :] = (_silu(ha) * hb).astype(BF16)
        y = jnp.dot(act_ref[...], wd_ref[...], preferred_element_type=F32)

        @pl.when(j == 0)
        def _():
            xo_ref[...] = x_ref[...] + y

        @pl.when(j > 0)
        def _():
            xo_ref[...] += y

    r = ts // hal
    row = pl.BlockSpec((ts, d), lambda i, j: (i, 0))
    prev = pl.BlockSpec((hal, d), lambda i, j: (jnp.maximum(i * r - 1, 0), 0))
    gsp = pl.BlockSpec((1, d), lambda i, j: (0, 0))
    wup = pl.BlockSpec((d, tf), lambda i, j: (0, j))
    cwsp = pl.BlockSpec((8, tf), lambda i, j: (0, j))
    bsp = pl.BlockSpec((1, tf), lambda i, j: (0, j))
    wdn = pl.BlockSpec((tf, d), lambda i, j: (j, 0))
    hid = pl.BlockSpec((ts, tf), lambda i, j: (i, j))
    return pl.pallas_call(
        body, name=name, grid=(s // ts, nj),
        in_specs=[row, prev, gsp, wup, wup, cwsp, cwsp, bsp, bsp, wdn],
        out_specs=(row, row, hid, hid, hid),
        out_shape=(jax.ShapeDtypeStruct((s, d), F32), jax.ShapeDtypeStruct((s, d), BF16),
                   jax.ShapeDtypeStruct((s, f), BF16), jax.ShapeDtypeStruct((s, f), BF16),
                   jax.ShapeDtypeStruct((s, f), BF16)),
        scratch_shapes=[pltpu.VMEM((ts + hal, d), BF16), pltpu.VMEM((ts + hal, tf), F32),
                        pltpu.VMEM((ts + hal, tf), F32)],
        compiler_params=_cparams("parallel", "arbitrary"),
    )(x, x, g, wa, wb, cwa, cwb, ba, bb, wd)


def ffn_bwd(dy, hpa, hpb, wa, wb, cwa, cwb, ba, bb, wd, name):
    s, d = dy.shape
    f = hpa.shape[1]
    ts, tf = min(FFN_ROW_TILE, s), FFN_COL_TILE
    hal = FFN_HALO
    nt, nj = s // ts, f // tf
    te = ts + hal
    first = hal - (FFN_K - 1)
    rc = min(FFN_ROW_CHUNK, ts)

    def body(dy_ref, dyn_ref, a_ref, ap_ref, an_ref, b_ref, bp_ref, bn_ref, wa_ref, wb_ref, cwa_ref, cwb_ref,
             ba_ref, bb_ref, wd_ref,
             dxn_ref, dpa_ref, dpb_ref, dwa_ref, dwb_ref, dba_ref, dbb_ref,
             dye_ref, ea_ref, eb_ref, dact_ref, da_ref, db_ref):
        i, j = pl.program_id(0), pl.program_id(1)
        last = i == nt - 1

        @pl.when(j == 0)
        def _():
            dye_ref[pl.ds(0, ts), :] = dy_ref[...].astype(BF16)
            dye_ref[pl.ds(ts, hal), :] = jnp.where(last, 0.0, dyn_ref[...]).astype(BF16)

        @pl.when((i == 0) & (j == 0))
        def _():
            for r in (dwa_ref, dwb_ref, dba_ref, dbb_ref):
                r[...] = jnp.zeros_like(r)

        dact_ref[...] = lax.dot_general(dye_ref[...], wd_ref[...], _NT, preferred_element_type=F32)
        for cur, prev, nxt, ext in ((a_ref, ap_ref, an_ref, ea_ref), (b_ref, bp_ref, bn_ref, eb_ref)):
            ext[pl.ds(0, hal), :] = jnp.where(i > 0, prev[...].astype(F32), 0.0)
            ext[pl.ds(hal, ts), :] = cur[...].astype(F32)
            ext[pl.ds(hal + ts, hal), :] = jnp.where(last, 0.0, nxt[...].astype(F32))
        zero = jnp.zeros((1, tf), F32)
        sums = {"ba": zero, "bb": zero, **{("a", k): zero for k in range(FFN_K)}, **{("b", k): zero for k in range(FFN_K)}}
        for r0 in list(range(0, ts, rc)) + [ts]:
            n = rc if r0 < ts else hal
            win_a = [ea_ref[pl.ds(first + r0 + k, n), :] for k in range(FFN_K)]
            win_b = [eb_ref[pl.ds(first + r0 + k, n), :] for k in range(FFN_K)]
            ha = sum(cwa_ref[pl.ds(k, 1), :] * win_a[k] for k in range(FFN_K)) + ba_ref[...]
            hb = sum(cwb_ref[pl.ds(k, 1), :] * win_b[k] for k in range(FFN_K)) + bb_ref[...]
            sig = _sigmoid(ha)
            gs = dact_ref[pl.ds(r0, n), :] * sig
            dha = gs * hb * (1.0 + ha * (1.0 - sig))
            dhb = gs * ha
            da_ref[pl.ds(r0, n), :] = dha
            db_ref[pl.ds(r0, n), :] = dhb
            if r0 < ts:
                sums["ba"] = sums["ba"] + jnp.sum(dha, axis=0, keepdims=True)
                sums["bb"] = sums["bb"] + jnp.sum(dhb, axis=0, keepdims=True)
                for k in range(FFN_K):
                    sums["a", k] = sums["a", k] + jnp.sum(dha * win_a[k], axis=0, keepdims=True)
                    sums["b", k] = sums["b", k] + jnp.sum(dhb * win_b[k], axis=0, keepdims=True)
        for r0 in range(0, ts, rc):
            dpa_ref[pl.ds(r0, rc), :] = _conv_taps(da_ref, cwa_ref, rc, r0, FFN_K, flip=True).astype(BF16)
            dpb_ref[pl.ds(r0, rc), :] = _conv_taps(db_ref, cwb_ref, rc, r0, FFN_K, flip=True).astype(BF16)
        dba_ref[j] += jnp.broadcast_to(sums["ba"], (8, tf))
        dbb_ref[j] += jnp.broadcast_to(sums["bb"], (8, tf))
        row = lax.broadcasted_iota(jnp.int32, (8, tf), 0)
        dwa_ref[j] += sum(jnp.where(row == k, sums["a", k], 0.0) for k in range(FFN_K))
        dwb_ref[j] += sum(jnp.where(row == k, sums["b", k], 0.0) for k in range(FFN_K))
        dxn = (lax.dot_general(dpa_ref[...], wa_ref[...], _NT, preferred_element_type=F32)
               + lax.dot_general(dpb_ref[...], wb_ref[...], _NT, preferred_element_type=F32))

        @pl.when(j == 0)
        def _():
            dxn_ref[...] = dxn

        @pl.when(j > 0)
        def _():
            dxn_ref[...] += dxn

    r = ts // hal
    lastblk = s // hal - 1
    row = pl.BlockSpec((ts, d), lambda i, j: (i, 0))
    rown = pl.BlockSpec((hal, d), lambda i, j: (jnp.minimum((i + 1) * r, lastblk), 0))
    cur = pl.BlockSpec((ts, tf), lambda i, j: (i, j))
    prev = pl.BlockSpec((hal, tf), lambda i, j: (jnp.maximum(i * r - 1, 0), j))
    nxt = pl.BlockSpec((hal, tf), lambda i, j: (jnp.minimum((i + 1) * r, lastblk), j))
    wup = pl.BlockSpec((d, tf), lambda i, j: (0, j))
    cwsp = pl.BlockSpec((8, tf), lambda i, j: (0, j))
    bsp = pl.BlockSpec((1, tf), lambda i, j: (0, j))
    wdn = pl.BlockSpec((tf, d), lambda i, j: (j, 0))
    accsp = pl.BlockSpec((nj, 8, tf), lambda i, j: (0, 0, 0))
    accshape = jax.ShapeDtypeStruct((nj, 8, tf), F32)
    return pl.pallas_call(
        body, name=name, grid=(nt, nj),
        in_specs=[row, rown, cur, prev, nxt, cur, prev, nxt, wup, wup, cwsp, cwsp, bsp, bsp, wdn],
        out_specs=(row, cur, cur, accsp, accsp, accsp, accsp),
        out_shape=(jax.ShapeDtypeStruct((s, d), F32), jax.ShapeDtypeStruct((s, f), BF16),
                   jax.ShapeDtypeStruct((s, f), BF16), accshape, accshape, accshape, accshape),
        scratch_shapes=[pltpu.VMEM((te, d), BF16), pltpu.VMEM((ts + 2 * hal, tf), F32),
                        pltpu.VMEM((ts + 2 * hal, tf), F32), pltpu.VMEM((te, tf), F32),
                        pltpu.VMEM((te, tf), F32), pltpu.VMEM((te, tf), F32)],
        compiler_params=_cparams("arbitrary", "arbitrary"),
    )(dy, dy, hpa, hpa, hpa, hpb, hpb, hpb, wa, wb, cwa, cwb, ba, bb, wd)


NQ = 4
SQ = SSM_STATE * 8
NS = SSM_GROUPS * SSM_STATE


def _s5_disc(log_dt, a_re, a_im, b_re, b_im, expand):
    dt = jnp.exp(log_dt)
    mag = jnp.exp(a_re * dt)
    lb_re, lb_im = mag * jnp.cos(a_im * dt), mag * jnp.sin(a_im * dt)
    den = a_re * a_re + a_im * a_im
    nr, ni = lb_re - 1.0, lb_im
    f_re = (nr * a_re + ni * a_im) / den
    f_im = (ni * a_re - nr * a_im) / den
    fe_re = jnp.dot(f_re, expand, precision=lax.Precision.HIGHEST, preferred_element_type=F32)
    fe_im = jnp.dot(f_im, expand, precision=lax.Precision.HIGHEST, preferred_element_type=F32)
    return lb_re, lb_im, fe_re * b_re - fe_im * b_im, fe_re * b_im + fe_im * b_re


def _expand_matrix():
    e = np.zeros((SSM_STATE, SSM_STATE * SSM_GROUP), np.float32)
    for p in range(SSM_STATE):
        e[p, p * SSM_GROUP:(p + 1) * SSM_GROUP] = 1.0
    return jnp.asarray(e)


def s5_params_fwd(log_dt, a_re, a_im, b_re, b_im):
    expand = _expand_matrix()

    def body(ld_ref, ar_ref, ai_ref, br_ref, bi_ref, e_ref, lr_ref, li_ref, bbr_ref, bbi_ref):
        lr, li, bbr, bbi = _s5_disc(ld_ref[...], ar_ref[...], ai_ref[...], br_ref[...], bi_ref[...], e_ref[...])
        lr_ref[...] = lr
        li_ref[...] = li
        bbr_ref[...] = bbr
        bbi_ref[...] = bbi

    g, p, pc = SSM_GROUPS, SSM_STATE, SSM_STATE * SSM_GROUP
    return pl.pallas_call(
        body, name="s5_params_fwd",
        out_shape=(jax.ShapeDtypeStruct((g, p), F32), jax.ShapeDtypeStruct((g, p), F32),
                   jax.ShapeDtypeStruct((g, pc), F32), jax.ShapeDtypeStruct((g, pc), F32)),
    )(log_dt, a_re, a_im, b_re, b_im, expand)


def s5_params_bwd(log_dt, a_re, a_im, b_re, b_im, dlr, dli, dbbr, dbbi):
    expand = _expand_matrix()

    def body(ld_ref, ar_ref, ai_ref, br_ref, bi_ref, e_ref, dlr_ref, dli_ref, dbbr_ref, dbbi_ref,
             dld_ref, dar_ref, dai_ref, dbr_ref, dbi_ref):
        e = e_ref[...]
        f = lambda ld, ar, ai, br, bi: _s5_disc(ld, ar, ai, br, bi, e)
        _, vjp = jax.vjp(f, ld_ref[...], ar_ref[...], ai_ref[...], br_ref[...], bi_ref[...])
        dld, dar, dai, dbr, dbi = vjp((dlr_ref[...], dli_ref[...], dbbr_ref[...], dbbi_ref[...]))
        dld_ref[...] = dld
        dar_ref[...] = dar
        dai_ref[...] = dai
        dbr_ref[...] = dbr
        dbi_ref[...] = dbi

    g, p, pc = SSM_GROUPS, SSM_STATE, SSM_STATE * SSM_GROUP
    return pl.pallas_call(
        body, name="s5_params_bwd",
        out_shape=(jax.ShapeDtypeStruct((g, 1), F32), jax.ShapeDtypeStruct((g, p), F32),
                   jax.ShapeDtypeStruct((g, p), F32), jax.ShapeDtypeStruct((g, pc), F32),
                   jax.ShapeDtypeStruct((g, pc), F32)),
    )(log_dt, a_re, a_im, b_re, b_im, expand, dlr, dli, dbbr, dbbi)


def _cmul(ar, ai, br, bi):
    return ar * br - ai * bi, ar * bi + ai * br


def _power_rows(lr, li, conj_rev):
    row = lax.broadcasted_iota(jnp.int32, (8, NS), 0)
    tr = jnp.zeros((8, NS), F32)
    ti = jnp.zeros((8, NS), F32)
    pr, pi = lr, li
    for r in range(8):
        dst = 7 - r if conj_rev else r
        tr = jnp.where(row == dst, pr, tr)
        ti = jnp.where(row == dst, -pi if conj_rev else pi, ti)
        if r < 7:
            pr, pi = _cmul(pr, pi, lr, li)
    return tr, ti


def _scan8(xr, xi, tr_ref, ti_ref, cr, ci, reverse):
    row = lax.broadcasted_iota(jnp.int32, xr.shape, 0)
    for d in (1, 2, 4):
        if reverse:
            sr, si = pltpu.roll(xr, 8 - d, 0), pltpu.roll(xi, 8 - d, 0)
            keep = row < 8 - d
            pw = 8 - d
        else:
            sr, si = pltpu.roll(xr, d, 0), pltpu.roll(xi, d, 0)
            keep = row >= d
            pw = d - 1
        mr, mi = _cmul(tr_ref[pl.ds(pw, 1), :], ti_ref[pl.ds(pw, 1), :], sr, si)
        xr = xr + jnp.where(keep, mr, 0.0)
        xi = xi + jnp.where(keep, mi, 0.0)
    mr, mi = _cmul(tr_ref[...], ti_ref[...], cr, ci)
    return xr + mr, xi + mi


def _row_of(x, r):
    row = lax.broadcasted_iota(jnp.int32, x.shape, 0)
    return jnp.sum(jnp.where(row == r, x, 0.0), axis=0, keepdims=True)


def s5_scan_fwd(u, lam_r, lam_i, bre, bim, cre, cim, dskip):
    s = u.shape[0]
    tt = min(SCAN_TILE, s)
    nb = tt // 8

    def body(nt, u_ref, lr_ref, li_ref, bre_ref, bim_ref, cre_ref, cim_ref, d_ref,
             xr_ref, xi_ref, y_ref, yg_ref, tr_ref, ti_ref, cr_ref, ci_ref):
        i = pl.program_id(0)

        @pl.when(i == 0)
        def _():
            tr, ti = _power_rows(lr_ref[...], li_ref[...], False)
            tr_ref[...] = tr
            ti_ref[...] = ti
            cr_ref[...] = jnp.zeros_like(cr_ref)
            ci_ref[...] = jnp.zeros_like(ci_ref)

        uv = u_ref[...]
        ub = uv.astype(BF16)
        for q in range(NQ):
            uq = ub[:, q * LANES:(q + 1) * LANES]
            xr_ref[:, q * SQ:(q + 1) * SQ] = jnp.dot(uq, bre_ref[q], preferred_element_type=F32)
            xi_ref[:, q * SQ:(q + 1) * SQ] = jnp.dot(uq, bim_ref[q], preferred_element_type=F32)

        def step(b, carry):
            cr, ci = carry
            rows = pl.ds(pl.multiple_of(b * 8, 8), 8)
            xr, xi = _scan8(xr_ref[rows, :], xi_ref[rows, :], tr_ref, ti_ref, cr, ci, False)
            xr_ref[rows, :] = xr
            xi_ref[rows, :] = xi
            return _row_of(xr, 7), _row_of(xi, 7)

        cr, ci = lax.fori_loop(0, nb, step, (cr_ref[...], ci_ref[...]), unroll=min(SCAN_UNROLL, nb))
        cr_ref[...] = cr
        ci_ref[...] = ci
        y = d_ref[...] * uv
        for q in range(NQ):
            yq = (jnp.dot(xr_ref[:, q * SQ:(q + 1) * SQ].astype(BF16), cre_ref[q], preferred_element_type=F32)
                  - jnp.dot(xi_ref[:, q * SQ:(q + 1) * SQ].astype(BF16), cim_ref[q], preferred_element_type=F32))
            y_ref[:, q * LANES:(q + 1) * LANES] = yq + y[:, q * LANES:(q + 1) * LANES]
        yg_ref[...] = _gelu(y_ref[...]).astype(BF16)

    return rowcall(body, rows=s, ts=tt,
                   ins=[(u, "row"), (lam_r, "full"), (lam_i, "full"), (bre, "full"), (bim, "full"),
                        (cre, "full"), (cim, "full"), (dskip, "full")],
                   outs=[((s, NS), F32, "row"), ((s, NS), F32, "row"), ((s, SSM_WIDTH), F32, "row"),
                         ((s, SSM_WIDTH), BF16, "row")], name="s5_scan_fwd",
                   scratch=[pltpu.VMEM((8, NS), F32), pltpu.VMEM((8, NS), F32),
                            pltpu.VMEM((1, NS), F32), pltpu.VMEM((1, NS), F32)])


def s5_scan_bwd(dyg, y, lam_r, lam_i, cre, cim):
    s = y.shape[0]
    tt = min(SCAN_TILE, s)
    nb = tt // 8

    def body(nt, dyg_ref, y_ref, lr_ref, li_ref, cre_ref, cim_ref,
             ar_ref, ai_ref, dy_ref, tr_ref, ti_ref, cr_ref, ci_ref):
        i = pl.program_id(0)

        @pl.when(i == 0)
        def _():
            tr, ti = _power_rows(lr_ref[...], li_ref[...], True)
            tr_ref[...] = tr
            ti_ref[...] = ti
            cr_ref[...] = jnp.zeros_like(cr_ref)
            ci_ref[...] = jnp.zeros_like(ci_ref)

        _, vjp = jax.vjp(_gelu, y_ref[...])
        dy = vjp(dyg_ref[...])[0]
        dyb = dy.astype(BF16)
        dy_ref[...] = dyb
        for q in range(NQ):
            dq = dyb[:, q * LANES:(q + 1) * LANES]
            ar_ref[:, q * SQ:(q + 1) * SQ] = lax.dot_general(dq, cre_ref[q], _NT, preferred_element_type=F32)
            ai_ref[:, q * SQ:(q + 1) * SQ] = -lax.dot_general(dq, cim_ref[q], _NT, preferred_element_type=F32)

        def step(b, carry):
            cr, ci = carry
            rows = pl.ds(pl.multiple_of((nb - 1 - b) * 8, 8), 8)
            xr, xi = _scan8(ar_ref[rows, :], ai_ref[rows, :], tr_ref, ti_ref, cr, ci, True)
            ar_ref[rows, :] = xr
            ai_ref[rows, :] = xi
            return _row_of(xr, 0), _row_of(xi, 0)

        cr, ci = lax.fori_loop(0, nb, step, (cr_ref[...], ci_ref[...]), unroll=min(SCAN_UNROLL, nb))
        cr_ref[...] = cr
        ci_ref[...] = ci

    return rowcall(body, rows=s, ts=tt,
                   ins=[(dyg, "rev"), (y, "rev"), (lam_r, "full"), (lam_i, "full"), (cre, "full"), (cim, "full")],
                   outs=[((s, NS), F32, "rev"), ((s, NS), F32, "rev"), ((s, SSM_WIDTH), BF16, "rev")],
                   name="s5_scan_bwd",
                   scratch=[pltpu.VMEM((8, NS), F32), pltpu.VMEM((8, NS), F32),
                            pltpu.VMEM((1, NS), F32), pltpu.VMEM((1, NS), F32)])


def s5_grads(u, dy, xr, xi, ar, ai, bre, bim, dskip):
    s = u.shape[0]
    tt = min(SCAN_TILE, s)

    def body(nt, u_ref, dy_ref, xr_ref, xrp_ref, xi_ref, xip_ref, ar_ref, ai_ref, bre_ref, bim_ref, d_ref,
             du_ref, dlr_ref, dli_ref, dbr_ref, dbi_ref, dcr_ref, dci_ref, dd_ref, er_ref, ei_ref):
        i = pl.program_id(0)

        @pl.when(i == 0)
        def _():
            for r in (dbr_ref, dbi_ref, dcr_ref, dci_ref):
                r[...] = jnp.zeros_like(r)

        uv, dyb = u_ref[...], dy_ref[...]
        dyf = dyb.astype(F32)
        av_r, av_i, xv_r, xv_i = ar_ref[...], ai_ref[...], xr_ref[...], xi_ref[...]
        er_ref[pl.ds(0, 8), :] = jnp.where(i > 0, xrp_ref[...], 0.0)
        ei_ref[pl.ds(0, 8), :] = jnp.where(i > 0, xip_ref[...], 0.0)
        er_ref[pl.ds(8, tt), :] = xv_r
        ei_ref[pl.ds(8, tt), :] = xv_i
        sr, si = er_ref[pl.ds(7, tt), :], ei_ref[pl.ds(7, tt), :]
        _acc(dlr_ref, i, av_r * sr + av_i * si)
        _acc(dli_ref, i, av_i * sr - av_r * si)
        _acc(dd_ref, i, dyf * uv)
        ub = uv.astype(BF16)
        ab_r, ab_i = av_r.astype(BF16), av_i.astype(BF16)
        xb_r, xb_i = xv_r.astype(BF16), xv_i.astype(BF16)
        du = d_ref[...] * dyf
        for q in range(NQ):
            cs, ss = slice(q * LANES, (q + 1) * LANES), slice(q * SQ, (q + 1) * SQ)
            dbr_ref[q] += lax.dot_general(ub[:, cs], ab_r[:, ss], _TN, preferred_element_type=F32)
            dbi_ref[q] += lax.dot_general(ub[:, cs], ab_i[:, ss], _TN, preferred_element_type=F32)
            dcr_ref[q] += lax.dot_general(xb_r[:, ss], dyb[:, cs], _TN, preferred_element_type=F32)
            dci_ref[q] -= lax.dot_general(xb_i[:, ss], dyb[:, cs], _TN, preferred_element_type=F32)
            du_ref[:, cs] = (du[:, cs]
                             + lax.dot_general(ab_r[:, ss], bre_ref[q], _NT, preferred_element_type=F32)
                             + lax.dot_general(ab_i[:, ss], bim_ref[q], _NT, preferred_element_type=F32))

    return rowcall(body, rows=s, ts=tt,
                   ins=[(u, "row"), (dy, "row"), (xr, "row"), (xr, "prev:8"), (xi, "row"), (xi, "prev:8"),
                        (ar, "row"), (ai, "row"), (bre, "full"), (bim, "full"), (dskip, "full")],
                   outs=[((s, SSM_WIDTH), F32, "row"), ((8, NS), F32, "acc"), ((8, NS), F32, "acc"),
                         ((NQ, LANES, SQ), F32, "acc"), ((NQ, LANES, SQ), F32, "acc"),
                         ((NQ, SQ, LANES), F32, "acc"), ((NQ, SQ, LANES), F32, "acc"),
                         ((8, SSM_WIDTH), F32, "acc")], name="s5_grads",
                   scratch=[pltpu.VMEM((tt + 8, NS), F32), pltpu.VMEM((tt + 8, NS), F32)])


def _glu_fn(za, zb):
    return za * _sigmoid(zb)


def glu_res_fwd(z, xres):
    s = z.shape[0]

    def body(nt, z_ref, x_ref, o_ref):
        o_ref[...] = x_ref[...] + _glu_fn(z_ref[:, :D_MODEL].astype(F32), z_ref[:, D_MODEL:].astype(F32))

    return rowcall(body, rows=s, ts=min(ROW_TILE, s), ins=[(z, "row"), (xres, "row")],
                   outs=[((s, D_MODEL), F32, "row")], name="glu_res_fwd")[0]


def glu_bwd(z, dout):
    s, c = z.shape

    def body(nt, z_ref, d_ref, dz_ref, dba_ref, dbb_ref):
        i = pl.program_id(0)
        _, vjp = jax.vjp(_glu_fn, z_ref[:, :D_MODEL].astype(F32), z_ref[:, D_MODEL:].astype(F32))
        dza, dzb = vjp(d_ref[...])
        dz_ref[:, :D_MODEL] = dza.astype(BF16)
        dz_ref[:, D_MODEL:] = dzb.astype(BF16)
        _acc(dba_ref, i, dza)
        _acc(dbb_ref, i, dzb)

    return rowcall(body, rows=s, ts=min(ROW_TILE, s), ins=[(z, "row"), (dout, "row")],
                   outs=[((s, c), BF16, "row"), ((8, D_MODEL), F32, "acc"), ((8, D_MODEL), F32, "acc")],
                   name="glu_bwd")


def loss_head(x, g, target):
    s, c = x.shape

    def body(nt, x_ref, g_ref, t_ref, loss_ref, dx_ref, dg_ref):
        i = pl.program_id(0)
        y, vjp = jax.vjp(_rms, x_ref[...], g_ref[...])
        err = y - t_ref[...]
        dx, dg = vjp(err * (1.0 / c))
        dx_ref[...] = dx
        _acc(dg_ref, i, dg)
        part = jnp.sum(jnp.sum(err * err, axis=-1, keepdims=True), axis=0, keepdims=True) * (0.5 / c)

        @pl.when(i == 0)
        def _():
            loss_ref[...] = jnp.zeros_like(loss_ref)

        loss_ref[...] += jnp.broadcast_to(part, loss_ref.shape)

    return rowcall(body, rows=s, ts=min(ROW_TILE, s), ins=[(x, "row"), (g, "full"), (target, "row")],
                   outs=[((8, LANES), F32, "acc"), ((s, c), F32, "row"), ((8, c), F32, "acc")], name="loss_head")


def _tile_rows(r, cands=(512, 256, 128, 64, 32, 16, 8)):
    return _pick(r, cands)


def add_to_bf16(a, b, name):
    n, r, c = a.shape
    tr = _tile_rows(r)

    def body(a_ref, b_ref, o_ref):
        o_ref[...] = (a_ref[...].astype(F32) + b_ref[...].astype(F32)).astype(BF16)

    spec = pl.BlockSpec((1, tr, c), lambda j, i: (j, i, 0))
    return pl.pallas_call(body, name=name, grid=(n, r // tr), in_specs=[spec, spec], out_specs=spec,
                          out_shape=jax.ShapeDtypeStruct((n, r, c), BF16),
                          compiler_params=_cparams("parallel", "parallel"))(a, b)


def sum_leading(a, name):
    n, r, c = a.shape
    tr = _tile_rows(r)

    def body(a_ref, o_ref):
        acc = a_ref[0].astype(F32)
        for k in range(1, n):
            acc = acc + a_ref[k].astype(F32)
        o_ref[...] = acc

    return pl.pallas_call(body, name=name, grid=(r // tr,),
                          in_specs=[pl.BlockSpec((n, tr, c), lambda i: (0, i, 0))],
                          out_specs=pl.BlockSpec((tr, c), lambda i: (i, 0)),
                          out_shape=jax.ShapeDtypeStruct((r, c), F32),
                          compiler_params=_cparams("parallel"))(a)


def adamw(w, g, m, v, name):
    r, c = w.shape
    tr = _tile_rows(r, (256, 128, 64, 32, 16, 8))
    c1 = 1.0 - ADAM_B1 ** ADAM_STEP
    c2 = 1.0 - ADAM_B2 ** ADAM_STEP

    def body(w_ref, g_ref, m_ref, v_ref, d_ref, nm_ref, nv_ref):
        gv = g_ref[...]
        mn = ADAM_B1 * m_ref[...] + (1.0 - ADAM_B1) * gv
        vn = ADAM_B2 * v_ref[...] + (1.0 - ADAM_B2) * (gv * gv)
        d_ref[...] = -ADAM_LR * ((mn / c1) / (jnp.sqrt(vn / c2) + ADAM_EPS) + ADAM_WD * w_ref[...])
        nm_ref[...] = mn
        nv_ref[...] = vn

    spec = pl.BlockSpec((tr, c), lambda i: (i, 0))
    shp = jax.ShapeDtypeStruct((r, c), F32)
    return pl.pallas_call(body, name=name, grid=(r // tr,), in_specs=[spec] * 4, out_specs=(spec,) * 3,
                          out_shape=(shp,) * 3, compiler_params=_cparams("parallel"))(w, g, m, v)


_ANY = pl.BlockSpec(memory_space=pl.ANY)


def all_gather8(block, name):
    r, c = block.shape

    def body(x_ref, out_ref, send_sems, recv_sems, local_sem):
        x, y, cc = lax.axis_index("x"), lax.axis_index("y"), lax.axis_index("c")
        me, sibling = (x, y, cc), (x, y, 1 - cc)
        chips = [(1 - x, y), (x, 1 - y), (1 - x, 1 - y)]

        def slot(px, py, pc):
            return out_ref.at[4 * px + 2 * py + pc]

        def copy(k, blk, to, src=None):
            return pltpu.make_async_remote_copy(
                src_ref=slot(*blk) if src is None else src, dst_ref=slot(*blk),
                send_sem=send_sems.at[k], recv_sem=recv_sems.at[k], device_id=to, device_id_type=MESH)

        mine = pltpu.make_async_copy(x_ref, slot(*me), local_sem)
        mine.start()
        first = [copy(0, me, sibling, src=x_ref)]
        first += [copy(1 + j, me, (*chip, cc), src=x_ref) for j, chip in enumerate(chips)]
        for cp in first:
            cp.start()
        passed = [copy(4 + j, (*chip, cc), sibling) for j, chip in enumerate(chips)]
        for j, chip in enumerate(chips):
            copy(1 + j, (*chip, cc), me).wait_recv()
            passed[j].start()
        copy(0, sibling, me).wait_recv()
        for j, chip in enumerate(chips):
            copy(4 + j, (*chip, 1 - cc), me).wait_recv()
        for cp in first + passed:
            cp.wait_send()
        mine.wait()

    return pl.pallas_call(
        body, name=name, in_specs=[_ANY], out_specs=_ANY,
        out_shape=jax.ShapeDtypeStruct((8, r, c), block.dtype),
        scratch_shapes=[pltpu.SemaphoreType.DMA((7,)), pltpu.SemaphoreType.DMA((7,)), pltpu.SemaphoreType.DMA],
    )(block)


def sibling_swap(block, name):
    def body(x_ref, out_ref, send_sem, recv_sem):
        x, y, cc = lax.axis_index("x"), lax.axis_index("y"), lax.axis_index("c")
        cp = pltpu.make_async_remote_copy(src_ref=x_ref, dst_ref=out_ref, send_sem=send_sem, recv_sem=recv_sem,
                                          device_id=(x, y, 1 - cc), device_id_type=MESH)
        cp.start()
        cp.wait()

    return pl.pallas_call(
        body, name=name, in_specs=[_ANY], out_specs=_ANY,
        out_shape=jax.ShapeDtypeStruct(block.shape, block.dtype),
        scratch_shapes=[pltpu.SemaphoreType.DMA, pltpu.SemaphoreType.DMA],
    )(block)


def chip_exchange(parts, name):
    def body(p_ref, out_ref, send_sems, recv_sems, local_sem):
        x, y, cc = lax.axis_index("x"), lax.axis_index("y"), lax.axis_index("c")
        me = 2 * x + y
        chips = [(1 - x, y), (x, 1 - y), (1 - x, 1 - y)]
        mine = pltpu.make_async_copy(p_ref.at[me], out_ref.at[me], local_sem)
        mine.start()
        sends = []
        for k, (px, py) in enumerate(chips):
            sends.append(pltpu.make_async_remote_copy(
                src_ref=p_ref.at[2 * px + py], dst_ref=out_ref.at[me],
                send_sem=send_sems.at[k], recv_sem=recv_sems.at[k], device_id=(px, py, cc), device_id_type=MESH))
        for cp in sends:
            cp.start()
        for k, (px, py) in enumerate(chips):
            pltpu.make_async_remote_copy(
                src_ref=p_ref.at[me], dst_ref=out_ref.at[2 * px + py],
                send_sem=send_sems.at[k], recv_sem=recv_sems.at[k], device_id=(px, py, cc),
                device_id_type=MESH).wait_recv()
        for cp in sends:
            cp.wait_send()
        mine.wait()

    return pl.pallas_call(
        body, name=name, in_specs=[_ANY], out_specs=_ANY,
        out_shape=jax.ShapeDtypeStruct(parts.shape, parts.dtype),
        scratch_shapes=[pltpu.SemaphoreType.DMA((3,)), pltpu.SemaphoreType.DMA((3,)), pltpu.SemaphoreType.DMA],
    )(parts)


PACK_COLS = 1024
SHARDED = (("l0_w_in", 1), ("l0_w_uq", 1), ("l0_w_ukv", 1), ("l0_w_out", 0), ("l0_w_up", 1), ("l0_w_down", 0),
           ("l1_w_in", 0), ("l1_w_glu", 1), ("l1_w_up", 1), ("l1_w_down", 0),
           ("l0_conv_w", 1), ("l0_ffn_conv_w", 1), ("l1_ffn_conv_w", 1))
REPLICATED = ("l0_mix_norm", "l0_conv_b", "l0_conv_ln_g", "l0_conv_ln_b", "l0_q_norm", "l0_kv_norm", "l0_ffn_norm",
              "l0_ffn_conv_b", "l1_mix_norm", "l1_log_dt", "l1_a_re", "l1_a_im", "l1_b_re", "l1_b_im", "l1_c_re",
              "l1_c_im", "l1_d", "l1_b_glu", "l1_ffn_norm", "l1_ffn_conv_b", "final_norm")


def _pack(arrs, dtype, mult):
    flat = jnp.concatenate([a.reshape(-1).astype(dtype) for a in arrs])
    n = flat.shape[0]
    total = -(-n // mult) * mult
    return jnp.pad(flat, (0, total - n))


def _unpack(flat, shapes):
    out, pos = [], 0
    for shp in shapes:
        n = int(np.prod(shp))
        out.append(flat[pos:pos + n].reshape(shp))
        pos += n
    return out


PACK_ROW_ALIGN = 16


def _pack_rows(arrs, dtype, row_mult):
    parts = []
    for a in arrs:
        n = int(np.prod(a.shape))
        rows = -(-n // PACK_COLS)
        if n % PACK_COLS == 0:
            r = a.astype(dtype).reshape(rows, PACK_COLS)
        else:
            r = jnp.pad(a.reshape(-1).astype(dtype), (0, rows * PACK_COLS - n)).reshape(rows, PACK_COLS)
        parts.append(jnp.pad(r, ((0, (-rows) % PACK_ROW_ALIGN), (0, 0))))
    p = jnp.concatenate(parts)
    return jnp.pad(p, ((0, (-p.shape[0]) % row_mult), (0, 0)))


def _unpack_rows(pack, shapes):
    out, r0 = [], 0
    for shp in shapes:
        n = int(np.prod(shp))
        rows = -(-n // PACK_COLS)
        piece = lax.optimization_barrier(pack[r0:r0 + rows])
        out.append(piece.reshape(shp) if n % PACK_COLS == 0 else piece.reshape(-1)[:n].reshape(shp))
        r0 += rows + (-rows) % PACK_ROW_ALIGN
    return out


def _shard(full, axis, j):
    n = full.shape[axis] // N_CHIPS
    return lax.slice_in_dim(full, j * n, (j + 1) * n, axis=axis)


def _block_diag(t):
    q, g, a, b = t.shape
    eye = jnp.eye(g, dtype=t.dtype)
    return jnp.einsum("qgab,gh->qgahb", t, eye).reshape(q, g * a, g * b)


def _block_diag_t(d, a, b):
    q = d.shape[0]
    d5 = d.reshape(q, 8, a, 8, b)
    eye = jnp.eye(8, dtype=d.dtype)
    return jnp.einsum("qgahb,gh->qgab", d5, eye)


def kernel(x, l0_mix_norm, l0_w_in, l0_conv_w, l0_conv_b, l0_conv_ln_g, l0_conv_ln_b, l0_q_norm, l0_kv_norm, l0_w_uq, l0_w_ukv, l0_w_out, l0_ffn_norm, l0_w_up, l0_ffn_conv_w, l0_ffn_conv_b, l0_w_down, l1_mix_norm, l1_w_in, l1_log_dt, l1_a_re, l1_a_im, l1_b_re, l1_b_im, l1_c_re, l1_c_im, l1_d, l1_w_glu, l1_b_glu, l1_ffn_norm, l1_w_up, l1_ffn_conv_w, l1_ffn_conv_b, l1_w_down, final_norm, loss_target, m_l0_mix_norm, m_l0_w_in, m_l0_conv_w, m_l0_conv_b, m_l0_conv_ln_g, m_l0_conv_ln_b, m_l0_q_norm, m_l0_kv_norm, m_l0_w_uq, m_l0_w_ukv, m_l0_w_out, m_l0_ffn_norm, m_l0_w_up, m_l0_ffn_conv_w, m_l0_ffn_conv_b, m_l0_w_down, m_l1_mix_norm, m_l1_w_in, m_l1_log_dt, m_l1_a_re, m_l1_a_im, m_l1_b_re, m_l1_b_im, m_l1_c_re, m_l1_c_im, m_l1_d, m_l1_w_glu, m_l1_b_glu, m_l1_ffn_norm, m_l1_w_up, m_l1_ffn_conv_w, m_l1_ffn_conv_b, m_l1_w_down, m_final_norm, v_l0_mix_norm, v_l0_w_in, v_l0_conv_w, v_l0_conv_b, v_l0_conv_ln_g, v_l0_conv_ln_b, v_l0_q_norm, v_l0_kv_norm, v_l0_w_uq, v_l0_w_ukv, v_l0_w_out, v_l0_ffn_norm, v_l0_w_up, v_l0_ffn_conv_w, v_l0_ffn_conv_b, v_l0_w_down, v_l1_mix_norm, v_l1_w_in, v_l1_log_dt, v_l1_a_re, v_l1_a_im, v_l1_b_re, v_l1_b_im, v_l1_c_re, v_l1_c_im, v_l1_d, v_l1_w_glu, v_l1_b_glu, v_l1_ffn_norm, v_l1_w_up, v_l1_ffn_conv_w, v_l1_ffn_conv_b, v_l1_w_down, v_final_norm):
    a = dict(locals())
    w = {n: a[n] for n in [s for s, _ in SHARDED] + list(REPLICATED)}
    mom = {n: a["m_" + n] for n in w}
    var = {n: a["v_" + n] for n in w}
    return _step(a["x"][0], a["loss_target"][0], w, mom, var)


def _gather_weights(w):
    cc = lax.axis_index("c")
    big = [n for n, _ in SHARDED[:10]]
    small = [n for n, _ in SHARDED[10:]]
    full = {}
    for names, dtype, mult in ((big, BF16, 2 * 256), (small, F32, 2 * PACK_ROW_ALIGN)):
        pack = _pack_rows([w[n] for n in names], dtype, mult)
        half = lax.dynamic_index_in_dim(pack.reshape(2, -1, PACK_COLS), cc, axis=0, keepdims=False)
        got = all_gather8(half, "gather_" + ("matrices" if dtype == BF16 else "conv_weights"))
        got = got.reshape(N_CHIPS, -1, PACK_COLS)
        shapes = [w[n].shape for n in names]
        per_chip = [_unpack_rows(got[j], shapes) for j in range(N_CHIPS)]
        for k, n in enumerate(names):
            axis = dict(SHARDED)[n]
            full[n] = jnp.concatenate([per_chip[j][k] for j in range(N_CHIPS)], axis=axis)
    return full


def _reduce_sharded(grads):
    cc = lax.axis_index("c")
    names = [n for n, _ in SHARDED]
    axes = dict(SHARDED)
    packs = [_pack_rows([_shard(grads[n], axes[n], j) for n in names], BF16, 2 * 256) for j in range(N_CHIPS)]
    g = jnp.stack(packs).reshape(N_CHIPS, 2, -1, PACK_COLS)
    keep = lax.dynamic_index_in_dim(g, cc, axis=1, keepdims=False)
    give = lax.dynamic_index_in_dim(g, 1 - cc, axis=1, keepdims=False)
    got = sibling_swap(give, "grad_swap_halves")
    parts = add_to_bf16(keep, got, "grad_add_sibling")
    landed = chip_exchange(parts, "grad_chip_exchange")
    mine = sum_leading(landed, "grad_sum_chips")
    theirs = sibling_swap(mine, "grad_swap_sums")
    lo = jnp.where(cc == 0, mine, theirs)
    hi = jnp.where(cc == 0, theirs, mine)
    shapes = [_shard(grads[n], axes[n], 0).shape for n in names]
    return dict(zip(names, _unpack_rows(jnp.concatenate([lo, hi]), shapes)))


def _reduce_replicated(grads):
    names = list(REPLICATED)
    flat = _pack([grads[n] for n in names], F32, 256 * LANES).reshape(-1, LANES)
    got = all_gather8(flat, "gather_small_grads")
    tot = sum_leading(got, "sum_small_grads").reshape(-1)
    return dict(zip(names, _unpack(tot, [grads[n].shape for n in names]))), flat.shape


def _row(v):
    return v.reshape(1, -1).astype(F32)


def _pad_rows(wt, rows):
    return jnp.pad(wt.astype(F32), ((0, rows - wt.shape[0]), (0, 0)))


def _ffn_fwd(xin, g, wa, wb, cw, cb, wd, tag):
    cwa, cwb = _pad_rows(cw[:, :D_FF], 8), _pad_rows(cw[:, D_FF:], 8)
    xout, xn, hpa, hpb, act = ffn_fwd(xin, _row(g), wa, wb, cwa, cwb, _row(cb[:D_FF]), _row(cb[D_FF:]), wd, tag)
    return xout, (xin, xn, hpa, hpb, act)


def _ffn_bwd(dxout, saved, g, wa, wb, cw, cb, wd, tag):
    xin, xn, hpa, hpb, act = saved
    d_wd = matmul(act, dxout, ta=True, name=f"{tag}_d_wdown")
    cwa, cwb = _pad_rows(cw[:, :D_FF], 8), _pad_rows(cw[:, D_FF:], 8)
    dxn, dpa, dpb, dwa, dwb, dba, dbb = ffn_bwd(dxout, hpa, hpb, wa, wb, cwa, cwb,
                                                _row(cb[:D_FF]), _row(cb[D_FF:]), wd, tag + "_bwd")
    d_wu = jnp.concatenate([matmul(xn, dpa, ta=True, name=f"{tag}_d_wup_a"),
                            matmul(xn, dpb, ta=True, name=f"{tag}_d_wup_b")], axis=1)
    dxin, dg = rms_bwd(xin, _row(g), dxn, dxout, f"{tag}_rms_bwd")
    taps = lambda t: t.transpose(1, 0, 2).reshape(8, -1)
    d_cw = jnp.concatenate([taps(dwa)[:FFN_K], taps(dwb)[:FFN_K]], axis=1)
    d_cb = jnp.concatenate([taps(dba)[0], taps(dbb)[0]])
    return dxin, dg[0], d_wu, d_cw, d_cb, d_wd


def _step(x, target, w, mom, var):
    s = x.shape[0]
    full = _gather_weights(w)
    cos, sin = rope_tables(s)

    w_in0 = full["l0_w_in"]
    w_in0p = jnp.concatenate([w_in0, jnp.zeros((D_MODEL, H0_W - w_in0.shape[1]), BF16)], axis=1)
    wq = full["l0_w_uq"].reshape(Q_LORA, N_HEADS, QK_NOPE + QK_ROPE)
    zq = lambda n: jnp.zeros((Q_LORA, N_HEADS, n), BF16)
    w_uqp = jnp.concatenate([wq[..., :QK_NOPE], zq(LANES - QK_NOPE), wq[..., QK_NOPE:], zq(LANES - QK_ROPE)],
                            axis=-1).reshape(Q_LORA, N_HEADS * HEAD_PAD)
    w_ukv = full["l0_w_ukv"]
    w_out = full["l0_w_out"]
    w_out_u = w_out[:CONV_WIDTH]
    wo = w_out[CONV_WIDTH:].reshape(N_HEADS, V_DIM, D_MODEL)
    w_out_a = jnp.concatenate([jnp.zeros_like(wo), wo], axis=1).reshape(N_HEADS * LANES, D_MODEL)
    conv_w = _pad_rows(full["l0_conv_w"], CONV_HALO)
    w_up0a, w_up0b = full["l0_w_up"][:, :D_FF], full["l0_w_up"][:, D_FF:]
    w_up1a, w_up1b = full["l1_w_up"][:, :D_FF], full["l1_w_up"][:, D_FF:]

    xn0 = rms_fwd(x, _row(w["l0_mix_norm"]), "l0_mix_rms")
    h0 = matmul(xn0, w_in0p, name="l0_in_proj")
    qn_g, kvn_g = _row(w["l0_q_norm"]), _row(w["l0_kv_norm"])
    u0, cq, ckv, kr = mixpre_fwd(h0, qn_g, kvn_g, cos, sin)
    cb, lg, lb = _row(w["l0_conv_b"]), _row(w["l0_conv_ln_g"]), _row(w["l0_conv_ln_b"])
    u = convln_fwd(u0, conv_w, cb, lg, lb)
    qraw = matmul(cq, w_uqp, name="l0_q_up")
    q = qrope_fwd(qraw, cos, sin)
    kv = matmul(ckv, w_ukv, out_dtype=BF16, name="l0_kv_up")
    o, lse = attn_fwd(q, kv, kr)
    x1 = matmul(u, w_out_u, res=x, name="l0_out_conv")
    x1 = matmul(o, w_out_a, res=x1, name="l0_out_attn")

    x2, ffn0 = _ffn_fwd(x1, w["l0_ffn_norm"], w_up0a, w_up0b, full["l0_ffn_conv_w"], w["l0_ffn_conv_b"],
                        full["l0_w_down"], "l0_ffn")

    g_, p_, c_ = SSM_GROUPS, SSM_STATE, SSM_GROUP
    s5_in = (w["l1_log_dt"].reshape(g_, 1), w["l1_a_re"], w["l1_a_im"],
             w["l1_b_re"].reshape(g_, p_ * c_), w["l1_b_im"].reshape(g_, p_ * c_))
    lam_r, lam_i, bb_r, bb_i = s5_params_fwd(*s5_in)
    lam_rf, lam_if = lam_r.reshape(1, NS), lam_i.reshape(1, NS)

    def b_blocks(bb):
        t = bb.reshape(NQ, 8, p_, c_).transpose(0, 1, 3, 2)
        return _block_diag(t).astype(BF16)

    def c_blocks(cm):
        t = cm.reshape(NQ, 8, c_, p_).transpose(0, 1, 3, 2)
        return _block_diag(t).astype(BF16)

    bre, bim = b_blocks(bb_r), b_blocks(bb_i)
    cre, cim = c_blocks(w["l1_c_re"]), c_blocks(w["l1_c_im"])
    dskip = _row(w["l1_d"])
    xn2 = rms_fwd(x2, _row(w["l1_mix_norm"]), "l1_mix_rms")
    u1 = matmul(xn2, full["l1_w_in"], name="l1_in_proj")
    xs_r, xs_i, y1, yg = s5_scan_fwd(u1, lam_rf, lam_if, bre, bim, cre, cim, dskip)
    z = matmul(yg, full["l1_w_glu"], bias=_row(w["l1_b_glu"]), out_dtype=BF16, name="l1_glu_proj")
    x3 = glu_res_fwd(z, x2)

    x4, ffn1 = _ffn_fwd(x3, w["l1_ffn_norm"], w_up1a, w_up1b, full["l1_ffn_conv_w"], w["l1_ffn_conv_b"],
                        full["l1_w_down"], "l1_ffn")
    loss_part, dx4, dgf = loss_head(x4, _row(w["final_norm"]), target)
    loss = lax.psum(loss_part[0, 0], ("x", "y", "c"))

    gr = {"final_norm": dgf[0]}

    dx3, gr["l1_ffn_norm"], gr["l1_w_up"], gr["l1_ffn_conv_w"], gr["l1_ffn_conv_b"], gr["l1_w_down"] = _ffn_bwd(
        dx4, ffn1, w["l1_ffn_norm"], w_up1a, w_up1b, full["l1_ffn_conv_w"], w["l1_ffn_conv_b"], full["l1_w_down"],
        "l1_ffn")

    dz, dbga, dbgb = glu_bwd(z, dx3)
    gr["l1_b_glu"] = jnp.concatenate([dbga[0], dbgb[0]])
    dyg = matmul(dz, full["l1_w_glu"], tb=True, name="l1_d_yg")
    gr["l1_w_glu"] = matmul(yg, dz, ta=True, name="l1_d_wglu")
    a_r, a_i, dy1 = s5_scan_bwd(dyg, y1, lam_rf, lam_if, cre, cim)
    du1, dlr, dli, dbr, dbi, dcr, dci, dd = s5_grads(u1, dy1, xs_r, xs_i, a_r, a_i, bre, bim, dskip)
    gr["l1_d"] = dd[0]

    def b_unblock(d):
        return _block_diag_t(d, c_, p_).transpose(0, 1, 3, 2).reshape(g_, p_ * c_)

    def c_unblock(d):
        return _block_diag_t(d, p_, c_).transpose(0, 1, 3, 2).reshape(g_, c_, p_)

    gr["l1_c_re"], gr["l1_c_im"] = c_unblock(dcr), c_unblock(dci)
    dld, dar, dai, dbre, dbim = s5_params_bwd(*s5_in, dlr[0].reshape(g_, p_), dli[0].reshape(g_, p_),
                                              b_unblock(dbr), b_unblock(dbi))
    gr["l1_log_dt"], gr["l1_a_re"], gr["l1_a_im"] = dld.reshape(g_), dar, dai
    gr["l1_b_re"], gr["l1_b_im"] = dbre.reshape(g_, p_, c_), dbim.reshape(g_, p_, c_)
    dxn2 = matmul(du1, full["l1_w_in"], tb=True, name="l1_d_xn")
    gr["l1_w_in"] = matmul(xn2, du1, ta=True, name="l1_d_win")
    dx2, dg = rms_bwd(x2, _row(w["l1_mix_norm"]), dxn2, dx3, "l1_mix_rms_bwd")
    gr["l1_mix_norm"] = dg[0]

    dx1, gr["l0_ffn_norm"], gr["l0_w_up"], gr["l0_ffn_conv_w"], gr["l0_ffn_conv_b"], gr["l0_w_down"] = _ffn_bwd(
        dx2, ffn0, w["l0_ffn_norm"], w_up0a, w_up0b, full["l0_ffn_conv_w"], w["l0_ffn_conv_b"], full["l0_w_down"],
        "l0_ffn")

    du = matmul(dx1, w_out_u, tb=True, out_dtype=BF16, name="l0_d_u")
    do = matmul(dx1, w_out_a, tb=True, out_dtype=BF16, name="l0_d_o")
    d_wout_u = matmul(u, dx1, ta=True, name="l0_d_wout_u")
    d_wout_a = matmul(o, dx1, ta=True, name="l0_d_wout_a")
    gr["l0_w_out"] = jnp.concatenate(
        [d_wout_u, d_wout_a.reshape(N_HEADS, LANES, D_MODEL)[:, LANES - V_DIM:].reshape(N_HEADS * V_DIM, D_MODEL)])
    dq, dkv, dkr = attn_bwd(q, kv, kr, o, do, lse)
    dqraw = qrope_bwd(dq, cos, sin)
    dcq = matmul(dqraw, w_uqp, tb=True, name="l0_d_cq")
    d_wuqp = matmul(cq, dqraw, ta=True, name="l0_d_wuq").reshape(Q_LORA, N_HEADS, HEAD_PAD)
    gr["l0_w_uq"] = jnp.concatenate([d_wuqp[..., :QK_NOPE], d_wuqp[..., LANES:LANES + QK_ROPE]],
                                    axis=-1).reshape(Q_LORA, -1)
    dckv = matmul(dkv, w_ukv, tb=True, name="l0_d_ckv")
    gr["l0_w_ukv"] = matmul(ckv, dkv, ta=True, name="l0_d_wukv")
    du1c, dlg, dlb, dcb = convln_bwd1(u0, conv_w, cb, lg, lb, du)
    gr["l0_conv_ln_g"], gr["l0_conv_ln_b"], gr["l0_conv_b"] = dlg[0], dlb[0], dcb[0]
    du0, dcw = convln_bwd2(u0, conv_w, du1c)
    gr["l0_conv_w"] = dcw[:CONV_K]
    dh0, dqn, dkvn = mixpre_bwd(h0, qn_g, kvn_g, cos, sin, du0, dcq, dckv, dkr)
    gr["l0_q_norm"], gr["l0_kv_norm"] = dqn[0], dkvn[0]
    dxn0 = matmul(dh0, w_in0p, tb=True, name="l0_d_xn")
    gr["l0_w_in"] = matmul(xn0, dh0, ta=True, name="l0_d_win")[:, :w_in0.shape[1]]
    grad_x, dg = rms_bwd(x, _row(w["l0_mix_norm"]), dxn0, dx1, "l0_mix_rms_bwd")
    gr["l0_mix_norm"] = dg[0]

    g_sh = _reduce_sharded(gr)
    g_rep, pack_shape = _reduce_replicated(gr)
    grad, delta, new_m, new_v = {}, {}, {}, {}
    for n, _ in SHARDED:
        shp = w[n].shape
        two_d = (lambda t: t.reshape(shp[0], -1))
        grad[n] = g_sh[n]
        delta[n], new_m[n], new_v[n] = adamw(two_d(w[n]), two_d(g_sh[n]), two_d(mom[n]), two_d(var[n]), f"adamw_{n}")
    names = list(REPLICATED)
    pk = lambda d: _pack([d[n] for n in names], F32, 256 * LANES).reshape(pack_shape)
    dl, nm, nv = adamw(pk(w), pk(g_rep), pk(mom), pk(var), "adamw_small")
    shapes = [w[n].shape for n in names]
    for n, d_, m_, v_ in zip(names, _unpack(dl.reshape(-1), shapes), _unpack(nm.reshape(-1), shapes),
                             _unpack(nv.reshape(-1), shapes)):
        grad[n], delta[n], new_m[n], new_v[n] = g_rep[n], d_, m_, v_

    order = ["l0_mix_norm", "l0_w_in", "l0_conv_w", "l0_conv_b", "l0_conv_ln_g", "l0_conv_ln_b", "l0_q_norm",
             "l0_kv_norm", "l0_w_uq", "l0_w_ukv", "l0_w_out", "l0_ffn_norm", "l0_w_up", "l0_ffn_conv_w",
             "l0_ffn_conv_b", "l0_w_down", "l1_mix_norm", "l1_w_in", "l1_log_dt", "l1_a_re", "l1_a_im", "l1_b_re",
             "l1_b_im", "l1_c_re", "l1_c_im", "l1_d", "l1_w_glu", "l1_b_glu", "l1_ffn_norm", "l1_w_up",
             "l1_ffn_conv_w", "l1_ffn_conv_b", "l1_w_down", "final_norm"]
    return (loss, grad_x[None], *[grad[n] for n in order], *[delta[n] for n in order],
            *[new_m[n] for n in order], *[new_v[n] for n in order])
```

```python
import functools
import math

import jax
import jax.numpy as jnp
import numpy as np
from jax import lax
from jax.experimental import pallas as pl
from jax.experimental.pallas import tpu as pltpu

F32 = jnp.float32
BF16 = jnp.bfloat16
MESH = pl.DeviceIdType.MESH

D_MODEL = 1024
EPS = 1e-6
LN_EPS = 1e-5
CONV_WIDTH = 512
CONV_K = 31
N_HEADS = 8
QK_NOPE = 64
QK_ROPE = 32
V_DIM = 64
Q_LORA = 256
KV_LORA = 128
ROPE_BASE = 10000.0
ATT_SCALE = (QK_NOPE + QK_ROPE) ** -0.5
SSM_WIDTH = 512
SSM_GROUP = 16
SSM_GROUPS = 32
SSM_STATE = 64
D_FF = 2816
FFN_K = 3
ADAM_LR = 0.001
ADAM_B1 = 0.9
ADAM_B2 = 0.999
ADAM_EPS = 1e-08
ADAM_WD = 0.01
ADAM_STEP = 10

N_CHIPS = 4
LANES = 128
HEAD_PAD = 256
CONV_HALO = 32
FFN_HALO = 16
VMEM_LIMIT = 56 * 1024 * 1024

ROW_TILE = 512
FFN_ROW_TILE = 1024
FFN_COL_TILE = 256
FFN_ROW_CHUNK = 64
CONV_ROW_CHUNK = 32
ATT_TILE = 1024
SCAN_TILE = 256
SCAN_UNROLL = 4


def _cparams(*sem):
    return pltpu.CompilerParams(dimension_semantics=tuple(sem), vmem_limit_bytes=VMEM_LIMIT)


def _pick(n, cands):
    for c in cands:
        if n % c == 0:
            return c
    return n


def matmul(a, b, *, ta=False, tb=False, res=None, bias=None, out_dtype=None, name):
    if out_dtype is None:
        out_dtype = BF16 if ta else F32
    if ta:
        kdim, m = a.shape
    else:
        m, kdim = a.shape
    if tb:
        n, k2 = b.shape
    else:
        k2, n = b.shape
    assert kdim == k2, (a.shape, b.shape, ta, tb)
    tn = _pick(n, (1408, 1024, 768, 512, 384, 256, 128))
    if ta:
        tm = _pick(m, (1408, 1024, 512, 256, 128))
        tk = _pick(kdim, (512, 256, 128))
    else:
        tm = _pick(m, (1024, 512, 256, 128))
        tk = kdim
        if kdim > 1024:
            tn = _pick(n, (512, 256, 128))
        if tm * tn > 1024 * 1024 and out_dtype == F32:
            tm = _pick(m, (512, 256, 128))
    nk = kdim // tk
    has_res, has_bias = res is not None, bias is not None
    dims = (((0,) if ta else (1,), (1,) if tb else (0,)), ((), ()))

    def body(*refs):
        a_ref, b_ref = refs[0], refs[1]
        pos = 2
        res_ref = bias_ref = None
        if has_res:
            res_ref = refs[pos]
            pos += 1
        if has_bias:
            bias_ref = refs[pos]
            pos += 1
        o_ref = refs[pos]

        def finish(r):
            if has_bias:
                r = r + bias_ref[...]
            if has_res:
                r = r + res_ref[...].astype(F32)
            o_ref[...] = r.astype(o_ref.dtype)

        prod = lax.dot_general(a_ref[...].astype(BF16), b_ref[...].astype(BF16), dims, preferred_element_type=F32)
        if nk == 1:
            finish(prod)
            return
        acc_ref = refs[pos + 1]
        k = pl.program_id(2)

        @pl.when(k == 0)
        def _():
            acc_ref[...] = prod

        @pl.when(k > 0)
        def _():
            acc_ref[...] += prod

        @pl.when(k == nk - 1)
        def _():
            finish(acc_ref[...])

    a_spec = pl.BlockSpec((tk, tm), lambda i, j, k: (k, i)) if ta else pl.BlockSpec((tm, tk), lambda i, j, k: (i, k))
    b_spec = pl.BlockSpec((tn, tk), lambda i, j, k: (j, k)) if tb else pl.BlockSpec((tk, tn), lambda i, j, k: (k, j))
    in_specs = [a_spec, b_spec]
    args = [a, b]
    if has_res:
        in_specs.append(pl.BlockSpec((tm, tn), lambda i, j, k: (i, j)))
        args.append(res)
    if has_bias:
        in_specs.append(pl.BlockSpec((1, tn), lambda i, j, k: (0, j)))
        args.append(bias)
    return pl.pallas_call(
        body, name=name, grid=(m // tm, n // tn, nk),
        in_specs=in_specs, out_specs=pl.BlockSpec((tm, tn), lambda i, j, k: (i, j)),
        out_shape=jax.ShapeDtypeStruct((m, n), out_dtype),
        scratch_shapes=[pltpu.VMEM((tm, tn), F32)] if nk > 1 else [],
        compiler_params=_cparams("parallel", "parallel", "arbitrary"),
    )(*args)


def rowcall(body, *, rows, ts, ins, outs, name, scratch=()):
    nt = rows // ts
    in_specs, args = [], []
    for arr, kind in ins:
        if kind == "row":
            in_specs.append(pl.BlockSpec((ts, arr.shape[1]), lambda i: (i, 0)))
        elif kind == "rev":
            in_specs.append(pl.BlockSpec((ts, arr.shape[1]), lambda i: (nt - 1 - i, 0)))
        elif kind == "full":
            nd = arr.ndim
            in_specs.append(pl.BlockSpec(arr.shape, lambda i, nd=nd: (0,) * nd))
        elif kind.startswith("prev:"):
            h = int(kind[5:])
            r = ts // h
            in_specs.append(pl.BlockSpec((h, arr.shape[1]), lambda i, r=r: (jnp.maximum(i * r - 1, 0), 0)))
        elif kind.startswith("next:"):
            h = int(kind[5:])
            r = ts // h
            last = rows // h - 1
            in_specs.append(pl.BlockSpec((h, arr.shape[1]), lambda i, r=r, last=last: (jnp.minimum((i + 1) * r, last), 0)))
        elif kind.startswith("revprev:"):
            h = int(kind[8:])
            r = ts // h
            in_specs.append(pl.BlockSpec((h, arr.shape[1]), lambda i, r=r: (jnp.maximum((nt - 1 - i) * r - 1, 0), 0)))
        else:
            raise ValueError(kind)
        args.append(arr)
    out_specs, out_shapes = [], []
    for shape, dtype, kind in outs:
        if kind == "row":
            out_specs.append(pl.BlockSpec((ts, shape[1]), lambda i: (i, 0)))
        elif kind == "rev":
            out_specs.append(pl.BlockSpec((ts, shape[1]), lambda i: (nt - 1 - i, 0)))
        else:
            nd = len(shape)
            out_specs.append(pl.BlockSpec(tuple(shape), lambda i, nd=nd: (0,) * nd))
        out_shapes.append(jax.ShapeDtypeStruct(tuple(shape), dtype))
    return pl.pallas_call(
        functools.partial(body, nt), name=name, grid=(nt,),
        in_specs=in_specs, out_specs=tuple(out_specs), out_shape=tuple(out_shapes),
        scratch_shapes=list(scratch),
        compiler_params=_cparams("arbitrary"),
    )(*args)


def _rms(x, g):
    return x * lax.rsqrt(jnp.mean(x * x, axis=-1, keepdims=True) + EPS) * g


def _layer_norm(x, g, b):
    mu = jnp.mean(x, axis=-1, keepdims=True)
    xc = x - mu
    var = jnp.mean(xc * xc, axis=-1, keepdims=True)
    return xc * lax.rsqrt(var + LN_EPS) * g + b


def _sigmoid(x):
    return 1.0 / (1.0 + jnp.exp(-x))


def _silu(x):
    return x * _sigmoid(x)


def _gelu(x):
    return 0.5 * x * (1.0 + jnp.tanh(math.sqrt(2.0 / math.pi) * (x + 0.044715 * (x * x * x))))


def _acc(ref, i, val):
    s = jnp.sum(val, axis=0, keepdims=True)

    @pl.when(i == 0)
    def _():
        ref[...] = jnp.zeros_like(ref)

    ref[...] += jnp.broadcast_to(s, ref.shape)


def rms_fwd(x, g, name):
    s, c = x.shape

    def body(nt, x_ref, g_ref, o_ref):
        o_ref[...] = _rms(x_ref[...], g_ref[...]).astype(BF16)

    return rowcall(body, rows=s, ts=min(ROW_TILE, s), ins=[(x, "row"), (g, "full")],
                   outs=[((s, c), BF16, "row")], name=name)[0]


def rms_bwd(x, g, dxn, dres, name):
    s, c = x.shape

    def body(nt, x_ref, g_ref, d_ref, r_ref, dx_ref, dg_ref):
        i = pl.program_id(0)
        _, vjp = jax.vjp(_rms, x_ref[...], g_ref[...])
        dx, dg = vjp(d_ref[...].astype(F32))
        dx_ref[...] = dx + r_ref[...]
        _acc(dg_ref, i, dg)

    return rowcall(body, rows=s, ts=min(ROW_TILE, s),
                   ins=[(x, "row"), (g, "full"), (dxn, "row"), (dres, "row")],
                   outs=[((s, c), F32, "row"), ((8, c), F32, "acc")], name=name)


def _partner(t):
    lane = lax.broadcasted_iota(jnp.int32, t.shape, 1)
    half = QK_ROPE // 2
    return jnp.where(lane % QK_ROPE < half, pltpu.roll(t, LANES - half, 1), pltpu.roll(t, half, 1))


def _rope(t, cos, sin):
    return t * cos + _partner(t) * sin


def _rope_t(d, cos, sin):
    return d * cos + _partner(d * sin)


def rope_tables(s):
    half = QK_ROPE // 2
    inv = ROPE_BASE ** (-jnp.arange(half, dtype=F32) / half)
    ang = jnp.arange(s).astype(F32)[:, None] * inv[None, :]
    cos, sin = jnp.cos(ang), jnp.sin(ang)
    z = jnp.zeros((s, LANES - QK_ROPE), F32)
    return jnp.concatenate([cos, cos, z], axis=1), jnp.concatenate([-sin, sin, z], axis=1)


H0_A, H0_G, H0_Q, H0_KV, H0_KR, H0_W = 0, 512, 1024, 1280, 1408, 1536


def _mixpre_fn(a, g, q, kv, qn, kvn):
    return a * _sigmoid(g), _rms(q, qn), _rms(kv, kvn)


def _h0_parts(h_ref):
    return (h_ref[:, H0_A:H0_G], h_ref[:, H0_G:H0_Q], h_ref[:, H0_Q:H0_KV], h_ref[:, H0_KV:H0_KR])


def mixpre_fwd(h0, qn, kvn, cos, sin):
    s = h0.shape[0]

    def body(nt, h_ref, qn_ref, kvn_ref, cos_ref, sin_ref, u0_ref, cq_ref, ckv_ref, kr_ref):
        u0, cq, ckv = _mixpre_fn(*_h0_parts(h_ref), qn_ref[...], kvn_ref[...])
        u0_ref[...] = u0
        cq_ref[...] = cq.astype(BF16)
        ckv_ref[...] = ckv.astype(BF16)
        kr_ref[...] = _rope(h_ref[:, H0_KR:H0_W], cos_ref[...], sin_ref[...]).astype(BF16)

    return rowcall(body, rows=s, ts=min(ROW_TILE, s),
                   ins=[(h0, "row"), (qn, "full"), (kvn, "full"), (cos, "row"), (sin, "row")],
                   outs=[((s, CONV_WIDTH), F32, "row"), ((s, Q_LORA), BF16, "row"),
                         ((s, KV_LORA), BF16, "row"), ((s, LANES), BF16, "row")], name="mixpre_fwd")


def mixpre_bwd(h0, qn, kvn, cos, sin, du0, dcq, dckv, dkr):
    s = h0.shape[0]

    def body(nt, h_ref, qn_ref, kvn_ref, cos_ref, sin_ref, du0_ref, dcq_ref, dckv_ref, dkr_ref,
             dh_ref, dqn_ref, dkvn_ref):
        i = pl.program_id(0)
        _, vjp = jax.vjp(_mixpre_fn, *_h0_parts(h_ref), qn_ref[...], kvn_ref[...])
        da, dg, dq, dkv, dqn, dkvn = vjp((du0_ref[...], dcq_ref[...], dckv_ref[...]))
        dh_ref[:, H0_A:H0_G] = da.astype(BF16)
        dh_ref[:, H0_G:H0_Q] = dg.astype(BF16)
        dh_ref[:, H0_Q:H0_KV] = dq.astype(BF16)
        dh_ref[:, H0_KV:H0_KR] = dkv.astype(BF16)
        dkr = dkr_ref[:, :LANES]
        for h in range(1, N_HEADS):
            dkr = dkr + dkr_ref[:, h * LANES:(h + 1) * LANES]
        dh_ref[:, H0_KR:H0_W] = _rope_t(dkr, cos_ref[...], sin_ref[...]).astype(BF16)
        _acc(dqn_ref, i, dqn)
        _acc(dkvn_ref, i, dkvn)

    return rowcall(body, rows=s, ts=min(ROW_TILE, s),
                   ins=[(h0, "row"), (qn, "full"), (kvn, "full"), (cos, "row"), (sin, "row"),
                        (du0, "row"), (dcq, "row"), (dckv, "row"), (dkr, "row")],
                   outs=[((s, H0_W), BF16, "row"), ((8, Q_LORA), F32, "acc"), ((8, KV_LORA), F32, "acc")],
                   name="mixpre_bwd")


def _conv_taps(ext_ref, w_ref, ts, first, ntaps, flip=False):
    acc = None
    for k in range(ntaps):
        term = w_ref[pl.ds(ntaps - 1 - k if flip else k, 1), :] * ext_ref[pl.ds(first + k, ts), :]
        acc = term if acc is None else acc + term
    return acc


def _ln_silu(u1, g, b):
    return _silu(_layer_norm(u1, g, b))


SUBLANES = 8


def _fill_shifted(sh_ref, parts, rows):
    pos = 0
    for p in parts:
        sh_ref[0, pl.ds(pos, p.shape[0]), :] = p
        pos += p.shape[0]
    sh_ref[0, pl.ds(rows, SUBLANES), :] = jnp.zeros((SUBLANES, sh_ref.shape[2]), F32)
    for r in range(1, SUBLANES):
        sh_ref[r, pl.ds(0, rows), :] = sh_ref[0, pl.ds(r, rows), :]


def _window(sh_ref, off, n):
    r = off % SUBLANES
    return sh_ref[r, pl.ds(off - r, n), :]


def _taps_aligned(sh_ref, w_ref, n, first, ntaps, flip=False):
    acc = None
    for k in range(ntaps):
        term = w_ref[pl.ds(ntaps - 1 - k if flip else k, 1), :] * _window(sh_ref, first + k, n)
        acc = term if acc is None else acc + term
    return acc


def _conv_scratch(ts, c):
    return pltpu.VMEM((SUBLANES, ts + CONV_HALO + SUBLANES, c), F32)


def convln_fwd(u0, w, b, lg, lb):
    s, c = u0.shape
    ts = min(ROW_TILE, s)
    rc = min(CONV_ROW_CHUNK, ts)
    first = CONV_HALO - (CONV_K - 1)

    def body(nt, cur_ref, prev_ref, w_ref, b_ref, lg_ref, lb_ref, o_ref, sh_ref):
        i = pl.program_id(0)
        _fill_shifted(sh_ref, [jnp.where(i > 0, prev_ref[...], 0.0), cur_ref[...]], ts + CONV_HALO)
        for r0 in range(0, ts, rc):
            u1 = _taps_aligned(sh_ref, w_ref, rc, first + r0, CONV_K) + b_ref[...]
            o_ref[pl.ds(r0, rc), :] = _ln_silu(u1, lg_ref[...], lb_ref[...]).astype(BF16)

    return rowcall(body, rows=s, ts=ts,
                   ins=[(u0, "row"), (u0, f"prev:{CONV_HALO}"), (w, "full"), (b, "full"), (lg, "full"), (lb, "full")],
                   outs=[((s, c), BF16, "row")], name="convln_fwd", scratch=[_conv_scratch(ts, c)])[0]


def convln_bwd1(u0, w, b, lg, lb, du):
    s, c = u0.shape
    ts = min(ROW_TILE, s)
    rc = min(CONV_ROW_CHUNK, ts)
    first = CONV_HALO - (CONV_K - 1)

    def body(nt, cur_ref, prev_ref, w_ref, b_ref, lg_ref, lb_ref, du_ref, du1_ref, dlg_ref, dlb_ref, dcb_ref, sh_ref):
        i = pl.program_id(0)
        _fill_shifted(sh_ref, [jnp.where(i > 0, prev_ref[...], 0.0), cur_ref[...]], ts + CONV_HALO)
        sums = [jnp.zeros((1, c), F32)] * 3
        for r0 in range(0, ts, rc):
            u1 = _taps_aligned(sh_ref, w_ref, rc, first + r0, CONV_K) + b_ref[...]
            _, vjp = jax.vjp(_ln_silu, u1, lg_ref[...], lb_ref[...])
            du1, dlg, dlb = vjp(du_ref[pl.ds(r0, rc), :].astype(F32))
            du1_ref[pl.ds(r0, rc), :] = du1
            parts = (dlg, dlb, jnp.sum(du1, axis=0, keepdims=True))
            sums = [a + jnp.sum(p, axis=0, keepdims=True) for a, p in zip(sums, parts)]
        _acc(dlg_ref, i, sums[0])
        _acc(dlb_ref, i, sums[1])
        _acc(dcb_ref, i, sums[2])

    return rowcall(body, rows=s, ts=ts,
                   ins=[(u0, "row"), (u0, f"prev:{CONV_HALO}"), (w, "full"), (b, "full"), (lg, "full"), (lb, "full"),
                        (du, "row")],
                   outs=[((s, c), F32, "row"), ((8, c), F32, "acc"), ((8, c), F32, "acc"), ((8, c), F32, "acc")],
                   name="convln_bwd1", scratch=[_conv_scratch(ts, c)])


def convln_bwd2(u0, w, du1):
    s, c = u0.shape
    ts = min(ROW_TILE, s)
    rc = min(CONV_ROW_CHUNK, ts)
    first = CONV_HALO - (CONV_K - 1)

    def body(nt, cur_ref, prev_ref, d_ref, dnext_ref, w_ref, du0_ref, dw_ref, sh_ref, dsh_ref):
        i = pl.program_id(0)
        _fill_shifted(sh_ref, [jnp.where(i > 0, prev_ref[...], 0.0), cur_ref[...]], ts + CONV_HALO)
        _fill_shifted(dsh_ref, [d_ref[...], jnp.where(i < nt - 1, dnext_ref[...], 0.0)], ts + CONV_HALO)
        for r0 in range(0, ts, rc):
            du0_ref[pl.ds(r0, rc), :] = _taps_aligned(dsh_ref, w_ref, rc, r0, CONV_K, flip=True)

        @pl.when(i == 0)
        def _():
            dw_ref[...] = jnp.zeros_like(dw_ref)

        for k in range(CONV_K):
            part = jnp.zeros((SUBLANES, c), F32)
            for r0 in range(0, ts, rc):
                prod = d_ref[pl.ds(r0, rc), :] * _window(sh_ref, first + k + r0, rc)
                for a in range(0, rc, SUBLANES):
                    part = part + prod[a:a + SUBLANES]
            dw_ref[pl.ds(k, 1), :] += jnp.sum(part, axis=0, keepdims=True)

    return rowcall(body, rows=s, ts=ts,
                   ins=[(u0, "row"), (u0, f"prev:{CONV_HALO}"), (du1, "row"), (du1, f"next:{CONV_HALO}"), (w, "full")],
                   outs=[((s, c), F32, "row"), ((CONV_HALO, c), F32, "acc")], name="convln_bwd2",
                   scratch=[_conv_scratch(ts, c), _conv_scratch(ts, c)])


def qrope_fwd(qraw, cos, sin):
    s = qraw.shape[0]

    def body(nt, q_ref, cos_ref, sin_ref, o_ref):
        cos_v, sin_v = cos_ref[...] * ATT_SCALE, sin_ref[...] * ATT_SCALE
        for h in range(N_HEADS):
            nope = q_ref[:, h * HEAD_PAD:h * HEAD_PAD + LANES] * ATT_SCALE
            o_ref[:, h * HEAD_PAD:h * HEAD_PAD + LANES] = nope.astype(BF16)
            r = q_ref[:, h * HEAD_PAD + LANES:(h + 1) * HEAD_PAD]
            o_ref[:, h * HEAD_PAD + LANES:(h + 1) * HEAD_PAD] = _rope(r, cos_v, sin_v).astype(BF16)

    return rowcall(body, rows=s, ts=min(ROW_TILE, s), ins=[(qraw, "row"), (cos, "row"), (sin, "row")],
                   outs=[((s, N_HEADS * HEAD_PAD), BF16, "row")], name="qrope_fwd")[0]


def qrope_bwd(dq, cos, sin):
    s = dq.shape[0]

    def body(nt, d_ref, cos_ref, sin_ref, o_ref):
        cos_v, sin_v = cos_ref[...] * ATT_SCALE, sin_ref[...] * ATT_SCALE
        for h in range(N_HEADS):
            nope = d_ref[:, h * HEAD_PAD:h * HEAD_PAD + LANES] * ATT_SCALE
            o_ref[:, h * HEAD_PAD:h * HEAD_PAD + LANES] = nope.astype(BF16)
            r = d_ref[:, h * HEAD_PAD + LANES:(h + 1) * HEAD_PAD].astype(F32)
            o_ref[:, h * HEAD_PAD + LANES:(h + 1) * HEAD_PAD] = _rope_t(r, cos_v, sin_v).astype(BF16)

    return rowcall(body, rows=s, ts=min(ROW_TILE, s), ins=[(dq, "row"), (cos, "row"), (sin, "row")],
                   outs=[((s, N_HEADS * HEAD_PAD), BF16, "row")], name="qrope_bwd")[0]


_NT = (((1,), (1,)), ((), ()))
_TN = (((0,), (0,)), ((), ()))


def _scores(q, kvr, diagonal):
    s = lax.dot_general(q, kvr, _NT, preferred_element_type=F32)
    if not diagonal:
        return s
    row = lax.broadcasted_iota(jnp.int32, s.shape, 0)
    col = lax.broadcasted_iota(jnp.int32, s.shape, 1)
    return jnp.where(col <= row, s, -jnp.inf)


def _on_causal_pairs(pair, k_blk, fn):
    @pl.when(k_blk < 2 * pair)
    def _():
        fn(0, False)
        fn(1, False)

    @pl.when(k_blk == 2 * pair)
    def _():
        fn(0, True)
        fn(1, False)

    @pl.when(k_blk == 2 * pair + 1)
    def _():
        fn(1, True)


def attn_fwd(q, kv, kr):
    s = q.shape[0]
    t = min(ATT_TILE, s // 2)
    n = s // t
    np_ = n // 2

    def body(q_ref, kv_ref, kr_ref, o_ref, lse_ref, m_ref, l_ref, acc_ref):
        i, j = pl.program_id(1), pl.program_id(2)

        @pl.when(j == 0)
        def _():
            m_ref[...] = jnp.full_like(m_ref, -jnp.inf)
            l_ref[...] = jnp.zeros_like(l_ref)
            acc_ref[...] = jnp.zeros_like(acc_ref)

        def block(sub, diagonal):
            kvv = kv_ref[...]
            kvr = jnp.concatenate([kvv, kr_ref[...]], axis=1)
            sc = _scores(q_ref[pl.ds(sub * t, t), :], kvr, diagonal)
            m_prev = m_ref[sub]
            m_new = jnp.maximum(m_prev, jnp.max(sc, axis=-1, keepdims=True))
            alpha = jnp.exp(m_prev - m_new)
            p = jnp.exp(sc - m_new)
            l_ref[sub] = alpha * l_ref[sub] + jnp.sum(p, axis=-1, keepdims=True)
            acc_ref[sub] = alpha * acc_ref[sub] + jnp.dot(p.astype(BF16), kvv, preferred_element_type=F32)
            m_ref[sub] = m_new

        _on_causal_pairs(i, j, block)

        @pl.when(j == 2 * i + 1)
        def _():
            for sub in range(2):
                l = l_ref[sub]
                o_ref[pl.ds(sub * t, t), :] = (acc_ref[sub] / l).astype(BF16)
                lse_ref[pl.ds(sub * t, t), :] = jnp.broadcast_to(m_ref[sub] + jnp.log(l), (t, LANES))

    kj = lambda h, i, j: (jnp.minimum(j, 2 * i + 1), h)
    return pl.pallas_call(
        body, name="attn_fwd", grid=(N_HEADS, np_, n),
        in_specs=[pl.BlockSpec((2 * t, HEAD_PAD), lambda h, i, j: (i, h)),
                  pl.BlockSpec((t, LANES), kj),
                  pl.BlockSpec((t, LANES), lambda h, i, j: (jnp.minimum(j, 2 * i + 1), 0))],
        out_specs=(pl.BlockSpec((2 * t, LANES), lambda h, i, j: (i, h)),
                   pl.BlockSpec((2 * t, LANES), lambda h, i, j: (i, h))),
        out_shape=(jax.ShapeDtypeStruct((s, N_HEADS * LANES), BF16),
                   jax.ShapeDtypeStruct((s, N_HEADS * LANES), F32)),
        scratch_shapes=[pltpu.VMEM((2, t, 1), F32), pltpu.VMEM((2, t, 1), F32), pltpu.VMEM((2, t, LANES), F32)],
        compiler_params=_cparams("parallel", "parallel", "arbitrary"),
    )(q, kv, kr)


def attn_bwd(q, kv, kr, o, do, lse):
    s = q.shape[0]
    t = min(ATT_TILE, s // 2)
    n = s // t
    np_ = n // 2

    def body(q_ref, kv_ref, kr_ref, o_ref, do_ref, lse_ref, dq_ref, dkv_ref, dkr_ref):
        j, i = pl.program_id(1), pl.program_id(2)

        @pl.when((i == 0) & (j == 0))
        def _():
            dq_ref[...] = jnp.zeros_like(dq_ref)

        @pl.when(i == 0)
        def _():
            dkv_ref[...] = jnp.zeros_like(dkv_ref)
            dkr_ref[...] = jnp.zeros_like(dkr_ref)

        def block(sub, diagonal):
            sl = pl.ds(sub * t, t)
            qv, dov, kvv = q_ref[sl, :], do_ref[sl, :], kv_ref[...]
            kvr = jnp.concatenate([kvv, kr_ref[...]], axis=1)
            p = jnp.exp(_scores(qv, kvr, diagonal) - lse_ref[sl, :1])
            dp = lax.dot_general(dov, kvv, _NT, preferred_element_type=F32)
            delta = jnp.sum(dov.astype(F32) * o_ref[sl, :].astype(F32), axis=-1, keepdims=True)
            ds = (p * (dp - delta)).astype(BF16)
            dk = lax.dot_general(ds, qv, _TN, preferred_element_type=F32)
            dkv_ref[...] += lax.dot_general(p.astype(BF16), dov, _TN, preferred_element_type=F32) + dk[:, :LANES]
            dkr_ref[...] += dk[:, LANES:]
            rows = pl.ds(pl.multiple_of((2 * i + sub) * t, t), t)
            dq_ref[rows, :] += jnp.dot(ds, kvr, preferred_element_type=F32)

        _on_causal_pairs(i, j, block)

    qi = lambda h, j, i: (jnp.maximum(i, lax.div(j, 2)), h)
    kj = lambda h, j, i: (j, h)
    return pl.pallas_call(
        body, name="attn_bwd", grid=(N_HEADS, n, np_),
        in_specs=[pl.BlockSpec((2 * t, HEAD_PAD), qi), pl.BlockSpec((t, LANES), kj),
                  pl.BlockSpec((t, LANES), lambda h, j, i: (j, 0)),
                  pl.BlockSpec((2 * t, LANES), qi), pl.BlockSpec((2 * t, LANES), qi), pl.BlockSpec((2 * t, LANES), qi)],
        out_specs=(pl.BlockSpec((s, HEAD_PAD), lambda h, j, i: (0, h)),
                   pl.BlockSpec((t, LANES), kj), pl.BlockSpec((t, LANES), kj)),
        out_shape=(jax.ShapeDtypeStruct((s, N_HEADS * HEAD_PAD), F32),
                   jax.ShapeDtypeStruct((s, N_HEADS * LANES), F32), jax.ShapeDtypeStruct((s, N_HEADS * LANES), F32)),
        compiler_params=_cparams("parallel", "arbitrary", "arbitrary"),
    )(q, kv, kr, o, do, lse)


def ffn_fwd(x, g, wa, wb, cwa, cwb, ba, bb, wd, name):
    s, d = x.shape
    f = wa.shape[1]
    ts, tf = min(FFN_ROW_TILE, s), FFN_COL_TILE
    hal = FFN_HALO
    nj = f // tf
    first = hal - (FFN_K - 1)
    rc = min(FFN_ROW_CHUNK, ts)

    def body(x_ref, xp_ref, g_ref, wa_ref, wb_ref, cwa_ref, cwb_ref, ba_ref, bb_ref, wd_ref,
             xo_ref, xn_ref, hpa_ref, hpb_ref, act_ref, xe_ref, ea_ref, eb_ref):
        i, j = pl.program_id(0), pl.program_id(1)

        @pl.when(j == 0)
        def _():
            xn = _rms(x_ref[...], g_ref[...]).astype(BF16)
            xn_ref[...] = xn
            xe_ref[pl.ds(hal, ts), :] = xn
            xe_ref[pl.ds(0, hal), :] = jnp.where(i > 0, _rms(xp_ref[...], g_ref[...]), 0.0).astype(BF16)
            xo_ref[...] = x_ref[...]

        halves = ((0, ts // 2), (ts // 2, ts))
        for lo, hi in halves:
            e0, e1 = (0 if lo == 0 else hal + lo), hal + hi
            xe = xe_ref[pl.ds(e0, e1 - e0), :]
            ea_ref[pl.ds(e0, e1 - e0), :] = jnp.dot(xe, wa_ref[...], preferred_element_type=F32)
            eb_ref[pl.ds(e0, e1 - e0), :] = jnp.dot(xe, wb_ref[...], preferred_element_type=F32)
        for lo, hi in halves:
            hpa_ref[pl.ds(lo, hi - lo), :] = ea_ref[pl.ds(hal + lo, hi - lo), :].astype(BF16)
            hpb_ref[pl.ds(lo, hi - lo), :] = eb_ref[pl.ds(hal + lo, hi - lo), :].astype(BF16)
            for r0 in range(lo, hi, rc):
                ha = _conv_taps(ea_ref, cwa_ref, rc, first + r0, FFN_K) + ba_ref[...]
                hb = _conv_taps(eb_ref, cwb_ref, rc, first + r0, FFN_K) + bb_ref[...]
                act_ref[pl.ds(r0, rc), :] = (_silu(ha) * hb).astype(BF16)
            xo_ref[pl.ds(lo, hi - lo), :] += jnp.dot(act_ref[pl.ds(lo, hi - lo), :], wd_ref[...],
                                                     preferred_element_type=F32)

    r = ts // hal
    row = pl.BlockSpec((ts, d), lambda i, j: (i, 0))
    prev = pl.BlockSpec((hal, d), lambda i, j: (jnp.maximum(i * r - 1, 0), 0))
    gsp = pl.BlockSpec((1, d), lambda i, j: (0, 0))
    wup = pl.BlockSpec((d, tf), lambda i, j: (0, j))
    cwsp = pl.BlockSpec((8, tf), lambda i, j: (0, j))
    bsp = pl.BlockSpec((1, tf), lambda i, j: (0, j))
    wdn = pl.BlockSpec((tf, d), lambda i, j: (j, 0))
    hid = pl.BlockSpec((ts, tf), lambda i, j: (i, j))
    return pl.pallas_call(
        body, name=name, grid=(s // ts, nj),
        in_specs=[row, prev, gsp, wup, wup, cwsp, cwsp, bsp, bsp, wdn],
        out_specs=(row, row, hid, hid, hid),
        out_shape=(jax.ShapeDtypeStruct((s, d), F32), jax.ShapeDtypeStruct((s, d), BF16),
                   jax.ShapeDtypeStruct((s, f), BF16), jax.ShapeDtypeStruct((s, f), BF16),
                   jax.ShapeDtypeStruct((s, f), BF16)),
        scratch_shapes=[pltpu.VMEM((ts + hal, d), BF16), pltpu.VMEM((ts + hal, tf), F32),
                        pltpu.VMEM((ts + hal, tf), F32)],
        compiler_params=_cparams("parallel", "arbitrary"),
    )(x, x, g, wa, wb, cwa, cwb, ba, bb, wd)


def ffn_bwd(dy, hpa, hpb, wa, wb, cwa, cwb, ba, bb, wd, name):
    s, d = dy.shape
    f = hpa.shape[1]
    ts, tf = min(FFN_ROW_TILE, s), FFN_COL_TILE
    hal = FFN_HALO
    nt, nj = s // ts, f // tf
    te = ts + hal
    first = hal - (FFN_K - 1)
    rc = min(FFN_ROW_CHUNK, ts)

    def body(dy_ref, dyn_ref, a_ref, ap_ref, an_ref, b_ref, bp_ref, bn_ref, wa_ref, wb_ref, cwa_ref, cwb_ref,
             ba_ref, bb_ref, wd_ref,
             dxn_ref, dpa_ref, dpb_ref, dwa_ref, dwb_ref, dba_ref, dbb_ref,
             dye_ref, ea_ref, eb_ref, dact_ref, da_ref, db_ref):
        i, j = pl.program_id(0), pl.program_id(1)
        last = i == nt - 1

        @pl.when(j == 0)
        def _():
            dye_ref[pl.ds(0, ts), :] = dy_ref[...].astype(BF16)
            dye_ref[pl.ds(ts, hal), :] = jnp.where(last, 0.0, dyn_ref[...]).astype(BF16)
            dxn_ref[...] = jnp.zeros_like(dxn_ref)

        @pl.when((i == 0) & (j == 0))
        def _():
            for r in (dwa_ref, dwb_ref, dba_ref, dbb_ref):
                r[...] = jnp.zeros_like(r)

        halves = ((0, ts // 2), (ts // 2, ts))
        for lo, hi in halves:
            n = hi - lo + (hal if hi == ts else 0)
            dact_ref[pl.ds(lo, n), :] = lax.dot_general(dye_ref[pl.ds(lo, n), :], wd_ref[...], _NT,
                                                        preferred_element_type=F32)
        for cur, prev, nxt, ext in ((a_ref, ap_ref, an_ref, ea_ref), (b_ref, bp_ref, bn_ref, eb_ref)):
            ext[pl.ds(0, hal), :] = jnp.where(i > 0, prev[...].astype(F32), 0.0)
            ext[pl.ds(hal, ts), :] = cur[...].astype(F32)
            ext[pl.ds(hal + ts, hal), :] = jnp.where(last, 0.0, nxt[...].astype(F32))
        zero = jnp.zeros((1, tf), F32)
        sums = {"ba": zero, "bb": zero, **{("a", k): zero for k in range(FFN_K)}, **{("b", k): zero for k in range(FFN_K)}}
        for r0 in list(range(0, ts, rc)) + [ts]:
            n = rc if r0 < ts else hal
            win_a = [ea_ref[pl.ds(first + r0 + k, n), :] for k in range(FFN_K)]
            win_b = [eb_ref[pl.ds(first + r0 + k, n), :] for k in range(FFN_K)]
            ha = sum(cwa_ref[pl.ds(k, 1), :] * win_a[k] for k in range(FFN_K)) + ba_ref[...]
            hb = sum(cwb_ref[pl.ds(k, 1), :] * win_b[k] for k in range(FFN_K)) + bb_ref[...]
            sig = _sigmoid(ha)
            gs = dact_ref[pl.ds(r0, n), :] * sig
            dha = gs * hb * (1.0 + ha * (1.0 - sig))
            dhb = gs * ha
            da_ref[pl.ds(r0, n), :] = dha
            db_ref[pl.ds(r0, n), :] = dhb
            if r0 < ts:
                sums["ba"] = sums["ba"] + jnp.sum(dha, axis=0, keepdims=True)
                sums["bb"] = sums["bb"] + jnp.sum(dhb, axis=0, keepdims=True)
                for k in range(FFN_K):
                    sums["a", k] = sums["a", k] + jnp.sum(dha * win_a[k], axis=0, keepdims=True)
                    sums["b", k] = sums["b", k] + jnp.sum(dhb * win_b[k], axis=0, keepdims=True)
        for lo, hi in halves:
            for r0 in range(lo, hi, rc):
                dpa_ref[pl.ds(r0, rc), :] = _conv_taps(da_ref, cwa_ref, rc, r0, FFN_K, flip=True).astype(BF16)
                dpb_ref[pl.ds(r0, rc), :] = _conv_taps(db_ref, cwb_ref, rc, r0, FFN_K, flip=True).astype(BF16)
            rows = pl.ds(lo, hi - lo)
            dxn_ref[rows, :] += (lax.dot_general(dpa_ref[rows, :], wa_ref[...], _NT, preferred_element_type=F32)
                                 + lax.dot_general(dpb_ref[rows, :], wb_ref[...], _NT, preferred_element_type=F32))
        dba_ref[j] += jnp.broadcast_to(sums["ba"], (8, tf))
        dbb_ref[j] += jnp.broadcast_to(sums["bb"], (8, tf))
        row = lax.broadcasted_iota(jnp.int32, (8, tf), 0)
        dwa_ref[j] += sum(jnp.where(row == k, sums["a", k], 0.0) for k in range(FFN_K))
        dwb_ref[j] += sum(jnp.where(row == k, sums["b", k], 0.0) for k in range(FFN_K))

    r = ts // hal
    lastblk = s // hal - 1
    row = pl.BlockSpec((ts, d), lambda i, j: (i, 0))
    rown = pl.BlockSpec((hal, d), lambda i, j: (jnp.minimum((i + 1) * r, lastblk), 0))
    cur = pl.BlockSpec((ts, tf), lambda i, j: (i, j))
    prev = pl.BlockSpec((hal, tf), lambda i, j: (jnp.maximum(i * r - 1, 0), j))
    nxt = pl.BlockSpec((hal, tf), lambda i, j: (jnp.minimum((i + 1) * r, lastblk), j))
    wup = pl.BlockSpec((d, tf), lambda i, j: (0, j))
    cwsp = pl.BlockSpec((8, tf), lambda i, j: (0, j))
    bsp = pl.BlockSpec((1, tf), lambda i, j: (0, j))
    wdn = pl.BlockSpec((tf, d), lambda i, j: (j, 0))
    accsp = pl.BlockSpec((nj, 8, tf), lambda i, j: (0, 0, 0))
    accshape = jax.ShapeDtypeStruct((nj, 8, tf), F32)
    return pl.pallas_call(
        body, name=name, grid=(nt, nj),
        in_specs=[row, rown, cur, prev, nxt, cur, prev, nxt, wup, wup, cwsp, cwsp, bsp, bsp, wdn],
        out_specs=(row, cur, cur, accsp, accsp, accsp, accsp),
        out_shape=(jax.ShapeDtypeStruct((s, d), F32), jax.ShapeDtypeStruct((s, f), BF16),
                   jax.ShapeDtypeStruct((s, f), BF16), accshape, accshape, accshape, accshape),
        scratch_shapes=[pltpu.VMEM((te, d), BF16), pltpu.VMEM((ts + 2 * hal, tf), F32),
                        pltpu.VMEM((ts + 2 * hal, tf), F32), pltpu.VMEM((te, tf), F32),
                        pltpu.VMEM((te, tf), F32), pltpu.VMEM((te, tf), F32)],
        compiler_params=_cparams("arbitrary", "arbitrary"),
    )(dy, dy, hpa, hpa, hpa, hpb, hpb, hpb, wa, wb, cwa, cwb, ba, bb, wd)


NQ = 4
SQ = SSM_STATE * 8
NS = SSM_GROUPS * SSM_STATE


def _s5_disc(log_dt, a_re, a_im, b_re, b_im, expand):
    dt = jnp.exp(log_dt)
    mag = jnp.exp(a_re * dt)
    lb_re, lb_im = mag * jnp.cos(a_im * dt), mag * jnp.sin(a_im * dt)
    den = a_re * a_re + a_im * a_im
    nr, ni = lb_re - 1.0, lb_im
    f_re = (nr * a_re + ni * a_im) / den
    f_im = (ni * a_re - nr * a_im) / den
    fe_re = jnp.dot(f_re, expand, precision=lax.Precision.HIGHEST, preferred_element_type=F32)
    fe_im = jnp.dot(f_im, expand, precision=lax.Precision.HIGHEST, preferred_element_type=F32)
    return lb_re, lb_im, fe_re * b_re - fe_im * b_im, fe_re * b_im + fe_im * b_re


def _expand_matrix():
    e = np.zeros((SSM_STATE, SSM_STATE * SSM_GROUP), np.float32)
    for p in range(SSM_STATE):
        e[p, p * SSM_GROUP:(p + 1) * SSM_GROUP] = 1.0
    return jnp.asarray(e)


def s5_params_fwd(log_dt, a_re, a_im, b_re, b_im):
    expand = _expand_matrix()

    def body(ld_ref, ar_ref, ai_ref, br_ref, bi_ref, e_ref, lr_ref, li_ref, bbr_ref, bbi_ref):
        lr, li, bbr, bbi = _s5_disc(ld_ref[...], ar_ref[...], ai_ref[...], br_ref[...], bi_ref[...], e_ref[...])
        lr_ref[...] = lr
        li_ref[...] = li
        bbr_ref[...] = bbr
        bbi_ref[...] = bbi

    g, p, pc = SSM_GROUPS, SSM_STATE, SSM_STATE * SSM_GROUP
    return pl.pallas_call(
        body, name="s5_params_fwd",
        out_shape=(jax.ShapeDtypeStruct((g, p), F32), jax.ShapeDtypeStruct((g, p), F32),
                   jax.ShapeDtypeStruct((g, pc), F32), jax.ShapeDtypeStruct((g, pc), F32)),
    )(log_dt, a_re, a_im, b_re, b_im, expand)


def s5_params_bwd(log_dt, a_re, a_im, b_re, b_im, dlr, dli, dbbr, dbbi):
    expand = _expand_matrix()

    def body(ld_ref, ar_ref, ai_ref, br_ref, bi_ref, e_ref, dlr_ref, dli_ref, dbbr_ref, dbbi_ref,
             dld_ref, dar_ref, dai_ref, dbr_ref, dbi_ref):
        e = e_ref[...]
        f = lambda ld, ar, ai, br, bi: _s5_disc(ld, ar, ai, br, bi, e)
        _, vjp = jax.vjp(f, ld_ref[...], ar_ref[...], ai_ref[...], br_ref[...], bi_ref[...])
        dld, dar, dai, dbr, dbi = vjp((dlr_ref[...], dli_ref[...], dbbr_ref[...], dbbi_ref[...]))
        dld_ref[...] = dld
        dar_ref[...] = dar
        dai_ref[...] = dai
        dbr_ref[...] = dbr
        dbi_ref[...] = dbi

    g, p, pc = SSM_GROUPS, SSM_STATE, SSM_STATE * SSM_GROUP
    return pl.pallas_call(
        body, name="s5_params_bwd",
        out_shape=(jax.ShapeDtypeStruct((g, 1), F32), jax.ShapeDtypeStruct((g, p), F32),
                   jax.ShapeDtypeStruct((g, p), F32), jax.ShapeDtypeStruct((g, pc), F32),
                   jax.ShapeDtypeStruct((g, pc), F32)),
    )(log_dt, a_re, a_im, b_re, b_im, expand, dlr, dli, dbbr, dbbi)


def _cmul(ar, ai, br, bi):
    return ar * br - ai * bi, ar * bi + ai * br


def _power_rows(lr, li, conj_rev):
    row = lax.broadcasted_iota(jnp.int32, (8, NS), 0)
    tr = jnp.zeros((8, NS), F32)
    ti = jnp.zeros((8, NS), F32)
    pr, pi = lr, li
    for r in range(8):
        dst = 7 - r if conj_rev else r
        tr = jnp.where(row == dst, pr, tr)
        ti = jnp.where(row == dst, -pi if conj_rev else pi, ti)
        if r < 7:
            pr, pi = _cmul(pr, pi, lr, li)
    return tr, ti


def _scan8(xr, xi, tr_ref, ti_ref, cr, ci, reverse):
    row = lax.broadcasted_iota(jnp.int32, xr.shape, 0)
    for d in (1, 2, 4):
        if reverse:
            sr, si = pltpu.roll(xr, 8 - d, 0), pltpu.roll(xi, 8 - d, 0)
            keep = row < 8 - d
            pw = 8 - d
        else:
            sr, si = pltpu.roll(xr, d, 0), pltpu.roll(xi, d, 0)
            keep = row >= d
            pw = d - 1
        mr, mi = _cmul(tr_ref[pl.ds(pw, 1), :], ti_ref[pl.ds(pw, 1), :], sr, si)
        xr = xr + jnp.where(keep, mr, 0.0)
        xi = xi + jnp.where(keep, mi, 0.0)
    mr, mi = _cmul(tr_ref[...], ti_ref[...], cr, ci)
    return xr + mr, xi + mi


def _row_of(x, r):
    row = lax.broadcasted_iota(jnp.int32, x.shape, 0)
    return jnp.sum(jnp.where(row == r, x, 0.0), axis=0, keepdims=True)


def s5_scan_fwd(u, lam_r, lam_i, bre, bim, cre, cim, dskip):
    s = u.shape[0]
    tt = min(SCAN_TILE, s)
    nb = tt // 8

    def body(nt, u_ref, lr_ref, li_ref, bre_ref, bim_ref, cre_ref, cim_ref, d_ref,
             xr_ref, xi_ref, y_ref, yg_ref, tr_ref, ti_ref, cr_ref, ci_ref):
        i = pl.program_id(0)

        @pl.when(i == 0)
        def _():
            tr, ti = _power_rows(lr_ref[...], li_ref[...], False)
            tr_ref[...] = tr
            ti_ref[...] = ti
            cr_ref[...] = jnp.zeros_like(cr_ref)
            ci_ref[...] = jnp.zeros_like(ci_ref)

        uv = u_ref[...]
        ub = uv.astype(BF16)
        for q in range(NQ):
            uq = ub[:, q * LANES:(q + 1) * LANES]
            xr_ref[:, q * SQ:(q + 1) * SQ] = jnp.dot(uq, bre_ref[q], preferred_element_type=F32)
            xi_ref[:, q * SQ:(q + 1) * SQ] = jnp.dot(uq, bim_ref[q], preferred_element_type=F32)

        def step(b, carry):
            cr, ci = carry
            rows = pl.ds(pl.multiple_of(b * 8, 8), 8)
            xr, xi = _scan8(xr_ref[rows, :], xi_ref[rows, :], tr_ref, ti_ref, cr, ci, False)
            xr_ref[rows, :] = xr
            xi_ref[rows, :] = xi
            return _row_of(xr, 7), _row_of(xi, 7)

        cr, ci = lax.fori_loop(0, nb, step, (cr_ref[...], ci_ref[...]), unroll=min(SCAN_UNROLL, nb))
        cr_ref[...] = cr
        ci_ref[...] = ci
        y = d_ref[...] * uv
        for q in range(NQ):
            yq = (jnp.dot(xr_ref[:, q * SQ:(q + 1) * SQ].astype(BF16), cre_ref[q], preferred_element_type=F32)
                  - jnp.dot(xi_ref[:, q * SQ:(q + 1) * SQ].astype(BF16), cim_ref[q], preferred_element_type=F32))
            y_ref[:, q * LANES:(q + 1) * LANES] = yq + y[:, q * LANES:(q + 1) * LANES]
        yg_ref[...] = _gelu(y_ref[...]).astype(BF16)

    return rowcall(body, rows=s, ts=tt,
                   ins=[(u, "row"), (lam_r, "full"), (lam_i, "full"), (bre, "full"), (bim, "full"),
                        (cre, "full"), (cim, "full"), (dskip, "full")],
                   outs=[((s, NS), F32, "row"), ((s, NS), F32, "row"), ((s, SSM_WIDTH), F32, "row"),
                         ((s, SSM_WIDTH), BF16, "row")], name="s5_scan_fwd",
                   scratch=[pltpu.VMEM((8, NS), F32), pltpu.VMEM((8, NS), F32),
                            pltpu.VMEM((1, NS), F32), pltpu.VMEM((1, NS), F32)])


def s5_scan_bwd(dyg, y, lam_r, lam_i, cre, cim):
    s = y.shape[0]
    tt = min(SCAN_TILE, s)
    nb = tt // 8

    def body(nt, dyg_ref, y_ref, lr_ref, li_ref, cre_ref, cim_ref,
             ar_ref, ai_ref, dy_ref, tr_ref, ti_ref, cr_ref, ci_ref):
        i = pl.program_id(0)

        @pl.when(i == 0)
        def _():
            tr, ti = _power_rows(lr_ref[...], li_ref[...], True)
            tr_ref[...] = tr
            ti_ref[...] = ti
            cr_ref[...] = jnp.zeros_like(cr_ref)
            ci_ref[...] = jnp.zeros_like(ci_ref)

        _, vjp = jax.vjp(_gelu, y_ref[...])
        dy = vjp(dyg_ref[...])[0]
        dyb = dy.astype(BF16)
        dy_ref[...] = dyb
        for q in range(NQ):
            dq = dyb[:, q * LANES:(q + 1) * LANES]
            ar_ref[:, q * SQ:(q + 1) * SQ] = lax.dot_general(dq, cre_ref[q], _NT, preferred_element_type=F32)
            ai_ref[:, q * SQ:(q + 1) * SQ] = -lax.dot_general(dq, cim_ref[q], _NT, preferred_element_type=F32)

        def step(b, carry):
            cr, ci = carry
            rows = pl.ds(pl.multiple_of((nb - 1 - b) * 8, 8), 8)
            xr, xi = _scan8(ar_ref[rows, :], ai_ref[rows, :], tr_ref, ti_ref, cr, ci, True)
            ar_ref[rows, :] = xr
            ai_ref[rows, :] = xi
            return _row_of(xr, 0), _row_of(xi, 0)

        cr, ci = lax.fori_loop(0, nb, step, (cr_ref[...], ci_ref[...]), unroll=min(SCAN_UNROLL, nb))
        cr_ref[...] = cr
        ci_ref[...] = ci

    return rowcall(body, rows=s, ts=tt,
                   ins=[(dyg, "rev"), (y, "rev"), (lam_r, "full"), (lam_i, "full"), (cre, "full"), (cim, "full")],
                   outs=[((s, NS), F32, "rev"), ((s, NS), F32, "rev"), ((s, SSM_WIDTH), BF16, "rev")],
                   name="s5_scan_bwd",
                   scratch=[pltpu.VMEM((8, NS), F32), pltpu.VMEM((8, NS), F32),
                            pltpu.VMEM((1, NS), F32), pltpu.VMEM((1, NS), F32)])


def s5_grads(u, dy, xr, xi, ar, ai, bre, bim, dskip):
    s = u.shape[0]
    tt = min(SCAN_TILE, s)

    def body(nt, u_ref, dy_ref, xr_ref, xrp_ref, xi_ref, xip_ref, ar_ref, ai_ref, bre_ref, bim_ref, d_ref,
             du_ref, dlr_ref, dli_ref, dbr_ref, dbi_ref, dcr_ref, dci_ref, dd_ref, er_ref, ei_ref):
        i = pl.program_id(0)

        @pl.when(i == 0)
        def _():
            for r in (dbr_ref, dbi_ref, dcr_ref, dci_ref):
                r[...] = jnp.zeros_like(r)

        uv, dyb = u_ref[...], dy_ref[...]
        dyf = dyb.astype(F32)
        av_r, av_i, xv_r, xv_i = ar_ref[...], ai_ref[...], xr_ref[...], xi_ref[...]
        er_ref[pl.ds(0, 8), :] = jnp.where(i > 0, xrp_ref[...], 0.0)
        ei_ref[pl.ds(0, 8), :] = jnp.where(i > 0, xip_ref[...], 0.0)
        er_ref[pl.ds(8, tt), :] = xv_r
        ei_ref[pl.ds(8, tt), :] = xv_i
        sr, si = er_ref[pl.ds(7, tt), :], ei_ref[pl.ds(7, tt), :]
        _acc(dlr_ref, i, av_r * sr + av_i * si)
        _acc(dli_ref, i, av_i * sr - av_r * si)
        _acc(dd_ref, i, dyf * uv)
        ub = uv.astype(BF16)
        ab_r, ab_i = av_r.astype(BF16), av_i.astype(BF16)
        xb_r, xb_i = xv_r.astype(BF16), xv_i.astype(BF16)
        du = d_ref[...] * dyf
        for q in range(NQ):
            cs, ss = slice(q * LANES, (q + 1) * LANES), slice(q * SQ, (q + 1) * SQ)
            dbr_ref[q] += lax.dot_general(ub[:, cs], ab_r[:, ss], _TN, preferred_element_type=F32)
            dbi_ref[q] += lax.dot_general(ub[:, cs], ab_i[:, ss], _TN, preferred_element_type=F32)
            dcr_ref[q] += lax.dot_general(xb_r[:, ss], dyb[:, cs], _TN, preferred_element_type=F32)
            dci_ref[q] -= lax.dot_general(xb_i[:, ss], dyb[:, cs], _TN, preferred_element_type=F32)
            du_ref[:, cs] = (du[:, cs]
                             + lax.dot_general(ab_r[:, ss], bre_ref[q], _NT, preferred_element_type=F32)
                             + lax.dot_general(ab_i[:, ss], bim_ref[q], _NT, preferred_element_type=F32))

    return rowcall(body, rows=s, ts=tt,
                   ins=[(u, "row"), (dy, "row"), (xr, "row"), (xr, "prev:8"), (xi, "row"), (xi, "prev:8"),
                        (ar, "row"), (ai, "row"), (bre, "full"), (bim, "full"), (dskip, "full")],
                   outs=[((s, SSM_WIDTH), F32, "row"), ((8, NS), F32, "acc"), ((8, NS), F32, "acc"),
                         ((NQ, LANES, SQ), F32, "acc"), ((NQ, LANES, SQ), F32, "acc"),
                         ((NQ, SQ, LANES), F32, "acc"), ((NQ, SQ, LANES), F32, "acc"),
                         ((8, SSM_WIDTH), F32, "acc")], name="s5_grads",
                   scratch=[pltpu.VMEM((tt + 8, NS), F32), pltpu.VMEM((tt + 8, NS), F32)])


def _glu_fn(za, zb):
    return za * _sigmoid(zb)


def glu_res_fwd(z, xres):
    s = z.shape[0]

    def body(nt, z_ref, x_ref, o_ref):
        o_ref[...] = x_ref[...] + _glu_fn(z_ref[:, :D_MODEL].astype(F32), z_ref[:, D_MODEL:].astype(F32))

    return rowcall(body, rows=s, ts=min(ROW_TILE, s), ins=[(z, "row"), (xres, "row")],
                   outs=[((s, D_MODEL), F32, "row")], name="glu_res_fwd")[0]


def glu_bwd(z, dout):
    s, c = z.shape

    def body(nt, z_ref, d_ref, dz_ref, dba_ref, dbb_ref):
        i = pl.program_id(0)
        _, vjp = jax.vjp(_glu_fn, z_ref[:, :D_MODEL].astype(F32), z_ref[:, D_MODEL:].astype(F32))
        dza, dzb = vjp(d_ref[...])
        dz_ref[:, :D_MODEL] = dza.astype(BF16)
        dz_ref[:, D_MODEL:] = dzb.astype(BF16)
        _acc(dba_ref, i, dza)
        _acc(dbb_ref, i, dzb)

    return rowcall(body, rows=s, ts=min(ROW_TILE, s), ins=[(z, "row"), (dout, "row")],
                   outs=[((s, c), BF16, "row"), ((8, D_MODEL), F32, "acc"), ((8, D_MODEL), F32, "acc")],
                   name="glu_bwd")


def loss_head(x, g, target):
    s, c = x.shape

    def body(nt, x_ref, g_ref, t_ref, loss_ref, dx_ref, dg_ref):
        i = pl.program_id(0)
        y, vjp = jax.vjp(_rms, x_ref[...], g_ref[...])
        err = y - t_ref[...]
        dx, dg = vjp(err * (1.0 / c))
        dx_ref[...] = dx
        _acc(dg_ref, i, dg)
        part = jnp.sum(jnp.sum(err * err, axis=-1, keepdims=True), axis=0, keepdims=True) * (0.5 / c)

        @pl.when(i == 0)
        def _():
            loss_ref[...] = jnp.zeros_like(loss_ref)

        loss_ref[...] += jnp.broadcast_to(part, loss_ref.shape)

    return rowcall(body, rows=s, ts=min(ROW_TILE, s), ins=[(x, "row"), (g, "full"), (target, "row")],
                   outs=[((8, LANES), F32, "acc"), ((s, c), F32, "row"), ((8, c), F32, "acc")], name="loss_head")


def _tile_rows(r, cands=(512, 256, 128, 64, 32, 16, 8)):
    return _pick(r, cands)


def add_to_bf16(a, b, name):
    n, r, c = a.shape
    tr = _tile_rows(r)

    def body(a_ref, b_ref, o_ref):
        o_ref[...] = (a_ref[...].astype(F32) + b_ref[...].astype(F32)).astype(BF16)

    spec = pl.BlockSpec((1, tr, c), lambda j, i: (j, i, 0))
    return pl.pallas_call(body, name=name, grid=(n, r // tr), in_specs=[spec, spec], out_specs=spec,
                          out_shape=jax.ShapeDtypeStruct((n, r, c), BF16),
                          compiler_params=_cparams("parallel", "parallel"))(a, b)


def sum_leading(a, name):
    n, r, c = a.shape
    tr = _tile_rows(r)

    def body(a_ref, o_ref):
        acc = a_ref[0].astype(F32)
        for k in range(1, n):
            acc = acc + a_ref[k].astype(F32)
        o_ref[...] = acc

    return pl.pallas_call(body, name=name, grid=(r // tr,),
                          in_specs=[pl.BlockSpec((n, tr, c), lambda i: (0, i, 0))],
                          out_specs=pl.BlockSpec((tr, c), lambda i: (i, 0)),
                          out_shape=jax.ShapeDtypeStruct((r, c), F32),
                          compiler_params=_cparams("parallel"))(a)


def adamw(w, g, m, v, name):
    r, c = w.shape
    tr = _tile_rows(r, (256, 128, 64, 32, 16, 8))
    c1 = 1.0 - ADAM_B1 ** ADAM_STEP
    c2 = 1.0 - ADAM_B2 ** ADAM_STEP

    def body(w_ref, g_ref, m_ref, v_ref, d_ref, nm_ref, nv_ref):
        gv = g_ref[...]
        mn = ADAM_B1 * m_ref[...] + (1.0 - ADAM_B1) * gv
        vn = ADAM_B2 * v_ref[...] + (1.0 - ADAM_B2) * (gv * gv)
        d_ref[...] = -ADAM_LR * ((mn / c1) / (jnp.sqrt(vn / c2) + ADAM_EPS) + ADAM_WD * w_ref[...])
        nm_ref[...] = mn
        nv_ref[...] = vn

    spec = pl.BlockSpec((tr, c), lambda i: (i, 0))
    shp = jax.ShapeDtypeStruct((r, c), F32)
    return pl.pallas_call(body, name=name, grid=(r // tr,), in_specs=[spec] * 4, out_specs=(spec,) * 3,
                          out_shape=(shp,) * 3, compiler_params=_cparams("parallel"))(w, g, m, v)


_ANY = pl.BlockSpec(memory_space=pl.ANY)


def all_gather8(block, name):
    r, c = block.shape

    def body(x_ref, out_ref, send_sems, recv_sems, local_sem):
        x, y, cc = lax.axis_index("x"), lax.axis_index("y"), lax.axis_index("c")
        me, sibling = (x, y, cc), (x, y, 1 - cc)
        chips = [(1 - x, y), (x, 1 - y), (1 - x, 1 - y)]

        def slot(px, py, pc):
            return out_ref.at[4 * px + 2 * py + pc]

        def copy(k, blk, to, src=None):
            return pltpu.make_async_remote_copy(
                src_ref=slot(*blk) if src is None else src, dst_ref=slot(*blk),
                send_sem=send_sems.at[k], recv_sem=recv_sems.at[k], device_id=to, device_id_type=MESH)

        mine = pltpu.make_async_copy(x_ref, slot(*me), local_sem)
        mine.start()
        first = [copy(0, me, sibling, src=x_ref)]
        first += [copy(1 + j, me, (*chip, cc), src=x_ref) for j, chip in enumerate(chips)]
        for cp in first:
            cp.start()
        passed = [copy(4 + j, (*chip, cc), sibling) for j, chip in enumerate(chips)]
        for j, chip in enumerate(chips):
            copy(1 + j, (*chip, cc), me).wait_recv()
            passed[j].start()
        copy(0, sibling, me).wait_recv()
        for j, chip in enumerate(chips):
            copy(4 + j, (*chip, 1 - cc), me).wait_recv()
        for cp in first + passed:
            cp.wait_send()
        mine.wait()

    return pl.pallas_call(
        body, name=name, in_specs=[_ANY], out_specs=_ANY,
        out_shape=jax.ShapeDtypeStruct((8, r, c), block.dtype),
        scratch_shapes=[pltpu.SemaphoreType.DMA((7,)), pltpu.SemaphoreType.DMA((7,)), pltpu.SemaphoreType.DMA],
    )(block)


def sibling_swap(block, name):
    def body(x_ref, out_ref, send_sem, recv_sem):
        x, y, cc = lax.axis_index("x"), lax.axis_index("y"), lax.axis_index("c")
        cp = pltpu.make_async_remote_copy(src_ref=x_ref, dst_ref=out_ref, send_sem=send_sem, recv_sem=recv_sem,
                                          device_id=(x, y, 1 - cc), device_id_type=MESH)
        cp.start()
        cp.wait()

    return pl.pallas_call(
        body, name=name, in_specs=[_ANY], out_specs=_ANY,
        out_shape=jax.ShapeDtypeStruct(block.shape, block.dtype),
        scratch_shapes=[pltpu.SemaphoreType.DMA, pltpu.SemaphoreType.DMA],
    )(block)


def chip_exchange(parts, name):
    def body(p_ref, out_ref, send_sems, recv_sems, local_sem):
        x, y, cc = lax.axis_index("x"), lax.axis_index("y"), lax.axis_index("c")
        me = 2 * x + y
        chips = [(1 - x, y), (x, 1 - y), (1 - x, 1 - y)]
        mine = pltpu.make_async_copy(p_ref.at[me], out_ref.at[me], local_sem)
        mine.start()
        sends = []
        for k, (px, py) in enumerate(chips):
            sends.append(pltpu.make_async_remote_copy(
                src_ref=p_ref.at[2 * px + py], dst_ref=out_ref.at[me],
                send_sem=send_sems.at[k], recv_sem=recv_sems.at[k], device_id=(px, py, cc), device_id_type=MESH))
        for cp in sends:
            cp.start()
        for k, (px, py) in enumerate(chips):
            pltpu.make_async_remote_copy(
                src_ref=p_ref.at[me], dst_ref=out_ref.at[2 * px + py],
                send_sem=send_sems.at[k], recv_sem=recv_sems.at[k], device_id=(px, py, cc),
                device_id_type=MESH).wait_recv()
        for cp in sends:
            cp.wait_send()
        mine.wait()

    return pl.pallas_call(
        body, name=name, in_specs=[_ANY], out_specs=_ANY,
        out_shape=jax.ShapeDtypeStruct(parts.shape, parts.dtype),
        scratch_shapes=[pltpu.SemaphoreType.DMA((3,)), pltpu.SemaphoreType.DMA((3,)), pltpu.SemaphoreType.DMA],
    )(parts)


PACK_COLS = 1024
SHARDED = (("l0_w_in", 1), ("l0_w_uq", 1), ("l0_w_ukv", 1), ("l0_w_out", 0), ("l0_w_up", 1), ("l0_w_down", 0),
           ("l1_w_in", 0), ("l1_w_glu", 1), ("l1_w_up", 1), ("l1_w_down", 0),
           ("l0_conv_w", 1), ("l0_ffn_conv_w", 1), ("l1_ffn_conv_w", 1))
REPLICATED = ("l0_mix_norm", "l0_conv_b", "l0_conv_ln_g", "l0_conv_ln_b", "l0_q_norm", "l0_kv_norm", "l0_ffn_norm",
              "l0_ffn_conv_b", "l1_mix_norm", "l1_log_dt", "l1_a_re", "l1_a_im", "l1_b_re", "l1_b_im", "l1_c_re",
              "l1_c_im", "l1_d", "l1_b_glu", "l1_ffn_norm", "l1_ffn_conv_b", "final_norm")


def _pack(arrs, dtype, mult):
    flat = jnp.concatenate([a.reshape(-1).astype(dtype) for a in arrs])
    n = flat.shape[0]
    total = -(-n // mult) * mult
    return jnp.pad(flat, (0, total - n))


def _unpack(flat, shapes):
    out, pos = [], 0
    for shp in shapes:
        n = int(np.prod(shp))
        out.append(flat[pos:pos + n].reshape(shp))
        pos += n
    return out


PACK_ROW_ALIGN = 16


def _pack_rows(arrs, dtype, row_mult):
    parts = []
    for a in arrs:
        n = int(np.prod(a.shape))
        rows = -(-n // PACK_COLS)
        if n % PACK_COLS == 0:
            r = a.astype(dtype).reshape(rows, PACK_COLS)
        else:
            r = jnp.pad(a.reshape(-1).astype(dtype), (0, rows * PACK_COLS - n)).reshape(rows, PACK_COLS)
        parts.append(jnp.pad(r, ((0, (-rows) % PACK_ROW_ALIGN), (0, 0))))
    p = jnp.concatenate(parts)
    return jnp.pad(p, ((0, (-p.shape[0]) % row_mult), (0, 0)))


def _unpack_rows(pack, shapes):
    out, r0 = [], 0
    for shp in shapes:
        n = int(np.prod(shp))
        rows = -(-n // PACK_COLS)
        piece = lax.optimization_barrier(pack[r0:r0 + rows])
        out.append(piece.reshape(shp) if n % PACK_COLS == 0 else piece.reshape(-1)[:n].reshape(shp))
        r0 += rows + (-rows) % PACK_ROW_ALIGN
    return out


def _shard(full, axis, j):
    n = full.shape[axis] // N_CHIPS
    return lax.slice_in_dim(full, j * n, (j + 1) * n, axis=axis)


def _block_diag(t):
    q, g, a, b = t.shape
    eye = jnp.eye(g, dtype=t.dtype)
    return jnp.einsum("qgab,gh->qgahb", t, eye).reshape(q, g * a, g * b)


def _block_diag_t(d, a, b):
    q = d.shape[0]
    d5 = d.reshape(q, 8, a, 8, b)
    eye = jnp.eye(8, dtype=d.dtype)
    return jnp.einsum("qgahb,gh->qgab", d5, eye)


def kernel(x, l0_mix_norm, l0_w_in, l0_conv_w, l0_conv_b, l0_conv_ln_g, l0_conv_ln_b, l0_q_norm, l0_kv_norm, l0_w_uq, l0_w_ukv, l0_w_out, l0_ffn_norm, l0_w_up, l0_ffn_conv_w, l0_ffn_conv_b, l0_w_down, l1_mix_norm, l1_w_in, l1_log_dt, l1_a_re, l1_a_im, l1_b_re, l1_b_im, l1_c_re, l1_c_im, l1_d, l1_w_glu, l1_b_glu, l1_ffn_norm, l1_w_up, l1_ffn_conv_w, l1_ffn_conv_b, l1_w_down, final_norm, loss_target, m_l0_mix_norm, m_l0_w_in, m_l0_conv_w, m_l0_conv_b, m_l0_conv_ln_g, m_l0_conv_ln_b, m_l0_q_norm, m_l0_kv_norm, m_l0_w_uq, m_l0_w_ukv, m_l0_w_out, m_l0_ffn_norm, m_l0_w_up, m_l0_ffn_conv_w, m_l0_ffn_conv_b, m_l0_w_down, m_l1_mix_norm, m_l1_w_in, m_l1_log_dt, m_l1_a_re, m_l1_a_im, m_l1_b_re, m_l1_b_im, m_l1_c_re, m_l1_c_im, m_l1_d, m_l1_w_glu, m_l1_b_glu, m_l1_ffn_norm, m_l1_w_up, m_l1_ffn_conv_w, m_l1_ffn_conv_b, m_l1_w_down, m_final_norm, v_l0_mix_norm, v_l0_w_in, v_l0_conv_w, v_l0_conv_b, v_l0_conv_ln_g, v_l0_conv_ln_b, v_l0_q_norm, v_l0_kv_norm, v_l0_w_uq, v_l0_w_ukv, v_l0_w_out, v_l0_ffn_norm, v_l0_w_up, v_l0_ffn_conv_w, v_l0_ffn_conv_b, v_l0_w_down, v_l1_mix_norm, v_l1_w_in, v_l1_log_dt, v_l1_a_re, v_l1_a_im, v_l1_b_re, v_l1_b_im, v_l1_c_re, v_l1_c_im, v_l1_d, v_l1_w_glu, v_l1_b_glu, v_l1_ffn_norm, v_l1_w_up, v_l1_ffn_conv_w, v_l1_ffn_conv_b, v_l1_w_down, v_final_norm):
    a = dict(locals())
    w = {n: a[n] for n in [s for s, _ in SHARDED] + list(REPLICATED)}
    mom = {n: a["m_" + n] for n in w}
    var = {n: a["v_" + n] for n in w}
    return _step(a["x"][0], a["loss_target"][0], w, mom, var)


def _gather_weights(w):
    cc = lax.axis_index("c")
    big = [n for n, _ in SHARDED[:10]]
    small = [n for n, _ in SHARDED[10:]]
    full = {}
    for names, dtype, mult in ((big, BF16, 2 * 256), (small, F32, 2 * PACK_ROW_ALIGN)):
        pack = _pack_rows([w[n] for n in names], dtype, mult)
        half = lax.dynamic_index_in_dim(pack.reshape(2, -1, PACK_COLS), cc, axis=0, keepdims=False)
        got = all_gather8(half, "gather_" + ("matrices" if dtype == BF16 else "conv_weights"))
        got = got.reshape(N_CHIPS, -1, PACK_COLS)
        shapes = [w[n].shape for n in names]
        per_chip = [_unpack_rows(got[j], shapes) for j in range(N_CHIPS)]
        for k, n in enumerate(names):
            axis = dict(SHARDED)[n]
            full[n] = jnp.concatenate([per_chip[j][k] for j in range(N_CHIPS)], axis=axis)
    return full


def _reduce_sharded(grads):
    cc = lax.axis_index("c")
    names = [n for n, _ in SHARDED]
    axes = dict(SHARDED)
    packs = [_pack_rows([_shard(grads[n], axes[n], j) for n in names], BF16, 2 * 256) for j in range(N_CHIPS)]
    g = jnp.stack(packs).reshape(N_CHIPS, 2, -1, PACK_COLS)
    keep = lax.dynamic_index_in_dim(g, cc, axis=1, keepdims=False)
    give = lax.dynamic_index_in_dim(g, 1 - cc, axis=1, keepdims=False)
    got = sibling_swap(give, "grad_swap_halves")
    parts = add_to_bf16(keep, got, "grad_add_sibling")
    landed = chip_exchange(parts, "grad_chip_exchange")
    mine = sum_leading(landed, "grad_sum_chips")
    theirs = sibling_swap(mine, "grad_swap_sums")
    lo = jnp.where(cc == 0, mine, theirs)
    hi = jnp.where(cc == 0, theirs, mine)
    shapes = [_shard(grads[n], axes[n], 0).shape for n in names]
    return dict(zip(names, _unpack_rows(jnp.concatenate([lo, hi]), shapes)))


def _reduce_replicated(grads):
    names = list(REPLICATED)
    flat = _pack([grads[n] for n in names], F32, 256 * LANES).reshape(-1, LANES)
    got = all_gather8(flat, "gather_small_grads")
    tot = sum_leading(got, "sum_small_grads").reshape(-1)
    return dict(zip(names, _unpack(tot, [grads[n].shape for n in names]))), flat.shape


def _row(v):
    return v.reshape(1, -1).astype(F32)


def _pad_rows(wt, rows):
    return jnp.pad(wt.astype(F32), ((0, rows - wt.shape[0]), (0, 0)))


def _ffn_fwd(xin, g, wa, wb, cw, cb, wd, tag):
    cwa, cwb = _pad_rows(cw[:, :D_FF], 8), _pad_rows(cw[:, D_FF:], 8)
    xout, xn, hpa, hpb, act = ffn_fwd(xin, _row(g), wa, wb, cwa, cwb, _row(cb[:D_FF]), _row(cb[D_FF:]), wd, tag)
    return xout, (xin, xn, hpa, hpb, act)


def _ffn_bwd(dxout, saved, g, wa, wb, cw, cb, wd, tag):
    xin, xn, hpa, hpb, act = saved
    d_wd = matmul(act, dxout, ta=True, name=f"{tag}_d_wdown")
    cwa, cwb = _pad_rows(cw[:, :D_FF], 8), _pad_rows(cw[:, D_FF:], 8)
    dxn, dpa, dpb, dwa, dwb, dba, dbb = ffn_bwd(dxout, hpa, hpb, wa, wb, cwa, cwb,
                                                _row(cb[:D_FF]), _row(cb[D_FF:]), wd, tag + "_bwd")
    d_wu = jnp.concatenate([matmul(xn, dpa, ta=True, name=f"{tag}_d_wup_a"),
                            matmul(xn, dpb, ta=True, name=f"{tag}_d_wup_b")], axis=1)
    dxin, dg = rms_bwd(xin, _row(g), dxn, dxout, f"{tag}_rms_bwd")
    taps = lambda t: t.transpose(1, 0, 2).reshape(8, -1)
    d_cw = jnp.concatenate([taps(dwa)[:FFN_K], taps(dwb)[:FFN_K]], axis=1)
    d_cb = jnp.concatenate([taps(dba)[0], taps(dbb)[0]])
    return dxin, dg[0], d_wu, d_cw, d_cb, d_wd


def _step(x, target, w, mom, var):
    s = x.shape[0]
    full = _gather_weights(w)
    cos, sin = rope_tables(s)

    w_in0 = full["l0_w_in"]
    w_in0p = jnp.concatenate([w_in0, jnp.zeros((D_MODEL, H0_W - w_in0.shape[1]), BF16)], axis=1)
    wq = full["l0_w_uq"].reshape(Q_LORA, N_HEADS, QK_NOPE + QK_ROPE)
    zq = lambda n: jnp.zeros((Q_LORA, N_HEADS, n), BF16)
    w_uqp = jnp.concatenate([wq[..., :QK_NOPE], zq(LANES - QK_NOPE), wq[..., QK_NOPE:], zq(LANES - QK_ROPE)],
                            axis=-1).reshape(Q_LORA, N_HEADS * HEAD_PAD)
    w_ukv = full["l0_w_ukv"]
    w_out = full["l0_w_out"]
    w_out_u = w_out[:CONV_WIDTH]
    wo = w_out[CONV_WIDTH:].reshape(N_HEADS, V_DIM, D_MODEL)
    w_out_a = jnp.concatenate([jnp.zeros_like(wo), wo], axis=1).reshape(N_HEADS * LANES, D_MODEL)
    conv_w = _pad_rows(full["l0_conv_w"], CONV_HALO)
    w_up0a, w_up0b = full["l0_w_up"][:, :D_FF], full["l0_w_up"][:, D_FF:]
    w_up1a, w_up1b = full["l1_w_up"][:, :D_FF], full["l1_w_up"][:, D_FF:]

    xn0 = rms_fwd(x, _row(w["l0_mix_norm"]), "l0_mix_rms")
    h0 = matmul(xn0, w_in0p, name="l0_in_proj")
    qn_g, kvn_g = _row(w["l0_q_norm"]), _row(w["l0_kv_norm"])
    u0, cq, ckv, kr = mixpre_fwd(h0, qn_g, kvn_g, cos, sin)
    cb, lg, lb = _row(w["l0_conv_b"]), _row(w["l0_conv_ln_g"]), _row(w["l0_conv_ln_b"])
    u = convln_fwd(u0, conv_w, cb, lg, lb)
    qraw = matmul(cq, w_uqp, name="l0_q_up")
    q = qrope_fwd(qraw, cos, sin)
    kv = matmul(ckv, w_ukv, out_dtype=BF16, name="l0_kv_up")
    o, lse = attn_fwd(q, kv, kr)
    x1 = matmul(u, w_out_u, res=x, name="l0_out_conv")
    x1 = matmul(o, w_out_a, res=x1, name="l0_out_attn")

    x2, ffn0 = _ffn_fwd(x1, w["l0_ffn_norm"], w_up0a, w_up0b, full["l0_ffn_conv_w"], w["l0_ffn_conv_b"],
                        full["l0_w_down"], "l0_ffn")

    g_, p_, c_ = SSM_GROUPS, SSM_STATE, SSM_GROUP
    s5_in = (w["l1_log_dt"].reshape(g_, 1), w["l1_a_re"], w["l1_a_im"],
             w["l1_b_re"].reshape(g_, p_ * c_), w["l1_b_im"].reshape(g_, p_ * c_))
    lam_r, lam_i, bb_r, bb_i = s5_params_fwd(*s5_in)
    lam_rf, lam_if = lam_r.reshape(1, NS), lam_i.reshape(1, NS)

    def b_blocks(bb):
        t = bb.reshape(NQ, 8, p_, c_).transpose(0, 1, 3, 2)
        return _block_diag(t).astype(BF16)

    def c_blocks(cm):
        t = cm.reshape(NQ, 8, c_, p_).transpose(0, 1, 3, 2)
        return _block_diag(t).astype(BF16)

    bre, bim = b_blocks(bb_r), b_blocks(bb_i)
    cre, cim = c_blocks(w["l1_c_re"]), c_blocks(w["l1_c_im"])
    dskip = _row(w["l1_d"])
    xn2 = rms_fwd(x2, _row(w["l1_mix_norm"]), "l1_mix_rms")
    u1 = matmul(xn2, full["l1_w_in"], name="l1_in_proj")
    xs_r, xs_i, y1, yg = s5_scan_fwd(u1, lam_rf, lam_if, bre, bim, cre, cim, dskip)
    z = matmul(yg, full["l1_w_glu"], bias=_row(w["l1_b_glu"]), out_dtype=BF16, name="l1_glu_proj")
    x3 = glu_res_fwd(z, x2)

    x4, ffn1 = _ffn_fwd(x3, w["l1_ffn_norm"], w_up1a, w_up1b, full["l1_ffn_conv_w"], w["l1_ffn_conv_b"],
                        full["l1_w_down"], "l1_ffn")
    loss_part, dx4, dgf = loss_head(x4, _row(w["final_norm"]), target)
    loss = lax.psum(loss_part[0, 0], ("x", "y", "c"))

    gr = {"final_norm": dgf[0]}

    dx3, gr["l1_ffn_norm"], gr["l1_w_up"], gr["l1_ffn_conv_w"], gr["l1_ffn_conv_b"], gr["l1_w_down"] = _ffn_bwd(
        dx4, ffn1, w["l1_ffn_norm"], w_up1a, w_up1b, full["l1_ffn_conv_w"], w["l1_ffn_conv_b"], full["l1_w_down"],
        "l1_ffn")

    dz, dbga, dbgb = glu_bwd(z, dx3)
    gr["l1_b_glu"] = jnp.concatenate([dbga[0], dbgb[0]])
    dyg = matmul(dz, full["l1_w_glu"], tb=True, name="l1_d_yg")
    gr["l1_w_glu"] = matmul(yg, dz, ta=True, name="l1_d_wglu")
    a_r, a_i, dy1 = s5_scan_bwd(dyg, y1, lam_rf, lam_if, cre, cim)
    du1, dlr, dli, dbr, dbi, dcr, dci, dd = s5_grads(u1, dy1, xs_r, xs_i, a_r, a_i, bre, bim, dskip)
    gr["l1_d"] = dd[0]

    def b_unblock(d):
        return _block_diag_t(d, c_, p_).transpose(0, 1, 3, 2).reshape(g_, p_ * c_)

    def c_unblock(d):
        return _block_diag_t(d, p_, c_).transpose(0, 1, 3, 2).reshape(g_, c_, p_)

    gr["l1_c_re"], gr["l1_c_im"] = c_unblock(dcr), c_unblock(dci)
    dld, dar, dai, dbre, dbim = s5_params_bwd(*s5_in, dlr[0].reshape(g_, p_), dli[0].reshape(g_, p_),
                                              b_unblock(dbr), b_unblock(dbi))
    gr["l1_log_dt"], gr["l1_a_re"], gr["l1_a_im"] = dld.reshape(g_), dar, dai
    gr["l1_b_re"], gr["l1_b_im"] = dbre.reshape(g_, p_, c_), dbim.reshape(g_, p_, c_)
    dxn2 = matmul(du1, full["l1_w_in"], tb=True, name="l1_d_xn")
    gr["l1_w_in"] = matmul(xn2, du1, ta=True, name="l1_d_win")
    dx2, dg = rms_bwd(x2, _row(w["l1_mix_norm"]), dxn2, dx3, "l1_mix_rms_bwd")
    gr["l1_mix_norm"] = dg[0]

    dx1, gr["l0_ffn_norm"], gr["l0_w_up"], gr["l0_ffn_conv_w"], gr["l0_ffn_conv_b"], gr["l0_w_down"] = _ffn_bwd(
        dx2, ffn0, w["l0_ffn_norm"], w_up0a, w_up0b, full["l0_ffn_conv_w"], w["l0_ffn_conv_b"], full["l0_w_down"],
        "l0_ffn")

    du = matmul(dx1, w_out_u, tb=True, out_dtype=BF16, name="l0_d_u")
    do = matmul(dx1, w_out_a, tb=True, out_dtype=BF16, name="l0_d_o")
    d_wout_u = matmul(u, dx1, ta=True, name="l0_d_wout_u")
    d_wout_a = matmul(o, dx1, ta=True, name="l0_d_wout_a")
    gr["l0_w_out"] = jnp.concatenate(
        [d_wout_u, d_wout_a.reshape(N_HEADS, LANES, D_MODEL)[:, LANES - V_DIM:].reshape(N_HEADS * V_DIM, D_MODEL)])
    dq, dkv, dkr = attn_bwd(q, kv, kr, o, do, lse)
    dqraw = qrope_bwd(dq, cos, sin)
    dcq = matmul(dqraw, w_uqp, tb=True, name="l0_d_cq")
    d_wuqp = matmul(cq, dqraw, ta=True, name="l0_d_wuq").reshape(Q_LORA, N_HEADS, HEAD_PAD)
    gr["l0_w_uq"] = jnp.concatenate([d_wuqp[..., :QK_NOPE], d_wuqp[..., LANES:LANES + QK_ROPE]],
                                    axis=-1).reshape(Q_LORA, -1)
    dckv = matmul(dkv, w_ukv, tb=True, name="l0_d_ckv")
    gr["l0_w_ukv"] = matmul(ckv, dkv, ta=True, name="l0_d_wukv")
    du1c, dlg, dlb, dcb = convln_bwd1(u0, conv_w, cb, lg, lb, du)
    gr["l0_conv_ln_g"], gr["l0_conv_ln_b"], gr["l0_conv_b"] = dlg[0], dlb[0], dcb[0]
    du0, dcw = convln_bwd2(u0, conv_w, du1c)
    gr["l0_conv_w"] = dcw[:CONV_K]
    dh0, dqn, dkvn = mixpre_bwd(h0, qn_g, kvn_g, cos, sin, du0, dcq, dckv, dkr)
    gr["l0_q_norm"], gr["l0_kv_norm"] = dqn[0], dkvn[0]
    dxn0 = matmul(dh0, w_in0p, tb=True, name="l0_d_xn")
    gr["l0_w_in"] = matmul(xn0, dh0, ta=True, name="l0_d_win")[:, :w_in0.shape[1]]
    grad_x, dg = rms_bwd(x, _row(w["l0_mix_norm"]), dxn0, dx1, "l0_mix_rms_bwd")
    gr["l0_mix_norm"] = dg[0]

    g_sh = _reduce_sharded(gr)
    g_rep, pack_shape = _reduce_replicated(gr)
    grad, delta, new_m, new_v = {}, {}, {}, {}
    for n, _ in SHARDED:
        shp = w[n].shape
        two_d = (lambda t: t.reshape(shp[0], -1))
        grad[n] = g_sh[n]
        delta[n], new_m[n], new_v[n] = adamw(two_d(w[n]), two_d(g_sh[n]), two_d(mom[n]), two_d(var[n]), f"adamw_{n}")
    names = list(REPLICATED)
    pk = lambda d: _pack([d[n] for n in names], F32, 256 * LANES).reshape(pack_shape)
    dl, nm, nv = adamw(pk(w), pk(g_rep), pk(mom), pk(var), "adamw_small")
    shapes = [w[n].shape for n in names]
    for n, d_, m_, v_ in zip(names, _unpack(dl.reshape(-1), shapes), _unpack(nm.reshape(-1), shapes),
                             _unpack(nv.reshape(-1), shapes)):
        grad[n], delta[n], new_m[n], new_v[n] = g_rep[n], d_, m_, v_

    order = ["l0_mix_norm", "l0_w_in", "l0_conv_w", "l0_conv_b", "l0_conv_ln_g", "l0_conv_ln_b", "l0_q_norm",
             "l0_kv_norm", "l0_w_uq", "l0_w_ukv", "l0_w_out", "l0_ffn_norm", "l0_w_up", "l0_ffn_conv_w",
             "l0_ffn_conv_b", "l0_w_down", "l1_mix_norm", "l1_w_in", "l1_log_dt", "l1_a_re", "l1_a_im", "l1_b_re",
             "l1_b_im", "l1_c_re", "l1_c_im", "l1_d", "l1_w_glu", "l1_b_glu", "l1_ffn_norm", "l1_w_up",
             "l1_ffn_conv_w", "l1_ffn_conv_b", "l1_w_down", "final_norm"]
    return (loss, grad_x[None], *[grad[n] for n in order], *[delta[n] for n in order],
            *[new_m[n] for n in order], *[new_v[n] for n in order])
```

```python
import functools
import math

import jax
import jax.numpy as jnp
import numpy as np
from jax import lax
from jax.experimental import pallas as pl
from jax.experimental.pallas import tpu as pltpu

F32 = jnp.float32
BF16 = jnp.bfloat16
MESH = pl.DeviceIdType.MESH

D_MODEL = 1024
EPS = 1e-6
LN_EPS = 1e-5
CONV_WIDTH = 512
CONV_K = 31
N_HEADS = 8
QK_NOPE = 64
QK_ROPE = 32
V_DIM = 64
Q_LORA = 256
KV_LORA = 128
ROPE_BASE = 10000.0
ATT_SCALE = (QK_NOPE + QK_ROPE) ** -0.5
SSM_WIDTH = 512
SSM_GROUP = 16
SSM_GROUPS = 32
SSM_STATE = 64
D_FF = 2816
FFN_K = 3
ADAM_LR = 0.001
ADAM_B1 = 0.9
ADAM_B2 = 0.999
ADAM_EPS = 1e-08
ADAM_WD = 0.01
ADAM_STEP = 10

N_CHIPS = 4
LANES = 128
HEAD_PAD = 256
CONV_HALO = 32
FFN_HALO = 16
VMEM_LIMIT = 56 * 1024 * 1024

ROW_TILE = 512
FFN_ROW_TILE = 1024
FFN_COL_TILE = 256
FFN_ROW_CHUNK = 64
CONV_ROW_CHUNK = 32
ATT_TILE = 1024
SCAN_TILE = 256
SCAN_UNROLL = 4


def _cparams(*sem):
    return pltpu.CompilerParams(dimension_semantics=tuple(sem), vmem_limit_bytes=VMEM_LIMIT)


def _pick(n, cands):
    for c in cands:
        if n % c == 0:
            return c
    return n


def matmul(a, b, *, ta=False, tb=False, res=None, bias=None, out_dtype=None, name):
    if out_dtype is None:
        out_dtype = BF16 if ta else F32
    if ta:
        kdim, m = a.shape
    else:
        m, kdim = a.shape
    if tb:
        n, k2 = b.shape
    else:
        k2, n = b.shape
    assert kdim == k2, (a.shape, b.shape, ta, tb)
    tn = _pick(n, (1408, 1024, 768, 512, 384, 256, 128))
    if ta:
        tm = _pick(m, (1408, 1024, 512, 256, 128))
        tk = _pick(kdim, (512, 256, 128))
    else:
        tm = _pick(m, (1024, 512, 256, 128))
        tk = kdim
        if kdim > 1024:
            tn = _pick(n, (512, 256, 128))
        if tm * tn > 1024 * 1024 and out_dtype == F32:
            tm = _pick(m, (512, 256, 128))
    nk = kdim // tk
    has_res, has_bias = res is not None, bias is not None
    dims = (((0,) if ta else (1,), (1,) if tb else (0,)), ((), ()))

    def body(*refs):
        a_ref, b_ref = refs[0], refs[1]
        pos = 2
        res_ref = bias_ref = None
        if has_res:
            res_ref = refs[pos]
            pos += 1
        if has_bias:
            bias_ref = refs[pos]
            pos += 1
        o_ref = refs[pos]

        def finish(r):
            if has_bias:
                r = r + bias_ref[...]
            if has_res:
                r = r + res_ref[...].astype(F32)
            o_ref[...] = r.astype(o_ref.dtype)

        prod = lax.dot_general(a_ref[...].astype(BF16), b_ref[...].astype(BF16), dims, preferred_element_type=F32)
        if nk == 1:
            finish(prod)
            return
        acc_ref = refs[pos + 1]
        k = pl.program_id(2)

        @pl.when(k == 0)
        def _():
            acc_ref[...] = prod

        @pl.when(k > 0)
        def _():
            acc_ref[...] += prod

        @pl.when(k == nk - 1)
        def _():
            finish(acc_ref[...])

    a_spec = pl.BlockSpec((tk, tm), lambda i, j, k: (k, i)) if ta else pl.BlockSpec((tm, tk), lambda i, j, k: (i, k))
    b_spec = pl.BlockSpec((tn, tk), lambda i, j, k: (j, k)) if tb else pl.BlockSpec((tk, tn), lambda i, j, k: (k, j))
    in_specs = [a_spec, b_spec]
    args = [a, b]
    if has_res:
        in_specs.append(pl.BlockSpec((tm, tn), lambda i, j, k: (i, j)))
        args.append(res)
    if has_bias:
        in_specs.append(pl.BlockSpec((1, tn), lambda i, j, k: (0, j)))
        args.append(bias)
    return pl.pallas_call(
        body, name=name, grid=(m // tm, n // tn, nk),
        in_specs=in_specs, out_specs=pl.BlockSpec((tm, tn), lambda i, j, k: (i, j)),
        out_shape=jax.ShapeDtypeStruct((m, n), out_dtype),
        scratch_shapes=[pltpu.VMEM((tm, tn), F32)] if nk > 1 else [],
        compiler_params=_cparams("parallel", "parallel", "arbitrary"),
    )(*args)


def rowcall(body, *, rows, ts, ins, outs, name, scratch=()):
    nt = rows // ts
    in_specs, args = [], []
    for arr, kind in ins:
        if kind == "row":
            in_specs.append(pl.BlockSpec((ts, arr.shape[1]), lambda i: (i, 0)))
        elif kind == "rev":
            in_specs.append(pl.BlockSpec((ts, arr.shape[1]), lambda i: (nt - 1 - i, 0)))
        elif kind == "full":
            nd = arr.ndim
            in_specs.append(pl.BlockSpec(arr.shape, lambda i, nd=nd: (0,) * nd))
        elif kind.startswith("prev:"):
            h = int(kind[5:])
            r = ts // h
            in_specs.append(pl.BlockSpec((h, arr.shape[1]), lambda i, r=r: (jnp.maximum(i * r - 1, 0), 0)))
        elif kind.startswith("next:"):
            h = int(kind[5:])
            r = ts // h
            last = rows // h - 1
            in_specs.append(pl.BlockSpec((h, arr.shape[1]), lambda i, r=r, last=last: (jnp.minimum((i + 1) * r, last), 0)))
        elif kind.startswith("revprev:"):
            h = int(kind[8:])
            r = ts // h
            in_specs.append(pl.BlockSpec((h, arr.shape[1]), lambda i, r=r: (jnp.maximum((nt - 1 - i) * r - 1, 0), 0)))
        else:
            raise ValueError(kind)
        args.append(arr)
    out_specs, out_shapes = [], []
    for shape, dtype, kind in outs:
        if kind == "row":
            out_specs.append(pl.BlockSpec((ts, shape[1]), lambda i: (i, 0)))
        elif kind == "rev":
            out_specs.append(pl.BlockSpec((ts, shape[1]), lambda i: (nt - 1 - i, 0)))
        else:
            nd = len(shape)
            out_specs.append(pl.BlockSpec(tuple(shape), lambda i, nd=nd: (0,) * nd))
        out_shapes.append(jax.ShapeDtypeStruct(tuple(shape), dtype))
    return pl.pallas_call(
        functools.partial(body, nt), name=name, grid=(nt,),
        in_specs=in_specs, out_specs=tuple(out_specs), out_shape=tuple(out_shapes),
        scratch_shapes=list(scratch),
        compiler_params=_cparams("arbitrary"),
    )(*args)


def _rms(x, g):
    return x * lax.rsqrt(jnp.mean(x * x, axis=-1, keepdims=True) + EPS) * g


def _layer_norm(x, g, b):
    mu = jnp.mean(x, axis=-1, keepdims=True)
    xc = x - mu
    var = jnp.mean(xc * xc, axis=-1, keepdims=True)
    return xc * lax.rsqrt(var + LN_EPS) * g + b


def _sigmoid(x):
    return 1.0 / (1.0 + jnp.exp(-x))


def _silu(x):
    return x * _sigmoid(x)


def _gelu(x):
    return 0.5 * x * (1.0 + jnp.tanh(math.sqrt(2.0 / math.pi) * (x + 0.044715 * (x * x * x))))


def _acc(ref, i, val):
    s = jnp.sum(val, axis=0, keepdims=True)

    @pl.when(i == 0)
    def _():
        ref[...] = jnp.zeros_like(ref)

    ref[...] += jnp.broadcast_to(s, ref.shape)


def rms_fwd(x, g, name):
    s, c = x.shape

    def body(nt, x_ref, g_ref, o_ref):
        o_ref[...] = _rms(x_ref[...], g_ref[...]).astype(BF16)

    return rowcall(body, rows=s, ts=min(ROW_TILE, s), ins=[(x, "row"), (g, "full")],
                   outs=[((s, c), BF16, "row")], name=name)[0]


def rms_bwd(x, g, dxn, dres, name):
    s, c = x.shape

    def body(nt, x_ref, g_ref, d_ref, r_ref, dx_ref, dg_ref):
        i = pl.program_id(0)
        _, vjp = jax.vjp(_rms, x_ref[...], g_ref[...])
        dx, dg = vjp(d_ref[...].astype(F32))
        dx_ref[...] = dx + r_ref[...]
        _acc(dg_ref, i, dg)

    return rowcall(body, rows=s, ts=min(ROW_TILE, s),
                   ins=[(x, "row"), (g, "full"), (dxn, "row"), (dres, "row")],
                   outs=[((s, c), F32, "row"), ((8, c), F32, "acc")], name=name)


def _partner(t):
    lane = lax.broadcasted_iota(jnp.int32, t.shape, 1)
    half = QK_ROPE // 2
    return jnp.where(lane % QK_ROPE < half, pltpu.roll(t, LANES - half, 1), pltpu.roll(t, half, 1))


def _rope(t, cos, sin):
    return t * cos + _partner(t) * sin


def _rope_t(d, cos, sin):
    return d * cos + _partner(d * sin)


def rope_tables(s):
    half = QK_ROPE // 2
    inv = ROPE_BASE ** (-jnp.arange(half, dtype=F32) / half)
    ang = jnp.arange(s).astype(F32)[:, None] * inv[None, :]
    cos, sin = jnp.cos(ang), jnp.sin(ang)
    z = jnp.zeros((s, LANES - QK_ROPE), F32)
    return jnp.concatenate([cos, cos, z], axis=1), jnp.concatenate([-sin, sin, z], axis=1)


H0_A, H0_G, H0_Q, H0_KV, H0_KR, H0_W = 0, 512, 1024, 1280, 1408, 1536


def _mixpre_fn(a, g, q, kv, qn, kvn):
    return a * _sigmoid(g), _rms(q, qn), _rms(kv, kvn)


def _h0_parts(h_ref):
    return (h_ref[:, H0_A:H0_G], h_ref[:, H0_G:H0_Q], h_ref[:, H0_Q:H0_KV], h_ref[:, H0_KV:H0_KR])


def mixpre_fwd(h0, qn, kvn, cos, sin):
    s = h0.shape[0]

    def body(nt, h_ref, qn_ref, kvn_ref, cos_ref, sin_ref, u0_ref, cq_ref, ckv_ref, kr_ref):
        u0, cq, ckv = _mixpre_fn(*_h0_parts(h_ref), qn_ref[...], kvn_ref[...])
        u0_ref[...] = u0
        cq_ref[...] = cq.astype(BF16)
        ckv_ref[...] = ckv.astype(BF16)
        kr_ref[...] = _rope(h_ref[:, H0_KR:H0_W], cos_ref[...], sin_ref[...]).astype(BF16)

    return rowcall(body, rows=s, ts=min(ROW_TILE, s),
                   ins=[(h0, "row"), (qn, "full"), (kvn, "full"), (cos, "row"), (sin, "row")],
                   outs=[((s, CONV_WIDTH), F32, "row"), ((s, Q_LORA), BF16, "row"),
                         ((s, KV_LORA), BF16, "row"), ((s, LANES), BF16, "row")], name="mixpre_fwd")


def mixpre_bwd(h0, qn, kvn, cos, sin, du0, dcq, dckv, dkr):
    s = h0.shape[0]

    def body(nt, h_ref, qn_ref, kvn_ref, cos_ref, sin_ref, du0_ref, dcq_ref, dckv_ref, dkr_ref,
             dh_ref, dqn_ref, dkvn_ref):
        i = pl.program_id(0)
        _, vjp = jax.vjp(_mixpre_fn, *_h0_parts(h_ref), qn_ref[...], kvn_ref[...])
        da, dg, dq, dkv, dqn, dkvn = vjp((du0_ref[...], dcq_ref[...], dckv_ref[...]))
        dh_ref[:, H0_A:H0_G] = da.astype(BF16)
        dh_ref[:, H0_G:H0_Q] = dg.astype(BF16)
        dh_ref[:, H0_Q:H0_KV] = dq.astype(BF16)
        dh_ref[:, H0_KV:H0_KR] = dkv.astype(BF16)
        dkr = dkr_ref[:, :LANES]
        for h in range(1, N_HEADS):
            dkr = dkr + dkr_ref[:, h * LANES:(h + 1) * LANES]
        dh_ref[:, H0_KR:H0_W] = _rope_t(dkr, cos_ref[...], sin_ref[...]).astype(BF16)
        _acc(dqn_ref, i, dqn)
        _acc(dkvn_ref, i, dkvn)

    return rowcall(body, rows=s, ts=min(ROW_TILE, s),
                   ins=[(h0, "row"), (qn, "full"), (kvn, "full"), (cos, "row"), (sin, "row"),
                        (du0, "row"), (dcq, "row"), (dckv, "row"), (dkr, "row")],
                   outs=[((s, H0_W), BF16, "row"), ((8, Q_LORA), F32, "acc"), ((8, KV_LORA), F32, "acc")],
                   name="mixpre_bwd")


def _conv_taps(ext_ref, w_ref, ts, first, ntaps, flip=False):
    acc = None
    for k in range(ntaps):
        term = w_ref[pl.ds(ntaps - 1 - k if flip else k, 1), :] * ext_ref[pl.ds(first + k, ts), :]
        acc = term if acc is None else acc + term
    return acc


def _ln_silu(u1, g, b):
    return _silu(_layer_norm(u1, g, b))


SUBLANES = 8


def _fill_shifted(sh_ref, parts, rows):
    pos = 0
    for p in parts:
        sh_ref[0, pl.ds(pos, p.shape[0]), :] = p
        pos += p.shape[0]
    sh_ref[0, pl.ds(rows, SUBLANES), :] = jnp.zeros((SUBLANES, sh_ref.shape[2]), F32)
    for r in range(1, SUBLANES):
        sh_ref[r, pl.ds(0, rows), :] = sh_ref[0, pl.ds(r, rows), :]


def _window(sh_ref, off, n):
    r = off % SUBLANES
    return sh_ref[r, pl.ds(off - r, n), :]


def _taps_aligned(sh_ref, w_ref, n, first, ntaps, flip=False):
    acc = None
    for k in range(ntaps):
        term = w_ref[pl.ds(ntaps - 1 - k if flip else k, 1), :] * _window(sh_ref, first + k, n)
        acc = term if acc is None else acc + term
    return acc


def _conv_scratch(ts, c):
    return pltpu.VMEM((SUBLANES, ts + CONV_HALO + SUBLANES, c), F32)


def convln_fwd(u0, w, b, lg, lb):
    s, c = u0.shape
    ts = min(ROW_TILE, s)
    rc = min(CONV_ROW_CHUNK, ts)
    first = CONV_HALO - (CONV_K - 1)

    def body(nt, cur_ref, prev_ref, w_ref, b_ref, lg_ref, lb_ref, o_ref, sh_ref):
        i = pl.program_id(0)
        _fill_shifted(sh_ref, [jnp.where(i > 0, prev_ref[...], 0.0), cur_ref[...]], ts + CONV_HALO)
        for r0 in range(0, ts, rc):
            u1 = _taps_aligned(sh_ref, w_ref, rc, first + r0, CONV_K) + b_ref[...]
            o_ref[pl.ds(r0, rc), :] = _ln_silu(u1, lg_ref[...], lb_ref[...]).astype(BF16)

    return rowcall(body, rows=s, ts=ts,
                   ins=[(u0, "row"), (u0, f"prev:{CONV_HALO}"), (w, "full"), (b, "full"), (lg, "full"), (lb, "full")],
                   outs=[((s, c), BF16, "row")], name="convln_fwd", scratch=[_conv_scratch(ts, c)])[0]


def convln_bwd1(u0, w, b, lg, lb, du):
    s, c = u0.shape
    ts = min(ROW_TILE, s)
    rc = min(CONV_ROW_CHUNK, ts)
    first = CONV_HALO - (CONV_K - 1)

    def body(nt, cur_ref, prev_ref, w_ref, b_ref, lg_ref, lb_ref, du_ref, du1_ref, dlg_ref, dlb_ref, dcb_ref, sh_ref):
        i = pl.program_id(0)
        _fill_shifted(sh_ref, [jnp.where(i > 0, prev_ref[...], 0.0), cur_ref[...]], ts + CONV_HALO)
        sums = [jnp.zeros((1, c), F32)] * 3
        for r0 in range(0, ts, rc):
            u1 = _taps_aligned(sh_ref, w_ref, rc, first + r0, CONV_K) + b_ref[...]
            _, vjp = jax.vjp(_ln_silu, u1, lg_ref[...], lb_ref[...])
            du1, dlg, dlb = vjp(du_ref[pl.ds(r0, rc), :].astype(F32))
            du1_ref[pl.ds(r0, rc), :] = du1
            parts = (dlg, dlb, jnp.sum(du1, axis=0, keepdims=True))
            sums = [a + jnp.sum(p, axis=0, keepdims=True) for a, p in zip(sums, parts)]
        _acc(dlg_ref, i, sums[0])
        _acc(dlb_ref, i, sums[1])
        _acc(dcb_ref, i, sums[2])

    return rowcall(body, rows=s, ts=ts,
                   ins=[(u0, "row"), (u0, f"prev:{CONV_HALO}"), (w, "full"), (b, "full"), (lg, "full"), (lb, "full"),
                        (du, "row")],
                   outs=[((s, c), F32, "row"), ((8, c), F32, "acc"), ((8, c), F32, "acc"), ((8, c), F32, "acc")],
                   name="convln_bwd1", scratch=[_conv_scratch(ts, c)])


def convln_bwd2(u0, w, du1):
    s, c = u0.shape
    ts = min(ROW_TILE, s)
    rc = min(CONV_ROW_CHUNK, ts)
    first = CONV_HALO - (CONV_K - 1)

    def body(nt, cur_ref, prev_ref, d_ref, dnext_ref, w_ref, du0_ref, dw_ref, sh_ref, dsh_ref):
        i = pl.program_id(0)
        _fill_shifted(sh_ref, [jnp.where(i > 0, prev_ref[...], 0.0), cur_ref[...]], ts + CONV_HALO)
        _fill_shifted(dsh_ref, [d_ref[...], jnp.where(i < nt - 1, dnext_ref[...], 0.0)], ts + CONV_HALO)
        for r0 in range(0, ts, rc):
            du0_ref[pl.ds(r0, rc), :] = _taps_aligned(dsh_ref, w_ref, rc, r0, CONV_K, flip=True)

        @pl.when(i == 0)
        def _():
            dw_ref[...] = jnp.zeros_like(dw_ref)

        for k in range(CONV_K):
            part = jnp.zeros((SUBLANES, c), F32)
            for r0 in range(0, ts, rc):
                prod = d_ref[pl.ds(r0, rc), :] * _window(sh_ref, first + k + r0, rc)
                for a in range(0, rc, SUBLANES):
                    part = part + prod[a:a + SUBLANES]
            dw_ref[pl.ds(k, 1), :] += jnp.sum(part, axis=0, keepdims=True)

    return rowcall(body, rows=s, ts=ts,
                   ins=[(u0, "row"), (u0, f"prev:{CONV_HALO}"), (du1, "row"), (du1, f"next:{CONV_HALO}"), (w, "full")],
                   outs=[((s, c), F32, "row"), ((CONV_HALO, c), F32, "acc")], name="convln_bwd2",
                   scratch=[_conv_scratch(ts, c), _conv_scratch(ts, c)])


def qrope_fwd(qraw, cos, sin):
    s = qraw.shape[0]

    def body(nt, q_ref, cos_ref, sin_ref, o_ref):
        cos_v, sin_v = cos_ref[...] * ATT_SCALE, sin_ref[...] * ATT_SCALE
        for h in range(N_HEADS):
            nope = q_ref[:, h * HEAD_PAD:h * HEAD_PAD + LANES] * ATT_SCALE
            o_ref[:, h * HEAD_PAD:h * HEAD_PAD + LANES] = nope.astype(BF16)
            r = q_ref[:, h * HEAD_PAD + LANES:(h + 1) * HEAD_PAD]
            o_ref[:, h * HEAD_PAD + LANES:(h + 1) * HEAD_PAD] = _rope(r, cos_v, sin_v).astype(BF16)

    return rowcall(body, rows=s, ts=min(ROW_TILE, s), ins=[(qraw, "row"), (cos, "row"), (sin, "row")],
                   outs=[((s, N_HEADS * HEAD_PAD), BF16, "row")], name="qrope_fwd")[0]


def qrope_bwd(dq, cos, sin):
    s = dq.shape[0]

    def body(nt, d_ref, cos_ref, sin_ref, o_ref):
        cos_v, sin_v = cos_ref[...] * ATT_SCALE, sin_ref[...] * ATT_SCALE
        for h in range(N_HEADS):
            nope = d_ref[:, h * HEAD_PAD:h * HEAD_PAD + LANES] * ATT_SCALE
            o_ref[:, h * HEAD_PAD:h * HEAD_PAD + LANES] = nope.astype(BF16)
            r = d_ref[:, h * HEAD_PAD + LANES:(h + 1) * HEAD_PAD].astype(F32)
            o_ref[:, h * HEAD_PAD + LANES:(h + 1) * HEAD_PAD] = _rope_t(r, cos_v, sin_v).astype(BF16)

    return rowcall(body, rows=s, ts=min(ROW_TILE, s), ins=[(dq, "row"), (cos, "row"), (sin, "row")],
                   outs=[((s, N_HEADS * HEAD_PAD), BF16, "row")], name="qrope_bwd")[0]


_NT = (((1,), (1,)), ((), ()))
_TN = (((0,), (0,)), ((), ()))


def _scores(q, kvr, diagonal):
    s = lax.dot_general(q, kvr, _NT, preferred_element_type=F32)
    if not diagonal:
        return s
    row = lax.broadcasted_iota(jnp.int32, s.shape, 0)
    col = lax.broadcasted_iota(jnp.int32, s.shape, 1)
    return jnp.where(col <= row, s, -jnp.inf)


def _on_causal_pairs(pair, k_blk, fn):
    @pl.when(k_blk < 2 * pair)
    def _():
        fn(0, False)
        fn(1, False)

    @pl.when(k_blk == 2 * pair)
    def _():
        fn(0, True)
        fn(1, False)

    @pl.when(k_blk == 2 * pair + 1)
    def _():
        fn(1, True)


def attn_fwd(q, kv, kr):
    s = q.shape[0]
    t = min(ATT_TILE, s // 2)
    n = s // t
    np_ = n // 2

    def body(q_ref, kv_ref, kr_ref, o_ref, lse_ref, m_ref, l_ref, acc_ref):
        i, j = pl.program_id(1), pl.program_id(2)

        @pl.when(j == 0)
        def _():
            m_ref[...] = jnp.full_like(m_ref, -jnp.inf)
            l_ref[...] = jnp.zeros_like(l_ref)
            acc_ref[...] = jnp.zeros_like(acc_ref)

        def block(sub, diagonal):
            kvv = kv_ref[...]
            kvr = jnp.concatenate([kvv, kr_ref[...]], axis=1)
            sc = _scores(q_ref[pl.ds(sub * t, t), :], kvr, diagonal)
            m_prev = m_ref[sub]
            m_new = jnp.maximum(m_prev, jnp.max(sc, axis=-1, keepdims=True))
            alpha = jnp.exp(m_prev - m_new)
            p = jnp.exp(sc - m_new)
            l_ref[sub] = alpha * l_ref[sub] + jnp.sum(p, axis=-1, keepdims=True)
            acc_ref[sub] = alpha * acc_ref[sub] + jnp.dot(p.astype(BF16), kvv, preferred_element_type=F32)
            m_ref[sub] = m_new

        _on_causal_pairs(i, j, block)

        @pl.when(j == 2 * i + 1)
        def _():
            for sub in range(2):
                l = l_ref[sub]
                o_ref[pl.ds(sub * t, t), :] = (acc_ref[sub] / l).astype(BF16)
                lse_ref[pl.ds(sub * t, t), :] = jnp.broadcast_to(m_ref[sub] + jnp.log(l), (t, LANES))

    kj = lambda h, i, j: (jnp.minimum(j, 2 * i + 1), h)
    return pl.pallas_call(
        body, name="attn_fwd", grid=(N_HEADS, np_, n),
        in_specs=[pl.BlockSpec((2 * t, HEAD_PAD), lambda h, i, j: (i, h)),
                  pl.BlockSpec((t, LANES), kj),
                  pl.BlockSpec((t, LANES), lambda h, i, j: (jnp.minimum(j, 2 * i + 1), 0))],
        out_specs=(pl.BlockSpec((2 * t, LANES), lambda h, i, j: (i, h)),
                   pl.BlockSpec((2 * t, LANES), lambda h, i, j: (i, h))),
        out_shape=(jax.ShapeDtypeStruct((s, N_HEADS * LANES), BF16),
                   jax.ShapeDtypeStruct((s, N_HEADS * LANES), F32)),
        scratch_shapes=[pltpu.VMEM((2, t, 1), F32), pltpu.VMEM((2, t, 1), F32), pltpu.VMEM((2, t, LANES), F32)],
        compiler_params=_cparams("parallel", "parallel", "arbitrary"),
    )(q, kv, kr)


def attn_bwd(q, kv, kr, o, do, lse):
    s = q.shape[0]
    t = min(ATT_TILE, s // 2)
    n = s // t
    np_ = n // 2

    def body(q_ref, kv_ref, kr_ref, o_ref, do_ref, lse_ref, dq_ref, dkv_ref, dkr_ref):
        j, i = pl.program_id(1), pl.program_id(2)

        @pl.when((i == 0) & (j == 0))
        def _():
            dq_ref[...] = jnp.zeros_like(dq_ref)

        @pl.when(i == 0)
        def _():
            dkv_ref[...] = jnp.zeros_like(dkv_ref)
            dkr_ref[...] = jnp.zeros_like(dkr_ref)

        def block(sub, diagonal):
            sl = pl.ds(sub * t, t)
            qv, dov, kvv = q_ref[sl, :], do_ref[sl, :], kv_ref[...]
            kvr = jnp.concatenate([kvv, kr_ref[...]], axis=1)
            p = jnp.exp(_scores(qv, kvr, diagonal) - lse_ref[sl, :1])
            dp = lax.dot_general(dov, kvv, _NT, preferred_element_type=F32)
            delta = jnp.sum(dov.astype(F32) * o_ref[sl, :].astype(F32), axis=-1, keepdims=True)
            ds = (p * (dp - delta)).astype(BF16)
            dk = lax.dot_general(ds, qv, _TN, preferred_element_type=F32)
            dkv_ref[...] += lax.dot_general(p.astype(BF16), dov, _TN, preferred_element_type=F32) + dk[:, :LANES]
            dkr_ref[...] += dk[:, LANES:]
            rows = pl.ds(pl.multiple_of((2 * i + sub) * t, t), t)
            dq_ref[rows, :] += jnp.dot(ds, kvr, preferred_element_type=F32)

        _on_causal_pairs(i, j, block)

    qi = lambda h, j, i: (jnp.maximum(i, lax.div(j, 2)), h)
    kj = lambda h, j, i: (j, h)
    return pl.pallas_call(
        body, name="attn_bwd", grid=(N_HEADS, n, np_),
        in_specs=[pl.BlockSpec((2 * t, HEAD_PAD), qi), pl.BlockSpec((t, LANES), kj),
                  pl.BlockSpec((t, LANES), lambda h, j, i: (j, 0)),
                  pl.BlockSpec((2 * t, LANES), qi), pl.BlockSpec((2 * t, LANES), qi), pl.BlockSpec((2 * t, LANES), qi)],
        out_specs=(pl.BlockSpec((s, HEAD_PAD), lambda h, j, i: (0, h)),
                   pl.BlockSpec((t, LANES), kj), pl.BlockSpec((t, LANES), kj)),
        out_shape=(jax.ShapeDtypeStruct((s, N_HEADS * HEAD_PAD), F32),
                   jax.ShapeDtypeStruct((s, N_HEADS * LANES), F32), jax.ShapeDtypeStruct((s, N_HEADS * LANES), F32)),
        compiler_params=_cparams("parallel", "arbitrary", "arbitrary"),
    )(q, kv, kr, o, do, lse)


def ffn_fwd(x, g, wa, wb, cwa, cwb, ba, bb, wd, name):
    s, d = x.shape
    f = wa.shape[1]
    ts, tf = min(FFN_ROW_TILE, s), FFN_COL_TILE
    hal = FFN_HALO
    nj = f // tf
    first = hal - (FFN_K - 1)
    rc = min(FFN_ROW_CHUNK, ts)

    def body(x_ref, xp_ref, g_ref, wa_ref, wb_ref, cwa_ref, cwb_ref, ba_ref, bb_ref, wd_ref,
             xo_ref, xn_ref, hpa_ref, hpb_ref, act_ref, xe_ref, ea_ref, eb_ref):
        i, j = pl.program_id(0), pl.program_id(1)

        @pl.when(j == 0)
        def _():
            xn = _rms(x_ref[...], g_ref[...]).astype(BF16)
            xn_ref[...] = xn
            xe_ref[pl.ds(hal, ts), :] = xn
            xe_ref[pl.ds(0, hal), :] = jnp.where(i > 0, _rms(xp_ref[...], g_ref[...]), 0.0).astype(BF16)
            xo_ref[...] = x_ref[...]

        halves = ((0, ts // 2), (ts // 2, ts))
        for lo, hi in halves:
            e0, e1 = (0 if lo == 0 else hal + lo), hal + hi
            xe = xe_ref[pl.ds(e0, e1 - e0), :]
            ea_ref[pl.ds(e0, e1 - e0), :] = jnp.dot(xe, wa_ref[...], preferred_element_type=F32)
            eb_ref[pl.ds(e0, e1 - e0), :] = jnp.dot(xe, wb_ref[...], preferred_element_type=F32)
        for lo, hi in halves:
            hpa_ref[pl.ds(lo, hi - lo), :] = ea_ref[pl.ds(hal + lo, hi - lo), :].astype(BF16)
            hpb_ref[pl.ds(lo, hi - lo), :] = eb_ref[pl.ds(hal + lo, hi - lo), :].astype(BF16)
            for r0 in range(lo, hi, rc):
                ha = _conv_taps(ea_ref, cwa_ref, rc, first + r0, FFN_K) + ba_ref[...]
                hb = _conv_taps(eb_ref, cwb_ref, rc, first + r0, FFN_K) + bb_ref[...]
                act_ref[pl.ds(r0, rc), :] = (_silu(ha) * hb).astype(BF16)
            xo_ref[pl.ds(lo, hi - lo), :] += jnp.dot(act_ref[pl.ds(lo, hi - lo), :], wd_ref[...],
                                                     preferred_element_type=F32)

    r = ts // hal
    row = pl.BlockSpec((ts, d), lambda i, j: (i, 0))
    prev = pl.BlockSpec((hal, d), lambda i, j: (jnp.maximum(i * r - 1, 0), 0))
    gsp = pl.BlockSpec((1, d), lambda i, j: (0, 0))
    wup = pl.BlockSpec((d, tf), lambda i, j: (0, j))
    cwsp = pl.BlockSpec((8, tf), lambda i, j: (0, j))
    bsp = pl.BlockSpec((1, tf), lambda i, j: (0, j))
    wdn = pl.BlockSpec((tf, d), lambda i, j: (j, 0))
    hid = pl.BlockSpec((ts, tf), lambda i, j: (i, j))
    return pl.pallas_call(
        body, name=name, grid=(s // ts, nj),
        in_specs=[row, prev, gsp, wup, wup, cwsp, cwsp, bsp, bsp, wdn],
        out_specs=(row, row, hid, hid, hid),
        out_shape=(jax.ShapeDtypeStruct((s, d), F32), jax.ShapeDtypeStruct((s, d), BF16),
                   jax.ShapeDtypeStruct((s, f), BF16), jax.ShapeDtypeStruct((s, f), BF16),
                   jax.ShapeDtypeStruct((s, f), BF16)),
        scratch_shapes=[pltpu.VMEM((ts + hal, d), BF16), pltpu.VMEM((ts + hal, tf), F32),
                        pltpu.VMEM((ts + hal, tf), F32)],
        compiler_params=_cparams("parallel", "arbitrary"),
    )(x, x, g, wa, wb, cwa, cwb, ba, bb, wd)


def ffn_bwd(dy, hpa, hpb, wa, wb, cwa, cwb, ba, bb, wd, name):
    s, d = dy.shape
    f = hpa.shape[1]
    ts, tf = min(FFN_ROW_TILE, s), FFN_COL_TILE
    hal = FFN_HALO
    nt, nj = s // ts, f // tf
    te = ts + hal
    first = hal - (FFN_K - 1)
    rc = min(FFN_ROW_CHUNK, ts)

    def body(dy_ref, dyn_ref, a_ref, ap_ref, an_ref, b_ref, bp_ref, bn_ref, wa_ref, wb_ref, cwa_ref, cwb_ref,
             ba_ref, bb_ref, wd_ref,
             dxn_ref, dpa_ref, dpb_ref, dwa_ref, dwb_ref, dba_ref, dbb_ref,
             dye_ref, ea_ref, eb_ref, dact_ref, da_ref, db_ref):
        i, j = pl.program_id(0), pl.program_id(1)
        last = i == nt - 1

        @pl.when(j == 0)
        def _():
            dye_ref[pl.ds(0, ts), :] = dy_ref[...].astype(BF16)
            dye_ref[pl.ds(ts, hal), :] = jnp.where(last, 0.0, dyn_ref[...]).astype(BF16)
            dxn_ref[...] = jnp.zeros_like(dxn_ref)

        @pl.when((i == 0) & (j == 0))
        def _():
            for r in (dwa_ref, dwb_ref, dba_ref, dbb_ref):
                r[...] = jnp.zeros_like(r)

        halves = ((0, ts // 2), (ts // 2, ts))
        for lo, hi in halves:
            n = hi - lo + (hal if hi == ts else 0)
            dact_ref[pl.ds(lo, n), :] = lax.dot_general(dye_ref[pl.ds(lo, n), :], wd_ref[...], _NT,
                                                        preferred_element_type=F32)
        for cur, prev, nxt, ext in ((a_ref, ap_ref, an_ref, ea_ref), (b_ref, bp_ref, bn_ref, eb_ref)):
            ext[pl.ds(0, hal), :] = jnp.where(i > 0, prev[...].astype(F32), 0.0)
            ext[pl.ds(hal, ts), :] = cur[...].astype(F32)
            ext[pl.ds(hal + ts, hal), :] = jnp.where(last, 0.0, nxt[...].astype(F32))
        zero = jnp.zeros((1, tf), F32)
        sums = {"ba": zero, "bb": zero, **{("a", k): zero for k in range(FFN_K)}, **{("b", k): zero for k in range(FFN_K)}}
        for r0 in list(range(0, ts, rc)) + [ts]:
            n = rc if r0 < ts else hal
            win_a = [ea_ref[pl.ds(first + r0 + k, n), :] for k in range(FFN_K)]
            win_b = [eb_ref[pl.ds(first + r0 + k, n), :] for k in range(FFN_K)]
            ha = sum(cwa_ref[pl.ds(k, 1), :] * win_a[k] for k in range(FFN_K)) + ba_ref[...]
            hb = sum(cwb_ref[pl.ds(k, 1), :] * win_b[k] for k in range(FFN_K)) + bb_ref[...]
            sig = _sigmoid(ha)
            gs = dact_ref[pl.ds(r0, n), :] * sig
            dha = gs * hb * (1.0 + ha * (1.0 - sig))
            dhb = gs * ha
            da_ref[pl.ds(r0, n), :] = dha
            db_ref[pl.ds(r0, n), :] = dhb
            if r0 < ts:
                sums["ba"] = sums["ba"] + jnp.sum(dha, axis=0, keepdims=True)
                sums["bb"] = sums["bb"] + jnp.sum(dhb, axis=0, keepdims=True)
                for k in range(FFN_K):
                    sums["a", k] = sums["a", k] + jnp.sum(dha * win_a[k], axis=0, keepdims=True)
                    sums["b", k] = sums["b", k] + jnp.sum(dhb * win_b[k], axis=0, keepdims=True)
        for lo, hi in halves:
            for r0 in range(lo, hi, rc):
                dpa_ref[pl.ds(r0, rc), :] = _conv_taps(da_ref, cwa_ref, rc, r0, FFN_K, flip=True).astype(BF16)
                dpb_ref[pl.ds(r0, rc), :] = _conv_taps(db_ref, cwb_ref, rc, r0, FFN_K, flip=True).astype(BF16)
            rows = pl.ds(lo, hi - lo)
            dxn_ref[rows, :] += (lax.dot_general(dpa_ref[rows, :], wa_ref[...], _NT, preferred_element_type=F32)
                                 + lax.dot_general(dpb_ref[rows, :], wb_ref[...], _NT, preferred_element_type=F32))
        dba_ref[j] += jnp.broadcast_to(sums["ba"], (8, tf))
        dbb_ref[j] += jnp.broadcast_to(sums["bb"], (8, tf))
        row = lax.broadcasted_iota(jnp.int32, (8, tf), 0)
        dwa_ref[j] += sum(jnp.where(row == k, sums["a", k], 0.0) for k in range(FFN_K))
        dwb_ref[j] += sum(jnp.where(row == k, sums["b", k], 0.0) for k in range(FFN_K))

    r = ts // hal
    lastblk = s // hal - 1
    row = pl.BlockSpec((ts, d), lambda i, j: (i, 0))
    rown = pl.BlockSpec((hal, d), lambda i, j: (jnp.minimum((i + 1) * r, lastblk), 0))
    cur = pl.BlockSpec((ts, tf), lambda i, j: (i, j))
    prev = pl.BlockSpec((hal, tf), lambda i, j: (jnp.maximum(i * r - 1, 0), j))
    nxt = pl.BlockSpec((hal, tf), lambda i, j: (jnp.minimum((i + 1) * r, lastblk), j))
    wup = pl.BlockSpec((d, tf), lambda i, j: (0, j))
    cwsp = pl.BlockSpec((8, tf), lambda i, j: (0, j))
    bsp = pl.BlockSpec((1, tf), lambda i, j: (0, j))
    wdn = pl.BlockSpec((tf, d), lambda i, j: (j, 0))
    accsp = pl.BlockSpec((nj, 8, tf), lambda i, j: (0, 0, 0))
    accshape = jax.ShapeDtypeStruct((nj, 8, tf), F32)
    return pl.pallas_call(
        body, name=name, grid=(nt, nj),
        in_specs=[row, rown, cur, prev, nxt, cur, prev, nxt, wup, wup, cwsp, cwsp, bsp, bsp, wdn],
        out_specs=(row, cur, cur, accsp, accsp, accsp, accsp),
        out_shape=(jax.ShapeDtypeStruct((s, d), F32), jax.ShapeDtypeStruct((s, f), BF16),
                   jax.ShapeDtypeStruct((s, f), BF16), accshape, accshape, accshape, accshape),
        scratch_shapes=[pltpu.VMEM((te, d), BF16), pltpu.VMEM((ts + 2 * hal, tf), F32),
                        pltpu.VMEM((ts + 2 * hal, tf), F32), pltpu.VMEM((te, tf), F32),
                        pltpu.VMEM((te, tf), F32), pltpu.VMEM((te, tf), F32)],
        compiler_params=_cparams("arbitrary", "arbitrary"),
    )(dy, dy, hpa, hpa, hpa, hpb, hpb, hpb, wa, wb, cwa, cwb, ba, bb, wd)


NQ = 4
SQ = SSM_STATE * 8
NS = SSM_GROUPS * SSM_STATE


def _s5_disc(log_dt, a_re, a_im, b_re, b_im, expand):
    dt = jnp.exp(log_dt)
    mag = jnp.exp(a_re * dt)
    lb_re, lb_im = mag * jnp.cos(a_im * dt), mag * jnp.sin(a_im * dt)
    den = a_re * a_re + a_im * a_im
    nr, ni = lb_re - 1.0, lb_im
    f_re = (nr * a_re + ni * a_im) / den
    f_im = (ni * a_re - nr * a_im) / den
    fe_re = jnp.dot(f_re, expand, precision=lax.Precision.HIGHEST, preferred_element_type=F32)
    fe_im = jnp.dot(f_im, expand, precision=lax.Precision.HIGHEST, preferred_element_type=F32)
    return lb_re, lb_im, fe_re * b_re - fe_im * b_im, fe_re * b_im + fe_im * b_re


def _expand_matrix():
    e = np.zeros((SSM_STATE, SSM_STATE * SSM_GROUP), np.float32)
    for p in range(SSM_STATE):
        e[p, p * SSM_GROUP:(p + 1) * SSM_GROUP] = 1.0
    return jnp.asarray(e)


def s5_params_fwd(log_dt, a_re, a_im, b_re, b_im):
    expand = _expand_matrix()

    def body(ld_ref, ar_ref, ai_ref, br_ref, bi_ref, e_ref, lr_ref, li_ref, bbr_ref, bbi_ref):
        lr, li, bbr, bbi = _s5_disc(ld_ref[...], ar_ref[...], ai_ref[...], br_ref[...], bi_ref[...], e_ref[...])
        lr_ref[...] = lr
        li_ref[...] = li
        bbr_ref[...] = bbr
        bbi_ref[...] = bbi

    g, p, pc = SSM_GROUPS, SSM_STATE, SSM_STATE * SSM_GROUP
    return pl.pallas_call(
        body, name="s5_params_fwd",
        out_shape=(jax.ShapeDtypeStruct((g, p), F32), jax.ShapeDtypeStruct((g, p), F32),
                   jax.ShapeDtypeStruct((g, pc), F32), jax.ShapeDtypeStruct((g, pc), F32)),
    )(log_dt, a_re, a_im, b_re, b_im, expand)


def s5_params_bwd(log_dt, a_re, a_im, b_re, b_im, dlr, dli, dbbr, dbbi):
    expand = _expand_matrix()

    def body(ld_ref, ar_ref, ai_ref, br_ref, bi_ref, e_ref, dlr_ref, dli_ref, dbbr_ref, dbbi_ref,
             dld_ref, dar_ref, dai_ref, dbr_ref, dbi_ref):
        e = e_ref[...]
        f = lambda ld, ar, ai, br, bi: _s5_disc(ld, ar, ai, br, bi, e)
        _, vjp = jax.vjp(f, ld_ref[...], ar_ref[...], ai_ref[...], br_ref[...], bi_ref[...])
        dld, dar, dai, dbr, dbi = vjp((dlr_ref[...], dli_ref[...], dbbr_ref[...], dbbi_ref[...]))
        dld_ref[...] = dld
        dar_ref[...] = dar
        dai_ref[...] = dai
        dbr_ref[...] = dbr
        dbi_ref[...] = dbi

    g, p, pc = SSM_GROUPS, SSM_STATE, SSM_STATE * SSM_GROUP
    return pl.pallas_call(
        body, name="s5_params_bwd",
        out_shape=(jax.ShapeDtypeStruct((g, 1), F32), jax.ShapeDtypeStruct((g, p), F32),
                   jax.ShapeDtypeStruct((g, p), F32), jax.ShapeDtypeStruct((g, pc), F32),
                   jax.ShapeDtypeStruct((g, pc), F32)),
    )(log_dt, a_re, a_im, b_re, b_im, expand, dlr, dli, dbbr, dbbi)


def _cmul(ar, ai, br, bi):
    return ar * br - ai * bi, ar * bi + ai * br


def _power_rows(lr, li, conj_rev):
    row = lax.broadcasted_iota(jnp.int32, (8, NS), 0)
    tr = jnp.zeros((8, NS), F32)
    ti = jnp.zeros((8, NS), F32)
    pr, pi = lr, li
    for r in range(8):
        dst = 7 - r if conj_rev else r
        tr = jnp.where(row == dst, pr, tr)
        ti = jnp.where(row == dst, -pi if conj_rev else pi, ti)
        if r < 7:
            pr, pi = _cmul(pr, pi, lr, li)
    return tr, ti


def _scan8(xr, xi, tr_ref, ti_ref, cr, ci, reverse):
    row = lax.broadcasted_iota(jnp.int32, xr.shape, 0)
    for d in (1, 2, 4):
        if reverse:
            sr, si = pltpu.roll(xr, 8 - d, 0), pltpu.roll(xi, 8 - d, 0)
            keep = row < 8 - d
            pw = 8 - d
        else:
            sr, si = pltpu.roll(xr, d, 0), pltpu.roll(xi, d, 0)
            keep = row >= d
            pw = d - 1
        mr, mi = _cmul(tr_ref[pl.ds(pw, 1), :], ti_ref[pl.ds(pw, 1), :], sr, si)
        xr = xr + jnp.where(keep, mr, 0.0)
        xi = xi + jnp.where(keep, mi, 0.0)
    mr, mi = _cmul(tr_ref[...], ti_ref[...], cr, ci)
    return xr + mr, xi + mi


def _row_of(x, r):
    row = lax.broadcasted_iota(jnp.int32, x.shape, 0)
    return jnp.sum(jnp.where(row == r, x, 0.0), axis=0, keepdims=True)


def s5_scan_fwd(u, lam_r, lam_i, bre, bim, cre, cim, dskip):
    s = u.shape[0]
    tt = min(SCAN_TILE, s)
    nb = tt // 8

    def body(nt, u_ref, lr_ref, li_ref, bre_ref, bim_ref, cre_ref, cim_ref, d_ref,
             xr_ref, xi_ref, y_ref, yg_ref, tr_ref, ti_ref, cr_ref, ci_ref):
        i = pl.program_id(0)

        @pl.when(i == 0)
        def _():
            tr, ti = _power_rows(lr_ref[...], li_ref[...], False)
            tr_ref[...] = tr
            ti_ref[...] = ti
            cr_ref[...] = jnp.zeros_like(cr_ref)
            ci_ref[...] = jnp.zeros_like(ci_ref)

        uv = u_ref[...]
        ub = uv.astype(BF16)
        for q in range(NQ):
            uq = ub[:, q * LANES:(q + 1) * LANES]
            xr_ref[:, q * SQ:(q + 1) * SQ] = jnp.dot(uq, bre_ref[q], preferred_element_type=F32)
            xi_ref[:, q * SQ:(q + 1) * SQ] = jnp.dot(uq, bim_ref[q], preferred_element_type=F32)

        def step(b, carry):
            cr, ci = carry
            rows = pl.ds(pl.multiple_of(b * 8, 8), 8)
            xr, xi = _scan8(xr_ref[rows, :], xi_ref[rows, :], tr_ref, ti_ref, cr, ci, False)
            xr_ref[rows, :] = xr
            xi_ref[rows, :] = xi
            return _row_of(xr, 7), _row_of(xi, 7)

        cr, ci = lax.fori_loop(0, nb, step, (cr_ref[...], ci_ref[...]), unroll=min(SCAN_UNROLL, nb))
        cr_ref[...] = cr
        ci_ref[...] = ci
        y = d_ref[...] * uv
        for q in range(NQ):
            yq = (jnp.dot(xr_ref[:, q * SQ:(q + 1) * SQ].astype(BF16), cre_ref[q], preferred_element_type=F32)
                  - jnp.dot(xi_ref[:, q * SQ:(q + 1) * SQ].astype(BF16), cim_ref[q], preferred_element_type=F32))
            y_ref[:, q * LANES:(q + 1) * LANES] = yq + y[:, q * LANES:(q + 1) * LANES]
        yg_ref[...] = _gelu(y_ref[...]).astype(BF16)

    return rowcall(body, rows=s, ts=tt,
                   ins=[(u, "row"), (lam_r, "full"), (lam_i, "full"), (bre, "full"), (bim, "full"),
                        (cre, "full"), (cim, "full"), (dskip, "full")],
                   outs=[((s, NS), F32, "row"), ((s, NS), F32, "row"), ((s, SSM_WIDTH), F32, "row"),
                         ((s, SSM_WIDTH), BF16, "row")], name="s5_scan_fwd",
                   scratch=[pltpu.VMEM((8, NS), F32), pltpu.VMEM((8, NS), F32),
                            pltpu.VMEM((1, NS), F32), pltpu.VMEM((1, NS), F32)])


def s5_scan_bwd(dyg, y, lam_r, lam_i, cre, cim):
    s = y.shape[0]
    tt = min(SCAN_TILE, s)
    nb = tt // 8

    def body(nt, dyg_ref, y_ref, lr_ref, li_ref, cre_ref, cim_ref,
             ar_ref, ai_ref, dy_ref, tr_ref, ti_ref, cr_ref, ci_ref):
        i = pl.program_id(0)

        @pl.when(i == 0)
        def _():
            tr, ti = _power_rows(lr_ref[...], li_ref[...], True)
            tr_ref[...] = tr
            ti_ref[...] = ti
            cr_ref[...] = jnp.zeros_like(cr_ref)
            ci_ref[...] = jnp.zeros_like(ci_ref)

        _, vjp = jax.vjp(_gelu, y_ref[...])
        dy = vjp(dyg_ref[...])[0]
        dyb = dy.astype(BF16)
        dy_ref[...] = dyb
        for q in range(NQ):
            dq = dyb[:, q * LANES:(q + 1) * LANES]
            ar_ref[:, q * SQ:(q + 1) * SQ] = lax.dot_general(dq, cre_ref[q], _NT, preferred_element_type=F32)
            ai_ref[:, q * SQ:(q + 1) * SQ] = -lax.dot_general(dq, cim_ref[q], _NT, preferred_element_type=F32)

        def step(b, carry):
            cr, ci = carry
            rows = pl.ds(pl.multiple_of((nb - 1 - b) * 8, 8), 8)
            xr, xi = _scan8(ar_ref[rows, :], ai_ref[rows, :], tr_ref, ti_ref, cr, ci, True)
            ar_ref[rows, :] = xr
            ai_ref[rows, :] = xi
            return _row_of(xr, 0), _row_of(xi, 0)

        cr, ci = lax.fori_loop(0, nb, step, (cr_ref[...], ci_ref[...]), unroll=min(SCAN_UNROLL, nb))
        cr_ref[...] = cr
        ci_ref[...] = ci

    return rowcall(body, rows=s, ts=tt,
                   ins=[(dyg, "rev"), (y, "rev"), (lam_r, "full"), (lam_i, "full"), (cre, "full"), (cim, "full")],
                   outs=[((s, NS), F32, "rev"), ((s, NS), F32, "rev"), ((s, SSM_WIDTH), BF16, "rev")],
                   name="s5_scan_bwd",
                   scratch=[pltpu.VMEM((8, NS), F32), pltpu.VMEM((8, NS), F32),
                            pltpu.VMEM((1, NS), F32), pltpu.VMEM((1, NS), F32)])


def s5_grads(u, dy, xr, xi, ar, ai, bre, bim, dskip):
    s = u.shape[0]
    tt = min(SCAN_TILE, s)

    def body(nt, u_ref, dy_ref, xr_ref, xrp_ref, xi_ref, xip_ref, ar_ref, ai_ref, bre_ref, bim_ref, d_ref,
             du_ref, dlr_ref, dli_ref, dbr_ref, dbi_ref, dcr_ref, dci_ref, dd_ref, er_ref, ei_ref):
        i = pl.program_id(0)

        @pl.when(i == 0)
        def _():
            for r in (dbr_ref, dbi_ref, dcr_ref, dci_ref):
                r[...] = jnp.zeros_like(r)

        uv, dyb = u_ref[...], dy_ref[...]
        dyf = dyb.astype(F32)
        av_r, av_i, xv_r, xv_i = ar_ref[...], ai_ref[...], xr_ref[...], xi_ref[...]
        er_ref[pl.ds(0, 8), :] = jnp.where(i > 0, xrp_ref[...], 0.0)
        ei_ref[pl.ds(0, 8), :] = jnp.where(i > 0, xip_ref[...], 0.0)
        er_ref[pl.ds(8, tt), :] = xv_r
        ei_ref[pl.ds(8, tt), :] = xv_i
        sr, si = er_ref[pl.ds(7, tt), :], ei_ref[pl.ds(7, tt), :]
        _acc(dlr_ref, i, av_r * sr + av_i * si)
        _acc(dli_ref, i, av_i * sr - av_r * si)
        _acc(dd_ref, i, dyf * uv)
        ub = uv.astype(BF16)
        ab_r, ab_i = av_r.astype(BF16), av_i.astype(BF16)
        xb_r, xb_i = xv_r.astype(BF16), xv_i.astype(BF16)
        du = d_ref[...] * dyf
        for q in range(NQ):
            cs, ss = slice(q * LANES, (q + 1) * LANES), slice(q * SQ, (q + 1) * SQ)
            dbr_ref[q] += lax.dot_general(ub[:, cs], ab_r[:, ss], _TN, preferred_element_type=F32)
            dbi_ref[q] += lax.dot_general(ub[:, cs], ab_i[:, ss], _TN, preferred_element_type=F32)
            dcr_ref[q] += lax.dot_general(xb_r[:, ss], dyb[:, cs], _TN, preferred_element_type=F32)
            dci_ref[q] -= lax.dot_general(xb_i[:, ss], dyb[:, cs], _TN, preferred_element_type=F32)
            du_ref[:, cs] = (du[:, cs]
                             + lax.dot_general(ab_r[:, ss], bre_ref[q], _NT, preferred_element_type=F32)
                             + lax.dot_general(ab_i[:, ss], bim_ref[q], _NT, preferred_element_type=F32))

    return rowcall(body, rows=s, ts=tt,
                   ins=[(u, "row"), (dy, "row"), (xr, "row"), (xr, "prev:8"), (xi, "row"), (xi, "prev:8"),
                        (ar, "row"), (ai, "row"), (bre, "full"), (bim, "full"), (dskip, "full")],
                   outs=[((s, SSM_WIDTH), F32, "row"), ((8, NS), F32, "acc"), ((8, NS), F32, "acc"),
                         ((NQ, LANES, SQ), F32, "acc"), ((NQ, LANES, SQ), F32, "acc"),
                         ((NQ, SQ, LANES), F32, "acc"), ((NQ, SQ, LANES), F32, "acc"),
                         ((8, SSM_WIDTH), F32, "acc")], name="s5_grads",
                   scratch=[pltpu.VMEM((tt + 8, NS), F32), pltpu.VMEM((tt + 8, NS), F32)])


def _glu_fn(za, zb):
    return za * _sigmoid(zb)


def glu_res_fwd(z, xres):
    s = z.shape[0]

    def body(nt, z_ref, x_ref, o_ref):
        o_ref[...] = x_ref[...] + _glu_fn(z_ref[:, :D_MODEL].astype(F32), z_ref[:, D_MODEL:].astype(F32))

    return rowcall(body, rows=s, ts=min(ROW_TILE, s), ins=[(z, "row"), (xres, "row")],
                   outs=[((s, D_MODEL), F32, "row")], name="glu_res_fwd")[0]


def glu_bwd(z, dout):
    s, c = z.shape

    def body(nt, z_ref, d_ref, dz_ref, dba_ref, dbb_ref):
        i = pl.program_id(0)
        _, vjp = jax.vjp(_glu_fn, z_ref[:, :D_MODEL].astype(F32), z_ref[:, D_MODEL:].astype(F32))
        dza, dzb = vjp(d_ref[...])
        dz_ref[:, :D_MODEL] = dza.astype(BF16)
        dz_ref[:, D_MODEL:] = dzb.astype(BF16)
        _acc(dba_ref, i, dza)
        _acc(dbb_ref, i, dzb)

    return rowcall(body, rows=s, ts=min(ROW_TILE, s), ins=[(z, "row"), (dout, "row")],
                   outs=[((s, c), BF16, "row"), ((8, D_MODEL), F32, "acc"), ((8, D_MODEL), F32, "acc")],
                   name="glu_bwd")


def loss_head(x, g, target):
    s, c = x.shape

    def body(nt, x_ref, g_ref, t_ref, loss_ref, dx_ref, dg_ref):
        i = pl.program_id(0)
        y, vjp = jax.vjp(_rms, x_ref[...], g_ref[...])
        err = y - t_ref[...]
        dx, dg = vjp(err * (1.0 / c))
        dx_ref[...] = dx
        _acc(dg_ref, i, dg)
        part = jnp.sum(jnp.sum(err * err, axis=-1, keepdims=True), axis=0, keepdims=True) * (0.5 / c)

        @pl.when(i == 0)
        def _():
            loss_ref[...] = jnp.zeros_like(loss_ref)

        loss_ref[...] += jnp.broadcast_to(part, loss_ref.shape)

    return rowcall(body, rows=s, ts=min(ROW_TILE, s), ins=[(x, "row"), (g, "full"), (target, "row")],
                   outs=[((8, LANES), F32, "acc"), ((s, c), F32, "row"), ((8, c), F32, "acc")], name="loss_head")


def _tile_rows(r, cands=(512, 256, 128, 64, 32, 16, 8)):
    return _pick(r, cands)


def add_to_bf16(a, b, name):
    n, r, c = a.shape
    tr = _tile_rows(r)

    def body(a_ref, b_ref, o_ref):
        o_ref[...] = (a_ref[...].astype(F32) + b_ref[...].astype(F32)).astype(BF16)

    spec = pl.BlockSpec((1, tr, c), lambda j, i: (j, i, 0))
    return pl.pallas_call(body, name=name, grid=(n, r // tr), in_specs=[spec, spec], out_specs=spec,
                          out_shape=jax.ShapeDtypeStruct((n, r, c), BF16),
                          compiler_params=_cparams("parallel", "parallel"))(a, b)


def sum_leading(a, name):
    n, r, c = a.shape
    tr = _tile_rows(r)

    def body(a_ref, o_ref):
        acc = a_ref[0].astype(F32)
        for k in range(1, n):
            acc = acc + a_ref[k].astype(F32)
        o_ref[...] = acc

    return pl.pallas_call(body, name=name, grid=(r // tr,),
                          in_specs=[pl.BlockSpec((n, tr, c), lambda i: (0, i, 0))],
                          out_specs=pl.BlockSpec((tr, c), lambda i: (i, 0)),
                          out_shape=jax.ShapeDtypeStruct((r, c), F32),
                          compiler_params=_cparams("parallel"))(a)


def adamw(w, g, m, v, name):
    r, c = w.shape
    tr = _tile_rows(r, (256, 128, 64, 32, 16, 8))
    c1 = 1.0 - ADAM_B1 ** ADAM_STEP
    c2 = 1.0 - ADAM_B2 ** ADAM_STEP

    def body(w_ref, g_ref, m_ref, v_ref, d_ref, nm_ref, nv_ref):
        gv = g_ref[...]
        mn = ADAM_B1 * m_ref[...] + (1.0 - ADAM_B1) * gv
        vn = ADAM_B2 * v_ref[...] + (1.0 - ADAM_B2) * (gv * gv)
        d_ref[...] = -ADAM_LR * ((mn / c1) / (jnp.sqrt(vn / c2) + ADAM_EPS) + ADAM_WD * w_ref[...])
        nm_ref[...] = mn
        nv_ref[...] = vn

    spec = pl.BlockSpec((tr, c), lambda i: (i, 0))
    shp = jax.ShapeDtypeStruct((r, c), F32)
    return pl.pallas_call(body, name=name, grid=(r // tr,), in_specs=[spec] * 4, out_specs=(spec,) * 3,
                          out_shape=(shp,) * 3, compiler_params=_cparams("parallel"))(w, g, m, v)


_ANY = pl.BlockSpec(memory_space=pl.ANY)


def all_gather8(block, name):
    r, c = block.shape

    def body(x_ref, out_ref, send_sems, recv_sems, local_sem):
        x, y, cc = lax.axis_index("x"), lax.axis_index("y"), lax.axis_index("c")
        me, sibling = (x, y, cc), (x, y, 1 - cc)
        chips = [(1 - x, y), (x, 1 - y), (1 - x, 1 - y)]

        def slot(px, py, pc):
            return out_ref.at[4 * px + 2 * py + pc]

        def copy(k, blk, to, src=None):
            return pltpu.make_async_remote_copy(
                src_ref=slot(*blk) if src is None else src, dst_ref=slot(*blk),
                send_sem=send_sems.at[k], recv_sem=recv_sems.at[k], device_id=to, device_id_type=MESH)

        mine = pltpu.make_async_copy(x_ref, slot(*me), local_sem)
        mine.start()
        first = [copy(0, me, sibling, src=x_ref)]
        first += [copy(1 + j, me, (*chip, cc), src=x_ref) for j, chip in enumerate(chips)]
        for cp in first:
            cp.start()
        passed = [copy(4 + j, (*chip, cc), sibling) for j, chip in enumerate(chips)]
        for j, chip in enumerate(chips):
            copy(1 + j, (*chip, cc), me).wait_recv()
            passed[j].start()
        copy(0, sibling, me).wait_recv()
        for j, chip in enumerate(chips):
            copy(4 + j, (*chip, 1 - cc), me).wait_recv()
        for cp in first + passed:
            cp.wait_send()
        mine.wait()

    return pl.pallas_call(
        body, name=name, in_specs=[_ANY], out_specs=_ANY,
        out_shape=jax.ShapeDtypeStruct((8, r, c), block.dtype),
        scratch_shapes=[pltpu.SemaphoreType.DMA((7,)), pltpu.SemaphoreType.DMA((7,)), pltpu.SemaphoreType.DMA],
    )(block)


_HBM = pl.BlockSpec(memory_space=pltpu.HBM)
_SEM = pl.BlockSpec(memory_space=pltpu.SEMAPHORE)
_DATAFLOW = pltpu.SideEffectType.DATAFLOW_SIDE_EFFECTING
N_REMOTE = 6


def _remote_peers(x, y, cc):
    return [(1 - x, y, cc), (x, 1 - y, cc), (1 - x, 1 - y, cc),
            (1 - x, y, 1 - cc), (x, 1 - y, 1 - cc), (1 - x, 1 - y, 1 - cc)]


def gather_start(block, after, name):
    r, c = block.shape

    def body(x_ref, land_ref, after_ref, send_sems, recv_sems, x_thru, land_thru, token):
        x, y, cc = lax.axis_index("x"), lax.axis_index("y"), lax.axis_index("c")
        for k, peer in enumerate(_remote_peers(x, y, cc)):
            pltpu.make_async_remote_copy(src_ref=x_ref, dst_ref=land_ref.at[4 * x + 2 * y + cc],
                                         send_sem=send_sems.at[k], recv_sem=recv_sems.at[k],
                                         device_id=peer, device_id_type=MESH).start()
        token[...] = jnp.zeros_like(token)

    land = pltpu.with_memory_space_constraint(lax.empty((8, r, c), block.dtype), pltpu.HBM)
    return pl.pallas_call(
        body, name=name,
        out_shape=(pltpu.SemaphoreType.DMA((N_REMOTE,)), pltpu.SemaphoreType.DMA((N_REMOTE,)),
                   pltpu.HBM((r, c), block.dtype), pltpu.HBM((8, r, c), block.dtype),
                   jax.ShapeDtypeStruct((8, LANES), F32)),
        in_specs=(_HBM, _HBM, _ANY), out_specs=(_SEM, _SEM, _HBM, _HBM, pl.BlockSpec(memory_space=pltpu.VMEM)),
        input_output_aliases={0: 2, 1: 3},
        compiler_params=pltpu.CompilerParams(has_side_effects=_DATAFLOW),
    )(pltpu.with_memory_space_constraint(block, pltpu.HBM), land, after)


def gather_wait(send_sems, recv_sems, block_thru, land_thru, after, name):
    def body(x_ref, land_ref, send_sems, recv_sems, after_ref, x_dead, got_ref):
        x, y, cc = lax.axis_index("x"), lax.axis_index("y"), lax.axis_index("c")
        for k, (px, py, pc) in enumerate(_remote_peers(x, y, cc)):
            cp = pltpu.make_async_remote_copy(src_ref=x_ref, dst_ref=land_ref.at[4 * px + 2 * py + pc],
                                              send_sem=send_sems.at[k], recv_sem=recv_sems.at[k],
                                              device_id=(px, py, pc), device_id_type=MESH)
            cp.wait_send()
            cp.wait_recv()

    return pl.pallas_call(
        body, name=name,
        out_shape=(pltpu.HBM(block_thru.shape, block_thru.dtype), pltpu.HBM(land_thru.shape, land_thru.dtype)),
        in_specs=(_HBM, _HBM, _SEM, _SEM, _ANY), out_specs=(_HBM, _HBM), input_output_aliases={0: 0, 1: 1},
        compiler_params=pltpu.CompilerParams(has_side_effects=_DATAFLOW),
    )(block_thru, land_thru, send_sems, recv_sems, after)[1]


def sibling_swap(block, name):
    def body(x_ref, out_ref, send_sem, recv_sem):
        x, y, cc = lax.axis_index("x"), lax.axis_index("y"), lax.axis_index("c")
        cp = pltpu.make_async_remote_copy(src_ref=x_ref, dst_ref=out_ref, send_sem=send_sem, recv_sem=recv_sem,
                                          device_id=(x, y, 1 - cc), device_id_type=MESH)
        cp.start()
        cp.wait()

    return pl.pallas_call(
        body, name=name, in_specs=[_ANY], out_specs=_ANY,
        out_shape=jax.ShapeDtypeStruct(block.shape, block.dtype),
        scratch_shapes=[pltpu.SemaphoreType.DMA, pltpu.SemaphoreType.DMA],
    )(block)


def chip_exchange(parts, name):
    def body(p_ref, out_ref, send_sems, recv_sems, local_sem):
        x, y, cc = lax.axis_index("x"), lax.axis_index("y"), lax.axis_index("c")
        me = 2 * x + y
        chips = [(1 - x, y), (x, 1 - y), (1 - x, 1 - y)]
        mine = pltpu.make_async_copy(p_ref.at[me], out_ref.at[me], local_sem)
        mine.start()
        sends = []
        for k, (px, py) in enumerate(chips):
            sends.append(pltpu.make_async_remote_copy(
                src_ref=p_ref.at[2 * px + py], dst_ref=out_ref.at[me],
                send_sem=send_sems.at[k], recv_sem=recv_sems.at[k], device_id=(px, py, cc), device_id_type=MESH))
        for cp in sends:
            cp.start()
        for k, (px, py) in enumerate(chips):
            pltpu.make_async_remote_copy(
                src_ref=p_ref.at[me], dst_ref=out_ref.at[2 * px + py],
                send_sem=send_sems.at[k], recv_sem=recv_sems.at[k], device_id=(px, py, cc),
                device_id_type=MESH).wait_recv()
        for cp in sends:
            cp.wait_send()
        mine.wait()

    return pl.pallas_call(
        body, name=name, in_specs=[_ANY], out_specs=_ANY,
        out_shape=jax.ShapeDtypeStruct(parts.shape, parts.dtype),
        scratch_shapes=[pltpu.SemaphoreType.DMA((3,)), pltpu.SemaphoreType.DMA((3,)), pltpu.SemaphoreType.DMA],
    )(parts)


PACK_COLS = 1024
SHARDED = (("l0_w_in", 1), ("l0_w_uq", 1), ("l0_w_ukv", 1), ("l0_w_out", 0), ("l0_w_up", 1), ("l0_w_down", 0),
           ("l1_w_in", 0), ("l1_w_glu", 1), ("l1_w_up", 1), ("l1_w_down", 0),
           ("l0_conv_w", 1), ("l0_ffn_conv_w", 1), ("l1_ffn_conv_w", 1))
REPLICATED = ("l0_mix_norm", "l0_conv_b", "l0_conv_ln_g", "l0_conv_ln_b", "l0_q_norm", "l0_kv_norm", "l0_ffn_norm",
              "l0_ffn_conv_b", "l1_mix_norm", "l1_log_dt", "l1_a_re", "l1_a_im", "l1_b_re", "l1_b_im", "l1_c_re",
              "l1_c_im", "l1_d", "l1_b_glu", "l1_ffn_norm", "l1_ffn_conv_b", "final_norm")


def _pack(arrs, dtype, mult):
    flat = jnp.concatenate([a.reshape(-1).astype(dtype) for a in arrs])
    n = flat.shape[0]
    total = -(-n // mult) * mult
    return jnp.pad(flat, (0, total - n))


def _unpack(flat, shapes):
    out, pos = [], 0
    for shp in shapes:
        n = int(np.prod(shp))
        out.append(flat[pos:pos + n].reshape(shp))
        pos += n
    return out


PACK_ROW_ALIGN = 16


def _pack_rows(arrs, dtype, row_mult):
    parts = []
    for a in arrs:
        n = int(np.prod(a.shape))
        rows = -(-n // PACK_COLS)
        if n % PACK_COLS == 0:
            r = a.astype(dtype).reshape(rows, PACK_COLS)
        else:
            r = jnp.pad(a.reshape(-1).astype(dtype), (0, rows * PACK_COLS - n)).reshape(rows, PACK_COLS)
        parts.append(jnp.pad(r, ((0, (-rows) % PACK_ROW_ALIGN), (0, 0))))
    p = jnp.concatenate(parts)
    return jnp.pad(p, ((0, (-p.shape[0]) % row_mult), (0, 0)))


def _unpack_rows(pack, shapes):
    out, r0 = [], 0
    for shp in shapes:
        n = int(np.prod(shp))
        rows = -(-n // PACK_COLS)
        piece = lax.optimization_barrier(pack[r0:r0 + rows])
        out.append(piece.reshape(shp) if n % PACK_COLS == 0 else piece.reshape(-1)[:n].reshape(shp))
        r0 += rows + (-rows) % PACK_ROW_ALIGN
    return out


def _shard(full, axis, j):
    n = full.shape[axis] // N_CHIPS
    return lax.slice_in_dim(full, j * n, (j + 1) * n, axis=axis)


def _block_diag(t):
    q, g, a, b = t.shape
    eye = jnp.eye(g, dtype=t.dtype)
    return jnp.einsum("qgab,gh->qgahb", t, eye).reshape(q, g * a, g * b)


def _block_diag_t(d, a, b):
    q = d.shape[0]
    d5 = d.reshape(q, 8, a, 8, b)
    eye = jnp.eye(8, dtype=d.dtype)
    return jnp.einsum("qgahb,gh->qgab", d5, eye)


def kernel(x, l0_mix_norm, l0_w_in, l0_conv_w, l0_conv_b, l0_conv_ln_g, l0_conv_ln_b, l0_q_norm, l0_kv_norm, l0_w_uq, l0_w_ukv, l0_w_out, l0_ffn_norm, l0_w_up, l0_ffn_conv_w, l0_ffn_conv_b, l0_w_down, l1_mix_norm, l1_w_in, l1_log_dt, l1_a_re, l1_a_im, l1_b_re, l1_b_im, l1_c_re, l1_c_im, l1_d, l1_w_glu, l1_b_glu, l1_ffn_norm, l1_w_up, l1_ffn_conv_w, l1_ffn_conv_b, l1_w_down, final_norm, loss_target, m_l0_mix_norm, m_l0_w_in, m_l0_conv_w, m_l0_conv_b, m_l0_conv_ln_g, m_l0_conv_ln_b, m_l0_q_norm, m_l0_kv_norm, m_l0_w_uq, m_l0_w_ukv, m_l0_w_out, m_l0_ffn_norm, m_l0_w_up, m_l0_ffn_conv_w, m_l0_ffn_conv_b, m_l0_w_down, m_l1_mix_norm, m_l1_w_in, m_l1_log_dt, m_l1_a_re, m_l1_a_im, m_l1_b_re, m_l1_b_im, m_l1_c_re, m_l1_c_im, m_l1_d, m_l1_w_glu, m_l1_b_glu, m_l1_ffn_norm, m_l1_w_up, m_l1_ffn_conv_w, m_l1_ffn_conv_b, m_l1_w_down, m_final_norm, v_l0_mix_norm, v_l0_w_in, v_l0_conv_w, v_l0_conv_b, v_l0_conv_ln_g, v_l0_conv_ln_b, v_l0_q_norm, v_l0_kv_norm, v_l0_w_uq, v_l0_w_ukv, v_l0_w_out, v_l0_ffn_norm, v_l0_w_up, v_l0_ffn_conv_w, v_l0_ffn_conv_b, v_l0_w_down, v_l1_mix_norm, v_l1_w_in, v_l1_log_dt, v_l1_a_re, v_l1_a_im, v_l1_b_re, v_l1_b_im, v_l1_c_re, v_l1_c_im, v_l1_d, v_l1_w_glu, v_l1_b_glu, v_l1_ffn_norm, v_l1_w_up, v_l1_ffn_conv_w, v_l1_ffn_conv_b, v_l1_w_down, v_final_norm):
    a = dict(locals())
    w = {n: a[n] for n in [s for s, _ in SHARDED] + list(REPLICATED)}
    mom = {n: a["m_" + n] for n in w}
    var = {n: a["v_" + n] for n in w}
    return _step(a["x"][0], a["loss_target"][0], w, mom, var)


FIRST_WEIGHTS = ("l0_w_in", "l0_w_uq", "l0_w_ukv", "l0_w_out")
LATER_WEIGHTS = ("l0_w_up", "l0_w_down", "l1_w_in", "l1_w_glu", "l1_w_up", "l1_w_down")


def _assemble(got, names, w):
    got = got.reshape(N_CHIPS, -1, PACK_COLS)
    shapes = [w[n].shape for n in names]
    per_chip = [_unpack_rows(got[j], shapes) for j in range(N_CHIPS)]
    axes = dict(SHARDED)
    return {n: jnp.concatenate([per_chip[j][k] for j in range(N_CHIPS)], axis=axes[n]) for k, n in enumerate(names)}


def _gather_weights(w):
    cc = lax.axis_index("c")
    small = [n for n, _ in SHARDED[10:]]
    full = {}
    for names, dtype, mult in ((FIRST_WEIGHTS, BF16, 2 * 256), (small, F32, 2 * PACK_ROW_ALIGN)):
        pack = _pack_rows([w[n] for n in names], dtype, mult)
        half = lax.dynamic_index_in_dim(pack.reshape(2, -1, PACK_COLS), cc, axis=0, keepdims=False)
        got = all_gather8(half, "gather_" + ("first_matrices" if dtype == BF16 else "conv_weights"))
        full.update(_assemble(got, names, w))
    pack = _pack_rows([w[n] for n in LATER_WEIGHTS], BF16, 2 * 256).reshape(2, -1, PACK_COLS)
    half = lax.dynamic_index_in_dim(pack, cc, axis=0, keepdims=False)
    send_sems, recv_sems, half_thru, land_thru, token = gather_start(half, got, "gather_later_start")
    return full, (send_sems, recv_sems, half_thru, land_thru, pack), token[0, 0]


def _finish_gather(pending, after, w):
    send_sems, recv_sems, half_thru, land_thru, pack = pending
    got = gather_wait(send_sems, recv_sems, half_thru, land_thru, after, "gather_later_wait")
    chip = 2 * lax.axis_index("x") + lax.axis_index("y")
    got = lax.dynamic_update_slice(got, pack, (2 * chip, 0, 0))
    return _assemble(got, LATER_WEIGHTS, w)


def _reduce_sharded(grads):
    cc = lax.axis_index("c")
    names = [n for n, _ in SHARDED]
    axes = dict(SHARDED)
    packs = [_pack_rows([_shard(grads[n], axes[n], j) for n in names], BF16, 2 * 256) for j in range(N_CHIPS)]
    g = jnp.stack(packs).reshape(N_CHIPS, 2, -1, PACK_COLS)
    keep = lax.dynamic_index_in_dim(g, cc, axis=1, keepdims=False)
    give = lax.dynamic_index_in_dim(g, 1 - cc, axis=1, keepdims=False)
    got = sibling_swap(give, "grad_swap_halves")
    parts = add_to_bf16(keep, got, "grad_add_sibling")
    landed = chip_exchange(parts, "grad_chip_exchange")
    mine = sum_leading(landed, "grad_sum_chips")
    theirs = sibling_swap(mine, "grad_swap_sums")
    lo = jnp.where(cc == 0, mine, theirs)
    hi = jnp.where(cc == 0, theirs, mine)
    shapes = [_shard(grads[n], axes[n], 0).shape for n in names]
    return dict(zip(names, _unpack_rows(jnp.concatenate([lo, hi]), shapes)))


def _reduce_replicated(grads):
    names = list(REPLICATED)
    flat = _pack([grads[n] for n in names], F32, 256 * LANES).reshape(-1, LANES)
    got = all_gather8(flat, "gather_small_grads")
    tot = sum_leading(got, "sum_small_grads").reshape(-1)
    return dict(zip(names, _unpack(tot, [grads[n].shape for n in names]))), flat.shape


def _row(v):
    return v.reshape(1, -1).astype(F32)


def _pad_rows(wt, rows):
    return jnp.pad(wt.astype(F32), ((0, rows - wt.shape[0]), (0, 0)))


def _ffn_fwd(xin, g, wa, wb, cw, cb, wd, tag):
    cwa, cwb = _pad_rows(cw[:, :D_FF], 8), _pad_rows(cw[:, D_FF:], 8)
    xout, xn, hpa, hpb, act = ffn_fwd(xin, _row(g), wa, wb, cwa, cwb, _row(cb[:D_FF]), _row(cb[D_FF:]), wd, tag)
    return xout, (xin, xn, hpa, hpb, act)


def _ffn_bwd(dxout, saved, g, wa, wb, cw, cb, wd, tag):
    xin, xn, hpa, hpb, act = saved
    d_wd = matmul(act, dxout, ta=True, name=f"{tag}_d_wdown")
    cwa, cwb = _pad_rows(cw[:, :D_FF], 8), _pad_rows(cw[:, D_FF:], 8)
    dxn, dpa, dpb, dwa, dwb, dba, dbb = ffn_bwd(dxout, hpa, hpb, wa, wb, cwa, cwb,
                                                _row(cb[:D_FF]), _row(cb[D_FF:]), wd, tag + "_bwd")
    d_wu = jnp.concatenate([matmul(xn, dpa, ta=True, name=f"{tag}_d_wup_a"),
                            matmul(xn, dpb, ta=True, name=f"{tag}_d_wup_b")], axis=1)
    dxin, dg = rms_bwd(xin, _row(g), dxn, dxout, f"{tag}_rms_bwd")
    taps = lambda t: t.transpose(1, 0, 2).reshape(8, -1)
    d_cw = jnp.concatenate([taps(dwa)[:FFN_K], taps(dwb)[:FFN_K]], axis=1)
    d_cb = jnp.concatenate([taps(dba)[0], taps(dbb)[0]])
    return dxin, dg[0], d_wu, d_cw, d_cb, d_wd


def _step(x, target, w, mom, var):
    s = x.shape[0]
    full, pending, zero = _gather_weights(w)
    cos, sin = rope_tables(s)

    w_in0 = full["l0_w_in"]
    w_in0p = jnp.concatenate([w_in0, jnp.zeros((D_MODEL, H0_W - w_in0.shape[1]), BF16)], axis=1)
    wq = full["l0_w_uq"].reshape(Q_LORA, N_HEADS, QK_NOPE + QK_ROPE)
    zq = lambda n: jnp.zeros((Q_LORA, N_HEADS, n), BF16)
    w_uqp = jnp.concatenate([wq[..., :QK_NOPE], zq(LANES - QK_NOPE), wq[..., QK_NOPE:], zq(LANES - QK_ROPE)],
                            axis=-1).reshape(Q_LORA, N_HEADS * HEAD_PAD)
    w_ukv = full["l0_w_ukv"]
    w_out = full["l0_w_out"]
    w_out_u = w_out[:CONV_WIDTH]
    wo = w_out[CONV_WIDTH:].reshape(N_HEADS, V_DIM, D_MODEL)
    w_out_a = jnp.concatenate([jnp.zeros_like(wo), wo], axis=1).reshape(N_HEADS * LANES, D_MODEL)
    conv_w = _pad_rows(full["l0_conv_w"], CONV_HALO)

    xn0 = rms_fwd(x, _row(w["l0_mix_norm"]) + zero, "l0_mix_rms")
    h0 = matmul(xn0, w_in0p, name="l0_in_proj")
    qn_g, kvn_g = _row(w["l0_q_norm"]), _row(w["l0_kv_norm"])
    u0, cq, ckv, kr = mixpre_fwd(h0, qn_g, kvn_g, cos, sin)
    cb, lg, lb = _row(w["l0_conv_b"]), _row(w["l0_conv_ln_g"]), _row(w["l0_conv_ln_b"])
    u = convln_fwd(u0, conv_w, cb, lg, lb)
    qraw = matmul(cq, w_uqp, name="l0_q_up")
    q = qrope_fwd(qraw, cos, sin)
    kv = matmul(ckv, w_ukv, out_dtype=BF16, name="l0_kv_up")
    o, lse = attn_fwd(q, kv, kr)
    x1 = matmul(u, w_out_u, res=x, name="l0_out_conv")
    x1 = matmul(o, w_out_a, res=x1, name="l0_out_attn")
    full.update(_finish_gather(pending, x1, w))
    w_up0a, w_up0b = full["l0_w_up"][:, :D_FF], full["l0_w_up"][:, D_FF:]
    w_up1a, w_up1b = full["l1_w_up"][:, :D_FF], full["l1_w_up"][:, D_FF:]

    x2, ffn0 = _ffn_fwd(x1, w["l0_ffn_norm"], w_up0a, w_up0b, full["l0_ffn_conv_w"], w["l0_ffn_conv_b"],
                        full["l0_w_down"], "l0_ffn")

    g_, p_, c_ = SSM_GROUPS, SSM_STATE, SSM_GROUP
    s5_in = (w["l1_log_dt"].reshape(g_, 1), w["l1_a_re"], w["l1_a_im"],
             w["l1_b_re"].reshape(g_, p_ * c_), w["l1_b_im"].reshape(g_, p_ * c_))
    lam_r, lam_i, bb_r, bb_i = s5_params_fwd(*s5_in)
    lam_rf, lam_if = lam_r.reshape(1, NS), lam_i.reshape(1, NS)

    def b_blocks(bb):
        t = bb.reshape(NQ, 8, p_, c_).transpose(0, 1, 3, 2)
        return _block_diag(t).astype(BF16)

    def c_blocks(cm):
        t = cm.reshape(NQ, 8, c_, p_).transpose(0, 1, 3, 2)
        return _block_diag(t).astype(BF16)

    bre, bim = b_blocks(bb_r), b_blocks(bb_i)
    cre, cim = c_blocks(w["l1_c_re"]), c_blocks(w["l1_c_im"])
    dskip = _row(w["l1_d"])
    xn2 = rms_fwd(x2, _row(w["l1_mix_norm"]), "l1_mix_rms")
    u1 = matmul(xn2, full["l1_w_in"], name="l1_in_proj")
    xs_r, xs_i, y1, yg = s5_scan_fwd(u1, lam_rf, lam_if, bre, bim, cre, cim, dskip)
    z = matmul(yg, full["l1_w_glu"], bias=_row(w["l1_b_glu"]), out_dtype=BF16, name="l1_glu_proj")
    x3 = glu_res_fwd(z, x2)

    x4, ffn1 = _ffn_fwd(x3, w["l1_ffn_norm"], w_up1a, w_up1b, full["l1_ffn_conv_w"], w["l1_ffn_conv_b"],
                        full["l1_w_down"], "l1_ffn")
    loss_part, dx4, dgf = loss_head(x4, _row(w["final_norm"]), target)
    loss = lax.psum(loss_part[0, 0], ("x", "y", "c"))

    gr = {"final_norm": dgf[0]}

    dx3, gr["l1_ffn_norm"], gr["l1_w_up"], gr["l1_ffn_conv_w"], gr["l1_ffn_conv_b"], gr["l1_w_down"] = _ffn_bwd(
        dx4, ffn1, w["l1_ffn_norm"], w_up1a, w_up1b, full["l1_ffn_conv_w"], w["l1_ffn_conv_b"], full["l1_w_down"],
        "l1_ffn")

    dz, dbga, dbgb = glu_bwd(z, dx3)
    gr["l1_b_glu"] = jnp.concatenate([dbga[0], dbgb[0]])
    dyg = matmul(dz, full["l1_w_glu"], tb=True, name="l1_d_yg")
    gr["l1_w_glu"] = matmul(yg, dz, ta=True, name="l1_d_wglu")
    a_r, a_i, dy1 = s5_scan_bwd(dyg, y1, lam_rf, lam_if, cre, cim)
    du1, dlr, dli, dbr, dbi, dcr, dci, dd = s5_grads(u1, dy1, xs_r, xs_i, a_r, a_i, bre, bim, dskip)
    gr["l1_d"] = dd[0]

    def b_unblock(d):
        return _block_diag_t(d, c_, p_).transpose(0, 1, 3, 2).reshape(g_, p_ * c_)

    def c_unblock(d):
        return _block_diag_t(d, p_, c_).transpose(0, 1, 3, 2).reshape(g_, c_, p_)

    gr["l1_c_re"], gr["l1_c_im"] = c_unblock(dcr), c_unblock(dci)
    dld, dar, dai, dbre, dbim = s5_params_bwd(*s5_in, dlr[0].reshape(g_, p_), dli[0].reshape(g_, p_),
                                              b_unblock(dbr), b_unblock(dbi))
    gr["l1_log_dt"], gr["l1_a_re"], gr["l1_a_im"] = dld.reshape(g_), dar, dai
    gr["l1_b_re"], gr["l1_b_im"] = dbre.reshape(g_, p_, c_), dbim.reshape(g_, p_, c_)
    dxn2 = matmul(du1, full["l1_w_in"], tb=True, name="l1_d_xn")
    gr["l1_w_in"] = matmul(xn2, du1, ta=True, name="l1_d_win")
    dx2, dg = rms_bwd(x2, _row(w["l1_mix_norm"]), dxn2, dx3, "l1_mix_rms_bwd")
    gr["l1_mix_norm"] = dg[0]

    dx1, gr["l0_ffn_norm"], gr["l0_w_up"], gr["l0_ffn_conv_w"], gr["l0_ffn_conv_b"], gr["l0_w_down"] = _ffn_bwd(
        dx2, ffn0, w["l0_ffn_norm"], w_up0a, w_up0b, full["l0_ffn_conv_w"], w["l0_ffn_conv_b"], full["l0_w_down"],
        "l0_ffn")

    du = matmul(dx1, w_out_u, tb=True, out_dtype=BF16, name="l0_d_u")
    do = matmul(dx1, w_out_a, tb=True, out_dtype=BF16, name="l0_d_o")
    d_wout_u = matmul(u, dx1, ta=True, name="l0_d_wout_u")
    d_wout_a = matmul(o, dx1, ta=True, name="l0_d_wout_a")
    gr["l0_w_out"] = jnp.concatenate(
        [d_wout_u, d_wout_a.reshape(N_HEADS, LANES, D_MODEL)[:, LANES - V_DIM:].reshape(N_HEADS * V_DIM, D_MODEL)])
    dq, dkv, dkr = attn_bwd(q, kv, kr, o, do, lse)
    dqraw = qrope_bwd(dq, cos, sin)
    dcq = matmul(dqraw, w_uqp, tb=True, name="l0_d_cq")
    d_wuqp = matmul(cq, dqraw, ta=True, name="l0_d_wuq").reshape(Q_LORA, N_HEADS, HEAD_PAD)
    gr["l0_w_uq"] = jnp.concatenate([d_wuqp[..., :QK_NOPE], d_wuqp[..., LANES:LANES + QK_ROPE]],
                                    axis=-1).reshape(Q_LORA, -1)
    dckv = matmul(dkv, w_ukv, tb=True, name="l0_d_ckv")
    gr["l0_w_ukv"] = matmul(ckv, dkv, ta=True, name="l0_d_wukv")
    du1c, dlg, dlb, dcb = convln_bwd1(u0, conv_w, cb, lg, lb, du)
    gr["l0_conv_ln_g"], gr["l0_conv_ln_b"], gr["l0_conv_b"] = dlg[0], dlb[0], dcb[0]
    du0, dcw = convln_bwd2(u0, conv_w, du1c)
    gr["l0_conv_w"] = dcw[:CONV_K]
    dh0, dqn, dkvn = mixpre_bwd(h0, qn_g, kvn_g, cos, sin, du0, dcq, dckv, dkr)
    gr["l0_q_norm"], gr["l0_kv_norm"] = dqn[0], dkvn[0]
    dxn0 = matmul(dh0, w_in0p, tb=True, name="l0_d_xn")
    gr["l0_w_in"] = matmul(xn0, dh0, ta=True, name="l0_d_win")[:, :w_in0.shape[1]]
    grad_x, dg = rms_bwd(x, _row(w["l0_mix_norm"]), dxn0, dx1, "l0_mix_rms_bwd")
    gr["l0_mix_norm"] = dg[0]

    g_sh = _reduce_sharded(gr)
    g_rep, pack_shape = _reduce_replicated(gr)
    grad, delta, new_m, new_v = {}, {}, {}, {}
    for n, _ in SHARDED:
        shp = w[n].shape
        two_d = (lambda t: t.reshape(shp[0], -1))
        grad[n] = g_sh[n]
        delta[n], new_m[n], new_v[n] = adamw(two_d(w[n]), two_d(g_sh[n]), two_d(mom[n]), two_d(var[n]), f"adamw_{n}")
    names = list(REPLICATED)
    pk = lambda d: _pack([d[n] for n in names], F32, 256 * LANES).reshape(pack_shape)
    dl, nm, nv = adamw(pk(w), pk(g_rep), pk(mom), pk(var), "adamw_small")
    shapes = [w[n].shape for n in names]
    for n, d_, m_, v_ in zip(names, _unpack(dl.reshape(-1), shapes), _unpack(nm.reshape(-1), shapes),
                             _unpack(nv.reshape(-1), shapes)):
        grad[n], delta[n], new_m[n], new_v[n] = g_rep[n], d_, m_, v_

    order = ["l0_mix_norm", "l0_w_in", "l0_conv_w", "l0_conv_b", "l0_conv_ln_g", "l0_conv_ln_b", "l0_q_norm",
             "l0_kv_norm", "l0_w_uq", "l0_w_ukv", "l0_w_out", "l0_ffn_norm", "l0_w_up", "l0_ffn_conv_w",
             "l0_ffn_conv_b", "l0_w_down", "l1_mix_norm", "l1_w_in", "l1_log_dt", "l1_a_re", "l1_a_im", "l1_b_re",
             "l1_b_im", "l1_c_re", "l1_c_im", "l1_d", "l1_w_glu", "l1_b_glu", "l1_ffn_norm", "l1_w_up",
             "l1_ffn_conv_w", "l1_ffn_conv_b", "l1_w_down", "final_norm"]
    return (loss, grad_x[None], *[grad[n] for n in order], *[delta[n] for n in order],
            *[new_m[n] for n in order], *[new_v[n] for n in order])
```

```python
import functools
import math

import jax
import jax.numpy as jnp
import numpy as np
from jax import lax
from jax.experimental import pallas as pl
from jax.experimental.pallas import tpu as pltpu

F32 = jnp.float32
BF16 = jnp.bfloat16
MESH = pl.DeviceIdType.MESH

D_MODEL = 1024
EPS = 1e-6
LN_EPS = 1e-5
CONV_WIDTH = 512
CONV_K = 31
N_HEADS = 8
QK_NOPE = 64
QK_ROPE = 32
V_DIM = 64
Q_LORA = 256
KV_LORA = 128
ROPE_BASE = 10000.0
ATT_SCALE = (QK_NOPE + QK_ROPE) ** -0.5
SSM_WIDTH = 512
SSM_GROUP = 16
SSM_GROUPS = 32
SSM_STATE = 64
D_FF = 2816
FFN_K = 3
ADAM_LR = 0.001
ADAM_B1 = 0.9
ADAM_B2 = 0.999
ADAM_EPS = 1e-08
ADAM_WD = 0.01
ADAM_STEP = 10

N_CHIPS = 4
LANES = 128
HEAD_PAD = 256
CONV_HALO = 32
FFN_HALO = 16
VMEM_LIMIT = 56 * 1024 * 1024

ROW_TILE = 512
FFN_ROW_TILE = 1024
FFN_COL_TILE = 256
FFN_ROW_CHUNK = 64
CONV_ROW_CHUNK = 32
ATT_TILE = 1024
SCAN_TILE = 256
SCAN_UNROLL = 4


def _cparams(*sem):
    return pltpu.CompilerParams(dimension_semantics=tuple(sem), vmem_limit_bytes=VMEM_LIMIT)


def _pick(n, cands):
    for c in cands:
        if n % c == 0:
            return c
    return n


def matmul(a, b, *, ta=False, tb=False, res=None, bias=None, out_dtype=None, name):
    if out_dtype is None:
        out_dtype = BF16 if ta else F32
    if ta:
        kdim, m = a.shape
    else:
        m, kdim = a.shape
    if tb:
        n, k2 = b.shape
    else:
        k2, n = b.shape
    assert kdim == k2, (a.shape, b.shape, ta, tb)
    tn = _pick(n, (1408, 1024, 768, 512, 384, 256, 128))
    if ta:
        tm = _pick(m, (1408, 1024, 512, 256, 128))
        tk = _pick(kdim, (512, 256, 128))
    else:
        tm = _pick(m, (1024, 512, 256, 128))
        tk = kdim
        if kdim > 1024:
            tn = _pick(n, (512, 256, 128))
        if tm * tn > 1024 * 1024 and out_dtype == F32:
            tm = _pick(m, (512, 256, 128))
    nk = kdim // tk
    has_res, has_bias = res is not None, bias is not None
    dims = (((0,) if ta else (1,), (1,) if tb else (0,)), ((), ()))

    def body(*refs):
        a_ref, b_ref = refs[0], refs[1]
        pos = 2
        res_ref = bias_ref = None
        if has_res:
            res_ref = refs[pos]
            pos += 1
        if has_bias:
            bias_ref = refs[pos]
            pos += 1
        o_ref = refs[pos]

        def finish(r):
            if has_bias:
                r = r + bias_ref[...]
            if has_res:
                r = r + res_ref[...].astype(F32)
            o_ref[...] = r.astype(o_ref.dtype)

        prod = lax.dot_general(a_ref[...].astype(BF16), b_ref[...].astype(BF16), dims, preferred_element_type=F32)
        if nk == 1:
            finish(prod)
            return
        acc_ref = refs[pos + 1]
        k = pl.program_id(2)

        @pl.when(k == 0)
        def _():
            acc_ref[...] = prod

        @pl.when(k > 0)
        def _():
            acc_ref[...] += prod

        @pl.when(k == nk - 1)
        def _():
            finish(acc_ref[...])

    a_spec = pl.BlockSpec((tk, tm), lambda i, j, k: (k, i)) if ta else pl.BlockSpec((tm, tk), lambda i, j, k: (i, k))
    b_spec = pl.BlockSpec((tn, tk), lambda i, j, k: (j, k)) if tb else pl.BlockSpec((tk, tn), lambda i, j, k: (k, j))
    in_specs = [a_spec, b_spec]
    args = [a, b]
    if has_res:
        in_specs.append(pl.BlockSpec((tm, tn), lambda i, j, k: (i, j)))
        args.append(res)
    if has_bias:
        in_specs.append(pl.BlockSpec((1, tn), lambda i, j, k: (0, j)))
        args.append(bias)
    return pl.pallas_call(
        body, name=name, grid=(m // tm, n // tn, nk),
        in_specs=in_specs, out_specs=pl.BlockSpec((tm, tn), lambda i, j, k: (i, j)),
        out_shape=jax.ShapeDtypeStruct((m, n), out_dtype),
        scratch_shapes=[pltpu.VMEM((tm, tn), F32)] if nk > 1 else [],
        compiler_params=_cparams("parallel", "parallel", "arbitrary"),
    )(*args)


def rowcall(body, *, rows, ts, ins, outs, name, scratch=()):
    nt = rows // ts
    in_specs, args = [], []
    for arr, kind in ins:
        if kind == "row":
            in_specs.append(pl.BlockSpec((ts, arr.shape[1]), lambda i: (i, 0)))
        elif kind == "rev":
            in_specs.append(pl.BlockSpec((ts, arr.shape[1]), lambda i: (nt - 1 - i, 0)))
        elif kind == "full":
            nd = arr.ndim
            in_specs.append(pl.BlockSpec(arr.shape, lambda i, nd=nd: (0,) * nd))
        elif kind.startswith("prev:"):
            h = int(kind[5:])
            r = ts // h
            in_specs.append(pl.BlockSpec((h, arr.shape[1]), lambda i, r=r: (jnp.maximum(i * r - 1, 0), 0)))
        elif kind.startswith("next:"):
            h = int(kind[5:])
            r = ts // h
            last = rows // h - 1
            in_specs.append(pl.BlockSpec((h, arr.shape[1]), lambda i, r=r, last=last: (jnp.minimum((i + 1) * r, last), 0)))
        elif kind.startswith("revprev:"):
            h = int(kind[8:])
            r = ts // h
            in_specs.append(pl.BlockSpec((h, arr.shape[1]), lambda i, r=r: (jnp.maximum((nt - 1 - i) * r - 1, 0), 0)))
        else:
            raise ValueError(kind)
        args.append(arr)
    out_specs, out_shapes = [], []
    for shape, dtype, kind in outs:
        if kind == "row":
            out_specs.append(pl.BlockSpec((ts, shape[1]), lambda i: (i, 0)))
        elif kind == "rev":
            out_specs.append(pl.BlockSpec((ts, shape[1]), lambda i: (nt - 1 - i, 0)))
        else:
            nd = len(shape)
            out_specs.append(pl.BlockSpec(tuple(shape), lambda i, nd=nd: (0,) * nd))
        out_shapes.append(jax.ShapeDtypeStruct(tuple(shape), dtype))
    return pl.pallas_call(
        functools.partial(body, nt), name=name, grid=(nt,),
        in_specs=in_specs, out_specs=tuple(out_specs), out_shape=tuple(out_shapes),
        scratch_shapes=list(scratch),
        compiler_params=_cparams("arbitrary"),
    )(*args)


def _rms(x, g):
    return x * lax.rsqrt(jnp.mean(x * x, axis=-1, keepdims=True) + EPS) * g


def _layer_norm(x, g, b):
    mu = jnp.mean(x, axis=-1, keepdims=True)
    xc = x - mu
    var = jnp.mean(xc * xc, axis=-1, keepdims=True)
    return xc * lax.rsqrt(var + LN_EPS) * g + b


def _sigmoid(x):
    return 1.0 / (1.0 + jnp.exp(-x))


def _silu(x):
    return x * _sigmoid(x)


def _gelu(x):
    return 0.5 * x * (1.0 + jnp.tanh(math.sqrt(2.0 / math.pi) * (x + 0.044715 * (x * x * x))))


def _acc(ref, i, val):
    s = jnp.sum(val, axis=0, keepdims=True)

    @pl.when(i == 0)
    def _():
        ref[...] = jnp.zeros_like(ref)

    ref[...] += jnp.broadcast_to(s, ref.shape)


def rms_fwd(x, g, name):
    s, c = x.shape

    def body(nt, x_ref, g_ref, o_ref):
        o_ref[...] = _rms(x_ref[...], g_ref[...]).astype(BF16)

    return rowcall(body, rows=s, ts=min(ROW_TILE, s), ins=[(x, "row"), (g, "full")],
                   outs=[((s, c), BF16, "row")], name=name)[0]


def rms_bwd(x, g, dxn, dres, name):
    s, c = x.shape

    def body(nt, x_ref, g_ref, d_ref, r_ref, dx_ref, dg_ref):
        i = pl.program_id(0)
        _, vjp = jax.vjp(_rms, x_ref[...], g_ref[...])
        dx, dg = vjp(d_ref[...].astype(F32))
        dx_ref[...] = dx + r_ref[...]
        _acc(dg_ref, i, dg)

    return rowcall(body, rows=s, ts=min(ROW_TILE, s),
                   ins=[(x, "row"), (g, "full"), (dxn, "row"), (dres, "row")],
                   outs=[((s, c), F32, "row"), ((8, c), F32, "acc")], name=name)


def _partner(t):
    lane = lax.broadcasted_iota(jnp.int32, t.shape, 1)
    half = QK_ROPE // 2
    return jnp.where(lane % QK_ROPE < half, pltpu.roll(t, LANES - half, 1), pltpu.roll(t, half, 1))


def _rope(t, cos, sin):
    return t * cos + _partner(t) * sin


def _rope_t(d, cos, sin):
    return d * cos + _partner(d * sin)


def rope_tables(s):
    half = QK_ROPE // 2
    inv = ROPE_BASE ** (-jnp.arange(half, dtype=F32) / half)
    ang = jnp.arange(s).astype(F32)[:, None] * inv[None, :]
    cos, sin = jnp.cos(ang), jnp.sin(ang)
    z = jnp.zeros((s, LANES - QK_ROPE), F32)
    return jnp.concatenate([cos, cos, z], axis=1), jnp.concatenate([-sin, sin, z], axis=1)


H0_A, H0_G, H0_Q, H0_KV, H0_KR, H0_W = 0, 512, 1024, 1280, 1408, 1536


def _mixpre_fn(a, g, q, kv, qn, kvn):
    return a * _sigmoid(g), _rms(q, qn), _rms(kv, kvn)


def _h0_parts(h_ref):
    return (h_ref[:, H0_A:H0_G], h_ref[:, H0_G:H0_Q], h_ref[:, H0_Q:H0_KV], h_ref[:, H0_KV:H0_KR])


def mixpre_fwd(h0, qn, kvn, cos, sin):
    s = h0.shape[0]

    def body(nt, h_ref, qn_ref, kvn_ref, cos_ref, sin_ref, u0_ref, cq_ref, ckv_ref, kr_ref):
        u0, cq, ckv = _mixpre_fn(*_h0_parts(h_ref), qn_ref[...], kvn_ref[...])
        u0_ref[...] = u0
        cq_ref[...] = cq.astype(BF16)
        ckv_ref[...] = ckv.astype(BF16)
        kr_ref[...] = _rope(h_ref[:, H0_KR:H0_W], cos_ref[...], sin_ref[...]).astype(BF16)

    return rowcall(body, rows=s, ts=min(ROW_TILE, s),
                   ins=[(h0, "row"), (qn, "full"), (kvn, "full"), (cos, "row"), (sin, "row")],
                   outs=[((s, CONV_WIDTH), F32, "row"), ((s, Q_LORA), BF16, "row"),
                         ((s, KV_LORA), BF16, "row"), ((s, LANES), BF16, "row")], name="mixpre_fwd")


def mixpre_bwd(h0, qn, kvn, cos, sin, du0, dcq, dckv, dkr):
    s = h0.shape[0]

    def body(nt, h_ref, qn_ref, kvn_ref, cos_ref, sin_ref, du0_ref, dcq_ref, dckv_ref, dkr_ref,
             dh_ref, dqn_ref, dkvn_ref):
        i = pl.program_id(0)
        _, vjp = jax.vjp(_mixpre_fn, *_h0_parts(h_ref), qn_ref[...], kvn_ref[...])
        da, dg, dq, dkv, dqn, dkvn = vjp((du0_ref[...], dcq_ref[...], dckv_ref[...]))
        dh_ref[:, H0_A:H0_G] = da.astype(BF16)
        dh_ref[:, H0_G:H0_Q] = dg.astype(BF16)
        dh_ref[:, H0_Q:H0_KV] = dq.astype(BF16)
        dh_ref[:, H0_KV:H0_KR] = dkv.astype(BF16)
        dkr = dkr_ref[:, :LANES]
        for h in range(1, N_HEADS):
            dkr = dkr + dkr_ref[:, h * LANES:(h + 1) * LANES]
        dh_ref[:, H0_KR:H0_W] = _rope_t(dkr, cos_ref[...], sin_ref[...]).astype(BF16)
        _acc(dqn_ref, i, dqn)
        _acc(dkvn_ref, i, dkvn)

    return rowcall(body, rows=s, ts=min(ROW_TILE, s),
                   ins=[(h0, "row"), (qn, "full"), (kvn, "full"), (cos, "row"), (sin, "row"),
                        (du0, "row"), (dcq, "row"), (dckv, "row"), (dkr, "row")],
                   outs=[((s, H0_W), BF16, "row"), ((8, Q_LORA), F32, "acc"), ((8, KV_LORA), F32, "acc")],
                   name="mixpre_bwd")


def _conv_taps(ext_ref, w_ref, ts, first, ntaps, flip=False):
    acc = None
    for k in range(ntaps):
        term = w_ref[pl.ds(ntaps - 1 - k if flip else k, 1), :] * ext_ref[pl.ds(first + k, ts), :]
        acc = term if acc is None else acc + term
    return acc


def _ln_silu(u1, g, b):
    return _silu(_layer_norm(u1, g, b))


SUBLANES = 8


def _fill_shifted(sh_ref, parts, rows):
    pos = 0
    for p in parts:
        sh_ref[0, pl.ds(pos, p.shape[0]), :] = p
        pos += p.shape[0]
    sh_ref[0, pl.ds(rows, SUBLANES), :] = jnp.zeros((SUBLANES, sh_ref.shape[2]), F32)
    for r in range(1, SUBLANES):
        sh_ref[r, pl.ds(0, rows), :] = sh_ref[0, pl.ds(r, rows), :]


def _window(sh_ref, off, n):
    r = off % SUBLANES
    return sh_ref[r, pl.ds(off - r, n), :]


def _taps_aligned(sh_ref, w_ref, n, first, ntaps, flip=False):
    acc = None
    for k in range(ntaps):
        term = w_ref[pl.ds(ntaps - 1 - k if flip else k, 1), :] * _window(sh_ref, first + k, n)
        acc = term if acc is None else acc + term
    return acc


def _conv_scratch(ts, c):
    return pltpu.VMEM((SUBLANES, ts + CONV_HALO + SUBLANES, c), F32)


def convln_fwd(u0, w, b, lg, lb):
    s, c = u0.shape
    ts = min(ROW_TILE, s)
    rc = min(CONV_ROW_CHUNK, ts)
    first = CONV_HALO - (CONV_K - 1)

    def body(nt, cur_ref, prev_ref, w_ref, b_ref, lg_ref, lb_ref, o_ref, sh_ref):
        i = pl.program_id(0)
        _fill_shifted(sh_ref, [jnp.where(i > 0, prev_ref[...], 0.0), cur_ref[...]], ts + CONV_HALO)
        for r0 in range(0, ts, rc):
            u1 = _taps_aligned(sh_ref, w_ref, rc, first + r0, CONV_K) + b_ref[...]
            o_ref[pl.ds(r0, rc), :] = _ln_silu(u1, lg_ref[...], lb_ref[...]).astype(BF16)

    return rowcall(body, rows=s, ts=ts,
                   ins=[(u0, "row"), (u0, f"prev:{CONV_HALO}"), (w, "full"), (b, "full"), (lg, "full"), (lb, "full")],
                   outs=[((s, c), BF16, "row")], name="convln_fwd", scratch=[_conv_scratch(ts, c)])[0]


def convln_bwd1(u0, w, b, lg, lb, du):
    s, c = u0.shape
    ts = min(ROW_TILE, s)
    rc = min(CONV_ROW_CHUNK, ts)
    first = CONV_HALO - (CONV_K - 1)

    def body(nt, cur_ref, prev_ref, w_ref, b_ref, lg_ref, lb_ref, du_ref, du1_ref, dlg_ref, dlb_ref, dcb_ref, sh_ref):
        i = pl.program_id(0)
        _fill_shifted(sh_ref, [jnp.where(i > 0, prev_ref[...], 0.0), cur_ref[...]], ts + CONV_HALO)
        sums = [jnp.zeros((1, c), F32)] * 3
        for r0 in range(0, ts, rc):
            u1 = _taps_aligned(sh_ref, w_ref, rc, first + r0, CONV_K) + b_ref[...]
            _, vjp = jax.vjp(_ln_silu, u1, lg_ref[...], lb_ref[...])
            du1, dlg, dlb = vjp(du_ref[pl.ds(r0, rc), :].astype(F32))
            du1_ref[pl.ds(r0, rc), :] = du1
            parts = (dlg, dlb, jnp.sum(du1, axis=0, keepdims=True))
            sums = [a + jnp.sum(p, axis=0, keepdims=True) for a, p in zip(sums, parts)]
        _acc(dlg_ref, i, sums[0])
        _acc(dlb_ref, i, sums[1])
        _acc(dcb_ref, i, sums[2])

    return rowcall(body, rows=s, ts=ts,
                   ins=[(u0, "row"), (u0, f"prev:{CONV_HALO}"), (w, "full"), (b, "full"), (lg, "full"), (lb, "full"),
                        (du, "row")],
                   outs=[((s, c), F32, "row"), ((8, c), F32, "acc"), ((8, c), F32, "acc"), ((8, c), F32, "acc")],
                   name="convln_bwd1", scratch=[_conv_scratch(ts, c)])


def convln_bwd2(u0, w, du1):
    s, c = u0.shape
    ts = min(ROW_TILE, s)
    rc = min(CONV_ROW_CHUNK, ts)
    first = CONV_HALO - (CONV_K - 1)

    def body(nt, cur_ref, prev_ref, d_ref, dnext_ref, w_ref, du0_ref, dw_ref, sh_ref, dsh_ref):
        i = pl.program_id(0)
        _fill_shifted(sh_ref, [jnp.where(i > 0, prev_ref[...], 0.0), cur_ref[...]], ts + CONV_HALO)
        _fill_shifted(dsh_ref, [d_ref[...], jnp.where(i < nt - 1, dnext_ref[...], 0.0)], ts + CONV_HALO)
        for r0 in range(0, ts, rc):
            du0_ref[pl.ds(r0, rc), :] = _taps_aligned(dsh_ref, w_ref, rc, r0, CONV_K, flip=True)

        @pl.when(i == 0)
        def _():
            dw_ref[...] = jnp.zeros_like(dw_ref)

        for k in range(CONV_K):
            part = jnp.zeros((SUBLANES, c), F32)
            for r0 in range(0, ts, rc):
                prod = d_ref[pl.ds(r0, rc), :] * _window(sh_ref, first + k + r0, rc)
                for a in range(0, rc, SUBLANES):
                    part = part + prod[a:a + SUBLANES]
            dw_ref[pl.ds(k, 1), :] += jnp.sum(part, axis=0, keepdims=True)

    return rowcall(body, rows=s, ts=ts,
                   ins=[(u0, "row"), (u0, f"prev:{CONV_HALO}"), (du1, "row"), (du1, f"next:{CONV_HALO}"), (w, "full")],
                   outs=[((s, c), F32, "row"), ((CONV_HALO, c), F32, "acc")], name="convln_bwd2",
                   scratch=[_conv_scratch(ts, c), _conv_scratch(ts, c)])


def qrope_fwd(qraw, cos, sin):
    s = qraw.shape[0]

    def body(nt, q_ref, cos_ref, sin_ref, o_ref):
        cos_v, sin_v = cos_ref[...] * ATT_SCALE, sin_ref[...] * ATT_SCALE
        for h in range(N_HEADS):
            nope = q_ref[:, h * HEAD_PAD:h * HEAD_PAD + LANES] * ATT_SCALE
            o_ref[:, h * HEAD_PAD:h * HEAD_PAD + LANES] = nope.astype(BF16)
            r = q_ref[:, h * HEAD_PAD + LANES:(h + 1) * HEAD_PAD]
            o_ref[:, h * HEAD_PAD + LANES:(h + 1) * HEAD_PAD] = _rope(r, cos_v, sin_v).astype(BF16)

    return rowcall(body, rows=s, ts=min(ROW_TILE, s), ins=[(qraw, "row"), (cos, "row"), (sin, "row")],
                   outs=[((s, N_HEADS * HEAD_PAD), BF16, "row")], name="qrope_fwd")[0]


def qrope_bwd(dq, cos, sin):
    s = dq.shape[0]

    def body(nt, d_ref, cos_ref, sin_ref, o_ref):
        cos_v, sin_v = cos_ref[...] * ATT_SCALE, sin_ref[...] * ATT_SCALE
        for h in range(N_HEADS):
            nope = d_ref[:, h * HEAD_PAD:h * HEAD_PAD + LANES] * ATT_SCALE
            o_ref[:, h * HEAD_PAD:h * HEAD_PAD + LANES] = nope.astype(BF16)
            r = d_ref[:, h * HEAD_PAD + LANES:(h + 1) * HEAD_PAD].astype(F32)
            o_ref[:, h * HEAD_PAD + LANES:(h + 1) * HEAD_PAD] = _rope_t(r, cos_v, sin_v).astype(BF16)

    return rowcall(body, rows=s, ts=min(ROW_TILE, s), ins=[(dq, "row"), (cos, "row"), (sin, "row")],
                   outs=[((s, N_HEADS * HEAD_PAD), BF16, "row")], name="qrope_bwd")[0]


_NT = (((1,), (1,)), ((), ()))
_TN = (((0,), (0,)), ((), ()))


def _scores(q, kvr, diagonal):
    s = lax.dot_general(q, kvr, _NT, preferred_element_type=F32)
    if not diagonal:
        return s
    row = lax.broadcasted_iota(jnp.int32, s.shape, 0)
    col = lax.broadcasted_iota(jnp.int32, s.shape, 1)
    return jnp.where(col <= row, s, -jnp.inf)


def _on_causal_pairs(pair, k_blk, fn):
    @pl.when(k_blk < 2 * pair)
    def _():
        fn(0, False)
        fn(1, False)

    @pl.when(k_blk == 2 * pair)
    def _():
        fn(0, True)
        fn(1, False)

    @pl.when(k_blk == 2 * pair + 1)
    def _():
        fn(1, True)


def attn_fwd(q, kv, kr):
    s = q.shape[0]
    t = min(ATT_TILE, s // 2)
    n = s // t
    np_ = n // 2

    def body(q_ref, kv_ref, kr_ref, o_ref, lse_ref, m_ref, l_ref, acc_ref):
        i, j = pl.program_id(1), pl.program_id(2)

        @pl.when(j == 0)
        def _():
            m_ref[...] = jnp.full_like(m_ref, -jnp.inf)
            l_ref[...] = jnp.zeros_like(l_ref)
            acc_ref[...] = jnp.zeros_like(acc_ref)

        def block(sub, diagonal):
            kvv = kv_ref[...]
            kvr = jnp.concatenate([kvv, kr_ref[...]], axis=1)
            sc = _scores(q_ref[pl.ds(sub * t, t), :], kvr, diagonal)
            m_prev = m_ref[sub]
            m_new = jnp.maximum(m_prev, jnp.max(sc, axis=-1, keepdims=True))
            alpha = jnp.exp(m_prev - m_new)
            p = jnp.exp(sc - m_new)
            l_ref[sub] = alpha * l_ref[sub] + jnp.sum(p, axis=-1, keepdims=True)
            acc_ref[sub] = alpha * acc_ref[sub] + jnp.dot(p.astype(BF16), kvv, preferred_element_type=F32)
            m_ref[sub] = m_new

        _on_causal_pairs(i, j, block)

        @pl.when(j == 2 * i + 1)
        def _():
            for sub in range(2):
                l = l_ref[sub]
                o_ref[pl.ds(sub * t, t), :] = (acc_ref[sub] / l).astype(BF16)
                lse_ref[pl.ds(sub * t, t), :] = jnp.broadcast_to(m_ref[sub] + jnp.log(l), (t, LANES))

    kj = lambda h, i, j: (jnp.minimum(j, 2 * i + 1), h)
    return pl.pallas_call(
        body, name="attn_fwd", grid=(N_HEADS, np_, n),
        in_specs=[pl.BlockSpec((2 * t, HEAD_PAD), lambda h, i, j: (i, h)),
                  pl.BlockSpec((t, LANES), kj),
                  pl.BlockSpec((t, LANES), lambda h, i, j: (jnp.minimum(j, 2 * i + 1), 0))],
        out_specs=(pl.BlockSpec((2 * t, LANES), lambda h, i, j: (i, h)),
                   pl.BlockSpec((2 * t, LANES), lambda h, i, j: (i, h))),
        out_shape=(jax.ShapeDtypeStruct((s, N_HEADS * LANES), BF16),
                   jax.ShapeDtypeStruct((s, N_HEADS * LANES), F32)),
        scratch_shapes=[pltpu.VMEM((2, t, 1), F32), pltpu.VMEM((2, t, 1), F32), pltpu.VMEM((2, t, LANES), F32)],
        compiler_params=_cparams("parallel", "parallel", "arbitrary"),
    )(q, kv, kr)


def attn_bwd(q, kv, kr, o, do, lse):
    s = q.shape[0]
    t = min(ATT_TILE, s // 2)
    n = s // t
    np_ = n // 2

    def body(q_ref, kv_ref, kr_ref, o_ref, do_ref, lse_ref, dq_ref, dkv_ref, dkr_ref):
        j, i = pl.program_id(1), pl.program_id(2)

        @pl.when((i == 0) & (j == 0))
        def _():
            dq_ref[...] = jnp.zeros_like(dq_ref)

        @pl.when(i == 0)
        def _():
            dkv_ref[...] = jnp.zeros_like(dkv_ref)
            dkr_ref[...] = jnp.zeros_like(dkr_ref)

        def block(sub, diagonal):
            sl = pl.ds(sub * t, t)
            qv, dov, kvv = q_ref[sl, :], do_ref[sl, :], kv_ref[...]
            kvr = jnp.concatenate([kvv, kr_ref[...]], axis=1)
            p = jnp.exp(_scores(qv, kvr, diagonal) - lse_ref[sl, :1])
            dp = lax.dot_general(dov, kvv, _NT, preferred_element_type=F32)
            delta = jnp.sum(dov.astype(F32) * o_ref[sl, :].astype(F32), axis=-1, keepdims=True)
            ds = (p * (dp - delta)).astype(BF16)
            dk = lax.dot_general(ds, qv, _TN, preferred_element_type=F32)
            dkv_ref[...] += lax.dot_general(p.astype(BF16), dov, _TN, preferred_element_type=F32) + dk[:, :LANES]
            dkr_ref[...] += dk[:, LANES:]
            rows = pl.ds(pl.multiple_of((2 * i + sub) * t, t), t)
            dq_ref[rows, :] += jnp.dot(ds, kvr, preferred_element_type=F32)

        _on_causal_pairs(i, j, block)

    qi = lambda h, j, i: (jnp.maximum(i, lax.div(j, 2)), h)
    kj = lambda h, j, i: (j, h)
    return pl.pallas_call(
        body, name="attn_bwd", grid=(N_HEADS, n, np_),
        in_specs=[pl.BlockSpec((2 * t, HEAD_PAD), qi), pl.BlockSpec((t, LANES), kj),
                  pl.BlockSpec((t, LANES), lambda h, j, i: (j, 0)),
                  pl.BlockSpec((2 * t, LANES), qi), pl.BlockSpec((2 * t, LANES), qi), pl.BlockSpec((2 * t, LANES), qi)],
        out_specs=(pl.BlockSpec((s, HEAD_PAD), lambda h, j, i: (0, h)),
                   pl.BlockSpec((t, LANES), kj), pl.BlockSpec((t, LANES), kj)),
        out_shape=(jax.ShapeDtypeStruct((s, N_HEADS * HEAD_PAD), F32),
                   jax.ShapeDtypeStruct((s, N_HEADS * LANES), F32), jax.ShapeDtypeStruct((s, N_HEADS * LANES), F32)),
        compiler_params=_cparams("parallel", "arbitrary", "arbitrary"),
    )(q, kv, kr, o, do, lse)


def ffn_fwd(x, g, wa, wb, cwa, cwb, ba, bb, wd, name):
    s, d = x.shape
    f = wa.shape[1]
    ts, tf = min(FFN_ROW_TILE, s), FFN_COL_TILE
    hal = FFN_HALO
    nj = f // tf
    first = hal - (FFN_K - 1)
    rc = min(FFN_ROW_CHUNK, ts)

    def body(x_ref, xp_ref, g_ref, wa_ref, wb_ref, cwa_ref, cwb_ref, ba_ref, bb_ref, wd_ref,
             xo_ref, xn_ref, hpa_ref, hpb_ref, act_ref, xe_ref, ea_ref, eb_ref):
        i, j = pl.program_id(0), pl.program_id(1)

        @pl.when(j == 0)
        def _():
            xn = _rms(x_ref[...], g_ref[...]).astype(BF16)
            xn_ref[...] = xn
            xe_ref[pl.ds(hal, ts), :] = xn
            xe_ref[pl.ds(0, hal), :] = jnp.where(i > 0, _rms(xp_ref[...], g_ref[...]), 0.0).astype(BF16)
            xo_ref[...] = x_ref[...]

        halves = ((0, ts // 2), (ts // 2, ts))
        for lo, hi in halves:
            e0, e1 = (0 if lo == 0 else hal + lo), hal + hi
            xe = xe_ref[pl.ds(e0, e1 - e0), :]
            ea_ref[pl.ds(e0, e1 - e0), :] = jnp.dot(xe, wa_ref[...], preferred_element_type=F32)
            eb_ref[pl.ds(e0, e1 - e0), :] = jnp.dot(xe, wb_ref[...], preferred_element_type=F32)
        for lo, hi in halves:
            hpa_ref[pl.ds(lo, hi - lo), :] = ea_ref[pl.ds(hal + lo, hi - lo), :].astype(BF16)
            hpb_ref[pl.ds(lo, hi - lo), :] = eb_ref[pl.ds(hal + lo, hi - lo), :].astype(BF16)
            for r0 in range(lo, hi, rc):
                ha = _conv_taps(ea_ref, cwa_ref, rc, first + r0, FFN_K) + ba_ref[...]
                hb = _conv_taps(eb_ref, cwb_ref, rc, first + r0, FFN_K) + bb_ref[...]
                act_ref[pl.ds(r0, rc), :] = (_silu(ha) * hb).astype(BF16)
            xo_ref[pl.ds(lo, hi - lo), :] += jnp.dot(act_ref[pl.ds(lo, hi - lo), :], wd_ref[...],
                                                     preferred_element_type=F32)

    r = ts // hal
    row = pl.BlockSpec((ts, d), lambda i, j: (i, 0))
    prev = pl.BlockSpec((hal, d), lambda i, j: (jnp.maximum(i * r - 1, 0), 0))
    gsp = pl.BlockSpec((1, d), lambda i, j: (0, 0))
    wup = pl.BlockSpec((d, tf), lambda i, j: (0, j))
    cwsp = pl.BlockSpec((8, tf), lambda i, j: (0, j))
    bsp = pl.BlockSpec((1, tf), lambda i, j: (0, j))
    wdn = pl.BlockSpec((tf, d), lambda i, j: (j, 0))
    hid = pl.BlockSpec((ts, tf), lambda i, j: (i, j))
    return pl.pallas_call(
        body, name=name, grid=(s // ts, nj),
        in_specs=[row, prev, gsp, wup, wup, cwsp, cwsp, bsp, bsp, wdn],
        out_specs=(row, row, hid, hid, hid),
        out_shape=(jax.ShapeDtypeStruct((s, d), F32), jax.ShapeDtypeStruct((s, d), BF16),
                   jax.ShapeDtypeStruct((s, f), BF16), jax.ShapeDtypeStruct((s, f), BF16),
                   jax.ShapeDtypeStruct((s, f), BF16)),
        scratch_shapes=[pltpu.VMEM((ts + hal, d), BF16), pltpu.VMEM((ts + hal, tf), F32),
                        pltpu.VMEM((ts + hal, tf), F32)],
        compiler_params=_cparams("parallel", "arbitrary"),
    )(x, x, g, wa, wb, cwa, cwb, ba, bb, wd)


def ffn_bwd(dy, hpa, hpb, wa, wb, cwa, cwb, ba, bb, wd, name):
    s, d = dy.shape
    f = hpa.shape[1]
    ts, tf = min(FFN_ROW_TILE, s), FFN_COL_TILE
    hal = FFN_HALO
    nt, nj = s // ts, f // tf
    te = ts + hal
    first = hal - (FFN_K - 1)
    rc = min(FFN_ROW_CHUNK, ts)

    def body(dy_ref, dyn_ref, a_ref, ap_ref, an_ref, b_ref, bp_ref, bn_ref, wa_ref, wb_ref, cwa_ref, cwb_ref,
             ba_ref, bb_ref, wd_ref,
             dxn_ref, dpa_ref, dpb_ref, dwa_ref, dwb_ref, dba_ref, dbb_ref,
             dye_ref, ea_ref, eb_ref, dact_ref, da_ref, db_ref):
        i, j = pl.program_id(0), pl.program_id(1)
        last = i == nt - 1

        @pl.when(j == 0)
        def _():
            dye_ref[pl.ds(0, ts), :] = dy_ref[...].astype(BF16)
            dye_ref[pl.ds(ts, hal), :] = jnp.where(last, 0.0, dyn_ref[...]).astype(BF16)
            dxn_ref[...] = jnp.zeros_like(dxn_ref)

        @pl.when((i == 0) & (j == 0))
        def _():
            for r in (dwa_ref, dwb_ref, dba_ref, dbb_ref):
                r[...] = jnp.zeros_like(r)

        halves = ((0, ts // 2), (ts // 2, ts))
        for lo, hi in halves:
            n = hi - lo + (hal if hi == ts else 0)
            dact_ref[pl.ds(lo, n), :] = lax.dot_general(dye_ref[pl.ds(lo, n), :], wd_ref[...], _NT,
                                                        preferred_element_type=F32)
        for cur, prev, nxt, ext in ((a_ref, ap_ref, an_ref, ea_ref), (b_ref, bp_ref, bn_ref, eb_ref)):
            ext[pl.ds(0, hal), :] = jnp.where(i > 0, prev[...].astype(F32), 0.0)
            ext[pl.ds(hal, ts), :] = cur[...].astype(F32)
            ext[pl.ds(hal + ts, hal), :] = jnp.where(last, 0.0, nxt[...].astype(F32))
        zero = jnp.zeros((1, tf), F32)
        sums = {"ba": zero, "bb": zero, **{("a", k): zero for k in range(FFN_K)}, **{("b", k): zero for k in range(FFN_K)}}
        for r0 in list(range(0, ts, rc)) + [ts]:
            n = rc if r0 < ts else hal
            win_a = [ea_ref[pl.ds(first + r0 + k, n), :] for k in range(FFN_K)]
            win_b = [eb_ref[pl.ds(first + r0 + k, n), :] for k in range(FFN_K)]
            ha = sum(cwa_ref[pl.ds(k, 1), :] * win_a[k] for k in range(FFN_K)) + ba_ref[...]
            hb = sum(cwb_ref[pl.ds(k, 1), :] * win_b[k] for k in range(FFN_K)) + bb_ref[...]
            sig = _sigmoid(ha)
            gs = dact_ref[pl.ds(r0, n), :] * sig
            dha = gs * hb * (1.0 + ha * (1.0 - sig))
            dhb = gs * ha
            da_ref[pl.ds(r0, n), :] = dha
            db_ref[pl.ds(r0, n), :] = dhb
            if r0 < ts:
                sums["ba"] = sums["ba"] + jnp.sum(dha, axis=0, keepdims=True)
                sums["bb"] = sums["bb"] + jnp.sum(dhb, axis=0, keepdims=True)
                for k in range(FFN_K):
                    sums["a", k] = sums["a", k] + jnp.sum(dha * win_a[k], axis=0, keepdims=True)
                    sums["b", k] = sums["b", k] + jnp.sum(dhb * win_b[k], axis=0, keepdims=True)
        for lo, hi in halves:
            for r0 in range(lo, hi, rc):
                dpa_ref[pl.ds(r0, rc), :] = _conv_taps(da_ref, cwa_ref, rc, r0, FFN_K, flip=True).astype(BF16)
                dpb_ref[pl.ds(r0, rc), :] = _conv_taps(db_ref, cwb_ref, rc, r0, FFN_K, flip=True).astype(BF16)
            rows = pl.ds(lo, hi - lo)
            dxn_ref[rows, :] += (lax.dot_general(dpa_ref[rows, :], wa_ref[...], _NT, preferred_element_type=F32)
                                 + lax.dot_general(dpb_ref[rows, :], wb_ref[...], _NT, preferred_element_type=F32))
        dba_ref[j] += jnp.broadcast_to(sums["ba"], (8, tf))
        dbb_ref[j] += jnp.broadcast_to(sums["bb"], (8, tf))
        row = lax.broadcasted_iota(jnp.int32, (8, tf), 0)
        dwa_ref[j] += sum(jnp.where(row == k, sums["a", k], 0.0) for k in range(FFN_K))
        dwb_ref[j] += sum(jnp.where(row == k, sums["b", k], 0.0) for k in range(FFN_K))

    r = ts // hal
    lastblk = s // hal - 1
    row = pl.BlockSpec((ts, d), lambda i, j: (i, 0))
    rown = pl.BlockSpec((hal, d), lambda i, j: (jnp.minimum((i + 1) * r, lastblk), 0))
    cur = pl.BlockSpec((ts, tf), lambda i, j: (i, j))
    prev = pl.BlockSpec((hal, tf), lambda i, j: (jnp.maximum(i * r - 1, 0), j))
    nxt = pl.BlockSpec((hal, tf), lambda i, j: (jnp.minimum((i + 1) * r, lastblk), j))
    wup = pl.BlockSpec((d, tf), lambda i, j: (0, j))
    cwsp = pl.BlockSpec((8, tf), lambda i, j: (0, j))
    bsp = pl.BlockSpec((1, tf), lambda i, j: (0, j))
    wdn = pl.BlockSpec((tf, d), lambda i, j: (j, 0))
    accsp = pl.BlockSpec((nj, 8, tf), lambda i, j: (0, 0, 0))
    accshape = jax.ShapeDtypeStruct((nj, 8, tf), F32)
    return pl.pallas_call(
        body, name=name, grid=(nt, nj),
        in_specs=[row, rown, cur, prev, nxt, cur, prev, nxt, wup, wup, cwsp, cwsp, bsp, bsp, wdn],
        out_specs=(row, cur, cur, accsp, accsp, accsp, accsp),
        out_shape=(jax.ShapeDtypeStruct((s, d), F32), jax.ShapeDtypeStruct((s, f), BF16),
                   jax.ShapeDtypeStruct((s, f), BF16), accshape, accshape, accshape, accshape),
        scratch_shapes=[pltpu.VMEM((te, d), BF16), pltpu.VMEM((ts + 2 * hal, tf), F32),
                        pltpu.VMEM((ts + 2 * hal, tf), F32), pltpu.VMEM((te, tf), F32),
                        pltpu.VMEM((te, tf), F32), pltpu.VMEM((te, tf), F32)],
        compiler_params=_cparams("arbitrary", "arbitrary"),
    )(dy, dy, hpa, hpa, hpa, hpb, hpb, hpb, wa, wb, cwa, cwb, ba, bb, wd)


NQ = 4
SQ = SSM_STATE * 8
NS = SSM_GROUPS * SSM_STATE


def _s5_disc(log_dt, a_re, a_im, b_re, b_im, expand):
    dt = jnp.exp(log_dt)
    mag = jnp.exp(a_re * dt)
    lb_re, lb_im = mag * jnp.cos(a_im * dt), mag * jnp.sin(a_im * dt)
    den = a_re * a_re + a_im * a_im
    nr, ni = lb_re - 1.0, lb_im
    f_re = (nr * a_re + ni * a_im) / den
    f_im = (ni * a_re - nr * a_im) / den
    fe_re = jnp.dot(f_re, expand, precision=lax.Precision.HIGHEST, preferred_element_type=F32)
    fe_im = jnp.dot(f_im, expand, precision=lax.Precision.HIGHEST, preferred_element_type=F32)
    return lb_re, lb_im, fe_re * b_re - fe_im * b_im, fe_re * b_im + fe_im * b_re


def _expand_matrix():
    e = np.zeros((SSM_STATE, SSM_STATE * SSM_GROUP), np.float32)
    for p in range(SSM_STATE):
        e[p, p * SSM_GROUP:(p + 1) * SSM_GROUP] = 1.0
    return jnp.asarray(e)


def s5_params_fwd(log_dt, a_re, a_im, b_re, b_im):
    expand = _expand_matrix()

    def body(ld_ref, ar_ref, ai_ref, br_ref, bi_ref, e_ref, lr_ref, li_ref, bbr_ref, bbi_ref):
        lr, li, bbr, bbi = _s5_disc(ld_ref[...], ar_ref[...], ai_ref[...], br_ref[...], bi_ref[...], e_ref[...])
        lr_ref[...] = lr
        li_ref[...] = li
        bbr_ref[...] = bbr
        bbi_ref[...] = bbi

    g, p, pc = SSM_GROUPS, SSM_STATE, SSM_STATE * SSM_GROUP
    return pl.pallas_call(
        body, name="s5_params_fwd",
        out_shape=(jax.ShapeDtypeStruct((g, p), F32), jax.ShapeDtypeStruct((g, p), F32),
                   jax.ShapeDtypeStruct((g, pc), F32), jax.ShapeDtypeStruct((g, pc), F32)),
    )(log_dt, a_re, a_im, b_re, b_im, expand)


def s5_params_bwd(log_dt, a_re, a_im, b_re, b_im, dlr, dli, dbbr, dbbi):
    expand = _expand_matrix()

    def body(ld_ref, ar_ref, ai_ref, br_ref, bi_ref, e_ref, dlr_ref, dli_ref, dbbr_ref, dbbi_ref,
             dld_ref, dar_ref, dai_ref, dbr_ref, dbi_ref):
        e = e_ref[...]
        f = lambda ld, ar, ai, br, bi: _s5_disc(ld, ar, ai, br, bi, e)
        _, vjp = jax.vjp(f, ld_ref[...], ar_ref[...], ai_ref[...], br_ref[...], bi_ref[...])
        dld, dar, dai, dbr, dbi = vjp((dlr_ref[...], dli_ref[...], dbbr_ref[...], dbbi_ref[...]))
        dld_ref[...] = dld
        dar_ref[...] = dar
        dai_ref[...] = dai
        dbr_ref[...] = dbr
        dbi_ref[...] = dbi

    g, p, pc = SSM_GROUPS, SSM_STATE, SSM_STATE * SSM_GROUP
    return pl.pallas_call(
        body, name="s5_params_bwd",
        out_shape=(jax.ShapeDtypeStruct((g, 1), F32), jax.ShapeDtypeStruct((g, p), F32),
                   jax.ShapeDtypeStruct((g, p), F32), jax.ShapeDtypeStruct((g, pc), F32),
                   jax.ShapeDtypeStruct((g, pc), F32)),
    )(log_dt, a_re, a_im, b_re, b_im, expand, dlr, dli, dbbr, dbbi)


def _cmul(ar, ai, br, bi):
    return ar * br - ai * bi, ar * bi + ai * br


def _power_rows(lr, li, conj_rev):
    row = lax.broadcasted_iota(jnp.int32, (8, NS), 0)
    tr = jnp.zeros((8, NS), F32)
    ti = jnp.zeros((8, NS), F32)
    pr, pi = lr, li
    for r in range(8):
        dst = 7 - r if conj_rev else r
        tr = jnp.where(row == dst, pr, tr)
        ti = jnp.where(row == dst, -pi if conj_rev else pi, ti)
        if r < 7:
            pr, pi = _cmul(pr, pi, lr, li)
    return tr, ti


def _scan8(xr, xi, tr_ref, ti_ref, cr, ci, reverse):
    row = lax.broadcasted_iota(jnp.int32, xr.shape, 0)
    for d in (1, 2, 4):
        if reverse:
            sr, si = pltpu.roll(xr, 8 - d, 0), pltpu.roll(xi, 8 - d, 0)
            keep = row < 8 - d
            pw = 8 - d
        else:
            sr, si = pltpu.roll(xr, d, 0), pltpu.roll(xi, d, 0)
            keep = row >= d
            pw = d - 1
        mr, mi = _cmul(tr_ref[pl.ds(pw, 1), :], ti_ref[pl.ds(pw, 1), :], sr, si)
        xr = xr + jnp.where(keep, mr, 0.0)
        xi = xi + jnp.where(keep, mi, 0.0)
    mr, mi = _cmul(tr_ref[...], ti_ref[...], cr, ci)
    return xr + mr, xi + mi


def _row_of(x, r):
    row = lax.broadcasted_iota(jnp.int32, x.shape, 0)
    return jnp.sum(jnp.where(row == r, x, 0.0), axis=0, keepdims=True)


def s5_scan_fwd(u, lam_r, lam_i, bre, bim, cre, cim, dskip):
    s = u.shape[0]
    tt = min(SCAN_TILE, s)
    nb = tt // 8

    def body(nt, u_ref, lr_ref, li_ref, bre_ref, bim_ref, cre_ref, cim_ref, d_ref,
             xr_ref, xi_ref, y_ref, yg_ref, tr_ref, ti_ref, cr_ref, ci_ref):
        i = pl.program_id(0)

        @pl.when(i == 0)
        def _():
            tr, ti = _power_rows(lr_ref[...], li_ref[...], False)
            tr_ref[...] = tr
            ti_ref[...] = ti
            cr_ref[...] = jnp.zeros_like(cr_ref)
            ci_ref[...] = jnp.zeros_like(ci_ref)

        uv = u_ref[...]
        ub = uv.astype(BF16)
        for q in range(NQ):
            uq = ub[:, q * LANES:(q + 1) * LANES]
            xr_ref[:, q * SQ:(q + 1) * SQ] = jnp.dot(uq, bre_ref[q], preferred_element_type=F32)
            xi_ref[:, q * SQ:(q + 1) * SQ] = jnp.dot(uq, bim_ref[q], preferred_element_type=F32)

        def step(b, carry):
            cr, ci = carry
            rows = pl.ds(pl.multiple_of(b * 8, 8), 8)
            xr, xi = _scan8(xr_ref[rows, :], xi_ref[rows, :], tr_ref, ti_ref, cr, ci, False)
            xr_ref[rows, :] = xr
            xi_ref[rows, :] = xi
            return _row_of(xr, 7), _row_of(xi, 7)

        cr, ci = lax.fori_loop(0, nb, step, (cr_ref[...], ci_ref[...]), unroll=min(SCAN_UNROLL, nb))
        cr_ref[...] = cr
        ci_ref[...] = ci
        y = d_ref[...] * uv
        for q in range(NQ):
            yq = (jnp.dot(xr_ref[:, q * SQ:(q + 1) * SQ].astype(BF16), cre_ref[q], preferred_element_type=F32)
                  - jnp.dot(xi_ref[:, q * SQ:(q + 1) * SQ].astype(BF16), cim_ref[q], preferred_element_type=F32))
            y_ref[:, q * LANES:(q + 1) * LANES] = yq + y[:, q * LANES:(q + 1) * LANES]
        yg_ref[...] = _gelu(y_ref[...]).astype(BF16)

    return rowcall(body, rows=s, ts=tt,
                   ins=[(u, "row"), (lam_r, "full"), (lam_i, "full"), (bre, "full"), (bim, "full"),
                        (cre, "full"), (cim, "full"), (dskip, "full")],
                   outs=[((s, NS), F32, "row"), ((s, NS), F32, "row"), ((s, SSM_WIDTH), F32, "row"),
                         ((s, SSM_WIDTH), BF16, "row")], name="s5_scan_fwd",
                   scratch=[pltpu.VMEM((8, NS), F32), pltpu.VMEM((8, NS), F32),
                            pltpu.VMEM((1, NS), F32), pltpu.VMEM((1, NS), F32)])


def s5_scan_bwd(dyg, y, lam_r, lam_i, cre, cim):
    s = y.shape[0]
    tt = min(SCAN_TILE, s)
    nb = tt // 8

    def body(nt, dyg_ref, y_ref, lr_ref, li_ref, cre_ref, cim_ref,
             ar_ref, ai_ref, dy_ref, tr_ref, ti_ref, cr_ref, ci_ref):
        i = pl.program_id(0)

        @pl.when(i == 0)
        def _():
            tr, ti = _power_rows(lr_ref[...], li_ref[...], True)
            tr_ref[...] = tr
            ti_ref[...] = ti
            cr_ref[...] = jnp.zeros_like(cr_ref)
            ci_ref[...] = jnp.zeros_like(ci_ref)

        _, vjp = jax.vjp(_gelu, y_ref[...])
        dy = vjp(dyg_ref[...])[0]
        dyb = dy.astype(BF16)
        dy_ref[...] = dyb
        for q in range(NQ):
            dq = dyb[:, q * LANES:(q + 1) * LANES]
            ar_ref[:, q * SQ:(q + 1) * SQ] = lax.dot_general(dq, cre_ref[q], _NT, preferred_element_type=F32)
            ai_ref[:, q * SQ:(q + 1) * SQ] = -lax.dot_general(dq, cim_ref[q], _NT, preferred_element_type=F32)

        def step(b, carry):
            cr, ci = carry
            rows = pl.ds(pl.multiple_of((nb - 1 - b) * 8, 8), 8)
            xr, xi = _scan8(ar_ref[rows, :], ai_ref[rows, :], tr_ref, ti_ref, cr, ci, True)
            ar_ref[rows, :] = xr
            ai_ref[rows, :] = xi
            return _row_of(xr, 0), _row_of(xi, 0)

        cr, ci = lax.fori_loop(0, nb, step, (cr_ref[...], ci_ref[...]), unroll=min(SCAN_UNROLL, nb))
        cr_ref[...] = cr
        ci_ref[...] = ci

    return rowcall(body, rows=s, ts=tt,
                   ins=[(dyg, "rev"), (y, "rev"), (lam_r, "full"), (lam_i, "full"), (cre, "full"), (cim, "full")],
                   outs=[((s, NS), F32, "rev"), ((s, NS), F32, "rev"), ((s, SSM_WIDTH), BF16, "rev")],
                   name="s5_scan_bwd",
                   scratch=[pltpu.VMEM((8, NS), F32), pltpu.VMEM((8, NS), F32),
                            pltpu.VMEM((1, NS), F32), pltpu.VMEM((1, NS), F32)])


def s5_grads(u, dy, xr, xi, ar, ai, bre, bim, dskip):
    s = u.shape[0]
    tt = min(SCAN_TILE, s)

    def body(nt, u_ref, dy_ref, xr_ref, xrp_ref, xi_ref, xip_ref, ar_ref, ai_ref, bre_ref, bim_ref, d_ref,
             du_ref, dlr_ref, dli_ref, dbr_ref, dbi_ref, dcr_ref, dci_ref, dd_ref, er_ref, ei_ref):
        i = pl.program_id(0)

        @pl.when(i == 0)
        def _():
            for r in (dbr_ref, dbi_ref, dcr_ref, dci_ref):
                r[...] = jnp.zeros_like(r)

        uv, dyb = u_ref[...], dy_ref[...]
        dyf = dyb.astype(F32)
        av_r, av_i, xv_r, xv_i = ar_ref[...], ai_ref[...], xr_ref[...], xi_ref[...]
        er_ref[pl.ds(0, 8), :] = jnp.where(i > 0, xrp_ref[...], 0.0)
        ei_ref[pl.ds(0, 8), :] = jnp.where(i > 0, xip_ref[...], 0.0)
        er_ref[pl.ds(8, tt), :] = xv_r
        ei_ref[pl.ds(8, tt), :] = xv_i
        sr, si = er_ref[pl.ds(7, tt), :], ei_ref[pl.ds(7, tt), :]
        _acc(dlr_ref, i, av_r * sr + av_i * si)
        _acc(dli_ref, i, av_i * sr - av_r * si)
        _acc(dd_ref, i, dyf * uv)
        ub = uv.astype(BF16)
        ab_r, ab_i = av_r.astype(BF16), av_i.astype(BF16)
        xb_r, xb_i = xv_r.astype(BF16), xv_i.astype(BF16)
        du = d_ref[...] * dyf
        for q in range(NQ):
            cs, ss = slice(q * LANES, (q + 1) * LANES), slice(q * SQ, (q + 1) * SQ)
            dbr_ref[q] += lax.dot_general(ub[:, cs], ab_r[:, ss], _TN, preferred_element_type=F32)
            dbi_ref[q] += lax.dot_general(ub[:, cs], ab_i[:, ss], _TN, preferred_element_type=F32)
            dcr_ref[q] += lax.dot_general(xb_r[:, ss], dyb[:, cs], _TN, preferred_element_type=F32)
            dci_ref[q] -= lax.dot_general(xb_i[:, ss], dyb[:, cs], _TN, preferred_element_type=F32)
            du_ref[:, cs] = (du[:, cs]
                             + lax.dot_general(ab_r[:, ss], bre_ref[q], _NT, preferred_element_type=F32)
                             + lax.dot_general(ab_i[:, ss], bim_ref[q], _NT, preferred_element_type=F32))

    return rowcall(body, rows=s, ts=tt,
                   ins=[(u, "row"), (dy, "row"), (xr, "row"), (xr, "prev:8"), (xi, "row"), (xi, "prev:8"),
                        (ar, "row"), (ai, "row"), (bre, "full"), (bim, "full"), (dskip, "full")],
                   outs=[((s, SSM_WIDTH), F32, "row"), ((8, NS), F32, "acc"), ((8, NS), F32, "acc"),
                         ((NQ, LANES, SQ), F32, "acc"), ((NQ, LANES, SQ), F32, "acc"),
                         ((NQ, SQ, LANES), F32, "acc"), ((NQ, SQ, LANES), F32, "acc"),
                         ((8, SSM_WIDTH), F32, "acc")], name="s5_grads",
                   scratch=[pltpu.VMEM((tt + 8, NS), F32), pltpu.VMEM((tt + 8, NS), F32)])


def _glu_fn(za, zb):
    return za * _sigmoid(zb)


def glu_res_fwd(z, xres):
    s = z.shape[0]

    def body(nt, z_ref, x_ref, o_ref):
        o_ref[...] = x_ref[...] + _glu_fn(z_ref[:, :D_MODEL].astype(F32), z_ref[:, D_MODEL:].astype(F32))

    return rowcall(body, rows=s, ts=min(ROW_TILE, s), ins=[(z, "row"), (xres, "row")],
                   outs=[((s, D_MODEL), F32, "row")], name="glu_res_fwd")[0]


def glu_bwd(z, dout):
    s, c = z.shape

    def body(nt, z_ref, d_ref, dz_ref, dba_ref, dbb_ref):
        i = pl.program_id(0)
        _, vjp = jax.vjp(_glu_fn, z_ref[:, :D_MODEL].astype(F32), z_ref[:, D_MODEL:].astype(F32))
        dza, dzb = vjp(d_ref[...])
        dz_ref[:, :D_MODEL] = dza.astype(BF16)
        dz_ref[:, D_MODEL:] = dzb.astype(BF16)
        _acc(dba_ref, i, dza)
        _acc(dbb_ref, i, dzb)

    return rowcall(body, rows=s, ts=min(ROW_TILE, s), ins=[(z, "row"), (dout, "row")],
                   outs=[((s, c), BF16, "row"), ((8, D_MODEL), F32, "acc"), ((8, D_MODEL), F32, "acc")],
                   name="glu_bwd")


def loss_head(x, g, target):
    s, c = x.shape

    def body(nt, x_ref, g_ref, t_ref, loss_ref, dx_ref, dg_ref):
        i = pl.program_id(0)
        y, vjp = jax.vjp(_rms, x_ref[...], g_ref[...])
        err = y - t_ref[...]
        dx, dg = vjp(err * (1.0 / c))
        dx_ref[...] = dx
        _acc(dg_ref, i, dg)
        part = jnp.sum(jnp.sum(err * err, axis=-1, keepdims=True), axis=0, keepdims=True) * (0.5 / c)

        @pl.when(i == 0)
        def _():
            loss_ref[...] = jnp.zeros_like(loss_ref)

        loss_ref[...] += jnp.broadcast_to(part, loss_ref.shape)

    return rowcall(body, rows=s, ts=min(ROW_TILE, s), ins=[(x, "row"), (g, "full"), (target, "row")],
                   outs=[((8, LANES), F32, "acc"), ((s, c), F32, "row"), ((8, c), F32, "acc")], name="loss_head")


def _tile_rows(r, cands=(512, 256, 128, 64, 32, 16, 8)):
    return _pick(r, cands)


def add_to_bf16(a, b, name):
    n, r, c = a.shape
    tr = _tile_rows(r)

    def body(a_ref, b_ref, o_ref):
        o_ref[...] = (a_ref[...].astype(F32) + b_ref[...].astype(F32)).astype(BF16)

    spec = pl.BlockSpec((1, tr, c), lambda j, i: (j, i, 0))
    return pl.pallas_call(body, name=name, grid=(n, r // tr), in_specs=[spec, spec], out_specs=spec,
                          out_shape=jax.ShapeDtypeStruct((n, r, c), BF16),
                          compiler_params=_cparams("parallel", "parallel"))(a, b)


def sum_leading(a, name):
    n, r, c = a.shape
    tr = _tile_rows(r)

    def body(a_ref, o_ref):
        acc = a_ref[0].astype(F32)
        for k in range(1, n):
            acc = acc + a_ref[k].astype(F32)
        o_ref[...] = acc

    return pl.pallas_call(body, name=name, grid=(r // tr,),
                          in_specs=[pl.BlockSpec((n, tr, c), lambda i: (0, i, 0))],
                          out_specs=pl.BlockSpec((tr, c), lambda i: (i, 0)),
                          out_shape=jax.ShapeDtypeStruct((r, c), F32),
                          compiler_params=_cparams("parallel"))(a)


def adamw(w, g, m, v, name):
    r, c = w.shape
    tr = _tile_rows(r, (256, 128, 64, 32, 16, 8))
    c1 = 1.0 - ADAM_B1 ** ADAM_STEP
    c2 = 1.0 - ADAM_B2 ** ADAM_STEP

    def body(w_ref, g_ref, m_ref, v_ref, d_ref, nm_ref, nv_ref):
        gv = g_ref[...]
        mn = ADAM_B1 * m_ref[...] + (1.0 - ADAM_B1) * gv
        vn = ADAM_B2 * v_ref[...] + (1.0 - ADAM_B2) * (gv * gv)
        d_ref[...] = -ADAM_LR * ((mn / c1) / (jnp.sqrt(vn / c2) + ADAM_EPS) + ADAM_WD * w_ref[...])
        nm_ref[...] = mn
        nv_ref[...] = vn

    spec = pl.BlockSpec((tr, c), lambda i: (i, 0))
    shp = jax.ShapeDtypeStruct((r, c), F32)
    return pl.pallas_call(body, name=name, grid=(r // tr,), in_specs=[spec] * 4, out_specs=(spec,) * 3,
                          out_shape=(shp,) * 3, compiler_params=_cparams("parallel"))(w, g, m, v)


_ANY = pl.BlockSpec(memory_space=pl.ANY)


def all_gather8(block, name):
    r, c = block.shape

    def body(x_ref, out_ref, send_sems, recv_sems, local_sem):
        x, y, cc = lax.axis_index("x"), lax.axis_index("y"), lax.axis_index("c")
        me, sibling = (x, y, cc), (x, y, 1 - cc)
        chips = [(1 - x, y), (x, 1 - y), (1 - x, 1 - y)]

        def slot(px, py, pc):
            return out_ref.at[4 * px + 2 * py + pc]

        def copy(k, blk, to, src=None):
            return pltpu.make_async_remote_copy(
                src_ref=slot(*blk) if src is None else src, dst_ref=slot(*blk),
                send_sem=send_sems.at[k], recv_sem=recv_sems.at[k], device_id=to, device_id_type=MESH)

        mine = pltpu.make_async_copy(x_ref, slot(*me), local_sem)
        mine.start()
        first = [copy(0, me, sibling, src=x_ref)]
        first += [copy(1 + j, me, (*chip, cc), src=x_ref) for j, chip in enumerate(chips)]
        for cp in first:
            cp.start()
        passed = [copy(4 + j, (*chip, cc), sibling) for j, chip in enumerate(chips)]
        for j, chip in enumerate(chips):
            copy(1 + j, (*chip, cc), me).wait_recv()
            passed[j].start()
        copy(0, sibling, me).wait_recv()
        for j, chip in enumerate(chips):
            copy(4 + j, (*chip, 1 - cc), me).wait_recv()
        for cp in first + passed:
            cp.wait_send()
        mine.wait()

    return pl.pallas_call(
        body, name=name, in_specs=[_ANY], out_specs=_ANY,
        out_shape=jax.ShapeDtypeStruct((8, r, c), block.dtype),
        scratch_shapes=[pltpu.SemaphoreType.DMA((7,)), pltpu.SemaphoreType.DMA((7,)), pltpu.SemaphoreType.DMA],
    )(block)


_HBM = pl.BlockSpec(memory_space=pltpu.HBM)
_SEM = pl.BlockSpec(memory_space=pltpu.SEMAPHORE)
_DATAFLOW = pltpu.SideEffectType.DATAFLOW_SIDE_EFFECTING
N_REMOTE = 6


def _remote_peers(x, y, cc):
    return [(1 - x, y, cc), (x, 1 - y, cc), (1 - x, 1 - y, cc),
            (1 - x, y, 1 - cc), (x, 1 - y, 1 - cc), (1 - x, 1 - y, 1 - cc)]


def gather_start(block, after, name):
    r, c = block.shape

    def body(x_ref, land_ref, after_ref, send_sems, recv_sems, x_thru, land_thru, token):
        x, y, cc = lax.axis_index("x"), lax.axis_index("y"), lax.axis_index("c")
        for k, peer in enumerate(_remote_peers(x, y, cc)):
            pltpu.make_async_remote_copy(src_ref=x_ref, dst_ref=land_ref.at[4 * x + 2 * y + cc],
                                         send_sem=send_sems.at[k], recv_sem=recv_sems.at[k],
                                         device_id=peer, device_id_type=MESH).start()
        token[...] = jnp.zeros_like(token)

    land = pltpu.with_memory_space_constraint(lax.empty((8, r, c), block.dtype), pltpu.HBM)
    return pl.pallas_call(
        body, name=name,
        out_shape=(pltpu.SemaphoreType.DMA((N_REMOTE,)), pltpu.SemaphoreType.DMA((N_REMOTE,)),
                   pltpu.HBM((r, c), block.dtype), pltpu.HBM((8, r, c), block.dtype),
                   jax.ShapeDtypeStruct((8, LANES), F32)),
        in_specs=(_HBM, _HBM, _ANY), out_specs=(_SEM, _SEM, _HBM, _HBM, pl.BlockSpec(memory_space=pltpu.VMEM)),
        input_output_aliases={0: 2, 1: 3},
        compiler_params=pltpu.CompilerParams(has_side_effects=_DATAFLOW),
    )(pltpu.with_memory_space_constraint(block, pltpu.HBM), land, after)


def gather_wait(send_sems, recv_sems, block_thru, land_thru, after, name):
    def body(x_ref, land_ref, send_sems, recv_sems, after_ref, x_dead, got_ref):
        x, y, cc = lax.axis_index("x"), lax.axis_index("y"), lax.axis_index("c")
        for k, (px, py, pc) in enumerate(_remote_peers(x, y, cc)):
            cp = pltpu.make_async_remote_copy(src_ref=x_ref, dst_ref=land_ref.at[4 * px + 2 * py + pc],
                                              send_sem=send_sems.at[k], recv_sem=recv_sems.at[k],
                                              device_id=(px, py, pc), device_id_type=MESH)
            cp.wait_send()
            cp.wait_recv()

    return pl.pallas_call(
        body, name=name,
        out_shape=(pltpu.HBM(block_thru.shape, block_thru.dtype), pltpu.HBM(land_thru.shape, land_thru.dtype)),
        in_specs=(_HBM, _HBM, _SEM, _SEM, _ANY), out_specs=(_HBM, _HBM), input_output_aliases={0: 0, 1: 1},
        compiler_params=pltpu.CompilerParams(has_side_effects=_DATAFLOW),
    )(block_thru, land_thru, send_sems, recv_sems, after)[1]


def _chip_peers(x, y):
    return [(1 - x, y), (x, 1 - y), (1 - x, 1 - y)]


def exchange_start(parts, name):
    def body(p_ref, land_ref, send_sems, recv_sems, p_thru, land_thru, token):
        x, y, cc = lax.axis_index("x"), lax.axis_index("y"), lax.axis_index("c")
        for k, (px, py) in enumerate(_chip_peers(x, y)):
            pltpu.make_async_remote_copy(src_ref=p_ref.at[2 * px + py], dst_ref=land_ref.at[2 * x + y],
                                         send_sem=send_sems.at[k], recv_sem=recv_sems.at[k],
                                         device_id=(px, py, cc), device_id_type=MESH).start()
        token[...] = jnp.zeros_like(token)

    land = pltpu.with_memory_space_constraint(lax.empty(parts.shape, parts.dtype), pltpu.HBM)
    return pl.pallas_call(
        body, name=name,
        out_shape=(pltpu.SemaphoreType.DMA((3,)), pltpu.SemaphoreType.DMA((3,)),
                   pltpu.HBM(parts.shape, parts.dtype), pltpu.HBM(parts.shape, parts.dtype),
                   jax.ShapeDtypeStruct((8, LANES), F32)),
        in_specs=(_HBM, _HBM), out_specs=(_SEM, _SEM, _HBM, _HBM, pl.BlockSpec(memory_space=pltpu.VMEM)),
        input_output_aliases={0: 2, 1: 3},
        compiler_params=pltpu.CompilerParams(has_side_effects=_DATAFLOW),
    )(pltpu.with_memory_space_constraint(parts, pltpu.HBM), land)


def exchange_wait(send_sems, recv_sems, parts_thru, land_thru, after, name):
    def body(p_ref, land_ref, send_sems, recv_sems, after_ref, p_dead, got_ref):
        x, y, cc = lax.axis_index("x"), lax.axis_index("y"), lax.axis_index("c")
        for k, (px, py) in enumerate(_chip_peers(x, y)):
            cp = pltpu.make_async_remote_copy(src_ref=p_ref.at[2 * px + py], dst_ref=land_ref.at[2 * px + py],
                                              send_sem=send_sems.at[k], recv_sem=recv_sems.at[k],
                                              device_id=(px, py, cc), device_id_type=MESH)
            cp.wait_send()
            cp.wait_recv()

    return pl.pallas_call(
        body, name=name,
        out_shape=(pltpu.HBM(parts_thru.shape, parts_thru.dtype), pltpu.HBM(land_thru.shape, land_thru.dtype)),
        in_specs=(_HBM, _HBM, _SEM, _SEM, _ANY), out_specs=(_HBM, _HBM), input_output_aliases={0: 0, 1: 1},
        compiler_params=pltpu.CompilerParams(has_side_effects=_DATAFLOW),
    )(parts_thru, land_thru, send_sems, recv_sems, after)


def sibling_swap(block, name):
    def body(x_ref, out_ref, send_sem, recv_sem):
        x, y, cc = lax.axis_index("x"), lax.axis_index("y"), lax.axis_index("c")
        cp = pltpu.make_async_remote_copy(src_ref=x_ref, dst_ref=out_ref, send_sem=send_sem, recv_sem=recv_sem,
                                          device_id=(x, y, 1 - cc), device_id_type=MESH)
        cp.start()
        cp.wait()

    return pl.pallas_call(
        body, name=name, in_specs=[_ANY], out_specs=_ANY,
        out_shape=jax.ShapeDtypeStruct(block.shape, block.dtype),
        scratch_shapes=[pltpu.SemaphoreType.DMA, pltpu.SemaphoreType.DMA],
    )(block)


PACK_COLS = 1024
SHARDED = (("l0_w_in", 1), ("l0_w_uq", 1), ("l0_w_ukv", 1), ("l0_w_out", 0), ("l0_w_up", 1), ("l0_w_down", 0),
           ("l1_w_in", 0), ("l1_w_glu", 1), ("l1_w_up", 1), ("l1_w_down", 0),
           ("l0_conv_w", 1), ("l0_ffn_conv_w", 1), ("l1_ffn_conv_w", 1))
REPLICATED = ("l0_mix_norm", "l0_conv_b", "l0_conv_ln_g", "l0_conv_ln_b", "l0_q_norm", "l0_kv_norm", "l0_ffn_norm",
              "l0_ffn_conv_b", "l1_mix_norm", "l1_log_dt", "l1_a_re", "l1_a_im", "l1_b_re", "l1_b_im", "l1_c_re",
              "l1_c_im", "l1_d", "l1_b_glu", "l1_ffn_norm", "l1_ffn_conv_b", "final_norm")


def _pack(arrs, dtype, mult):
    flat = jnp.concatenate([a.reshape(-1).astype(dtype) for a in arrs])
    n = flat.shape[0]
    total = -(-n // mult) * mult
    return jnp.pad(flat, (0, total - n))


def _unpack(flat, shapes):
    out, pos = [], 0
    for shp in shapes:
        n = int(np.prod(shp))
        out.append(flat[pos:pos + n].reshape(shp))
        pos += n
    return out


PACK_ROW_ALIGN = 16


def _pack_rows(arrs, dtype, row_mult):
    parts = []
    for a in arrs:
        n = int(np.prod(a.shape))
        rows = -(-n // PACK_COLS)
        if n % PACK_COLS == 0:
            r = a.astype(dtype).reshape(rows, PACK_COLS)
        else:
            r = jnp.pad(a.reshape(-1).astype(dtype), (0, rows * PACK_COLS - n)).reshape(rows, PACK_COLS)
        parts.append(jnp.pad(r, ((0, (-rows) % PACK_ROW_ALIGN), (0, 0))))
    p = jnp.concatenate(parts)
    return jnp.pad(p, ((0, (-p.shape[0]) % row_mult), (0, 0)))


def _unpack_rows(pack, shapes):
    out, r0 = [], 0
    for shp in shapes:
        n = int(np.prod(shp))
        rows = -(-n // PACK_COLS)
        piece = lax.optimization_barrier(pack[r0:r0 + rows])
        out.append(piece.reshape(shp) if n % PACK_COLS == 0 else piece.reshape(-1)[:n].reshape(shp))
        r0 += rows + (-rows) % PACK_ROW_ALIGN
    return out


def _shard(full, axis, j):
    n = full.shape[axis] // N_CHIPS
    return lax.slice_in_dim(full, j * n, (j + 1) * n, axis=axis)


def _block_diag(t):
    q, g, a, b = t.shape
    eye = jnp.eye(g, dtype=t.dtype)
    return jnp.einsum("qgab,gh->qgahb", t, eye).reshape(q, g * a, g * b)


def _block_diag_t(d, a, b):
    q = d.shape[0]
    d5 = d.reshape(q, 8, a, 8, b)
    eye = jnp.eye(8, dtype=d.dtype)
    return jnp.einsum("qgahb,gh->qgab", d5, eye)


def kernel(x, l0_mix_norm, l0_w_in, l0_conv_w, l0_conv_b, l0_conv_ln_g, l0_conv_ln_b, l0_q_norm, l0_kv_norm, l0_w_uq, l0_w_ukv, l0_w_out, l0_ffn_norm, l0_w_up, l0_ffn_conv_w, l0_ffn_conv_b, l0_w_down, l1_mix_norm, l1_w_in, l1_log_dt, l1_a_re, l1_a_im, l1_b_re, l1_b_im, l1_c_re, l1_c_im, l1_d, l1_w_glu, l1_b_glu, l1_ffn_norm, l1_w_up, l1_ffn_conv_w, l1_ffn_conv_b, l1_w_down, final_norm, loss_target, m_l0_mix_norm, m_l0_w_in, m_l0_conv_w, m_l0_conv_b, m_l0_conv_ln_g, m_l0_conv_ln_b, m_l0_q_norm, m_l0_kv_norm, m_l0_w_uq, m_l0_w_ukv, m_l0_w_out, m_l0_ffn_norm, m_l0_w_up, m_l0_ffn_conv_w, m_l0_ffn_conv_b, m_l0_w_down, m_l1_mix_norm, m_l1_w_in, m_l1_log_dt, m_l1_a_re, m_l1_a_im, m_l1_b_re, m_l1_b_im, m_l1_c_re, m_l1_c_im, m_l1_d, m_l1_w_glu, m_l1_b_glu, m_l1_ffn_norm, m_l1_w_up, m_l1_ffn_conv_w, m_l1_ffn_conv_b, m_l1_w_down, m_final_norm, v_l0_mix_norm, v_l0_w_in, v_l0_conv_w, v_l0_conv_b, v_l0_conv_ln_g, v_l0_conv_ln_b, v_l0_q_norm, v_l0_kv_norm, v_l0_w_uq, v_l0_w_ukv, v_l0_w_out, v_l0_ffn_norm, v_l0_w_up, v_l0_ffn_conv_w, v_l0_ffn_conv_b, v_l0_w_down, v_l1_mix_norm, v_l1_w_in, v_l1_log_dt, v_l1_a_re, v_l1_a_im, v_l1_b_re, v_l1_b_im, v_l1_c_re, v_l1_c_im, v_l1_d, v_l1_w_glu, v_l1_b_glu, v_l1_ffn_norm, v_l1_w_up, v_l1_ffn_conv_w, v_l1_ffn_conv_b, v_l1_w_down, v_final_norm):
    a = dict(locals())
    w = {n: a[n] for n in [s for s, _ in SHARDED] + list(REPLICATED)}
    mom = {n: a["m_" + n] for n in w}
    var = {n: a["v_" + n] for n in w}
    return _step(a["x"][0], a["loss_target"][0], w, mom, var)


FIRST_WEIGHTS = ("l0_w_in", "l0_w_uq", "l0_w_ukv", "l0_w_out")
LATER_WEIGHTS = ("l0_w_up", "l0_w_down", "l1_w_in", "l1_w_glu", "l1_w_up", "l1_w_down")


def _assemble(got, names, w):
    got = got.reshape(N_CHIPS, -1, PACK_COLS)
    shapes = [w[n].shape for n in names]
    per_chip = [_unpack_rows(got[j], shapes) for j in range(N_CHIPS)]
    axes = dict(SHARDED)
    return {n: jnp.concatenate([per_chip[j][k] for j in range(N_CHIPS)], axis=axes[n]) for k, n in enumerate(names)}


def _gather_weights(w):
    cc = lax.axis_index("c")
    small = [n for n, _ in SHARDED[10:]]
    full = {}
    for names, dtype, mult in ((FIRST_WEIGHTS, BF16, 2 * 256), (small, F32, 2 * PACK_ROW_ALIGN)):
        pack = _pack_rows([w[n] for n in names], dtype, mult)
        half = lax.dynamic_index_in_dim(pack.reshape(2, -1, PACK_COLS), cc, axis=0, keepdims=False)
        got = all_gather8(half, "gather_" + ("first_matrices" if dtype == BF16 else "conv_weights"))
        full.update(_assemble(got, names, w))
    pack = _pack_rows([w[n] for n in LATER_WEIGHTS], BF16, 2 * 256).reshape(2, -1, PACK_COLS)
    half = lax.dynamic_index_in_dim(pack, cc, axis=0, keepdims=False)
    send_sems, recv_sems, half_thru, land_thru, token = gather_start(half, got, "gather_later_start")
    return full, (send_sems, recv_sems, half_thru, land_thru, pack), token[0, 0]


def _finish_gather(pending, after, w):
    send_sems, recv_sems, half_thru, land_thru, pack = pending
    got = gather_wait(send_sems, recv_sems, half_thru, land_thru, after, "gather_later_wait")
    chip = 2 * lax.axis_index("x") + lax.axis_index("y")
    got = lax.dynamic_update_slice(got, pack, (2 * chip, 0, 0))
    return _assemble(got, LATER_WEIGHTS, w)


def _reduce_begin(grads, names, tag):
    cc = lax.axis_index("c")
    axes = dict(SHARDED)
    packs = [_pack_rows([_shard(grads[n], axes[n], j) for n in names], BF16, 2 * 256) for j in range(N_CHIPS)]
    g = jnp.stack(packs).reshape(N_CHIPS, 2, -1, PACK_COLS)
    keep = lax.dynamic_index_in_dim(g, cc, axis=1, keepdims=False)
    give = lax.dynamic_index_in_dim(g, 1 - cc, axis=1, keepdims=False)
    got = sibling_swap(give, f"grad_swap_halves_{tag}")
    parts = add_to_bf16(keep, got, f"grad_add_sibling_{tag}")
    send_sems, recv_sems, parts_thru, land_thru, token = exchange_start(parts, f"grad_exchange_start_{tag}")
    shapes = [_shard(grads[n], axes[n], 0).shape for n in names]
    return (send_sems, recv_sems, parts_thru, land_thru, list(names), shapes), token[0, 0]


def _reduce_end(state, after, tag):
    send_sems, recv_sems, parts_thru, land_thru, names, shapes = state
    cc = lax.axis_index("c")
    chip = 2 * lax.axis_index("x") + lax.axis_index("y")
    parts, landed = exchange_wait(send_sems, recv_sems, parts_thru, land_thru, after, f"grad_exchange_wait_{tag}")
    own = lax.dynamic_index_in_dim(parts, chip, axis=0, keepdims=True)
    landed = lax.dynamic_update_slice(landed, own, (chip, 0, 0))
    mine = sum_leading(landed, f"grad_sum_chips_{tag}")
    theirs = sibling_swap(mine, f"grad_swap_sums_{tag}")
    lo = jnp.where(cc == 0, mine, theirs)
    hi = jnp.where(cc == 0, theirs, mine)
    return dict(zip(names, _unpack_rows(jnp.concatenate([lo, hi]), shapes)))


def _reduce_replicated(grads):
    names = list(REPLICATED)
    flat = _pack([grads[n] for n in names], F32, 256 * LANES).reshape(-1, LANES)
    got = all_gather8(flat, "gather_small_grads")
    tot = sum_leading(got, "sum_small_grads").reshape(-1)
    return dict(zip(names, _unpack(tot, [grads[n].shape for n in names]))), flat.shape


def _row(v):
    return v.reshape(1, -1).astype(F32)


def _pad_rows(wt, rows):
    return jnp.pad(wt.astype(F32), ((0, rows - wt.shape[0]), (0, 0)))


def _ffn_fwd(xin, g, wa, wb, cw, cb, wd, tag):
    cwa, cwb = _pad_rows(cw[:, :D_FF], 8), _pad_rows(cw[:, D_FF:], 8)
    xout, xn, hpa, hpb, act = ffn_fwd(xin, _row(g), wa, wb, cwa, cwb, _row(cb[:D_FF]), _row(cb[D_FF:]), wd, tag)
    return xout, (xin, xn, hpa, hpb, act)


def _ffn_bwd(dxout, saved, g, wa, wb, cw, cb, wd, tag, zero=0.0):
    xin, xn, hpa, hpb, act = saved
    d_wd = matmul(act, dxout, ta=True, name=f"{tag}_d_wdown")
    cwa, cwb = _pad_rows(cw[:, :D_FF], 8), _pad_rows(cw[:, D_FF:], 8)
    dxn, dpa, dpb, dwa, dwb, dba, dbb = ffn_bwd(dxout, hpa, hpb, wa, wb, cwa, cwb,
                                                _row(cb[:D_FF]) + zero, _row(cb[D_FF:]), wd, tag + "_bwd")
    d_wu = jnp.concatenate([matmul(xn, dpa, ta=True, name=f"{tag}_d_wup_a"),
                            matmul(xn, dpb, ta=True, name=f"{tag}_d_wup_b")], axis=1)
    dxin, dg = rms_bwd(xin, _row(g), dxn, dxout, f"{tag}_rms_bwd")
    taps = lambda t: t.transpose(1, 0, 2).reshape(8, -1)
    d_cw = jnp.concatenate([taps(dwa)[:FFN_K], taps(dwb)[:FFN_K]], axis=1)
    d_cb = jnp.concatenate([taps(dba)[0], taps(dbb)[0]])
    return dxin, dg[0], d_wu, d_cw, d_cb, d_wd


def _step(x, target, w, mom, var):
    s = x.shape[0]
    full, pending, zero = _gather_weights(w)
    cos, sin = rope_tables(s)

    w_in0 = full["l0_w_in"]
    w_in0p = jnp.concatenate([w_in0, jnp.zeros((D_MODEL, H0_W - w_in0.shape[1]), BF16)], axis=1)
    wq = full["l0_w_uq"].reshape(Q_LORA, N_HEADS, QK_NOPE + QK_ROPE)
    zq = lambda n: jnp.zeros((Q_LORA, N_HEADS, n), BF16)
    w_uqp = jnp.concatenate([wq[..., :QK_NOPE], zq(LANES - QK_NOPE), wq[..., QK_NOPE:], zq(LANES - QK_ROPE)],
                            axis=-1).reshape(Q_LORA, N_HEADS * HEAD_PAD)
    w_ukv = full["l0_w_ukv"]
    w_out = full["l0_w_out"]
    w_out_u = w_out[:CONV_WIDTH]
    wo = w_out[CONV_WIDTH:].reshape(N_HEADS, V_DIM, D_MODEL)
    w_out_a = jnp.concatenate([jnp.zeros_like(wo), wo], axis=1).reshape(N_HEADS * LANES, D_MODEL)
    conv_w = _pad_rows(full["l0_conv_w"], CONV_HALO)

    xn0 = rms_fwd(x, _row(w["l0_mix_norm"]) + zero, "l0_mix_rms")
    h0 = matmul(xn0, w_in0p, name="l0_in_proj")
    qn_g, kvn_g = _row(w["l0_q_norm"]), _row(w["l0_kv_norm"])
    u0, cq, ckv, kr = mixpre_fwd(h0, qn_g, kvn_g, cos, sin)
    cb, lg, lb = _row(w["l0_conv_b"]), _row(w["l0_conv_ln_g"]), _row(w["l0_conv_ln_b"])
    u = convln_fwd(u0, conv_w, cb, lg, lb)
    qraw = matmul(cq, w_uqp, name="l0_q_up")
    q = qrope_fwd(qraw, cos, sin)
    kv = matmul(ckv, w_ukv, out_dtype=BF16, name="l0_kv_up")
    o, lse = attn_fwd(q, kv, kr)
    x1 = matmul(u, w_out_u, res=x, name="l0_out_conv")
    x1 = matmul(o, w_out_a, res=x1, name="l0_out_attn")
    full.update(_finish_gather(pending, x1, w))
    w_up0a, w_up0b = full["l0_w_up"][:, :D_FF], full["l0_w_up"][:, D_FF:]
    w_up1a, w_up1b = full["l1_w_up"][:, :D_FF], full["l1_w_up"][:, D_FF:]

    x2, ffn0 = _ffn_fwd(x1, w["l0_ffn_norm"], w_up0a, w_up0b, full["l0_ffn_conv_w"], w["l0_ffn_conv_b"],
                        full["l0_w_down"], "l0_ffn")

    g_, p_, c_ = SSM_GROUPS, SSM_STATE, SSM_GROUP
    s5_in = (w["l1_log_dt"].reshape(g_, 1), w["l1_a_re"], w["l1_a_im"],
             w["l1_b_re"].reshape(g_, p_ * c_), w["l1_b_im"].reshape(g_, p_ * c_))
    lam_r, lam_i, bb_r, bb_i = s5_params_fwd(*s5_in)
    lam_rf, lam_if = lam_r.reshape(1, NS), lam_i.reshape(1, NS)

    def b_blocks(bb):
        t = bb.reshape(NQ, 8, p_, c_).transpose(0, 1, 3, 2)
        return _block_diag(t).astype(BF16)

    def c_blocks(cm):
        t = cm.reshape(NQ, 8, c_, p_).transpose(0, 1, 3, 2)
        return _block_diag(t).astype(BF16)

    bre, bim = b_blocks(bb_r), b_blocks(bb_i)
    cre, cim = c_blocks(w["l1_c_re"]), c_blocks(w["l1_c_im"])
    dskip = _row(w["l1_d"])
    xn2 = rms_fwd(x2, _row(w["l1_mix_norm"]), "l1_mix_rms")
    u1 = matmul(xn2, full["l1_w_in"], name="l1_in_proj")
    xs_r, xs_i, y1, yg = s5_scan_fwd(u1, lam_rf, lam_if, bre, bim, cre, cim, dskip)
    z = matmul(yg, full["l1_w_glu"], bias=_row(w["l1_b_glu"]), out_dtype=BF16, name="l1_glu_proj")
    x3 = glu_res_fwd(z, x2)

    x4, ffn1 = _ffn_fwd(x3, w["l1_ffn_norm"], w_up1a, w_up1b, full["l1_ffn_conv_w"], w["l1_ffn_conv_b"],
                        full["l1_w_down"], "l1_ffn")
    loss_part, dx4, dgf = loss_head(x4, _row(w["final_norm"]), target)
    loss = lax.psum(loss_part[0, 0], ("x", "y", "c"))

    gr = {"final_norm": dgf[0]}

    dx3, gr["l1_ffn_norm"], gr["l1_w_up"], gr["l1_ffn_conv_w"], gr["l1_ffn_conv_b"], gr["l1_w_down"] = _ffn_bwd(
        dx4, ffn1, w["l1_ffn_norm"], w_up1a, w_up1b, full["l1_ffn_conv_w"], w["l1_ffn_conv_b"], full["l1_w_down"],
        "l1_ffn")

    dz, dbga, dbgb = glu_bwd(z, dx3)
    gr["l1_b_glu"] = jnp.concatenate([dbga[0], dbgb[0]])
    dyg = matmul(dz, full["l1_w_glu"], tb=True, name="l1_d_yg")
    gr["l1_w_glu"] = matmul(yg, dz, ta=True, name="l1_d_wglu")
    a_r, a_i, dy1 = s5_scan_bwd(dyg, y1, lam_rf, lam_if, cre, cim)
    du1, dlr, dli, dbr, dbi, dcr, dci, dd = s5_grads(u1, dy1, xs_r, xs_i, a_r, a_i, bre, bim, dskip)
    gr["l1_d"] = dd[0]

    def b_unblock(d):
        return _block_diag_t(d, c_, p_).transpose(0, 1, 3, 2).reshape(g_, p_ * c_)

    def c_unblock(d):
        return _block_diag_t(d, p_, c_).transpose(0, 1, 3, 2).reshape(g_, c_, p_)

    gr["l1_c_re"], gr["l1_c_im"] = c_unblock(dcr), c_unblock(dci)
    dld, dar, dai, dbre, dbim = s5_params_bwd(*s5_in, dlr[0].reshape(g_, p_), dli[0].reshape(g_, p_),
                                              b_unblock(dbr), b_unblock(dbi))
    gr["l1_log_dt"], gr["l1_a_re"], gr["l1_a_im"] = dld.reshape(g_), dar, dai
    gr["l1_b_re"], gr["l1_b_im"] = dbre.reshape(g_, p_, c_), dbim.reshape(g_, p_, c_)
    dxn2 = matmul(du1, full["l1_w_in"], tb=True, name="l1_d_xn")
    gr["l1_w_in"] = matmul(xn2, du1, ta=True, name="l1_d_win")
    dx2, dg = rms_bwd(x2, _row(w["l1_mix_norm"]), dxn2, dx3, "l1_mix_rms_bwd")
    gr["l1_mix_norm"] = dg[0]
    red_a, zero_a = _reduce_begin(gr, ("l1_w_up", "l1_w_down", "l1_w_glu", "l1_w_in"), "a")

    dx1, gr["l0_ffn_norm"], gr["l0_w_up"], gr["l0_ffn_conv_w"], gr["l0_ffn_conv_b"], gr["l0_w_down"] = _ffn_bwd(
        dx2, ffn0, w["l0_ffn_norm"], w_up0a, w_up0b, full["l0_ffn_conv_w"], w["l0_ffn_conv_b"], full["l0_w_down"],
        "l0_ffn", zero_a)
    red_b, zero_b = _reduce_begin(gr, ("l0_w_up", "l0_w_down"), "b")

    du = matmul(dx1, w_out_u + zero_b.astype(BF16), tb=True, out_dtype=BF16, name="l0_d_u")
    do = matmul(dx1, w_out_a, tb=True, out_dtype=BF16, name="l0_d_o")
    d_wout_u = matmul(u, dx1, ta=True, name="l0_d_wout_u")
    d_wout_a = matmul(o, dx1, ta=True, name="l0_d_wout_a")
    gr["l0_w_out"] = jnp.concatenate(
        [d_wout_u, d_wout_a.reshape(N_HEADS, LANES, D_MODEL)[:, LANES - V_DIM:].reshape(N_HEADS * V_DIM, D_MODEL)])
    dq, dkv, dkr = attn_bwd(q, kv, kr, o, do, lse)
    dqraw = qrope_bwd(dq, cos, sin)
    dcq = matmul(dqraw, w_uqp, tb=True, name="l0_d_cq")
    d_wuqp = matmul(cq, dqraw, ta=True, name="l0_d_wuq").reshape(Q_LORA, N_HEADS, HEAD_PAD)
    gr["l0_w_uq"] = jnp.concatenate([d_wuqp[..., :QK_NOPE], d_wuqp[..., LANES:LANES + QK_ROPE]],
                                    axis=-1).reshape(Q_LORA, -1)
    dckv = matmul(dkv, w_ukv, tb=True, name="l0_d_ckv")
    gr["l0_w_ukv"] = matmul(ckv, dkv, ta=True, name="l0_d_wukv")
    du1c, dlg, dlb, dcb = convln_bwd1(u0, conv_w, cb, lg, lb, du)
    gr["l0_conv_ln_g"], gr["l0_conv_ln_b"], gr["l0_conv_b"] = dlg[0], dlb[0], dcb[0]
    du0, dcw = convln_bwd2(u0, conv_w, du1c)
    gr["l0_conv_w"] = dcw[:CONV_K]
    dh0, dqn, dkvn = mixpre_bwd(h0, qn_g, kvn_g, cos, sin, du0, dcq, dckv, dkr)
    gr["l0_q_norm"], gr["l0_kv_norm"] = dqn[0], dkvn[0]
    dxn0 = matmul(dh0, w_in0p, tb=True, name="l0_d_xn")
    gr["l0_w_in"] = matmul(xn0, dh0, ta=True, name="l0_d_win")[:, :w_in0.shape[1]]
    grad_x, dg = rms_bwd(x, _row(w["l0_mix_norm"]), dxn0, dx1, "l0_mix_rms_bwd")
    gr["l0_mix_norm"] = dg[0]

    rest = [n for n, _ in SHARDED if n not in red_a[-2] + red_b[-2]]
    red_c, _ = _reduce_begin(gr, rest, "c")
    g_sh = {**_reduce_end(red_a, grad_x, "a"), **_reduce_end(red_b, grad_x, "b"), **_reduce_end(red_c, grad_x, "c")}
    g_rep, pack_shape = _reduce_replicated(gr)
    grad, delta, new_m, new_v = {}, {}, {}, {}
    for n, _ in SHARDED:
        shp = w[n].shape
        two_d = (lambda t: t.reshape(shp[0], -1))
        grad[n] = g_sh[n]
        delta[n], new_m[n], new_v[n] = adamw(two_d(w[n]), two_d(g_sh[n]), two_d(mom[n]), two_d(var[n]), f"adamw_{n}")
    names = list(REPLICATED)
    pk = lambda d: _pack([d[n] for n in names], F32, 256 * LANES).reshape(pack_shape)
    dl, nm, nv = adamw(pk(w), pk(g_rep), pk(mom), pk(var), "adamw_small")
    shapes = [w[n].shape for n in names]
    for n, d_, m_, v_ in zip(names, _unpack(dl.reshape(-1), shapes), _unpack(nm.reshape(-1), shapes),
                             _unpack(nv.reshape(-1), shapes)):
        grad[n], delta[n], new_m[n], new_v[n] = g_rep[n], d_, m_, v_

    order = ["l0_mix_norm", "l0_w_in", "l0_conv_w", "l0_conv_b", "l0_conv_ln_g", "l0_conv_ln_b", "l0_q_norm",
             "l0_kv_norm", "l0_w_uq", "l0_w_ukv", "l0_w_out", "l0_ffn_norm", "l0_w_up", "l0_ffn_conv_w",
             "l0_ffn_conv_b", "l0_w_down", "l1_mix_norm", "l1_w_in", "l1_log_dt", "l1_a_re", "l1_a_im", "l1_b_re",
             "l1_b_im", "l1_c_re", "l1_c_im", "l1_d", "l1_w_glu", "l1_b_glu", "l1_ffn_norm", "l1_w_up",
             "l1_ffn_conv_w", "l1_ffn_conv_b", "l1_w_down", "final_norm"]
    return (loss, grad_x[None], *[grad[n] for n in order], *[delta[n] for n in order],
            *[new_m[n] for n in order], *[new_v[n] for n in order])
```

```python
import functools
import math

import jax
import jax.numpy as jnp
import numpy as np
from jax import lax
from jax.experimental import pallas as pl
from jax.experimental.pallas import tpu as pltpu

F32 = jnp.float32
BF16 = jnp.bfloat16
MESH = pl.DeviceIdType.MESH

D_MODEL = 1024
EPS = 1e-6
LN_EPS = 1e-5
CONV_WIDTH = 512
CONV_K = 31
N_HEADS = 8
QK_NOPE = 64
QK_ROPE = 32
V_DIM = 64
Q_LORA = 256
KV_LORA = 128
ROPE_BASE = 10000.0
ATT_SCALE = (QK_NOPE + QK_ROPE) ** -0.5
SSM_WIDTH = 512
SSM_GROUP = 16
SSM_GROUPS = 32
SSM_STATE = 64
D_FF = 2816
FFN_K = 3
ADAM_LR = 0.001
ADAM_B1 = 0.9
ADAM_B2 = 0.999
ADAM_EPS = 1e-08
ADAM_WD = 0.01
ADAM_STEP = 10

N_CHIPS = 4
LANES = 128
HEAD_PAD = 256
CONV_HALO = 32
FFN_HALO = 16
VMEM_LIMIT = 56 * 1024 * 1024

ROW_TILE = 512
FFN_ROW_TILE = 1024
FFN_COL_TILE = 256
FFN_ROW_CHUNK = 64
CONV_ROW_CHUNK = 32
FFN_BWD_PARTS = 4
ATT_TILE = 1024
SCAN_TILE = 256
SCAN_UNROLL = 4


def _cparams(*sem):
    return pltpu.CompilerParams(dimension_semantics=tuple(sem), vmem_limit_bytes=VMEM_LIMIT)


def _pick(n, cands):
    for c in cands:
        if n % c == 0:
            return c
    return n


def matmul(a, b, *, ta=False, tb=False, res=None, bias=None, out_dtype=None, name):
    if out_dtype is None:
        out_dtype = BF16 if ta else F32
    if ta:
        kdim, m = a.shape
    else:
        m, kdim = a.shape
    if tb:
        n, k2 = b.shape
    else:
        k2, n = b.shape
    assert kdim == k2, (a.shape, b.shape, ta, tb)
    tn = _pick(n, (1408, 1024, 768, 512, 384, 256, 128))
    if ta:
        tm = _pick(m, (1408, 1024, 512, 256, 128))
        tk = _pick(kdim, (1024, 512, 256, 128))
    else:
        tm = _pick(m, (1024, 512, 256, 128))
        tk = kdim
        if kdim > 1024:
            tn = _pick(n, (512, 256, 128))
        if tm * tn > 1024 * 1024 and out_dtype == F32:
            tm = _pick(m, (512, 256, 128))
    nk = kdim // tk
    has_res, has_bias = res is not None, bias is not None
    dims = (((0,) if ta else (1,), (1,) if tb else (0,)), ((), ()))

    def body(*refs):
        a_ref, b_ref = refs[0], refs[1]
        pos = 2
        res_ref = bias_ref = None
        if has_res:
            res_ref = refs[pos]
            pos += 1
        if has_bias:
            bias_ref = refs[pos]
            pos += 1
        o_ref = refs[pos]

        def finish(r):
            if has_bias:
                r = r + bias_ref[...]
            if has_res:
                r = r + res_ref[...].astype(F32)
            o_ref[...] = r.astype(o_ref.dtype)

        prod = lax.dot_general(a_ref[...].astype(BF16), b_ref[...].astype(BF16), dims, preferred_element_type=F32)
        if nk == 1:
            finish(prod)
            return
        acc_ref = refs[pos + 1]
        k = pl.program_id(2)

        @pl.when(k == 0)
        def _():
            acc_ref[...] = prod

        @pl.when(k > 0)
        def _():
            acc_ref[...] += prod

        @pl.when(k == nk - 1)
        def _():
            finish(acc_ref[...])

    a_spec = pl.BlockSpec((tk, tm), lambda i, j, k: (k, i)) if ta else pl.BlockSpec((tm, tk), lambda i, j, k: (i, k))
    b_spec = pl.BlockSpec((tn, tk), lambda i, j, k: (j, k)) if tb else pl.BlockSpec((tk, tn), lambda i, j, k: (k, j))
    in_specs = [a_spec, b_spec]
    args = [a, b]
    if has_res:
        in_specs.append(pl.BlockSpec((tm, tn), lambda i, j, k: (i, j)))
        args.append(res)
    if has_bias:
        in_specs.append(pl.BlockSpec((1, tn), lambda i, j, k: (0, j)))
        args.append(bias)
    return pl.pallas_call(
        body, name=name, grid=(m // tm, n // tn, nk),
        in_specs=in_specs, out_specs=pl.BlockSpec((tm, tn), lambda i, j, k: (i, j)),
        out_shape=jax.ShapeDtypeStruct((m, n), out_dtype),
        scratch_shapes=[pltpu.VMEM((tm, tn), F32)] if nk > 1 else [],
        compiler_params=_cparams("parallel", "parallel", "arbitrary"),
    )(*args)


def rowcall(body, *, rows, ts, ins, outs, name, scratch=()):
    nt = rows // ts
    in_specs, args = [], []
    for arr, kind in ins:
        if kind == "row":
            in_specs.append(pl.BlockSpec((ts, arr.shape[1]), lambda i: (i, 0)))
        elif kind == "rev":
            in_specs.append(pl.BlockSpec((ts, arr.shape[1]), lambda i: (nt - 1 - i, 0)))
        elif kind == "full":
            nd = arr.ndim
            in_specs.append(pl.BlockSpec(arr.shape, lambda i, nd=nd: (0,) * nd))
        elif kind.startswith("prev:"):
            h = int(kind[5:])
            r = ts // h
            in_specs.append(pl.BlockSpec((h, arr.shape[1]), lambda i, r=r: (jnp.maximum(i * r - 1, 0), 0)))
        elif kind.startswith("next:"):
            h = int(kind[5:])
            r = ts // h
            last = rows // h - 1
            in_specs.append(pl.BlockSpec((h, arr.shape[1]), lambda i, r=r, last=last: (jnp.minimum((i + 1) * r, last), 0)))
        elif kind.startswith("revprev:"):
            h = int(kind[8:])
            r = ts // h
            in_specs.append(pl.BlockSpec((h, arr.shape[1]), lambda i, r=r: (jnp.maximum((nt - 1 - i) * r - 1, 0), 0)))
        else:
            raise ValueError(kind)
        args.append(arr)
    out_specs, out_shapes = [], []
    for shape, dtype, kind in outs:
        if kind == "row":
            out_specs.append(pl.BlockSpec((ts, shape[1]), lambda i: (i, 0)))
        elif kind == "rev":
            out_specs.append(pl.BlockSpec((ts, shape[1]), lambda i: (nt - 1 - i, 0)))
        else:
            nd = len(shape)
            out_specs.append(pl.BlockSpec(tuple(shape), lambda i, nd=nd: (0,) * nd))
        out_shapes.append(jax.ShapeDtypeStruct(tuple(shape), dtype))
    return pl.pallas_call(
        functools.partial(body, nt), name=name, grid=(nt,),
        in_specs=in_specs, out_specs=tuple(out_specs), out_shape=tuple(out_shapes),
        scratch_shapes=list(scratch),
        compiler_params=_cparams("arbitrary"),
    )(*args)


def _rms(x, g):
    return x * lax.rsqrt(jnp.mean(x * x, axis=-1, keepdims=True) + EPS) * g


def _layer_norm(x, g, b):
    mu = jnp.mean(x, axis=-1, keepdims=True)
    xc = x - mu
    var = jnp.mean(xc * xc, axis=-1, keepdims=True)
    return xc * lax.rsqrt(var + LN_EPS) * g + b


def _sigmoid(x):
    return 1.0 / (1.0 + jnp.exp(-x))


def _silu(x):
    return x * _sigmoid(x)


def _gelu(x):
    return 0.5 * x * (1.0 + jnp.tanh(math.sqrt(2.0 / math.pi) * (x + 0.044715 * (x * x * x))))


def _acc(ref, i, val):
    s = jnp.sum(val, axis=0, keepdims=True)

    @pl.when(i == 0)
    def _():
        ref[...] = jnp.zeros_like(ref)

    ref[...] += jnp.broadcast_to(s, ref.shape)


def rms_fwd(x, g, name):
    s, c = x.shape

    def body(nt, x_ref, g_ref, o_ref):
        o_ref[...] = _rms(x_ref[...], g_ref[...]).astype(BF16)

    return rowcall(body, rows=s, ts=min(ROW_TILE, s), ins=[(x, "row"), (g, "full")],
                   outs=[((s, c), BF16, "row")], name=name)[0]


def rms_bwd(x, g, dxn, dres, name):
    s, c = x.shape

    def body(nt, x_ref, g_ref, d_ref, r_ref, dx_ref, dg_ref):
        i = pl.program_id(0)
        _, vjp = jax.vjp(_rms, x_ref[...], g_ref[...])
        dx, dg = vjp(d_ref[...].astype(F32))
        dx_ref[...] = dx + r_ref[...]
        _acc(dg_ref, i, dg)

    return rowcall(body, rows=s, ts=min(ROW_TILE, s),
                   ins=[(x, "row"), (g, "full"), (dxn, "row"), (dres, "row")],
                   outs=[((s, c), F32, "row"), ((8, c), F32, "acc")], name=name)


def _partner(t):
    lane = lax.broadcasted_iota(jnp.int32, t.shape, 1)
    half = QK_ROPE // 2
    return jnp.where(lane % QK_ROPE < half, pltpu.roll(t, LANES - half, 1), pltpu.roll(t, half, 1))


def _rope(t, cos, sin):
    return t * cos + _partner(t) * sin


def _rope_t(d, cos, sin):
    return d * cos + _partner(d * sin)


def rope_tables(s):
    half = QK_ROPE // 2
    inv = ROPE_BASE ** (-jnp.arange(half, dtype=F32) / half)
    ang = jnp.arange(s).astype(F32)[:, None] * inv[None, :]
    cos, sin = jnp.cos(ang), jnp.sin(ang)
    z = jnp.zeros((s, LANES - QK_ROPE), F32)
    return jnp.concatenate([cos, cos, z], axis=1), jnp.concatenate([-sin, sin, z], axis=1)


H0_A, H0_G, H0_Q, H0_KV, H0_KR, H0_W = 0, 512, 1024, 1280, 1408, 1536


def _mixpre_fn(a, g, q, kv, qn, kvn):
    return a * _sigmoid(g), _rms(q, qn), _rms(kv, kvn)


def _h0_parts(h_ref):
    return (h_ref[:, H0_A:H0_G], h_ref[:, H0_G:H0_Q], h_ref[:, H0_Q:H0_KV], h_ref[:, H0_KV:H0_KR])


def mixpre_fwd(h0, qn, kvn, cos, sin):
    s = h0.shape[0]

    def body(nt, h_ref, qn_ref, kvn_ref, cos_ref, sin_ref, u0_ref, cq_ref, ckv_ref, kr_ref):
        u0, cq, ckv = _mixpre_fn(*_h0_parts(h_ref), qn_ref[...], kvn_ref[...])
        u0_ref[...] = u0
        cq_ref[...] = cq.astype(BF16)
        ckv_ref[...] = ckv.astype(BF16)
        kr_ref[...] = _rope(h_ref[:, H0_KR:H0_W], cos_ref[...], sin_ref[...]).astype(BF16)

    return rowcall(body, rows=s, ts=min(ROW_TILE, s),
                   ins=[(h0, "row"), (qn, "full"), (kvn, "full"), (cos, "row"), (sin, "row")],
                   outs=[((s, CONV_WIDTH), F32, "row"), ((s, Q_LORA), BF16, "row"),
                         ((s, KV_LORA), BF16, "row"), ((s, LANES), BF16, "row")], name="mixpre_fwd")


def mixpre_bwd(h0, qn, kvn, cos, sin, du0, dcq, dckv, dkr):
    s = h0.shape[0]

    def body(nt, h_ref, qn_ref, kvn_ref, cos_ref, sin_ref, du0_ref, dcq_ref, dckv_ref, dkr_ref,
             dh_ref, dqn_ref, dkvn_ref):
        i = pl.program_id(0)
        _, vjp = jax.vjp(_mixpre_fn, *_h0_parts(h_ref), qn_ref[...], kvn_ref[...])
        da, dg, dq, dkv, dqn, dkvn = vjp((du0_ref[...], dcq_ref[...], dckv_ref[...]))
        dh_ref[:, H0_A:H0_G] = da.astype(BF16)
        dh_ref[:, H0_G:H0_Q] = dg.astype(BF16)
        dh_ref[:, H0_Q:H0_KV] = dq.astype(BF16)
        dh_ref[:, H0_KV:H0_KR] = dkv.astype(BF16)
        dkr = dkr_ref[:, :LANES]
        for h in range(1, N_HEADS):
            dkr = dkr + dkr_ref[:, h * LANES:(h + 1) * LANES]
        dh_ref[:, H0_KR:H0_W] = _rope_t(dkr, cos_ref[...], sin_ref[...]).astype(BF16)
        _acc(dqn_ref, i, dqn)
        _acc(dkvn_ref, i, dkvn)

    return rowcall(body, rows=s, ts=min(ROW_TILE, s),
                   ins=[(h0, "row"), (qn, "full"), (kvn, "full"), (cos, "row"), (sin, "row"),
                        (du0, "row"), (dcq, "row"), (dckv, "row"), (dkr, "row")],
                   outs=[((s, H0_W), BF16, "row"), ((8, Q_LORA), F32, "acc"), ((8, KV_LORA), F32, "acc")],
                   name="mixpre_bwd")


def _conv_taps(ext_ref, w_ref, ts, first, ntaps, flip=False):
    acc = None
    for k in range(ntaps):
        term = w_ref[pl.ds(ntaps - 1 - k if flip else k, 1), :] * ext_ref[pl.ds(first + k, ts), :]
        acc = term if acc is None else acc + term
    return acc


def _ln_silu(u1, g, b):
    return _silu(_layer_norm(u1, g, b))


SUBLANES = 8


def _fill_shifted(sh_ref, parts, rows):
    pos = 0
    for p in parts:
        sh_ref[0, pl.ds(pos, p.shape[0]), :] = p
        pos += p.shape[0]
    sh_ref[0, pl.ds(rows, SUBLANES), :] = jnp.zeros((SUBLANES, sh_ref.shape[2]), F32)
    for r in range(1, SUBLANES):
        sh_ref[r, pl.ds(0, rows), :] = sh_ref[0, pl.ds(r, rows), :]


def _window(sh_ref, off, n):
    r = off % SUBLANES
    return sh_ref[r, pl.ds(off - r, n), :]


def _taps_aligned(sh_ref, w_ref, n, first, ntaps, flip=False):
    acc = None
    for k in range(ntaps):
        term = w_ref[pl.ds(ntaps - 1 - k if flip else k, 1), :] * _window(sh_ref, first + k, n)
        acc = term if acc is None else acc + term
    return acc


def _conv_scratch(ts, c):
    return pltpu.VMEM((SUBLANES, ts + CONV_HALO + SUBLANES, c), F32)


def convln_fwd(u0, w, b, lg, lb):
    s, c = u0.shape
    ts = min(ROW_TILE, s)
    rc = min(CONV_ROW_CHUNK, ts)
    first = CONV_HALO - (CONV_K - 1)

    def body(nt, cur_ref, prev_ref, w_ref, b_ref, lg_ref, lb_ref, o_ref, sh_ref):
        i = pl.program_id(0)
        _fill_shifted(sh_ref, [jnp.where(i > 0, prev_ref[...], 0.0), cur_ref[...]], ts + CONV_HALO)
        for r0 in range(0, ts, rc):
            u1 = _taps_aligned(sh_ref, w_ref, rc, first + r0, CONV_K) + b_ref[...]
            o_ref[pl.ds(r0, rc), :] = _ln_silu(u1, lg_ref[...], lb_ref[...]).astype(BF16)

    return rowcall(body, rows=s, ts=ts,
                   ins=[(u0, "row"), (u0, f"prev:{CONV_HALO}"), (w, "full"), (b, "full"), (lg, "full"), (lb, "full")],
                   outs=[((s, c), BF16, "row")], name="convln_fwd", scratch=[_conv_scratch(ts, c)])[0]


def convln_bwd1(u0, w, b, lg, lb, du):
    s, c = u0.shape
    ts = min(ROW_TILE, s)
    rc = min(CONV_ROW_CHUNK, ts)
    first = CONV_HALO - (CONV_K - 1)

    def body(nt, cur_ref, prev_ref, w_ref, b_ref, lg_ref, lb_ref, du_ref, du1_ref, dlg_ref, dlb_ref, dcb_ref, sh_ref):
        i = pl.program_id(0)
        _fill_shifted(sh_ref, [jnp.where(i > 0, prev_ref[...], 0.0), cur_ref[...]], ts + CONV_HALO)
        sums = [jnp.zeros((1, c), F32)] * 3
        for r0 in range(0, ts, rc):
            u1 = _taps_aligned(sh_ref, w_ref, rc, first + r0, CONV_K) + b_ref[...]
            _, vjp = jax.vjp(_ln_silu, u1, lg_ref[...], lb_ref[...])
            du1, dlg, dlb = vjp(du_ref[pl.ds(r0, rc), :].astype(F32))
            du1_ref[pl.ds(r0, rc), :] = du1
            parts = (dlg, dlb, jnp.sum(du1, axis=0, keepdims=True))
            sums = [a + jnp.sum(p, axis=0, keepdims=True) for a, p in zip(sums, parts)]
        _acc(dlg_ref, i, sums[0])
        _acc(dlb_ref, i, sums[1])
        _acc(dcb_ref, i, sums[2])

    return rowcall(body, rows=s, ts=ts,
                   ins=[(u0, "row"), (u0, f"prev:{CONV_HALO}"), (w, "full"), (b, "full"), (lg, "full"), (lb, "full"),
                        (du, "row")],
                   outs=[((s, c), F32, "row"), ((8, c), F32, "acc"), ((8, c), F32, "acc"), ((8, c), F32, "acc")],
                   name="convln_bwd1", scratch=[_conv_scratch(ts, c)])


def convln_bwd2(u0, w, du1):
    s, c = u0.shape
    ts = min(ROW_TILE, s)
    rc = min(CONV_ROW_CHUNK, ts)
    first = CONV_HALO - (CONV_K - 1)

    def body(nt, cur_ref, prev_ref, d_ref, dnext_ref, w_ref, du0_ref, dw_ref, sh_ref, dsh_ref):
        i = pl.program_id(0)
        _fill_shifted(sh_ref, [jnp.where(i > 0, prev_ref[...], 0.0), cur_ref[...]], ts + CONV_HALO)
        _fill_shifted(dsh_ref, [d_ref[...], jnp.where(i < nt - 1, dnext_ref[...], 0.0)], ts + CONV_HALO)
        for r0 in range(0, ts, rc):
            du0_ref[pl.ds(r0, rc), :] = _taps_aligned(dsh_ref, w_ref, rc, r0, CONV_K, flip=True)

        @pl.when(i == 0)
        def _():
            dw_ref[...] = jnp.zeros_like(dw_ref)

        for k in range(CONV_K):
            part = jnp.zeros((SUBLANES, c), F32)
            for r0 in range(0, ts, rc):
                prod = d_ref[pl.ds(r0, rc), :] * _window(sh_ref, first + k + r0, rc)
                for a in range(0, rc, SUBLANES):
                    part = part + prod[a:a + SUBLANES]
            dw_ref[pl.ds(k, 1), :] += jnp.sum(part, axis=0, keepdims=True)

    return rowcall(body, rows=s, ts=ts,
                   ins=[(u0, "row"), (u0, f"prev:{CONV_HALO}"), (du1, "row"), (du1, f"next:{CONV_HALO}"), (w, "full")],
                   outs=[((s, c), F32, "row"), ((CONV_HALO, c), F32, "acc")], name="convln_bwd2",
                   scratch=[_conv_scratch(ts, c), _conv_scratch(ts, c)])


def qrope_fwd(qraw, cos, sin):
    s = qraw.shape[0]

    def body(nt, q_ref, cos_ref, sin_ref, o_ref):
        cos_v, sin_v = cos_ref[...] * ATT_SCALE, sin_ref[...] * ATT_SCALE
        for h in range(N_HEADS):
            nope = q_ref[:, h * HEAD_PAD:h * HEAD_PAD + LANES] * ATT_SCALE
            o_ref[:, h * HEAD_PAD:h * HEAD_PAD + LANES] = nope.astype(BF16)
            r = q_ref[:, h * HEAD_PAD + LANES:(h + 1) * HEAD_PAD]
            o_ref[:, h * HEAD_PAD + LANES:(h + 1) * HEAD_PAD] = _rope(r, cos_v, sin_v).astype(BF16)

    return rowcall(body, rows=s, ts=min(ROW_TILE, s), ins=[(qraw, "row"), (cos, "row"), (sin, "row")],
                   outs=[((s, N_HEADS * HEAD_PAD), BF16, "row")], name="qrope_fwd")[0]


def qrope_bwd(dq, cos, sin):
    s = dq.shape[0]

    def body(nt, d_ref, cos_ref, sin_ref, o_ref):
        cos_v, sin_v = cos_ref[...] * ATT_SCALE, sin_ref[...] * ATT_SCALE
        for h in range(N_HEADS):
            nope = d_ref[:, h * HEAD_PAD:h * HEAD_PAD + LANES] * ATT_SCALE
            o_ref[:, h * HEAD_PAD:h * HEAD_PAD + LANES] = nope.astype(BF16)
            r = d_ref[:, h * HEAD_PAD + LANES:(h + 1) * HEAD_PAD].astype(F32)
            o_ref[:, h * HEAD_PAD + LANES:(h + 1) * HEAD_PAD] = _rope_t(r, cos_v, sin_v).astype(BF16)

    return rowcall(body, rows=s, ts=min(ROW_TILE, s), ins=[(dq, "row"), (cos, "row"), (sin, "row")],
                   outs=[((s, N_HEADS * HEAD_PAD), BF16, "row")], name="qrope_bwd")[0]


_NT = (((1,), (1,)), ((), ()))
_TN = (((0,), (0,)), ((), ()))


def _scores(q, kvr, diagonal):
    s = lax.dot_general(q, kvr, _NT, preferred_element_type=F32)
    if not diagonal:
        return s
    row = lax.broadcasted_iota(jnp.int32, s.shape, 0)
    col = lax.broadcasted_iota(jnp.int32, s.shape, 1)
    return jnp.where(col <= row, s, -jnp.inf)


def _on_causal_pairs(pair, k_blk, fn):
    @pl.when(k_blk < 2 * pair)
    def _():
        fn(0, False)
        fn(1, False)

    @pl.when(k_blk == 2 * pair)
    def _():
        fn(0, True)
        fn(1, False)

    @pl.when(k_blk == 2 * pair + 1)
    def _():
        fn(1, True)


def attn_fwd(q, kv, kr):
    s = q.shape[0]
    t = min(ATT_TILE, s // 2)
    n = s // t
    np_ = n // 2

    def body(q_ref, kv_ref, kr_ref, o_ref, lse_ref, m_ref, l_ref, acc_ref):
        i, j = pl.program_id(1), pl.program_id(2)

        @pl.when(j == 0)
        def _():
            m_ref[...] = jnp.full_like(m_ref, -jnp.inf)
            l_ref[...] = jnp.zeros_like(l_ref)
            acc_ref[...] = jnp.zeros_like(acc_ref)

        def block(sub, diagonal):
            kvv = kv_ref[...]
            kvr = jnp.concatenate([kvv, kr_ref[...]], axis=1)
            sc = _scores(q_ref[pl.ds(sub * t, t), :], kvr, diagonal)
            m_prev = m_ref[sub]
            m_new = jnp.maximum(m_prev, jnp.max(sc, axis=-1, keepdims=True))
            alpha = jnp.exp(m_prev - m_new)
            p = jnp.exp(sc - m_new)
            l_ref[sub] = alpha * l_ref[sub] + jnp.sum(p, axis=-1, keepdims=True)
            acc_ref[sub] = alpha * acc_ref[sub] + jnp.dot(p.astype(BF16), kvv, preferred_element_type=F32)
            m_ref[sub] = m_new

        _on_causal_pairs(i, j, block)

        @pl.when(j == 2 * i + 1)
        def _():
            for sub in range(2):
                l = l_ref[sub]
                o_ref[pl.ds(sub * t, t), :] = (acc_ref[sub] / l).astype(BF16)
                lse_ref[pl.ds(sub * t, t), :] = jnp.broadcast_to(m_ref[sub] + jnp.log(l), (t, LANES))

    kj = lambda h, i, j: (jnp.minimum(j, 2 * i + 1), h)
    return pl.pallas_call(
        body, name="attn_fwd", grid=(N_HEADS, np_, n),
        in_specs=[pl.BlockSpec((2 * t, HEAD_PAD), lambda h, i, j: (i, h)),
                  pl.BlockSpec((t, LANES), kj),
                  pl.BlockSpec((t, LANES), lambda h, i, j: (jnp.minimum(j, 2 * i + 1), 0))],
        out_specs=(pl.BlockSpec((2 * t, LANES), lambda h, i, j: (i, h)),
                   pl.BlockSpec((2 * t, LANES), lambda h, i, j: (i, h))),
        out_shape=(jax.ShapeDtypeStruct((s, N_HEADS * LANES), BF16),
                   jax.ShapeDtypeStruct((s, N_HEADS * LANES), F32)),
        scratch_shapes=[pltpu.VMEM((2, t, 1), F32), pltpu.VMEM((2, t, 1), F32), pltpu.VMEM((2, t, LANES), F32)],
        compiler_params=_cparams("parallel", "parallel", "arbitrary"),
    )(q, kv, kr)


def attn_bwd(q, kv, kr, o, do, lse):
    s = q.shape[0]
    t = min(ATT_TILE, s // 2)
    n = s // t
    np_ = n // 2

    def body(q_ref, kv_ref, kr_ref, o_ref, do_ref, lse_ref, dq_ref, dkv_ref, dkr_ref):
        j, i = pl.program_id(1), pl.program_id(2)

        @pl.when((i == 0) & (j == 0))
        def _():
            dq_ref[...] = jnp.zeros_like(dq_ref)

        @pl.when(i == 0)
        def _():
            dkv_ref[...] = jnp.zeros_like(dkv_ref)
            dkr_ref[...] = jnp.zeros_like(dkr_ref)

        def block(sub, diagonal):
            sl = pl.ds(sub * t, t)
            qv, dov, kvv = q_ref[sl, :], do_ref[sl, :], kv_ref[...]
            kvr = jnp.concatenate([kvv, kr_ref[...]], axis=1)
            p = jnp.exp(_scores(qv, kvr, diagonal) - lse_ref[sl, :1])
            dp = lax.dot_general(dov, kvv, _NT, preferred_element_type=F32)
            delta = jnp.sum(dov.astype(F32) * o_ref[sl, :].astype(F32), axis=-1, keepdims=True)
            ds = (p * (dp - delta)).astype(BF16)
            dk = lax.dot_general(ds, qv, _TN, preferred_element_type=F32)
            dkv_ref[...] += lax.dot_general(p.astype(BF16), dov, _TN, preferred_element_type=F32) + dk[:, :LANES]
            dkr_ref[...] += dk[:, LANES:]
            rows = pl.ds(pl.multiple_of((2 * i + sub) * t, t), t)
            dq_ref[rows, :] += jnp.dot(ds, kvr, preferred_element_type=F32)

        _on_causal_pairs(i, j, block)

    qi = lambda h, j, i: (jnp.maximum(i, lax.div(j, 2)), h)
    kj = lambda h, j, i: (j, h)
    return pl.pallas_call(
        body, name="attn_bwd", grid=(N_HEADS, n, np_),
        in_specs=[pl.BlockSpec((2 * t, HEAD_PAD), qi), pl.BlockSpec((t, LANES), kj),
                  pl.BlockSpec((t, LANES), lambda h, j, i: (j, 0)),
                  pl.BlockSpec((2 * t, LANES), qi), pl.BlockSpec((2 * t, LANES), qi), pl.BlockSpec((2 * t, LANES), qi)],
        out_specs=(pl.BlockSpec((s, HEAD_PAD), lambda h, j, i: (0, h)),
                   pl.BlockSpec((t, LANES), kj), pl.BlockSpec((t, LANES), kj)),
        out_shape=(jax.ShapeDtypeStruct((s, N_HEADS * HEAD_PAD), F32),
                   jax.ShapeDtypeStruct((s, N_HEADS * LANES), F32), jax.ShapeDtypeStruct((s, N_HEADS * LANES), F32)),
        compiler_params=_cparams("parallel", "arbitrary", "arbitrary"),
    )(q, kv, kr, o, do, lse)


def ffn_fwd(x, g, wa, wb, cwa, cwb, ba, bb, wd, name):
    s, d = x.shape
    f = wa.shape[1]
    ts, tf = min(FFN_ROW_TILE, s), FFN_COL_TILE
    hal = FFN_HALO
    nj = f // tf
    first = hal - (FFN_K - 1)
    rc = min(FFN_ROW_CHUNK, ts)

    def body(x_ref, xp_ref, g_ref, wa_ref, wb_ref, cwa_ref, cwb_ref, ba_ref, bb_ref, wd_ref,
             xo_ref, xn_ref, hpa_ref, hpb_ref, act_ref, xe_ref, ea_ref, eb_ref):
        i, j = pl.program_id(0), pl.program_id(1)

        @pl.when(j == 0)
        def _():
            xn = _rms(x_ref[...], g_ref[...]).astype(BF16)
            xn_ref[...] = xn
            xe_ref[pl.ds(hal, ts), :] = xn
            xe_ref[pl.ds(0, hal), :] = jnp.where(i > 0, _rms(xp_ref[...], g_ref[...]), 0.0).astype(BF16)
            xo_ref[...] = x_ref[...]

        halves = ((0, ts // 2), (ts // 2, ts))
        for lo, hi in halves:
            e0, e1 = (0 if lo == 0 else hal + lo), hal + hi
            xe = xe_ref[pl.ds(e0, e1 - e0), :]
            ea_ref[pl.ds(e0, e1 - e0), :] = jnp.dot(xe, wa_ref[...], preferred_element_type=F32)
            eb_ref[pl.ds(e0, e1 - e0), :] = jnp.dot(xe, wb_ref[...], preferred_element_type=F32)
        for lo, hi in halves:
            hpa_ref[pl.ds(lo, hi - lo), :] = ea_ref[pl.ds(hal + lo, hi - lo), :].astype(BF16)
            hpb_ref[pl.ds(lo, hi - lo), :] = eb_ref[pl.ds(hal + lo, hi - lo), :].astype(BF16)
            for r0 in range(lo, hi, rc):
                ha = _conv_taps(ea_ref, cwa_ref, rc, first + r0, FFN_K) + ba_ref[...]
                hb = _conv_taps(eb_ref, cwb_ref, rc, first + r0, FFN_K) + bb_ref[...]
                act_ref[pl.ds(r0, rc), :] = (_silu(ha) * hb).astype(BF16)
            xo_ref[pl.ds(lo, hi - lo), :] += jnp.dot(act_ref[pl.ds(lo, hi - lo), :], wd_ref[...],
                                                     preferred_element_type=F32)

    r = ts // hal
    row = pl.BlockSpec((ts, d), lambda i, j: (i, 0))
    prev = pl.BlockSpec((hal, d), lambda i, j: (jnp.maximum(i * r - 1, 0), 0))
    gsp = pl.BlockSpec((1, d), lambda i, j: (0, 0))
    wup = pl.BlockSpec((d, tf), lambda i, j: (0, j))
    cwsp = pl.BlockSpec((8, tf), lambda i, j: (0, j))
    bsp = pl.BlockSpec((1, tf), lambda i, j: (0, j))
    wdn = pl.BlockSpec((tf, d), lambda i, j: (j, 0))
    hid = pl.BlockSpec((ts, tf), lambda i, j: (i, j))
    return pl.pallas_call(
        body, name=name, grid=(s // ts, nj),
        in_specs=[row, prev, gsp, wup, wup, cwsp, cwsp, bsp, bsp, wdn],
        out_specs=(row, row, hid, hid, hid),
        out_shape=(jax.ShapeDtypeStruct((s, d), F32), jax.ShapeDtypeStruct((s, d), BF16),
                   jax.ShapeDtypeStruct((s, f), BF16), jax.ShapeDtypeStruct((s, f), BF16),
                   jax.ShapeDtypeStruct((s, f), BF16)),
        scratch_shapes=[pltpu.VMEM((ts + hal, d), BF16), pltpu.VMEM((ts + hal, tf), F32),
                        pltpu.VMEM((ts + hal, tf), F32)],
        compiler_params=_cparams("parallel", "arbitrary"),
    )(x, x, g, wa, wb, cwa, cwb, ba, bb, wd)


def ffn_bwd(dy, hpa, hpb, wa, wb, cwa, cwb, ba, bb, wd, name):
    s, d = dy.shape
    f = hpa.shape[1]
    ts, tf = min(FFN_ROW_TILE, s), FFN_COL_TILE
    hal = FFN_HALO
    nt, nj = s // ts, f // tf
    te = ts + hal
    first = hal - (FFN_K - 1)
    rc = min(FFN_ROW_CHUNK, ts)

    def body(dy_ref, dyn_ref, a_ref, ap_ref, an_ref, b_ref, bp_ref, bn_ref, wa_ref, wb_ref, cwa_ref, cwb_ref,
             ba_ref, bb_ref, wd_ref,
             dxn_ref, dpa_ref, dpb_ref, dwa_ref, dwb_ref, dba_ref, dbb_ref,
             dye_ref, ea_ref, eb_ref, dact_ref, da_ref, db_ref):
        i, j = pl.program_id(0), pl.program_id(1)
        last = i == nt - 1

        @pl.when(j == 0)
        def _():
            dye_ref[pl.ds(0, ts), :] = dy_ref[...].astype(BF16)
            dye_ref[pl.ds(ts, hal), :] = jnp.where(last, 0.0, dyn_ref[...]).astype(BF16)
            dxn_ref[...] = jnp.zeros_like(dxn_ref)

        @pl.when((i == 0) & (j == 0))
        def _():
            for r in (dwa_ref, dwb_ref, dba_ref, dbb_ref):
                r[...] = jnp.zeros_like(r)

        halves = tuple((k * ts // FFN_BWD_PARTS, (k + 1) * ts // FFN_BWD_PARTS) for k in range(FFN_BWD_PARTS))
        for lo, hi in halves:
            n = hi - lo + (hal if hi == ts else 0)
            dact_ref[pl.ds(lo, n), :] = lax.dot_general(dye_ref[pl.ds(lo, n), :], wd_ref[...], _NT,
                                                        preferred_element_type=F32)
        for cur, prev, nxt, ext in ((a_ref, ap_ref, an_ref, ea_ref), (b_ref, bp_ref, bn_ref, eb_ref)):
            ext[pl.ds(0, hal), :] = jnp.where(i > 0, prev[...].astype(F32), 0.0)
            ext[pl.ds(hal, ts), :] = cur[...].astype(F32)
            ext[pl.ds(hal + ts, hal), :] = jnp.where(last, 0.0, nxt[...].astype(F32))
        zero = jnp.zeros((1, tf), F32)
        sums = {"ba": zero, "bb": zero, **{("a", k): zero for k in range(FFN_K)}, **{("b", k): zero for k in range(FFN_K)}}
        for r0 in list(range(0, ts, rc)) + [ts]:
            n = rc if r0 < ts else hal
            win_a = [ea_ref[pl.ds(first + r0 + k, n), :] for k in range(FFN_K)]
            win_b = [eb_ref[pl.ds(first + r0 + k, n), :] for k in range(FFN_K)]
            ha = sum(cwa_ref[pl.ds(k, 1), :] * win_a[k] for k in range(FFN_K)) + ba_ref[...]
            hb = sum(cwb_ref[pl.ds(k, 1), :] * win_b[k] for k in range(FFN_K)) + bb_ref[...]
            sig = _sigmoid(ha)
            gs = dact_ref[pl.ds(r0, n), :] * sig
            dha = gs * hb * (1.0 + ha * (1.0 - sig))
            dhb = gs * ha
            da_ref[pl.ds(r0, n), :] = dha
            db_ref[pl.ds(r0, n), :] = dhb
            if r0 < ts:
                sums["ba"] = sums["ba"] + jnp.sum(dha, axis=0, keepdims=True)
                sums["bb"] = sums["bb"] + jnp.sum(dhb, axis=0, keepdims=True)
                for k in range(FFN_K):
                    sums["a", k] = sums["a", k] + jnp.sum(dha * win_a[k], axis=0, keepdims=True)
                    sums["b", k] = sums["b", k] + jnp.sum(dhb * win_b[k], axis=0, keepdims=True)
        for lo, hi in halves:
            for r0 in range(lo, hi, rc):
                dpa_ref[pl.ds(r0, rc), :] = _conv_taps(da_ref, cwa_ref, rc, r0, FFN_K, flip=True).astype(BF16)
                dpb_ref[pl.ds(r0, rc), :] = _conv_taps(db_ref, cwb_ref, rc, r0, FFN_K, flip=True).astype(BF16)
            rows = pl.ds(lo, hi - lo)
            dxn_ref[rows, :] += (lax.dot_general(dpa_ref[rows, :], wa_ref[...], _NT, preferred_element_type=F32)
                                 + lax.dot_general(dpb_ref[rows, :], wb_ref[...], _NT, preferred_element_type=F32))
        dba_ref[j] += jnp.broadcast_to(sums["ba"], (8, tf))
        dbb_ref[j] += jnp.broadcast_to(sums["bb"], (8, tf))
        row = lax.broadcasted_iota(jnp.int32, (8, tf), 0)
        dwa_ref[j] += sum(jnp.where(row == k, sums["a", k], 0.0) for k in range(FFN_K))
        dwb_ref[j] += sum(jnp.where(row == k, sums["b", k], 0.0) for k in range(FFN_K))

    r = ts // hal
    lastblk = s // hal - 1
    row = pl.BlockSpec((ts, d), lambda i, j: (i, 0))
    rown = pl.BlockSpec((hal, d), lambda i, j: (jnp.minimum((i + 1) * r, lastblk), 0))
    cur = pl.BlockSpec((ts, tf), lambda i, j: (i, j))
    prev = pl.BlockSpec((hal, tf), lambda i, j: (jnp.maximum(i * r - 1, 0), j))
    nxt = pl.BlockSpec((hal, tf), lambda i, j: (jnp.minimum((i + 1) * r, lastblk), j))
    wup = pl.BlockSpec((d, tf), lambda i, j: (0, j))
    cwsp = pl.BlockSpec((8, tf), lambda i, j: (0, j))
    bsp = pl.BlockSpec((1, tf), lambda i, j: (0, j))
    wdn = pl.BlockSpec((tf, d), lambda i, j: (j, 0))
    accsp = pl.BlockSpec((nj, 8, tf), lambda i, j: (0, 0, 0))
    accshape = jax.ShapeDtypeStruct((nj, 8, tf), F32)
    return pl.pallas_call(
        body, name=name, grid=(nt, nj),
        in_specs=[row, rown, cur, prev, nxt, cur, prev, nxt, wup, wup, cwsp, cwsp, bsp, bsp, wdn],
        out_specs=(row, cur, cur, accsp, accsp, accsp, accsp),
        out_shape=(jax.ShapeDtypeStruct((s, d), F32), jax.ShapeDtypeStruct((s, f), BF16),
                   jax.ShapeDtypeStruct((s, f), BF16), accshape, accshape, accshape, accshape),
        scratch_shapes=[pltpu.VMEM((te, d), BF16), pltpu.VMEM((ts + 2 * hal, tf), F32),
                        pltpu.VMEM((ts + 2 * hal, tf), F32), pltpu.VMEM((te, tf), F32),
                        pltpu.VMEM((te, tf), F32), pltpu.VMEM((te, tf), F32)],
        compiler_params=_cparams("arbitrary", "arbitrary"),
    )(dy, dy, hpa, hpa, hpa, hpb, hpb, hpb, wa, wb, cwa, cwb, ba, bb, wd)


NQ = 4
SQ = SSM_STATE * 8
NS = SSM_GROUPS * SSM_STATE


def _s5_disc(log_dt, a_re, a_im, b_re, b_im, expand):
    dt = jnp.exp(log_dt)
    mag = jnp.exp(a_re * dt)
    lb_re, lb_im = mag * jnp.cos(a_im * dt), mag * jnp.sin(a_im * dt)
    den = a_re * a_re + a_im * a_im
    nr, ni = lb_re - 1.0, lb_im
    f_re = (nr * a_re + ni * a_im) / den
    f_im = (ni * a_re - nr * a_im) / den
    fe_re = jnp.dot(f_re, expand, precision=lax.Precision.HIGHEST, preferred_element_type=F32)
    fe_im = jnp.dot(f_im, expand, precision=lax.Precision.HIGHEST, preferred_element_type=F32)
    return lb_re, lb_im, fe_re * b_re - fe_im * b_im, fe_re * b_im + fe_im * b_re


def _expand_matrix():
    e = np.zeros((SSM_STATE, SSM_STATE * SSM_GROUP), np.float32)
    for p in range(SSM_STATE):
        e[p, p * SSM_GROUP:(p + 1) * SSM_GROUP] = 1.0
    return jnp.asarray(e)


def s5_params_fwd(log_dt, a_re, a_im, b_re, b_im):
    expand = _expand_matrix()

    def body(ld_ref, ar_ref, ai_ref, br_ref, bi_ref, e_ref, lr_ref, li_ref, bbr_ref, bbi_ref):
        lr, li, bbr, bbi = _s5_disc(ld_ref[...], ar_ref[...], ai_ref[...], br_ref[...], bi_ref[...], e_ref[...])
        lr_ref[...] = lr
        li_ref[...] = li
        bbr_ref[...] = bbr
        bbi_ref[...] = bbi

    g, p, pc = SSM_GROUPS, SSM_STATE, SSM_STATE * SSM_GROUP
    return pl.pallas_call(
        body, name="s5_params_fwd",
        out_shape=(jax.ShapeDtypeStruct((g, p), F32), jax.ShapeDtypeStruct((g, p), F32),
                   jax.ShapeDtypeStruct((g, pc), F32), jax.ShapeDtypeStruct((g, pc), F32)),
    )(log_dt, a_re, a_im, b_re, b_im, expand)


def s5_params_bwd(log_dt, a_re, a_im, b_re, b_im, dlr, dli, dbbr, dbbi):
    expand = _expand_matrix()

    def body(ld_ref, ar_ref, ai_ref, br_ref, bi_ref, e_ref, dlr_ref, dli_ref, dbbr_ref, dbbi_ref,
             dld_ref, dar_ref, dai_ref, dbr_ref, dbi_ref):
        e = e_ref[...]
        f = lambda ld, ar, ai, br, bi: _s5_disc(ld, ar, ai, br, bi, e)
        _, vjp = jax.vjp(f, ld_ref[...], ar_ref[...], ai_ref[...], br_ref[...], bi_ref[...])
        dld, dar, dai, dbr, dbi = vjp((dlr_ref[...], dli_ref[...], dbbr_ref[...], dbbi_ref[...]))
        dld_ref[...] = dld
        dar_ref[...] = dar
        dai_ref[...] = dai
        dbr_ref[...] = dbr
        dbi_ref[...] = dbi

    g, p, pc = SSM_GROUPS, SSM_STATE, SSM_STATE * SSM_GROUP
    return pl.pallas_call(
        body, name="s5_params_bwd",
        out_shape=(jax.ShapeDtypeStruct((g, 1), F32), jax.ShapeDtypeStruct((g, p), F32),
                   jax.ShapeDtypeStruct((g, p), F32), jax.ShapeDtypeStruct((g, pc), F32),
                   jax.ShapeDtypeStruct((g, pc), F32)),
    )(log_dt, a_re, a_im, b_re, b_im, expand, dlr, dli, dbbr, dbbi)


def _cmul(ar, ai, br, bi):
    return ar * br - ai * bi, ar * bi + ai * br


def _power_rows(lr, li, conj_rev):
    row = lax.broadcasted_iota(jnp.int32, (8, NS), 0)
    tr = jnp.zeros((8, NS), F32)
    ti = jnp.zeros((8, NS), F32)
    pr, pi = lr, li
    for r in range(8):
        dst = 7 - r if conj_rev else r
        tr = jnp.where(row == dst, pr, tr)
        ti = jnp.where(row == dst, -pi if conj_rev else pi, ti)
        if r < 7:
            pr, pi = _cmul(pr, pi, lr, li)
    return tr, ti


def _scan8(xr, xi, tr_ref, ti_ref, cr, ci, reverse):
    row = lax.broadcasted_iota(jnp.int32, xr.shape, 0)
    for d in (1, 2, 4):
        if reverse:
            sr, si = pltpu.roll(xr, 8 - d, 0), pltpu.roll(xi, 8 - d, 0)
            keep = row < 8 - d
            pw = 8 - d
        else:
            sr, si = pltpu.roll(xr, d, 0), pltpu.roll(xi, d, 0)
            keep = row >= d
            pw = d - 1
        mr, mi = _cmul(tr_ref[pl.ds(pw, 1), :], ti_ref[pl.ds(pw, 1), :], sr, si)
        xr = xr + jnp.where(keep, mr, 0.0)
        xi = xi + jnp.where(keep, mi, 0.0)
    mr, mi = _cmul(tr_ref[...], ti_ref[...], cr, ci)
    return xr + mr, xi + mi


def _row_of(x, r):
    row = lax.broadcasted_iota(jnp.int32, x.shape, 0)
    return jnp.sum(jnp.where(row == r, x, 0.0), axis=0, keepdims=True)


def s5_scan_fwd(u, lam_r, lam_i, bre, bim, cre, cim, dskip):
    s = u.shape[0]
    tt = min(SCAN_TILE, s)
    nb = tt // 8

    def body(nt, u_ref, lr_ref, li_ref, bre_ref, bim_ref, cre_ref, cim_ref, d_ref,
             xr_ref, xi_ref, y_ref, yg_ref, tr_ref, ti_ref, cr_ref, ci_ref):
        i = pl.program_id(0)

        @pl.when(i == 0)
        def _():
            tr, ti = _power_rows(lr_ref[...], li_ref[...], False)
            tr_ref[...] = tr
            ti_ref[...] = ti
            cr_ref[...] = jnp.zeros_like(cr_ref)
            ci_ref[...] = jnp.zeros_like(ci_ref)

        uv = u_ref[...]
        ub = uv.astype(BF16)
        for q in range(NQ):
            uq = ub[:, q * LANES:(q + 1) * LANES]
            xr_ref[:, q * SQ:(q + 1) * SQ] = jnp.dot(uq, bre_ref[q], preferred_element_type=F32)
            xi_ref[:, q * SQ:(q + 1) * SQ] = jnp.dot(uq, bim_ref[q], preferred_element_type=F32)

        def step(b, carry):
            cr, ci = carry
            rows = pl.ds(pl.multiple_of(b * 8, 8), 8)
            xr, xi = _scan8(xr_ref[rows, :], xi_ref[rows, :], tr_ref, ti_ref, cr, ci, False)
            xr_ref[rows, :] = xr
            xi_ref[rows, :] = xi
            return _row_of(xr, 7), _row_of(xi, 7)

        cr, ci = lax.fori_loop(0, nb, step, (cr_ref[...], ci_ref[...]), unroll=min(SCAN_UNROLL, nb))
        cr_ref[...] = cr
        ci_ref[...] = ci
        y = d_ref[...] * uv
        for q in range(NQ):
            yq = (jnp.dot(xr_ref[:, q * SQ:(q + 1) * SQ].astype(BF16), cre_ref[q], preferred_element_type=F32)
                  - jnp.dot(xi_ref[:, q * SQ:(q + 1) * SQ].astype(BF16), cim_ref[q], preferred_element_type=F32))
            y_ref[:, q * LANES:(q + 1) * LANES] = yq + y[:, q * LANES:(q + 1) * LANES]
        yg_ref[...] = _gelu(y_ref[...]).astype(BF16)

    return rowcall(body, rows=s, ts=tt,
                   ins=[(u, "row"), (lam_r, "full"), (lam_i, "full"), (bre, "full"), (bim, "full"),
                        (cre, "full"), (cim, "full"), (dskip, "full")],
                   outs=[((s, NS), F32, "row"), ((s, NS), F32, "row"), ((s, SSM_WIDTH), F32, "row"),
                         ((s, SSM_WIDTH), BF16, "row")], name="s5_scan_fwd",
                   scratch=[pltpu.VMEM((8, NS), F32), pltpu.VMEM((8, NS), F32),
                            pltpu.VMEM((1, NS), F32), pltpu.VMEM((1, NS), F32)])


def s5_scan_bwd(dyg, y, lam_r, lam_i, cre, cim):
    s = y.shape[0]
    tt = min(SCAN_TILE, s)
    nb = tt // 8

    def body(nt, dyg_ref, y_ref, lr_ref, li_ref, cre_ref, cim_ref,
             ar_ref, ai_ref, dy_ref, tr_ref, ti_ref, cr_ref, ci_ref):
        i = pl.program_id(0)

        @pl.when(i == 0)
        def _():
            tr, ti = _power_rows(lr_ref[...], li_ref[...], True)
            tr_ref[...] = tr
            ti_ref[...] = ti
            cr_ref[...] = jnp.zeros_like(cr_ref)
            ci_ref[...] = jnp.zeros_like(ci_ref)

        _, vjp = jax.vjp(_gelu, y_ref[...])
        dy = vjp(dyg_ref[...])[0]
        dyb = dy.astype(BF16)
        dy_ref[...] = dyb
        for q in range(NQ):
            dq = dyb[:, q * LANES:(q + 1) * LANES]
            ar_ref[:, q * SQ:(q + 1) * SQ] = lax.dot_general(dq, cre_ref[q], _NT, preferred_element_type=F32)
            ai_ref[:, q * SQ:(q + 1) * SQ] = -lax.dot_general(dq, cim_ref[q], _NT, preferred_element_type=F32)

        def step(b, carry):
            cr, ci = carry
            rows = pl.ds(pl.multiple_of((nb - 1 - b) * 8, 8), 8)
            xr, xi = _scan8(ar_ref[rows, :], ai_ref[rows, :], tr_ref, ti_ref, cr, ci, True)
            ar_ref[rows, :] = xr
            ai_ref[rows, :] = xi
            return _row_of(xr, 0), _row_of(xi, 0)

        cr, ci = lax.fori_loop(0, nb, step, (cr_ref[...], ci_ref[...]), unroll=min(SCAN_UNROLL, nb))
        cr_ref[...] = cr
        ci_ref[...] = ci

    return rowcall(body, rows=s, ts=tt,
                   ins=[(dyg, "rev"), (y, "rev"), (lam_r, "full"), (lam_i, "full"), (cre, "full"), (cim, "full")],
                   outs=[((s, NS), F32, "rev"), ((s, NS), F32, "rev"), ((s, SSM_WIDTH), BF16, "rev")],
                   name="s5_scan_bwd",
                   scratch=[pltpu.VMEM((8, NS), F32), pltpu.VMEM((8, NS), F32),
                            pltpu.VMEM((1, NS), F32), pltpu.VMEM((1, NS), F32)])


def s5_grads(u, dy, xr, xi, ar, ai, bre, bim, dskip):
    s = u.shape[0]
    tt = min(SCAN_TILE, s)

    def body(nt, u_ref, dy_ref, xr_ref, xrp_ref, xi_ref, xip_ref, ar_ref, ai_ref, bre_ref, bim_ref, d_ref,
             du_ref, dlr_ref, dli_ref, dbr_ref, dbi_ref, dcr_ref, dci_ref, dd_ref, er_ref, ei_ref):
        i = pl.program_id(0)

        @pl.when(i == 0)
        def _():
            for r in (dbr_ref, dbi_ref, dcr_ref, dci_ref):
                r[...] = jnp.zeros_like(r)

        uv, dyb = u_ref[...], dy_ref[...]
        dyf = dyb.astype(F32)
        av_r, av_i, xv_r, xv_i = ar_ref[...], ai_ref[...], xr_ref[...], xi_ref[...]
        er_ref[pl.ds(0, 8), :] = jnp.where(i > 0, xrp_ref[...], 0.0)
        ei_ref[pl.ds(0, 8), :] = jnp.where(i > 0, xip_ref[...], 0.0)
        er_ref[pl.ds(8, tt), :] = xv_r
        ei_ref[pl.ds(8, tt), :] = xv_i
        sr, si = er_ref[pl.ds(7, tt), :], ei_ref[pl.ds(7, tt), :]
        _acc(dlr_ref, i, av_r * sr + av_i * si)
        _acc(dli_ref, i, av_i * sr - av_r * si)
        _acc(dd_ref, i, dyf * uv)
        ub = uv.astype(BF16)
        ab_r, ab_i = av_r.astype(BF16), av_i.astype(BF16)
        xb_r, xb_i = xv_r.astype(BF16), xv_i.astype(BF16)
        du = d_ref[...] * dyf
        for q in range(NQ):
            cs, ss = slice(q * LANES, (q + 1) * LANES), slice(q * SQ, (q + 1) * SQ)
            dbr_ref[q] += lax.dot_general(ub[:, cs], ab_r[:, ss], _TN, preferred_element_type=F32)
            dbi_ref[q] += lax.dot_general(ub[:, cs], ab_i[:, ss], _TN, preferred_element_type=F32)
            dcr_ref[q] += lax.dot_general(xb_r[:, ss], dyb[:, cs], _TN, preferred_element_type=F32)
            dci_ref[q] -= lax.dot_general(xb_i[:, ss], dyb[:, cs], _TN, preferred_element_type=F32)
            du_ref[:, cs] = (du[:, cs]
                             + lax.dot_general(ab_r[:, ss], bre_ref[q], _NT, preferred_element_type=F32)
                             + lax.dot_general(ab_i[:, ss], bim_ref[q], _NT, preferred_element_type=F32))

    return rowcall(body, rows=s, ts=tt,
                   ins=[(u, "row"), (dy, "row"), (xr, "row"), (xr, "prev:8"), (xi, "row"), (xi, "prev:8"),
                        (ar, "row"), (ai, "row"), (bre, "full"), (bim, "full"), (dskip, "full")],
                   outs=[((s, SSM_WIDTH), F32, "row"), ((8, NS), F32, "acc"), ((8, NS), F32, "acc"),
                         ((NQ, LANES, SQ), F32, "acc"), ((NQ, LANES, SQ), F32, "acc"),
                         ((NQ, SQ, LANES), F32, "acc"), ((NQ, SQ, LANES), F32, "acc"),
                         ((8, SSM_WIDTH), F32, "acc")], name="s5_grads",
                   scratch=[pltpu.VMEM((tt + 8, NS), F32), pltpu.VMEM((tt + 8, NS), F32)])


def _glu_fn(za, zb):
    return za * _sigmoid(zb)


def glu_res_fwd(z, xres):
    s = z.shape[0]

    def body(nt, z_ref, x_ref, o_ref):
        o_ref[...] = x_ref[...] + _glu_fn(z_ref[:, :D_MODEL].astype(F32), z_ref[:, D_MODEL:].astype(F32))

    return rowcall(body, rows=s, ts=min(ROW_TILE, s), ins=[(z, "row"), (xres, "row")],
                   outs=[((s, D_MODEL), F32, "row")], name="glu_res_fwd")[0]


def glu_bwd(z, dout):
    s, c = z.shape

    def body(nt, z_ref, d_ref, dz_ref, dba_ref, dbb_ref):
        i = pl.program_id(0)
        _, vjp = jax.vjp(_glu_fn, z_ref[:, :D_MODEL].astype(F32), z_ref[:, D_MODEL:].astype(F32))
        dza, dzb = vjp(d_ref[...])
        dz_ref[:, :D_MODEL] = dza.astype(BF16)
        dz_ref[:, D_MODEL:] = dzb.astype(BF16)
        _acc(dba_ref, i, dza)
        _acc(dbb_ref, i, dzb)

    return rowcall(body, rows=s, ts=min(ROW_TILE, s), ins=[(z, "row"), (dout, "row")],
                   outs=[((s, c), BF16, "row"), ((8, D_MODEL), F32, "acc"), ((8, D_MODEL), F32, "acc")],
                   name="glu_bwd")


def loss_head(x, g, target):
    s, c = x.shape

    def body(nt, x_ref, g_ref, t_ref, loss_ref, dx_ref, dg_ref):
        i = pl.program_id(0)
        y, vjp = jax.vjp(_rms, x_ref[...], g_ref[...])
        err = y - t_ref[...]
        dx, dg = vjp(err * (1.0 / c))
        dx_ref[...] = dx
        _acc(dg_ref, i, dg)
        part = jnp.sum(jnp.sum(err * err, axis=-1, keepdims=True), axis=0, keepdims=True) * (0.5 / c)

        @pl.when(i == 0)
        def _():
            loss_ref[...] = jnp.zeros_like(loss_ref)

        loss_ref[...] += jnp.broadcast_to(part, loss_ref.shape)

    return rowcall(body, rows=s, ts=min(ROW_TILE, s), ins=[(x, "row"), (g, "full"), (target, "row")],
                   outs=[((8, LANES), F32, "acc"), ((s, c), F32, "row"), ((8, c), F32, "acc")], name="loss_head")


def _tile_rows(r, cands=(512, 256, 128, 64, 32, 16, 8)):
    return _pick(r, cands)


def add_to_bf16(a, b, name):
    n, r, c = a.shape
    tr = _tile_rows(r)

    def body(a_ref, b_ref, o_ref):
        o_ref[...] = (a_ref[...].astype(F32) + b_ref[...].astype(F32)).astype(BF16)

    spec = pl.BlockSpec((1, tr, c), lambda j, i: (j, i, 0))
    return pl.pallas_call(body, name=name, grid=(n, r // tr), in_specs=[spec, spec], out_specs=spec,
                          out_shape=jax.ShapeDtypeStruct((n, r, c), BF16),
                          compiler_params=_cparams("parallel", "parallel"))(a, b)


def sum_leading(a, name):
    n, r, c = a.shape
    tr = _tile_rows(r)

    def body(a_ref, o_ref):
        acc = a_ref[0].astype(F32)
        for k in range(1, n):
            acc = acc + a_ref[k].astype(F32)
        o_ref[...] = acc

    return pl.pallas_call(body, name=name, grid=(r // tr,),
                          in_specs=[pl.BlockSpec((n, tr, c), lambda i: (0, i, 0))],
                          out_specs=pl.BlockSpec((tr, c), lambda i: (i, 0)),
                          out_shape=jax.ShapeDtypeStruct((r, c), F32),
                          compiler_params=_cparams("parallel"))(a)


def adamw(w, g, m, v, name):
    r, c = w.shape
    tr = _tile_rows(r, (256, 128, 64, 32, 16, 8))
    c1 = 1.0 - ADAM_B1 ** ADAM_STEP
    c2 = 1.0 - ADAM_B2 ** ADAM_STEP

    def body(w_ref, g_ref, m_ref, v_ref, d_ref, nm_ref, nv_ref):
        gv = g_ref[...]
        mn = ADAM_B1 * m_ref[...] + (1.0 - ADAM_B1) * gv
        vn = ADAM_B2 * v_ref[...] + (1.0 - ADAM_B2) * (gv * gv)
        d_ref[...] = -ADAM_LR * ((mn / c1) / (jnp.sqrt(vn / c2) + ADAM_EPS) + ADAM_WD * w_ref[...])
        nm_ref[...] = mn
        nv_ref[...] = vn

    spec = pl.BlockSpec((tr, c), lambda i: (i, 0))
    shp = jax.ShapeDtypeStruct((r, c), F32)
    return pl.pallas_call(body, name=name, grid=(r // tr,), in_specs=[spec] * 4, out_specs=(spec,) * 3,
                          out_shape=(shp,) * 3, compiler_params=_cparams("parallel"))(w, g, m, v)


_ANY = pl.BlockSpec(memory_space=pl.ANY)


def all_gather8(block, name):
    r, c = block.shape

    def body(x_ref, out_ref, send_sems, recv_sems, local_sem):
        x, y, cc = lax.axis_index("x"), lax.axis_index("y"), lax.axis_index("c")
        me, sibling = (x, y, cc), (x, y, 1 - cc)
        chips = [(1 - x, y), (x, 1 - y), (1 - x, 1 - y)]

        def slot(px, py, pc):
            return out_ref.at[4 * px + 2 * py + pc]

        def copy(k, blk, to, src=None):
            return pltpu.make_async_remote_copy(
                src_ref=slot(*blk) if src is None else src, dst_ref=slot(*blk),
                send_sem=send_sems.at[k], recv_sem=recv_sems.at[k], device_id=to, device_id_type=MESH)

        mine = pltpu.make_async_copy(x_ref, slot(*me), local_sem)
        mine.start()
        first = [copy(0, me, sibling, src=x_ref)]
        first += [copy(1 + j, me, (*chip, cc), src=x_ref) for j, chip in enumerate(chips)]
        for cp in first:
            cp.start()
        passed = [copy(4 + j, (*chip, cc), sibling) for j, chip in enumerate(chips)]
        for j, chip in enumerate(chips):
            copy(1 + j, (*chip, cc), me).wait_recv()
            passed[j].start()
        copy(0, sibling, me).wait_recv()
        for j, chip in enumerate(chips):
            copy(4 + j, (*chip, 1 - cc), me).wait_recv()
        for cp in first + passed:
            cp.wait_send()
        mine.wait()

    return pl.pallas_call(
        body, name=name, in_specs=[_ANY], out_specs=_ANY,
        out_shape=jax.ShapeDtypeStruct((8, r, c), block.dtype),
        scratch_shapes=[pltpu.SemaphoreType.DMA((7,)), pltpu.SemaphoreType.DMA((7,)), pltpu.SemaphoreType.DMA],
    )(block)


_HBM = pl.BlockSpec(memory_space=pltpu.HBM)
_SEM = pl.BlockSpec(memory_space=pltpu.SEMAPHORE)
_DATAFLOW = pltpu.SideEffectType.DATAFLOW_SIDE_EFFECTING
N_REMOTE = 6


def _remote_peers(x, y, cc):
    return [(1 - x, y, cc), (x, 1 - y, cc), (1 - x, 1 - y, cc),
            (1 - x, y, 1 - cc), (x, 1 - y, 1 - cc), (1 - x, 1 - y, 1 - cc)]


def gather_start(block, after, name):
    r, c = block.shape

    def body(x_ref, land_ref, after_ref, send_sems, recv_sems, x_thru, land_thru, token):
        x, y, cc = lax.axis_index("x"), lax.axis_index("y"), lax.axis_index("c")
        for k, peer in enumerate(_remote_peers(x, y, cc)):
            pltpu.make_async_remote_copy(src_ref=x_ref, dst_ref=land_ref.at[4 * x + 2 * y + cc],
                                         send_sem=send_sems.at[k], recv_sem=recv_sems.at[k],
                                         device_id=peer, device_id_type=MESH).start()
        token[...] = jnp.zeros_like(token)

    land = pltpu.with_memory_space_constraint(lax.empty((8, r, c), block.dtype), pltpu.HBM)
    return pl.pallas_call(
        body, name=name,
        out_shape=(pltpu.SemaphoreType.DMA((N_REMOTE,)), pltpu.SemaphoreType.DMA((N_REMOTE,)),
                   pltpu.HBM((r, c), block.dtype), pltpu.HBM((8, r, c), block.dtype),
                   jax.ShapeDtypeStruct((8, LANES), F32)),
        in_specs=(_HBM, _HBM, _ANY), out_specs=(_SEM, _SEM, _HBM, _HBM, pl.BlockSpec(memory_space=pltpu.VMEM)),
        input_output_aliases={0: 2, 1: 3},
        compiler_params=pltpu.CompilerParams(has_side_effects=_DATAFLOW),
    )(pltpu.with_memory_space_constraint(block, pltpu.HBM), land, after)


def gather_wait(send_sems, recv_sems, block_thru, land_thru, after, name):
    def body(x_ref, land_ref, send_sems, recv_sems, after_ref, x_dead, got_ref):
        x, y, cc = lax.axis_index("x"), lax.axis_index("y"), lax.axis_index("c")
        for k, (px, py, pc) in enumerate(_remote_peers(x, y, cc)):
            cp = pltpu.make_async_remote_copy(src_ref=x_ref, dst_ref=land_ref.at[4 * px + 2 * py + pc],
                                              send_sem=send_sems.at[k], recv_sem=recv_sems.at[k],
                                              device_id=(px, py, pc), device_id_type=MESH)
            cp.wait_send()
            cp.wait_recv()

    return pl.pallas_call(
        body, name=name,
        out_shape=(pltpu.HBM(block_thru.shape, block_thru.dtype), pltpu.HBM(land_thru.shape, land_thru.dtype)),
        in_specs=(_HBM, _HBM, _SEM, _SEM, _ANY), out_specs=(_HBM, _HBM), input_output_aliases={0: 0, 1: 1},
        compiler_params=pltpu.CompilerParams(has_side_effects=_DATAFLOW),
    )(block_thru, land_thru, send_sems, recv_sems, after)[1]


def _chip_peers(x, y):
    return [(1 - x, y), (x, 1 - y), (1 - x, 1 - y)]


def exchange_start(parts, name):
    def body(p_ref, land_ref, send_sems, recv_sems, p_thru, land_thru, token):
        x, y, cc = lax.axis_index("x"), lax.axis_index("y"), lax.axis_index("c")
        for k, (px, py) in enumerate(_chip_peers(x, y)):
            pltpu.make_async_remote_copy(src_ref=p_ref.at[2 * px + py], dst_ref=land_ref.at[2 * x + y],
                                         send_sem=send_sems.at[k], recv_sem=recv_sems.at[k],
                                         device_id=(px, py, cc), device_id_type=MESH).start()
        token[...] = jnp.zeros_like(token)

    land = pltpu.with_memory_space_constraint(lax.empty(parts.shape, parts.dtype), pltpu.HBM)
    return pl.pallas_call(
        body, name=name,
        out_shape=(pltpu.SemaphoreType.DMA((3,)), pltpu.SemaphoreType.DMA((3,)),
                   pltpu.HBM(parts.shape, parts.dtype), pltpu.HBM(parts.shape, parts.dtype),
                   jax.ShapeDtypeStruct((8, LANES), F32)),
        in_specs=(_HBM, _HBM), out_specs=(_SEM, _SEM, _HBM, _HBM, pl.BlockSpec(memory_space=pltpu.VMEM)),
        input_output_aliases={0: 2, 1: 3},
        compiler_params=pltpu.CompilerParams(has_side_effects=_DATAFLOW),
    )(pltpu.with_memory_space_constraint(parts, pltpu.HBM), land)


def exchange_wait(send_sems, recv_sems, parts_thru, land_thru, after, name):
    def body(p_ref, land_ref, send_sems, recv_sems, after_ref, p_dead, got_ref):
        x, y, cc = lax.axis_index("x"), lax.axis_index("y"), lax.axis_index("c")
        for k, (px, py) in enumerate(_chip_peers(x, y)):
            cp = pltpu.make_async_remote_copy(src_ref=p_ref.at[2 * px + py], dst_ref=land_ref.at[2 * px + py],
                                              send_sem=send_sems.at[k], recv_sem=recv_sems.at[k],
                                              device_id=(px, py, cc), device_id_type=MESH)
            cp.wait_send()
            cp.wait_recv()

    return pl.pallas_call(
        body, name=name,
        out_shape=(pltpu.HBM(parts_thru.shape, parts_thru.dtype), pltpu.HBM(land_thru.shape, land_thru.dtype)),
        in_specs=(_HBM, _HBM, _SEM, _SEM, _ANY), out_specs=(_HBM, _HBM), input_output_aliases={0: 0, 1: 1},
        compiler_params=pltpu.CompilerParams(has_side_effects=_DATAFLOW),
    )(parts_thru, land_thru, send_sems, recv_sems, after)


def sibling_swap(block, name):
    def body(x_ref, out_ref, send_sem, recv_sem):
        x, y, cc = lax.axis_index("x"), lax.axis_index("y"), lax.axis_index("c")
        cp = pltpu.make_async_remote_copy(src_ref=x_ref, dst_ref=out_ref, send_sem=send_sem, recv_sem=recv_sem,
                                          device_id=(x, y, 1 - cc), device_id_type=MESH)
        cp.start()
        cp.wait()

    return pl.pallas_call(
        body, name=name, in_specs=[_ANY], out_specs=_ANY,
        out_shape=jax.ShapeDtypeStruct(block.shape, block.dtype),
        scratch_shapes=[pltpu.SemaphoreType.DMA, pltpu.SemaphoreType.DMA],
    )(block)


PACK_COLS = 1024
SHARDED = (("l0_w_in", 1), ("l0_w_uq", 1), ("l0_w_ukv", 1), ("l0_w_out", 0), ("l0_w_up", 1), ("l0_w_down", 0),
           ("l1_w_in", 0), ("l1_w_glu", 1), ("l1_w_up", 1), ("l1_w_down", 0),
           ("l0_conv_w", 1), ("l0_ffn_conv_w", 1), ("l1_ffn_conv_w", 1))
REPLICATED = ("l0_mix_norm", "l0_conv_b", "l0_conv_ln_g", "l0_conv_ln_b", "l0_q_norm", "l0_kv_norm", "l0_ffn_norm",
              "l0_ffn_conv_b", "l1_mix_norm", "l1_log_dt", "l1_a_re", "l1_a_im", "l1_b_re", "l1_b_im", "l1_c_re",
              "l1_c_im", "l1_d", "l1_b_glu", "l1_ffn_norm", "l1_ffn_conv_b", "final_norm")


def _pack(arrs, dtype, mult):
    flat = jnp.concatenate([a.reshape(-1).astype(dtype) for a in arrs])
    n = flat.shape[0]
    total = -(-n // mult) * mult
    return jnp.pad(flat, (0, total - n))


def _unpack(flat, shapes):
    out, pos = [], 0
    for shp in shapes:
        n = int(np.prod(shp))
        out.append(flat[pos:pos + n].reshape(shp))
        pos += n
    return out


PACK_ROW_ALIGN = 16


def _pack_rows(arrs, dtype, row_mult):
    parts = []
    for a in arrs:
        n = int(np.prod(a.shape))
        rows = -(-n // PACK_COLS)
        if n % PACK_COLS == 0:
            r = a.astype(dtype).reshape(rows, PACK_COLS)
        else:
            r = jnp.pad(a.reshape(-1).astype(dtype), (0, rows * PACK_COLS - n)).reshape(rows, PACK_COLS)
        parts.append(jnp.pad(r, ((0, (-rows) % PACK_ROW_ALIGN), (0, 0))))
    p = jnp.concatenate(parts)
    return jnp.pad(p, ((0, (-p.shape[0]) % row_mult), (0, 0)))


def _unpack_rows(pack, shapes):
    out, r0 = [], 0
    for shp in shapes:
        n = int(np.prod(shp))
        rows = -(-n // PACK_COLS)
        piece = lax.optimization_barrier(pack[r0:r0 + rows])
        out.append(piece.reshape(shp) if n % PACK_COLS == 0 else piece.reshape(-1)[:n].reshape(shp))
        r0 += rows + (-rows) % PACK_ROW_ALIGN
    return out


def _shard(full, axis, j):
    n = full.shape[axis] // N_CHIPS
    return lax.slice_in_dim(full, j * n, (j + 1) * n, axis=axis)


def _block_diag(t):
    q, g, a, b = t.shape
    eye = jnp.eye(g, dtype=t.dtype)
    return jnp.einsum("qgab,gh->qgahb", t, eye).reshape(q, g * a, g * b)


def _block_diag_t(d, a, b):
    q = d.shape[0]
    d5 = d.reshape(q, 8, a, 8, b)
    eye = jnp.eye(8, dtype=d.dtype)
    return jnp.einsum("qgahb,gh->qgab", d5, eye)


def kernel(x, l0_mix_norm, l0_w_in, l0_conv_w, l0_conv_b, l0_conv_ln_g, l0_conv_ln_b, l0_q_norm, l0_kv_norm, l0_w_uq, l0_w_ukv, l0_w_out, l0_ffn_norm, l0_w_up, l0_ffn_conv_w, l0_ffn_conv_b, l0_w_down, l1_mix_norm, l1_w_in, l1_log_dt, l1_a_re, l1_a_im, l1_b_re, l1_b_im, l1_c_re, l1_c_im, l1_d, l1_w_glu, l1_b_glu, l1_ffn_norm, l1_w_up, l1_ffn_conv_w, l1_ffn_conv_b, l1_w_down, final_norm, loss_target, m_l0_mix_norm, m_l0_w_in, m_l0_conv_w, m_l0_conv_b, m_l0_conv_ln_g, m_l0_conv_ln_b, m_l0_q_norm, m_l0_kv_norm, m_l0_w_uq, m_l0_w_ukv, m_l0_w_out, m_l0_ffn_norm, m_l0_w_up, m_l0_ffn_conv_w, m_l0_ffn_conv_b, m_l0_w_down, m_l1_mix_norm, m_l1_w_in, m_l1_log_dt, m_l1_a_re, m_l1_a_im, m_l1_b_re, m_l1_b_im, m_l1_c_re, m_l1_c_im, m_l1_d, m_l1_w_glu, m_l1_b_glu, m_l1_ffn_norm, m_l1_w_up, m_l1_ffn_conv_w, m_l1_ffn_conv_b, m_l1_w_down, m_final_norm, v_l0_mix_norm, v_l0_w_in, v_l0_conv_w, v_l0_conv_b, v_l0_conv_ln_g, v_l0_conv_ln_b, v_l0_q_norm, v_l0_kv_norm, v_l0_w_uq, v_l0_w_ukv, v_l0_w_out, v_l0_ffn_norm, v_l0_w_up, v_l0_ffn_conv_w, v_l0_ffn_conv_b, v_l0_w_down, v_l1_mix_norm, v_l1_w_in, v_l1_log_dt, v_l1_a_re, v_l1_a_im, v_l1_b_re, v_l1_b_im, v_l1_c_re, v_l1_c_im, v_l1_d, v_l1_w_glu, v_l1_b_glu, v_l1_ffn_norm, v_l1_w_up, v_l1_ffn_conv_w, v_l1_ffn_conv_b, v_l1_w_down, v_final_norm):
    a = dict(locals())
    w = {n: a[n] for n in [s for s, _ in SHARDED] + list(REPLICATED)}
    mom = {n: a["m_" + n] for n in w}
    var = {n: a["v_" + n] for n in w}
    return _step(a["x"][0], a["loss_target"][0], w, mom, var)


FIRST_WEIGHTS = ("l0_w_in", "l0_w_uq", "l0_w_ukv", "l0_w_out")
LATER_WEIGHTS = ("l0_w_up", "l0_w_down", "l1_w_in", "l1_w_glu", "l1_w_up", "l1_w_down")


def _assemble(got, names, w):
    got = got.reshape(N_CHIPS, -1, PACK_COLS)
    shapes = [w[n].shape for n in names]
    per_chip = [_unpack_rows(got[j], shapes) for j in range(N_CHIPS)]
    axes = dict(SHARDED)
    return {n: jnp.concatenate([per_chip[j][k] for j in range(N_CHIPS)], axis=axes[n]) for k, n in enumerate(names)}


def _gather_weights(w):
    cc = lax.axis_index("c")
    small = [n for n, _ in SHARDED[10:]]
    full = {}
    for names, dtype, mult in ((FIRST_WEIGHTS, BF16, 2 * 256), (small, F32, 2 * PACK_ROW_ALIGN)):
        pack = _pack_rows([w[n] for n in names], dtype, mult)
        half = lax.dynamic_index_in_dim(pack.reshape(2, -1, PACK_COLS), cc, axis=0, keepdims=False)
        got = all_gather8(half, "gather_" + ("first_matrices" if dtype == BF16 else "conv_weights"))
        full.update(_assemble(got, names, w))
    pack = _pack_rows([w[n] for n in LATER_WEIGHTS], BF16, 2 * 256).reshape(2, -1, PACK_COLS)
    half = lax.dynamic_index_in_dim(pack, cc, axis=0, keepdims=False)
    send_sems, recv_sems, half_thru, land_thru, token = gather_start(half, got, "gather_later_start")
    return full, (send_sems, recv_sems, half_thru, land_thru, pack), token[0, 0]


def _finish_gather(pending, after, w):
    send_sems, recv_sems, half_thru, land_thru, pack = pending
    got = gather_wait(send_sems, recv_sems, half_thru, land_thru, after, "gather_later_wait")
    chip = 2 * lax.axis_index("x") + lax.axis_index("y")
    got = lax.dynamic_update_slice(got, pack, (2 * chip, 0, 0))
    return _assemble(got, LATER_WEIGHTS, w)


def _reduce_begin(grads, names, tag):
    cc = lax.axis_index("c")
    axes = dict(SHARDED)
    packs = [_pack_rows([_shard(grads[n], axes[n], j) for n in names], BF16, 2 * 256) for j in range(N_CHIPS)]
    g = jnp.stack(packs).reshape(N_CHIPS, 2, -1, PACK_COLS)
    keep = lax.dynamic_index_in_dim(g, cc, axis=1, keepdims=False)
    give = lax.dynamic_index_in_dim(g, 1 - cc, axis=1, keepdims=False)
    got = sibling_swap(give, f"grad_swap_halves_{tag}")
    parts = add_to_bf16(keep, got, f"grad_add_sibling_{tag}")
    send_sems, recv_sems, parts_thru, land_thru, token = exchange_start(parts, f"grad_exchange_start_{tag}")
    shapes = [_shard(grads[n], axes[n], 0).shape for n in names]
    return (send_sems, recv_sems, parts_thru, land_thru, list(names), shapes), token[0, 0]


def _reduce_end(state, after, tag):
    send_sems, recv_sems, parts_thru, land_thru, names, shapes = state
    cc = lax.axis_index("c")
    chip = 2 * lax.axis_index("x") + lax.axis_index("y")
    parts, landed = exchange_wait(send_sems, recv_sems, parts_thru, land_thru, after, f"grad_exchange_wait_{tag}")
    own = lax.dynamic_index_in_dim(parts, chip, axis=0, keepdims=True)
    landed = lax.dynamic_update_slice(landed, own, (chip, 0, 0))
    mine = sum_leading(landed, f"grad_sum_chips_{tag}")
    theirs = sibling_swap(mine, f"grad_swap_sums_{tag}")
    lo = jnp.where(cc == 0, mine, theirs)
    hi = jnp.where(cc == 0, theirs, mine)
    return dict(zip(names, _unpack_rows(jnp.concatenate([lo, hi]), shapes)))


def _reduce_replicated(grads, loss_row):
    names = list(REPLICATED)
    flat = _pack([grads[n] for n in names] + [loss_row], F32, 256 * LANES).reshape(-1, LANES)
    got = all_gather8(flat, "gather_small_grads")
    tot = sum_leading(got, "sum_small_grads").reshape(-1)
    parts = _unpack(tot, [grads[n].shape for n in names] + [loss_row.shape])
    return dict(zip(names, parts[:-1])), parts[-1][0]


def _row(v):
    return v.reshape(1, -1).astype(F32)


def _pad_rows(wt, rows):
    return jnp.pad(wt.astype(F32), ((0, rows - wt.shape[0]), (0, 0)))


def _ffn_fwd(xin, g, wa, wb, cw, cb, wd, tag):
    cwa, cwb = _pad_rows(cw[:, :D_FF], 8), _pad_rows(cw[:, D_FF:], 8)
    xout, xn, hpa, hpb, act = ffn_fwd(xin, _row(g), wa, wb, cwa, cwb, _row(cb[:D_FF]), _row(cb[D_FF:]), wd, tag)
    return xout, (xin, xn, hpa, hpb, act)


def _ffn_bwd(dxout, saved, g, wa, wb, cw, cb, wd, tag, zero=0.0):
    xin, xn, hpa, hpb, act = saved
    d_wd = matmul(act, dxout, ta=True, name=f"{tag}_d_wdown")
    cwa, cwb = _pad_rows(cw[:, :D_FF], 8), _pad_rows(cw[:, D_FF:], 8)
    dxn, dpa, dpb, dwa, dwb, dba, dbb = ffn_bwd(dxout, hpa, hpb, wa, wb, cwa, cwb,
                                                _row(cb[:D_FF]) + zero, _row(cb[D_FF:]), wd, tag + "_bwd")
    d_wu = jnp.concatenate([matmul(xn, dpa, ta=True, name=f"{tag}_d_wup_a"),
                            matmul(xn, dpb, ta=True, name=f"{tag}_d_wup_b")], axis=1)
    dxin, dg = rms_bwd(xin, _row(g), dxn, dxout, f"{tag}_rms_bwd")
    taps = lambda t: t.transpose(1, 0, 2).reshape(8, -1)
    d_cw = jnp.concatenate([taps(dwa)[:FFN_K], taps(dwb)[:FFN_K]], axis=1)
    d_cb = jnp.concatenate([taps(dba)[0], taps(dbb)[0]])
    return dxin, dg[0], d_wu, d_cw, d_cb, d_wd


def _step(x, target, w, mom, var):
    s = x.shape[0]
    full, pending, zero = _gather_weights(w)
    cos, sin = rope_tables(s)

    w_in0 = full["l0_w_in"]
    w_in0p = jnp.concatenate([w_in0, jnp.zeros((D_MODEL, H0_W - w_in0.shape[1]), BF16)], axis=1)
    wq = full["l0_w_uq"].reshape(Q_LORA, N_HEADS, QK_NOPE + QK_ROPE)
    zq = lambda n: jnp.zeros((Q_LORA, N_HEADS, n), BF16)
    w_uqp = jnp.concatenate([wq[..., :QK_NOPE], zq(LANES - QK_NOPE), wq[..., QK_NOPE:], zq(LANES - QK_ROPE)],
                            axis=-1).reshape(Q_LORA, N_HEADS * HEAD_PAD)
    w_ukv = full["l0_w_ukv"]
    w_out = full["l0_w_out"]
    w_out_u = w_out[:CONV_WIDTH]
    wo = w_out[CONV_WIDTH:].reshape(N_HEADS, V_DIM, D_MODEL)
    w_out_a = jnp.concatenate([jnp.zeros_like(wo), wo], axis=1).reshape(N_HEADS * LANES, D_MODEL)
    conv_w = _pad_rows(full["l0_conv_w"], CONV_HALO)

    xn0 = rms_fwd(x, _row(w["l0_mix_norm"]) + zero, "l0_mix_rms")
    h0 = matmul(xn0, w_in0p, name="l0_in_proj")
    qn_g, kvn_g = _row(w["l0_q_norm"]), _row(w["l0_kv_norm"])
    u0, cq, ckv, kr = mixpre_fwd(h0, qn_g, kvn_g, cos, sin)
    cb, lg, lb = _row(w["l0_conv_b"]), _row(w["l0_conv_ln_g"]), _row(w["l0_conv_ln_b"])
    u = convln_fwd(u0, conv_w, cb, lg, lb)
    qraw = matmul(cq, w_uqp, name="l0_q_up")
    q = qrope_fwd(qraw, cos, sin)
    kv = matmul(ckv, w_ukv, out_dtype=BF16, name="l0_kv_up")
    o, lse = attn_fwd(q, kv, kr)
    x1 = matmul(u, w_out_u, res=x, name="l0_out_conv")
    x1 = matmul(o, w_out_a, res=x1, name="l0_out_attn")
    full.update(_finish_gather(pending, x1, w))
    w_up0a, w_up0b = full["l0_w_up"][:, :D_FF], full["l0_w_up"][:, D_FF:]
    w_up1a, w_up1b = full["l1_w_up"][:, :D_FF], full["l1_w_up"][:, D_FF:]

    x2, ffn0 = _ffn_fwd(x1, w["l0_ffn_norm"], w_up0a, w_up0b, full["l0_ffn_conv_w"], w["l0_ffn_conv_b"],
                        full["l0_w_down"], "l0_ffn")

    g_, p_, c_ = SSM_GROUPS, SSM_STATE, SSM_GROUP
    s5_in = (w["l1_log_dt"].reshape(g_, 1), w["l1_a_re"], w["l1_a_im"],
             w["l1_b_re"].reshape(g_, p_ * c_), w["l1_b_im"].reshape(g_, p_ * c_))
    lam_r, lam_i, bb_r, bb_i = s5_params_fwd(*s5_in)
    lam_rf, lam_if = lam_r.reshape(1, NS), lam_i.reshape(1, NS)

    def b_blocks(bb):
        t = bb.reshape(NQ, 8, p_, c_).transpose(0, 1, 3, 2)
        return _block_diag(t).astype(BF16)

    def c_blocks(cm):
        t = cm.reshape(NQ, 8, c_, p_).transpose(0, 1, 3, 2)
        return _block_diag(t).astype(BF16)

    bre, bim = b_blocks(bb_r), b_blocks(bb_i)
    cre, cim = c_blocks(w["l1_c_re"]), c_blocks(w["l1_c_im"])
    dskip = _row(w["l1_d"])
    xn2 = rms_fwd(x2, _row(w["l1_mix_norm"]), "l1_mix_rms")
    u1 = matmul(xn2, full["l1_w_in"], name="l1_in_proj")
    xs_r, xs_i, y1, yg = s5_scan_fwd(u1, lam_rf, lam_if, bre, bim, cre, cim, dskip)
    z = matmul(yg, full["l1_w_glu"], bias=_row(w["l1_b_glu"]), out_dtype=BF16, name="l1_glu_proj")
    x3 = glu_res_fwd(z, x2)

    x4, ffn1 = _ffn_fwd(x3, w["l1_ffn_norm"], w_up1a, w_up1b, full["l1_ffn_conv_w"], w["l1_ffn_conv_b"],
                        full["l1_w_down"], "l1_ffn")
    loss_part, dx4, dgf = loss_head(x4, _row(w["final_norm"]), target)

    gr = {"final_norm": dgf[0]}

    dx3, gr["l1_ffn_norm"], gr["l1_w_up"], gr["l1_ffn_conv_w"], gr["l1_ffn_conv_b"], gr["l1_w_down"] = _ffn_bwd(
        dx4, ffn1, w["l1_ffn_norm"], w_up1a, w_up1b, full["l1_ffn_conv_w"], w["l1_ffn_conv_b"], full["l1_w_down"],
        "l1_ffn")

    dz, dbga, dbgb = glu_bwd(z, dx3)
    gr["l1_b_glu"] = jnp.concatenate([dbga[0], dbgb[0]])
    dyg = matmul(dz, full["l1_w_glu"], tb=True, name="l1_d_yg")
    gr["l1_w_glu"] = matmul(yg, dz, ta=True, name="l1_d_wglu")
    a_r, a_i, dy1 = s5_scan_bwd(dyg, y1, lam_rf, lam_if, cre, cim)
    du1, dlr, dli, dbr, dbi, dcr, dci, dd = s5_grads(u1, dy1, xs_r, xs_i, a_r, a_i, bre, bim, dskip)
    gr["l1_d"] = dd[0]

    def b_unblock(d):
        return _block_diag_t(d, c_, p_).transpose(0, 1, 3, 2).reshape(g_, p_ * c_)

    def c_unblock(d):
        return _block_diag_t(d, p_, c_).transpose(0, 1, 3, 2).reshape(g_, c_, p_)

    gr["l1_c_re"], gr["l1_c_im"] = c_unblock(dcr), c_unblock(dci)
    dld, dar, dai, dbre, dbim = s5_params_bwd(*s5_in, dlr[0].reshape(g_, p_), dli[0].reshape(g_, p_),
                                              b_unblock(dbr), b_unblock(dbi))
    gr["l1_log_dt"], gr["l1_a_re"], gr["l1_a_im"] = dld.reshape(g_), dar, dai
    gr["l1_b_re"], gr["l1_b_im"] = dbre.reshape(g_, p_, c_), dbim.reshape(g_, p_, c_)
    dxn2 = matmul(du1, full["l1_w_in"], tb=True, name="l1_d_xn")
    gr["l1_w_in"] = matmul(xn2, du1, ta=True, name="l1_d_win")
    dx2, dg = rms_bwd(x2, _row(w["l1_mix_norm"]), dxn2, dx3, "l1_mix_rms_bwd")
    gr["l1_mix_norm"] = dg[0]
    red_a, zero_a = _reduce_begin(gr, ("l1_w_up", "l1_w_down", "l1_w_glu", "l1_w_in"), "a")

    dx1, gr["l0_ffn_norm"], gr["l0_w_up"], gr["l0_ffn_conv_w"], gr["l0_ffn_conv_b"], gr["l0_w_down"] = _ffn_bwd(
        dx2, ffn0, w["l0_ffn_norm"], w_up0a, w_up0b, full["l0_ffn_conv_w"], w["l0_ffn_conv_b"], full["l0_w_down"],
        "l0_ffn", zero_a)
    red_b, zero_b = _reduce_begin(gr, ("l0_w_up", "l0_w_down"), "b")

    du = matmul(dx1, w_out_u + zero_b.astype(BF16), tb=True, out_dtype=BF16, name="l0_d_u")
    do = matmul(dx1, w_out_a, tb=True, out_dtype=BF16, name="l0_d_o")
    d_wout_u = matmul(u, dx1, ta=True, name="l0_d_wout_u")
    d_wout_a = matmul(o, dx1, ta=True, name="l0_d_wout_a")
    gr["l0_w_out"] = jnp.concatenate(
        [d_wout_u, d_wout_a.reshape(N_HEADS, LANES, D_MODEL)[:, LANES - V_DIM:].reshape(N_HEADS * V_DIM, D_MODEL)])
    dq, dkv, dkr = attn_bwd(q, kv, kr, o, do, lse)
    dqraw = qrope_bwd(dq, cos, sin)
    dcq = matmul(dqraw, w_uqp, tb=True, name="l0_d_cq")
    d_wuqp = matmul(cq, dqraw, ta=True, name="l0_d_wuq").reshape(Q_LORA, N_HEADS, HEAD_PAD)
    gr["l0_w_uq"] = jnp.concatenate([d_wuqp[..., :QK_NOPE], d_wuqp[..., LANES:LANES + QK_ROPE]],
                                    axis=-1).reshape(Q_LORA, -1)
    dckv = matmul(dkv, w_ukv, tb=True, name="l0_d_ckv")
    gr["l0_w_ukv"] = matmul(ckv, dkv, ta=True, name="l0_d_wukv")
    du1c, dlg, dlb, dcb = convln_bwd1(u0, conv_w, cb, lg, lb, du)
    gr["l0_conv_ln_g"], gr["l0_conv_ln_b"], gr["l0_conv_b"] = dlg[0], dlb[0], dcb[0]
    du0, dcw = convln_bwd2(u0, conv_w, du1c)
    gr["l0_conv_w"] = dcw[:CONV_K]
    dh0, dqn, dkvn = mixpre_bwd(h0, qn_g, kvn_g, cos, sin, du0, dcq, dckv, dkr)
    gr["l0_q_norm"], gr["l0_kv_norm"] = dqn[0], dkvn[0]
    dxn0 = matmul(dh0, w_in0p, tb=True, name="l0_d_xn")
    gr["l0_w_in"] = matmul(xn0, dh0, ta=True, name="l0_d_win")[:, :w_in0.shape[1]]
    grad_x, dg = rms_bwd(x, _row(w["l0_mix_norm"]), dxn0, dx1, "l0_mix_rms_bwd")
    gr["l0_mix_norm"] = dg[0]

    rest = [n for n, _ in SHARDED if n not in red_a[-2] + red_b[-2]]
    red_c, _ = _reduce_begin(gr, rest, "c")
    g_sh = {**_reduce_end(red_a, grad_x, "a"), **_reduce_end(red_b, grad_x, "b"), **_reduce_end(red_c, grad_x, "c")}
    g_rep, loss = _reduce_replicated(gr, loss_part[0])
    grad, delta, new_m, new_v = {}, {}, {}, {}
    for n, _ in SHARDED:
        shp = w[n].shape
        two_d = (lambda t: t.reshape(shp[0], -1))
        grad[n] = g_sh[n]
        delta[n], new_m[n], new_v[n] = adamw(two_d(w[n]), two_d(g_sh[n]), two_d(mom[n]), two_d(var[n]), f"adamw_{n}")
    names = list(REPLICATED)
    pk = lambda d: _pack([d[n] for n in names], F32, 256 * LANES).reshape(-1, LANES)
    dl, nm, nv = adamw(pk(w), pk(g_rep), pk(mom), pk(var), "adamw_small")
    shapes = [w[n].shape for n in names]
    for n, d_, m_, v_ in zip(names, _unpack(dl.reshape(-1), shapes), _unpack(nm.reshape(-1), shapes),
                             _unpack(nv.reshape(-1), shapes)):
        grad[n], delta[n], new_m[n], new_v[n] = g_rep[n], d_, m_, v_

    order = ["l0_mix_norm", "l0_w_in", "l0_conv_w", "l0_conv_b", "l0_conv_ln_g", "l0_conv_ln_b", "l0_q_norm",
             "l0_kv_norm", "l0_w_uq", "l0_w_ukv", "l0_w_out", "l0_ffn_norm", "l0_w_up", "l0_ffn_conv_w",
             "l0_ffn_conv_b", "l0_w_down", "l1_mix_norm", "l1_w_in", "l1_log_dt", "l1_a_re", "l1_a_im", "l1_b_re",
             "l1_b_im", "l1_c_re", "l1_c_im", "l1_d", "l1_w_glu", "l1_b_glu", "l1_ffn_norm", "l1_w_up",
             "l1_ffn_conv_w", "l1_ffn_conv_b", "l1_w_down", "final_norm"]
    return (loss, grad_x[None], *[grad[n] for n in order], *[delta[n] for n in order],
            *[new_m[n] for n in order], *[new_v[n] for n in order])
```

```python
import functools
import math

import jax
import jax.numpy as jnp
import numpy as np
from jax import lax
from jax.experimental import pallas as pl
from jax.experimental.pallas import tpu as pltpu

F32 = jnp.float32
BF16 = jnp.bfloat16
MESH = pl.DeviceIdType.MESH

D_MODEL = 1024
EPS = 1e-6
LN_EPS = 1e-5
CONV_WIDTH = 512
CONV_K = 31
N_HEADS = 8
QK_NOPE = 64
QK_ROPE = 32
V_DIM = 64
Q_LORA = 256
KV_LORA = 128
ROPE_BASE = 10000.0
ATT_SCALE = (QK_NOPE + QK_ROPE) ** -0.5
SSM_WIDTH = 512
SSM_GROUP = 16
SSM_GROUPS = 32
SSM_STATE = 64
D_FF = 2816
FFN_K = 3
ADAM_LR = 0.001
ADAM_B1 = 0.9
ADAM_B2 = 0.999
ADAM_EPS = 1e-08
ADAM_WD = 0.01
ADAM_STEP = 10

N_CHIPS = 4
LANES = 128
HEAD_PAD = 256
CONV_HALO = 32
FFN_HALO = 16
VMEM_LIMIT = 56 * 1024 * 1024

ROW_TILE = 512
FFN_ROW_TILE = 1024
FFN_COL_TILE = 256
FFN_ROW_CHUNK = 64
CONV_ROW_CHUNK = 32
FFN_BWD_PARTS = 4
ATT_TILE = 1024
SCAN_TILE = 256
SCAN_UNROLL = 4


def _cparams(*sem):
    return pltpu.CompilerParams(dimension_semantics=tuple(sem), vmem_limit_bytes=VMEM_LIMIT)


def _pick(n, cands):
    for c in cands:
        if n % c == 0:
            return c
    return n


def matmul(a, b, *, ta=False, tb=False, res=None, bias=None, out_dtype=None, name):
    if out_dtype is None:
        out_dtype = BF16 if ta else F32
    if ta:
        kdim, m = a.shape
    else:
        m, kdim = a.shape
    if tb:
        n, k2 = b.shape
    else:
        k2, n = b.shape
    assert kdim == k2, (a.shape, b.shape, ta, tb)
    tn = _pick(n, (1408, 1024, 768, 512, 384, 256, 128))
    if ta:
        tm = _pick(m, (1408, 1024, 512, 256, 128))
        tk = _pick(kdim, (1024, 512, 256, 128))
    else:
        tm = _pick(m, (1024, 512, 256, 128))
        tk = kdim
        if kdim > 1024:
            tn = _pick(n, (512, 256, 128))
        if tm * tn > 1024 * 1024 and out_dtype == F32:
            tm = _pick(m, (512, 256, 128))
    nk = kdim // tk
    has_res, has_bias = res is not None, bias is not None
    dims = (((0,) if ta else (1,), (1,) if tb else (0,)), ((), ()))

    def body(*refs):
        a_ref, b_ref = refs[0], refs[1]
        pos = 2
        res_ref = bias_ref = None
        if has_res:
            res_ref = refs[pos]
            pos += 1
        if has_bias:
            bias_ref = refs[pos]
            pos += 1
        o_ref = refs[pos]

        def finish(r):
            if has_bias:
                r = r + bias_ref[...]
            if has_res:
                r = r + res_ref[...].astype(F32)
            o_ref[...] = r.astype(o_ref.dtype)

        prod = lax.dot_general(a_ref[...].astype(BF16), b_ref[...].astype(BF16), dims, preferred_element_type=F32)
        if nk == 1:
            finish(prod)
            return
        acc_ref = refs[pos + 1]
        k = pl.program_id(2)

        @pl.when(k == 0)
        def _():
            acc_ref[...] = prod

        @pl.when(k > 0)
        def _():
            acc_ref[...] += prod

        @pl.when(k == nk - 1)
        def _():
            finish(acc_ref[...])

    a_spec = pl.BlockSpec((tk, tm), lambda i, j, k: (k, i)) if ta else pl.BlockSpec((tm, tk), lambda i, j, k: (i, k))
    b_spec = pl.BlockSpec((tn, tk), lambda i, j, k: (j, k)) if tb else pl.BlockSpec((tk, tn), lambda i, j, k: (k, j))
    in_specs = [a_spec, b_spec]
    args = [a, b]
    if has_res:
        in_specs.append(pl.BlockSpec((tm, tn), lambda i, j, k: (i, j)))
        args.append(res)
    if has_bias:
        in_specs.append(pl.BlockSpec((1, tn), lambda i, j, k: (0, j)))
        args.append(bias)
    return pl.pallas_call(
        body, name=name, grid=(m // tm, n // tn, nk),
        in_specs=in_specs, out_specs=pl.BlockSpec((tm, tn), lambda i, j, k: (i, j)),
        out_shape=jax.ShapeDtypeStruct((m, n), out_dtype),
        scratch_shapes=[pltpu.VMEM((tm, tn), F32)] if nk > 1 else [],
        compiler_params=_cparams("parallel", "parallel", "arbitrary"),
    )(*args)


def rowcall(body, *, rows, ts, ins, outs, name, scratch=()):
    nt = rows // ts
    in_specs, args = [], []
    for arr, kind in ins:
        if kind == "row":
            in_specs.append(pl.BlockSpec((ts, arr.shape[1]), lambda i: (i, 0)))
        elif kind == "rev":
            in_specs.append(pl.BlockSpec((ts, arr.shape[1]), lambda i: (nt - 1 - i, 0)))
        elif kind == "full":
            nd = arr.ndim
            in_specs.append(pl.BlockSpec(arr.shape, lambda i, nd=nd: (0,) * nd))
        elif kind.startswith("prev:"):
            h = int(kind[5:])
            r = ts // h
            in_specs.append(pl.BlockSpec((h, arr.shape[1]), lambda i, r=r: (jnp.maximum(i * r - 1, 0), 0)))
        elif kind.startswith("next:"):
            h = int(kind[5:])
            r = ts // h
            last = rows // h - 1
            in_specs.append(pl.BlockSpec((h, arr.shape[1]), lambda i, r=r, last=last: (jnp.minimum((i + 1) * r, last), 0)))
        elif kind.startswith("revprev:"):
            h = int(kind[8:])
            r = ts // h
            in_specs.append(pl.BlockSpec((h, arr.shape[1]), lambda i, r=r: (jnp.maximum((nt - 1 - i) * r - 1, 0), 0)))
        else:
            raise ValueError(kind)
        args.append(arr)
    out_specs, out_shapes = [], []
    for shape, dtype, kind in outs:
        if kind == "row":
            out_specs.append(pl.BlockSpec((ts, shape[1]), lambda i: (i, 0)))
        elif kind == "rev":
            out_specs.append(pl.BlockSpec((ts, shape[1]), lambda i: (nt - 1 - i, 0)))
        else:
            nd = len(shape)
            out_specs.append(pl.BlockSpec(tuple(shape), lambda i, nd=nd: (0,) * nd))
        out_shapes.append(jax.ShapeDtypeStruct(tuple(shape), dtype))
    return pl.pallas_call(
        functools.partial(body, nt), name=name, grid=(nt,),
        in_specs=in_specs, out_specs=tuple(out_specs), out_shape=tuple(out_shapes),
        scratch_shapes=list(scratch),
        compiler_params=_cparams("arbitrary"),
    )(*args)


def _rms(x, g):
    return x * lax.rsqrt(jnp.mean(x * x, axis=-1, keepdims=True) + EPS) * g


def _layer_norm(x, g, b):
    mu = jnp.mean(x, axis=-1, keepdims=True)
    xc = x - mu
    var = jnp.mean(xc * xc, axis=-1, keepdims=True)
    return xc * lax.rsqrt(var + LN_EPS) * g + b


def _sigmoid(x):
    return 1.0 / (1.0 + jnp.exp(-x))


def _silu(x):
    return x * _sigmoid(x)


def _gelu(x):
    return 0.5 * x * (1.0 + jnp.tanh(math.sqrt(2.0 / math.pi) * (x + 0.044715 * (x * x * x))))


def _acc(ref, i, val):
    s = jnp.sum(val, axis=0, keepdims=True)

    @pl.when(i == 0)
    def _():
        ref[...] = jnp.zeros_like(ref)

    ref[...] += jnp.broadcast_to(s, ref.shape)


def rms_fwd(x, g, name):
    s, c = x.shape

    def body(nt, x_ref, g_ref, o_ref):
        o_ref[...] = _rms(x_ref[...], g_ref[...]).astype(BF16)

    return rowcall(body, rows=s, ts=min(ROW_TILE, s), ins=[(x, "row"), (g, "full")],
                   outs=[((s, c), BF16, "row")], name=name)[0]


def rms_bwd(x, g, dxn, dres, name):
    s, c = x.shape

    def body(nt, x_ref, g_ref, d_ref, r_ref, dx_ref, dg_ref):
        i = pl.program_id(0)
        _, vjp = jax.vjp(_rms, x_ref[...], g_ref[...])
        dx, dg = vjp(d_ref[...].astype(F32))
        dx_ref[...] = dx + r_ref[...]
        _acc(dg_ref, i, dg)

    return rowcall(body, rows=s, ts=min(ROW_TILE, s),
                   ins=[(x, "row"), (g, "full"), (dxn, "row"), (dres, "row")],
                   outs=[((s, c), F32, "row"), ((8, c), F32, "acc")], name=name)


def _partner(t):
    lane = lax.broadcasted_iota(jnp.int32, t.shape, 1)
    half = QK_ROPE // 2
    return jnp.where(lane % QK_ROPE < half, pltpu.roll(t, LANES - half, 1), pltpu.roll(t, half, 1))


def _rope(t, cos, sin):
    return t * cos + _partner(t) * sin


def _rope_t(d, cos, sin):
    return d * cos + _partner(d * sin)


def rope_tables(s):
    half = QK_ROPE // 2
    inv = ROPE_BASE ** (-jnp.arange(half, dtype=F32) / half)
    ang = jnp.arange(s).astype(F32)[:, None] * inv[None, :]
    cos, sin = jnp.cos(ang), jnp.sin(ang)
    z = jnp.zeros((s, LANES - QK_ROPE), F32)
    return jnp.concatenate([cos, cos, z], axis=1), jnp.concatenate([-sin, sin, z], axis=1)


H0_A, H0_G, H0_Q, H0_KV, H0_KR, H0_W = 0, 512, 1024, 1280, 1408, 1536


def _mixpre_fn(a, g, q, kv, qn, kvn):
    return a * _sigmoid(g), _rms(q, qn), _rms(kv, kvn)


def _h0_parts(h_ref):
    return (h_ref[:, H0_A:H0_G], h_ref[:, H0_G:H0_Q], h_ref[:, H0_Q:H0_KV], h_ref[:, H0_KV:H0_KR])


def mixpre_fwd(h0, qn, kvn, cos, sin):
    s = h0.shape[0]

    def body(nt, h_ref, qn_ref, kvn_ref, cos_ref, sin_ref, u0_ref, cq_ref, ckv_ref, kr_ref):
        u0, cq, ckv = _mixpre_fn(*_h0_parts(h_ref), qn_ref[...], kvn_ref[...])
        u0_ref[...] = u0
        cq_ref[...] = cq.astype(BF16)
        ckv_ref[...] = ckv.astype(BF16)
        kr_ref[...] = _rope(h_ref[:, H0_KR:H0_W], cos_ref[...], sin_ref[...]).astype(BF16)

    return rowcall(body, rows=s, ts=min(ROW_TILE, s),
                   ins=[(h0, "row"), (qn, "full"), (kvn, "full"), (cos, "row"), (sin, "row")],
                   outs=[((s, CONV_WIDTH), F32, "row"), ((s, Q_LORA), BF16, "row"),
                         ((s, KV_LORA), BF16, "row"), ((s, LANES), BF16, "row")], name="mixpre_fwd")


def mixpre_bwd(h0, qn, kvn, cos, sin, du0, dcq, dckv, dkr):
    s = h0.shape[0]

    def body(nt, h_ref, qn_ref, kvn_ref, cos_ref, sin_ref, du0_ref, dcq_ref, dckv_ref, dkr_ref,
             dh_ref, dqn_ref, dkvn_ref):
        i = pl.program_id(0)
        _, vjp = jax.vjp(_mixpre_fn, *_h0_parts(h_ref), qn_ref[...], kvn_ref[...])
        da, dg, dq, dkv, dqn, dkvn = vjp((du0_ref[...], dcq_ref[...], dckv_ref[...]))
        dh_ref[:, H0_A:H0_G] = da.astype(BF16)
        dh_ref[:, H0_G:H0_Q] = dg.astype(BF16)
        dh_ref[:, H0_Q:H0_KV] = dq.astype(BF16)
        dh_ref[:, H0_KV:H0_KR] = dkv.astype(BF16)
        dkr = dkr_ref[:, :LANES]
        for h in range(1, N_HEADS):
            dkr = dkr + dkr_ref[:, h * LANES:(h + 1) * LANES]
        dh_ref[:, H0_KR:H0_W] = _rope_t(dkr, cos_ref[...], sin_ref[...]).astype(BF16)
        _acc(dqn_ref, i, dqn)
        _acc(dkvn_ref, i, dkvn)

    return rowcall(body, rows=s, ts=min(ROW_TILE, s),
                   ins=[(h0, "row"), (qn, "full"), (kvn, "full"), (cos, "row"), (sin, "row"),
                        (du0, "row"), (dcq, "row"), (dckv, "row"), (dkr, "row")],
                   outs=[((s, H0_W), BF16, "row"), ((8, Q_LORA), F32, "acc"), ((8, KV_LORA), F32, "acc")],
                   name="mixpre_bwd")


def _conv_taps(ext_ref, w_ref, ts, first, ntaps, flip=False):
    acc = None
    for k in range(ntaps):
        term = w_ref[pl.ds(ntaps - 1 - k if flip else k, 1), :] * ext_ref[pl.ds(first + k, ts), :]
        acc = term if acc is None else acc + term
    return acc


def _ln_silu(u1, g, b):
    return _silu(_layer_norm(u1, g, b))


SUBLANES = 8


def _fill_shifted(sh_ref, parts, rows):
    pos = 0
    for p in parts:
        sh_ref[0, pl.ds(pos, p.shape[0]), :] = p
        pos += p.shape[0]
    sh_ref[0, pl.ds(rows, SUBLANES), :] = jnp.zeros((SUBLANES, sh_ref.shape[2]), F32)
    for r in range(1, SUBLANES):
        sh_ref[r, pl.ds(0, rows), :] = sh_ref[0, pl.ds(r, rows), :]


def _window(sh_ref, off, n):
    r = off % SUBLANES
    return sh_ref[r, pl.ds(off - r, n), :]


def _taps_aligned(sh_ref, w_ref, n, first, ntaps, flip=False):
    acc = None
    for k in range(ntaps):
        term = w_ref[pl.ds(ntaps - 1 - k if flip else k, 1), :] * _window(sh_ref, first + k, n)
        acc = term if acc is None else acc + term
    return acc


def _conv_scratch(ts, c):
    return pltpu.VMEM((SUBLANES, ts + CONV_HALO + SUBLANES, c), F32)


def convln_fwd(u0, w, b, lg, lb):
    s, c = u0.shape
    ts = min(ROW_TILE, s)
    rc = min(CONV_ROW_CHUNK, ts)
    first = CONV_HALO - (CONV_K - 1)

    def body(nt, cur_ref, prev_ref, w_ref, b_ref, lg_ref, lb_ref, o_ref, sh_ref):
        i = pl.program_id(0)
        _fill_shifted(sh_ref, [jnp.where(i > 0, prev_ref[...], 0.0), cur_ref[...]], ts + CONV_HALO)
        for r0 in range(0, ts, rc):
            u1 = _taps_aligned(sh_ref, w_ref, rc, first + r0, CONV_K) + b_ref[...]
            o_ref[pl.ds(r0, rc), :] = _ln_silu(u1, lg_ref[...], lb_ref[...]).astype(BF16)

    return rowcall(body, rows=s, ts=ts,
                   ins=[(u0, "row"), (u0, f"prev:{CONV_HALO}"), (w, "full"), (b, "full"), (lg, "full"), (lb, "full")],
                   outs=[((s, c), BF16, "row")], name="convln_fwd", scratch=[_conv_scratch(ts, c)])[0]


def convln_bwd1(u0, w, b, lg, lb, du):
    s, c = u0.shape
    ts = min(ROW_TILE, s)
    rc = min(CONV_ROW_CHUNK, ts)
    first = CONV_HALO - (CONV_K - 1)

    def body(nt, cur_ref, prev_ref, w_ref, b_ref, lg_ref, lb_ref, du_ref, du1_ref, dlg_ref, dlb_ref, dcb_ref, sh_ref):
        i = pl.program_id(0)
        _fill_shifted(sh_ref, [jnp.where(i > 0, prev_ref[...], 0.0), cur_ref[...]], ts + CONV_HALO)
        sums = [jnp.zeros((1, c), F32)] * 3
        for r0 in range(0, ts, rc):
            u1 = _taps_aligned(sh_ref, w_ref, rc, first + r0, CONV_K) + b_ref[...]
            _, vjp = jax.vjp(_ln_silu, u1, lg_ref[...], lb_ref[...])
            du1, dlg, dlb = vjp(du_ref[pl.ds(r0, rc), :].astype(F32))
            du1_ref[pl.ds(r0, rc), :] = du1
            parts = (dlg, dlb, jnp.sum(du1, axis=0, keepdims=True))
            sums = [a + jnp.sum(p, axis=0, keepdims=True) for a, p in zip(sums, parts)]
        _acc(dlg_ref, i, sums[0])
        _acc(dlb_ref, i, sums[1])
        _acc(dcb_ref, i, sums[2])

    return rowcall(body, rows=s, ts=ts,
                   ins=[(u0, "row"), (u0, f"prev:{CONV_HALO}"), (w, "full"), (b, "full"), (lg, "full"), (lb, "full"),
                        (du, "row")],
                   outs=[((s, c), F32, "row"), ((8, c), F32, "acc"), ((8, c), F32, "acc"), ((8, c), F32, "acc")],
                   name="convln_bwd1", scratch=[_conv_scratch(ts, c)])


def convln_bwd2(u0, w, du1):
    s, c = u0.shape
    ts = min(ROW_TILE, s)
    rc = min(CONV_ROW_CHUNK, ts)
    first = CONV_HALO - (CONV_K - 1)

    def body(nt, cur_ref, prev_ref, d_ref, dnext_ref, w_ref, du0_ref, dw_ref, sh_ref, dsh_ref):
        i = pl.program_id(0)
        _fill_shifted(sh_ref, [jnp.where(i > 0, prev_ref[...], 0.0), cur_ref[...]], ts + CONV_HALO)
        _fill_shifted(dsh_ref, [d_ref[...], jnp.where(i < nt - 1, dnext_ref[...], 0.0)], ts + CONV_HALO)
        for r0 in range(0, ts, rc):
            du0_ref[pl.ds(r0, rc), :] = _taps_aligned(dsh_ref, w_ref, rc, r0, CONV_K, flip=True)

        @pl.when(i == 0)
        def _():
            dw_ref[...] = jnp.zeros_like(dw_ref)

        for k in range(CONV_K):
            part = jnp.zeros((SUBLANES, c), F32)
            for r0 in range(0, ts, rc):
                prod = d_ref[pl.ds(r0, rc), :] * _window(sh_ref, first + k + r0, rc)
                for a in range(0, rc, SUBLANES):
                    part = part + prod[a:a + SUBLANES]
            dw_ref[pl.ds(k, 1), :] += jnp.sum(part, axis=0, keepdims=True)

    return rowcall(body, rows=s, ts=ts,
                   ins=[(u0, "row"), (u0, f"prev:{CONV_HALO}"), (du1, "row"), (du1, f"next:{CONV_HALO}"), (w, "full")],
                   outs=[((s, c), F32, "row"), ((CONV_HALO, c), F32, "acc")], name="convln_bwd2",
                   scratch=[_conv_scratch(ts, c), _conv_scratch(ts, c)])


def qrope_fwd(qraw, cos, sin):
    s = qraw.shape[0]

    def body(nt, q_ref, cos_ref, sin_ref, o_ref):
        cos_v, sin_v = cos_ref[...] * ATT_SCALE, sin_ref[...] * ATT_SCALE
        for h in range(N_HEADS):
            nope = q_ref[:, h * HEAD_PAD:h * HEAD_PAD + LANES] * ATT_SCALE
            o_ref[:, h * HEAD_PAD:h * HEAD_PAD + LANES] = nope.astype(BF16)
            r = q_ref[:, h * HEAD_PAD + LANES:(h + 1) * HEAD_PAD]
            o_ref[:, h * HEAD_PAD + LANES:(h + 1) * HEAD_PAD] = _rope(r, cos_v, sin_v).astype(BF16)

    return rowcall(body, rows=s, ts=min(ROW_TILE, s), ins=[(qraw, "row"), (cos, "row"), (sin, "row")],
                   outs=[((s, N_HEADS * HEAD_PAD), BF16, "row")], name="qrope_fwd")[0]


def qrope_bwd(dq, cos, sin):
    s = dq.shape[0]

    def body(nt, d_ref, cos_ref, sin_ref, o_ref):
        cos_v, sin_v = cos_ref[...] * ATT_SCALE, sin_ref[...] * ATT_SCALE
        for h in range(N_HEADS):
            nope = d_ref[:, h * HEAD_PAD:h * HEAD_PAD + LANES] * ATT_SCALE
            o_ref[:, h * HEAD_PAD:h * HEAD_PAD + LANES] = nope.astype(BF16)
            r = d_ref[:, h * HEAD_PAD + LANES:(h + 1) * HEAD_PAD].astype(F32)
            o_ref[:, h * HEAD_PAD + LANES:(h + 1) * HEAD_PAD] = _rope_t(r, cos_v, sin_v).astype(BF16)

    return rowcall(body, rows=s, ts=min(ROW_TILE, s), ins=[(dq, "row"), (cos, "row"), (sin, "row")],
                   outs=[((s, N_HEADS * HEAD_PAD), BF16, "row")], name="qrope_bwd")[0]


_NT = (((1,), (1,)), ((), ()))
_TN = (((0,), (0,)), ((), ()))


def _scores(q, kvr, diagonal):
    s = lax.dot_general(q, kvr, _NT, preferred_element_type=F32)
    if not diagonal:
        return s
    row = lax.broadcasted_iota(jnp.int32, s.shape, 0)
    col = lax.broadcasted_iota(jnp.int32, s.shape, 1)
    return jnp.where(col <= row, s, -jnp.inf)


def _on_causal_pairs(pair, k_blk, fn):
    @pl.when(k_blk < 2 * pair)
    def _():
        fn(0, False)
        fn(1, False)

    @pl.when(k_blk == 2 * pair)
    def _():
        fn(0, True)
        fn(1, False)

    @pl.when(k_blk == 2 * pair + 1)
    def _():
        fn(1, True)


def attn_fwd(q, kv, kr):
    s = q.shape[0]
    t = min(ATT_TILE, s // 2)
    n = s // t
    np_ = n // 2

    def body(q_ref, kv_ref, kr_ref, o_ref, lse_ref, m_ref, l_ref, acc_ref):
        i, j = pl.program_id(1), pl.program_id(2)

        @pl.when(j == 0)
        def _():
            m_ref[...] = jnp.full_like(m_ref, -jnp.inf)
            l_ref[...] = jnp.zeros_like(l_ref)
            acc_ref[...] = jnp.zeros_like(acc_ref)

        def block(sub, diagonal):
            kvv = kv_ref[...]
            kvr = jnp.concatenate([kvv, kr_ref[...]], axis=1)
            sc = _scores(q_ref[pl.ds(sub * t, t), :], kvr, diagonal)
            m_prev = m_ref[sub]
            m_new = jnp.maximum(m_prev, jnp.max(sc, axis=-1, keepdims=True))
            alpha = jnp.exp(m_prev - m_new)
            p = jnp.exp(sc - m_new)
            l_ref[sub] = alpha * l_ref[sub] + jnp.sum(p, axis=-1, keepdims=True)
            acc_ref[sub] = alpha * acc_ref[sub] + jnp.dot(p.astype(BF16), kvv, preferred_element_type=F32)
            m_ref[sub] = m_new

        _on_causal_pairs(i, j, block)

        @pl.when(j == 2 * i + 1)
        def _():
            for sub in range(2):
                l = l_ref[sub]
                o_ref[pl.ds(sub * t, t), :] = (acc_ref[sub] / l).astype(BF16)
                lse_ref[pl.ds(sub * t, t), :] = jnp.broadcast_to(m_ref[sub] + jnp.log(l), (t, LANES))

    kj = lambda h, i, j: (jnp.minimum(j, 2 * i + 1), h)
    return pl.pallas_call(
        body, name="attn_fwd", grid=(N_HEADS, np_, n),
        in_specs=[pl.BlockSpec((2 * t, HEAD_PAD), lambda h, i, j: (i, h)),
                  pl.BlockSpec((t, LANES), kj),
                  pl.BlockSpec((t, LANES), lambda h, i, j: (jnp.minimum(j, 2 * i + 1), 0))],
        out_specs=(pl.BlockSpec((2 * t, LANES), lambda h, i, j: (i, h)),
                   pl.BlockSpec((2 * t, LANES), lambda h, i, j: (i, h))),
        out_shape=(jax.ShapeDtypeStruct((s, N_HEADS * LANES), BF16),
                   jax.ShapeDtypeStruct((s, N_HEADS * LANES), F32)),
        scratch_shapes=[pltpu.VMEM((2, t, 1), F32), pltpu.VMEM((2, t, 1), F32), pltpu.VMEM((2, t, LANES), F32)],
        compiler_params=_cparams("parallel", "parallel", "arbitrary"),
    )(q, kv, kr)


def attn_bwd(q, kv, kr, o, do, lse):
    s = q.shape[0]
    t = min(ATT_TILE, s // 2)
    n = s // t
    np_ = n // 2

    def body(q_ref, kv_ref, kr_ref, o_ref, do_ref, lse_ref, dq_ref, dkv_ref, dkr_ref):
        j, i = pl.program_id(1), pl.program_id(2)

        @pl.when((i == 0) & (j == 0))
        def _():
            dq_ref[...] = jnp.zeros_like(dq_ref)

        @pl.when(i == 0)
        def _():
            dkv_ref[...] = jnp.zeros_like(dkv_ref)
            dkr_ref[...] = jnp.zeros_like(dkr_ref)

        def block(sub, diagonal):
            sl = pl.ds(sub * t, t)
            qv, dov, kvv = q_ref[sl, :], do_ref[sl, :], kv_ref[...]
            kvr = jnp.concatenate([kvv, kr_ref[...]], axis=1)
            p = jnp.exp(_scores(qv, kvr, diagonal) - lse_ref[sl, :1])
            dp = lax.dot_general(dov, kvv, _NT, preferred_element_type=F32)
            delta = jnp.sum(dov.astype(F32) * o_ref[sl, :].astype(F32), axis=-1, keepdims=True)
            ds = (p * (dp - delta)).astype(BF16)
            dk = lax.dot_general(ds, qv, _TN, preferred_element_type=F32)
            dkv_ref[...] += lax.dot_general(p.astype(BF16), dov, _TN, preferred_element_type=F32) + dk[:, :LANES]
            dkr_ref[...] += dk[:, LANES:]
            rows = pl.ds(pl.multiple_of((2 * i + sub) * t, t), t)
            dq_ref[rows, :] += jnp.dot(ds, kvr, preferred_element_type=F32)

        _on_causal_pairs(i, j, block)

    qi = lambda h, j, i: (jnp.maximum(i, lax.div(j, 2)), h)
    kj = lambda h, j, i: (j, h)
    return pl.pallas_call(
        body, name="attn_bwd", grid=(N_HEADS, n, np_),
        in_specs=[pl.BlockSpec((2 * t, HEAD_PAD), qi), pl.BlockSpec((t, LANES), kj),
                  pl.BlockSpec((t, LANES), lambda h, j, i: (j, 0)),
                  pl.BlockSpec((2 * t, LANES), qi), pl.BlockSpec((2 * t, LANES), qi), pl.BlockSpec((2 * t, LANES), qi)],
        out_specs=(pl.BlockSpec((s, HEAD_PAD), lambda h, j, i: (0, h)),
                   pl.BlockSpec((t, LANES), kj), pl.BlockSpec((t, LANES), kj)),
        out_shape=(jax.ShapeDtypeStruct((s, N_HEADS * HEAD_PAD), F32),
                   jax.ShapeDtypeStruct((s, N_HEADS * LANES), F32), jax.ShapeDtypeStruct((s, N_HEADS * LANES), F32)),
        compiler_params=_cparams("parallel", "arbitrary", "arbitrary"),
    )(q, kv, kr, o, do, lse)


def ffn_fwd(x, g, wa, wb, cwa, cwb, ba, bb, wd, name):
    s, d = x.shape
    f = wa.shape[1]
    ts, tf = min(FFN_ROW_TILE, s), FFN_COL_TILE
    hal = FFN_HALO
    nj = f // tf
    first = hal - (FFN_K - 1)
    rc = min(FFN_ROW_CHUNK, ts)

    def body(x_ref, xp_ref, g_ref, wa_ref, wb_ref, cwa_ref, cwb_ref, ba_ref, bb_ref, wd_ref,
             xo_ref, xn_ref, hpa_ref, hpb_ref, act_ref, xe_ref, ea_ref, eb_ref):
        i, j = pl.program_id(0), pl.program_id(1)

        @pl.when(j == 0)
        def _():
            xn = _rms(x_ref[...], g_ref[...]).astype(BF16)
            xn_ref[...] = xn
            xe_ref[pl.ds(hal, ts), :] = xn
            xe_ref[pl.ds(0, hal), :] = jnp.where(i > 0, _rms(xp_ref[...], g_ref[...]), 0.0).astype(BF16)
            xo_ref[...] = x_ref[...]

        halves = ((0, ts // 2), (ts // 2, ts))
        for lo, hi in halves:
            e0, e1 = (0 if lo == 0 else hal + lo), hal + hi
            xe = xe_ref[pl.ds(e0, e1 - e0), :]
            ea_ref[pl.ds(e0, e1 - e0), :] = jnp.dot(xe, wa_ref[...], preferred_element_type=F32)
            eb_ref[pl.ds(e0, e1 - e0), :] = jnp.dot(xe, wb_ref[...], preferred_element_type=F32)
        for lo, hi in halves:
            hpa_ref[pl.ds(lo, hi - lo), :] = ea_ref[pl.ds(hal + lo, hi - lo), :].astype(BF16)
            hpb_ref[pl.ds(lo, hi - lo), :] = eb_ref[pl.ds(hal + lo, hi - lo), :].astype(BF16)
            for r0 in range(lo, hi, rc):
                ha = _conv_taps(ea_ref, cwa_ref, rc, first + r0, FFN_K) + ba_ref[...]
                hb = _conv_taps(eb_ref, cwb_ref, rc, first + r0, FFN_K) + bb_ref[...]
                act_ref[pl.ds(r0, rc), :] = (_silu(ha) * hb).astype(BF16)
            xo_ref[pl.ds(lo, hi - lo), :] += jnp.dot(act_ref[pl.ds(lo, hi - lo), :], wd_ref[...],
                                                     preferred_element_type=F32)

    r = ts // hal
    row = pl.BlockSpec((ts, d), lambda i, j: (i, 0))
    prev = pl.BlockSpec((hal, d), lambda i, j: (jnp.maximum(i * r - 1, 0), 0))
    gsp = pl.BlockSpec((1, d), lambda i, j: (0, 0))
    wup = pl.BlockSpec((d, tf), lambda i, j: (0, j))
    cwsp = pl.BlockSpec((8, tf), lambda i, j: (0, j))
    bsp = pl.BlockSpec((1, tf), lambda i, j: (0, j))
    wdn = pl.BlockSpec((tf, d), lambda i, j: (j, 0))
    hid = pl.BlockSpec((ts, tf), lambda i, j: (i, j))
    return pl.pallas_call(
        body, name=name, grid=(s // ts, nj),
        in_specs=[row, prev, gsp, wup, wup, cwsp, cwsp, bsp, bsp, wdn],
        out_specs=(row, row, hid, hid, hid),
        out_shape=(jax.ShapeDtypeStruct((s, d), F32), jax.ShapeDtypeStruct((s, d), BF16),
                   jax.ShapeDtypeStruct((s, f), BF16), jax.ShapeDtypeStruct((s, f), BF16),
                   jax.ShapeDtypeStruct((s, f), BF16)),
        scratch_shapes=[pltpu.VMEM((ts + hal, d), BF16), pltpu.VMEM((ts + hal, tf), F32),
                        pltpu.VMEM((ts + hal, tf), F32)],
        compiler_params=_cparams("parallel", "arbitrary"),
    )(x, x, g, wa, wb, cwa, cwb, ba, bb, wd)


def ffn_bwd(dy, x, g, hpa, hpb, wa, wb, cwa, cwb, ba, bb, wd, name):
    s, d = dy.shape
    f = hpa.shape[1]
    ts, tf = min(FFN_ROW_TILE, s), FFN_COL_TILE
    hal = FFN_HALO
    nt, nj = s // ts, f // tf
    te = ts + hal
    first = hal - (FFN_K - 1)
    rc = min(FFN_ROW_CHUNK, ts)

    def body(dy_ref, dyn_ref, x_ref, g_ref, a_ref, ap_ref, an_ref, b_ref, bp_ref, bn_ref, wa_ref, wb_ref,
             cwa_ref, cwb_ref, ba_ref, bb_ref, wd_ref,
             dx_ref, dpa_ref, dpb_ref, dwa_ref, dwb_ref, dba_ref, dbb_ref, dg_ref,
             dye_ref, ea_ref, eb_ref, dact_ref, da_ref, db_ref, dxn_ref):
        i, j = pl.program_id(0), pl.program_id(1)
        last = i == nt - 1

        @pl.when(j == 0)
        def _():
            dye_ref[pl.ds(0, ts), :] = dy_ref[...].astype(BF16)
            dye_ref[pl.ds(ts, hal), :] = jnp.where(last, 0.0, dyn_ref[...]).astype(BF16)
            dxn_ref[...] = jnp.zeros_like(dxn_ref)

        @pl.when((i == 0) & (j == 0))
        def _():
            for r in (dwa_ref, dwb_ref, dba_ref, dbb_ref, dg_ref):
                r[...] = jnp.zeros_like(r)

        halves = tuple((k * ts // FFN_BWD_PARTS, (k + 1) * ts // FFN_BWD_PARTS) for k in range(FFN_BWD_PARTS))
        for lo, hi in halves:
            n = hi - lo + (hal if hi == ts else 0)
            dact_ref[pl.ds(lo, n), :] = lax.dot_general(dye_ref[pl.ds(lo, n), :], wd_ref[...], _NT,
                                                        preferred_element_type=F32)
        for cur, prev, nxt, ext in ((a_ref, ap_ref, an_ref, ea_ref), (b_ref, bp_ref, bn_ref, eb_ref)):
            ext[pl.ds(0, hal), :] = jnp.where(i > 0, prev[...].astype(F32), 0.0)
            ext[pl.ds(hal, ts), :] = cur[...].astype(F32)
            ext[pl.ds(hal + ts, hal), :] = jnp.where(last, 0.0, nxt[...].astype(F32))
        zero = jnp.zeros((1, tf), F32)
        sums = {"ba": zero, "bb": zero, **{("a", k): zero for k in range(FFN_K)}, **{("b", k): zero for k in range(FFN_K)}}
        for r0 in list(range(0, ts, rc)) + [ts]:
            n = rc if r0 < ts else hal
            win_a = [ea_ref[pl.ds(first + r0 + k, n), :] for k in range(FFN_K)]
            win_b = [eb_ref[pl.ds(first + r0 + k, n), :] for k in range(FFN_K)]
            ha = sum(cwa_ref[pl.ds(k, 1), :] * win_a[k] for k in range(FFN_K)) + ba_ref[...]
            hb = sum(cwb_ref[pl.ds(k, 1), :] * win_b[k] for k in range(FFN_K)) + bb_ref[...]
            sig = _sigmoid(ha)
            gs = dact_ref[pl.ds(r0, n), :] * sig
            dha = gs * hb * (1.0 + ha * (1.0 - sig))
            dhb = gs * ha
            da_ref[pl.ds(r0, n), :] = dha
            db_ref[pl.ds(r0, n), :] = dhb
            if r0 < ts:
                sums["ba"] = sums["ba"] + jnp.sum(dha, axis=0, keepdims=True)
                sums["bb"] = sums["bb"] + jnp.sum(dhb, axis=0, keepdims=True)
                for k in range(FFN_K):
                    sums["a", k] = sums["a", k] + jnp.sum(dha * win_a[k], axis=0, keepdims=True)
                    sums["b", k] = sums["b", k] + jnp.sum(dhb * win_b[k], axis=0, keepdims=True)
        for lo, hi in halves:
            for r0 in range(lo, hi, rc):
                dpa_ref[pl.ds(r0, rc), :] = _conv_taps(da_ref, cwa_ref, rc, r0, FFN_K, flip=True).astype(BF16)
                dpb_ref[pl.ds(r0, rc), :] = _conv_taps(db_ref, cwb_ref, rc, r0, FFN_K, flip=True).astype(BF16)
            rows = pl.ds(lo, hi - lo)
            dxn_ref[rows, :] += (lax.dot_general(dpa_ref[rows, :], wa_ref[...], _NT, preferred_element_type=F32)
                                 + lax.dot_general(dpb_ref[rows, :], wb_ref[...], _NT, preferred_element_type=F32))
        dba_ref[j] += jnp.broadcast_to(sums["ba"], (8, tf))
        dbb_ref[j] += jnp.broadcast_to(sums["bb"], (8, tf))
        row = lax.broadcasted_iota(jnp.int32, (8, tf), 0)
        dwa_ref[j] += sum(jnp.where(row == k, sums["a", k], 0.0) for k in range(FFN_K))
        dwb_ref[j] += sum(jnp.where(row == k, sums["b", k], 0.0) for k in range(FFN_K))

        @pl.when(j == nj - 1)
        def _():
            _, vjp = jax.vjp(_rms, x_ref[...], g_ref[...])
            dx, dg = vjp(dxn_ref[...])
            dx_ref[...] = dx + dy_ref[...]
            dg_ref[...] += jnp.broadcast_to(jnp.sum(dg, axis=0, keepdims=True), dg_ref.shape)

    r = ts // hal
    lastblk = s // hal - 1
    row = pl.BlockSpec((ts, d), lambda i, j: (i, 0))
    gsp = pl.BlockSpec((1, d), lambda i, j: (0, 0))
    dgsp = pl.BlockSpec((8, d), lambda i, j: (0, 0))
    rown = pl.BlockSpec((hal, d), lambda i, j: (jnp.minimum((i + 1) * r, lastblk), 0))
    cur = pl.BlockSpec((ts, tf), lambda i, j: (i, j))
    prev = pl.BlockSpec((hal, tf), lambda i, j: (jnp.maximum(i * r - 1, 0), j))
    nxt = pl.BlockSpec((hal, tf), lambda i, j: (jnp.minimum((i + 1) * r, lastblk), j))
    wup = pl.BlockSpec((d, tf), lambda i, j: (0, j))
    cwsp = pl.BlockSpec((8, tf), lambda i, j: (0, j))
    bsp = pl.BlockSpec((1, tf), lambda i, j: (0, j))
    wdn = pl.BlockSpec((tf, d), lambda i, j: (j, 0))
    accsp = pl.BlockSpec((nj, 8, tf), lambda i, j: (0, 0, 0))
    accshape = jax.ShapeDtypeStruct((nj, 8, tf), F32)
    return pl.pallas_call(
        body, name=name, grid=(nt, nj),
        in_specs=[row, rown, row, gsp, cur, prev, nxt, cur, prev, nxt, wup, wup, cwsp, cwsp, bsp, bsp, wdn],
        out_specs=(row, cur, cur, accsp, accsp, accsp, accsp, dgsp),
        out_shape=(jax.ShapeDtypeStruct((s, d), F32), jax.ShapeDtypeStruct((s, f), BF16),
                   jax.ShapeDtypeStruct((s, f), BF16), accshape, accshape, accshape, accshape,
                   jax.ShapeDtypeStruct((8, d), F32)),
        scratch_shapes=[pltpu.VMEM((te, d), BF16), pltpu.VMEM((ts + 2 * hal, tf), F32),
                        pltpu.VMEM((ts + 2 * hal, tf), F32), pltpu.VMEM((te, tf), F32),
                        pltpu.VMEM((te, tf), F32), pltpu.VMEM((te, tf), F32), pltpu.VMEM((ts, d), F32)],
        compiler_params=_cparams("arbitrary", "arbitrary"),
    )(dy, dy, x, g, hpa, hpa, hpa, hpb, hpb, hpb, wa, wb, cwa, cwb, ba, bb, wd)


NQ = 4
SQ = SSM_STATE * 8
NS = SSM_GROUPS * SSM_STATE


def _s5_disc(log_dt, a_re, a_im, b_re, b_im, expand):
    dt = jnp.exp(log_dt)
    mag = jnp.exp(a_re * dt)
    lb_re, lb_im = mag * jnp.cos(a_im * dt), mag * jnp.sin(a_im * dt)
    den = a_re * a_re + a_im * a_im
    nr, ni = lb_re - 1.0, lb_im
    f_re = (nr * a_re + ni * a_im) / den
    f_im = (ni * a_re - nr * a_im) / den
    fe_re = jnp.dot(f_re, expand, precision=lax.Precision.HIGHEST, preferred_element_type=F32)
    fe_im = jnp.dot(f_im, expand, precision=lax.Precision.HIGHEST, preferred_element_type=F32)
    return lb_re, lb_im, fe_re * b_re - fe_im * b_im, fe_re * b_im + fe_im * b_re


def _expand_matrix():
    e = np.zeros((SSM_STATE, SSM_STATE * SSM_GROUP), np.float32)
    for p in range(SSM_STATE):
        e[p, p * SSM_GROUP:(p + 1) * SSM_GROUP] = 1.0
    return jnp.asarray(e)


def s5_params_fwd(log_dt, a_re, a_im, b_re, b_im):
    expand = _expand_matrix()

    def body(ld_ref, ar_ref, ai_ref, br_ref, bi_ref, e_ref, lr_ref, li_ref, bbr_ref, bbi_ref):
        lr, li, bbr, bbi = _s5_disc(ld_ref[...], ar_ref[...], ai_ref[...], br_ref[...], bi_ref[...], e_ref[...])
        lr_ref[...] = lr
        li_ref[...] = li
        bbr_ref[...] = bbr
        bbi_ref[...] = bbi

    g, p, pc = SSM_GROUPS, SSM_STATE, SSM_STATE * SSM_GROUP
    return pl.pallas_call(
        body, name="s5_params_fwd",
        out_shape=(jax.ShapeDtypeStruct((g, p), F32), jax.ShapeDtypeStruct((g, p), F32),
                   jax.ShapeDtypeStruct((g, pc), F32), jax.ShapeDtypeStruct((g, pc), F32)),
    )(log_dt, a_re, a_im, b_re, b_im, expand)


def s5_params_bwd(log_dt, a_re, a_im, b_re, b_im, dlr, dli, dbbr, dbbi):
    expand = _expand_matrix()

    def body(ld_ref, ar_ref, ai_ref, br_ref, bi_ref, e_ref, dlr_ref, dli_ref, dbbr_ref, dbbi_ref,
             dld_ref, dar_ref, dai_ref, dbr_ref, dbi_ref):
        e = e_ref[...]
        f = lambda ld, ar, ai, br, bi: _s5_disc(ld, ar, ai, br, bi, e)
        _, vjp = jax.vjp(f, ld_ref[...], ar_ref[...], ai_ref[...], br_ref[...], bi_ref[...])
        dld, dar, dai, dbr, dbi = vjp((dlr_ref[...], dli_ref[...], dbbr_ref[...], dbbi_ref[...]))
        dld_ref[...] = dld
        dar_ref[...] = dar
        dai_ref[...] = dai
        dbr_ref[...] = dbr
        dbi_ref[...] = dbi

    g, p, pc = SSM_GROUPS, SSM_STATE, SSM_STATE * SSM_GROUP
    return pl.pallas_call(
        body, name="s5_params_bwd",
        out_shape=(jax.ShapeDtypeStruct((g, 1), F32), jax.ShapeDtypeStruct((g, p), F32),
                   jax.ShapeDtypeStruct((g, p), F32), jax.ShapeDtypeStruct((g, pc), F32),
                   jax.ShapeDtypeStruct((g, pc), F32)),
    )(log_dt, a_re, a_im, b_re, b_im, expand, dlr, dli, dbbr, dbbi)


def _cmul(ar, ai, br, bi):
    return ar * br - ai * bi, ar * bi + ai * br


def _power_rows(lr, li, conj_rev):
    row = lax.broadcasted_iota(jnp.int32, (8, NS), 0)
    tr = jnp.zeros((8, NS), F32)
    ti = jnp.zeros((8, NS), F32)
    pr, pi = lr, li
    for r in range(8):
        dst = 7 - r if conj_rev else r
        tr = jnp.where(row == dst, pr, tr)
        ti = jnp.where(row == dst, -pi if conj_rev else pi, ti)
        if r < 7:
            pr, pi = _cmul(pr, pi, lr, li)
    return tr, ti


def _scan8(xr, xi, tr_ref, ti_ref, cr, ci, reverse):
    row = lax.broadcasted_iota(jnp.int32, xr.shape, 0)
    for d in (1, 2, 4):
        if reverse:
            sr, si = pltpu.roll(xr, 8 - d, 0), pltpu.roll(xi, 8 - d, 0)
            keep = row < 8 - d
            pw = 8 - d
        else:
            sr, si = pltpu.roll(xr, d, 0), pltpu.roll(xi, d, 0)
            keep = row >= d
            pw = d - 1
        mr, mi = _cmul(tr_ref[pl.ds(pw, 1), :], ti_ref[pl.ds(pw, 1), :], sr, si)
        xr = xr + jnp.where(keep, mr, 0.0)
        xi = xi + jnp.where(keep, mi, 0.0)
    mr, mi = _cmul(tr_ref[...], ti_ref[...], cr, ci)
    return xr + mr, xi + mi


def _row_of(x, r):
    row = lax.broadcasted_iota(jnp.int32, x.shape, 0)
    return jnp.sum(jnp.where(row == r, x, 0.0), axis=0, keepdims=True)


def s5_scan_fwd(u, lam_r, lam_i, bre, bim, cre, cim, dskip):
    s = u.shape[0]
    tt = min(SCAN_TILE, s)
    nb = tt // 8

    def body(nt, u_ref, lr_ref, li_ref, bre_ref, bim_ref, cre_ref, cim_ref, d_ref,
             xr_ref, xi_ref, y_ref, yg_ref, tr_ref, ti_ref, cr_ref, ci_ref):
        i = pl.program_id(0)

        @pl.when(i == 0)
        def _():
            tr, ti = _power_rows(lr_ref[...], li_ref[...], False)
            tr_ref[...] = tr
            ti_ref[...] = ti
            cr_ref[...] = jnp.zeros_like(cr_ref)
            ci_ref[...] = jnp.zeros_like(ci_ref)

        uv = u_ref[...]
        ub = uv.astype(BF16)
        for q in range(NQ):
            uq = ub[:, q * LANES:(q + 1) * LANES]
            xr_ref[:, q * SQ:(q + 1) * SQ] = jnp.dot(uq, bre_ref[q], preferred_element_type=F32)
            xi_ref[:, q * SQ:(q + 1) * SQ] = jnp.dot(uq, bim_ref[q], preferred_element_type=F32)

        def step(b, carry):
            cr, ci = carry
            rows = pl.ds(pl.multiple_of(b * 8, 8), 8)
            xr, xi = _scan8(xr_ref[rows, :], xi_ref[rows, :], tr_ref, ti_ref, cr, ci, False)
            xr_ref[rows, :] = xr
            xi_ref[rows, :] = xi
            return _row_of(xr, 7), _row_of(xi, 7)

        cr, ci = lax.fori_loop(0, nb, step, (cr_ref[...], ci_ref[...]), unroll=min(SCAN_UNROLL, nb))
        cr_ref[...] = cr
        ci_ref[...] = ci
        y = d_ref[...] * uv
        for q in range(NQ):
            yq = (jnp.dot(xr_ref[:, q * SQ:(q + 1) * SQ].astype(BF16), cre_ref[q], preferred_element_type=F32)
                  - jnp.dot(xi_ref[:, q * SQ:(q + 1) * SQ].astype(BF16), cim_ref[q], preferred_element_type=F32))
            y_ref[:, q * LANES:(q + 1) * LANES] = yq + y[:, q * LANES:(q + 1) * LANES]
        yg_ref[...] = _gelu(y_ref[...]).astype(BF16)

    return rowcall(body, rows=s, ts=tt,
                   ins=[(u, "row"), (lam_r, "full"), (lam_i, "full"), (bre, "full"), (bim, "full"),
                        (cre, "full"), (cim, "full"), (dskip, "full")],
                   outs=[((s, NS), F32, "row"), ((s, NS), F32, "row"), ((s, SSM_WIDTH), F32, "row"),
                         ((s, SSM_WIDTH), BF16, "row")], name="s5_scan_fwd",
                   scratch=[pltpu.VMEM((8, NS), F32), pltpu.VMEM((8, NS), F32),
                            pltpu.VMEM((1, NS), F32), pltpu.VMEM((1, NS), F32)])


def s5_scan_bwd(dyg, y, lam_r, lam_i, cre, cim):
    s = y.shape[0]
    tt = min(SCAN_TILE, s)
    nb = tt // 8

    def body(nt, dyg_ref, y_ref, lr_ref, li_ref, cre_ref, cim_ref,
             ar_ref, ai_ref, dy_ref, tr_ref, ti_ref, cr_ref, ci_ref):
        i = pl.program_id(0)

        @pl.when(i == 0)
        def _():
            tr, ti = _power_rows(lr_ref[...], li_ref[...], True)
            tr_ref[...] = tr
            ti_ref[...] = ti
            cr_ref[...] = jnp.zeros_like(cr_ref)
            ci_ref[...] = jnp.zeros_like(ci_ref)

        _, vjp = jax.vjp(_gelu, y_ref[...])
        dy = vjp(dyg_ref[...])[0]
        dyb = dy.astype(BF16)
        dy_ref[...] = dyb
        for q in range(NQ):
            dq = dyb[:, q * LANES:(q + 1) * LANES]
            ar_ref[:, q * SQ:(q + 1) * SQ] = lax.dot_general(dq, cre_ref[q], _NT, preferred_element_type=F32)
            ai_ref[:, q * SQ:(q + 1) * SQ] = -lax.dot_general(dq, cim_ref[q], _NT, preferred_element_type=F32)

        def step(b, carry):
            cr, ci = carry
            rows = pl.ds(pl.multiple_of((nb - 1 - b) * 8, 8), 8)
            xr, xi = _scan8(ar_ref[rows, :], ai_ref[rows, :], tr_ref, ti_ref, cr, ci, True)
            ar_ref[rows, :] = xr
            ai_ref[rows, :] = xi
            return _row_of(xr, 0), _row_of(xi, 0)

        cr, ci = lax.fori_loop(0, nb, step, (cr_ref[...], ci_ref[...]), unroll=min(SCAN_UNROLL, nb))
        cr_ref[...] = cr
        ci_ref[...] = ci

    return rowcall(body, rows=s, ts=tt,
                   ins=[(dyg, "rev"), (y, "rev"), (lam_r, "full"), (lam_i, "full"), (cre, "full"), (cim, "full")],
                   outs=[((s, NS), F32, "rev"), ((s, NS), F32, "rev"), ((s, SSM_WIDTH), BF16, "rev")],
                   name="s5_scan_bwd",
                   scratch=[pltpu.VMEM((8, NS), F32), pltpu.VMEM((8, NS), F32),
                            pltpu.VMEM((1, NS), F32), pltpu.VMEM((1, NS), F32)])


def s5_grads(u, dy, xr, xi, ar, ai, bre, bim, dskip):
    s = u.shape[0]
    tt = min(SCAN_TILE, s)

    def body(nt, u_ref, dy_ref, xr_ref, xrp_ref, xi_ref, xip_ref, ar_ref, ai_ref, bre_ref, bim_ref, d_ref,
             du_ref, dlr_ref, dli_ref, dbr_ref, dbi_ref, dcr_ref, dci_ref, dd_ref, er_ref, ei_ref):
        i = pl.program_id(0)

        @pl.when(i == 0)
        def _():
            for r in (dbr_ref, dbi_ref, dcr_ref, dci_ref):
                r[...] = jnp.zeros_like(r)

        uv, dyb = u_ref[...], dy_ref[...]
        dyf = dyb.astype(F32)
        av_r, av_i, xv_r, xv_i = ar_ref[...], ai_ref[...], xr_ref[...], xi_ref[...]
        er_ref[pl.ds(0, 8), :] = jnp.where(i > 0, xrp_ref[...], 0.0)
        ei_ref[pl.ds(0, 8), :] = jnp.where(i > 0, xip_ref[...], 0.0)
        er_ref[pl.ds(8, tt), :] = xv_r
        ei_ref[pl.ds(8, tt), :] = xv_i
        sr, si = er_ref[pl.ds(7, tt), :], ei_ref[pl.ds(7, tt), :]
        _acc(dlr_ref, i, av_r * sr + av_i * si)
        _acc(dli_ref, i, av_i * sr - av_r * si)
        _acc(dd_ref, i, dyf * uv)
        ub = uv.astype(BF16)
        ab_r, ab_i = av_r.astype(BF16), av_i.astype(BF16)
        xb_r, xb_i = xv_r.astype(BF16), xv_i.astype(BF16)
        du = d_ref[...] * dyf
        for q in range(NQ):
            cs, ss = slice(q * LANES, (q + 1) * LANES), slice(q * SQ, (q + 1) * SQ)
            dbr_ref[q] += lax.dot_general(ub[:, cs], ab_r[:, ss], _TN, preferred_element_type=F32)
            dbi_ref[q] += lax.dot_general(ub[:, cs], ab_i[:, ss], _TN, preferred_element_type=F32)
            dcr_ref[q] += lax.dot_general(xb_r[:, ss], dyb[:, cs], _TN, preferred_element_type=F32)
            dci_ref[q] -= lax.dot_general(xb_i[:, ss], dyb[:, cs], _TN, preferred_element_type=F32)
            du_ref[:, cs] = (du[:, cs]
                             + lax.dot_general(ab_r[:, ss], bre_ref[q], _NT, preferred_element_type=F32)
                             + lax.dot_general(ab_i[:, ss], bim_ref[q], _NT, preferred_element_type=F32))

    return rowcall(body, rows=s, ts=tt,
                   ins=[(u, "row"), (dy, "row"), (xr, "row"), (xr, "prev:8"), (xi, "row"), (xi, "prev:8"),
                        (ar, "row"), (ai, "row"), (bre, "full"), (bim, "full"), (dskip, "full")],
                   outs=[((s, SSM_WIDTH), F32, "row"), ((8, NS), F32, "acc"), ((8, NS), F32, "acc"),
                         ((NQ, LANES, SQ), F32, "acc"), ((NQ, LANES, SQ), F32, "acc"),
                         ((NQ, SQ, LANES), F32, "acc"), ((NQ, SQ, LANES), F32, "acc"),
                         ((8, SSM_WIDTH), F32, "acc")], name="s5_grads",
                   scratch=[pltpu.VMEM((tt + 8, NS), F32), pltpu.VMEM((tt + 8, NS), F32)])


def _glu_fn(za, zb):
    return za * _sigmoid(zb)


def glu_res_fwd(z, xres):
    s = z.shape[0]

    def body(nt, z_ref, x_ref, o_ref):
        o_ref[...] = x_ref[...] + _glu_fn(z_ref[:, :D_MODEL].astype(F32), z_ref[:, D_MODEL:].astype(F32))

    return rowcall(body, rows=s, ts=min(ROW_TILE, s), ins=[(z, "row"), (xres, "row")],
                   outs=[((s, D_MODEL), F32, "row")], name="glu_res_fwd")[0]


def glu_bwd(z, dout):
    s, c = z.shape

    def body(nt, z_ref, d_ref, dz_ref, dba_ref, dbb_ref):
        i = pl.program_id(0)
        _, vjp = jax.vjp(_glu_fn, z_ref[:, :D_MODEL].astype(F32), z_ref[:, D_MODEL:].astype(F32))
        dza, dzb = vjp(d_ref[...])
        dz_ref[:, :D_MODEL] = dza.astype(BF16)
        dz_ref[:, D_MODEL:] = dzb.astype(BF16)
        _acc(dba_ref, i, dza)
        _acc(dbb_ref, i, dzb)

    return rowcall(body, rows=s, ts=min(ROW_TILE, s), ins=[(z, "row"), (dout, "row")],
                   outs=[((s, c), BF16, "row"), ((8, D_MODEL), F32, "acc"), ((8, D_MODEL), F32, "acc")],
                   name="glu_bwd")


def loss_head(x, g, target):
    s, c = x.shape

    def body(nt, x_ref, g_ref, t_ref, loss_ref, dx_ref, dg_ref):
        i = pl.program_id(0)
        y, vjp = jax.vjp(_rms, x_ref[...], g_ref[...])
        err = y - t_ref[...]
        dx, dg = vjp(err * (1.0 / c))
        dx_ref[...] = dx
        _acc(dg_ref, i, dg)
        part = jnp.sum(jnp.sum(err * err, axis=-1, keepdims=True), axis=0, keepdims=True) * (0.5 / c)

        @pl.when(i == 0)
        def _():
            loss_ref[...] = jnp.zeros_like(loss_ref)

        loss_ref[...] += jnp.broadcast_to(part, loss_ref.shape)

    return rowcall(body, rows=s, ts=min(ROW_TILE, s), ins=[(x, "row"), (g, "full"), (target, "row")],
                   outs=[((8, LANES), F32, "acc"), ((s, c), F32, "row"), ((8, c), F32, "acc")], name="loss_head")


def _tile_rows(r, cands=(512, 256, 128, 64, 32, 16, 8)):
    return _pick(r, cands)


def add_to_bf16(a, b, name):
    n, r, c = a.shape
    tr = _tile_rows(r)

    def body(a_ref, b_ref, o_ref):
        o_ref[...] = (a_ref[...].astype(F32) + b_ref[...].astype(F32)).astype(BF16)

    spec = pl.BlockSpec((1, tr, c), lambda j, i: (j, i, 0))
    return pl.pallas_call(body, name=name, grid=(n, r // tr), in_specs=[spec, spec], out_specs=spec,
                          out_shape=jax.ShapeDtypeStruct((n, r, c), BF16),
                          compiler_params=_cparams("parallel", "parallel"))(a, b)


def sum_leading(a, name):
    n, r, c = a.shape
    tr = _tile_rows(r)

    def body(a_ref, o_ref):
        acc = a_ref[0].astype(F32)
        for k in range(1, n):
            acc = acc + a_ref[k].astype(F32)
        o_ref[...] = acc

    return pl.pallas_call(body, name=name, grid=(r // tr,),
                          in_specs=[pl.BlockSpec((n, tr, c), lambda i: (0, i, 0))],
                          out_specs=pl.BlockSpec((tr, c), lambda i: (i, 0)),
                          out_shape=jax.ShapeDtypeStruct((r, c), F32),
                          compiler_params=_cparams("parallel"))(a)


def adamw(w, g, m, v, name):
    r, c = w.shape
    tr = _tile_rows(r, (256, 128, 64, 32, 16, 8))
    c1 = 1.0 - ADAM_B1 ** ADAM_STEP
    c2 = 1.0 - ADAM_B2 ** ADAM_STEP

    def body(w_ref, g_ref, m_ref, v_ref, d_ref, nm_ref, nv_ref):
        gv = g_ref[...]
        mn = ADAM_B1 * m_ref[...] + (1.0 - ADAM_B1) * gv
        vn = ADAM_B2 * v_ref[...] + (1.0 - ADAM_B2) * (gv * gv)
        d_ref[...] = -ADAM_LR * ((mn / c1) / (jnp.sqrt(vn / c2) + ADAM_EPS) + ADAM_WD * w_ref[...])
        nm_ref[...] = mn
        nv_ref[...] = vn

    spec = pl.BlockSpec((tr, c), lambda i: (i, 0))
    shp = jax.ShapeDtypeStruct((r, c), F32)
    return pl.pallas_call(body, name=name, grid=(r // tr,), in_specs=[spec] * 4, out_specs=(spec,) * 3,
                          out_shape=(shp,) * 3, compiler_params=_cparams("parallel"))(w, g, m, v)


_ANY = pl.BlockSpec(memory_space=pl.ANY)


def all_gather8(block, name):
    r, c = block.shape

    def body(x_ref, out_ref, send_sems, recv_sems, local_sem):
        x, y, cc = lax.axis_index("x"), lax.axis_index("y"), lax.axis_index("c")
        me, sibling = (x, y, cc), (x, y, 1 - cc)
        chips = [(1 - x, y), (x, 1 - y), (1 - x, 1 - y)]

        def slot(px, py, pc):
            return out_ref.at[4 * px + 2 * py + pc]

        def copy(k, blk, to, src=None):
            return pltpu.make_async_remote_copy(
                src_ref=slot(*blk) if src is None else src, dst_ref=slot(*blk),
                send_sem=send_sems.at[k], recv_sem=recv_sems.at[k], device_id=to, device_id_type=MESH)

        mine = pltpu.make_async_copy(x_ref, slot(*me), local_sem)
        mine.start()
        first = [copy(0, me, sibling, src=x_ref)]
        first += [copy(1 + j, me, (*chip, cc), src=x_ref) for j, chip in enumerate(chips)]
        for cp in first:
            cp.start()
        passed = [copy(4 + j, (*chip, cc), sibling) for j, chip in enumerate(chips)]
        for j, chip in enumerate(chips):
            copy(1 + j, (*chip, cc), me).wait_recv()
            passed[j].start()
        copy(0, sibling, me).wait_recv()
        for j, chip in enumerate(chips):
            copy(4 + j, (*chip, 1 - cc), me).wait_recv()
        for cp in first + passed:
            cp.wait_send()
        mine.wait()

    return pl.pallas_call(
        body, name=name, in_specs=[_ANY], out_specs=_ANY,
        out_shape=jax.ShapeDtypeStruct((8, r, c), block.dtype),
        scratch_shapes=[pltpu.SemaphoreType.DMA((7,)), pltpu.SemaphoreType.DMA((7,)), pltpu.SemaphoreType.DMA],
    )(block)


_HBM = pl.BlockSpec(memory_space=pltpu.HBM)
_SEM = pl.BlockSpec(memory_space=pltpu.SEMAPHORE)
_DATAFLOW = pltpu.SideEffectType.DATAFLOW_SIDE_EFFECTING
N_REMOTE = 6


def _remote_peers(x, y, cc):
    return [(1 - x, y, cc), (x, 1 - y, cc), (1 - x, 1 - y, cc),
            (1 - x, y, 1 - cc), (x, 1 - y, 1 - cc), (1 - x, 1 - y, 1 - cc)]


def gather_start(block, after, name):
    r, c = block.shape

    def body(x_ref, land_ref, after_ref, send_sems, recv_sems, x_thru, land_thru, token):
        x, y, cc = lax.axis_index("x"), lax.axis_index("y"), lax.axis_index("c")
        for k, peer in enumerate(_remote_peers(x, y, cc)):
            pltpu.make_async_remote_copy(src_ref=x_ref, dst_ref=land_ref.at[4 * x + 2 * y + cc],
                                         send_sem=send_sems.at[k], recv_sem=recv_sems.at[k],
                                         device_id=peer, device_id_type=MESH).start()
        token[...] = jnp.zeros_like(token)

    land = pltpu.with_memory_space_constraint(lax.empty((8, r, c), block.dtype), pltpu.HBM)
    return pl.pallas_call(
        body, name=name,
        out_shape=(pltpu.SemaphoreType.DMA((N_REMOTE,)), pltpu.SemaphoreType.DMA((N_REMOTE,)),
                   pltpu.HBM((r, c), block.dtype), pltpu.HBM((8, r, c), block.dtype),
                   jax.ShapeDtypeStruct((8, LANES), F32)),
        in_specs=(_HBM, _HBM, _ANY), out_specs=(_SEM, _SEM, _HBM, _HBM, pl.BlockSpec(memory_space=pltpu.VMEM)),
        input_output_aliases={0: 2, 1: 3},
        compiler_params=pltpu.CompilerParams(has_side_effects=_DATAFLOW),
    )(pltpu.with_memory_space_constraint(block, pltpu.HBM), land, after)


def gather_wait(send_sems, recv_sems, block_thru, land_thru, after, name):
    def body(x_ref, land_ref, send_sems, recv_sems, after_ref, x_dead, got_ref):
        x, y, cc = lax.axis_index("x"), lax.axis_index("y"), lax.axis_index("c")
        for k, (px, py, pc) in enumerate(_remote_peers(x, y, cc)):
            cp = pltpu.make_async_remote_copy(src_ref=x_ref, dst_ref=land_ref.at[4 * px + 2 * py + pc],
                                              send_sem=send_sems.at[k], recv_sem=recv_sems.at[k],
                                              device_id=(px, py, pc), device_id_type=MESH)
            cp.wait_send()
            cp.wait_recv()

    return pl.pallas_call(
        body, name=name,
        out_shape=(pltpu.HBM(block_thru.shape, block_thru.dtype), pltpu.HBM(land_thru.shape, land_thru.dtype)),
        in_specs=(_HBM, _HBM, _SEM, _SEM, _ANY), out_specs=(_HBM, _HBM), input_output_aliases={0: 0, 1: 1},
        compiler_params=pltpu.CompilerParams(has_side_effects=_DATAFLOW),
    )(block_thru, land_thru, send_sems, recv_sems, after)[1]


def _chip_peers(x, y):
    return [(1 - x, y), (x, 1 - y), (1 - x, 1 - y)]


def exchange_start(parts, name):
    def body(p_ref, land_ref, send_sems, recv_sems, p_thru, land_thru, token):
        x, y, cc = lax.axis_index("x"), lax.axis_index("y"), lax.axis_index("c")
        for k, (px, py) in enumerate(_chip_peers(x, y)):
            pltpu.make_async_remote_copy(src_ref=p_ref.at[2 * px + py], dst_ref=land_ref.at[2 * x + y],
                                         send_sem=send_sems.at[k], recv_sem=recv_sems.at[k],
                                         device_id=(px, py, cc), device_id_type=MESH).start()
        token[...] = jnp.zeros_like(token)

    land = pltpu.with_memory_space_constraint(lax.empty(parts.shape, parts.dtype), pltpu.HBM)
    return pl.pallas_call(
        body, name=name,
        out_shape=(pltpu.SemaphoreType.DMA((3,)), pltpu.SemaphoreType.DMA((3,)),
                   pltpu.HBM(parts.shape, parts.dtype), pltpu.HBM(parts.shape, parts.dtype),
                   jax.ShapeDtypeStruct((8, LANES), F32)),
        in_specs=(_HBM, _HBM), out_specs=(_SEM, _SEM, _HBM, _HBM, pl.BlockSpec(memory_space=pltpu.VMEM)),
        input_output_aliases={0: 2, 1: 3},
        compiler_params=pltpu.CompilerParams(has_side_effects=_DATAFLOW),
    )(pltpu.with_memory_space_constraint(parts, pltpu.HBM), land)


def exchange_wait(send_sems, recv_sems, parts_thru, land_thru, after, name):
    def body(p_ref, land_ref, send_sems, recv_sems, after_ref, p_dead, got_ref):
        x, y, cc = lax.axis_index("x"), lax.axis_index("y"), lax.axis_index("c")
        for k, (px, py) in enumerate(_chip_peers(x, y)):
            cp = pltpu.make_async_remote_copy(src_ref=p_ref.at[2 * px + py], dst_ref=land_ref.at[2 * px + py],
                                              send_sem=send_sems.at[k], recv_sem=recv_sems.at[k],
                                              device_id=(px, py, cc), device_id_type=MESH)
            cp.wait_send()
            cp.wait_recv()

    return pl.pallas_call(
        body, name=name,
        out_shape=(pltpu.HBM(parts_thru.shape, parts_thru.dtype), pltpu.HBM(land_thru.shape, land_thru.dtype)),
        in_specs=(_HBM, _HBM, _SEM, _SEM, _ANY), out_specs=(_HBM, _HBM), input_output_aliases={0: 0, 1: 1},
        compiler_params=pltpu.CompilerParams(has_side_effects=_DATAFLOW),
    )(parts_thru, land_thru, send_sems, recv_sems, after)


def sibling_swap(block, name):
    def body(x_ref, out_ref, send_sem, recv_sem):
        x, y, cc = lax.axis_index("x"), lax.axis_index("y"), lax.axis_index("c")
        cp = pltpu.make_async_remote_copy(src_ref=x_ref, dst_ref=out_ref, send_sem=send_sem, recv_sem=recv_sem,
                                          device_id=(x, y, 1 - cc), device_id_type=MESH)
        cp.start()
        cp.wait()

    return pl.pallas_call(
        body, name=name, in_specs=[_ANY], out_specs=_ANY,
        out_shape=jax.ShapeDtypeStruct(block.shape, block.dtype),
        scratch_shapes=[pltpu.SemaphoreType.DMA, pltpu.SemaphoreType.DMA],
    )(block)


PACK_COLS = 1024
SHARDED = (("l0_w_in", 1), ("l0_w_uq", 1), ("l0_w_ukv", 1), ("l0_w_out", 0), ("l0_w_up", 1), ("l0_w_down", 0),
           ("l1_w_in", 0), ("l1_w_glu", 1), ("l1_w_up", 1), ("l1_w_down", 0),
           ("l0_conv_w", 1), ("l0_ffn_conv_w", 1), ("l1_ffn_conv_w", 1))
REPLICATED = ("l0_mix_norm", "l0_conv_b", "l0_conv_ln_g", "l0_conv_ln_b", "l0_q_norm", "l0_kv_norm", "l0_ffn_norm",
              "l0_ffn_conv_b", "l1_mix_norm", "l1_log_dt", "l1_a_re", "l1_a_im", "l1_b_re", "l1_b_im", "l1_c_re",
              "l1_c_im", "l1_d", "l1_b_glu", "l1_ffn_norm", "l1_ffn_conv_b", "final_norm")


def _pack(arrs, dtype, mult):
    flat = jnp.concatenate([a.reshape(-1).astype(dtype) for a in arrs])
    n = flat.shape[0]
    total = -(-n // mult) * mult
    return jnp.pad(flat, (0, total - n))


def _unpack(flat, shapes):
    out, pos = [], 0
    for shp in shapes:
        n = int(np.prod(shp))
        out.append(flat[pos:pos + n].reshape(shp))
        pos += n
    return out


PACK_ROW_ALIGN = 16


def _pack_rows(arrs, dtype, row_mult):
    parts = []
    for a in arrs:
        n = int(np.prod(a.shape))
        rows = -(-n // PACK_COLS)
        if n % PACK_COLS == 0:
            r = a.astype(dtype).reshape(rows, PACK_COLS)
        else:
            r = jnp.pad(a.reshape(-1).astype(dtype), (0, rows * PACK_COLS - n)).reshape(rows, PACK_COLS)
        parts.append(jnp.pad(r, ((0, (-rows) % PACK_ROW_ALIGN), (0, 0))))
    p = jnp.concatenate(parts)
    return jnp.pad(p, ((0, (-p.shape[0]) % row_mult), (0, 0)))


def _unpack_rows(pack, shapes):
    out, r0 = [], 0
    for shp in shapes:
        n = int(np.prod(shp))
        rows = -(-n // PACK_COLS)
        piece = lax.optimization_barrier(pack[r0:r0 + rows])
        out.append(piece.reshape(shp) if n % PACK_COLS == 0 else piece.reshape(-1)[:n].reshape(shp))
        r0 += rows + (-rows) % PACK_ROW_ALIGN
    return out


def _shard(full, axis, j):
    n = full.shape[axis] // N_CHIPS
    return lax.slice_in_dim(full, j * n, (j + 1) * n, axis=axis)


def _block_diag(t):
    q, g, a, b = t.shape
    eye = jnp.eye(g, dtype=t.dtype)
    return jnp.einsum("qgab,gh->qgahb", t, eye).reshape(q, g * a, g * b)


def _block_diag_t(d, a, b):
    q = d.shape[0]
    d5 = d.reshape(q, 8, a, 8, b)
    eye = jnp.eye(8, dtype=d.dtype)
    return jnp.einsum("qgahb,gh->qgab", d5, eye)


def kernel(x, l0_mix_norm, l0_w_in, l0_conv_w, l0_conv_b, l0_conv_ln_g, l0_conv_ln_b, l0_q_norm, l0_kv_norm, l0_w_uq, l0_w_ukv, l0_w_out, l0_ffn_norm, l0_w_up, l0_ffn_conv_w, l0_ffn_conv_b, l0_w_down, l1_mix_norm, l1_w_in, l1_log_dt, l1_a_re, l1_a_im, l1_b_re, l1_b_im, l1_c_re, l1_c_im, l1_d, l1_w_glu, l1_b_glu, l1_ffn_norm, l1_w_up, l1_ffn_conv_w, l1_ffn_conv_b, l1_w_down, final_norm, loss_target, m_l0_mix_norm, m_l0_w_in, m_l0_conv_w, m_l0_conv_b, m_l0_conv_ln_g, m_l0_conv_ln_b, m_l0_q_norm, m_l0_kv_norm, m_l0_w_uq, m_l0_w_ukv, m_l0_w_out, m_l0_ffn_norm, m_l0_w_up, m_l0_ffn_conv_w, m_l0_ffn_conv_b, m_l0_w_down, m_l1_mix_norm, m_l1_w_in, m_l1_log_dt, m_l1_a_re, m_l1_a_im, m_l1_b_re, m_l1_b_im, m_l1_c_re, m_l1_c_im, m_l1_d, m_l1_w_glu, m_l1_b_glu, m_l1_ffn_norm, m_l1_w_up, m_l1_ffn_conv_w, m_l1_ffn_conv_b, m_l1_w_down, m_final_norm, v_l0_mix_norm, v_l0_w_in, v_l0_conv_w, v_l0_conv_b, v_l0_conv_ln_g, v_l0_conv_ln_b, v_l0_q_norm, v_l0_kv_norm, v_l0_w_uq, v_l0_w_ukv, v_l0_w_out, v_l0_ffn_norm, v_l0_w_up, v_l0_ffn_conv_w, v_l0_ffn_conv_b, v_l0_w_down, v_l1_mix_norm, v_l1_w_in, v_l1_log_dt, v_l1_a_re, v_l1_a_im, v_l1_b_re, v_l1_b_im, v_l1_c_re, v_l1_c_im, v_l1_d, v_l1_w_glu, v_l1_b_glu, v_l1_ffn_norm, v_l1_w_up, v_l1_ffn_conv_w, v_l1_ffn_conv_b, v_l1_w_down, v_final_norm):
    a = dict(locals())
    w = {n: a[n] for n in [s for s, _ in SHARDED] + list(REPLICATED)}
    mom = {n: a["m_" + n] for n in w}
    var = {n: a["v_" + n] for n in w}
    return _step(a["x"][0], a["loss_target"][0], w, mom, var)


FIRST_WEIGHTS = ("l0_w_in", "l0_w_uq", "l0_w_ukv", "l0_w_out")
LATER_WEIGHTS = ("l0_w_up", "l0_w_down", "l1_w_in", "l1_w_glu", "l1_w_up", "l1_w_down")


def _assemble(got, names, w):
    got = got.reshape(N_CHIPS, -1, PACK_COLS)
    shapes = [w[n].shape for n in names]
    per_chip = [_unpack_rows(got[j], shapes) for j in range(N_CHIPS)]
    axes = dict(SHARDED)
    return {n: jnp.concatenate([per_chip[j][k] for j in range(N_CHIPS)], axis=axes[n]) for k, n in enumerate(names)}


def _gather_weights(w):
    cc = lax.axis_index("c")
    small = [n for n, _ in SHARDED[10:]]
    full = {}
    for names, dtype, mult in ((FIRST_WEIGHTS, BF16, 2 * 256), (small, F32, 2 * PACK_ROW_ALIGN)):
        pack = _pack_rows([w[n] for n in names], dtype, mult)
        half = lax.dynamic_index_in_dim(pack.reshape(2, -1, PACK_COLS), cc, axis=0, keepdims=False)
        got = all_gather8(half, "gather_" + ("first_matrices" if dtype == BF16 else "conv_weights"))
        full.update(_assemble(got, names, w))
    pack = _pack_rows([w[n] for n in LATER_WEIGHTS], BF16, 2 * 256).reshape(2, -1, PACK_COLS)
    half = lax.dynamic_index_in_dim(pack, cc, axis=0, keepdims=False)
    send_sems, recv_sems, half_thru, land_thru, token = gather_start(half, got, "gather_later_start")
    return full, (send_sems, recv_sems, half_thru, land_thru, pack), token[0, 0]


def _finish_gather(pending, after, w):
    send_sems, recv_sems, half_thru, land_thru, pack = pending
    got = gather_wait(send_sems, recv_sems, half_thru, land_thru, after, "gather_later_wait")
    chip = 2 * lax.axis_index("x") + lax.axis_index("y")
    got = lax.dynamic_update_slice(got, pack, (2 * chip, 0, 0))
    return _assemble(got, LATER_WEIGHTS, w)


def _reduce_begin(grads, names, tag):
    cc = lax.axis_index("c")
    axes = dict(SHARDED)
    packs = [_pack_rows([_shard(grads[n], axes[n], j) for n in names], BF16, 2 * 256) for j in range(N_CHIPS)]
    g = jnp.stack(packs).reshape(N_CHIPS, 2, -1, PACK_COLS)
    keep = lax.dynamic_index_in_dim(g, cc, axis=1, keepdims=False)
    give = lax.dynamic_index_in_dim(g, 1 - cc, axis=1, keepdims=False)
    got = sibling_swap(give, f"grad_swap_halves_{tag}")
    parts = add_to_bf16(keep, got, f"grad_add_sibling_{tag}")
    send_sems, recv_sems, parts_thru, land_thru, token = exchange_start(parts, f"grad_exchange_start_{tag}")
    shapes = [_shard(grads[n], axes[n], 0).shape for n in names]
    return (send_sems, recv_sems, parts_thru, land_thru, list(names), shapes), token[0, 0]


def _reduce_end(state, after, tag):
    send_sems, recv_sems, parts_thru, land_thru, names, shapes = state
    cc = lax.axis_index("c")
    chip = 2 * lax.axis_index("x") + lax.axis_index("y")
    parts, landed = exchange_wait(send_sems, recv_sems, parts_thru, land_thru, after, f"grad_exchange_wait_{tag}")
    own = lax.dynamic_index_in_dim(parts, chip, axis=0, keepdims=True)
    landed = lax.dynamic_update_slice(landed, own, (chip, 0, 0))
    mine = sum_leading(landed, f"grad_sum_chips_{tag}")
    theirs = sibling_swap(mine, f"grad_swap_sums_{tag}")
    lo = jnp.where(cc == 0, mine, theirs)
    hi = jnp.where(cc == 0, theirs, mine)
    return dict(zip(names, _unpack_rows(jnp.concatenate([lo, hi]), shapes)))


def _reduce_replicated(grads, loss_row):
    names = list(REPLICATED)
    flat = _pack([grads[n] for n in names] + [loss_row], F32, 256 * LANES).reshape(-1, LANES)
    got = all_gather8(flat, "gather_small_grads")
    tot = sum_leading(got, "sum_small_grads").reshape(-1)
    parts = _unpack(tot, [grads[n].shape for n in names] + [loss_row.shape])
    return dict(zip(names, parts[:-1])), parts[-1][0]


def _row(v):
    return v.reshape(1, -1).astype(F32)


def _pad_rows(wt, rows):
    return jnp.pad(wt.astype(F32), ((0, rows - wt.shape[0]), (0, 0)))


def _ffn_fwd(xin, g, wa, wb, cw, cb, wd, tag):
    cwa, cwb = _pad_rows(cw[:, :D_FF], 8), _pad_rows(cw[:, D_FF:], 8)
    xout, xn, hpa, hpb, act = ffn_fwd(xin, _row(g), wa, wb, cwa, cwb, _row(cb[:D_FF]), _row(cb[D_FF:]), wd, tag)
    return xout, (xin, xn, hpa, hpb, act)


def _ffn_bwd(dxout, saved, g, wa, wb, cw, cb, wd, tag, zero=0.0):
    xin, xn, hpa, hpb, act = saved
    d_wd = matmul(act, dxout, ta=True, name=f"{tag}_d_wdown")
    cwa, cwb = _pad_rows(cw[:, :D_FF], 8), _pad_rows(cw[:, D_FF:], 8)
    dxin, dpa, dpb, dwa, dwb, dba, dbb, dg = ffn_bwd(dxout, xin, _row(g), hpa, hpb, wa, wb, cwa, cwb,
                                                     _row(cb[:D_FF]) + zero, _row(cb[D_FF:]), wd, tag + "_bwd")
    d_wu = jnp.concatenate([matmul(xn, dpa, ta=True, name=f"{tag}_d_wup_a"),
                            matmul(xn, dpb, ta=True, name=f"{tag}_d_wup_b")], axis=1)
    taps = lambda t: t.transpose(1, 0, 2).reshape(8, -1)
    d_cw = jnp.concatenate([taps(dwa)[:FFN_K], taps(dwb)[:FFN_K]], axis=1)
    d_cb = jnp.concatenate([taps(dba)[0], taps(dbb)[0]])
    return dxin, dg[0], d_wu, d_cw, d_cb, d_wd


def _step(x, target, w, mom, var):
    s = x.shape[0]
    full, pending, zero = _gather_weights(w)
    cos, sin = rope_tables(s)

    w_in0 = full["l0_w_in"]
    w_in0p = jnp.concatenate([w_in0, jnp.zeros((D_MODEL, H0_W - w_in0.shape[1]), BF16)], axis=1)
    wq = full["l0_w_uq"].reshape(Q_LORA, N_HEADS, QK_NOPE + QK_ROPE)
    zq = lambda n: jnp.zeros((Q_LORA, N_HEADS, n), BF16)
    w_uqp = jnp.concatenate([wq[..., :QK_NOPE], zq(LANES - QK_NOPE), wq[..., QK_NOPE:], zq(LANES - QK_ROPE)],
                            axis=-1).reshape(Q_LORA, N_HEADS * HEAD_PAD)
    w_ukv = full["l0_w_ukv"]
    w_out = full["l0_w_out"]
    w_out_u = w_out[:CONV_WIDTH]
    wo = w_out[CONV_WIDTH:].reshape(N_HEADS, V_DIM, D_MODEL)
    w_out_a = jnp.concatenate([jnp.zeros_like(wo), wo], axis=1).reshape(N_HEADS * LANES, D_MODEL)
    conv_w = _pad_rows(full["l0_conv_w"], CONV_HALO)

    xn0 = rms_fwd(x, _row(w["l0_mix_norm"]) + zero, "l0_mix_rms")
    h0 = matmul(xn0, w_in0p, name="l0_in_proj")
    qn_g, kvn_g = _row(w["l0_q_norm"]), _row(w["l0_kv_norm"])
    u0, cq, ckv, kr = mixpre_fwd(h0, qn_g, kvn_g, cos, sin)
    cb, lg, lb = _row(w["l0_conv_b"]), _row(w["l0_conv_ln_g"]), _row(w["l0_conv_ln_b"])
    u = convln_fwd(u0, conv_w, cb, lg, lb)
    qraw = matmul(cq, w_uqp, name="l0_q_up")
    q = qrope_fwd(qraw, cos, sin)
    kv = matmul(ckv, w_ukv, out_dtype=BF16, name="l0_kv_up")
    o, lse = attn_fwd(q, kv, kr)
    x1 = matmul(u, w_out_u, res=x, name="l0_out_conv")
    x1 = matmul(o, w_out_a, res=x1, name="l0_out_attn")
    full.update(_finish_gather(pending, x1, w))
    w_up0a, w_up0b = full["l0_w_up"][:, :D_FF], full["l0_w_up"][:, D_FF:]
    w_up1a, w_up1b = full["l1_w_up"][:, :D_FF], full["l1_w_up"][:, D_FF:]

    x2, ffn0 = _ffn_fwd(x1, w["l0_ffn_norm"], w_up0a, w_up0b, full["l0_ffn_conv_w"], w["l0_ffn_conv_b"],
                        full["l0_w_down"], "l0_ffn")

    g_, p_, c_ = SSM_GROUPS, SSM_STATE, SSM_GROUP
    s5_in = (w["l1_log_dt"].reshape(g_, 1), w["l1_a_re"], w["l1_a_im"],
             w["l1_b_re"].reshape(g_, p_ * c_), w["l1_b_im"].reshape(g_, p_ * c_))
    lam_r, lam_i, bb_r, bb_i = s5_params_fwd(*s5_in)
    lam_rf, lam_if = lam_r.reshape(1, NS), lam_i.reshape(1, NS)

    def b_blocks(bb):
        t = bb.reshape(NQ, 8, p_, c_).transpose(0, 1, 3, 2)
        return _block_diag(t).astype(BF16)

    def c_blocks(cm):
        t = cm.reshape(NQ, 8, c_, p_).transpose(0, 1, 3, 2)
        return _block_diag(t).astype(BF16)

    bre, bim = b_blocks(bb_r), b_blocks(bb_i)
    cre, cim = c_blocks(w["l1_c_re"]), c_blocks(w["l1_c_im"])
    dskip = _row(w["l1_d"])
    xn2 = rms_fwd(x2, _row(w["l1_mix_norm"]), "l1_mix_rms")
    u1 = matmul(xn2, full["l1_w_in"], name="l1_in_proj")
    xs_r, xs_i, y1, yg = s5_scan_fwd(u1, lam_rf, lam_if, bre, bim, cre, cim, dskip)
    z = matmul(yg, full["l1_w_glu"], bias=_row(w["l1_b_glu"]), out_dtype=BF16, name="l1_glu_proj")
    x3 = glu_res_fwd(z, x2)

    x4, ffn1 = _ffn_fwd(x3, w["l1_ffn_norm"], w_up1a, w_up1b, full["l1_ffn_conv_w"], w["l1_ffn_conv_b"],
                        full["l1_w_down"], "l1_ffn")
    loss_part, dx4, dgf = loss_head(x4, _row(w["final_norm"]), target)

    gr = {"final_norm": dgf[0]}

    dx3, gr["l1_ffn_norm"], gr["l1_w_up"], gr["l1_ffn_conv_w"], gr["l1_ffn_conv_b"], gr["l1_w_down"] = _ffn_bwd(
        dx4, ffn1, w["l1_ffn_norm"], w_up1a, w_up1b, full["l1_ffn_conv_w"], w["l1_ffn_conv_b"], full["l1_w_down"],
        "l1_ffn")

    dz, dbga, dbgb = glu_bwd(z, dx3)
    gr["l1_b_glu"] = jnp.concatenate([dbga[0], dbgb[0]])
    dyg = matmul(dz, full["l1_w_glu"], tb=True, name="l1_d_yg")
    gr["l1_w_glu"] = matmul(yg, dz, ta=True, name="l1_d_wglu")
    a_r, a_i, dy1 = s5_scan_bwd(dyg, y1, lam_rf, lam_if, cre, cim)
    du1, dlr, dli, dbr, dbi, dcr, dci, dd = s5_grads(u1, dy1, xs_r, xs_i, a_r, a_i, bre, bim, dskip)
    gr["l1_d"] = dd[0]

    def b_unblock(d):
        return _block_diag_t(d, c_, p_).transpose(0, 1, 3, 2).reshape(g_, p_ * c_)

    def c_unblock(d):
        return _block_diag_t(d, p_, c_).transpose(0, 1, 3, 2).reshape(g_, c_, p_)

    gr["l1_c_re"], gr["l1_c_im"] = c_unblock(dcr), c_unblock(dci)
    dld, dar, dai, dbre, dbim = s5_params_bwd(*s5_in, dlr[0].reshape(g_, p_), dli[0].reshape(g_, p_),
                                              b_unblock(dbr), b_unblock(dbi))
    gr["l1_log_dt"], gr["l1_a_re"], gr["l1_a_im"] = dld.reshape(g_), dar, dai
    gr["l1_b_re"], gr["l1_b_im"] = dbre.reshape(g_, p_, c_), dbim.reshape(g_, p_, c_)
    dxn2 = matmul(du1, full["l1_w_in"], tb=True, name="l1_d_xn")
    gr["l1_w_in"] = matmul(xn2, du1, ta=True, name="l1_d_win")
    dx2, dg = rms_bwd(x2, _row(w["l1_mix_norm"]), dxn2, dx3, "l1_mix_rms_bwd")
    gr["l1_mix_norm"] = dg[0]
    red_a, zero_a = _reduce_begin(gr, ("l1_w_up", "l1_w_down", "l1_w_glu", "l1_w_in"), "a")

    dx1, gr["l0_ffn_norm"], gr["l0_w_up"], gr["l0_ffn_conv_w"], gr["l0_ffn_conv_b"], gr["l0_w_down"] = _ffn_bwd(
        dx2, ffn0, w["l0_ffn_norm"], w_up0a, w_up0b, full["l0_ffn_conv_w"], w["l0_ffn_conv_b"], full["l0_w_down"],
        "l0_ffn", zero_a)
    red_b, zero_b = _reduce_begin(gr, ("l0_w_up", "l0_w_down"), "b")

    du = matmul(dx1, w_out_u + zero_b.astype(BF16), tb=True, out_dtype=BF16, name="l0_d_u")
    do = matmul(dx1, w_out_a, tb=True, out_dtype=BF16, name="l0_d_o")
    d_wout_u = matmul(u, dx1, ta=True, name="l0_d_wout_u")
    d_wout_a = matmul(o, dx1, ta=True, name="l0_d_wout_a")
    gr["l0_w_out"] = jnp.concatenate(
        [d_wout_u, d_wout_a.reshape(N_HEADS, LANES, D_MODEL)[:, LANES - V_DIM:].reshape(N_HEADS * V_DIM, D_MODEL)])
    dq, dkv, dkr = attn_bwd(q, kv, kr, o, do, lse)
    dqraw = qrope_bwd(dq, cos, sin)
    dcq = matmul(dqraw, w_uqp, tb=True, name="l0_d_cq")
    d_wuqp = matmul(cq, dqraw, ta=True, name="l0_d_wuq").reshape(Q_LORA, N_HEADS, HEAD_PAD)
    gr["l0_w_uq"] = jnp.concatenate([d_wuqp[..., :QK_NOPE], d_wuqp[..., LANES:LANES + QK_ROPE]],
                                    axis=-1).reshape(Q_LORA, -1)
    dckv = matmul(dkv, w_ukv, tb=True, name="l0_d_ckv")
    gr["l0_w_ukv"] = matmul(ckv, dkv, ta=True, name="l0_d_wukv")
    du1c, dlg, dlb, dcb = convln_bwd1(u0, conv_w, cb, lg, lb, du)
    gr["l0_conv_ln_g"], gr["l0_conv_ln_b"], gr["l0_conv_b"] = dlg[0], dlb[0], dcb[0]
    du0, dcw = convln_bwd2(u0, conv_w, du1c)
    gr["l0_conv_w"] = dcw[:CONV_K]
    dh0, dqn, dkvn = mixpre_bwd(h0, qn_g, kvn_g, cos, sin, du0, dcq, dckv, dkr)
    gr["l0_q_norm"], gr["l0_kv_norm"] = dqn[0], dkvn[0]
    dxn0 = matmul(dh0, w_in0p, tb=True, name="l0_d_xn")
    gr["l0_w_in"] = matmul(xn0, dh0, ta=True, name="l0_d_win")[:, :w_in0.shape[1]]
    grad_x, dg = rms_bwd(x, _row(w["l0_mix_norm"]), dxn0, dx1, "l0_mix_rms_bwd")
    gr["l0_mix_norm"] = dg[0]

    rest = [n for n, _ in SHARDED if n not in red_a[-2] + red_b[-2]]
    red_c, _ = _reduce_begin(gr, rest, "c")
    g_sh = {**_reduce_end(red_a, grad_x, "a"), **_reduce_end(red_b, grad_x, "b"), **_reduce_end(red_c, grad_x, "c")}
    g_rep, loss = _reduce_replicated(gr, loss_part[0])
    grad, delta, new_m, new_v = {}, {}, {}, {}
    for n, _ in SHARDED:
        shp = w[n].shape
        two_d = (lambda t: t.reshape(shp[0], -1))
        grad[n] = g_sh[n]
        delta[n], new_m[n], new_v[n] = adamw(two_d(w[n]), two_d(g_sh[n]), two_d(mom[n]), two_d(var[n]), f"adamw_{n}")
    names = list(REPLICATED)
    pk = lambda d: _pack([d[n] for n in names], F32, 256 * LANES).reshape(-1, LANES)
    dl, nm, nv = adamw(pk(w), pk(g_rep), pk(mom), pk(var), "adamw_small")
    shapes = [w[n].shape for n in names]
    for n, d_, m_, v_ in zip(names, _unpack(dl.reshape(-1), shapes), _unpack(nm.reshape(-1), shapes),
                             _unpack(nv.reshape(-1), shapes)):
        grad[n], delta[n], new_m[n], new_v[n] = g_rep[n], d_, m_, v_

    order = ["l0_mix_norm", "l0_w_in", "l0_conv_w", "l0_conv_b", "l0_conv_ln_g", "l0_conv_ln_b", "l0_q_norm",
             "l0_kv_norm", "l0_w_uq", "l0_w_ukv", "l0_w_out", "l0_ffn_norm", "l0_w_up", "l0_ffn_conv_w",
             "l0_ffn_conv_b", "l0_w_down", "l1_mix_norm", "l1_w_in", "l1_log_dt", "l1_a_re", "l1_a_im", "l1_b_re",
             "l1_b_im", "l1_c_re", "l1_c_im", "l1_d", "l1_w_glu", "l1_b_glu", "l1_ffn_norm", "l1_w_up",
             "l1_ffn_conv_w", "l1_ffn_conv_b", "l1_w_down", "final_norm"]
    return (loss, grad_x[None], *[grad[n] for n in order], *[delta[n] for n in order],
            *[new_m[n] for n in order], *[new_v[n] for n in order])
```

```python
import functools
import math

import jax
import jax.numpy as jnp
import numpy as np
from jax import lax
from jax.experimental import pallas as pl
from jax.experimental.pallas import tpu as pltpu

F32 = jnp.float32
BF16 = jnp.bfloat16
MESH = pl.DeviceIdType.MESH

D_MODEL = 1024
EPS = 1e-6
LN_EPS = 1e-5
CONV_WIDTH = 512
CONV_K = 31
N_HEADS = 8
QK_NOPE = 64
QK_ROPE = 32
V_DIM = 64
Q_LORA = 256
KV_LORA = 128
ROPE_BASE = 10000.0
ATT_SCALE = (QK_NOPE + QK_ROPE) ** -0.5
SSM_WIDTH = 512
SSM_GROUP = 16
SSM_GROUPS = 32
SSM_STATE = 64
D_FF = 2816
FFN_K = 3
ADAM_LR = 0.001
ADAM_B1 = 0.9
ADAM_B2 = 0.999
ADAM_EPS = 1e-08
ADAM_WD = 0.01
ADAM_STEP = 10

N_CHIPS = 4
LANES = 128
HEAD_PAD = 256
CONV_HALO = 32
FFN_HALO = 16
VMEM_LIMIT = 56 * 1024 * 1024

ROW_TILE = 512
FFN_ROW_TILE = 1024
FFN_COL_TILE = 256
FFN_ROW_CHUNK = 64
CONV_ROW_CHUNK = 32
FFN_BWD_PARTS = 4
ATT_TILE = 1024
SCAN_TILE = 256
SCAN_UNROLL = 4


def _cparams(*sem):
    return pltpu.CompilerParams(dimension_semantics=tuple(sem), vmem_limit_bytes=VMEM_LIMIT)


def _pick(n, cands):
    for c in cands:
        if n % c == 0:
            return c
    return n


def matmul(a, b, *, ta=False, tb=False, res=None, bias=None, out_dtype=None, name):
    if out_dtype is None:
        out_dtype = BF16 if ta else F32
    if ta:
        kdim, m = a.shape
    else:
        m, kdim = a.shape
    if tb:
        n, k2 = b.shape
    else:
        k2, n = b.shape
    assert kdim == k2, (a.shape, b.shape, ta, tb)
    tn = _pick(n, (1408, 1024, 768, 512, 384, 256, 128))
    if ta:
        tm = _pick(m, (1408, 1024, 512, 256, 128))
        tk = _pick(kdim, (1024, 512, 256, 128))
    else:
        tm = _pick(m, (1024, 512, 256, 128))
        tk = kdim
        if kdim > 1024:
            tn = _pick(n, (512, 256, 128))
        if tm * tn > 1024 * 1024 and out_dtype == F32:
            tm = _pick(m, (512, 256, 128))
    nk = kdim // tk
    has_res, has_bias = res is not None, bias is not None
    dims = (((0,) if ta else (1,), (1,) if tb else (0,)), ((), ()))

    def body(*refs):
        a_ref, b_ref = refs[0], refs[1]
        pos = 2
        res_ref = bias_ref = None
        if has_res:
            res_ref = refs[pos]
            pos += 1
        if has_bias:
            bias_ref = refs[pos]
            pos += 1
        o_ref = refs[pos]

        def finish(r):
            if has_bias:
                r = r + bias_ref[...]
            if has_res:
                r = r + res_ref[...].astype(F32)
            o_ref[...] = r.astype(o_ref.dtype)

        prod = lax.dot_general(a_ref[...].astype(BF16), b_ref[...].astype(BF16), dims, preferred_element_type=F32)
        if nk == 1:
            finish(prod)
            return
        acc_ref = refs[pos + 1]
        k = pl.program_id(2)

        @pl.when(k == 0)
        def _():
            acc_ref[...] = prod

        @pl.when(k > 0)
        def _():
            acc_ref[...] += prod

        @pl.when(k == nk - 1)
        def _():
            finish(acc_ref[...])

    a_spec = pl.BlockSpec((tk, tm), lambda i, j, k: (k, i)) if ta else pl.BlockSpec((tm, tk), lambda i, j, k: (i, k))
    b_spec = pl.BlockSpec((tn, tk), lambda i, j, k: (j, k)) if tb else pl.BlockSpec((tk, tn), lambda i, j, k: (k, j))
    in_specs = [a_spec, b_spec]
    args = [a, b]
    if has_res:
        in_specs.append(pl.BlockSpec((tm, tn), lambda i, j, k: (i, j)))
        args.append(res)
    if has_bias:
        in_specs.append(pl.BlockSpec((1, tn), lambda i, j, k: (0, j)))
        args.append(bias)
    return pl.pallas_call(
        body, name=name, grid=(m // tm, n // tn, nk),
        in_specs=in_specs, out_specs=pl.BlockSpec((tm, tn), lambda i, j, k: (i, j)),
        out_shape=jax.ShapeDtypeStruct((m, n), out_dtype),
        scratch_shapes=[pltpu.VMEM((tm, tn), F32)] if nk > 1 else [],
        compiler_params=_cparams("parallel", "parallel", "arbitrary"),
    )(*args)


def rowcall(body, *, rows, ts, ins, outs, name, scratch=()):
    nt = rows // ts
    in_specs, args = [], []
    for arr, kind in ins:
        if kind == "row":
            in_specs.append(pl.BlockSpec((ts, arr.shape[1]), lambda i: (i, 0)))
        elif kind == "rev":
            in_specs.append(pl.BlockSpec((ts, arr.shape[1]), lambda i: (nt - 1 - i, 0)))
        elif kind == "full":
            nd = arr.ndim
            in_specs.append(pl.BlockSpec(arr.shape, lambda i, nd=nd: (0,) * nd))
        elif kind.startswith("prev:"):
            h = int(kind[5:])
            r = ts // h
            in_specs.append(pl.BlockSpec((h, arr.shape[1]), lambda i, r=r: (jnp.maximum(i * r - 1, 0), 0)))
        elif kind.startswith("next:"):
            h = int(kind[5:])
            r = ts // h
            last = rows // h - 1
            in_specs.append(pl.BlockSpec((h, arr.shape[1]), lambda i, r=r, last=last: (jnp.minimum((i + 1) * r, last), 0)))
        elif kind.startswith("revprev:"):
            h = int(kind[8:])
            r = ts // h
            in_specs.append(pl.BlockSpec((h, arr.shape[1]), lambda i, r=r: (jnp.maximum((nt - 1 - i) * r - 1, 0), 0)))
        else:
            raise ValueError(kind)
        args.append(arr)
    out_specs, out_shapes = [], []
    for shape, dtype, kind in outs:
        if kind == "row":
            out_specs.append(pl.BlockSpec((ts, shape[1]), lambda i: (i, 0)))
        elif kind == "rev":
            out_specs.append(pl.BlockSpec((ts, shape[1]), lambda i: (nt - 1 - i, 0)))
        else:
            nd = len(shape)
            out_specs.append(pl.BlockSpec(tuple(shape), lambda i, nd=nd: (0,) * nd))
        out_shapes.append(jax.ShapeDtypeStruct(tuple(shape), dtype))
    return pl.pallas_call(
        functools.partial(body, nt), name=name, grid=(nt,),
        in_specs=in_specs, out_specs=tuple(out_specs), out_shape=tuple(out_shapes),
        scratch_shapes=list(scratch),
        compiler_params=_cparams("arbitrary"),
    )(*args)


def _rms(x, g):
    return x * lax.rsqrt(jnp.mean(x * x, axis=-1, keepdims=True) + EPS) * g


def _layer_norm(x, g, b):
    mu = jnp.mean(x, axis=-1, keepdims=True)
    xc = x - mu
    var = jnp.mean(xc * xc, axis=-1, keepdims=True)
    return xc * lax.rsqrt(var + LN_EPS) * g + b


def _sigmoid(x):
    return 1.0 / (1.0 + jnp.exp(-x))


def _silu(x):
    return x * _sigmoid(x)


def _gelu(x):
    return 0.5 * x * (1.0 + jnp.tanh(math.sqrt(2.0 / math.pi) * (x + 0.044715 * (x * x * x))))


def _acc(ref, i, val):
    s = jnp.sum(val, axis=0, keepdims=True)

    @pl.when(i == 0)
    def _():
        ref[...] = jnp.zeros_like(ref)

    ref[...] += jnp.broadcast_to(s, ref.shape)


def rms_fwd(x, g, name):
    s, c = x.shape

    def body(nt, x_ref, g_ref, o_ref):
        o_ref[...] = _rms(x_ref[...], g_ref[...]).astype(BF16)

    return rowcall(body, rows=s, ts=min(ROW_TILE, s), ins=[(x, "row"), (g, "full")],
                   outs=[((s, c), BF16, "row")], name=name)[0]


def rms_bwd(x, g, dxn, dres, name):
    s, c = x.shape

    def body(nt, x_ref, g_ref, d_ref, r_ref, dx_ref, dg_ref):
        i = pl.program_id(0)
        _, vjp = jax.vjp(_rms, x_ref[...], g_ref[...])
        dx, dg = vjp(d_ref[...].astype(F32))
        dx_ref[...] = dx + r_ref[...]
        _acc(dg_ref, i, dg)

    return rowcall(body, rows=s, ts=min(ROW_TILE, s),
                   ins=[(x, "row"), (g, "full"), (dxn, "row"), (dres, "row")],
                   outs=[((s, c), F32, "row"), ((8, c), F32, "acc")], name=name)


def _partner(t):
    lane = lax.broadcasted_iota(jnp.int32, t.shape, 1)
    half = QK_ROPE // 2
    return jnp.where(lane % QK_ROPE < half, pltpu.roll(t, LANES - half, 1), pltpu.roll(t, half, 1))


def _rope(t, cos, sin):
    return t * cos + _partner(t) * sin


def _rope_t(d, cos, sin):
    return d * cos + _partner(d * sin)


def rope_tables(s):
    half = QK_ROPE // 2
    inv = ROPE_BASE ** (-jnp.arange(half, dtype=F32) / half)
    ang = jnp.arange(s).astype(F32)[:, None] * inv[None, :]
    cos, sin = jnp.cos(ang), jnp.sin(ang)
    z = jnp.zeros((s, LANES - QK_ROPE), F32)
    return jnp.concatenate([cos, cos, z], axis=1), jnp.concatenate([-sin, sin, z], axis=1)


H0_A, H0_G, H0_Q, H0_KV, H0_KR, H0_W = 0, 512, 1024, 1280, 1408, 1536


def _mixpre_fn(a, g, q, kv, qn, kvn):
    return a * _sigmoid(g), _rms(q, qn), _rms(kv, kvn)


def _h0_parts(h_ref):
    return (h_ref[:, H0_A:H0_G], h_ref[:, H0_G:H0_Q], h_ref[:, H0_Q:H0_KV], h_ref[:, H0_KV:H0_KR])


def mixpre_fwd(h0, qn, kvn, cos, sin):
    s = h0.shape[0]

    def body(nt, h_ref, qn_ref, kvn_ref, cos_ref, sin_ref, u0_ref, cq_ref, ckv_ref, kr_ref):
        u0, cq, ckv = _mixpre_fn(*_h0_parts(h_ref), qn_ref[...], kvn_ref[...])
        u0_ref[...] = u0
        cq_ref[...] = cq.astype(BF16)
        ckv_ref[...] = ckv.astype(BF16)
        kr_ref[...] = _rope(h_ref[:, H0_KR:H0_W], cos_ref[...], sin_ref[...]).astype(BF16)

    return rowcall(body, rows=s, ts=min(ROW_TILE, s),
                   ins=[(h0, "row"), (qn, "full"), (kvn, "full"), (cos, "row"), (sin, "row")],
                   outs=[((s, CONV_WIDTH), F32, "row"), ((s, Q_LORA), BF16, "row"),
                         ((s, KV_LORA), BF16, "row"), ((s, LANES), BF16, "row")], name="mixpre_fwd")


def mixpre_bwd(h0, qn, kvn, cos, sin, du0, dcq, dckv, dkr):
    s = h0.shape[0]

    def body(nt, h_ref, qn_ref, kvn_ref, cos_ref, sin_ref, du0_ref, dcq_ref, dckv_ref, dkr_ref,
             dh_ref, dqn_ref, dkvn_ref):
        i = pl.program_id(0)
        _, vjp = jax.vjp(_mixpre_fn, *_h0_parts(h_ref), qn_ref[...], kvn_ref[...])
        da, dg, dq, dkv, dqn, dkvn = vjp((du0_ref[...], dcq_ref[...], dckv_ref[...]))
        dh_ref[:, H0_A:H0_G] = da.astype(BF16)
        dh_ref[:, H0_G:H0_Q] = dg.astype(BF16)
        dh_ref[:, H0_Q:H0_KV] = dq.astype(BF16)
        dh_ref[:, H0_KV:H0_KR] = dkv.astype(BF16)
        dkr = dkr_ref[:, :LANES]
        for h in range(1, N_HEADS):
            dkr = dkr + dkr_ref[:, h * LANES:(h + 1) * LANES]
        dh_ref[:, H0_KR:H0_W] = _rope_t(dkr, cos_ref[...], sin_ref[...]).astype(BF16)
        _acc(dqn_ref, i, dqn)
        _acc(dkvn_ref, i, dkvn)

    return rowcall(body, rows=s, ts=min(ROW_TILE, s),
                   ins=[(h0, "row"), (qn, "full"), (kvn, "full"), (cos, "row"), (sin, "row"),
                        (du0, "row"), (dcq, "row"), (dckv, "row"), (dkr, "row")],
                   outs=[((s, H0_W), BF16, "row"), ((8, Q_LORA), F32, "acc"), ((8, KV_LORA), F32, "acc")],
                   name="mixpre_bwd")


def _conv_taps(ext_ref, w_ref, ts, first, ntaps, flip=False):
    acc = None
    for k in range(ntaps):
        term = w_ref[pl.ds(ntaps - 1 - k if flip else k, 1), :] * ext_ref[pl.ds(first + k, ts), :]
        acc = term if acc is None else acc + term
    return acc


def _ln_silu(u1, g, b):
    return _silu(_layer_norm(u1, g, b))


SUBLANES = 8


def _fill_shifted(sh_ref, parts, rows):
    pos = 0
    for p in parts:
        sh_ref[0, pl.ds(pos, p.shape[0]), :] = p
        pos += p.shape[0]
    sh_ref[0, pl.ds(rows, SUBLANES), :] = jnp.zeros((SUBLANES, sh_ref.shape[2]), F32)
    for r in range(1, SUBLANES):
        sh_ref[r, pl.ds(0, rows), :] = sh_ref[0, pl.ds(r, rows), :]


def _window(sh_ref, off, n):
    r = off % SUBLANES
    return sh_ref[r, pl.ds(off - r, n), :]


def _taps_aligned(sh_ref, w_ref, n, first, ntaps, flip=False):
    acc = None
    for k in range(ntaps):
        term = w_ref[pl.ds(ntaps - 1 - k if flip else k, 1), :] * _window(sh_ref, first + k, n)
        acc = term if acc is None else acc + term
    return acc


def _conv_scratch(ts, c):
    return pltpu.VMEM((SUBLANES, ts + CONV_HALO + SUBLANES, c), F32)


def convln_fwd(u0, w, b, lg, lb):
    s, c = u0.shape
    ts = min(ROW_TILE, s)
    rc = min(CONV_ROW_CHUNK, ts)
    first = CONV_HALO - (CONV_K - 1)

    def body(nt, cur_ref, prev_ref, w_ref, b_ref, lg_ref, lb_ref, o_ref, sh_ref):
        i = pl.program_id(0)
        _fill_shifted(sh_ref, [jnp.where(i > 0, prev_ref[...], 0.0), cur_ref[...]], ts + CONV_HALO)
        for r0 in range(0, ts, rc):
            u1 = _taps_aligned(sh_ref, w_ref, rc, first + r0, CONV_K) + b_ref[...]
            o_ref[pl.ds(r0, rc), :] = _ln_silu(u1, lg_ref[...], lb_ref[...]).astype(BF16)

    return rowcall(body, rows=s, ts=ts,
                   ins=[(u0, "row"), (u0, f"prev:{CONV_HALO}"), (w, "full"), (b, "full"), (lg, "full"), (lb, "full")],
                   outs=[((s, c), BF16, "row")], name="convln_fwd", scratch=[_conv_scratch(ts, c)])[0]


def convln_bwd1(u0, w, b, lg, lb, du):
    s, c = u0.shape
    ts = min(ROW_TILE, s)
    rc = min(CONV_ROW_CHUNK, ts)
    first = CONV_HALO - (CONV_K - 1)

    def body(nt, cur_ref, prev_ref, w_ref, b_ref, lg_ref, lb_ref, du_ref, du1_ref, dlg_ref, dlb_ref, dcb_ref, sh_ref):
        i = pl.program_id(0)
        _fill_shifted(sh_ref, [jnp.where(i > 0, prev_ref[...], 0.0), cur_ref[...]], ts + CONV_HALO)
        sums = [jnp.zeros((1, c), F32)] * 3
        for r0 in range(0, ts, rc):
            u1 = _taps_aligned(sh_ref, w_ref, rc, first + r0, CONV_K) + b_ref[...]
            _, vjp = jax.vjp(_ln_silu, u1, lg_ref[...], lb_ref[...])
            du1, dlg, dlb = vjp(du_ref[pl.ds(r0, rc), :].astype(F32))
            du1_ref[pl.ds(r0, rc), :] = du1
            parts = (dlg, dlb, jnp.sum(du1, axis=0, keepdims=True))
            sums = [a + jnp.sum(p, axis=0, keepdims=True) for a, p in zip(sums, parts)]
        _acc(dlg_ref, i, sums[0])
        _acc(dlb_ref, i, sums[1])
        _acc(dcb_ref, i, sums[2])

    return rowcall(body, rows=s, ts=ts,
                   ins=[(u0, "row"), (u0, f"prev:{CONV_HALO}"), (w, "full"), (b, "full"), (lg, "full"), (lb, "full"),
                        (du, "row")],
                   outs=[((s, c), F32, "row"), ((8, c), F32, "acc"), ((8, c), F32, "acc"), ((8, c), F32, "acc")],
                   name="convln_bwd1", scratch=[_conv_scratch(ts, c)])


def convln_bwd2(u0, w, du1):
    s, c = u0.shape
    ts = min(ROW_TILE, s)
    rc = min(CONV_ROW_CHUNK, ts)
    first = CONV_HALO - (CONV_K - 1)

    def body(nt, cur_ref, prev_ref, d_ref, dnext_ref, w_ref, du0_ref, dw_ref, sh_ref, dsh_ref):
        i = pl.program_id(0)
        _fill_shifted(sh_ref, [jnp.where(i > 0, prev_ref[...], 0.0), cur_ref[...]], ts + CONV_HALO)
        _fill_shifted(dsh_ref, [d_ref[...], jnp.where(i < nt - 1, dnext_ref[...], 0.0)], ts + CONV_HALO)
        for r0 in range(0, ts, rc):
            du0_ref[pl.ds(r0, rc), :] = _taps_aligned(dsh_ref, w_ref, rc, r0, CONV_K, flip=True)

        @pl.when(i == 0)
        def _():
            dw_ref[...] = jnp.zeros_like(dw_ref)

        for k in range(CONV_K):
            part = jnp.zeros((SUBLANES, c), F32)
            for r0 in range(0, ts, rc):
                prod = d_ref[pl.ds(r0, rc), :] * _window(sh_ref, first + k + r0, rc)
                for a in range(0, rc, SUBLANES):
                    part = part + prod[a:a + SUBLANES]
            dw_ref[pl.ds(k, 1), :] += jnp.sum(part, axis=0, keepdims=True)

    return rowcall(body, rows=s, ts=ts,
                   ins=[(u0, "row"), (u0, f"prev:{CONV_HALO}"), (du1, "row"), (du1, f"next:{CONV_HALO}"), (w, "full")],
                   outs=[((s, c), F32, "row"), ((CONV_HALO, c), F32, "acc")], name="convln_bwd2",
                   scratch=[_conv_scratch(ts, c), _conv_scratch(ts, c)])


def q_up_rope(cq, w_uqp, cos, sin):
    s, kdim = cq.shape
    n = w_uqp.shape[1]
    ts = min(ROW_TILE, s)

    def body(nt, cq_ref, w_ref, cos_ref, sin_ref, o_ref):
        cos_v, sin_v = cos_ref[...] * ATT_SCALE, sin_ref[...] * ATT_SCALE
        for h in range(N_HEADS):
            cols = slice(h * HEAD_PAD, (h + 1) * HEAD_PAD)
            qh = jnp.dot(cq_ref[...], w_ref[:, cols], preferred_element_type=F32)
            o_ref[:, h * HEAD_PAD:h * HEAD_PAD + LANES] = (qh[:, :LANES] * ATT_SCALE).astype(BF16)
            o_ref[:, h * HEAD_PAD + LANES:(h + 1) * HEAD_PAD] = _rope(qh[:, LANES:], cos_v, sin_v).astype(BF16)

    return rowcall(body, rows=s, ts=ts, ins=[(cq, "row"), (w_uqp, "full"), (cos, "row"), (sin, "row")],
                   outs=[((s, n), BF16, "row")], name="l0_q_up_rope")[0]


def qrope_bwd(dq, cos, sin):
    s = dq.shape[0]

    def body(nt, d_ref, cos_ref, sin_ref, o_ref):
        cos_v, sin_v = cos_ref[...] * ATT_SCALE, sin_ref[...] * ATT_SCALE
        for h in range(N_HEADS):
            nope = d_ref[:, h * HEAD_PAD:h * HEAD_PAD + LANES] * ATT_SCALE
            o_ref[:, h * HEAD_PAD:h * HEAD_PAD + LANES] = nope.astype(BF16)
            r = d_ref[:, h * HEAD_PAD + LANES:(h + 1) * HEAD_PAD].astype(F32)
            o_ref[:, h * HEAD_PAD + LANES:(h + 1) * HEAD_PAD] = _rope_t(r, cos_v, sin_v).astype(BF16)

    return rowcall(body, rows=s, ts=min(ROW_TILE, s), ins=[(dq, "row"), (cos, "row"), (sin, "row")],
                   outs=[((s, N_HEADS * HEAD_PAD), BF16, "row")], name="qrope_bwd")[0]


_NT = (((1,), (1,)), ((), ()))
_TN = (((0,), (0,)), ((), ()))


def _scores(q, kvr, diagonal):
    s = lax.dot_general(q, kvr, _NT, preferred_element_type=F32)
    if not diagonal:
        return s
    row = lax.broadcasted_iota(jnp.int32, s.shape, 0)
    col = lax.broadcasted_iota(jnp.int32, s.shape, 1)
    return jnp.where(col <= row, s, -jnp.inf)


def _on_causal_pairs(pair, k_blk, fn):
    @pl.when(k_blk < 2 * pair)
    def _():
        fn(0, False)
        fn(1, False)

    @pl.when(k_blk == 2 * pair)
    def _():
        fn(0, True)
        fn(1, False)

    @pl.when(k_blk == 2 * pair + 1)
    def _():
        fn(1, True)


def attn_fwd(q, kv, kr):
    s = q.shape[0]
    t = min(ATT_TILE, s // 2)
    n = s // t
    np_ = n // 2

    def body(q_ref, kv_ref, kr_ref, o_ref, lse_ref, m_ref, l_ref, acc_ref):
        i, j = pl.program_id(1), pl.program_id(2)

        @pl.when(j == 0)
        def _():
            m_ref[...] = jnp.full_like(m_ref, -jnp.inf)
            l_ref[...] = jnp.zeros_like(l_ref)
            acc_ref[...] = jnp.zeros_like(acc_ref)

        def block(sub, diagonal):
            kvv = kv_ref[...]
            kvr = jnp.concatenate([kvv, kr_ref[...]], axis=1)
            sc = _scores(q_ref[pl.ds(sub * t, t), :], kvr, diagonal)
            m_prev = m_ref[sub]
            m_new = jnp.maximum(m_prev, jnp.max(sc, axis=-1, keepdims=True))
            alpha = jnp.exp(m_prev - m_new)
            p = jnp.exp(sc - m_new)
            l_ref[sub] = alpha * l_ref[sub] + jnp.sum(p, axis=-1, keepdims=True)
            acc_ref[sub] = alpha * acc_ref[sub] + jnp.dot(p.astype(BF16), kvv, preferred_element_type=F32)
            m_ref[sub] = m_new

        _on_causal_pairs(i, j, block)

        @pl.when(j == 2 * i + 1)
        def _():
            for sub in range(2):
                l = l_ref[sub]
                o_ref[pl.ds(sub * t, t), :] = (acc_ref[sub] / l).astype(BF16)
                lse_ref[pl.ds(sub * t, t), :] = jnp.broadcast_to(m_ref[sub] + jnp.log(l), (t, LANES))

    kj = lambda h, i, j: (jnp.minimum(j, 2 * i + 1), h)
    return pl.pallas_call(
        body, name="attn_fwd", grid=(N_HEADS, np_, n),
        in_specs=[pl.BlockSpec((2 * t, HEAD_PAD), lambda h, i, j: (i, h)),
                  pl.BlockSpec((t, LANES), kj),
                  pl.BlockSpec((t, LANES), lambda h, i, j: (jnp.minimum(j, 2 * i + 1), 0))],
        out_specs=(pl.BlockSpec((2 * t, LANES), lambda h, i, j: (i, h)),
                   pl.BlockSpec((2 * t, LANES), lambda h, i, j: (i, h))),
        out_shape=(jax.ShapeDtypeStruct((s, N_HEADS * LANES), BF16),
                   jax.ShapeDtypeStruct((s, N_HEADS * LANES), F32)),
        scratch_shapes=[pltpu.VMEM((2, t, 1), F32), pltpu.VMEM((2, t, 1), F32), pltpu.VMEM((2, t, LANES), F32)],
        compiler_params=_cparams("parallel", "parallel", "arbitrary"),
    )(q, kv, kr)


def attn_bwd(q, kv, kr, o, do, lse):
    s = q.shape[0]
    t = min(ATT_TILE, s // 2)
    n = s // t
    np_ = n // 2

    def body(q_ref, kv_ref, kr_ref, o_ref, do_ref, lse_ref, dq_ref, dkv_ref, dkr_ref):
        j, i = pl.program_id(1), pl.program_id(2)

        @pl.when((i == 0) & (j == 0))
        def _():
            dq_ref[...] = jnp.zeros_like(dq_ref)

        @pl.when(i == 0)
        def _():
            dkv_ref[...] = jnp.zeros_like(dkv_ref)
            dkr_ref[...] = jnp.zeros_like(dkr_ref)

        def block(sub, diagonal):
            sl = pl.ds(sub * t, t)
            qv, dov, kvv = q_ref[sl, :], do_ref[sl, :], kv_ref[...]
            kvr = jnp.concatenate([kvv, kr_ref[...]], axis=1)
            p = jnp.exp(_scores(qv, kvr, diagonal) - lse_ref[sl, :1])
            dp = lax.dot_general(dov, kvv, _NT, preferred_element_type=F32)
            delta = jnp.sum(dov.astype(F32) * o_ref[sl, :].astype(F32), axis=-1, keepdims=True)
            ds = (p * (dp - delta)).astype(BF16)
            dk = lax.dot_general(ds, qv, _TN, preferred_element_type=F32)
            dkv_ref[...] += lax.dot_general(p.astype(BF16), dov, _TN, preferred_element_type=F32) + dk[:, :LANES]
            dkr_ref[...] += dk[:, LANES:]
            rows = pl.ds(pl.multiple_of((2 * i + sub) * t, t), t)
            dq_ref[rows, :] += jnp.dot(ds, kvr, preferred_element_type=F32)

        _on_causal_pairs(i, j, block)

    qi = lambda h, j, i: (jnp.maximum(i, lax.div(j, 2)), h)
    kj = lambda h, j, i: (j, h)
    return pl.pallas_call(
        body, name="attn_bwd", grid=(N_HEADS, n, np_),
        in_specs=[pl.BlockSpec((2 * t, HEAD_PAD), qi), pl.BlockSpec((t, LANES), kj),
                  pl.BlockSpec((t, LANES), lambda h, j, i: (j, 0)),
                  pl.BlockSpec((2 * t, LANES), qi), pl.BlockSpec((2 * t, LANES), qi), pl.BlockSpec((2 * t, LANES), qi)],
        out_specs=(pl.BlockSpec((s, HEAD_PAD), lambda h, j, i: (0, h)),
                   pl.BlockSpec((t, LANES), kj), pl.BlockSpec((t, LANES), kj)),
        out_shape=(jax.ShapeDtypeStruct((s, N_HEADS * HEAD_PAD), F32),
                   jax.ShapeDtypeStruct((s, N_HEADS * LANES), F32), jax.ShapeDtypeStruct((s, N_HEADS * LANES), F32)),
        compiler_params=_cparams("parallel", "arbitrary", "arbitrary"),
    )(q, kv, kr, o, do, lse)


def ffn_fwd(x, g, wa, wb, cwa, cwb, ba, bb, wd, name):
    s, d = x.shape
    f = wa.shape[1]
    ts, tf = min(FFN_ROW_TILE, s), FFN_COL_TILE
    hal = FFN_HALO
    nj = f // tf
    first = hal - (FFN_K - 1)
    rc = min(FFN_ROW_CHUNK, ts)

    def body(x_ref, xp_ref, g_ref, wa_ref, wb_ref, cwa_ref, cwb_ref, ba_ref, bb_ref, wd_ref,
             xo_ref, xn_ref, hpa_ref, hpb_ref, act_ref, xe_ref, ea_ref, eb_ref):
        i, j = pl.program_id(0), pl.program_id(1)

        @pl.when(j == 0)
        def _():
            xn = _rms(x_ref[...], g_ref[...]).astype(BF16)
            xn_ref[...] = xn
            xe_ref[pl.ds(hal, ts), :] = xn
            xe_ref[pl.ds(0, hal), :] = jnp.where(i > 0, _rms(xp_ref[...], g_ref[...]), 0.0).astype(BF16)
            xo_ref[...] = x_ref[...]

        halves = ((0, ts // 2), (ts // 2, ts))
        for lo, hi in halves:
            e0, e1 = (0 if lo == 0 else hal + lo), hal + hi
            xe = xe_ref[pl.ds(e0, e1 - e0), :]
            ea_ref[pl.ds(e0, e1 - e0), :] = jnp.dot(xe, wa_ref[...], preferred_element_type=F32)
            eb_ref[pl.ds(e0, e1 - e0), :] = jnp.dot(xe, wb_ref[...], preferred_element_type=F32)
        for lo, hi in halves:
            hpa_ref[pl.ds(lo, hi - lo), :] = ea_ref[pl.ds(hal + lo, hi - lo), :].astype(BF16)
            hpb_ref[pl.ds(lo, hi - lo), :] = eb_ref[pl.ds(hal + lo, hi - lo), :].astype(BF16)
            for r0 in range(lo, hi, rc):
                ha = _conv_taps(ea_ref, cwa_ref, rc, first + r0, FFN_K) + ba_ref[...]
                hb = _conv_taps(eb_ref, cwb_ref, rc, first + r0, FFN_K) + bb_ref[...]
                act_ref[pl.ds(r0, rc), :] = (_silu(ha) * hb).astype(BF16)
            xo_ref[pl.ds(lo, hi - lo), :] += jnp.dot(act_ref[pl.ds(lo, hi - lo), :], wd_ref[...],
                                                     preferred_element_type=F32)

    r = ts // hal
    row = pl.BlockSpec((ts, d), lambda i, j: (i, 0))
    prev = pl.BlockSpec((hal, d), lambda i, j: (jnp.maximum(i * r - 1, 0), 0))
    gsp = pl.BlockSpec((1, d), lambda i, j: (0, 0))
    wup = pl.BlockSpec((d, tf), lambda i, j: (0, j))
    cwsp = pl.BlockSpec((8, tf), lambda i, j: (0, j))
    bsp = pl.BlockSpec((1, tf), lambda i, j: (0, j))
    wdn = pl.BlockSpec((tf, d), lambda i, j: (j, 0))
    hid = pl.BlockSpec((ts, tf), lambda i, j: (i, j))
    return pl.pallas_call(
        body, name=name, grid=(s // ts, nj),
        in_specs=[row, prev, gsp, wup, wup, cwsp, cwsp, bsp, bsp, wdn],
        out_specs=(row, row, hid, hid, hid),
        out_shape=(jax.ShapeDtypeStruct((s, d), F32), jax.ShapeDtypeStruct((s, d), BF16),
                   jax.ShapeDtypeStruct((s, f), BF16), jax.ShapeDtypeStruct((s, f), BF16),
                   jax.ShapeDtypeStruct((s, f), BF16)),
        scratch_shapes=[pltpu.VMEM((ts + hal, d), BF16), pltpu.VMEM((ts + hal, tf), F32),
                        pltpu.VMEM((ts + hal, tf), F32)],
        compiler_params=_cparams("parallel", "arbitrary"),
    )(x, x, g, wa, wb, cwa, cwb, ba, bb, wd)


def ffn_bwd(dy, x, g, hpa, hpb, wa, wb, cwa, cwb, ba, bb, wd, name):
    s, d = dy.shape
    f = hpa.shape[1]
    ts, tf = min(FFN_ROW_TILE, s), FFN_COL_TILE
    hal = FFN_HALO
    nt, nj = s // ts, f // tf
    te = ts + hal
    first = hal - (FFN_K - 1)
    rc = min(FFN_ROW_CHUNK, ts)

    def body(dy_ref, dyn_ref, x_ref, g_ref, a_ref, ap_ref, an_ref, b_ref, bp_ref, bn_ref, wa_ref, wb_ref,
             cwa_ref, cwb_ref, ba_ref, bb_ref, wd_ref,
             dx_ref, dpa_ref, dpb_ref, dwa_ref, dwb_ref, dba_ref, dbb_ref, dg_ref,
             dye_ref, ea_ref, eb_ref, dact_ref, da_ref, db_ref, dxn_ref):
        i, j = pl.program_id(0), pl.program_id(1)
        last = i == nt - 1

        @pl.when(j == 0)
        def _():
            dye_ref[pl.ds(0, ts), :] = dy_ref[...].astype(BF16)
            dye_ref[pl.ds(ts, hal), :] = jnp.where(last, 0.0, dyn_ref[...]).astype(BF16)
            dxn_ref[...] = jnp.zeros_like(dxn_ref)

        @pl.when((i == 0) & (j == 0))
        def _():
            for r in (dwa_ref, dwb_ref, dba_ref, dbb_ref, dg_ref):
                r[...] = jnp.zeros_like(r)

        halves = tuple((k * ts // FFN_BWD_PARTS, (k + 1) * ts // FFN_BWD_PARTS) for k in range(FFN_BWD_PARTS))
        for lo, hi in halves:
            n = hi - lo + (hal if hi == ts else 0)
            dact_ref[pl.ds(lo, n), :] = lax.dot_general(dye_ref[pl.ds(lo, n), :], wd_ref[...], _NT,
                                                        preferred_element_type=F32)
        for cur, prev, nxt, ext in ((a_ref, ap_ref, an_ref, ea_ref), (b_ref, bp_ref, bn_ref, eb_ref)):
            ext[pl.ds(0, hal), :] = jnp.where(i > 0, prev[...].astype(F32), 0.0)
            ext[pl.ds(hal, ts), :] = cur[...].astype(F32)
            ext[pl.ds(hal + ts, hal), :] = jnp.where(last, 0.0, nxt[...].astype(F32))
        zero = jnp.zeros((1, tf), F32)
        sums = {"ba": zero, "bb": zero, **{("a", k): zero for k in range(FFN_K)}, **{("b", k): zero for k in range(FFN_K)}}
        for r0 in list(range(0, ts, rc)) + [ts]:
            n = rc if r0 < ts else hal
            win_a = [ea_ref[pl.ds(first + r0 + k, n), :] for k in range(FFN_K)]
            win_b = [eb_ref[pl.ds(first + r0 + k, n), :] for k in range(FFN_K)]
            ha = sum(cwa_ref[pl.ds(k, 1), :] * win_a[k] for k in range(FFN_K)) + ba_ref[...]
            hb = sum(cwb_ref[pl.ds(k, 1), :] * win_b[k] for k in range(FFN_K)) + bb_ref[...]
            sig = _sigmoid(ha)
            gs = dact_ref[pl.ds(r0, n), :] * sig
            dha = gs * hb * (1.0 + ha * (1.0 - sig))
            dhb = gs * ha
            da_ref[pl.ds(r0, n), :] = dha
            db_ref[pl.ds(r0, n), :] = dhb
            if r0 < ts:
                sums["ba"] = sums["ba"] + jnp.sum(dha, axis=0, keepdims=True)
                sums["bb"] = sums["bb"] + jnp.sum(dhb, axis=0, keepdims=True)
                for k in range(FFN_K):
                    sums["a", k] = sums["a", k] + jnp.sum(dha * win_a[k], axis=0, keepdims=True)
                    sums["b", k] = sums["b", k] + jnp.sum(dhb * win_b[k], axis=0, keepdims=True)
        for lo, hi in halves:
            for r0 in range(lo, hi, rc):
                dpa_ref[pl.ds(r0, rc), :] = _conv_taps(da_ref, cwa_ref, rc, r0, FFN_K, flip=True).astype(BF16)
                dpb_ref[pl.ds(r0, rc), :] = _conv_taps(db_ref, cwb_ref, rc, r0, FFN_K, flip=True).astype(BF16)
            rows = pl.ds(lo, hi - lo)
            dxn_ref[rows, :] += (lax.dot_general(dpa_ref[rows, :], wa_ref[...], _NT, preferred_element_type=F32)
                                 + lax.dot_general(dpb_ref[rows, :], wb_ref[...], _NT, preferred_element_type=F32))
        dba_ref[j] += jnp.broadcast_to(sums["ba"], (8, tf))
        dbb_ref[j] += jnp.broadcast_to(sums["bb"], (8, tf))
        row = lax.broadcasted_iota(jnp.int32, (8, tf), 0)
        dwa_ref[j] += sum(jnp.where(row == k, sums["a", k], 0.0) for k in range(FFN_K))
        dwb_ref[j] += sum(jnp.where(row == k, sums["b", k], 0.0) for k in range(FFN_K))

        @pl.when(j == nj - 1)
        def _():
            _, vjp = jax.vjp(_rms, x_ref[...], g_ref[...])
            dx, dg = vjp(dxn_ref[...])
            dx_ref[...] = dx + dy_ref[...]
            dg_ref[...] += jnp.broadcast_to(jnp.sum(dg, axis=0, keepdims=True), dg_ref.shape)

    r = ts // hal
    lastblk = s // hal - 1
    row = pl.BlockSpec((ts, d), lambda i, j: (i, 0))
    gsp = pl.BlockSpec((1, d), lambda i, j: (0, 0))
    dgsp = pl.BlockSpec((8, d), lambda i, j: (0, 0))
    rown = pl.BlockSpec((hal, d), lambda i, j: (jnp.minimum((i + 1) * r, lastblk), 0))
    cur = pl.BlockSpec((ts, tf), lambda i, j: (i, j))
    prev = pl.BlockSpec((hal, tf), lambda i, j: (jnp.maximum(i * r - 1, 0), j))
    nxt = pl.BlockSpec((hal, tf), lambda i, j: (jnp.minimum((i + 1) * r, lastblk), j))
    wup = pl.BlockSpec((d, tf), lambda i, j: (0, j))
    cwsp = pl.BlockSpec((8, tf), lambda i, j: (0, j))
    bsp = pl.BlockSpec((1, tf), lambda i, j: (0, j))
    wdn = pl.BlockSpec((tf, d), lambda i, j: (j, 0))
    accsp = pl.BlockSpec((nj, 8, tf), lambda i, j: (0, 0, 0))
    accshape = jax.ShapeDtypeStruct((nj, 8, tf), F32)
    return pl.pallas_call(
        body, name=name, grid=(nt, nj),
        in_specs=[row, rown, row, gsp, cur, prev, nxt, cur, prev, nxt, wup, wup, cwsp, cwsp, bsp, bsp, wdn],
        out_specs=(row, cur, cur, accsp, accsp, accsp, accsp, dgsp),
        out_shape=(jax.ShapeDtypeStruct((s, d), F32), jax.ShapeDtypeStruct((s, f), BF16),
                   jax.ShapeDtypeStruct((s, f), BF16), accshape, accshape, accshape, accshape,
                   jax.ShapeDtypeStruct((8, d), F32)),
        scratch_shapes=[pltpu.VMEM((te, d), BF16), pltpu.VMEM((ts + 2 * hal, tf), F32),
                        pltpu.VMEM((ts + 2 * hal, tf), F32), pltpu.VMEM((te, tf), F32),
                        pltpu.VMEM((te, tf), F32), pltpu.VMEM((te, tf), F32), pltpu.VMEM((ts, d), F32)],
        compiler_params=_cparams("arbitrary", "arbitrary"),
    )(dy, dy, x, g, hpa, hpa, hpa, hpb, hpb, hpb, wa, wb, cwa, cwb, ba, bb, wd)


NQ = 4
SQ = SSM_STATE * 8
NS = SSM_GROUPS * SSM_STATE


def _s5_disc(log_dt, a_re, a_im, b_re, b_im, expand):
    dt = jnp.exp(log_dt)
    mag = jnp.exp(a_re * dt)
    lb_re, lb_im = mag * jnp.cos(a_im * dt), mag * jnp.sin(a_im * dt)
    den = a_re * a_re + a_im * a_im
    nr, ni = lb_re - 1.0, lb_im
    f_re = (nr * a_re + ni * a_im) / den
    f_im = (ni * a_re - nr * a_im) / den
    fe_re = jnp.dot(f_re, expand, precision=lax.Precision.HIGHEST, preferred_element_type=F32)
    fe_im = jnp.dot(f_im, expand, precision=lax.Precision.HIGHEST, preferred_element_type=F32)
    return lb_re, lb_im, fe_re * b_re - fe_im * b_im, fe_re * b_im + fe_im * b_re


def _expand_matrix():
    e = np.zeros((SSM_STATE, SSM_STATE * SSM_GROUP), np.float32)
    for p in range(SSM_STATE):
        e[p, p * SSM_GROUP:(p + 1) * SSM_GROUP] = 1.0
    return jnp.asarray(e)


def s5_params_fwd(log_dt, a_re, a_im, b_re, b_im):
    expand = _expand_matrix()

    def body(ld_ref, ar_ref, ai_ref, br_ref, bi_ref, e_ref, lr_ref, li_ref, bbr_ref, bbi_ref):
        lr, li, bbr, bbi = _s5_disc(ld_ref[...], ar_ref[...], ai_ref[...], br_ref[...], bi_ref[...], e_ref[...])
        lr_ref[...] = lr
        li_ref[...] = li
        bbr_ref[...] = bbr
        bbi_ref[...] = bbi

    g, p, pc = SSM_GROUPS, SSM_STATE, SSM_STATE * SSM_GROUP
    return pl.pallas_call(
        body, name="s5_params_fwd",
        out_shape=(jax.ShapeDtypeStruct((g, p), F32), jax.ShapeDtypeStruct((g, p), F32),
                   jax.ShapeDtypeStruct((g, pc), F32), jax.ShapeDtypeStruct((g, pc), F32)),
    )(log_dt, a_re, a_im, b_re, b_im, expand)


def s5_params_bwd(log_dt, a_re, a_im, b_re, b_im, dlr, dli, dbbr, dbbi):
    expand = _expand_matrix()

    def body(ld_ref, ar_ref, ai_ref, br_ref, bi_ref, e_ref, dlr_ref, dli_ref, dbbr_ref, dbbi_ref,
             dld_ref, dar_ref, dai_ref, dbr_ref, dbi_ref):
        e = e_ref[...]
        f = lambda ld, ar, ai, br, bi: _s5_disc(ld, ar, ai, br, bi, e)
        _, vjp = jax.vjp(f, ld_ref[...], ar_ref[...], ai_ref[...], br_ref[...], bi_ref[...])
        dld, dar, dai, dbr, dbi = vjp((dlr_ref[...], dli_ref[...], dbbr_ref[...], dbbi_ref[...]))
        dld_ref[...] = dld
        dar_ref[...] = dar
        dai_ref[...] = dai
        dbr_ref[...] = dbr
        dbi_ref[...] = dbi

    g, p, pc = SSM_GROUPS, SSM_STATE, SSM_STATE * SSM_GROUP
    return pl.pallas_call(
        body, name="s5_params_bwd",
        out_shape=(jax.ShapeDtypeStruct((g, 1), F32), jax.ShapeDtypeStruct((g, p), F32),
                   jax.ShapeDtypeStruct((g, p), F32), jax.ShapeDtypeStruct((g, pc), F32),
                   jax.ShapeDtypeStruct((g, pc), F32)),
    )(log_dt, a_re, a_im, b_re, b_im, expand, dlr, dli, dbbr, dbbi)


def _cmul(ar, ai, br, bi):
    return ar * br - ai * bi, ar * bi + ai * br


def _power_rows(lr, li, conj_rev):
    row = lax.broadcasted_iota(jnp.int32, (8, NS), 0)
    tr = jnp.zeros((8, NS), F32)
    ti = jnp.zeros((8, NS), F32)
    pr, pi = lr, li
    for r in range(8):
        dst = 7 - r if conj_rev else r
        tr = jnp.where(row == dst, pr, tr)
        ti = jnp.where(row == dst, -pi if conj_rev else pi, ti)
        if r < 7:
            pr, pi = _cmul(pr, pi, lr, li)
    return tr, ti


def _scan8(xr, xi, tr_ref, ti_ref, cr, ci, reverse):
    row = lax.broadcasted_iota(jnp.int32, xr.shape, 0)
    for d in (1, 2, 4):
        if reverse:
            sr, si = pltpu.roll(xr, 8 - d, 0), pltpu.roll(xi, 8 - d, 0)
            keep = row < 8 - d
            pw = 8 - d
        else:
            sr, si = pltpu.roll(xr, d, 0), pltpu.roll(xi, d, 0)
            keep = row >= d
            pw = d - 1
        mr, mi = _cmul(tr_ref[pl.ds(pw, 1), :], ti_ref[pl.ds(pw, 1), :], sr, si)
        xr = xr + jnp.where(keep, mr, 0.0)
        xi = xi + jnp.where(keep, mi, 0.0)
    mr, mi = _cmul(tr_ref[...], ti_ref[...], cr, ci)
    return xr + mr, xi + mi


def _row_of(x, r):
    row = lax.broadcasted_iota(jnp.int32, x.shape, 0)
    return jnp.sum(jnp.where(row == r, x, 0.0), axis=0, keepdims=True)


def s5_scan_fwd(u, lam_r, lam_i, bre, bim, cre, cim, dskip):
    s = u.shape[0]
    tt = min(SCAN_TILE, s)
    nb = tt // 8

    def body(nt, u_ref, lr_ref, li_ref, bre_ref, bim_ref, cre_ref, cim_ref, d_ref,
             xr_ref, xi_ref, y_ref, yg_ref, tr_ref, ti_ref, cr_ref, ci_ref):
        i = pl.program_id(0)

        @pl.when(i == 0)
        def _():
            tr, ti = _power_rows(lr_ref[...], li_ref[...], False)
            tr_ref[...] = tr
            ti_ref[...] = ti
            cr_ref[...] = jnp.zeros_like(cr_ref)
            ci_ref[...] = jnp.zeros_like(ci_ref)

        uv = u_ref[...]
        ub = uv.astype(BF16)
        for q in range(NQ):
            uq = ub[:, q * LANES:(q + 1) * LANES]
            xr_ref[:, q * SQ:(q + 1) * SQ] = jnp.dot(uq, bre_ref[q], preferred_element_type=F32)
            xi_ref[:, q * SQ:(q + 1) * SQ] = jnp.dot(uq, bim_ref[q], preferred_element_type=F32)

        def step(b, carry):
            cr, ci = carry
            rows = pl.ds(pl.multiple_of(b * 8, 8), 8)
            xr, xi = _scan8(xr_ref[rows, :], xi_ref[rows, :], tr_ref, ti_ref, cr, ci, False)
            xr_ref[rows, :] = xr
            xi_ref[rows, :] = xi
            return _row_of(xr, 7), _row_of(xi, 7)

        cr, ci = lax.fori_loop(0, nb, step, (cr_ref[...], ci_ref[...]), unroll=min(SCAN_UNROLL, nb))
        cr_ref[...] = cr
        ci_ref[...] = ci
        y = d_ref[...] * uv
        for q in range(NQ):
            yq = (jnp.dot(xr_ref[:, q * SQ:(q + 1) * SQ].astype(BF16), cre_ref[q], preferred_element_type=F32)
                  - jnp.dot(xi_ref[:, q * SQ:(q + 1) * SQ].astype(BF16), cim_ref[q], preferred_element_type=F32))
            y_ref[:, q * LANES:(q + 1) * LANES] = yq + y[:, q * LANES:(q + 1) * LANES]
        yg_ref[...] = _gelu(y_ref[...]).astype(BF16)

    return rowcall(body, rows=s, ts=tt,
                   ins=[(u, "row"), (lam_r, "full"), (lam_i, "full"), (bre, "full"), (bim, "full"),
                        (cre, "full"), (cim, "full"), (dskip, "full")],
                   outs=[((s, NS), F32, "row"), ((s, NS), F32, "row"), ((s, SSM_WIDTH), F32, "row"),
                         ((s, SSM_WIDTH), BF16, "row")], name="s5_scan_fwd",
                   scratch=[pltpu.VMEM((8, NS), F32), pltpu.VMEM((8, NS), F32),
                            pltpu.VMEM((1, NS), F32), pltpu.VMEM((1, NS), F32)])


def s5_scan_bwd(dyg, y, lam_r, lam_i, cre, cim):
    s = y.shape[0]
    tt = min(SCAN_TILE, s)
    nb = tt // 8

    def body(nt, dyg_ref, y_ref, lr_ref, li_ref, cre_ref, cim_ref,
             ar_ref, ai_ref, dy_ref, tr_ref, ti_ref, cr_ref, ci_ref):
        i = pl.program_id(0)

        @pl.when(i == 0)
        def _():
            tr, ti = _power_rows(lr_ref[...], li_ref[...], True)
            tr_ref[...] = tr
            ti_ref[...] = ti
            cr_ref[...] = jnp.zeros_like(cr_ref)
            ci_ref[...] = jnp.zeros_like(ci_ref)

        _, vjp = jax.vjp(_gelu, y_ref[...])
        dy = vjp(dyg_ref[...])[0]
        dyb = dy.astype(BF16)
        dy_ref[...] = dyb
        for q in range(NQ):
            dq = dyb[:, q * LANES:(q + 1) * LANES]
            ar_ref[:, q * SQ:(q + 1) * SQ] = lax.dot_general(dq, cre_ref[q], _NT, preferred_element_type=F32)
            ai_ref[:, q * SQ:(q + 1) * SQ] = -lax.dot_general(dq, cim_ref[q], _NT, preferred_element_type=F32)

        def step(b, carry):
            cr, ci = carry
            rows = pl.ds(pl.multiple_of((nb - 1 - b) * 8, 8), 8)
            xr, xi = _scan8(ar_ref[rows, :], ai_ref[rows, :], tr_ref, ti_ref, cr, ci, True)
            ar_ref[rows, :] = xr
            ai_ref[rows, :] = xi
            return _row_of(xr, 0), _row_of(xi, 0)

        cr, ci = lax.fori_loop(0, nb, step, (cr_ref[...], ci_ref[...]), unroll=min(SCAN_UNROLL, nb))
        cr_ref[...] = cr
        ci_ref[...] = ci

    return rowcall(body, rows=s, ts=tt,
                   ins=[(dyg, "rev"), (y, "rev"), (lam_r, "full"), (lam_i, "full"), (cre, "full"), (cim, "full")],
                   outs=[((s, NS), F32, "rev"), ((s, NS), F32, "rev"), ((s, SSM_WIDTH), BF16, "rev")],
                   name="s5_scan_bwd",
                   scratch=[pltpu.VMEM((8, NS), F32), pltpu.VMEM((8, NS), F32),
                            pltpu.VMEM((1, NS), F32), pltpu.VMEM((1, NS), F32)])


def s5_grads(u, dy, xr, xi, ar, ai, bre, bim, dskip):
    s = u.shape[0]
    tt = min(SCAN_TILE, s)

    def body(nt, u_ref, dy_ref, xr_ref, xrp_ref, xi_ref, xip_ref, ar_ref, ai_ref, bre_ref, bim_ref, d_ref,
             du_ref, dlr_ref, dli_ref, dbr_ref, dbi_ref, dcr_ref, dci_ref, dd_ref, er_ref, ei_ref):
        i = pl.program_id(0)

        @pl.when(i == 0)
        def _():
            for r in (dbr_ref, dbi_ref, dcr_ref, dci_ref):
                r[...] = jnp.zeros_like(r)

        uv, dyb = u_ref[...], dy_ref[...]
        dyf = dyb.astype(F32)
        av_r, av_i, xv_r, xv_i = ar_ref[...], ai_ref[...], xr_ref[...], xi_ref[...]
        er_ref[pl.ds(0, 8), :] = jnp.where(i > 0, xrp_ref[...], 0.0)
        ei_ref[pl.ds(0, 8), :] = jnp.where(i > 0, xip_ref[...], 0.0)
        er_ref[pl.ds(8, tt), :] = xv_r
        ei_ref[pl.ds(8, tt), :] = xv_i
        sr, si = er_ref[pl.ds(7, tt), :], ei_ref[pl.ds(7, tt), :]
        _acc(dlr_ref, i, av_r * sr + av_i * si)
        _acc(dli_ref, i, av_i * sr - av_r * si)
        _acc(dd_ref, i, dyf * uv)
        ub = uv.astype(BF16)
        ab_r, ab_i = av_r.astype(BF16), av_i.astype(BF16)
        xb_r, xb_i = xv_r.astype(BF16), xv_i.astype(BF16)
        du = d_ref[...] * dyf
        for q in range(NQ):
            cs, ss = slice(q * LANES, (q + 1) * LANES), slice(q * SQ, (q + 1) * SQ)
            dbr_ref[q] += lax.dot_general(ub[:, cs], ab_r[:, ss], _TN, preferred_element_type=F32)
            dbi_ref[q] += lax.dot_general(ub[:, cs], ab_i[:, ss], _TN, preferred_element_type=F32)
            dcr_ref[q] += lax.dot_general(xb_r[:, ss], dyb[:, cs], _TN, preferred_element_type=F32)
            dci_ref[q] -= lax.dot_general(xb_i[:, ss], dyb[:, cs], _TN, preferred_element_type=F32)
            du_ref[:, cs] = (du[:, cs]
                             + lax.dot_general(ab_r[:, ss], bre_ref[q], _NT, preferred_element_type=F32)
                             + lax.dot_general(ab_i[:, ss], bim_ref[q], _NT, preferred_element_type=F32))

    return rowcall(body, rows=s, ts=tt,
                   ins=[(u, "row"), (dy, "row"), (xr, "row"), (xr, "prev:8"), (xi, "row"), (xi, "prev:8"),
                        (ar, "row"), (ai, "row"), (bre, "full"), (bim, "full"), (dskip, "full")],
                   outs=[((s, SSM_WIDTH), F32, "row"), ((8, NS), F32, "acc"), ((8, NS), F32, "acc"),
                         ((NQ, LANES, SQ), F32, "acc"), ((NQ, LANES, SQ), F32, "acc"),
                         ((NQ, SQ, LANES), F32, "acc"), ((NQ, SQ, LANES), F32, "acc"),
                         ((8, SSM_WIDTH), F32, "acc")], name="s5_grads",
                   scratch=[pltpu.VMEM((tt + 8, NS), F32), pltpu.VMEM((tt + 8, NS), F32)])


def _glu_fn(za, zb):
    return za * _sigmoid(zb)


def glu_res_fwd(z, xres):
    s = z.shape[0]

    def body(nt, z_ref, x_ref, o_ref):
        o_ref[...] = x_ref[...] + _glu_fn(z_ref[:, :D_MODEL].astype(F32), z_ref[:, D_MODEL:].astype(F32))

    return rowcall(body, rows=s, ts=min(ROW_TILE, s), ins=[(z, "row"), (xres, "row")],
                   outs=[((s, D_MODEL), F32, "row")], name="glu_res_fwd")[0]


def glu_bwd(z, dout):
    s, c = z.shape

    def body(nt, z_ref, d_ref, dz_ref, dba_ref, dbb_ref):
        i = pl.program_id(0)
        _, vjp = jax.vjp(_glu_fn, z_ref[:, :D_MODEL].astype(F32), z_ref[:, D_MODEL:].astype(F32))
        dza, dzb = vjp(d_ref[...])
        dz_ref[:, :D_MODEL] = dza.astype(BF16)
        dz_ref[:, D_MODEL:] = dzb.astype(BF16)
        _acc(dba_ref, i, dza)
        _acc(dbb_ref, i, dzb)

    return rowcall(body, rows=s, ts=min(ROW_TILE, s), ins=[(z, "row"), (dout, "row")],
                   outs=[((s, c), BF16, "row"), ((8, D_MODEL), F32, "acc"), ((8, D_MODEL), F32, "acc")],
                   name="glu_bwd")


def loss_head(x, g, target):
    s, c = x.shape

    def body(nt, x_ref, g_ref, t_ref, loss_ref, dx_ref, dg_ref):
        i = pl.program_id(0)
        y, vjp = jax.vjp(_rms, x_ref[...], g_ref[...])
        err = y - t_ref[...]
        dx, dg = vjp(err * (1.0 / c))
        dx_ref[...] = dx
        _acc(dg_ref, i, dg)
        part = jnp.sum(jnp.sum(err * err, axis=-1, keepdims=True), axis=0, keepdims=True) * (0.5 / c)

        @pl.when(i == 0)
        def _():
            loss_ref[...] = jnp.zeros_like(loss_ref)

        loss_ref[...] += jnp.broadcast_to(part, loss_ref.shape)

    return rowcall(body, rows=s, ts=min(ROW_TILE, s), ins=[(x, "row"), (g, "full"), (target, "row")],
                   outs=[((8, LANES), F32, "acc"), ((s, c), F32, "row"), ((8, c), F32, "acc")], name="loss_head")


def _tile_rows(r, cands=(512, 256, 128, 64, 32, 16, 8)):
    return _pick(r, cands)


def add_to_bf16(a, b, name):
    n, r, c = a.shape
    tr = _tile_rows(r)

    def body(a_ref, b_ref, o_ref):
        o_ref[...] = (a_ref[...].astype(F32) + b_ref[...].astype(F32)).astype(BF16)

    spec = pl.BlockSpec((1, tr, c), lambda j, i: (j, i, 0))
    return pl.pallas_call(body, name=name, grid=(n, r // tr), in_specs=[spec, spec], out_specs=spec,
                          out_shape=jax.ShapeDtypeStruct((n, r, c), BF16),
                          compiler_params=_cparams("parallel", "parallel"))(a, b)


def sum_leading(a, name):
    n, r, c = a.shape
    tr = _tile_rows(r)

    def body(a_ref, o_ref):
        acc = a_ref[0].astype(F32)
        for k in range(1, n):
            acc = acc + a_ref[k].astype(F32)
        o_ref[...] = acc

    return pl.pallas_call(body, name=name, grid=(r // tr,),
                          in_specs=[pl.BlockSpec((n, tr, c), lambda i: (0, i, 0))],
                          out_specs=pl.BlockSpec((tr, c), lambda i: (i, 0)),
                          out_shape=jax.ShapeDtypeStruct((r, c), F32),
                          compiler_params=_cparams("parallel"))(a)


def adamw(w, g, m, v, name):
    r, c = w.shape
    tr = _tile_rows(r, (256, 128, 64, 32, 16, 8))
    c1 = 1.0 - ADAM_B1 ** ADAM_STEP
    c2 = 1.0 - ADAM_B2 ** ADAM_STEP

    def body(w_ref, g_ref, m_ref, v_ref, d_ref, nm_ref, nv_ref):
        gv = g_ref[...]
        mn = ADAM_B1 * m_ref[...] + (1.0 - ADAM_B1) * gv
        vn = ADAM_B2 * v_ref[...] + (1.0 - ADAM_B2) * (gv * gv)
        d_ref[...] = -ADAM_LR * ((mn / c1) / (jnp.sqrt(vn / c2) + ADAM_EPS) + ADAM_WD * w_ref[...])
        nm_ref[...] = mn
        nv_ref[...] = vn

    spec = pl.BlockSpec((tr, c), lambda i: (i, 0))
    shp = jax.ShapeDtypeStruct((r, c), F32)
    return pl.pallas_call(body, name=name, grid=(r // tr,), in_specs=[spec] * 4, out_specs=(spec,) * 3,
                          out_shape=(shp,) * 3, compiler_params=_cparams("parallel"))(w, g, m, v)


_ANY = pl.BlockSpec(memory_space=pl.ANY)


def all_gather8(block, name):
    r, c = block.shape

    def body(x_ref, out_ref, send_sems, recv_sems, local_sem):
        x, y, cc = lax.axis_index("x"), lax.axis_index("y"), lax.axis_index("c")
        me, sibling = (x, y, cc), (x, y, 1 - cc)
        chips = [(1 - x, y), (x, 1 - y), (1 - x, 1 - y)]

        def slot(px, py, pc):
            return out_ref.at[4 * px + 2 * py + pc]

        def copy(k, blk, to, src=None):
            return pltpu.make_async_remote_copy(
                src_ref=slot(*blk) if src is None else src, dst_ref=slot(*blk),
                send_sem=send_sems.at[k], recv_sem=recv_sems.at[k], device_id=to, device_id_type=MESH)

        mine = pltpu.make_async_copy(x_ref, slot(*me), local_sem)
        mine.start()
        first = [copy(0, me, sibling, src=x_ref)]
        first += [copy(1 + j, me, (*chip, cc), src=x_ref) for j, chip in enumerate(chips)]
        for cp in first:
            cp.start()
        passed = [copy(4 + j, (*chip, cc), sibling) for j, chip in enumerate(chips)]
        for j, chip in enumerate(chips):
            copy(1 + j, (*chip, cc), me).wait_recv()
            passed[j].start()
        copy(0, sibling, me).wait_recv()
        for j, chip in enumerate(chips):
            copy(4 + j, (*chip, 1 - cc), me).wait_recv()
        for cp in first + passed:
            cp.wait_send()
        mine.wait()

    return pl.pallas_call(
        body, name=name, in_specs=[_ANY], out_specs=_ANY,
        out_shape=jax.ShapeDtypeStruct((8, r, c), block.dtype),
        scratch_shapes=[pltpu.SemaphoreType.DMA((7,)), pltpu.SemaphoreType.DMA((7,)), pltpu.SemaphoreType.DMA],
    )(block)


_HBM = pl.BlockSpec(memory_space=pltpu.HBM)
_SEM = pl.BlockSpec(memory_space=pltpu.SEMAPHORE)
_DATAFLOW = pltpu.SideEffectType.DATAFLOW_SIDE_EFFECTING
N_REMOTE = 6


def _remote_peers(x, y, cc):
    return [(1 - x, y, cc), (x, 1 - y, cc), (1 - x, 1 - y, cc),
            (1 - x, y, 1 - cc), (x, 1 - y, 1 - cc), (1 - x, 1 - y, 1 - cc)]


def gather_start(block, after, name):
    r, c = block.shape

    def body(x_ref, land_ref, after_ref, send_sems, recv_sems, x_thru, land_thru, token):
        x, y, cc = lax.axis_index("x"), lax.axis_index("y"), lax.axis_index("c")
        for k, peer in enumerate(_remote_peers(x, y, cc)):
            pltpu.make_async_remote_copy(src_ref=x_ref, dst_ref=land_ref.at[4 * x + 2 * y + cc],
                                         send_sem=send_sems.at[k], recv_sem=recv_sems.at[k],
                                         device_id=peer, device_id_type=MESH).start()
        token[...] = jnp.zeros_like(token)

    land = pltpu.with_memory_space_constraint(lax.empty((8, r, c), block.dtype), pltpu.HBM)
    return pl.pallas_call(
        body, name=name,
        out_shape=(pltpu.SemaphoreType.DMA((N_REMOTE,)), pltpu.SemaphoreType.DMA((N_REMOTE,)),
                   pltpu.HBM((r, c), block.dtype), pltpu.HBM((8, r, c), block.dtype),
                   jax.ShapeDtypeStruct((8, LANES), F32)),
        in_specs=(_HBM, _HBM, _ANY), out_specs=(_SEM, _SEM, _HBM, _HBM, pl.BlockSpec(memory_space=pltpu.VMEM)),
        input_output_aliases={0: 2, 1: 3},
        compiler_params=pltpu.CompilerParams(has_side_effects=_DATAFLOW),
    )(pltpu.with_memory_space_constraint(block, pltpu.HBM), land, after)


def gather_wait(send_sems, recv_sems, block_thru, land_thru, after, name):
    def body(x_ref, land_ref, send_sems, recv_sems, after_ref, x_dead, got_ref):
        x, y, cc = lax.axis_index("x"), lax.axis_index("y"), lax.axis_index("c")
        for k, (px, py, pc) in enumerate(_remote_peers(x, y, cc)):
            cp = pltpu.make_async_remote_copy(src_ref=x_ref, dst_ref=land_ref.at[4 * px + 2 * py + pc],
                                              send_sem=send_sems.at[k], recv_sem=recv_sems.at[k],
                                              device_id=(px, py, pc), device_id_type=MESH)
            cp.wait_send()
            cp.wait_recv()

    return pl.pallas_call(
        body, name=name,
        out_shape=(pltpu.HBM(block_thru.shape, block_thru.dtype), pltpu.HBM(land_thru.shape, land_thru.dtype)),
        in_specs=(_HBM, _HBM, _SEM, _SEM, _ANY), out_specs=(_HBM, _HBM), input_output_aliases={0: 0, 1: 1},
        compiler_params=pltpu.CompilerParams(has_side_effects=_DATAFLOW),
    )(block_thru, land_thru, send_sems, recv_sems, after)[1]


def _chip_peers(x, y):
    return [(1 - x, y), (x, 1 - y), (1 - x, 1 - y)]


def exchange_start(parts, name):
    def body(p_ref, land_ref, send_sems, recv_sems, p_thru, land_thru, token):
        x, y, cc = lax.axis_index("x"), lax.axis_index("y"), lax.axis_index("c")
        for k, (px, py) in enumerate(_chip_peers(x, y)):
            pltpu.make_async_remote_copy(src_ref=p_ref.at[2 * px + py], dst_ref=land_ref.at[2 * x + y],
                                         send_sem=send_sems.at[k], recv_sem=recv_sems.at[k],
                                         device_id=(px, py, cc), device_id_type=MESH).start()
        token[...] = jnp.zeros_like(token)

    land = pltpu.with_memory_space_constraint(lax.empty(parts.shape, parts.dtype), pltpu.HBM)
    return pl.pallas_call(
        body, name=name,
        out_shape=(pltpu.SemaphoreType.DMA((3,)), pltpu.SemaphoreType.DMA((3,)),
                   pltpu.HBM(parts.shape, parts.dtype), pltpu.HBM(parts.shape, parts.dtype),
                   jax.ShapeDtypeStruct((8, LANES), F32)),
        in_specs=(_HBM, _HBM), out_specs=(_SEM, _SEM, _HBM, _HBM, pl.BlockSpec(memory_space=pltpu.VMEM)),
        input_output_aliases={0: 2, 1: 3},
        compiler_params=pltpu.CompilerParams(has_side_effects=_DATAFLOW),
    )(pltpu.with_memory_space_constraint(parts, pltpu.HBM), land)


def exchange_wait(send_sems, recv_sems, parts_thru, land_thru, after, name):
    def body(p_ref, land_ref, send_sems, recv_sems, after_ref, p_dead, got_ref):
        x, y, cc = lax.axis_index("x"), lax.axis_index("y"), lax.axis_index("c")
        for k, (px, py) in enumerate(_chip_peers(x, y)):
            cp = pltpu.make_async_remote_copy(src_ref=p_ref.at[2 * px + py], dst_ref=land_ref.at[2 * px + py],
                                              send_sem=send_sems.at[k], recv_sem=recv_sems.at[k],
                                              device_id=(px, py, cc), device_id_type=MESH)
            cp.wait_send()
            cp.wait_recv()

    return pl.pallas_call(
        body, name=name,
        out_shape=(pltpu.HBM(parts_thru.shape, parts_thru.dtype), pltpu.HBM(land_thru.shape, land_thru.dtype)),
        in_specs=(_HBM, _HBM, _SEM, _SEM, _ANY), out_specs=(_HBM, _HBM), input_output_aliases={0: 0, 1: 1},
        compiler_params=pltpu.CompilerParams(has_side_effects=_DATAFLOW),
    )(parts_thru, land_thru, send_sems, recv_sems, after)


def sibling_swap(block, name):
    def body(x_ref, out_ref, send_sem, recv_sem):
        x, y, cc = lax.axis_index("x"), lax.axis_index("y"), lax.axis_index("c")
        cp = pltpu.make_async_remote_copy(src_ref=x_ref, dst_ref=out_ref, send_sem=send_sem, recv_sem=recv_sem,
                                          device_id=(x, y, 1 - cc), device_id_type=MESH)
        cp.start()
        cp.wait()

    return pl.pallas_call(
        body, name=name, in_specs=[_ANY], out_specs=_ANY,
        out_shape=jax.ShapeDtypeStruct(block.shape, block.dtype),
        scratch_shapes=[pltpu.SemaphoreType.DMA, pltpu.SemaphoreType.DMA],
    )(block)


PACK_COLS = 1024
SHARDED = (("l0_w_in", 1), ("l0_w_uq", 1), ("l0_w_ukv", 1), ("l0_w_out", 0), ("l0_w_up", 1), ("l0_w_down", 0),
           ("l1_w_in", 0), ("l1_w_glu", 1), ("l1_w_up", 1), ("l1_w_down", 0),
           ("l0_conv_w", 1), ("l0_ffn_conv_w", 1), ("l1_ffn_conv_w", 1))
REPLICATED = ("l0_mix_norm", "l0_conv_b", "l0_conv_ln_g", "l0_conv_ln_b", "l0_q_norm", "l0_kv_norm", "l0_ffn_norm",
              "l0_ffn_conv_b", "l1_mix_norm", "l1_log_dt", "l1_a_re", "l1_a_im", "l1_b_re", "l1_b_im", "l1_c_re",
              "l1_c_im", "l1_d", "l1_b_glu", "l1_ffn_norm", "l1_ffn_conv_b", "final_norm")


def _pack(arrs, dtype, mult):
    flat = jnp.concatenate([a.reshape(-1).astype(dtype) for a in arrs])
    n = flat.shape[0]
    total = -(-n // mult) * mult
    return jnp.pad(flat, (0, total - n))


def _unpack(flat, shapes):
    out, pos = [], 0
    for shp in shapes:
        n = int(np.prod(shp))
        out.append(flat[pos:pos + n].reshape(shp))
        pos += n
    return out


PACK_ROW_ALIGN = 16


def _pack_rows(arrs, dtype, row_mult):
    parts = []
    for a in arrs:
        n = int(np.prod(a.shape))
        rows = -(-n // PACK_COLS)
        if n % PACK_COLS == 0:
            r = a.astype(dtype).reshape(rows, PACK_COLS)
        else:
            r = jnp.pad(a.reshape(-1).astype(dtype), (0, rows * PACK_COLS - n)).reshape(rows, PACK_COLS)
        parts.append(jnp.pad(r, ((0, (-rows) % PACK_ROW_ALIGN), (0, 0))))
    p = jnp.concatenate(parts)
    return jnp.pad(p, ((0, (-p.shape[0]) % row_mult), (0, 0)))


def _unpack_rows(pack, shapes):
    out, r0 = [], 0
    for shp in shapes:
        n = int(np.prod(shp))
        rows = -(-n // PACK_COLS)
        piece = lax.optimization_barrier(pack[r0:r0 + rows])
        out.append(piece.reshape(shp) if n % PACK_COLS == 0 else piece.reshape(-1)[:n].reshape(shp))
        r0 += rows + (-rows) % PACK_ROW_ALIGN
    return out


def _shard(full, axis, j):
    n = full.shape[axis] // N_CHIPS
    return lax.slice_in_dim(full, j * n, (j + 1) * n, axis=axis)


def _block_diag(t):
    q, g, a, b = t.shape
    eye = jnp.eye(g, dtype=t.dtype)
    return jnp.einsum("qgab,gh->qgahb", t, eye).reshape(q, g * a, g * b)


def _block_diag_t(d, a, b):
    q = d.shape[0]
    d5 = d.reshape(q, 8, a, 8, b)
    eye = jnp.eye(8, dtype=d.dtype)
    return jnp.einsum("qgahb,gh->qgab", d5, eye)


def kernel(x, l0_mix_norm, l0_w_in, l0_conv_w, l0_conv_b, l0_conv_ln_g, l0_conv_ln_b, l0_q_norm, l0_kv_norm, l0_w_uq, l0_w_ukv, l0_w_out, l0_ffn_norm, l0_w_up, l0_ffn_conv_w, l0_ffn_conv_b, l0_w_down, l1_mix_norm, l1_w_in, l1_log_dt, l1_a_re, l1_a_im, l1_b_re, l1_b_im, l1_c_re, l1_c_im, l1_d, l1_w_glu, l1_b_glu, l1_ffn_norm, l1_w_up, l1_ffn_conv_w, l1_ffn_conv_b, l1_w_down, final_norm, loss_target, m_l0_mix_norm, m_l0_w_in, m_l0_conv_w, m_l0_conv_b, m_l0_conv_ln_g, m_l0_conv_ln_b, m_l0_q_norm, m_l0_kv_norm, m_l0_w_uq, m_l0_w_ukv, m_l0_w_out, m_l0_ffn_norm, m_l0_w_up, m_l0_ffn_conv_w, m_l0_ffn_conv_b, m_l0_w_down, m_l1_mix_norm, m_l1_w_in, m_l1_log_dt, m_l1_a_re, m_l1_a_im, m_l1_b_re, m_l1_b_im, m_l1_c_re, m_l1_c_im, m_l1_d, m_l1_w_glu, m_l1_b_glu, m_l1_ffn_norm, m_l1_w_up, m_l1_ffn_conv_w, m_l1_ffn_conv_b, m_l1_w_down, m_final_norm, v_l0_mix_norm, v_l0_w_in, v_l0_conv_w, v_l0_conv_b, v_l0_conv_ln_g, v_l0_conv_ln_b, v_l0_q_norm, v_l0_kv_norm, v_l0_w_uq, v_l0_w_ukv, v_l0_w_out, v_l0_ffn_norm, v_l0_w_up, v_l0_ffn_conv_w, v_l0_ffn_conv_b, v_l0_w_down, v_l1_mix_norm, v_l1_w_in, v_l1_log_dt, v_l1_a_re, v_l1_a_im, v_l1_b_re, v_l1_b_im, v_l1_c_re, v_l1_c_im, v_l1_d, v_l1_w_glu, v_l1_b_glu, v_l1_ffn_norm, v_l1_w_up, v_l1_ffn_conv_w, v_l1_ffn_conv_b, v_l1_w_down, v_final_norm):
    a = dict(locals())
    w = {n: a[n] for n in [s for s, _ in SHARDED] + list(REPLICATED)}
    mom = {n: a["m_" + n] for n in w}
    var = {n: a["v_" + n] for n in w}
    return _step(a["x"][0], a["loss_target"][0], w, mom, var)


FIRST_WEIGHTS = ("l0_w_in", "l0_w_uq", "l0_w_ukv", "l0_w_out")
LATER_WEIGHTS = ("l0_w_up", "l0_w_down", "l1_w_in", "l1_w_glu", "l1_w_up", "l1_w_down")


def _assemble(got, names, w):
    got = got.reshape(N_CHIPS, -1, PACK_COLS)
    shapes = [w[n].shape for n in names]
    per_chip = [_unpack_rows(got[j], shapes) for j in range(N_CHIPS)]
    axes = dict(SHARDED)
    return {n: jnp.concatenate([per_chip[j][k] for j in range(N_CHIPS)], axis=axes[n]) for k, n in enumerate(names)}


def _gather_weights(w):
    cc = lax.axis_index("c")
    small = [n for n, _ in SHARDED[10:]]
    full = {}
    for names, dtype, mult in ((FIRST_WEIGHTS, BF16, 2 * 256), (small, F32, 2 * PACK_ROW_ALIGN)):
        pack = _pack_rows([w[n] for n in names], dtype, mult)
        half = lax.dynamic_index_in_dim(pack.reshape(2, -1, PACK_COLS), cc, axis=0, keepdims=False)
        got = all_gather8(half, "gather_" + ("first_matrices" if dtype == BF16 else "conv_weights"))
        full.update(_assemble(got, names, w))
    pack = _pack_rows([w[n] for n in LATER_WEIGHTS], BF16, 2 * 256).reshape(2, -1, PACK_COLS)
    half = lax.dynamic_index_in_dim(pack, cc, axis=0, keepdims=False)
    send_sems, recv_sems, half_thru, land_thru, token = gather_start(half, got, "gather_later_start")
    return full, (send_sems, recv_sems, half_thru, land_thru, pack), token[0, 0]


def _finish_gather(pending, after, w):
    send_sems, recv_sems, half_thru, land_thru, pack = pending
    got = gather_wait(send_sems, recv_sems, half_thru, land_thru, after, "gather_later_wait")
    chip = 2 * lax.axis_index("x") + lax.axis_index("y")
    got = lax.dynamic_update_slice(got, pack, (2 * chip, 0, 0))
    return _assemble(got, LATER_WEIGHTS, w)


def _reduce_begin(grads, names, tag):
    cc = lax.axis_index("c")
    axes = dict(SHARDED)
    packs = [_pack_rows([_shard(grads[n], axes[n], j) for n in names], BF16, 2 * 256) for j in range(N_CHIPS)]
    g = jnp.stack(packs).reshape(N_CHIPS, 2, -1, PACK_COLS)
    keep = lax.dynamic_index_in_dim(g, cc, axis=1, keepdims=False)
    give = lax.dynamic_index_in_dim(g, 1 - cc, axis=1, keepdims=False)
    got = sibling_swap(give, f"grad_swap_halves_{tag}")
    parts = add_to_bf16(keep, got, f"grad_add_sibling_{tag}")
    send_sems, recv_sems, parts_thru, land_thru, token = exchange_start(parts, f"grad_exchange_start_{tag}")
    shapes = [_shard(grads[n], axes[n], 0).shape for n in names]
    return (send_sems, recv_sems, parts_thru, land_thru, list(names), shapes), token[0, 0]


def _reduce_end(state, after, tag):
    send_sems, recv_sems, parts_thru, land_thru, names, shapes = state
    cc = lax.axis_index("c")
    chip = 2 * lax.axis_index("x") + lax.axis_index("y")
    parts, landed = exchange_wait(send_sems, recv_sems, parts_thru, land_thru, after, f"grad_exchange_wait_{tag}")
    own = lax.dynamic_index_in_dim(parts, chip, axis=0, keepdims=True)
    landed = lax.dynamic_update_slice(landed, own, (chip, 0, 0))
    mine = sum_leading(landed, f"grad_sum_chips_{tag}")
    theirs = sibling_swap(mine, f"grad_swap_sums_{tag}")
    lo = jnp.where(cc == 0, mine, theirs)
    hi = jnp.where(cc == 0, theirs, mine)
    return dict(zip(names, _unpack_rows(jnp.concatenate([lo, hi]), shapes)))


def _reduce_replicated(grads, loss_row):
    names = list(REPLICATED)
    flat = _pack([grads[n] for n in names] + [loss_row], F32, 256 * LANES).reshape(-1, LANES)
    got = all_gather8(flat, "gather_small_grads")
    tot = sum_leading(got, "sum_small_grads").reshape(-1)
    parts = _unpack(tot, [grads[n].shape for n in names] + [loss_row.shape])
    return dict(zip(names, parts[:-1])), parts[-1][0]


def _row(v):
    return v.reshape(1, -1).astype(F32)


def _pad_rows(wt, rows):
    return jnp.pad(wt.astype(F32), ((0, rows - wt.shape[0]), (0, 0)))


def _ffn_fwd(xin, g, wa, wb, cw, cb, wd, tag):
    cwa, cwb = _pad_rows(cw[:, :D_FF], 8), _pad_rows(cw[:, D_FF:], 8)
    xout, xn, hpa, hpb, act = ffn_fwd(xin, _row(g), wa, wb, cwa, cwb, _row(cb[:D_FF]), _row(cb[D_FF:]), wd, tag)
    return xout, (xin, xn, hpa, hpb, act)


def _ffn_bwd(dxout, saved, g, wa, wb, cw, cb, wd, tag, zero=0.0):
    xin, xn, hpa, hpb, act = saved
    d_wd = matmul(act, dxout, ta=True, name=f"{tag}_d_wdown")
    cwa, cwb = _pad_rows(cw[:, :D_FF], 8), _pad_rows(cw[:, D_FF:], 8)
    dxin, dpa, dpb, dwa, dwb, dba, dbb, dg = ffn_bwd(dxout, xin, _row(g), hpa, hpb, wa, wb, cwa, cwb,
                                                     _row(cb[:D_FF]) + zero, _row(cb[D_FF:]), wd, tag + "_bwd")
    d_wu = jnp.concatenate([matmul(xn, dpa, ta=True, name=f"{tag}_d_wup_a"),
                            matmul(xn, dpb, ta=True, name=f"{tag}_d_wup_b")], axis=1)
    taps = lambda t: t.transpose(1, 0, 2).reshape(8, -1)
    d_cw = jnp.concatenate([taps(dwa)[:FFN_K], taps(dwb)[:FFN_K]], axis=1)
    d_cb = jnp.concatenate([taps(dba)[0], taps(dbb)[0]])
    return dxin, dg[0], d_wu, d_cw, d_cb, d_wd


def _step(x, target, w, mom, var):
    s = x.shape[0]
    full, pending, zero = _gather_weights(w)
    cos, sin = rope_tables(s)

    w_in0 = full["l0_w_in"]
    w_in0p = jnp.concatenate([w_in0, jnp.zeros((D_MODEL, H0_W - w_in0.shape[1]), BF16)], axis=1)
    wq = full["l0_w_uq"].reshape(Q_LORA, N_HEADS, QK_NOPE + QK_ROPE)
    zq = lambda n: jnp.zeros((Q_LORA, N_HEADS, n), BF16)
    w_uqp = jnp.concatenate([wq[..., :QK_NOPE], zq(LANES - QK_NOPE), wq[..., QK_NOPE:], zq(LANES - QK_ROPE)],
                            axis=-1).reshape(Q_LORA, N_HEADS * HEAD_PAD)
    w_ukv = full["l0_w_ukv"]
    w_out = full["l0_w_out"]
    w_out_u = w_out[:CONV_WIDTH]
    wo = w_out[CONV_WIDTH:].reshape(N_HEADS, V_DIM, D_MODEL)
    w_out_a = jnp.concatenate([jnp.zeros_like(wo), wo], axis=1).reshape(N_HEADS * LANES, D_MODEL)
    conv_w = _pad_rows(full["l0_conv_w"], CONV_HALO)

    xn0 = rms_fwd(x, _row(w["l0_mix_norm"]) + zero, "l0_mix_rms")
    h0 = matmul(xn0, w_in0p, name="l0_in_proj")
    qn_g, kvn_g = _row(w["l0_q_norm"]), _row(w["l0_kv_norm"])
    u0, cq, ckv, kr = mixpre_fwd(h0, qn_g, kvn_g, cos, sin)
    cb, lg, lb = _row(w["l0_conv_b"]), _row(w["l0_conv_ln_g"]), _row(w["l0_conv_ln_b"])
    u = convln_fwd(u0, conv_w, cb, lg, lb)
    q = q_up_rope(cq, w_uqp, cos, sin)
    kv = matmul(ckv, w_ukv, out_dtype=BF16, name="l0_kv_up")
    o, lse = attn_fwd(q, kv, kr)
    x1 = matmul(u, w_out_u, res=x, name="l0_out_conv")
    x1 = matmul(o, w_out_a, res=x1, name="l0_out_attn")
    full.update(_finish_gather(pending, x1, w))
    w_up0a, w_up0b = full["l0_w_up"][:, :D_FF], full["l0_w_up"][:, D_FF:]
    w_up1a, w_up1b = full["l1_w_up"][:, :D_FF], full["l1_w_up"][:, D_FF:]

    x2, ffn0 = _ffn_fwd(x1, w["l0_ffn_norm"], w_up0a, w_up0b, full["l0_ffn_conv_w"], w["l0_ffn_conv_b"],
                        full["l0_w_down"], "l0_ffn")

    g_, p_, c_ = SSM_GROUPS, SSM_STATE, SSM_GROUP
    s5_in = (w["l1_log_dt"].reshape(g_, 1), w["l1_a_re"], w["l1_a_im"],
             w["l1_b_re"].reshape(g_, p_ * c_), w["l1_b_im"].reshape(g_, p_ * c_))
    lam_r, lam_i, bb_r, bb_i = s5_params_fwd(*s5_in)
    lam_rf, lam_if = lam_r.reshape(1, NS), lam_i.reshape(1, NS)

    def b_blocks(bb):
        t = bb.reshape(NQ, 8, p_, c_).transpose(0, 1, 3, 2)
        return _block_diag(t).astype(BF16)

    def c_blocks(cm):
        t = cm.reshape(NQ, 8, c_, p_).transpose(0, 1, 3, 2)
        return _block_diag(t).astype(BF16)

    bre, bim = b_blocks(bb_r), b_blocks(bb_i)
    cre, cim = c_blocks(w["l1_c_re"]), c_blocks(w["l1_c_im"])
    dskip = _row(w["l1_d"])
    xn2 = rms_fwd(x2, _row(w["l1_mix_norm"]), "l1_mix_rms")
    u1 = matmul(xn2, full["l1_w_in"], name="l1_in_proj")
    xs_r, xs_i, y1, yg = s5_scan_fwd(u1, lam_rf, lam_if, bre, bim, cre, cim, dskip)
    z = matmul(yg, full["l1_w_glu"], bias=_row(w["l1_b_glu"]), out_dtype=BF16, name="l1_glu_proj")
    x3 = glu_res_fwd(z, x2)

    x4, ffn1 = _ffn_fwd(x3, w["l1_ffn_norm"], w_up1a, w_up1b, full["l1_ffn_conv_w"], w["l1_ffn_conv_b"],
                        full["l1_w_down"], "l1_ffn")
    loss_part, dx4, dgf = loss_head(x4, _row(w["final_norm"]), target)

    gr = {"final_norm": dgf[0]}

    dx3, gr["l1_ffn_norm"], gr["l1_w_up"], gr["l1_ffn_conv_w"], gr["l1_ffn_conv_b"], gr["l1_w_down"] = _ffn_bwd(
        dx4, ffn1, w["l1_ffn_norm"], w_up1a, w_up1b, full["l1_ffn_conv_w"], w["l1_ffn_conv_b"], full["l1_w_down"],
        "l1_ffn")

    dz, dbga, dbgb = glu_bwd(z, dx3)
    gr["l1_b_glu"] = jnp.concatenate([dbga[0], dbgb[0]])
    dyg = matmul(dz, full["l1_w_glu"], tb=True, name="l1_d_yg")
    gr["l1_w_glu"] = matmul(yg, dz, ta=True, name="l1_d_wglu")
    a_r, a_i, dy1 = s5_scan_bwd(dyg, y1, lam_rf, lam_if, cre, cim)
    du1, dlr, dli, dbr, dbi, dcr, dci, dd = s5_grads(u1, dy1, xs_r, xs_i, a_r, a_i, bre, bim, dskip)
    gr["l1_d"] = dd[0]

    def b_unblock(d):
        return _block_diag_t(d, c_, p_).transpose(0, 1, 3, 2).reshape(g_, p_ * c_)

    def c_unblock(d):
        return _block_diag_t(d, p_, c_).transpose(0, 1, 3, 2).reshape(g_, c_, p_)

    gr["l1_c_re"], gr["l1_c_im"] = c_unblock(dcr), c_unblock(dci)
    dld, dar, dai, dbre, dbim = s5_params_bwd(*s5_in, dlr[0].reshape(g_, p_), dli[0].reshape(g_, p_),
                                              b_unblock(dbr), b_unblock(dbi))
    gr["l1_log_dt"], gr["l1_a_re"], gr["l1_a_im"] = dld.reshape(g_), dar, dai
    gr["l1_b_re"], gr["l1_b_im"] = dbre.reshape(g_, p_, c_), dbim.reshape(g_, p_, c_)
    dxn2 = matmul(du1, full["l1_w_in"], tb=True, name="l1_d_xn")
    gr["l1_w_in"] = matmul(xn2, du1, ta=True, name="l1_d_win")
    dx2, dg = rms_bwd(x2, _row(w["l1_mix_norm"]), dxn2, dx3, "l1_mix_rms_bwd")
    gr["l1_mix_norm"] = dg[0]
    red_a, zero_a = _reduce_begin(gr, ("l1_w_up", "l1_w_down", "l1_w_glu", "l1_w_in"), "a")

    dx1, gr["l0_ffn_norm"], gr["l0_w_up"], gr["l0_ffn_conv_w"], gr["l0_ffn_conv_b"], gr["l0_w_down"] = _ffn_bwd(
        dx2, ffn0, w["l0_ffn_norm"], w_up0a, w_up0b, full["l0_ffn_conv_w"], w["l0_ffn_conv_b"], full["l0_w_down"],
        "l0_ffn", zero_a)
    red_b, zero_b = _reduce_begin(gr, ("l0_w_up", "l0_w_down"), "b")

    du = matmul(dx1, w_out_u + zero_b.astype(BF16), tb=True, out_dtype=BF16, name="l0_d_u")
    do = matmul(dx1, w_out_a, tb=True, out_dtype=BF16, name="l0_d_o")
    d_wout_u = matmul(u, dx1, ta=True, name="l0_d_wout_u")
    d_wout_a = matmul(o, dx1, ta=True, name="l0_d_wout_a")
    gr["l0_w_out"] = jnp.concatenate(
        [d_wout_u, d_wout_a.reshape(N_HEADS, LANES, D_MODEL)[:, LANES - V_DIM:].reshape(N_HEADS * V_DIM, D_MODEL)])
    dq, dkv, dkr = attn_bwd(q, kv, kr, o, do, lse)
    dqraw = qrope_bwd(dq, cos, sin)
    dcq = matmul(dqraw, w_uqp, tb=True, name="l0_d_cq")
    d_wuqp = matmul(cq, dqraw, ta=True, name="l0_d_wuq").reshape(Q_LORA, N_HEADS, HEAD_PAD)
    gr["l0_w_uq"] = jnp.concatenate([d_wuqp[..., :QK_NOPE], d_wuqp[..., LANES:LANES + QK_ROPE]],
                                    axis=-1).reshape(Q_LORA, -1)
    dckv = matmul(dkv, w_ukv, tb=True, name="l0_d_ckv")
    gr["l0_w_ukv"] = matmul(ckv, dkv, ta=True, name="l0_d_wukv")
    du1c, dlg, dlb, dcb = convln_bwd1(u0, conv_w, cb, lg, lb, du)
    gr["l0_conv_ln_g"], gr["l0_conv_ln_b"], gr["l0_conv_b"] = dlg[0], dlb[0], dcb[0]
    du0, dcw = convln_bwd2(u0, conv_w, du1c)
    gr["l0_conv_w"] = dcw[:CONV_K]
    dh0, dqn, dkvn = mixpre_bwd(h0, qn_g, kvn_g, cos, sin, du0, dcq, dckv, dkr)
    gr["l0_q_norm"], gr["l0_kv_norm"] = dqn[0], dkvn[0]
    dxn0 = matmul(dh0, w_in0p, tb=True, name="l0_d_xn")
    gr["l0_w_in"] = matmul(xn0, dh0, ta=True, name="l0_d_win")[:, :w_in0.shape[1]]
    grad_x, dg = rms_bwd(x, _row(w["l0_mix_norm"]), dxn0, dx1, "l0_mix_rms_bwd")
    gr["l0_mix_norm"] = dg[0]

    rest = [n for n, _ in SHARDED if n not in red_a[-2] + red_b[-2]]
    red_c, _ = _reduce_begin(gr, rest, "c")
    g_sh = {**_reduce_end(red_a, grad_x, "a"), **_reduce_end(red_b, grad_x, "b"), **_reduce_end(red_c, grad_x, "c")}
    g_rep, loss = _reduce_replicated(gr, loss_part[0])
    grad, delta, new_m, new_v = {}, {}, {}, {}
    for n, _ in SHARDED:
        shp = w[n].shape
        two_d = (lambda t: t.reshape(shp[0], -1))
        grad[n] = g_sh[n]
        delta[n], new_m[n], new_v[n] = adamw(two_d(w[n]), two_d(g_sh[n]), two_d(mom[n]), two_d(var[n]), f"adamw_{n}")
    names = list(REPLICATED)
    pk = lambda d: _pack([d[n] for n in names], F32, 256 * LANES).reshape(-1, LANES)
    dl, nm, nv = adamw(pk(w), pk(g_rep), pk(mom), pk(var), "adamw_small")
    shapes = [w[n].shape for n in names]
    for n, d_, m_, v_ in zip(names, _unpack(dl.reshape(-1), shapes), _unpack(nm.reshape(-1), shapes),
                             _unpack(nv.reshape(-1), shapes)):
        grad[n], delta[n], new_m[n], new_v[n] = g_rep[n], d_, m_, v_

    order = ["l0_mix_norm", "l0_w_in", "l0_conv_w", "l0_conv_b", "l0_conv_ln_g", "l0_conv_ln_b", "l0_q_norm",
             "l0_kv_norm", "l0_w_uq", "l0_w_ukv", "l0_w_out", "l0_ffn_norm", "l0_w_up", "l0_ffn_conv_w",
             "l0_ffn_conv_b", "l0_w_down", "l1_mix_norm", "l1_w_in", "l1_log_dt", "l1_a_re", "l1_a_im", "l1_b_re",
             "l1_b_im", "l1_c_re", "l1_c_im", "l1_d", "l1_w_glu", "l1_b_glu", "l1_ffn_norm", "l1_w_up",
             "l1_ffn_conv_w", "l1_ffn_conv_b", "l1_w_down", "final_norm"]
    return (loss, grad_x[None], *[grad[n] for n in order], *[delta[n] for n in order],
            *[new_m[n] for n in order], *[new_v[n] for n in order])
```

```python
import functools
import math

import jax
import jax.numpy as jnp
import numpy as np
from jax import lax
from jax.experimental import pallas as pl
from jax.experimental.pallas import tpu as pltpu

F32 = jnp.float32
BF16 = jnp.bfloat16
MESH = pl.DeviceIdType.MESH

D_MODEL = 1024
EPS = 1e-6
LN_EPS = 1e-5
CONV_WIDTH = 512
CONV_K = 31
N_HEADS = 8
QK_NOPE = 64
QK_ROPE = 32
V_DIM = 64
Q_LORA = 256
KV_LORA = 128
ROPE_BASE = 10000.0
ATT_SCALE = (QK_NOPE + QK_ROPE) ** -0.5
SSM_WIDTH = 512
SSM_GROUP = 16
SSM_GROUPS = 32
SSM_STATE = 64
D_FF = 2816
FFN_K = 3
ADAM_LR = 0.001
ADAM_B1 = 0.9
ADAM_B2 = 0.999
ADAM_EPS = 1e-08
ADAM_WD = 0.01
ADAM_STEP = 10

N_CHIPS = 4
LANES = 128
HEAD_PAD = 256
CONV_HALO = 32
FFN_HALO = 16
VMEM_LIMIT = 56 * 1024 * 1024

ROW_TILE = 512
FFN_ROW_TILE = 1024
FFN_COL_TILE = 256
FFN_ROW_CHUNK = 64
CONV_ROW_CHUNK = 32
FFN_BWD_PARTS = 4
ATT_TILE = 1024
SCAN_TILE = 256
SCAN_UNROLL = 4


def _cparams(*sem):
    return pltpu.CompilerParams(dimension_semantics=tuple(sem), vmem_limit_bytes=VMEM_LIMIT)


def _pick(n, cands):
    for c in cands:
        if n % c == 0:
            return c
    return n


def matmul(a, b, *, ta=False, tb=False, res=None, bias=None, out_dtype=None, name):
    if out_dtype is None:
        out_dtype = BF16 if ta else F32
    if ta:
        kdim, m = a.shape
    else:
        m, kdim = a.shape
    if tb:
        n, k2 = b.shape
    else:
        k2, n = b.shape
    assert kdim == k2, (a.shape, b.shape, ta, tb)
    tn = _pick(n, (1408, 1024, 768, 512, 384, 256, 128))
    if ta:
        tm = _pick(m, (1408, 1024, 512, 256, 128))
        tk = _pick(kdim, (1024, 512, 256, 128))
    else:
        tm = _pick(m, (1024, 512, 256, 128))
        tk = kdim
        if kdim > 1024:
            tn = _pick(n, (512, 256, 128))
        if tm * tn > 1024 * 1024 and out_dtype == F32:
            tm = _pick(m, (512, 256, 128))
    nk = kdim // tk
    has_res, has_bias = res is not None, bias is not None
    dims = (((0,) if ta else (1,), (1,) if tb else (0,)), ((), ()))

    def body(*refs):
        a_ref, b_ref = refs[0], refs[1]
        pos = 2
        res_ref = bias_ref = None
        if has_res:
            res_ref = refs[pos]
            pos += 1
        if has_bias:
            bias_ref = refs[pos]
            pos += 1
        o_ref = refs[pos]

        def finish(r):
            if has_bias:
                r = r + bias_ref[...]
            if has_res:
                r = r + res_ref[...].astype(F32)
            o_ref[...] = r.astype(o_ref.dtype)

        prod = lax.dot_general(a_ref[...].astype(BF16), b_ref[...].astype(BF16), dims, preferred_element_type=F32)
        if nk == 1:
            finish(prod)
            return
        acc_ref = refs[pos + 1]
        k = pl.program_id(2)

        @pl.when(k == 0)
        def _():
            acc_ref[...] = prod

        @pl.when(k > 0)
        def _():
            acc_ref[...] += prod

        @pl.when(k == nk - 1)
        def _():
            finish(acc_ref[...])

    a_spec = pl.BlockSpec((tk, tm), lambda i, j, k: (k, i)) if ta else pl.BlockSpec((tm, tk), lambda i, j, k: (i, k))
    b_spec = pl.BlockSpec((tn, tk), lambda i, j, k: (j, k)) if tb else pl.BlockSpec((tk, tn), lambda i, j, k: (k, j))
    in_specs = [a_spec, b_spec]
    args = [a, b]
    if has_res:
        in_specs.append(pl.BlockSpec((tm, tn), lambda i, j, k: (i, j)))
        args.append(res)
    if has_bias:
        in_specs.append(pl.BlockSpec((1, tn), lambda i, j, k: (0, j)))
        args.append(bias)
    return pl.pallas_call(
        body, name=name, grid=(m // tm, n // tn, nk),
        in_specs=in_specs, out_specs=pl.BlockSpec((tm, tn), lambda i, j, k: (i, j)),
        out_shape=jax.ShapeDtypeStruct((m, n), out_dtype),
        scratch_shapes=[pltpu.VMEM((tm, tn), F32)] if nk > 1 else [],
        compiler_params=_cparams("parallel", "parallel", "arbitrary"),
    )(*args)


def rowcall(body, *, rows, ts, ins, outs, name, scratch=()):
    nt = rows // ts
    in_specs, args = [], []
    for arr, kind in ins:
        if kind == "row":
            in_specs.append(pl.BlockSpec((ts, arr.shape[1]), lambda i: (i, 0)))
        elif kind == "rev":
            in_specs.append(pl.BlockSpec((ts, arr.shape[1]), lambda i: (nt - 1 - i, 0)))
        elif kind == "full":
            nd = arr.ndim
            in_specs.append(pl.BlockSpec(arr.shape, lambda i, nd=nd: (0,) * nd))
        elif kind.startswith("prev:"):
            h = int(kind[5:])
            r = ts // h
            in_specs.append(pl.BlockSpec((h, arr.shape[1]), lambda i, r=r: (jnp.maximum(i * r - 1, 0), 0)))
        elif kind.startswith("next:"):
            h = int(kind[5:])
            r = ts // h
            last = rows // h - 1
            in_specs.append(pl.BlockSpec((h, arr.shape[1]), lambda i, r=r, last=last: (jnp.minimum((i + 1) * r, last), 0)))
        elif kind.startswith("revprev:"):
            h = int(kind[8:])
            r = ts // h
            in_specs.append(pl.BlockSpec((h, arr.shape[1]), lambda i, r=r: (jnp.maximum((nt - 1 - i) * r - 1, 0), 0)))
        else:
            raise ValueError(kind)
        args.append(arr)
    out_specs, out_shapes = [], []
    for shape, dtype, kind in outs:
        if kind == "row":
            out_specs.append(pl.BlockSpec((ts, shape[1]), lambda i: (i, 0)))
        elif kind == "rev":
            out_specs.append(pl.BlockSpec((ts, shape[1]), lambda i: (nt - 1 - i, 0)))
        else:
            nd = len(shape)
            out_specs.append(pl.BlockSpec(tuple(shape), lambda i, nd=nd: (0,) * nd))
        out_shapes.append(jax.ShapeDtypeStruct(tuple(shape), dtype))
    return pl.pallas_call(
        functools.partial(body, nt), name=name, grid=(nt,),
        in_specs=in_specs, out_specs=tuple(out_specs), out_shape=tuple(out_shapes),
        scratch_shapes=list(scratch),
        compiler_params=_cparams("arbitrary"),
    )(*args)


def _rms(x, g):
    return x * lax.rsqrt(jnp.mean(x * x, axis=-1, keepdims=True) + EPS) * g


def _layer_norm(x, g, b):
    mu = jnp.mean(x, axis=-1, keepdims=True)
    xc = x - mu
    var = jnp.mean(xc * xc, axis=-1, keepdims=True)
    return xc * lax.rsqrt(var + LN_EPS) * g + b


def _sigmoid(x):
    return 1.0 / (1.0 + jnp.exp(-x))


def _silu(x):
    return x * _sigmoid(x)


def _gelu(x):
    return 0.5 * x * (1.0 + jnp.tanh(math.sqrt(2.0 / math.pi) * (x + 0.044715 * (x * x * x))))


def _acc(ref, i, val):
    s = jnp.sum(val, axis=0, keepdims=True)

    @pl.when(i == 0)
    def _():
        ref[...] = jnp.zeros_like(ref)

    ref[...] += jnp.broadcast_to(s, ref.shape)


def rms_fwd(x, g, name):
    s, c = x.shape

    def body(nt, x_ref, g_ref, o_ref):
        o_ref[...] = _rms(x_ref[...], g_ref[...]).astype(BF16)

    return rowcall(body, rows=s, ts=min(ROW_TILE, s), ins=[(x, "row"), (g, "full")],
                   outs=[((s, c), BF16, "row")], name=name)[0]


def rms_bwd(x, g, dxn, dres, name):
    s, c = x.shape

    def body(nt, x_ref, g_ref, d_ref, r_ref, dx_ref, dg_ref):
        i = pl.program_id(0)
        _, vjp = jax.vjp(_rms, x_ref[...], g_ref[...])
        dx, dg = vjp(d_ref[...].astype(F32))
        dx_ref[...] = dx + r_ref[...]
        _acc(dg_ref, i, dg)

    return rowcall(body, rows=s, ts=min(ROW_TILE, s),
                   ins=[(x, "row"), (g, "full"), (dxn, "row"), (dres, "row")],
                   outs=[((s, c), F32, "row"), ((8, c), F32, "acc")], name=name)


def _partner(t):
    lane = lax.broadcasted_iota(jnp.int32, t.shape, 1)
    half = QK_ROPE // 2
    return jnp.where(lane % QK_ROPE < half, pltpu.roll(t, LANES - half, 1), pltpu.roll(t, half, 1))


def _rope(t, cos, sin):
    return t * cos + _partner(t) * sin


def _rope_t(d, cos, sin):
    return d * cos + _partner(d * sin)


def rope_tables(s):
    half = QK_ROPE // 2
    inv = ROPE_BASE ** (-jnp.arange(half, dtype=F32) / half)
    ang = jnp.arange(s).astype(F32)[:, None] * inv[None, :]
    cos, sin = jnp.cos(ang), jnp.sin(ang)
    z = jnp.zeros((s, LANES - QK_ROPE), F32)
    return jnp.concatenate([cos, cos, z], axis=1), jnp.concatenate([-sin, sin, z], axis=1)


H0_A, H0_G, H0_Q, H0_KV, H0_KR, H0_W = 0, 512, 1024, 1280, 1408, 1536


def _mixpre_fn(a, g, q, kv, qn, kvn):
    return a * _sigmoid(g), _rms(q, qn), _rms(kv, kvn)


def _h0_parts(h_ref):
    return (h_ref[:, H0_A:H0_G], h_ref[:, H0_G:H0_Q], h_ref[:, H0_Q:H0_KV], h_ref[:, H0_KV:H0_KR])


def mixpre_fwd(h0, qn, kvn, cos, sin):
    s = h0.shape[0]

    def body(nt, h_ref, qn_ref, kvn_ref, cos_ref, sin_ref, u0_ref, cq_ref, ckv_ref, kr_ref):
        u0, cq, ckv = _mixpre_fn(*_h0_parts(h_ref), qn_ref[...], kvn_ref[...])
        u0_ref[...] = u0
        cq_ref[...] = cq.astype(BF16)
        ckv_ref[...] = ckv.astype(BF16)
        kr_ref[...] = _rope(h_ref[:, H0_KR:H0_W], cos_ref[...], sin_ref[...]).astype(BF16)

    return rowcall(body, rows=s, ts=min(ROW_TILE, s),
                   ins=[(h0, "row"), (qn, "full"), (kvn, "full"), (cos, "row"), (sin, "row")],
                   outs=[((s, CONV_WIDTH), F32, "row"), ((s, Q_LORA), BF16, "row"),
                         ((s, KV_LORA), BF16, "row"), ((s, LANES), BF16, "row")], name="mixpre_fwd")


def mixpre_bwd(h0, qn, kvn, cos, sin, du0, dcq, dckv, dkr):
    s = h0.shape[0]

    def body(nt, h_ref, qn_ref, kvn_ref, cos_ref, sin_ref, du0_ref, dcq_ref, dckv_ref, dkr_ref,
             dh_ref, dqn_ref, dkvn_ref):
        i = pl.program_id(0)
        _, vjp = jax.vjp(_mixpre_fn, *_h0_parts(h_ref), qn_ref[...], kvn_ref[...])
        da, dg, dq, dkv, dqn, dkvn = vjp((du0_ref[...], dcq_ref[...], dckv_ref[...]))
        dh_ref[:, H0_A:H0_G] = da.astype(BF16)
        dh_ref[:, H0_G:H0_Q] = dg.astype(BF16)
        dh_ref[:, H0_Q:H0_KV] = dq.astype(BF16)
        dh_ref[:, H0_KV:H0_KR] = dkv.astype(BF16)
        dkr = dkr_ref[:, :LANES]
        for h in range(1, N_HEADS):
            dkr = dkr + dkr_ref[:, h * LANES:(h + 1) * LANES]
        dh_ref[:, H0_KR:H0_W] = _rope_t(dkr, cos_ref[...], sin_ref[...]).astype(BF16)
        _acc(dqn_ref, i, dqn)
        _acc(dkvn_ref, i, dkvn)

    return rowcall(body, rows=s, ts=min(ROW_TILE, s),
                   ins=[(h0, "row"), (qn, "full"), (kvn, "full"), (cos, "row"), (sin, "row"),
                        (du0, "row"), (dcq, "row"), (dckv, "row"), (dkr, "row")],
                   outs=[((s, H0_W), BF16, "row"), ((8, Q_LORA), F32, "acc"), ((8, KV_LORA), F32, "acc")],
                   name="mixpre_bwd")


def _conv_taps(ext_ref, w_ref, ts, first, ntaps, flip=False):
    acc = None
    for k in range(ntaps):
        term = w_ref[pl.ds(ntaps - 1 - k if flip else k, 1), :] * ext_ref[pl.ds(first + k, ts), :]
        acc = term if acc is None else acc + term
    return acc


def _ln_silu(u1, g, b):
    return _silu(_layer_norm(u1, g, b))


SUBLANES = 8


def _fill_shifted(sh_ref, parts, rows):
    pos = 0
    for p in parts:
        sh_ref[0, pl.ds(pos, p.shape[0]), :] = p
        pos += p.shape[0]
    sh_ref[0, pl.ds(rows, SUBLANES), :] = jnp.zeros((SUBLANES, sh_ref.shape[2]), F32)
    for r in range(1, SUBLANES):
        sh_ref[r, pl.ds(0, rows), :] = sh_ref[0, pl.ds(r, rows), :]


def _window(sh_ref, off, n):
    r = off % SUBLANES
    return sh_ref[r, pl.ds(off - r, n), :]


def _taps_aligned(sh_ref, w_ref, n, first, ntaps, flip=False):
    acc = None
    for k in range(ntaps):
        term = w_ref[pl.ds(ntaps - 1 - k if flip else k, 1), :] * _window(sh_ref, first + k, n)
        acc = term if acc is None else acc + term
    return acc


def _conv_scratch(ts, c):
    return pltpu.VMEM((SUBLANES, ts + CONV_HALO + SUBLANES, c), F32)


def convln_fwd(u0, w, b, lg, lb):
    s, c = u0.shape
    ts = min(ROW_TILE, s)
    rc = min(CONV_ROW_CHUNK, ts)
    first = CONV_HALO - (CONV_K - 1)

    def body(nt, cur_ref, prev_ref, w_ref, b_ref, lg_ref, lb_ref, o_ref, sh_ref):
        i = pl.program_id(0)
        _fill_shifted(sh_ref, [jnp.where(i > 0, prev_ref[...], 0.0), cur_ref[...]], ts + CONV_HALO)
        for r0 in range(0, ts, rc):
            u1 = _taps_aligned(sh_ref, w_ref, rc, first + r0, CONV_K) + b_ref[...]
            o_ref[pl.ds(r0, rc), :] = _ln_silu(u1, lg_ref[...], lb_ref[...]).astype(BF16)

    return rowcall(body, rows=s, ts=ts,
                   ins=[(u0, "row"), (u0, f"prev:{CONV_HALO}"), (w, "full"), (b, "full"), (lg, "full"), (lb, "full")],
                   outs=[((s, c), BF16, "row")], name="convln_fwd", scratch=[_conv_scratch(ts, c)])[0]


def convln_bwd1(u0, w, b, lg, lb, du):
    s, c = u0.shape
    ts = min(ROW_TILE, s)
    rc = min(CONV_ROW_CHUNK, ts)
    first = CONV_HALO - (CONV_K - 1)

    def body(nt, cur_ref, prev_ref, w_ref, b_ref, lg_ref, lb_ref, du_ref, du1_ref, dlg_ref, dlb_ref, dcb_ref, sh_ref):
        i = pl.program_id(0)
        _fill_shifted(sh_ref, [jnp.where(i > 0, prev_ref[...], 0.0), cur_ref[...]], ts + CONV_HALO)
        sums = [jnp.zeros((1, c), F32)] * 3
        for r0 in range(0, ts, rc):
            u1 = _taps_aligned(sh_ref, w_ref, rc, first + r0, CONV_K) + b_ref[...]
            _, vjp = jax.vjp(_ln_silu, u1, lg_ref[...], lb_ref[...])
            du1, dlg, dlb = vjp(du_ref[pl.ds(r0, rc), :].astype(F32))
            du1_ref[pl.ds(r0, rc), :] = du1
            parts = (dlg, dlb, jnp.sum(du1, axis=0, keepdims=True))
            sums = [a + jnp.sum(p, axis=0, keepdims=True) for a, p in zip(sums, parts)]
        _acc(dlg_ref, i, sums[0])
        _acc(dlb_ref, i, sums[1])
        _acc(dcb_ref, i, sums[2])

    return rowcall(body, rows=s, ts=ts,
                   ins=[(u0, "row"), (u0, f"prev:{CONV_HALO}"), (w, "full"), (b, "full"), (lg, "full"), (lb, "full"),
                        (du, "row")],
                   outs=[((s, c), F32, "row"), ((8, c), F32, "acc"), ((8, c), F32, "acc"), ((8, c), F32, "acc")],
                   name="convln_bwd1", scratch=[_conv_scratch(ts, c)])


def convln_bwd2(u0, w, du1):
    s, c = u0.shape
    ts = min(ROW_TILE, s)
    rc = min(CONV_ROW_CHUNK, ts)
    first = CONV_HALO - (CONV_K - 1)

    def body(nt, cur_ref, prev_ref, d_ref, dnext_ref, w_ref, du0_ref, dw_ref, sh_ref, dsh_ref):
        i = pl.program_id(0)
        _fill_shifted(sh_ref, [jnp.where(i > 0, prev_ref[...], 0.0), cur_ref[...]], ts + CONV_HALO)
        _fill_shifted(dsh_ref, [d_ref[...], jnp.where(i < nt - 1, dnext_ref[...], 0.0)], ts + CONV_HALO)
        for r0 in range(0, ts, rc):
            du0_ref[pl.ds(r0, rc), :] = _taps_aligned(dsh_ref, w_ref, rc, r0, CONV_K, flip=True)

        @pl.when(i == 0)
        def _():
            dw_ref[...] = jnp.zeros_like(dw_ref)

        for k in range(CONV_K):
            part = jnp.zeros((SUBLANES, c), F32)
            for r0 in range(0, ts, rc):
                prod = d_ref[pl.ds(r0, rc), :] * _window(sh_ref, first + k + r0, rc)
                for a in range(0, rc, SUBLANES):
                    part = part + prod[a:a + SUBLANES]
            dw_ref[pl.ds(k, 1), :] += jnp.sum(part, axis=0, keepdims=True)

    return rowcall(body, rows=s, ts=ts,
                   ins=[(u0, "row"), (u0, f"prev:{CONV_HALO}"), (du1, "row"), (du1, f"next:{CONV_HALO}"), (w, "full")],
                   outs=[((s, c), F32, "row"), ((CONV_HALO, c), F32, "acc")], name="convln_bwd2",
                   scratch=[_conv_scratch(ts, c), _conv_scratch(ts, c)])


def q_up_rope(cq, w_uqp, cos, sin):
    s, kdim = cq.shape
    n = w_uqp.shape[1]
    ts = min(ROW_TILE, s)

    def body(nt, cq_ref, w_ref, cos_ref, sin_ref, o_ref):
        cos_v, sin_v = cos_ref[...] * ATT_SCALE, sin_ref[...] * ATT_SCALE
        for h in range(N_HEADS):
            cols = slice(h * HEAD_PAD, (h + 1) * HEAD_PAD)
            qh = jnp.dot(cq_ref[...], w_ref[:, cols], preferred_element_type=F32)
            o_ref[:, h * HEAD_PAD:h * HEAD_PAD + LANES] = (qh[:, :LANES] * ATT_SCALE).astype(BF16)
            o_ref[:, h * HEAD_PAD + LANES:(h + 1) * HEAD_PAD] = _rope(qh[:, LANES:], cos_v, sin_v).astype(BF16)

    return rowcall(body, rows=s, ts=ts, ins=[(cq, "row"), (w_uqp, "full"), (cos, "row"), (sin, "row")],
                   outs=[((s, n), BF16, "row")], name="l0_q_up_rope")[0]


def qrope_bwd(dq, cos, sin):
    s = dq.shape[0]

    def body(nt, d_ref, cos_ref, sin_ref, o_ref):
        cos_v, sin_v = cos_ref[...] * ATT_SCALE, sin_ref[...] * ATT_SCALE
        for h in range(N_HEADS):
            nope = d_ref[:, h * HEAD_PAD:h * HEAD_PAD + LANES] * ATT_SCALE
            o_ref[:, h * HEAD_PAD:h * HEAD_PAD + LANES] = nope.astype(BF16)
            r = d_ref[:, h * HEAD_PAD + LANES:(h + 1) * HEAD_PAD].astype(F32)
            o_ref[:, h * HEAD_PAD + LANES:(h + 1) * HEAD_PAD] = _rope_t(r, cos_v, sin_v).astype(BF16)

    return rowcall(body, rows=s, ts=min(ROW_TILE, s), ins=[(dq, "row"), (cos, "row"), (sin, "row")],
                   outs=[((s, N_HEADS * HEAD_PAD), BF16, "row")], name="qrope_bwd")[0]


_NT = (((1,), (1,)), ((), ()))
_TN = (((0,), (0,)), ((), ()))


def _scores(q, kvr, diagonal):
    s = lax.dot_general(q, kvr, _NT, preferred_element_type=F32)
    if not diagonal:
        return s
    row = lax.broadcasted_iota(jnp.int32, s.shape, 0)
    col = lax.broadcasted_iota(jnp.int32, s.shape, 1)
    return jnp.where(col <= row, s, -jnp.inf)


def _on_causal_pairs(pair, k_blk, fn):
    @pl.when(k_blk < 2 * pair)
    def _():
        fn(0, False)
        fn(1, False)

    @pl.when(k_blk == 2 * pair)
    def _():
        fn(0, True)
        fn(1, False)

    @pl.when(k_blk == 2 * pair + 1)
    def _():
        fn(1, True)


def attn_fwd(q, kv, kr):
    s = q.shape[0]
    t = min(ATT_TILE, s // 2)
    n = s // t
    np_ = n // 2

    def body(q_ref, kv_ref, kr_ref, o_ref, lse_ref, m_ref, l_ref, acc_ref):
        i, j = pl.program_id(1), pl.program_id(2)

        @pl.when(j == 0)
        def _():
            m_ref[...] = jnp.full_like(m_ref, -jnp.inf)
            l_ref[...] = jnp.zeros_like(l_ref)
            acc_ref[...] = jnp.zeros_like(acc_ref)

        def block(sub, diagonal):
            kvv = kv_ref[...]
            kvr = jnp.concatenate([kvv, kr_ref[...]], axis=1)
            sc = _scores(q_ref[pl.ds(sub * t, t), :], kvr, diagonal)
            m_prev = m_ref[sub]
            m_new = jnp.maximum(m_prev, jnp.max(sc, axis=-1, keepdims=True))
            alpha = jnp.exp(m_prev - m_new)
            p = jnp.exp(sc - m_new)
            l_ref[sub] = alpha * l_ref[sub] + jnp.sum(p, axis=-1, keepdims=True)
            acc_ref[sub] = alpha * acc_ref[sub] + jnp.dot(p.astype(BF16), kvv, preferred_element_type=F32)
            m_ref[sub] = m_new

        _on_causal_pairs(i, j, block)

        @pl.when(j == 2 * i + 1)
        def _():
            for sub in range(2):
                l = l_ref[sub]
                o_ref[pl.ds(sub * t, t), :] = (acc_ref[sub] / l).astype(BF16)
                lse_ref[pl.ds(sub * t, t), :] = jnp.broadcast_to(m_ref[sub] + jnp.log(l), (t, LANES))

    kj = lambda h, i, j: (jnp.minimum(j, 2 * i + 1), h)
    return pl.pallas_call(
        body, name="attn_fwd", grid=(N_HEADS, np_, n),
        in_specs=[pl.BlockSpec((2 * t, HEAD_PAD), lambda h, i, j: (i, h)),
                  pl.BlockSpec((t, LANES), kj),
                  pl.BlockSpec((t, LANES), lambda h, i, j: (jnp.minimum(j, 2 * i + 1), 0))],
        out_specs=(pl.BlockSpec((2 * t, LANES), lambda h, i, j: (i, h)),
                   pl.BlockSpec((2 * t, LANES), lambda h, i, j: (i, h))),
        out_shape=(jax.ShapeDtypeStruct((s, N_HEADS * LANES), BF16),
                   jax.ShapeDtypeStruct((s, N_HEADS * LANES), F32)),
        scratch_shapes=[pltpu.VMEM((2, t, 1), F32), pltpu.VMEM((2, t, 1), F32), pltpu.VMEM((2, t, LANES), F32)],
        compiler_params=_cparams("parallel", "parallel", "arbitrary"),
    )(q, kv, kr)


def attn_bwd(q, kv, kr, o, do, lse):
    s = q.shape[0]
    t = min(ATT_TILE, s // 2)
    n = s // t
    np_ = n // 2

    def body(q_ref, kv_ref, kr_ref, o_ref, do_ref, lse_ref, dq_ref, dkv_ref, dkr_ref):
        j, i = pl.program_id(1), pl.program_id(2)

        @pl.when((i == 0) & (j == 0))
        def _():
            dq_ref[...] = jnp.zeros_like(dq_ref)

        @pl.when(i == 0)
        def _():
            dkv_ref[...] = jnp.zeros_like(dkv_ref)
            dkr_ref[...] = jnp.zeros_like(dkr_ref)

        def block(sub, diagonal):
            sl = pl.ds(sub * t, t)
            qv, dov, kvv = q_ref[sl, :], do_ref[sl, :], kv_ref[...]
            kvr = jnp.concatenate([kvv, kr_ref[...]], axis=1)
            p = jnp.exp(_scores(qv, kvr, diagonal) - lse_ref[sl, :1])
            dp = lax.dot_general(dov, kvv, _NT, preferred_element_type=F32)
            delta = jnp.sum(dov.astype(F32) * o_ref[sl, :].astype(F32), axis=-1, keepdims=True)
            ds = (p * (dp - delta)).astype(BF16)
            dk = lax.dot_general(ds, qv, _TN, preferred_element_type=F32)
            dkv_ref[...] += lax.dot_general(p.astype(BF16), dov, _TN, preferred_element_type=F32) + dk[:, :LANES]
            dkr_ref[...] += dk[:, LANES:]
            rows = pl.ds(pl.multiple_of((2 * i + sub) * t, t), t)
            dq_ref[rows, :] += jnp.dot(ds, kvr, preferred_element_type=F32)

        _on_causal_pairs(i, j, block)

    qi = lambda h, j, i: (jnp.maximum(i, lax.div(j, 2)), h)
    kj = lambda h, j, i: (j, h)
    return pl.pallas_call(
        body, name="attn_bwd", grid=(N_HEADS, n, np_),
        in_specs=[pl.BlockSpec((2 * t, HEAD_PAD), qi), pl.BlockSpec((t, LANES), kj),
                  pl.BlockSpec((t, LANES), lambda h, j, i: (j, 0)),
                  pl.BlockSpec((2 * t, LANES), qi), pl.BlockSpec((2 * t, LANES), qi), pl.BlockSpec((2 * t, LANES), qi)],
        out_specs=(pl.BlockSpec((s, HEAD_PAD), lambda h, j, i: (0, h)),
                   pl.BlockSpec((t, LANES), kj), pl.BlockSpec((t, LANES), kj)),
        out_shape=(jax.ShapeDtypeStruct((s, N_HEADS * HEAD_PAD), F32),
                   jax.ShapeDtypeStruct((s, N_HEADS * LANES), F32), jax.ShapeDtypeStruct((s, N_HEADS * LANES), F32)),
        compiler_params=_cparams("parallel", "arbitrary", "arbitrary"),
    )(q, kv, kr, o, do, lse)


def ffn_fwd(x, g, wa, wb, cwa, cwb, ba, bb, wd, name):
    s, d = x.shape
    f = wa.shape[1]
    ts, tf = min(FFN_ROW_TILE, s), FFN_COL_TILE
    hal = FFN_HALO
    nj = f // tf
    first = hal - (FFN_K - 1)
    rc = min(FFN_ROW_CHUNK, ts)

    def body(x_ref, xp_ref, g_ref, wa_ref, wb_ref, cwa_ref, cwb_ref, ba_ref, bb_ref, wd_ref,
             xo_ref, xn_ref, hpa_ref, hpb_ref, act_ref, xe_ref, ea_ref, eb_ref):
        i, j = pl.program_id(0), pl.program_id(1)

        @pl.when(j == 0)
        def _():
            xn = _rms(x_ref[...], g_ref[...]).astype(BF16)
            xn_ref[...] = xn
            xe_ref[pl.ds(hal, ts), :] = xn
            xe_ref[pl.ds(0, hal), :] = jnp.where(i > 0, _rms(xp_ref[...], g_ref[...]), 0.0).astype(BF16)
            xo_ref[...] = x_ref[...]

        halves = ((0, ts // 2), (ts // 2, ts))
        for lo, hi in halves:
            e0, e1 = (0 if lo == 0 else hal + lo), hal + hi
            xe = xe_ref[pl.ds(e0, e1 - e0), :]
            ea_ref[pl.ds(e0, e1 - e0), :] = jnp.dot(xe, wa_ref[...], preferred_element_type=F32)
            eb_ref[pl.ds(e0, e1 - e0), :] = jnp.dot(xe, wb_ref[...], preferred_element_type=F32)
        for lo, hi in halves:
            hpa_ref[pl.ds(lo, hi - lo), :] = ea_ref[pl.ds(hal + lo, hi - lo), :].astype(BF16)
            hpb_ref[pl.ds(lo, hi - lo), :] = eb_ref[pl.ds(hal + lo, hi - lo), :].astype(BF16)
            for r0 in range(lo, hi, rc):
                ha = _conv_taps(ea_ref, cwa_ref, rc, first + r0, FFN_K) + ba_ref[...]
                hb = _conv_taps(eb_ref, cwb_ref, rc, first + r0, FFN_K) + bb_ref[...]
                act_ref[pl.ds(r0, rc), :] = (_silu(ha) * hb).astype(BF16)
            xo_ref[pl.ds(lo, hi - lo), :] += jnp.dot(act_ref[pl.ds(lo, hi - lo), :], wd_ref[...],
                                                     preferred_element_type=F32)

    r = ts // hal
    row = pl.BlockSpec((ts, d), lambda i, j: (i, 0))
    prev = pl.BlockSpec((hal, d), lambda i, j: (jnp.maximum(i * r - 1, 0), 0))
    gsp = pl.BlockSpec((1, d), lambda i, j: (0, 0))
    wup = pl.BlockSpec((d, tf), lambda i, j: (0, j))
    cwsp = pl.BlockSpec((8, tf), lambda i, j: (0, j))
    bsp = pl.BlockSpec((1, tf), lambda i, j: (0, j))
    wdn = pl.BlockSpec((tf, d), lambda i, j: (j, 0))
    hid = pl.BlockSpec((ts, tf), lambda i, j: (i, j))
    return pl.pallas_call(
        body, name=name, grid=(s // ts, nj),
        in_specs=[row, prev, gsp, wup, wup, cwsp, cwsp, bsp, bsp, wdn],
        out_specs=(row, row, hid, hid, hid),
        out_shape=(jax.ShapeDtypeStruct((s, d), F32), jax.ShapeDtypeStruct((s, d), BF16),
                   jax.ShapeDtypeStruct((s, f), BF16), jax.ShapeDtypeStruct((s, f), BF16),
                   jax.ShapeDtypeStruct((s, f), BF16)),
        scratch_shapes=[pltpu.VMEM((ts + hal, d), BF16), pltpu.VMEM((ts + hal, tf), F32),
                        pltpu.VMEM((ts + hal, tf), F32)],
        compiler_params=_cparams("parallel", "arbitrary"),
    )(x, x, g, wa, wb, cwa, cwb, ba, bb, wd)


def ffn_bwd(dy, x, g, hpa, hpb, wa, wb, cwa, cwb, ba, bb, wd, name):
    s, d = dy.shape
    f = hpa.shape[1]
    ts, tf = min(FFN_ROW_TILE, s), FFN_COL_TILE
    hal = FFN_HALO
    nt, nj = s // ts, f // tf
    te = ts + hal
    first = hal - (FFN_K - 1)
    rc = min(FFN_ROW_CHUNK, ts)

    def body(dy_ref, dyn_ref, x_ref, g_ref, a_ref, ap_ref, an_ref, b_ref, bp_ref, bn_ref, wa_ref, wb_ref,
             cwa_ref, cwb_ref, ba_ref, bb_ref, wd_ref,
             dx_ref, dpa_ref, dpb_ref, dwa_ref, dwb_ref, dba_ref, dbb_ref, dg_ref,
             dye_ref, ea_ref, eb_ref, dact_ref, da_ref, db_ref, dxn_ref):
        i, j = pl.program_id(0), pl.program_id(1)
        last = i == nt - 1

        @pl.when(j == 0)
        def _():
            dye_ref[pl.ds(0, ts), :] = dy_ref[...].astype(BF16)
            dye_ref[pl.ds(ts, hal), :] = jnp.where(last, 0.0, dyn_ref[...]).astype(BF16)
            dxn_ref[...] = jnp.zeros_like(dxn_ref)

        @pl.when((i == 0) & (j == 0))
        def _():
            for r in (dwa_ref, dwb_ref, dba_ref, dbb_ref, dg_ref):
                r[...] = jnp.zeros_like(r)

        halves = tuple((k * ts // FFN_BWD_PARTS, (k + 1) * ts // FFN_BWD_PARTS) for k in range(FFN_BWD_PARTS))
        for lo, hi in halves:
            n = hi - lo + (hal if hi == ts else 0)
            dact_ref[pl.ds(lo, n), :] = lax.dot_general(dye_ref[pl.ds(lo, n), :], wd_ref[...], _NT,
                                                        preferred_element_type=F32)
        for cur, prev, nxt, ext in ((a_ref, ap_ref, an_ref, ea_ref), (b_ref, bp_ref, bn_ref, eb_ref)):
            ext[pl.ds(0, hal), :] = jnp.where(i > 0, prev[...].astype(F32), 0.0)
            ext[pl.ds(hal, ts), :] = cur[...].astype(F32)
            ext[pl.ds(hal + ts, hal), :] = jnp.where(last, 0.0, nxt[...].astype(F32))
        zero = jnp.zeros((1, tf), F32)
        sums = {"ba": zero, "bb": zero, **{("a", k): zero for k in range(FFN_K)}, **{("b", k): zero for k in range(FFN_K)}}
        for r0 in list(range(0, ts, rc)) + [ts]:
            n = rc if r0 < ts else hal
            win_a = [ea_ref[pl.ds(first + r0 + k, n), :] for k in range(FFN_K)]
            win_b = [eb_ref[pl.ds(first + r0 + k, n), :] for k in range(FFN_K)]
            ha = sum(cwa_ref[pl.ds(k, 1), :] * win_a[k] for k in range(FFN_K)) + ba_ref[...]
            hb = sum(cwb_ref[pl.ds(k, 1), :] * win_b[k] for k in range(FFN_K)) + bb_ref[...]
            sig = _sigmoid(ha)
            gs = dact_ref[pl.ds(r0, n), :] * sig
            dha = gs * hb * (1.0 + ha * (1.0 - sig))
            dhb = gs * ha
            da_ref[pl.ds(r0, n), :] = dha
            db_ref[pl.ds(r0, n), :] = dhb
            if r0 < ts:
                sums["ba"] = sums["ba"] + jnp.sum(dha, axis=0, keepdims=True)
                sums["bb"] = sums["bb"] + jnp.sum(dhb, axis=0, keepdims=True)
                for k in range(FFN_K):
                    sums["a", k] = sums["a", k] + jnp.sum(dha * win_a[k], axis=0, keepdims=True)
                    sums["b", k] = sums["b", k] + jnp.sum(dhb * win_b[k], axis=0, keepdims=True)
        for lo, hi in halves:
            for r0 in range(lo, hi, rc):
                dpa_ref[pl.ds(r0, rc), :] = _conv_taps(da_ref, cwa_ref, rc, r0, FFN_K, flip=True).astype(BF16)
                dpb_ref[pl.ds(r0, rc), :] = _conv_taps(db_ref, cwb_ref, rc, r0, FFN_K, flip=True).astype(BF16)
            rows = pl.ds(lo, hi - lo)
            dxn_ref[rows, :] += (lax.dot_general(dpa_ref[rows, :], wa_ref[...], _NT, preferred_element_type=F32)
                                 + lax.dot_general(dpb_ref[rows, :], wb_ref[...], _NT, preferred_element_type=F32))
        dba_ref[j] += jnp.broadcast_to(sums["ba"], (8, tf))
        dbb_ref[j] += jnp.broadcast_to(sums["bb"], (8, tf))
        row = lax.broadcasted_iota(jnp.int32, (8, tf), 0)
        dwa_ref[j] += sum(jnp.where(row == k, sums["a", k], 0.0) for k in range(FFN_K))
        dwb_ref[j] += sum(jnp.where(row == k, sums["b", k], 0.0) for k in range(FFN_K))

        @pl.when(j == nj - 1)
        def _():
            _, vjp = jax.vjp(_rms, x_ref[...], g_ref[...])
            dx, dg = vjp(dxn_ref[...])
            dx_ref[...] = dx + dy_ref[...]
            dg_ref[...] += jnp.broadcast_to(jnp.sum(dg, axis=0, keepdims=True), dg_ref.shape)

    r = ts // hal
    lastblk = s // hal - 1
    row = pl.BlockSpec((ts, d), lambda i, j: (i, 0))
    gsp = pl.BlockSpec((1, d), lambda i, j: (0, 0))
    dgsp = pl.BlockSpec((8, d), lambda i, j: (0, 0))
    rown = pl.BlockSpec((hal, d), lambda i, j: (jnp.minimum((i + 1) * r, lastblk), 0))
    cur = pl.BlockSpec((ts, tf), lambda i, j: (i, j))
    prev = pl.BlockSpec((hal, tf), lambda i, j: (jnp.maximum(i * r - 1, 0), j))
    nxt = pl.BlockSpec((hal, tf), lambda i, j: (jnp.minimum((i + 1) * r, lastblk), j))
    wup = pl.BlockSpec((d, tf), lambda i, j: (0, j))
    cwsp = pl.BlockSpec((8, tf), lambda i, j: (0, j))
    bsp = pl.BlockSpec((1, tf), lambda i, j: (0, j))
    wdn = pl.BlockSpec((tf, d), lambda i, j: (j, 0))
    accsp = pl.BlockSpec((nj, 8, tf), lambda i, j: (0, 0, 0))
    accshape = jax.ShapeDtypeStruct((nj, 8, tf), F32)
    return pl.pallas_call(
        body, name=name, grid=(nt, nj),
        in_specs=[row, rown, row, gsp, cur, prev, nxt, cur, prev, nxt, wup, wup, cwsp, cwsp, bsp, bsp, wdn],
        out_specs=(row, cur, cur, accsp, accsp, accsp, accsp, dgsp),
        out_shape=(jax.ShapeDtypeStruct((s, d), F32), jax.ShapeDtypeStruct((s, f), BF16),
                   jax.ShapeDtypeStruct((s, f), BF16), accshape, accshape, accshape, accshape,
                   jax.ShapeDtypeStruct((8, d), F32)),
        scratch_shapes=[pltpu.VMEM((te, d), BF16), pltpu.VMEM((ts + 2 * hal, tf), F32),
                        pltpu.VMEM((ts + 2 * hal, tf), F32), pltpu.VMEM((te, tf), F32),
                        pltpu.VMEM((te, tf), F32), pltpu.VMEM((te, tf), F32), pltpu.VMEM((ts, d), F32)],
        compiler_params=_cparams("arbitrary", "arbitrary"),
    )(dy, dy, x, g, hpa, hpa, hpa, hpb, hpb, hpb, wa, wb, cwa, cwb, ba, bb, wd)


NQ = 4
SQ = SSM_STATE * 8
NS = SSM_GROUPS * SSM_STATE


def _s5_disc(log_dt, a_re, a_im, b_re, b_im, expand):
    dt = jnp.exp(log_dt)
    mag = jnp.exp(a_re * dt)
    lb_re, lb_im = mag * jnp.cos(a_im * dt), mag * jnp.sin(a_im * dt)
    den = a_re * a_re + a_im * a_im
    nr, ni = lb_re - 1.0, lb_im
    f_re = (nr * a_re + ni * a_im) / den
    f_im = (ni * a_re - nr * a_im) / den
    fe_re = jnp.dot(f_re, expand, precision=lax.Precision.HIGHEST, preferred_element_type=F32)
    fe_im = jnp.dot(f_im, expand, precision=lax.Precision.HIGHEST, preferred_element_type=F32)
    return lb_re, lb_im, fe_re * b_re - fe_im * b_im, fe_re * b_im + fe_im * b_re


def _expand_matrix():
    e = np.zeros((SSM_STATE, SSM_STATE * SSM_GROUP), np.float32)
    for p in range(SSM_STATE):
        e[p, p * SSM_GROUP:(p + 1) * SSM_GROUP] = 1.0
    return jnp.asarray(e)


def s5_params_fwd(log_dt, a_re, a_im, b_re, b_im):
    expand = _expand_matrix()

    def body(ld_ref, ar_ref, ai_ref, br_ref, bi_ref, e_ref, lr_ref, li_ref, bbr_ref, bbi_ref):
        lr, li, bbr, bbi = _s5_disc(ld_ref[...], ar_ref[...], ai_ref[...], br_ref[...], bi_ref[...], e_ref[...])
        lr_ref[...] = lr
        li_ref[...] = li
        bbr_ref[...] = bbr
        bbi_ref[...] = bbi

    g, p, pc = SSM_GROUPS, SSM_STATE, SSM_STATE * SSM_GROUP
    return pl.pallas_call(
        body, name="s5_params_fwd",
        out_shape=(jax.ShapeDtypeStruct((g, p), F32), jax.ShapeDtypeStruct((g, p), F32),
                   jax.ShapeDtypeStruct((g, pc), F32), jax.ShapeDtypeStruct((g, pc), F32)),
    )(log_dt, a_re, a_im, b_re, b_im, expand)


def s5_params_bwd(log_dt, a_re, a_im, b_re, b_im, dlr, dli, dbbr, dbbi):
    expand = _expand_matrix()

    def body(ld_ref, ar_ref, ai_ref, br_ref, bi_ref, e_ref, dlr_ref, dli_ref, dbbr_ref, dbbi_ref,
             dld_ref, dar_ref, dai_ref, dbr_ref, dbi_ref):
        e = e_ref[...]
        f = lambda ld, ar, ai, br, bi: _s5_disc(ld, ar, ai, br, bi, e)
        _, vjp = jax.vjp(f, ld_ref[...], ar_ref[...], ai_ref[...], br_ref[...], bi_ref[...])
        dld, dar, dai, dbr, dbi = vjp((dlr_ref[...], dli_ref[...], dbbr_ref[...], dbbi_ref[...]))
        dld_ref[...] = dld
        dar_ref[...] = dar
        dai_ref[...] = dai
        dbr_ref[...] = dbr
        dbi_ref[...] = dbi

    g, p, pc = SSM_GROUPS, SSM_STATE, SSM_STATE * SSM_GROUP
    return pl.pallas_call(
        body, name="s5_params_bwd",
        out_shape=(jax.ShapeDtypeStruct((g, 1), F32), jax.ShapeDtypeStruct((g, p), F32),
                   jax.ShapeDtypeStruct((g, p), F32), jax.ShapeDtypeStruct((g, pc), F32),
                   jax.ShapeDtypeStruct((g, pc), F32)),
    )(log_dt, a_re, a_im, b_re, b_im, expand, dlr, dli, dbbr, dbbi)


def _cmul(ar, ai, br, bi):
    return ar * br - ai * bi, ar * bi + ai * br


def _power_rows(lr, li, conj_rev):
    row = lax.broadcasted_iota(jnp.int32, (8, NS), 0)
    tr = jnp.zeros((8, NS), F32)
    ti = jnp.zeros((8, NS), F32)
    pr, pi = lr, li
    for r in range(8):
        dst = 7 - r if conj_rev else r
        tr = jnp.where(row == dst, pr, tr)
        ti = jnp.where(row == dst, -pi if conj_rev else pi, ti)
        if r < 7:
            pr, pi = _cmul(pr, pi, lr, li)
    return tr, ti


def _scan8(xr, xi, tr_ref, ti_ref, cr, ci, reverse):
    row = lax.broadcasted_iota(jnp.int32, xr.shape, 0)
    for d in (1, 2, 4):
        if reverse:
            sr, si = pltpu.roll(xr, 8 - d, 0), pltpu.roll(xi, 8 - d, 0)
            keep = row < 8 - d
            pw = 8 - d
        else:
            sr, si = pltpu.roll(xr, d, 0), pltpu.roll(xi, d, 0)
            keep = row >= d
            pw = d - 1
        mr, mi = _cmul(tr_ref[pl.ds(pw, 1), :], ti_ref[pl.ds(pw, 1), :], sr, si)
        xr = xr + jnp.where(keep, mr, 0.0)
        xi = xi + jnp.where(keep, mi, 0.0)
    mr, mi = _cmul(tr_ref[...], ti_ref[...], cr, ci)
    return xr + mr, xi + mi


def _row_of(x, r):
    row = lax.broadcasted_iota(jnp.int32, x.shape, 0)
    return jnp.sum(jnp.where(row == r, x, 0.0), axis=0, keepdims=True)


def s5_scan_fwd(u, lam_r, lam_i, bre, bim, cre, cim, dskip):
    s = u.shape[0]
    tt = min(SCAN_TILE, s)
    nb = tt // 8

    def body(nt, u_ref, lr_ref, li_ref, bre_ref, bim_ref, cre_ref, cim_ref, d_ref,
             xr_ref, xi_ref, y_ref, yg_ref, tr_ref, ti_ref, cr_ref, ci_ref):
        i = pl.program_id(0)

        @pl.when(i == 0)
        def _():
            tr, ti = _power_rows(lr_ref[...], li_ref[...], False)
            tr_ref[...] = tr
            ti_ref[...] = ti
            cr_ref[...] = jnp.zeros_like(cr_ref)
            ci_ref[...] = jnp.zeros_like(ci_ref)

        uv = u_ref[...]
        ub = uv.astype(BF16)
        for q in range(NQ):
            uq = ub[:, q * LANES:(q + 1) * LANES]
            xr_ref[:, q * SQ:(q + 1) * SQ] = jnp.dot(uq, bre_ref[q], preferred_element_type=F32)
            xi_ref[:, q * SQ:(q + 1) * SQ] = jnp.dot(uq, bim_ref[q], preferred_element_type=F32)

        def step(b, carry):
            cr, ci = carry
            rows = pl.ds(pl.multiple_of(b * 8, 8), 8)
            xr, xi = _scan8(xr_ref[rows, :], xi_ref[rows, :], tr_ref, ti_ref, cr, ci, False)
            xr_ref[rows, :] = xr
            xi_ref[rows, :] = xi
            return _row_of(xr, 7), _row_of(xi, 7)

        cr, ci = lax.fori_loop(0, nb, step, (cr_ref[...], ci_ref[...]), unroll=min(SCAN_UNROLL, nb))
        cr_ref[...] = cr
        ci_ref[...] = ci
        y = d_ref[...] * uv
        for q in range(NQ):
            yq = (jnp.dot(xr_ref[:, q * SQ:(q + 1) * SQ].astype(BF16), cre_ref[q], preferred_element_type=F32)
                  - jnp.dot(xi_ref[:, q * SQ:(q + 1) * SQ].astype(BF16), cim_ref[q], preferred_element_type=F32))
            y_ref[:, q * LANES:(q + 1) * LANES] = yq + y[:, q * LANES:(q + 1) * LANES]
        yg_ref[...] = _gelu(y_ref[...]).astype(BF16)

    return rowcall(body, rows=s, ts=tt,
                   ins=[(u, "row"), (lam_r, "full"), (lam_i, "full"), (bre, "full"), (bim, "full"),
                        (cre, "full"), (cim, "full"), (dskip, "full")],
                   outs=[((s, NS), F32, "row"), ((s, NS), F32, "row"), ((s, SSM_WIDTH), F32, "row"),
                         ((s, SSM_WIDTH), BF16, "row")], name="s5_scan_fwd",
                   scratch=[pltpu.VMEM((8, NS), F32), pltpu.VMEM((8, NS), F32),
                            pltpu.VMEM((1, NS), F32), pltpu.VMEM((1, NS), F32)])


def s5_scan_bwd(dyg, y, lam_r, lam_i, cre, cim):
    s = y.shape[0]
    tt = min(SCAN_TILE, s)
    nb = tt // 8

    def body(nt, dyg_ref, y_ref, lr_ref, li_ref, cre_ref, cim_ref,
             ar_ref, ai_ref, dy_ref, tr_ref, ti_ref, cr_ref, ci_ref):
        i = pl.program_id(0)

        @pl.when(i == 0)
        def _():
            tr, ti = _power_rows(lr_ref[...], li_ref[...], True)
            tr_ref[...] = tr
            ti_ref[...] = ti
            cr_ref[...] = jnp.zeros_like(cr_ref)
            ci_ref[...] = jnp.zeros_like(ci_ref)

        _, vjp = jax.vjp(_gelu, y_ref[...])
        dy = vjp(dyg_ref[...])[0]
        dyb = dy.astype(BF16)
        dy_ref[...] = dyb
        for q in range(NQ):
            dq = dyb[:, q * LANES:(q + 1) * LANES]
            ar_ref[:, q * SQ:(q + 1) * SQ] = lax.dot_general(dq, cre_ref[q], _NT, preferred_element_type=F32)
            ai_ref[:, q * SQ:(q + 1) * SQ] = -lax.dot_general(dq, cim_ref[q], _NT, preferred_element_type=F32)

        def step(b, carry):
            cr, ci = carry
            rows = pl.ds(pl.multiple_of((nb - 1 - b) * 8, 8), 8)
            xr, xi = _scan8(ar_ref[rows, :], ai_ref[rows, :], tr_ref, ti_ref, cr, ci, True)
            ar_ref[rows, :] = xr
            ai_ref[rows, :] = xi
            return _row_of(xr, 0), _row_of(xi, 0)

        cr, ci = lax.fori_loop(0, nb, step, (cr_ref[...], ci_ref[...]), unroll=min(SCAN_UNROLL, nb))
        cr_ref[...] = cr
        ci_ref[...] = ci

    return rowcall(body, rows=s, ts=tt,
                   ins=[(dyg, "rev"), (y, "rev"), (lam_r, "full"), (lam_i, "full"), (cre, "full"), (cim, "full")],
                   outs=[((s, NS), F32, "rev"), ((s, NS), F32, "rev"), ((s, SSM_WIDTH), BF16, "rev")],
                   name="s5_scan_bwd",
                   scratch=[pltpu.VMEM((8, NS), F32), pltpu.VMEM((8, NS), F32),
                            pltpu.VMEM((1, NS), F32), pltpu.VMEM((1, NS), F32)])


def s5_grads(u, dy, xr, xi, ar, ai, bre, bim, dskip):
    s = u.shape[0]
    tt = min(SCAN_TILE, s)

    def body(nt, u_ref, dy_ref, xr_ref, xrp_ref, xi_ref, xip_ref, ar_ref, ai_ref, bre_ref, bim_ref, d_ref,
             du_ref, dlr_ref, dli_ref, dbr_ref, dbi_ref, dcr_ref, dci_ref, dd_ref, er_ref, ei_ref):
        i = pl.program_id(0)

        @pl.when(i == 0)
        def _():
            for r in (dbr_ref, dbi_ref, dcr_ref, dci_ref):
                r[...] = jnp.zeros_like(r)

        uv, dyb = u_ref[...], dy_ref[...]
        dyf = dyb.astype(F32)
        av_r, av_i, xv_r, xv_i = ar_ref[...], ai_ref[...], xr_ref[...], xi_ref[...]
        er_ref[pl.ds(0, 8), :] = jnp.where(i > 0, xrp_ref[...], 0.0)
        ei_ref[pl.ds(0, 8), :] = jnp.where(i > 0, xip_ref[...], 0.0)
        er_ref[pl.ds(8, tt), :] = xv_r
        ei_ref[pl.ds(8, tt), :] = xv_i
        sr, si = er_ref[pl.ds(7, tt), :], ei_ref[pl.ds(7, tt), :]
        _acc(dlr_ref, i, av_r * sr + av_i * si)
        _acc(dli_ref, i, av_i * sr - av_r * si)
        _acc(dd_ref, i, dyf * uv)
        ub = uv.astype(BF16)
        ab_r, ab_i = av_r.astype(BF16), av_i.astype(BF16)
        xb_r, xb_i = xv_r.astype(BF16), xv_i.astype(BF16)
        du = d_ref[...] * dyf
        for q in range(NQ):
            cs, ss = slice(q * LANES, (q + 1) * LANES), slice(q * SQ, (q + 1) * SQ)
            dbr_ref[q] += lax.dot_general(ub[:, cs], ab_r[:, ss], _TN, preferred_element_type=F32)
            dbi_ref[q] += lax.dot_general(ub[:, cs], ab_i[:, ss], _TN, preferred_element_type=F32)
            dcr_ref[q] += lax.dot_general(xb_r[:, ss], dyb[:, cs], _TN, preferred_element_type=F32)
            dci_ref[q] -= lax.dot_general(xb_i[:, ss], dyb[:, cs], _TN, preferred_element_type=F32)
            du_ref[:, cs] = (du[:, cs]
                             + lax.dot_general(ab_r[:, ss], bre_ref[q], _NT, preferred_element_type=F32)
                             + lax.dot_general(ab_i[:, ss], bim_ref[q], _NT, preferred_element_type=F32))

    return rowcall(body, rows=s, ts=tt,
                   ins=[(u, "row"), (dy, "row"), (xr, "row"), (xr, "prev:8"), (xi, "row"), (xi, "prev:8"),
                        (ar, "row"), (ai, "row"), (bre, "full"), (bim, "full"), (dskip, "full")],
                   outs=[((s, SSM_WIDTH), F32, "row"), ((8, NS), F32, "acc"), ((8, NS), F32, "acc"),
                         ((NQ, LANES, SQ), F32, "acc"), ((NQ, LANES, SQ), F32, "acc"),
                         ((NQ, SQ, LANES), F32, "acc"), ((NQ, SQ, LANES), F32, "acc"),
                         ((8, SSM_WIDTH), F32, "acc")], name="s5_grads",
                   scratch=[pltpu.VMEM((tt + 8, NS), F32), pltpu.VMEM((tt + 8, NS), F32)])


def _glu_fn(za, zb):
    return za * _sigmoid(zb)


def glu_proj_res(yg, w_glu, b_glu, xres):
    s = yg.shape[0]

    def body(nt, y_ref, w_ref, b_ref, x_ref, o_ref, z_ref):
        yv = y_ref[...]
        za = jnp.dot(yv, w_ref[:, :D_MODEL], preferred_element_type=F32) + b_ref[:, :D_MODEL]
        zb = jnp.dot(yv, w_ref[:, D_MODEL:], preferred_element_type=F32) + b_ref[:, D_MODEL:]
        z_ref[:, :D_MODEL] = za.astype(BF16)
        z_ref[:, D_MODEL:] = zb.astype(BF16)
        o_ref[...] = x_ref[...] + _glu_fn(za, zb)

    return rowcall(body, rows=s, ts=min(ROW_TILE, s),
                   ins=[(yg, "row"), (w_glu, "full"), (b_glu, "full"), (xres, "row")],
                   outs=[((s, D_MODEL), F32, "row"), ((s, 2 * D_MODEL), BF16, "row")], name="l1_glu_proj_res")


def glu_bwd(z, dout):
    s, c = z.shape

    def body(nt, z_ref, d_ref, dz_ref, dba_ref, dbb_ref):
        i = pl.program_id(0)
        _, vjp = jax.vjp(_glu_fn, z_ref[:, :D_MODEL].astype(F32), z_ref[:, D_MODEL:].astype(F32))
        dza, dzb = vjp(d_ref[...])
        dz_ref[:, :D_MODEL] = dza.astype(BF16)
        dz_ref[:, D_MODEL:] = dzb.astype(BF16)
        _acc(dba_ref, i, dza)
        _acc(dbb_ref, i, dzb)

    return rowcall(body, rows=s, ts=min(ROW_TILE, s), ins=[(z, "row"), (dout, "row")],
                   outs=[((s, c), BF16, "row"), ((8, D_MODEL), F32, "acc"), ((8, D_MODEL), F32, "acc")],
                   name="glu_bwd")


def loss_head(x, g, target):
    s, c = x.shape

    def body(nt, x_ref, g_ref, t_ref, loss_ref, dx_ref, dg_ref):
        i = pl.program_id(0)
        y, vjp = jax.vjp(_rms, x_ref[...], g_ref[...])
        err = y - t_ref[...]
        dx, dg = vjp(err * (1.0 / c))
        dx_ref[...] = dx
        _acc(dg_ref, i, dg)
        part = jnp.sum(jnp.sum(err * err, axis=-1, keepdims=True), axis=0, keepdims=True) * (0.5 / c)

        @pl.when(i == 0)
        def _():
            loss_ref[...] = jnp.zeros_like(loss_ref)

        loss_ref[...] += jnp.broadcast_to(part, loss_ref.shape)

    return rowcall(body, rows=s, ts=min(ROW_TILE, s), ins=[(x, "row"), (g, "full"), (target, "row")],
                   outs=[((8, LANES), F32, "acc"), ((s, c), F32, "row"), ((8, c), F32, "acc")], name="loss_head")


def _tile_rows(r, cands=(512, 256, 128, 64, 32, 16, 8)):
    return _pick(r, cands)


def add_to_bf16(a, b, name):
    n, r, c = a.shape
    tr = _tile_rows(r)

    def body(a_ref, b_ref, o_ref):
        o_ref[...] = (a_ref[...].astype(F32) + b_ref[...].astype(F32)).astype(BF16)

    spec = pl.BlockSpec((1, tr, c), lambda j, i: (j, i, 0))
    return pl.pallas_call(body, name=name, grid=(n, r // tr), in_specs=[spec, spec], out_specs=spec,
                          out_shape=jax.ShapeDtypeStruct((n, r, c), BF16),
                          compiler_params=_cparams("parallel", "parallel"))(a, b)


def sum_leading(a, name):
    n, r, c = a.shape
    tr = _tile_rows(r)

    def body(a_ref, o_ref):
        acc = a_ref[0].astype(F32)
        for k in range(1, n):
            acc = acc + a_ref[k].astype(F32)
        o_ref[...] = acc

    return pl.pallas_call(body, name=name, grid=(r // tr,),
                          in_specs=[pl.BlockSpec((n, tr, c), lambda i: (0, i, 0))],
                          out_specs=pl.BlockSpec((tr, c), lambda i: (i, 0)),
                          out_shape=jax.ShapeDtypeStruct((r, c), F32),
                          compiler_params=_cparams("parallel"))(a)


def adamw(w, g, m, v, name):
    r, c = w.shape
    tr = _tile_rows(r, (256, 128, 64, 32, 16, 8))
    c1 = 1.0 - ADAM_B1 ** ADAM_STEP
    c2 = 1.0 - ADAM_B2 ** ADAM_STEP

    def body(w_ref, g_ref, m_ref, v_ref, d_ref, nm_ref, nv_ref):
        gv = g_ref[...]
        mn = ADAM_B1 * m_ref[...] + (1.0 - ADAM_B1) * gv
        vn = ADAM_B2 * v_ref[...] + (1.0 - ADAM_B2) * (gv * gv)
        d_ref[...] = -ADAM_LR * ((mn / c1) / (jnp.sqrt(vn / c2) + ADAM_EPS) + ADAM_WD * w_ref[...])
        nm_ref[...] = mn
        nv_ref[...] = vn

    spec = pl.BlockSpec((tr, c), lambda i: (i, 0))
    shp = jax.ShapeDtypeStruct((r, c), F32)
    return pl.pallas_call(body, name=name, grid=(r // tr,), in_specs=[spec] * 4, out_specs=(spec,) * 3,
                          out_shape=(shp,) * 3, compiler_params=_cparams("parallel"))(w, g, m, v)


_ANY = pl.BlockSpec(memory_space=pl.ANY)


def all_gather8(block, name):
    r, c = block.shape

    def body(x_ref, out_ref, send_sems, recv_sems, local_sem):
        x, y, cc = lax.axis_index("x"), lax.axis_index("y"), lax.axis_index("c")
        me, sibling = (x, y, cc), (x, y, 1 - cc)
        chips = [(1 - x, y), (x, 1 - y), (1 - x, 1 - y)]

        def slot(px, py, pc):
            return out_ref.at[4 * px + 2 * py + pc]

        def copy(k, blk, to, src=None):
            return pltpu.make_async_remote_copy(
                src_ref=slot(*blk) if src is None else src, dst_ref=slot(*blk),
                send_sem=send_sems.at[k], recv_sem=recv_sems.at[k], device_id=to, device_id_type=MESH)

        mine = pltpu.make_async_copy(x_ref, slot(*me), local_sem)
        mine.start()
        first = [copy(0, me, sibling, src=x_ref)]
        first += [copy(1 + j, me, (*chip, cc), src=x_ref) for j, chip in enumerate(chips)]
        for cp in first:
            cp.start()
        passed = [copy(4 + j, (*chip, cc), sibling) for j, chip in enumerate(chips)]
        for j, chip in enumerate(chips):
            copy(1 + j, (*chip, cc), me).wait_recv()
            passed[j].start()
        copy(0, sibling, me).wait_recv()
        for j, chip in enumerate(chips):
            copy(4 + j, (*chip, 1 - cc), me).wait_recv()
        for cp in first + passed:
            cp.wait_send()
        mine.wait()

    return pl.pallas_call(
        body, name=name, in_specs=[_ANY], out_specs=_ANY,
        out_shape=jax.ShapeDtypeStruct((8, r, c), block.dtype),
        scratch_shapes=[pltpu.SemaphoreType.DMA((7,)), pltpu.SemaphoreType.DMA((7,)), pltpu.SemaphoreType.DMA],
    )(block)


_HBM = pl.BlockSpec(memory_space=pltpu.HBM)
_SEM = pl.BlockSpec(memory_space=pltpu.SEMAPHORE)
_DATAFLOW = pltpu.SideEffectType.DATAFLOW_SIDE_EFFECTING
N_REMOTE = 6


def _remote_peers(x, y, cc):
    return [(1 - x, y, cc), (x, 1 - y, cc), (1 - x, 1 - y, cc),
            (1 - x, y, 1 - cc), (x, 1 - y, 1 - cc), (1 - x, 1 - y, 1 - cc)]


def gather_start(block, after, name):
    r, c = block.shape

    def body(x_ref, land_ref, after_ref, send_sems, recv_sems, x_thru, land_thru, token):
        x, y, cc = lax.axis_index("x"), lax.axis_index("y"), lax.axis_index("c")
        for k, peer in enumerate(_remote_peers(x, y, cc)):
            pltpu.make_async_remote_copy(src_ref=x_ref, dst_ref=land_ref.at[4 * x + 2 * y + cc],
                                         send_sem=send_sems.at[k], recv_sem=recv_sems.at[k],
                                         device_id=peer, device_id_type=MESH).start()
        token[...] = jnp.zeros_like(token)

    land = pltpu.with_memory_space_constraint(lax.empty((8, r, c), block.dtype), pltpu.HBM)
    return pl.pallas_call(
        body, name=name,
        out_shape=(pltpu.SemaphoreType.DMA((N_REMOTE,)), pltpu.SemaphoreType.DMA((N_REMOTE,)),
                   pltpu.HBM((r, c), block.dtype), pltpu.HBM((8, r, c), block.dtype),
                   jax.ShapeDtypeStruct((8, LANES), F32)),
        in_specs=(_HBM, _HBM, _ANY), out_specs=(_SEM, _SEM, _HBM, _HBM, pl.BlockSpec(memory_space=pltpu.VMEM)),
        input_output_aliases={0: 2, 1: 3},
        compiler_params=pltpu.CompilerParams(has_side_effects=_DATAFLOW),
    )(pltpu.with_memory_space_constraint(block, pltpu.HBM), land, after)


def gather_wait(send_sems, recv_sems, block_thru, land_thru, after, name):
    def body(x_ref, land_ref, send_sems, recv_sems, after_ref, x_dead, got_ref):
        x, y, cc = lax.axis_index("x"), lax.axis_index("y"), lax.axis_index("c")
        for k, (px, py, pc) in enumerate(_remote_peers(x, y, cc)):
            cp = pltpu.make_async_remote_copy(src_ref=x_ref, dst_ref=land_ref.at[4 * px + 2 * py + pc],
                                              send_sem=send_sems.at[k], recv_sem=recv_sems.at[k],
                                              device_id=(px, py, pc), device_id_type=MESH)
            cp.wait_send()
            cp.wait_recv()

    return pl.pallas_call(
        body, name=name,
        out_shape=(pltpu.HBM(block_thru.shape, block_thru.dtype), pltpu.HBM(land_thru.shape, land_thru.dtype)),
        in_specs=(_HBM, _HBM, _SEM, _SEM, _ANY), out_specs=(_HBM, _HBM), input_output_aliases={0: 0, 1: 1},
        compiler_params=pltpu.CompilerParams(has_side_effects=_DATAFLOW),
    )(block_thru, land_thru, send_sems, recv_sems, after)[1]


def _chip_peers(x, y):
    return [(1 - x, y), (x, 1 - y), (1 - x, 1 - y)]


def exchange_start(parts, name):
    def body(p_ref, land_ref, send_sems, recv_sems, p_thru, land_thru, token):
        x, y, cc = lax.axis_index("x"), lax.axis_index("y"), lax.axis_index("c")
        for k, (px, py) in enumerate(_chip_peers(x, y)):
            pltpu.make_async_remote_copy(src_ref=p_ref.at[2 * px + py], dst_ref=land_ref.at[2 * x + y],
                                         send_sem=send_sems.at[k], recv_sem=recv_sems.at[k],
                                         device_id=(px, py, cc), device_id_type=MESH).start()
        token[...] = jnp.zeros_like(token)

    land = pltpu.with_memory_space_constraint(lax.empty(parts.shape, parts.dtype), pltpu.HBM)
    return pl.pallas_call(
        body, name=name,
        out_shape=(pltpu.SemaphoreType.DMA((3,)), pltpu.SemaphoreType.DMA((3,)),
                   pltpu.HBM(parts.shape, parts.dtype), pltpu.HBM(parts.shape, parts.dtype),
                   jax.ShapeDtypeStruct((8, LANES), F32)),
        in_specs=(_HBM, _HBM), out_specs=(_SEM, _SEM, _HBM, _HBM, pl.BlockSpec(memory_space=pltpu.VMEM)),
        input_output_aliases={0: 2, 1: 3},
        compiler_params=pltpu.CompilerParams(has_side_effects=_DATAFLOW),
    )(pltpu.with_memory_space_constraint(parts, pltpu.HBM), land)


def exchange_wait(send_sems, recv_sems, parts_thru, land_thru, after, name):
    def body(p_ref, land_ref, send_sems, recv_sems, after_ref, p_dead, got_ref):
        x, y, cc = lax.axis_index("x"), lax.axis_index("y"), lax.axis_index("c")
        for k, (px, py) in enumerate(_chip_peers(x, y)):
            cp = pltpu.make_async_remote_copy(src_ref=p_ref.at[2 * px + py], dst_ref=land_ref.at[2 * px + py],
                                              send_sem=send_sems.at[k], recv_sem=recv_sems.at[k],
                                              device_id=(px, py, cc), device_id_type=MESH)
            cp.wait_send()
            cp.wait_recv()

    return pl.pallas_call(
        body, name=name,
        out_shape=(pltpu.HBM(parts_thru.shape, parts_thru.dtype), pltpu.HBM(land_thru.shape, land_thru.dtype)),
        in_specs=(_HBM, _HBM, _SEM, _SEM, _ANY), out_specs=(_HBM, _HBM), input_output_aliases={0: 0, 1: 1},
        compiler_params=pltpu.CompilerParams(has_side_effects=_DATAFLOW),
    )(parts_thru, land_thru, send_sems, recv_sems, after)


def sibling_swap(block, name):
    def body(x_ref, out_ref, send_sem, recv_sem):
        x, y, cc = lax.axis_index("x"), lax.axis_index("y"), lax.axis_index("c")
        cp = pltpu.make_async_remote_copy(src_ref=x_ref, dst_ref=out_ref, send_sem=send_sem, recv_sem=recv_sem,
                                          device_id=(x, y, 1 - cc), device_id_type=MESH)
        cp.start()
        cp.wait()

    return pl.pallas_call(
        body, name=name, in_specs=[_ANY], out_specs=_ANY,
        out_shape=jax.ShapeDtypeStruct(block.shape, block.dtype),
        scratch_shapes=[pltpu.SemaphoreType.DMA, pltpu.SemaphoreType.DMA],
    )(block)


PACK_COLS = 1024
SHARDED = (("l0_w_in", 1), ("l0_w_uq", 1), ("l0_w_ukv", 1), ("l0_w_out", 0), ("l0_w_up", 1), ("l0_w_down", 0),
           ("l1_w_in", 0), ("l1_w_glu", 1), ("l1_w_up", 1), ("l1_w_down", 0),
           ("l0_conv_w", 1), ("l0_ffn_conv_w", 1), ("l1_ffn_conv_w", 1))
REPLICATED = ("l0_mix_norm", "l0_conv_b", "l0_conv_ln_g", "l0_conv_ln_b", "l0_q_norm", "l0_kv_norm", "l0_ffn_norm",
              "l0_ffn_conv_b", "l1_mix_norm", "l1_log_dt", "l1_a_re", "l1_a_im", "l1_b_re", "l1_b_im", "l1_c_re",
              "l1_c_im", "l1_d", "l1_b_glu", "l1_ffn_norm", "l1_ffn_conv_b", "final_norm")


def _pack(arrs, dtype, mult):
    flat = jnp.concatenate([a.reshape(-1).astype(dtype) for a in arrs])
    n = flat.shape[0]
    total = -(-n // mult) * mult
    return jnp.pad(flat, (0, total - n))


def _unpack(flat, shapes):
    out, pos = [], 0
    for shp in shapes:
        n = int(np.prod(shp))
        out.append(flat[pos:pos + n].reshape(shp))
        pos += n
    return out


PACK_ROW_ALIGN = 16


def _pack_rows(arrs, dtype, row_mult):
    parts = []
    for a in arrs:
        n = int(np.prod(a.shape))
        rows = -(-n // PACK_COLS)
        if n % PACK_COLS == 0:
            r = a.astype(dtype).reshape(rows, PACK_COLS)
        else:
            r = jnp.pad(a.reshape(-1).astype(dtype), (0, rows * PACK_COLS - n)).reshape(rows, PACK_COLS)
        parts.append(jnp.pad(r, ((0, (-rows) % PACK_ROW_ALIGN), (0, 0))))
    p = jnp.concatenate(parts)
    return jnp.pad(p, ((0, (-p.shape[0]) % row_mult), (0, 0)))


def _unpack_rows(pack, shapes):
    out, r0 = [], 0
    for shp in shapes:
        n = int(np.prod(shp))
        rows = -(-n // PACK_COLS)
        piece = lax.optimization_barrier(pack[r0:r0 + rows])
        out.append(piece.reshape(shp) if n % PACK_COLS == 0 else piece.reshape(-1)[:n].reshape(shp))
        r0 += rows + (-rows) % PACK_ROW_ALIGN
    return out


def _shard(full, axis, j):
    n = full.shape[axis] // N_CHIPS
    return lax.slice_in_dim(full, j * n, (j + 1) * n, axis=axis)


def _block_diag(t):
    q, g, a, b = t.shape
    eye = jnp.eye(g, dtype=t.dtype)
    return jnp.einsum("qgab,gh->qgahb", t, eye).reshape(q, g * a, g * b)


def _block_diag_t(d, a, b):
    q = d.shape[0]
    d5 = d.reshape(q, 8, a, 8, b)
    eye = jnp.eye(8, dtype=d.dtype)
    return jnp.einsum("qgahb,gh->qgab", d5, eye)


def kernel(x, l0_mix_norm, l0_w_in, l0_conv_w, l0_conv_b, l0_conv_ln_g, l0_conv_ln_b, l0_q_norm, l0_kv_norm, l0_w_uq, l0_w_ukv, l0_w_out, l0_ffn_norm, l0_w_up, l0_ffn_conv_w, l0_ffn_conv_b, l0_w_down, l1_mix_norm, l1_w_in, l1_log_dt, l1_a_re, l1_a_im, l1_b_re, l1_b_im, l1_c_re, l1_c_im, l1_d, l1_w_glu, l1_b_glu, l1_ffn_norm, l1_w_up, l1_ffn_conv_w, l1_ffn_conv_b, l1_w_down, final_norm, loss_target, m_l0_mix_norm, m_l0_w_in, m_l0_conv_w, m_l0_conv_b, m_l0_conv_ln_g, m_l0_conv_ln_b, m_l0_q_norm, m_l0_kv_norm, m_l0_w_uq, m_l0_w_ukv, m_l0_w_out, m_l0_ffn_norm, m_l0_w_up, m_l0_ffn_conv_w, m_l0_ffn_conv_b, m_l0_w_down, m_l1_mix_norm, m_l1_w_in, m_l1_log_dt, m_l1_a_re, m_l1_a_im, m_l1_b_re, m_l1_b_im, m_l1_c_re, m_l1_c_im, m_l1_d, m_l1_w_glu, m_l1_b_glu, m_l1_ffn_norm, m_l1_w_up, m_l1_ffn_conv_w, m_l1_ffn_conv_b, m_l1_w_down, m_final_norm, v_l0_mix_norm, v_l0_w_in, v_l0_conv_w, v_l0_conv_b, v_l0_conv_ln_g, v_l0_conv_ln_b, v_l0_q_norm, v_l0_kv_norm, v_l0_w_uq, v_l0_w_ukv, v_l0_w_out, v_l0_ffn_norm, v_l0_w_up, v_l0_ffn_conv_w, v_l0_ffn_conv_b, v_l0_w_down, v_l1_mix_norm, v_l1_w_in, v_l1_log_dt, v_l1_a_re, v_l1_a_im, v_l1_b_re, v_l1_b_im, v_l1_c_re, v_l1_c_im, v_l1_d, v_l1_w_glu, v_l1_b_glu, v_l1_ffn_norm, v_l1_w_up, v_l1_ffn_conv_w, v_l1_ffn_conv_b, v_l1_w_down, v_final_norm):
    a = dict(locals())
    w = {n: a[n] for n in [s for s, _ in SHARDED] + list(REPLICATED)}
    mom = {n: a["m_" + n] for n in w}
    var = {n: a["v_" + n] for n in w}
    return _step(a["x"][0], a["loss_target"][0], w, mom, var)


FIRST_WEIGHTS = ("l0_w_in", "l0_w_uq", "l0_w_ukv", "l0_w_out")
LATER_WEIGHTS = ("l0_w_up", "l0_w_down", "l1_w_in", "l1_w_glu", "l1_w_up", "l1_w_down")


def _assemble(got, names, w):
    got = got.reshape(N_CHIPS, -1, PACK_COLS)
    shapes = [w[n].shape for n in names]
    per_chip = [_unpack_rows(got[j], shapes) for j in range(N_CHIPS)]
    axes = dict(SHARDED)
    return {n: jnp.concatenate([per_chip[j][k] for j in range(N_CHIPS)], axis=axes[n]) for k, n in enumerate(names)}


def _gather_weights(w):
    cc = lax.axis_index("c")
    small = [n for n, _ in SHARDED[10:]]
    full = {}
    for names, dtype, mult in ((FIRST_WEIGHTS, BF16, 2 * 256), (small, F32, 2 * PACK_ROW_ALIGN)):
        pack = _pack_rows([w[n] for n in names], dtype, mult)
        half = lax.dynamic_index_in_dim(pack.reshape(2, -1, PACK_COLS), cc, axis=0, keepdims=False)
        got = all_gather8(half, "gather_" + ("first_matrices" if dtype == BF16 else "conv_weights"))
        full.update(_assemble(got, names, w))
    pack = _pack_rows([w[n] for n in LATER_WEIGHTS], BF16, 2 * 256).reshape(2, -1, PACK_COLS)
    half = lax.dynamic_index_in_dim(pack, cc, axis=0, keepdims=False)
    send_sems, recv_sems, half_thru, land_thru, token = gather_start(half, got, "gather_later_start")
    return full, (send_sems, recv_sems, half_thru, land_thru, pack), token[0, 0]


def _finish_gather(pending, after, w):
    send_sems, recv_sems, half_thru, land_thru, pack = pending
    got = gather_wait(send_sems, recv_sems, half_thru, land_thru, after, "gather_later_wait")
    chip = 2 * lax.axis_index("x") + lax.axis_index("y")
    got = lax.dynamic_update_slice(got, pack, (2 * chip, 0, 0))
    return _assemble(got, LATER_WEIGHTS, w)


def _reduce_begin(grads, names, tag):
    cc = lax.axis_index("c")
    axes = dict(SHARDED)
    packs = [_pack_rows([_shard(grads[n], axes[n], j) for n in names], BF16, 2 * 256) for j in range(N_CHIPS)]
    g = jnp.stack(packs).reshape(N_CHIPS, 2, -1, PACK_COLS)
    keep = lax.dynamic_index_in_dim(g, cc, axis=1, keepdims=False)
    give = lax.dynamic_index_in_dim(g, 1 - cc, axis=1, keepdims=False)
    got = sibling_swap(give, f"grad_swap_halves_{tag}")
    parts = add_to_bf16(keep, got, f"grad_add_sibling_{tag}")
    send_sems, recv_sems, parts_thru, land_thru, token = exchange_start(parts, f"grad_exchange_start_{tag}")
    shapes = [_shard(grads[n], axes[n], 0).shape for n in names]
    return (send_sems, recv_sems, parts_thru, land_thru, list(names), shapes), token[0, 0]


def _reduce_end(state, after, tag):
    send_sems, recv_sems, parts_thru, land_thru, names, shapes = state
    cc = lax.axis_index("c")
    chip = 2 * lax.axis_index("x") + lax.axis_index("y")
    parts, landed = exchange_wait(send_sems, recv_sems, parts_thru, land_thru, after, f"grad_exchange_wait_{tag}")
    own = lax.dynamic_index_in_dim(parts, chip, axis=0, keepdims=True)
    landed = lax.dynamic_update_slice(landed, own, (chip, 0, 0))
    mine = sum_leading(landed, f"grad_sum_chips_{tag}")
    theirs = sibling_swap(mine, f"grad_swap_sums_{tag}")
    lo = jnp.where(cc == 0, mine, theirs)
    hi = jnp.where(cc == 0, theirs, mine)
    return dict(zip(names, _unpack_rows(jnp.concatenate([lo, hi]), shapes)))


def _reduce_replicated(grads, loss_row):
    names = list(REPLICATED)
    flat = _pack([grads[n] for n in names] + [loss_row], F32, 256 * LANES).reshape(-1, LANES)
    got = all_gather8(flat, "gather_small_grads")
    tot = sum_leading(got, "sum_small_grads").reshape(-1)
    parts = _unpack(tot, [grads[n].shape for n in names] + [loss_row.shape])
    return dict(zip(names, parts[:-1])), parts[-1][0]


def _row(v):
    return v.reshape(1, -1).astype(F32)


def _pad_rows(wt, rows):
    return jnp.pad(wt.astype(F32), ((0, rows - wt.shape[0]), (0, 0)))


def _ffn_fwd(xin, g, wa, wb, cw, cb, wd, tag):
    cwa, cwb = _pad_rows(cw[:, :D_FF], 8), _pad_rows(cw[:, D_FF:], 8)
    xout, xn, hpa, hpb, act = ffn_fwd(xin, _row(g), wa, wb, cwa, cwb, _row(cb[:D_FF]), _row(cb[D_FF:]), wd, tag)
    return xout, (xin, xn, hpa, hpb, act)


def _ffn_bwd(dxout, saved, g, wa, wb, cw, cb, wd, tag, zero=0.0):
    xin, xn, hpa, hpb, act = saved
    d_wd = matmul(act, dxout, ta=True, name=f"{tag}_d_wdown")
    cwa, cwb = _pad_rows(cw[:, :D_FF], 8), _pad_rows(cw[:, D_FF:], 8)
    dxin, dpa, dpb, dwa, dwb, dba, dbb, dg = ffn_bwd(dxout, xin, _row(g), hpa, hpb, wa, wb, cwa, cwb,
                                                     _row(cb[:D_FF]) + zero, _row(cb[D_FF:]), wd, tag + "_bwd")
    d_wu = jnp.concatenate([matmul(xn, dpa, ta=True, name=f"{tag}_d_wup_a"),
                            matmul(xn, dpb, ta=True, name=f"{tag}_d_wup_b")], axis=1)
    taps = lambda t: t.transpose(1, 0, 2).reshape(8, -1)
    d_cw = jnp.concatenate([taps(dwa)[:FFN_K], taps(dwb)[:FFN_K]], axis=1)
    d_cb = jnp.concatenate([taps(dba)[0], taps(dbb)[0]])
    return dxin, dg[0], d_wu, d_cw, d_cb, d_wd


def _step(x, target, w, mom, var):
    s = x.shape[0]
    full, pending, zero = _gather_weights(w)
    cos, sin = rope_tables(s)

    w_in0 = full["l0_w_in"]
    w_in0p = jnp.concatenate([w_in0, jnp.zeros((D_MODEL, H0_W - w_in0.shape[1]), BF16)], axis=1)
    wq = full["l0_w_uq"].reshape(Q_LORA, N_HEADS, QK_NOPE + QK_ROPE)
    zq = lambda n: jnp.zeros((Q_LORA, N_HEADS, n), BF16)
    w_uqp = jnp.concatenate([wq[..., :QK_NOPE], zq(LANES - QK_NOPE), wq[..., QK_NOPE:], zq(LANES - QK_ROPE)],
                            axis=-1).reshape(Q_LORA, N_HEADS * HEAD_PAD)
    w_ukv = full["l0_w_ukv"]
    w_out = full["l0_w_out"]
    w_out_u = w_out[:CONV_WIDTH]
    wo = w_out[CONV_WIDTH:].reshape(N_HEADS, V_DIM, D_MODEL)
    w_out_a = jnp.concatenate([jnp.zeros_like(wo), wo], axis=1).reshape(N_HEADS * LANES, D_MODEL)
    conv_w = _pad_rows(full["l0_conv_w"], CONV_HALO)

    xn0 = rms_fwd(x, _row(w["l0_mix_norm"]) + zero, "l0_mix_rms")
    h0 = matmul(xn0, w_in0p, name="l0_in_proj")
    qn_g, kvn_g = _row(w["l0_q_norm"]), _row(w["l0_kv_norm"])
    u0, cq, ckv, kr = mixpre_fwd(h0, qn_g, kvn_g, cos, sin)
    cb, lg, lb = _row(w["l0_conv_b"]), _row(w["l0_conv_ln_g"]), _row(w["l0_conv_ln_b"])
    u = convln_fwd(u0, conv_w, cb, lg, lb)
    q = q_up_rope(cq, w_uqp, cos, sin)
    kv = matmul(ckv, w_ukv, out_dtype=BF16, name="l0_kv_up")
    o, lse = attn_fwd(q, kv, kr)
    x1 = matmul(u, w_out_u, res=x, name="l0_out_conv")
    x1 = matmul(o, w_out_a, res=x1, name="l0_out_attn")
    full.update(_finish_gather(pending, x1, w))
    w_up0a, w_up0b = full["l0_w_up"][:, :D_FF], full["l0_w_up"][:, D_FF:]
    w_up1a, w_up1b = full["l1_w_up"][:, :D_FF], full["l1_w_up"][:, D_FF:]

    x2, ffn0 = _ffn_fwd(x1, w["l0_ffn_norm"], w_up0a, w_up0b, full["l0_ffn_conv_w"], w["l0_ffn_conv_b"],
                        full["l0_w_down"], "l0_ffn")

    g_, p_, c_ = SSM_GROUPS, SSM_STATE, SSM_GROUP
    s5_in = (w["l1_log_dt"].reshape(g_, 1), w["l1_a_re"], w["l1_a_im"],
             w["l1_b_re"].reshape(g_, p_ * c_), w["l1_b_im"].reshape(g_, p_ * c_))
    lam_r, lam_i, bb_r, bb_i = s5_params_fwd(*s5_in)
    lam_rf, lam_if = lam_r.reshape(1, NS), lam_i.reshape(1, NS)

    def b_blocks(bb):
        t = bb.reshape(NQ, 8, p_, c_).transpose(0, 1, 3, 2)
        return _block_diag(t).astype(BF16)

    def c_blocks(cm):
        t = cm.reshape(NQ, 8, c_, p_).transpose(0, 1, 3, 2)
        return _block_diag(t).astype(BF16)

    bre, bim = b_blocks(bb_r), b_blocks(bb_i)
    cre, cim = c_blocks(w["l1_c_re"]), c_blocks(w["l1_c_im"])
    dskip = _row(w["l1_d"])
    xn2 = rms_fwd(x2, _row(w["l1_mix_norm"]), "l1_mix_rms")
    u1 = matmul(xn2, full["l1_w_in"], name="l1_in_proj")
    xs_r, xs_i, y1, yg = s5_scan_fwd(u1, lam_rf, lam_if, bre, bim, cre, cim, dskip)
    x3, z = glu_proj_res(yg, full["l1_w_glu"], _row(w["l1_b_glu"]), x2)

    x4, ffn1 = _ffn_fwd(x3, w["l1_ffn_norm"], w_up1a, w_up1b, full["l1_ffn_conv_w"], w["l1_ffn_conv_b"],
                        full["l1_w_down"], "l1_ffn")
    loss_part, dx4, dgf = loss_head(x4, _row(w["final_norm"]), target)

    gr = {"final_norm": dgf[0]}

    dx3, gr["l1_ffn_norm"], gr["l1_w_up"], gr["l1_ffn_conv_w"], gr["l1_ffn_conv_b"], gr["l1_w_down"] = _ffn_bwd(
        dx4, ffn1, w["l1_ffn_norm"], w_up1a, w_up1b, full["l1_ffn_conv_w"], w["l1_ffn_conv_b"], full["l1_w_down"],
        "l1_ffn")

    dz, dbga, dbgb = glu_bwd(z, dx3)
    gr["l1_b_glu"] = jnp.concatenate([dbga[0], dbgb[0]])
    dyg = matmul(dz, full["l1_w_glu"], tb=True, name="l1_d_yg")
    gr["l1_w_glu"] = matmul(yg, dz, ta=True, name="l1_d_wglu")
    a_r, a_i, dy1 = s5_scan_bwd(dyg, y1, lam_rf, lam_if, cre, cim)
    du1, dlr, dli, dbr, dbi, dcr, dci, dd = s5_grads(u1, dy1, xs_r, xs_i, a_r, a_i, bre, bim, dskip)
    gr["l1_d"] = dd[0]

    def b_unblock(d):
        return _block_diag_t(d, c_, p_).transpose(0, 1, 3, 2).reshape(g_, p_ * c_)

    def c_unblock(d):
        return _block_diag_t(d, p_, c_).transpose(0, 1, 3, 2).reshape(g_, c_, p_)

    gr["l1_c_re"], gr["l1_c_im"] = c_unblock(dcr), c_unblock(dci)
    dld, dar, dai, dbre, dbim = s5_params_bwd(*s5_in, dlr[0].reshape(g_, p_), dli[0].reshape(g_, p_),
                                              b_unblock(dbr), b_unblock(dbi))
    gr["l1_log_dt"], gr["l1_a_re"], gr["l1_a_im"] = dld.reshape(g_), dar, dai
    gr["l1_b_re"], gr["l1_b_im"] = dbre.reshape(g_, p_, c_), dbim.reshape(g_, p_, c_)
    dxn2 = matmul(du1, full["l1_w_in"], tb=True, name="l1_d_xn")
    gr["l1_w_in"] = matmul(xn2, du1, ta=True, name="l1_d_win")
    dx2, dg = rms_bwd(x2, _row(w["l1_mix_norm"]), dxn2, dx3, "l1_mix_rms_bwd")
    gr["l1_mix_norm"] = dg[0]
    red_a, zero_a = _reduce_begin(gr, ("l1_w_up", "l1_w_down", "l1_w_glu", "l1_w_in"), "a")

    dx1, gr["l0_ffn_norm"], gr["l0_w_up"], gr["l0_ffn_conv_w"], gr["l0_ffn_conv_b"], gr["l0_w_down"] = _ffn_bwd(
        dx2, ffn0, w["l0_ffn_norm"], w_up0a, w_up0b, full["l0_ffn_conv_w"], w["l0_ffn_conv_b"], full["l0_w_down"],
        "l0_ffn", zero_a)
    red_b, zero_b = _reduce_begin(gr, ("l0_w_up", "l0_w_down"), "b")

    du = matmul(dx1, w_out_u + zero_b.astype(BF16), tb=True, out_dtype=BF16, name="l0_d_u")
    do = matmul(dx1, w_out_a, tb=True, out_dtype=BF16, name="l0_d_o")
    d_wout_u = matmul(u, dx1, ta=True, name="l0_d_wout_u")
    d_wout_a = matmul(o, dx1, ta=True, name="l0_d_wout_a")
    gr["l0_w_out"] = jnp.concatenate(
        [d_wout_u, d_wout_a.reshape(N_HEADS, LANES, D_MODEL)[:, LANES - V_DIM:].reshape(N_HEADS * V_DIM, D_MODEL)])
    dq, dkv, dkr = attn_bwd(q, kv, kr, o, do, lse)
    dqraw = qrope_bwd(dq, cos, sin)
    dcq = matmul(dqraw, w_uqp, tb=True, name="l0_d_cq")
    d_wuqp = matmul(cq, dqraw, ta=True, name="l0_d_wuq").reshape(Q_LORA, N_HEADS, HEAD_PAD)
    gr["l0_w_uq"] = jnp.concatenate([d_wuqp[..., :QK_NOPE], d_wuqp[..., LANES:LANES + QK_ROPE]],
                                    axis=-1).reshape(Q_LORA, -1)
    dckv = matmul(dkv, w_ukv, tb=True, name="l0_d_ckv")
    gr["l0_w_ukv"] = matmul(ckv, dkv, ta=True, name="l0_d_wukv")
    du1c, dlg, dlb, dcb = convln_bwd1(u0, conv_w, cb, lg, lb, du)
    gr["l0_conv_ln_g"], gr["l0_conv_ln_b"], gr["l0_conv_b"] = dlg[0], dlb[0], dcb[0]
    du0, dcw = convln_bwd2(u0, conv_w, du1c)
    gr["l0_conv_w"] = dcw[:CONV_K]
    dh0, dqn, dkvn = mixpre_bwd(h0, qn_g, kvn_g, cos, sin, du0, dcq, dckv, dkr)
    gr["l0_q_norm"], gr["l0_kv_norm"] = dqn[0], dkvn[0]
    dxn0 = matmul(dh0, w_in0p, tb=True, name="l0_d_xn")
    gr["l0_w_in"] = matmul(xn0, dh0, ta=True, name="l0_d_win")[:, :w_in0.shape[1]]
    grad_x, dg = rms_bwd(x, _row(w["l0_mix_norm"]), dxn0, dx1, "l0_mix_rms_bwd")
    gr["l0_mix_norm"] = dg[0]

    rest = [n for n, _ in SHARDED if n not in red_a[-2] + red_b[-2]]
    red_c, _ = _reduce_begin(gr, rest, "c")
    g_sh = {**_reduce_end(red_a, grad_x, "a"), **_reduce_end(red_b, grad_x, "b"), **_reduce_end(red_c, grad_x, "c")}
    g_rep, loss = _reduce_replicated(gr, loss_part[0])
    grad, delta, new_m, new_v = {}, {}, {}, {}
    for n, _ in SHARDED:
        shp = w[n].shape
        two_d = (lambda t: t.reshape(shp[0], -1))
        grad[n] = g_sh[n]
        delta[n], new_m[n], new_v[n] = adamw(two_d(w[n]), two_d(g_sh[n]), two_d(mom[n]), two_d(var[n]), f"adamw_{n}")
    names = list(REPLICATED)
    pk = lambda d: _pack([d[n] for n in names], F32, 256 * LANES).reshape(-1, LANES)
    dl, nm, nv = adamw(pk(w), pk(g_rep), pk(mom), pk(var), "adamw_small")
    shapes = [w[n].shape for n in names]
    for n, d_, m_, v_ in zip(names, _unpack(dl.reshape(-1), shapes), _unpack(nm.reshape(-1), shapes),
                             _unpack(nv.reshape(-1), shapes)):
        grad[n], delta[n], new_m[n], new_v[n] = g_rep[n], d_, m_, v_

    order = ["l0_mix_norm", "l0_w_in", "l0_conv_w", "l0_conv_b", "l0_conv_ln_g", "l0_conv_ln_b", "l0_q_norm",
             "l0_kv_norm", "l0_w_uq", "l0_w_ukv", "l0_w_out", "l0_ffn_norm", "l0_w_up", "l0_ffn_conv_w",
             "l0_ffn_conv_b", "l0_w_down", "l1_mix_norm", "l1_w_in", "l1_log_dt", "l1_a_re", "l1_a_im", "l1_b_re",
             "l1_b_im", "l1_c_re", "l1_c_im", "l1_d", "l1_w_glu", "l1_b_glu", "l1_ffn_norm", "l1_w_up",
             "l1_ffn_conv_w", "l1_ffn_conv_b", "l1_w_down", "final_norm"]
    return (loss, grad_x[None], *[grad[n] for n in order], *[delta[n] for n in order],
            *[new_m[n] for n in order], *[new_v[n] for n in order])
```

```python
import functools
import math

import jax
import jax.numpy as jnp
import numpy as np
from jax import lax
from jax.experimental import pallas as pl
from jax.experimental.pallas import tpu as pltpu

F32 = jnp.float32
BF16 = jnp.bfloat16
MESH = pl.DeviceIdType.MESH

D_MODEL = 1024
EPS = 1e-6
LN_EPS = 1e-5
CONV_WIDTH = 512
CONV_K = 31
N_HEADS = 8
QK_NOPE = 64
QK_ROPE = 32
V_DIM = 64
Q_LORA = 256
KV_LORA = 128
ROPE_BASE = 10000.0
ATT_SCALE = (QK_NOPE + QK_ROPE) ** -0.5
SSM_WIDTH = 512
SSM_GROUP = 16
SSM_GROUPS = 32
SSM_STATE = 64
D_FF = 2816
FFN_K = 3
ADAM_LR = 0.001
ADAM_B1 = 0.9
ADAM_B2 = 0.999
ADAM_EPS = 1e-08
ADAM_WD = 0.01
ADAM_STEP = 10

N_CHIPS = 4
LANES = 128
HEAD_PAD = 256
CONV_HALO = 32
FFN_HALO = 16
VMEM_LIMIT = 56 * 1024 * 1024

ROW_TILE = 512
FFN_ROW_TILE = 1024
FFN_COL_TILE = 256
FFN_ROW_CHUNK = 64
CONV_ROW_CHUNK = 32
FFN_BWD_PARTS = 4
ATT_TILE = 1024
SCAN_TILE = 256
SCAN_UNROLL = 4


def _cparams(*sem):
    return pltpu.CompilerParams(dimension_semantics=tuple(sem), vmem_limit_bytes=VMEM_LIMIT)


def _pick(n, cands):
    for c in cands:
        if n % c == 0:
            return c
    return n


def matmul(a, b, *, ta=False, tb=False, res=None, bias=None, out_dtype=None, name):
    if out_dtype is None:
        out_dtype = BF16 if ta else F32
    if ta:
        kdim, m = a.shape
    else:
        m, kdim = a.shape
    if tb:
        n, k2 = b.shape
    else:
        k2, n = b.shape
    assert kdim == k2, (a.shape, b.shape, ta, tb)
    tn = _pick(n, (1408, 1024, 768, 512, 384, 256, 128))
    if ta:
        tm = _pick(m, (1408, 1024, 512, 256, 128))
        tk = _pick(kdim, (2048, 1024, 512, 256, 128))
    else:
        tm = _pick(m, (1024, 512, 256, 128))
        tk = kdim
        if kdim > 1024:
            tn = _pick(n, (512, 256, 128))
        if tm * tn > 1024 * 1024 and out_dtype == F32:
            tm = _pick(m, (512, 256, 128))
    nk = kdim // tk
    has_res, has_bias = res is not None, bias is not None
    dims = (((0,) if ta else (1,), (1,) if tb else (0,)), ((), ()))

    def body(*refs):
        a_ref, b_ref = refs[0], refs[1]
        pos = 2
        res_ref = bias_ref = None
        if has_res:
            res_ref = refs[pos]
            pos += 1
        if has_bias:
            bias_ref = refs[pos]
            pos += 1
        o_ref = refs[pos]

        def finish(r):
            if has_bias:
                r = r + bias_ref[...]
            if has_res:
                r = r + res_ref[...].astype(F32)
            o_ref[...] = r.astype(o_ref.dtype)

        prod = lax.dot_general(a_ref[...].astype(BF16), b_ref[...].astype(BF16), dims, preferred_element_type=F32)
        if nk == 1:
            finish(prod)
            return
        acc_ref = refs[pos + 1]
        k = pl.program_id(2)

        @pl.when(k == 0)
        def _():
            acc_ref[...] = prod

        @pl.when(k > 0)
        def _():
            acc_ref[...] += prod

        @pl.when(k == nk - 1)
        def _():
            finish(acc_ref[...])

    a_spec = pl.BlockSpec((tk, tm), lambda i, j, k: (k, i)) if ta else pl.BlockSpec((tm, tk), lambda i, j, k: (i, k))
    b_spec = pl.BlockSpec((tn, tk), lambda i, j, k: (j, k)) if tb else pl.BlockSpec((tk, tn), lambda i, j, k: (k, j))
    in_specs = [a_spec, b_spec]
    args = [a, b]
    if has_res:
        in_specs.append(pl.BlockSpec((tm, tn), lambda i, j, k: (i, j)))
        args.append(res)
    if has_bias:
        in_specs.append(pl.BlockSpec((1, tn), lambda i, j, k: (0, j)))
        args.append(bias)
    return pl.pallas_call(
        body, name=name, grid=(m // tm, n // tn, nk),
        in_specs=in_specs, out_specs=pl.BlockSpec((tm, tn), lambda i, j, k: (i, j)),
        out_shape=jax.ShapeDtypeStruct((m, n), out_dtype),
        scratch_shapes=[pltpu.VMEM((tm, tn), F32)] if nk > 1 else [],
        compiler_params=_cparams("parallel", "parallel", "arbitrary"),
    )(*args)


def rowcall(body, *, rows, ts, ins, outs, name, scratch=()):
    nt = rows // ts
    in_specs, args = [], []
    for arr, kind in ins:
        if kind == "row":
            in_specs.append(pl.BlockSpec((ts, arr.shape[1]), lambda i: (i, 0)))
        elif kind == "rev":
            in_specs.append(pl.BlockSpec((ts, arr.shape[1]), lambda i: (nt - 1 - i, 0)))
        elif kind == "full":
            nd = arr.ndim
            in_specs.append(pl.BlockSpec(arr.shape, lambda i, nd=nd: (0,) * nd))
        elif kind.startswith("prev:"):
            h = int(kind[5:])
            r = ts // h
            in_specs.append(pl.BlockSpec((h, arr.shape[1]), lambda i, r=r: (jnp.maximum(i * r - 1, 0), 0)))
        elif kind.startswith("next:"):
            h = int(kind[5:])
            r = ts // h
            last = rows // h - 1
            in_specs.append(pl.BlockSpec((h, arr.shape[1]), lambda i, r=r, last=last: (jnp.minimum((i + 1) * r, last), 0)))
        elif kind.startswith("revprev:"):
            h = int(kind[8:])
            r = ts // h
            in_specs.append(pl.BlockSpec((h, arr.shape[1]), lambda i, r=r: (jnp.maximum((nt - 1 - i) * r - 1, 0), 0)))
        else:
            raise ValueError(kind)
        args.append(arr)
    out_specs, out_shapes = [], []
    for shape, dtype, kind in outs:
        if kind == "row":
            out_specs.append(pl.BlockSpec((ts, shape[1]), lambda i: (i, 0)))
        elif kind == "rev":
            out_specs.append(pl.BlockSpec((ts, shape[1]), lambda i: (nt - 1 - i, 0)))
        else:
            nd = len(shape)
            out_specs.append(pl.BlockSpec(tuple(shape), lambda i, nd=nd: (0,) * nd))
        out_shapes.append(jax.ShapeDtypeStruct(tuple(shape), dtype))
    return pl.pallas_call(
        functools.partial(body, nt), name=name, grid=(nt,),
        in_specs=in_specs, out_specs=tuple(out_specs), out_shape=tuple(out_shapes),
        scratch_shapes=list(scratch),
        compiler_params=_cparams("arbitrary"),
    )(*args)


def _rms(x, g):
    return x * lax.rsqrt(jnp.mean(x * x, axis=-1, keepdims=True) + EPS) * g


def _layer_norm(x, g, b):
    mu = jnp.mean(x, axis=-1, keepdims=True)
    xc = x - mu
    var = jnp.mean(xc * xc, axis=-1, keepdims=True)
    return xc * lax.rsqrt(var + LN_EPS) * g + b


def _sigmoid(x):
    return 1.0 / (1.0 + jnp.exp(-x))


def _silu(x):
    return x * _sigmoid(x)


def _gelu(x):
    return 0.5 * x * (1.0 + jnp.tanh(math.sqrt(2.0 / math.pi) * (x + 0.044715 * (x * x * x))))


def _acc(ref, i, val):
    s = jnp.sum(val, axis=0, keepdims=True)

    @pl.when(i == 0)
    def _():
        ref[...] = jnp.zeros_like(ref)

    ref[...] += jnp.broadcast_to(s, ref.shape)


def rms_fwd(x, g, name):
    s, c = x.shape

    def body(nt, x_ref, g_ref, o_ref):
        o_ref[...] = _rms(x_ref[...], g_ref[...]).astype(BF16)

    return rowcall(body, rows=s, ts=min(ROW_TILE, s), ins=[(x, "row"), (g, "full")],
                   outs=[((s, c), BF16, "row")], name=name)[0]


def rms_bwd(x, g, dxn, dres, name):
    s, c = x.shape

    def body(nt, x_ref, g_ref, d_ref, r_ref, dx_ref, dg_ref):
        i = pl.program_id(0)
        _, vjp = jax.vjp(_rms, x_ref[...], g_ref[...])
        dx, dg = vjp(d_ref[...].astype(F32))
        dx_ref[...] = dx + r_ref[...]
        _acc(dg_ref, i, dg)

    return rowcall(body, rows=s, ts=min(ROW_TILE, s),
                   ins=[(x, "row"), (g, "full"), (dxn, "row"), (dres, "row")],
                   outs=[((s, c), F32, "row"), ((8, c), F32, "acc")], name=name)


def _partner(t):
    lane = lax.broadcasted_iota(jnp.int32, t.shape, 1)
    half = QK_ROPE // 2
    return jnp.where(lane % QK_ROPE < half, pltpu.roll(t, LANES - half, 1), pltpu.roll(t, half, 1))


def _rope(t, cos, sin):
    return t * cos + _partner(t) * sin


def _rope_t(d, cos, sin):
    return d * cos + _partner(d * sin)


def rope_tables(s):
    half = QK_ROPE // 2
    inv = ROPE_BASE ** (-jnp.arange(half, dtype=F32) / half)
    ang = jnp.arange(s).astype(F32)[:, None] * inv[None, :]
    cos, sin = jnp.cos(ang), jnp.sin(ang)
    z = jnp.zeros((s, LANES - QK_ROPE), F32)
    return jnp.concatenate([cos, cos, z], axis=1), jnp.concatenate([-sin, sin, z], axis=1)


H0_A, H0_G, H0_Q, H0_KV, H0_KR, H0_W = 0, 512, 1024, 1280, 1408, 1536


def _mixpre_fn(a, g, q, kv, qn, kvn):
    return a * _sigmoid(g), _rms(q, qn), _rms(kv, kvn)


def _h0_parts(h_ref):
    return (h_ref[:, H0_A:H0_G], h_ref[:, H0_G:H0_Q], h_ref[:, H0_Q:H0_KV], h_ref[:, H0_KV:H0_KR])


def mixpre_fwd(h0, qn, kvn, cos, sin):
    s = h0.shape[0]

    def body(nt, h_ref, qn_ref, kvn_ref, cos_ref, sin_ref, u0_ref, cq_ref, ckv_ref, kr_ref):
        u0, cq, ckv = _mixpre_fn(*_h0_parts(h_ref), qn_ref[...], kvn_ref[...])
        u0_ref[...] = u0
        cq_ref[...] = cq.astype(BF16)
        ckv_ref[...] = ckv.astype(BF16)
        kr_ref[...] = _rope(h_ref[:, H0_KR:H0_W], cos_ref[...], sin_ref[...]).astype(BF16)

    return rowcall(body, rows=s, ts=min(ROW_TILE, s),
                   ins=[(h0, "row"), (qn, "full"), (kvn, "full"), (cos, "row"), (sin, "row")],
                   outs=[((s, CONV_WIDTH), F32, "row"), ((s, Q_LORA), BF16, "row"),
                         ((s, KV_LORA), BF16, "row"), ((s, LANES), BF16, "row")], name="mixpre_fwd")


def mixpre_bwd(h0, qn, kvn, cos, sin, du0, dcq, dckv, dkr):
    s = h0.shape[0]

    def body(nt, h_ref, qn_ref, kvn_ref, cos_ref, sin_ref, du0_ref, dcq_ref, dckv_ref, dkr_ref,
             dh_ref, dqn_ref, dkvn_ref):
        i = pl.program_id(0)
        _, vjp = jax.vjp(_mixpre_fn, *_h0_parts(h_ref), qn_ref[...], kvn_ref[...])
        da, dg, dq, dkv, dqn, dkvn = vjp((du0_ref[...], dcq_ref[...], dckv_ref[...]))
        dh_ref[:, H0_A:H0_G] = da.astype(BF16)
        dh_ref[:, H0_G:H0_Q] = dg.astype(BF16)
        dh_ref[:, H0_Q:H0_KV] = dq.astype(BF16)
        dh_ref[:, H0_KV:H0_KR] = dkv.astype(BF16)
        dkr = dkr_ref[:, :LANES]
        for h in range(1, N_HEADS):
            dkr = dkr + dkr_ref[:, h * LANES:(h + 1) * LANES]
        dh_ref[:, H0_KR:H0_W] = _rope_t(dkr, cos_ref[...], sin_ref[...]).astype(BF16)
        _acc(dqn_ref, i, dqn)
        _acc(dkvn_ref, i, dkvn)

    return rowcall(body, rows=s, ts=min(ROW_TILE, s),
                   ins=[(h0, "row"), (qn, "full"), (kvn, "full"), (cos, "row"), (sin, "row"),
                        (du0, "row"), (dcq, "row"), (dckv, "row"), (dkr, "row")],
                   outs=[((s, H0_W), BF16, "row"), ((8, Q_LORA), F32, "acc"), ((8, KV_LORA), F32, "acc")],
                   name="mixpre_bwd")


def _conv_taps(ext_ref, w_ref, ts, first, ntaps, flip=False):
    acc = None
    for k in range(ntaps):
        term = w_ref[pl.ds(ntaps - 1 - k if flip else k, 1), :] * ext_ref[pl.ds(first + k, ts), :]
        acc = term if acc is None else acc + term
    return acc


def _ln_silu(u1, g, b):
    return _silu(_layer_norm(u1, g, b))


SUBLANES = 8


def _fill_shifted(sh_ref, parts, rows):
    pos = 0
    for p in parts:
        sh_ref[0, pl.ds(pos, p.shape[0]), :] = p
        pos += p.shape[0]
    sh_ref[0, pl.ds(rows, SUBLANES), :] = jnp.zeros((SUBLANES, sh_ref.shape[2]), F32)
    for r in range(1, SUBLANES):
        sh_ref[r, pl.ds(0, rows), :] = sh_ref[0, pl.ds(r, rows), :]


def _window(sh_ref, off, n):
    r = off % SUBLANES
    return sh_ref[r, pl.ds(off - r, n), :]


def _taps_aligned(sh_ref, w_ref, n, first, ntaps, flip=False):
    acc = None
    for k in range(ntaps):
        term = w_ref[pl.ds(ntaps - 1 - k if flip else k, 1), :] * _window(sh_ref, first + k, n)
        acc = term if acc is None else acc + term
    return acc


def _conv_scratch(ts, c):
    return pltpu.VMEM((SUBLANES, ts + CONV_HALO + SUBLANES, c), F32)


def convln_fwd(u0, w, b, lg, lb):
    s, c = u0.shape
    ts = min(ROW_TILE, s)
    rc = min(CONV_ROW_CHUNK, ts)
    first = CONV_HALO - (CONV_K - 1)

    def body(nt, cur_ref, prev_ref, w_ref, b_ref, lg_ref, lb_ref, o_ref, sh_ref):
        i = pl.program_id(0)
        _fill_shifted(sh_ref, [jnp.where(i > 0, prev_ref[...], 0.0), cur_ref[...]], ts + CONV_HALO)
        for r0 in range(0, ts, rc):
            u1 = _taps_aligned(sh_ref, w_ref, rc, first + r0, CONV_K) + b_ref[...]
            o_ref[pl.ds(r0, rc), :] = _ln_silu(u1, lg_ref[...], lb_ref[...]).astype(BF16)

    return rowcall(body, rows=s, ts=ts,
                   ins=[(u0, "row"), (u0, f"prev:{CONV_HALO}"), (w, "full"), (b, "full"), (lg, "full"), (lb, "full")],
                   outs=[((s, c), BF16, "row")], name="convln_fwd", scratch=[_conv_scratch(ts, c)])[0]


def convln_bwd1(u0, w, b, lg, lb, du):
    s, c = u0.shape
    ts = min(ROW_TILE, s)
    rc = min(CONV_ROW_CHUNK, ts)
    first = CONV_HALO - (CONV_K - 1)

    def body(nt, cur_ref, prev_ref, w_ref, b_ref, lg_ref, lb_ref, du_ref, du1_ref, dlg_ref, dlb_ref, dcb_ref, sh_ref):
        i = pl.program_id(0)
        _fill_shifted(sh_ref, [jnp.where(i > 0, prev_ref[...], 0.0), cur_ref[...]], ts + CONV_HALO)
        sums = [jnp.zeros((1, c), F32)] * 3
        for r0 in range(0, ts, rc):
            u1 = _taps_aligned(sh_ref, w_ref, rc, first + r0, CONV_K) + b_ref[...]
            _, vjp = jax.vjp(_ln_silu, u1, lg_ref[...], lb_ref[...])
            du1, dlg, dlb = vjp(du_ref[pl.ds(r0, rc), :].astype(F32))
            du1_ref[pl.ds(r0, rc), :] = du1
            parts = (dlg, dlb, jnp.sum(du1, axis=0, keepdims=True))
            sums = [a + jnp.sum(p, axis=0, keepdims=True) for a, p in zip(sums, parts)]
        _acc(dlg_ref, i, sums[0])
        _acc(dlb_ref, i, sums[1])
        _acc(dcb_ref, i, sums[2])

    return rowcall(body, rows=s, ts=ts,
                   ins=[(u0, "row"), (u0, f"prev:{CONV_HALO}"), (w, "full"), (b, "full"), (lg, "full"), (lb, "full"),
                        (du, "row")],
                   outs=[((s, c), F32, "row"), ((8, c), F32, "acc"), ((8, c), F32, "acc"), ((8, c), F32, "acc")],
                   name="convln_bwd1", scratch=[_conv_scratch(ts, c)])


def convln_bwd2(u0, w, du1):
    s, c = u0.shape
    ts = min(ROW_TILE, s)
    rc = min(CONV_ROW_CHUNK, ts)
    first = CONV_HALO - (CONV_K - 1)

    def body(nt, cur_ref, prev_ref, d_ref, dnext_ref, w_ref, du0_ref, dw_ref, sh_ref, dsh_ref):
        i = pl.program_id(0)
        _fill_shifted(sh_ref, [jnp.where(i > 0, prev_ref[...], 0.0), cur_ref[...]], ts + CONV_HALO)
        _fill_shifted(dsh_ref, [d_ref[...], jnp.where(i < nt - 1, dnext_ref[...], 0.0)], ts + CONV_HALO)
        for r0 in range(0, ts, rc):
            du0_ref[pl.ds(r0, rc), :] = _taps_aligned(dsh_ref, w_ref, rc, r0, CONV_K, flip=True)

        @pl.when(i == 0)
        def _():
            dw_ref[...] = jnp.zeros_like(dw_ref)

        for k in range(CONV_K):
            part = jnp.zeros((SUBLANES, c), F32)
            for r0 in range(0, ts, rc):
                prod = d_ref[pl.ds(r0, rc), :] * _window(sh_ref, first + k + r0, rc)
                for a in range(0, rc, SUBLANES):
                    part = part + prod[a:a + SUBLANES]
            dw_ref[pl.ds(k, 1), :] += jnp.sum(part, axis=0, keepdims=True)

    return rowcall(body, rows=s, ts=ts,
                   ins=[(u0, "row"), (u0, f"prev:{CONV_HALO}"), (du1, "row"), (du1, f"next:{CONV_HALO}"), (w, "full")],
                   outs=[((s, c), F32, "row"), ((CONV_HALO, c), F32, "acc")], name="convln_bwd2",
                   scratch=[_conv_scratch(ts, c), _conv_scratch(ts, c)])


def q_up_rope(cq, w_uqp, cos, sin):
    s, kdim = cq.shape
    n = w_uqp.shape[1]
    ts = min(ROW_TILE, s)

    def body(nt, cq_ref, w_ref, cos_ref, sin_ref, o_ref):
        cos_v, sin_v = cos_ref[...] * ATT_SCALE, sin_ref[...] * ATT_SCALE
        for h in range(N_HEADS):
            cols = slice(h * HEAD_PAD, (h + 1) * HEAD_PAD)
            qh = jnp.dot(cq_ref[...], w_ref[:, cols], preferred_element_type=F32)
            o_ref[:, h * HEAD_PAD:h * HEAD_PAD + LANES] = (qh[:, :LANES] * ATT_SCALE).astype(BF16)
            o_ref[:, h * HEAD_PAD + LANES:(h + 1) * HEAD_PAD] = _rope(qh[:, LANES:], cos_v, sin_v).astype(BF16)

    return rowcall(body, rows=s, ts=ts, ins=[(cq, "row"), (w_uqp, "full"), (cos, "row"), (sin, "row")],
                   outs=[((s, n), BF16, "row")], name="l0_q_up_rope")[0]


def qrope_bwd(dq, cos, sin):
    s = dq.shape[0]

    def body(nt, d_ref, cos_ref, sin_ref, o_ref):
        cos_v, sin_v = cos_ref[...] * ATT_SCALE, sin_ref[...] * ATT_SCALE
        for h in range(N_HEADS):
            nope = d_ref[:, h * HEAD_PAD:h * HEAD_PAD + LANES] * ATT_SCALE
            o_ref[:, h * HEAD_PAD:h * HEAD_PAD + LANES] = nope.astype(BF16)
            r = d_ref[:, h * HEAD_PAD + LANES:(h + 1) * HEAD_PAD].astype(F32)
            o_ref[:, h * HEAD_PAD + LANES:(h + 1) * HEAD_PAD] = _rope_t(r, cos_v, sin_v).astype(BF16)

    return rowcall(body, rows=s, ts=min(ROW_TILE, s), ins=[(dq, "row"), (cos, "row"), (sin, "row")],
                   outs=[((s, N_HEADS * HEAD_PAD), BF16, "row")], name="qrope_bwd")[0]


_NT = (((1,), (1,)), ((), ()))
_TN = (((0,), (0,)), ((), ()))


def _scores(q, kvr, diagonal):
    s = lax.dot_general(q, kvr, _NT, preferred_element_type=F32)
    if not diagonal:
        return s
    row = lax.broadcasted_iota(jnp.int32, s.shape, 0)
    col = lax.broadcasted_iota(jnp.int32, s.shape, 1)
    return jnp.where(col <= row, s, -jnp.inf)


def _on_causal_pairs(pair, k_blk, fn):
    @pl.when(k_blk < 2 * pair)
    def _():
        fn(0, False)
        fn(1, False)

    @pl.when(k_blk == 2 * pair)
    def _():
        fn(0, True)
        fn(1, False)

    @pl.when(k_blk == 2 * pair + 1)
    def _():
        fn(1, True)


def attn_fwd(q, kv, kr):
    s = q.shape[0]
    t = min(ATT_TILE, s // 2)
    n = s // t
    np_ = n // 2

    def body(q_ref, kv_ref, kr_ref, o_ref, lse_ref, m_ref, l_ref, acc_ref):
        i, j = pl.program_id(1), pl.program_id(2)

        @pl.when(j == 0)
        def _():
            m_ref[...] = jnp.full_like(m_ref, -jnp.inf)
            l_ref[...] = jnp.zeros_like(l_ref)
            acc_ref[...] = jnp.zeros_like(acc_ref)

        def block(sub, diagonal):
            kvv = kv_ref[...]
            kvr = jnp.concatenate([kvv, kr_ref[...]], axis=1)
            sc = _scores(q_ref[pl.ds(sub * t, t), :], kvr, diagonal)
            m_prev = m_ref[sub]
            m_new = jnp.maximum(m_prev, jnp.max(sc, axis=-1, keepdims=True))
            alpha = jnp.exp(m_prev - m_new)
            p = jnp.exp(sc - m_new)
            l_ref[sub] = alpha * l_ref[sub] + jnp.sum(p, axis=-1, keepdims=True)
            acc_ref[sub] = alpha * acc_ref[sub] + jnp.dot(p.astype(BF16), kvv, preferred_element_type=F32)
            m_ref[sub] = m_new

        _on_causal_pairs(i, j, block)

        @pl.when(j == 2 * i + 1)
        def _():
            for sub in range(2):
                l = l_ref[sub]
                o_ref[pl.ds(sub * t, t), :] = (acc_ref[sub] / l).astype(BF16)
                lse_ref[pl.ds(sub * t, t), :] = jnp.broadcast_to(m_ref[sub] + jnp.log(l), (t, LANES))

    kj = lambda h, i, j: (jnp.minimum(j, 2 * i + 1), h)
    return pl.pallas_call(
        body, name="attn_fwd", grid=(N_HEADS, np_, n),
        in_specs=[pl.BlockSpec((2 * t, HEAD_PAD), lambda h, i, j: (i, h)),
                  pl.BlockSpec((t, LANES), kj),
                  pl.BlockSpec((t, LANES), lambda h, i, j: (jnp.minimum(j, 2 * i + 1), 0))],
        out_specs=(pl.BlockSpec((2 * t, LANES), lambda h, i, j: (i, h)),
                   pl.BlockSpec((2 * t, LANES), lambda h, i, j: (i, h))),
        out_shape=(jax.ShapeDtypeStruct((s, N_HEADS * LANES), BF16),
                   jax.ShapeDtypeStruct((s, N_HEADS * LANES), F32)),
        scratch_shapes=[pltpu.VMEM((2, t, 1), F32), pltpu.VMEM((2, t, 1), F32), pltpu.VMEM((2, t, LANES), F32)],
        compiler_params=_cparams("parallel", "parallel", "arbitrary"),
    )(q, kv, kr)


def attn_bwd(q, kv, kr, o, do, lse):
    s = q.shape[0]
    t = min(ATT_TILE, s // 2)
    n = s // t
    np_ = n // 2

    def body(q_ref, kv_ref, kr_ref, o_ref, do_ref, lse_ref, dq_ref, dkv_ref, dkr_ref):
        j, i = pl.program_id(1), pl.program_id(2)

        @pl.when((i == 0) & (j == 0))
        def _():
            dq_ref[...] = jnp.zeros_like(dq_ref)

        @pl.when(i == 0)
        def _():
            dkv_ref[...] = jnp.zeros_like(dkv_ref)
            dkr_ref[...] = jnp.zeros_like(dkr_ref)

        def block(sub, diagonal):
            sl = pl.ds(sub * t, t)
            qv, dov, kvv = q_ref[sl, :], do_ref[sl, :], kv_ref[...]
            kvr = jnp.concatenate([kvv, kr_ref[...]], axis=1)
            p = jnp.exp(_scores(qv, kvr, diagonal) - lse_ref[sl, :1])
            dp = lax.dot_general(dov, kvv, _NT, preferred_element_type=F32)
            delta = jnp.sum(dov.astype(F32) * o_ref[sl, :].astype(F32), axis=-1, keepdims=True)
            ds = (p * (dp - delta)).astype(BF16)
            dk = lax.dot_general(ds, qv, _TN, preferred_element_type=F32)
            dkv_ref[...] += lax.dot_general(p.astype(BF16), dov, _TN, preferred_element_type=F32) + dk[:, :LANES]
            dkr_ref[...] += dk[:, LANES:]
            rows = pl.ds(pl.multiple_of((2 * i + sub) * t, t), t)
            dq_ref[rows, :] += jnp.dot(ds, kvr, preferred_element_type=F32)

        _on_causal_pairs(i, j, block)

    qi = lambda h, j, i: (jnp.maximum(i, lax.div(j, 2)), h)
    kj = lambda h, j, i: (j, h)
    return pl.pallas_call(
        body, name="attn_bwd", grid=(N_HEADS, n, np_),
        in_specs=[pl.BlockSpec((2 * t, HEAD_PAD), qi), pl.BlockSpec((t, LANES), kj),
                  pl.BlockSpec((t, LANES), lambda h, j, i: (j, 0)),
                  pl.BlockSpec((2 * t, LANES), qi), pl.BlockSpec((2 * t, LANES), qi), pl.BlockSpec((2 * t, LANES), qi)],
        out_specs=(pl.BlockSpec((s, HEAD_PAD), lambda h, j, i: (0, h)),
                   pl.BlockSpec((t, LANES), kj), pl.BlockSpec((t, LANES), kj)),
        out_shape=(jax.ShapeDtypeStruct((s, N_HEADS * HEAD_PAD), F32),
                   jax.ShapeDtypeStruct((s, N_HEADS * LANES), F32), jax.ShapeDtypeStruct((s, N_HEADS * LANES), F32)),
        compiler_params=_cparams("parallel", "arbitrary", "arbitrary"),
    )(q, kv, kr, o, do, lse)


def ffn_fwd(x, g, wa, wb, cwa, cwb, ba, bb, wd, name):
    s, d = x.shape
    f = wa.shape[1]
    ts, tf = min(FFN_ROW_TILE, s), FFN_COL_TILE
    hal = FFN_HALO
    nj = f // tf
    first = hal - (FFN_K - 1)
    rc = min(FFN_ROW_CHUNK, ts)

    def body(x_ref, xp_ref, g_ref, wa_ref, wb_ref, cwa_ref, cwb_ref, ba_ref, bb_ref, wd_ref,
             xo_ref, xn_ref, hpa_ref, hpb_ref, act_ref, xe_ref, ea_ref, eb_ref):
        i, j = pl.program_id(0), pl.program_id(1)

        @pl.when(j == 0)
        def _():
            xn = _rms(x_ref[...], g_ref[...]).astype(BF16)
            xn_ref[...] = xn
            xe_ref[pl.ds(hal, ts), :] = xn
            xe_ref[pl.ds(0, hal), :] = jnp.where(i > 0, _rms(xp_ref[...], g_ref[...]), 0.0).astype(BF16)
            xo_ref[...] = x_ref[...]

        halves = ((0, ts // 2), (ts // 2, ts))
        for lo, hi in halves:
            e0, e1 = (0 if lo == 0 else hal + lo), hal + hi
            xe = xe_ref[pl.ds(e0, e1 - e0), :]
            ea_ref[pl.ds(e0, e1 - e0), :] = jnp.dot(xe, wa_ref[...], preferred_element_type=F32)
            eb_ref[pl.ds(e0, e1 - e0), :] = jnp.dot(xe, wb_ref[...], preferred_element_type=F32)
        for lo, hi in halves:
            hpa_ref[pl.ds(lo, hi - lo), :] = ea_ref[pl.ds(hal + lo, hi - lo), :].astype(BF16)
            hpb_ref[pl.ds(lo, hi - lo), :] = eb_ref[pl.ds(hal + lo, hi - lo), :].astype(BF16)
            for r0 in range(lo, hi, rc):
                ha = _conv_taps(ea_ref, cwa_ref, rc, first + r0, FFN_K) + ba_ref[...]
                hb = _conv_taps(eb_ref, cwb_ref, rc, first + r0, FFN_K) + bb_ref[...]
                act_ref[pl.ds(r0, rc), :] = (_silu(ha) * hb).astype(BF16)
            xo_ref[pl.ds(lo, hi - lo), :] += jnp.dot(act_ref[pl.ds(lo, hi - lo), :], wd_ref[...],
                                                     preferred_element_type=F32)

    r = ts // hal
    row = pl.BlockSpec((ts, d), lambda i, j: (i, 0))
    prev = pl.BlockSpec((hal, d), lambda i, j: (jnp.maximum(i * r - 1, 0), 0))
    gsp = pl.BlockSpec((1, d), lambda i, j: (0, 0))
    wup = pl.BlockSpec((d, tf), lambda i, j: (0, j))
    cwsp = pl.BlockSpec((8, tf), lambda i, j: (0, j))
    bsp = pl.BlockSpec((1, tf), lambda i, j: (0, j))
    wdn = pl.BlockSpec((tf, d), lambda i, j: (j, 0))
    hid = pl.BlockSpec((ts, tf), lambda i, j: (i, j))
    return pl.pallas_call(
        body, name=name, grid=(s // ts, nj),
        in_specs=[row, prev, gsp, wup, wup, cwsp, cwsp, bsp, bsp, wdn],
        out_specs=(row, row, hid, hid, hid),
        out_shape=(jax.ShapeDtypeStruct((s, d), F32), jax.ShapeDtypeStruct((s, d), BF16),
                   jax.ShapeDtypeStruct((s, f), BF16), jax.ShapeDtypeStruct((s, f), BF16),
                   jax.ShapeDtypeStruct((s, f), BF16)),
        scratch_shapes=[pltpu.VMEM((ts + hal, d), BF16), pltpu.VMEM((ts + hal, tf), F32),
                        pltpu.VMEM((ts + hal, tf), F32)],
        compiler_params=_cparams("parallel", "arbitrary"),
    )(x, x, g, wa, wb, cwa, cwb, ba, bb, wd)


def ffn_bwd(dy, x, g, hpa, hpb, wa, wb, cwa, cwb, ba, bb, wd, name):
    s, d = dy.shape
    f = hpa.shape[1]
    ts, tf = min(FFN_ROW_TILE, s), FFN_COL_TILE
    hal = FFN_HALO
    nt, nj = s // ts, f // tf
    te = ts + hal
    first = hal - (FFN_K - 1)
    rc = min(FFN_ROW_CHUNK, ts)

    def body(dy_ref, dyn_ref, x_ref, g_ref, a_ref, ap_ref, an_ref, b_ref, bp_ref, bn_ref, wa_ref, wb_ref,
             cwa_ref, cwb_ref, ba_ref, bb_ref, wd_ref,
             dx_ref, dpa_ref, dpb_ref, dwa_ref, dwb_ref, dba_ref, dbb_ref, dg_ref,
             dye_ref, ea_ref, eb_ref, dact_ref, da_ref, db_ref, dxn_ref):
        i, j = pl.program_id(0), pl.program_id(1)
        last = i == nt - 1

        @pl.when(j == 0)
        def _():
            dye_ref[pl.ds(0, ts), :] = dy_ref[...].astype(BF16)
            dye_ref[pl.ds(ts, hal), :] = jnp.where(last, 0.0, dyn_ref[...]).astype(BF16)
            dxn_ref[...] = jnp.zeros_like(dxn_ref)

        @pl.when((i == 0) & (j == 0))
        def _():
            for r in (dwa_ref, dwb_ref, dba_ref, dbb_ref, dg_ref):
                r[...] = jnp.zeros_like(r)

        halves = tuple((k * ts // FFN_BWD_PARTS, (k + 1) * ts // FFN_BWD_PARTS) for k in range(FFN_BWD_PARTS))
        for lo, hi in halves:
            n = hi - lo + (hal if hi == ts else 0)
            dact_ref[pl.ds(lo, n), :] = lax.dot_general(dye_ref[pl.ds(lo, n), :], wd_ref[...], _NT,
                                                        preferred_element_type=F32)
        for cur, prev, nxt, ext in ((a_ref, ap_ref, an_ref, ea_ref), (b_ref, bp_ref, bn_ref, eb_ref)):
            ext[pl.ds(0, hal), :] = jnp.where(i > 0, prev[...].astype(F32), 0.0)
            ext[pl.ds(hal, ts), :] = cur[...].astype(F32)
            ext[pl.ds(hal + ts, hal), :] = jnp.where(last, 0.0, nxt[...].astype(F32))
        zero = jnp.zeros((1, tf), F32)
        sums = {"ba": zero, "bb": zero, **{("a", k): zero for k in range(FFN_K)}, **{("b", k): zero for k in range(FFN_K)}}
        for r0 in list(range(0, ts, rc)) + [ts]:
            n = rc if r0 < ts else hal
            win_a = [ea_ref[pl.ds(first + r0 + k, n), :] for k in range(FFN_K)]
            win_b = [eb_ref[pl.ds(first + r0 + k, n), :] for k in range(FFN_K)]
            ha = sum(cwa_ref[pl.ds(k, 1), :] * win_a[k] for k in range(FFN_K)) + ba_ref[...]
            hb = sum(cwb_ref[pl.ds(k, 1), :] * win_b[k] for k in range(FFN_K)) + bb_ref[...]
            sig = _sigmoid(ha)
            gs = dact_ref[pl.ds(r0, n), :] * sig
            dha = gs * hb * (1.0 + ha * (1.0 - sig))
            dhb = gs * ha
            da_ref[pl.ds(r0, n), :] = dha
            db_ref[pl.ds(r0, n), :] = dhb
            if r0 < ts:
                sums["ba"] = sums["ba"] + jnp.sum(dha, axis=0, keepdims=True)
                sums["bb"] = sums["bb"] + jnp.sum(dhb, axis=0, keepdims=True)
                for k in range(FFN_K):
                    sums["a", k] = sums["a", k] + jnp.sum(dha * win_a[k], axis=0, keepdims=True)
                    sums["b", k] = sums["b", k] + jnp.sum(dhb * win_b[k], axis=0, keepdims=True)
        for lo, hi in halves:
            for r0 in range(lo, hi, rc):
                dpa_ref[pl.ds(r0, rc), :] = _conv_taps(da_ref, cwa_ref, rc, r0, FFN_K, flip=True).astype(BF16)
                dpb_ref[pl.ds(r0, rc), :] = _conv_taps(db_ref, cwb_ref, rc, r0, FFN_K, flip=True).astype(BF16)
            rows = pl.ds(lo, hi - lo)
            dxn_ref[rows, :] += (lax.dot_general(dpa_ref[rows, :], wa_ref[...], _NT, preferred_element_type=F32)
                                 + lax.dot_general(dpb_ref[rows, :], wb_ref[...], _NT, preferred_element_type=F32))
        dba_ref[j] += jnp.broadcast_to(sums["ba"], (8, tf))
        dbb_ref[j] += jnp.broadcast_to(sums["bb"], (8, tf))
        row = lax.broadcasted_iota(jnp.int32, (8, tf), 0)
        dwa_ref[j] += sum(jnp.where(row == k, sums["a", k], 0.0) for k in range(FFN_K))
        dwb_ref[j] += sum(jnp.where(row == k, sums["b", k], 0.0) for k in range(FFN_K))

        @pl.when(j == nj - 1)
        def _():
            _, vjp = jax.vjp(_rms, x_ref[...], g_ref[...])
            dx, dg = vjp(dxn_ref[...])
            dx_ref[...] = dx + dy_ref[...]
            dg_ref[...] += jnp.broadcast_to(jnp.sum(dg, axis=0, keepdims=True), dg_ref.shape)

    r = ts // hal
    lastblk = s // hal - 1
    row = pl.BlockSpec((ts, d), lambda i, j: (i, 0))
    gsp = pl.BlockSpec((1, d), lambda i, j: (0, 0))
    dgsp = pl.BlockSpec((8, d), lambda i, j: (0, 0))
    rown = pl.BlockSpec((hal, d), lambda i, j: (jnp.minimum((i + 1) * r, lastblk), 0))
    cur = pl.BlockSpec((ts, tf), lambda i, j: (i, j))
    prev = pl.BlockSpec((hal, tf), lambda i, j: (jnp.maximum(i * r - 1, 0), j))
    nxt = pl.BlockSpec((hal, tf), lambda i, j: (jnp.minimum((i + 1) * r, lastblk), j))
    wup = pl.BlockSpec((d, tf), lambda i, j: (0, j))
    cwsp = pl.BlockSpec((8, tf), lambda i, j: (0, j))
    bsp = pl.BlockSpec((1, tf), lambda i, j: (0, j))
    wdn = pl.BlockSpec((tf, d), lambda i, j: (j, 0))
    accsp = pl.BlockSpec((nj, 8, tf), lambda i, j: (0, 0, 0))
    accshape = jax.ShapeDtypeStruct((nj, 8, tf), F32)
    return pl.pallas_call(
        body, name=name, grid=(nt, nj),
        in_specs=[row, rown, row, gsp, cur, prev, nxt, cur, prev, nxt, wup, wup, cwsp, cwsp, bsp, bsp, wdn],
        out_specs=(row, cur, cur, accsp, accsp, accsp, accsp, dgsp),
        out_shape=(jax.ShapeDtypeStruct((s, d), F32), jax.ShapeDtypeStruct((s, f), BF16),
                   jax.ShapeDtypeStruct((s, f), BF16), accshape, accshape, accshape, accshape,
                   jax.ShapeDtypeStruct((8, d), F32)),
        scratch_shapes=[pltpu.VMEM((te, d), BF16), pltpu.VMEM((ts + 2 * hal, tf), F32),
                        pltpu.VMEM((ts + 2 * hal, tf), F32), pltpu.VMEM((te, tf), F32),
                        pltpu.VMEM((te, tf), F32), pltpu.VMEM((te, tf), F32), pltpu.VMEM((ts, d), F32)],
        compiler_params=_cparams("arbitrary", "arbitrary"),
    )(dy, dy, x, g, hpa, hpa, hpa, hpb, hpb, hpb, wa, wb, cwa, cwb, ba, bb, wd)


NQ = 4
SQ = SSM_STATE * 8
NS = SSM_GROUPS * SSM_STATE


def _s5_disc(log_dt, a_re, a_im, b_re, b_im, expand):
    dt = jnp.exp(log_dt)
    mag = jnp.exp(a_re * dt)
    lb_re, lb_im = mag * jnp.cos(a_im * dt), mag * jnp.sin(a_im * dt)
    den = a_re * a_re + a_im * a_im
    nr, ni = lb_re - 1.0, lb_im
    f_re = (nr * a_re + ni * a_im) / den
    f_im = (ni * a_re - nr * a_im) / den
    fe_re = jnp.dot(f_re, expand, precision=lax.Precision.HIGHEST, preferred_element_type=F32)
    fe_im = jnp.dot(f_im, expand, precision=lax.Precision.HIGHEST, preferred_element_type=F32)
    return lb_re, lb_im, fe_re * b_re - fe_im * b_im, fe_re * b_im + fe_im * b_re


def _expand_matrix():
    e = np.zeros((SSM_STATE, SSM_STATE * SSM_GROUP), np.float32)
    for p in range(SSM_STATE):
        e[p, p * SSM_GROUP:(p + 1) * SSM_GROUP] = 1.0
    return jnp.asarray(e)


def s5_params_fwd(log_dt, a_re, a_im, b_re, b_im):
    expand = _expand_matrix()

    def body(ld_ref, ar_ref, ai_ref, br_ref, bi_ref, e_ref, lr_ref, li_ref, bbr_ref, bbi_ref):
        lr, li, bbr, bbi = _s5_disc(ld_ref[...], ar_ref[...], ai_ref[...], br_ref[...], bi_ref[...], e_ref[...])
        lr_ref[...] = lr
        li_ref[...] = li
        bbr_ref[...] = bbr
        bbi_ref[...] = bbi

    g, p, pc = SSM_GROUPS, SSM_STATE, SSM_STATE * SSM_GROUP
    return pl.pallas_call(
        body, name="s5_params_fwd",
        out_shape=(jax.ShapeDtypeStruct((g, p), F32), jax.ShapeDtypeStruct((g, p), F32),
                   jax.ShapeDtypeStruct((g, pc), F32), jax.ShapeDtypeStruct((g, pc), F32)),
    )(log_dt, a_re, a_im, b_re, b_im, expand)


def s5_params_bwd(log_dt, a_re, a_im, b_re, b_im, dlr, dli, dbbr, dbbi):
    expand = _expand_matrix()

    def body(ld_ref, ar_ref, ai_ref, br_ref, bi_ref, e_ref, dlr_ref, dli_ref, dbbr_ref, dbbi_ref,
             dld_ref, dar_ref, dai_ref, dbr_ref, dbi_ref):
        e = e_ref[...]
        f = lambda ld, ar, ai, br, bi: _s5_disc(ld, ar, ai, br, bi, e)
        _, vjp = jax.vjp(f, ld_ref[...], ar_ref[...], ai_ref[...], br_ref[...], bi_ref[...])
        dld, dar, dai, dbr, dbi = vjp((dlr_ref[...], dli_ref[...], dbbr_ref[...], dbbi_ref[...]))
        dld_ref[...] = dld
        dar_ref[...] = dar
        dai_ref[...] = dai
        dbr_ref[...] = dbr
        dbi_ref[...] = dbi

    g, p, pc = SSM_GROUPS, SSM_STATE, SSM_STATE * SSM_GROUP
    return pl.pallas_call(
        body, name="s5_params_bwd",
        out_shape=(jax.ShapeDtypeStruct((g, 1), F32), jax.ShapeDtypeStruct((g, p), F32),
                   jax.ShapeDtypeStruct((g, p), F32), jax.ShapeDtypeStruct((g, pc), F32),
                   jax.ShapeDtypeStruct((g, pc), F32)),
    )(log_dt, a_re, a_im, b_re, b_im, expand, dlr, dli, dbbr, dbbi)


def _cmul(ar, ai, br, bi):
    return ar * br - ai * bi, ar * bi + ai * br


def _power_rows(lr, li, conj_rev):
    row = lax.broadcasted_iota(jnp.int32, (8, NS), 0)
    tr = jnp.zeros((8, NS), F32)
    ti = jnp.zeros((8, NS), F32)
    pr, pi = lr, li
    for r in range(8):
        dst = 7 - r if conj_rev else r
        tr = jnp.where(row == dst, pr, tr)
        ti = jnp.where(row == dst, -pi if conj_rev else pi, ti)
        if r < 7:
            pr, pi = _cmul(pr, pi, lr, li)
    return tr, ti


def _scan8(xr, xi, tr_ref, ti_ref, cr, ci, reverse):
    row = lax.broadcasted_iota(jnp.int32, xr.shape, 0)
    for d in (1, 2, 4):
        if reverse:
            sr, si = pltpu.roll(xr, 8 - d, 0), pltpu.roll(xi, 8 - d, 0)
            keep = row < 8 - d
            pw = 8 - d
        else:
            sr, si = pltpu.roll(xr, d, 0), pltpu.roll(xi, d, 0)
            keep = row >= d
            pw = d - 1
        mr, mi = _cmul(tr_ref[pl.ds(pw, 1), :], ti_ref[pl.ds(pw, 1), :], sr, si)
        xr = xr + jnp.where(keep, mr, 0.0)
        xi = xi + jnp.where(keep, mi, 0.0)
    mr, mi = _cmul(tr_ref[...], ti_ref[...], cr, ci)
    return xr + mr, xi + mi


def _row_of(x, r):
    row = lax.broadcasted_iota(jnp.int32, x.shape, 0)
    return jnp.sum(jnp.where(row == r, x, 0.0), axis=0, keepdims=True)


def s5_scan_fwd(u, lam_r, lam_i, bre, bim, cre, cim, dskip):
    s = u.shape[0]
    tt = min(SCAN_TILE, s)
    nb = tt // 8

    def body(nt, u_ref, lr_ref, li_ref, bre_ref, bim_ref, cre_ref, cim_ref, d_ref,
             xr_ref, xi_ref, y_ref, yg_ref, tr_ref, ti_ref, cr_ref, ci_ref):
        i = pl.program_id(0)

        @pl.when(i == 0)
        def _():
            tr, ti = _power_rows(lr_ref[...], li_ref[...], False)
            tr_ref[...] = tr
            ti_ref[...] = ti
            cr_ref[...] = jnp.zeros_like(cr_ref)
            ci_ref[...] = jnp.zeros_like(ci_ref)

        uv = u_ref[...]
        ub = uv.astype(BF16)
        for q in range(NQ):
            uq = ub[:, q * LANES:(q + 1) * LANES]
            xr_ref[:, q * SQ:(q + 1) * SQ] = jnp.dot(uq, bre_ref[q], preferred_element_type=F32)
            xi_ref[:, q * SQ:(q + 1) * SQ] = jnp.dot(uq, bim_ref[q], preferred_element_type=F32)

        def step(b, carry):
            cr, ci = carry
            rows = pl.ds(pl.multiple_of(b * 8, 8), 8)
            xr, xi = _scan8(xr_ref[rows, :], xi_ref[rows, :], tr_ref, ti_ref, cr, ci, False)
            xr_ref[rows, :] = xr
            xi_ref[rows, :] = xi
            return _row_of(xr, 7), _row_of(xi, 7)

        cr, ci = lax.fori_loop(0, nb, step, (cr_ref[...], ci_ref[...]), unroll=min(SCAN_UNROLL, nb))
        cr_ref[...] = cr
        ci_ref[...] = ci
        y = d_ref[...] * uv
        for q in range(NQ):
            yq = (jnp.dot(xr_ref[:, q * SQ:(q + 1) * SQ].astype(BF16), cre_ref[q], preferred_element_type=F32)
                  - jnp.dot(xi_ref[:, q * SQ:(q + 1) * SQ].astype(BF16), cim_ref[q], preferred_element_type=F32))
            y_ref[:, q * LANES:(q + 1) * LANES] = yq + y[:, q * LANES:(q + 1) * LANES]
        yg_ref[...] = _gelu(y_ref[...]).astype(BF16)

    return rowcall(body, rows=s, ts=tt,
                   ins=[(u, "row"), (lam_r, "full"), (lam_i, "full"), (bre, "full"), (bim, "full"),
                        (cre, "full"), (cim, "full"), (dskip, "full")],
                   outs=[((s, NS), F32, "row"), ((s, NS), F32, "row"), ((s, SSM_WIDTH), F32, "row"),
                         ((s, SSM_WIDTH), BF16, "row")], name="s5_scan_fwd",
                   scratch=[pltpu.VMEM((8, NS), F32), pltpu.VMEM((8, NS), F32),
                            pltpu.VMEM((1, NS), F32), pltpu.VMEM((1, NS), F32)])


def s5_scan_bwd(dyg, y, lam_r, lam_i, cre, cim):
    s = y.shape[0]
    tt = min(SCAN_TILE, s)
    nb = tt // 8

    def body(nt, dyg_ref, y_ref, lr_ref, li_ref, cre_ref, cim_ref,
             ar_ref, ai_ref, dy_ref, tr_ref, ti_ref, cr_ref, ci_ref):
        i = pl.program_id(0)

        @pl.when(i == 0)
        def _():
            tr, ti = _power_rows(lr_ref[...], li_ref[...], True)
            tr_ref[...] = tr
            ti_ref[...] = ti
            cr_ref[...] = jnp.zeros_like(cr_ref)
            ci_ref[...] = jnp.zeros_like(ci_ref)

        _, vjp = jax.vjp(_gelu, y_ref[...])
        dy = vjp(dyg_ref[...])[0]
        dyb = dy.astype(BF16)
        dy_ref[...] = dyb
        for q in range(NQ):
            dq = dyb[:, q * LANES:(q + 1) * LANES]
            ar_ref[:, q * SQ:(q + 1) * SQ] = lax.dot_general(dq, cre_ref[q], _NT, preferred_element_type=F32)
            ai_ref[:, q * SQ:(q + 1) * SQ] = -lax.dot_general(dq, cim_ref[q], _NT, preferred_element_type=F32)

        def step(b, carry):
            cr, ci = carry
            rows = pl.ds(pl.multiple_of((nb - 1 - b) * 8, 8), 8)
            xr, xi = _scan8(ar_ref[rows, :], ai_ref[rows, :], tr_ref, ti_ref, cr, ci, True)
            ar_ref[rows, :] = xr
            ai_ref[rows, :] = xi
            return _row_of(xr, 0), _row_of(xi, 0)

        cr, ci = lax.fori_loop(0, nb, step, (cr_ref[...], ci_ref[...]), unroll=min(SCAN_UNROLL, nb))
        cr_ref[...] = cr
        ci_ref[...] = ci

    return rowcall(body, rows=s, ts=tt,
                   ins=[(dyg, "rev"), (y, "rev"), (lam_r, "full"), (lam_i, "full"), (cre, "full"), (cim, "full")],
                   outs=[((s, NS), F32, "rev"), ((s, NS), F32, "rev"), ((s, SSM_WIDTH), BF16, "rev")],
                   name="s5_scan_bwd",
                   scratch=[pltpu.VMEM((8, NS), F32), pltpu.VMEM((8, NS), F32),
                            pltpu.VMEM((1, NS), F32), pltpu.VMEM((1, NS), F32)])


def s5_grads(u, dy, xr, xi, ar, ai, bre, bim, dskip):
    s = u.shape[0]
    tt = min(SCAN_TILE, s)

    def body(nt, u_ref, dy_ref, xr_ref, xrp_ref, xi_ref, xip_ref, ar_ref, ai_ref, bre_ref, bim_ref, d_ref,
             du_ref, dlr_ref, dli_ref, dbr_ref, dbi_ref, dcr_ref, dci_ref, dd_ref, er_ref, ei_ref):
        i = pl.program_id(0)

        @pl.when(i == 0)
        def _():
            for r in (dbr_ref, dbi_ref, dcr_ref, dci_ref):
                r[...] = jnp.zeros_like(r)

        uv, dyb = u_ref[...], dy_ref[...]
        dyf = dyb.astype(F32)
        av_r, av_i, xv_r, xv_i = ar_ref[...], ai_ref[...], xr_ref[...], xi_ref[...]
        er_ref[pl.ds(0, 8), :] = jnp.where(i > 0, xrp_ref[...], 0.0)
        ei_ref[pl.ds(0, 8), :] = jnp.where(i > 0, xip_ref[...], 0.0)
        er_ref[pl.ds(8, tt), :] = xv_r
        ei_ref[pl.ds(8, tt), :] = xv_i
        sr, si = er_ref[pl.ds(7, tt), :], ei_ref[pl.ds(7, tt), :]
        _acc(dlr_ref, i, av_r * sr + av_i * si)
        _acc(dli_ref, i, av_i * sr - av_r * si)
        _acc(dd_ref, i, dyf * uv)
        ub = uv.astype(BF16)
        ab_r, ab_i = av_r.astype(BF16), av_i.astype(BF16)
        xb_r, xb_i = xv_r.astype(BF16), xv_i.astype(BF16)
        du = d_ref[...] * dyf
        for q in range(NQ):
            cs, ss = slice(q * LANES, (q + 1) * LANES), slice(q * SQ, (q + 1) * SQ)
            dbr_ref[q] += lax.dot_general(ub[:, cs], ab_r[:, ss], _TN, preferred_element_type=F32)
            dbi_ref[q] += lax.dot_general(ub[:, cs], ab_i[:, ss], _TN, preferred_element_type=F32)
            dcr_ref[q] += lax.dot_general(xb_r[:, ss], dyb[:, cs], _TN, preferred_element_type=F32)
            dci_ref[q] -= lax.dot_general(xb_i[:, ss], dyb[:, cs], _TN, preferred_element_type=F32)
            du_ref[:, cs] = (du[:, cs]
                             + lax.dot_general(ab_r[:, ss], bre_ref[q], _NT, preferred_element_type=F32)
                             + lax.dot_general(ab_i[:, ss], bim_ref[q], _NT, preferred_element_type=F32))

    return rowcall(body, rows=s, ts=tt,
                   ins=[(u, "row"), (dy, "row"), (xr, "row"), (xr, "prev:8"), (xi, "row"), (xi, "prev:8"),
                        (ar, "row"), (ai, "row"), (bre, "full"), (bim, "full"), (dskip, "full")],
                   outs=[((s, SSM_WIDTH), F32, "row"), ((8, NS), F32, "acc"), ((8, NS), F32, "acc"),
                         ((NQ, LANES, SQ), F32, "acc"), ((NQ, LANES, SQ), F32, "acc"),
                         ((NQ, SQ, LANES), F32, "acc"), ((NQ, SQ, LANES), F32, "acc"),
                         ((8, SSM_WIDTH), F32, "acc")], name="s5_grads",
                   scratch=[pltpu.VMEM((tt + 8, NS), F32), pltpu.VMEM((tt + 8, NS), F32)])


def _glu_fn(za, zb):
    return za * _sigmoid(zb)


def glu_proj_res(yg, w_glu, b_glu, xres):
    s = yg.shape[0]

    def body(nt, y_ref, w_ref, b_ref, x_ref, o_ref, z_ref):
        yv = y_ref[...]
        za = jnp.dot(yv, w_ref[:, :D_MODEL], preferred_element_type=F32) + b_ref[:, :D_MODEL]
        zb = jnp.dot(yv, w_ref[:, D_MODEL:], preferred_element_type=F32) + b_ref[:, D_MODEL:]
        z_ref[:, :D_MODEL] = za.astype(BF16)
        z_ref[:, D_MODEL:] = zb.astype(BF16)
        o_ref[...] = x_ref[...] + _glu_fn(za, zb)

    return rowcall(body, rows=s, ts=min(ROW_TILE, s),
                   ins=[(yg, "row"), (w_glu, "full"), (b_glu, "full"), (xres, "row")],
                   outs=[((s, D_MODEL), F32, "row"), ((s, 2 * D_MODEL), BF16, "row")], name="l1_glu_proj_res")


def glu_bwd(z, dout):
    s, c = z.shape

    def body(nt, z_ref, d_ref, dz_ref, dba_ref, dbb_ref):
        i = pl.program_id(0)
        _, vjp = jax.vjp(_glu_fn, z_ref[:, :D_MODEL].astype(F32), z_ref[:, D_MODEL:].astype(F32))
        dza, dzb = vjp(d_ref[...])
        dz_ref[:, :D_MODEL] = dza.astype(BF16)
        dz_ref[:, D_MODEL:] = dzb.astype(BF16)
        _acc(dba_ref, i, dza)
        _acc(dbb_ref, i, dzb)

    return rowcall(body, rows=s, ts=min(ROW_TILE, s), ins=[(z, "row"), (dout, "row")],
                   outs=[((s, c), BF16, "row"), ((8, D_MODEL), F32, "acc"), ((8, D_MODEL), F32, "acc")],
                   name="glu_bwd")


def loss_head(x, g, target):
    s, c = x.shape

    def body(nt, x_ref, g_ref, t_ref, loss_ref, dx_ref, dg_ref):
        i = pl.program_id(0)
        y, vjp = jax.vjp(_rms, x_ref[...], g_ref[...])
        err = y - t_ref[...]
        dx, dg = vjp(err * (1.0 / c))
        dx_ref[...] = dx
        _acc(dg_ref, i, dg)
        part = jnp.sum(jnp.sum(err * err, axis=-1, keepdims=True), axis=0, keepdims=True) * (0.5 / c)

        @pl.when(i == 0)
        def _():
            loss_ref[...] = jnp.zeros_like(loss_ref)

        loss_ref[...] += jnp.broadcast_to(part, loss_ref.shape)

    return rowcall(body, rows=s, ts=min(ROW_TILE, s), ins=[(x, "row"), (g, "full"), (target, "row")],
                   outs=[((8, LANES), F32, "acc"), ((s, c), F32, "row"), ((8, c), F32, "acc")], name="loss_head")


def _tile_rows(r, cands=(512, 256, 128, 64, 32, 16, 8)):
    return _pick(r, cands)


def add_to_bf16(a, b, name):
    n, r, c = a.shape
    tr = _tile_rows(r)

    def body(a_ref, b_ref, o_ref):
        o_ref[...] = (a_ref[...].astype(F32) + b_ref[...].astype(F32)).astype(BF16)

    spec = pl.BlockSpec((1, tr, c), lambda j, i: (j, i, 0))
    return pl.pallas_call(body, name=name, grid=(n, r // tr), in_specs=[spec, spec], out_specs=spec,
                          out_shape=jax.ShapeDtypeStruct((n, r, c), BF16),
                          compiler_params=_cparams("parallel", "parallel"))(a, b)


def sum_leading(a, name):
    n, r, c = a.shape
    tr = _tile_rows(r)

    def body(a_ref, o_ref):
        acc = a_ref[0].astype(F32)
        for k in range(1, n):
            acc = acc + a_ref[k].astype(F32)
        o_ref[...] = acc

    return pl.pallas_call(body, name=name, grid=(r // tr,),
                          in_specs=[pl.BlockSpec((n, tr, c), lambda i: (0, i, 0))],
                          out_specs=pl.BlockSpec((tr, c), lambda i: (i, 0)),
                          out_shape=jax.ShapeDtypeStruct((r, c), F32),
                          compiler_params=_cparams("parallel"))(a)


def adamw(w, g, m, v, name):
    r, c = w.shape
    tr = _tile_rows(r, (256, 128, 64, 32, 16, 8))
    c1 = 1.0 - ADAM_B1 ** ADAM_STEP
    c2 = 1.0 - ADAM_B2 ** ADAM_STEP

    def body(w_ref, g_ref, m_ref, v_ref, d_ref, nm_ref, nv_ref):
        gv = g_ref[...]
        mn = ADAM_B1 * m_ref[...] + (1.0 - ADAM_B1) * gv
        vn = ADAM_B2 * v_ref[...] + (1.0 - ADAM_B2) * (gv * gv)
        d_ref[...] = -ADAM_LR * ((mn / c1) / (jnp.sqrt(vn / c2) + ADAM_EPS) + ADAM_WD * w_ref[...])
        nm_ref[...] = mn
        nv_ref[...] = vn

    spec = pl.BlockSpec((tr, c), lambda i: (i, 0))
    shp = jax.ShapeDtypeStruct((r, c), F32)
    return pl.pallas_call(body, name=name, grid=(r // tr,), in_specs=[spec] * 4, out_specs=(spec,) * 3,
                          out_shape=(shp,) * 3, compiler_params=_cparams("parallel"))(w, g, m, v)


_ANY = pl.BlockSpec(memory_space=pl.ANY)


def all_gather8(block, name):
    r, c = block.shape

    def body(x_ref, out_ref, send_sems, recv_sems, local_sem):
        x, y, cc = lax.axis_index("x"), lax.axis_index("y"), lax.axis_index("c")
        me, sibling = (x, y, cc), (x, y, 1 - cc)
        chips = [(1 - x, y), (x, 1 - y), (1 - x, 1 - y)]

        def slot(px, py, pc):
            return out_ref.at[4 * px + 2 * py + pc]

        def copy(k, blk, to, src=None):
            return pltpu.make_async_remote_copy(
                src_ref=slot(*blk) if src is None else src, dst_ref=slot(*blk),
                send_sem=send_sems.at[k], recv_sem=recv_sems.at[k], device_id=to, device_id_type=MESH)

        mine = pltpu.make_async_copy(x_ref, slot(*me), local_sem)
        mine.start()
        first = [copy(0, me, sibling, src=x_ref)]
        first += [copy(1 + j, me, (*chip, cc), src=x_ref) for j, chip in enumerate(chips)]
        for cp in first:
            cp.start()
        passed = [copy(4 + j, (*chip, cc), sibling) for j, chip in enumerate(chips)]
        for j, chip in enumerate(chips):
            copy(1 + j, (*chip, cc), me).wait_recv()
            passed[j].start()
        copy(0, sibling, me).wait_recv()
        for j, chip in enumerate(chips):
            copy(4 + j, (*chip, 1 - cc), me).wait_recv()
        for cp in first + passed:
            cp.wait_send()
        mine.wait()

    return pl.pallas_call(
        body, name=name, in_specs=[_ANY], out_specs=_ANY,
        out_shape=jax.ShapeDtypeStruct((8, r, c), block.dtype),
        scratch_shapes=[pltpu.SemaphoreType.DMA((7,)), pltpu.SemaphoreType.DMA((7,)), pltpu.SemaphoreType.DMA],
    )(block)


_HBM = pl.BlockSpec(memory_space=pltpu.HBM)
_SEM = pl.BlockSpec(memory_space=pltpu.SEMAPHORE)
_DATAFLOW = pltpu.SideEffectType.DATAFLOW_SIDE_EFFECTING
N_REMOTE = 6


def _remote_peers(x, y, cc):
    return [(1 - x, y, cc), (x, 1 - y, cc), (1 - x, 1 - y, cc),
            (1 - x, y, 1 - cc), (x, 1 - y, 1 - cc), (1 - x, 1 - y, 1 - cc)]


def gather_start(block, after, name):
    r, c = block.shape

    def body(x_ref, land_ref, after_ref, send_sems, recv_sems, x_thru, land_thru, token):
        x, y, cc = lax.axis_index("x"), lax.axis_index("y"), lax.axis_index("c")
        for k, peer in enumerate(_remote_peers(x, y, cc)):
            pltpu.make_async_remote_copy(src_ref=x_ref, dst_ref=land_ref.at[4 * x + 2 * y + cc],
                                         send_sem=send_sems.at[k], recv_sem=recv_sems.at[k],
                                         device_id=peer, device_id_type=MESH).start()
        token[...] = jnp.zeros_like(token)

    land = pltpu.with_memory_space_constraint(lax.empty((8, r, c), block.dtype), pltpu.HBM)
    return pl.pallas_call(
        body, name=name,
        out_shape=(pltpu.SemaphoreType.DMA((N_REMOTE,)), pltpu.SemaphoreType.DMA((N_REMOTE,)),
                   pltpu.HBM((r, c), block.dtype), pltpu.HBM((8, r, c), block.dtype),
                   jax.ShapeDtypeStruct((8, LANES), F32)),
        in_specs=(_HBM, _HBM, _ANY), out_specs=(_SEM, _SEM, _HBM, _HBM, pl.BlockSpec(memory_space=pltpu.VMEM)),
        input_output_aliases={0: 2, 1: 3},
        compiler_params=pltpu.CompilerParams(has_side_effects=_DATAFLOW),
    )(pltpu.with_memory_space_constraint(block, pltpu.HBM), land, after)


def gather_wait(send_sems, recv_sems, block_thru, land_thru, after, name):
    def body(x_ref, land_ref, send_sems, recv_sems, after_ref, x_dead, got_ref):
        x, y, cc = lax.axis_index("x"), lax.axis_index("y"), lax.axis_index("c")
        for k, (px, py, pc) in enumerate(_remote_peers(x, y, cc)):
            cp = pltpu.make_async_remote_copy(src_ref=x_ref, dst_ref=land_ref.at[4 * px + 2 * py + pc],
                                              send_sem=send_sems.at[k], recv_sem=recv_sems.at[k],
                                              device_id=(px, py, pc), device_id_type=MESH)
            cp.wait_send()
            cp.wait_recv()

    return pl.pallas_call(
        body, name=name,
        out_shape=(pltpu.HBM(block_thru.shape, block_thru.dtype), pltpu.HBM(land_thru.shape, land_thru.dtype)),
        in_specs=(_HBM, _HBM, _SEM, _SEM, _ANY), out_specs=(_HBM, _HBM), input_output_aliases={0: 0, 1: 1},
        compiler_params=pltpu.CompilerParams(has_side_effects=_DATAFLOW),
    )(block_thru, land_thru, send_sems, recv_sems, after)[1]


def _chip_peers(x, y):
    return [(1 - x, y), (x, 1 - y), (1 - x, 1 - y)]


def exchange_start(parts, name):
    def body(p_ref, land_ref, send_sems, recv_sems, p_thru, land_thru, token):
        x, y, cc = lax.axis_index("x"), lax.axis_index("y"), lax.axis_index("c")
        for k, (px, py) in enumerate(_chip_peers(x, y)):
            pltpu.make_async_remote_copy(src_ref=p_ref.at[2 * px + py], dst_ref=land_ref.at[2 * x + y],
                                         send_sem=send_sems.at[k], recv_sem=recv_sems.at[k],
                                         device_id=(px, py, cc), device_id_type=MESH).start()
        token[...] = jnp.zeros_like(token)

    land = pltpu.with_memory_space_constraint(lax.empty(parts.shape, parts.dtype), pltpu.HBM)
    return pl.pallas_call(
        body, name=name,
        out_shape=(pltpu.SemaphoreType.DMA((3,)), pltpu.SemaphoreType.DMA((3,)),
                   pltpu.HBM(parts.shape, parts.dtype), pltpu.HBM(parts.shape, parts.dtype),
                   jax.ShapeDtypeStruct((8, LANES), F32)),
        in_specs=(_HBM, _HBM), out_specs=(_SEM, _SEM, _HBM, _HBM, pl.BlockSpec(memory_space=pltpu.VMEM)),
        input_output_aliases={0: 2, 1: 3},
        compiler_params=pltpu.CompilerParams(has_side_effects=_DATAFLOW),
    )(pltpu.with_memory_space_constraint(parts, pltpu.HBM), land)


def exchange_wait(send_sems, recv_sems, parts_thru, land_thru, after, name):
    def body(p_ref, land_ref, send_sems, recv_sems, after_ref, p_dead, got_ref):
        x, y, cc = lax.axis_index("x"), lax.axis_index("y"), lax.axis_index("c")
        for k, (px, py) in enumerate(_chip_peers(x, y)):
            cp = pltpu.make_async_remote_copy(src_ref=p_ref.at[2 * px + py], dst_ref=land_ref.at[2 * px + py],
                                              send_sem=send_sems.at[k], recv_sem=recv_sems.at[k],
                                              device_id=(px, py, cc), device_id_type=MESH)
            cp.wait_send()
            cp.wait_recv()

    return pl.pallas_call(
        body, name=name,
        out_shape=(pltpu.HBM(parts_thru.shape, parts_thru.dtype), pltpu.HBM(land_thru.shape, land_thru.dtype)),
        in_specs=(_HBM, _HBM, _SEM, _SEM, _ANY), out_specs=(_HBM, _HBM), input_output_aliases={0: 0, 1: 1},
        compiler_params=pltpu.CompilerParams(has_side_effects=_DATAFLOW),
    )(parts_thru, land_thru, send_sems, recv_sems, after)


def sibling_swap(block, name):
    def body(x_ref, out_ref, send_sem, recv_sem):
        x, y, cc = lax.axis_index("x"), lax.axis_index("y"), lax.axis_index("c")
        cp = pltpu.make_async_remote_copy(src_ref=x_ref, dst_ref=out_ref, send_sem=send_sem, recv_sem=recv_sem,
                                          device_id=(x, y, 1 - cc), device_id_type=MESH)
        cp.start()
        cp.wait()

    return pl.pallas_call(
        body, name=name, in_specs=[_ANY], out_specs=_ANY,
        out_shape=jax.ShapeDtypeStruct(block.shape, block.dtype),
        scratch_shapes=[pltpu.SemaphoreType.DMA, pltpu.SemaphoreType.DMA],
    )(block)


PACK_COLS = 1024
SHARDED = (("l0_w_in", 1), ("l0_w_uq", 1), ("l0_w_ukv", 1), ("l0_w_out", 0), ("l0_w_up", 1), ("l0_w_down", 0),
           ("l1_w_in", 0), ("l1_w_glu", 1), ("l1_w_up", 1), ("l1_w_down", 0),
           ("l0_conv_w", 1), ("l0_ffn_conv_w", 1), ("l1_ffn_conv_w", 1))
REPLICATED = ("l0_mix_norm", "l0_conv_b", "l0_conv_ln_g", "l0_conv_ln_b", "l0_q_norm", "l0_kv_norm", "l0_ffn_norm",
              "l0_ffn_conv_b", "l1_mix_norm", "l1_log_dt", "l1_a_re", "l1_a_im", "l1_b_re", "l1_b_im", "l1_c_re",
              "l1_c_im", "l1_d", "l1_b_glu", "l1_ffn_norm", "l1_ffn_conv_b", "final_norm")


def _pack(arrs, dtype, mult):
    flat = jnp.concatenate([a.reshape(-1).astype(dtype) for a in arrs])
    n = flat.shape[0]
    total = -(-n // mult) * mult
    return jnp.pad(flat, (0, total - n))


def _unpack(flat, shapes):
    out, pos = [], 0
    for shp in shapes:
        n = int(np.prod(shp))
        out.append(flat[pos:pos + n].reshape(shp))
        pos += n
    return out


PACK_ROW_ALIGN = 16


def _pack_rows(arrs, dtype, row_mult):
    parts = []
    for a in arrs:
        n = int(np.prod(a.shape))
        rows = -(-n // PACK_COLS)
        if n % PACK_COLS == 0:
            r = a.astype(dtype).reshape(rows, PACK_COLS)
        else:
            r = jnp.pad(a.reshape(-1).astype(dtype), (0, rows * PACK_COLS - n)).reshape(rows, PACK_COLS)
        parts.append(jnp.pad(r, ((0, (-rows) % PACK_ROW_ALIGN), (0, 0))))
    p = jnp.concatenate(parts)
    return jnp.pad(p, ((0, (-p.shape[0]) % row_mult), (0, 0)))


def _unpack_rows(pack, shapes):
    out, r0 = [], 0
    for shp in shapes:
        n = int(np.prod(shp))
        rows = -(-n // PACK_COLS)
        piece = lax.optimization_barrier(pack[r0:r0 + rows])
        out.append(piece.reshape(shp) if n % PACK_COLS == 0 else piece.reshape(-1)[:n].reshape(shp))
        r0 += rows + (-rows) % PACK_ROW_ALIGN
    return out


def _shard(full, axis, j):
    n = full.shape[axis] // N_CHIPS
    return lax.slice_in_dim(full, j * n, (j + 1) * n, axis=axis)


def _block_diag(t):
    q, g, a, b = t.shape
    eye = jnp.eye(g, dtype=t.dtype)
    return jnp.einsum("qgab,gh->qgahb", t, eye).reshape(q, g * a, g * b)


def _block_diag_t(d, a, b):
    q = d.shape[0]
    d5 = d.reshape(q, 8, a, 8, b)
    eye = jnp.eye(8, dtype=d.dtype)
    return jnp.einsum("qgahb,gh->qgab", d5, eye)


def kernel(x, l0_mix_norm, l0_w_in, l0_conv_w, l0_conv_b, l0_conv_ln_g, l0_conv_ln_b, l0_q_norm, l0_kv_norm, l0_w_uq, l0_w_ukv, l0_w_out, l0_ffn_norm, l0_w_up, l0_ffn_conv_w, l0_ffn_conv_b, l0_w_down, l1_mix_norm, l1_w_in, l1_log_dt, l1_a_re, l1_a_im, l1_b_re, l1_b_im, l1_c_re, l1_c_im, l1_d, l1_w_glu, l1_b_glu, l1_ffn_norm, l1_w_up, l1_ffn_conv_w, l1_ffn_conv_b, l1_w_down, final_norm, loss_target, m_l0_mix_norm, m_l0_w_in, m_l0_conv_w, m_l0_conv_b, m_l0_conv_ln_g, m_l0_conv_ln_b, m_l0_q_norm, m_l0_kv_norm, m_l0_w_uq, m_l0_w_ukv, m_l0_w_out, m_l0_ffn_norm, m_l0_w_up, m_l0_ffn_conv_w, m_l0_ffn_conv_b, m_l0_w_down, m_l1_mix_norm, m_l1_w_in, m_l1_log_dt, m_l1_a_re, m_l1_a_im, m_l1_b_re, m_l1_b_im, m_l1_c_re, m_l1_c_im, m_l1_d, m_l1_w_glu, m_l1_b_glu, m_l1_ffn_norm, m_l1_w_up, m_l1_ffn_conv_w, m_l1_ffn_conv_b, m_l1_w_down, m_final_norm, v_l0_mix_norm, v_l0_w_in, v_l0_conv_w, v_l0_conv_b, v_l0_conv_ln_g, v_l0_conv_ln_b, v_l0_q_norm, v_l0_kv_norm, v_l0_w_uq, v_l0_w_ukv, v_l0_w_out, v_l0_ffn_norm, v_l0_w_up, v_l0_ffn_conv_w, v_l0_ffn_conv_b, v_l0_w_down, v_l1_mix_norm, v_l1_w_in, v_l1_log_dt, v_l1_a_re, v_l1_a_im, v_l1_b_re, v_l1_b_im, v_l1_c_re, v_l1_c_im, v_l1_d, v_l1_w_glu, v_l1_b_glu, v_l1_ffn_norm, v_l1_w_up, v_l1_ffn_conv_w, v_l1_ffn_conv_b, v_l1_w_down, v_final_norm):
    a = dict(locals())
    w = {n: a[n] for n in [s for s, _ in SHARDED] + list(REPLICATED)}
    mom = {n: a["m_" + n] for n in w}
    var = {n: a["v_" + n] for n in w}
    return _step(a["x"][0], a["loss_target"][0], w, mom, var)


FIRST_WEIGHTS = ("l0_w_in", "l0_w_uq", "l0_w_ukv", "l0_w_out")
LATER_WEIGHTS = ("l0_w_up", "l0_w_down", "l1_w_in", "l1_w_glu", "l1_w_up", "l1_w_down")


def _assemble(got, names, w):
    got = got.reshape(N_CHIPS, -1, PACK_COLS)
    shapes = [w[n].shape for n in names]
    per_chip = [_unpack_rows(got[j], shapes) for j in range(N_CHIPS)]
    axes = dict(SHARDED)
    return {n: jnp.concatenate([per_chip[j][k] for j in range(N_CHIPS)], axis=axes[n]) for k, n in enumerate(names)}


def _gather_weights(w):
    cc = lax.axis_index("c")
    small = [n for n, _ in SHARDED[10:]]
    full = {}
    for names, dtype, mult in ((FIRST_WEIGHTS, BF16, 2 * 256), (small, F32, 2 * PACK_ROW_ALIGN)):
        pack = _pack_rows([w[n] for n in names], dtype, mult)
        half = lax.dynamic_index_in_dim(pack.reshape(2, -1, PACK_COLS), cc, axis=0, keepdims=False)
        got = all_gather8(half, "gather_" + ("first_matrices" if dtype == BF16 else "conv_weights"))
        full.update(_assemble(got, names, w))
    pack = _pack_rows([w[n] for n in LATER_WEIGHTS], BF16, 2 * 256).reshape(2, -1, PACK_COLS)
    half = lax.dynamic_index_in_dim(pack, cc, axis=0, keepdims=False)
    send_sems, recv_sems, half_thru, land_thru, token = gather_start(half, got, "gather_later_start")
    return full, (send_sems, recv_sems, half_thru, land_thru, pack), token[0, 0]


def _finish_gather(pending, after, w):
    send_sems, recv_sems, half_thru, land_thru, pack = pending
    got = gather_wait(send_sems, recv_sems, half_thru, land_thru, after, "gather_later_wait")
    chip = 2 * lax.axis_index("x") + lax.axis_index("y")
    got = lax.dynamic_update_slice(got, pack, (2 * chip, 0, 0))
    return _assemble(got, LATER_WEIGHTS, w)


def _reduce_begin(grads, names, tag):
    cc = lax.axis_index("c")
    axes = dict(SHARDED)
    packs = [_pack_rows([_shard(grads[n], axes[n], j) for n in names], BF16, 2 * 256) for j in range(N_CHIPS)]
    g = jnp.stack(packs).reshape(N_CHIPS, 2, -1, PACK_COLS)
    keep = lax.dynamic_index_in_dim(g, cc, axis=1, keepdims=False)
    give = lax.dynamic_index_in_dim(g, 1 - cc, axis=1, keepdims=False)
    got = sibling_swap(give, f"grad_swap_halves_{tag}")
    parts = add_to_bf16(keep, got, f"grad_add_sibling_{tag}")
    send_sems, recv_sems, parts_thru, land_thru, token = exchange_start(parts, f"grad_exchange_start_{tag}")
    shapes = [_shard(grads[n], axes[n], 0).shape for n in names]
    return (send_sems, recv_sems, parts_thru, land_thru, list(names), shapes), token[0, 0]


def _reduce_end(state, after, tag):
    send_sems, recv_sems, parts_thru, land_thru, names, shapes = state
    cc = lax.axis_index("c")
    chip = 2 * lax.axis_index("x") + lax.axis_index("y")
    parts, landed = exchange_wait(send_sems, recv_sems, parts_thru, land_thru, after, f"grad_exchange_wait_{tag}")
    own = lax.dynamic_index_in_dim(parts, chip, axis=0, keepdims=True)
    landed = lax.dynamic_update_slice(landed, own, (chip, 0, 0))
    mine = sum_leading(landed, f"grad_sum_chips_{tag}")
    theirs = sibling_swap(mine, f"grad_swap_sums_{tag}")
    lo = jnp.where(cc == 0, mine, theirs)
    hi = jnp.where(cc == 0, theirs, mine)
    return dict(zip(names, _unpack_rows(jnp.concatenate([lo, hi]), shapes)))


def _reduce_replicated(grads, loss_row):
    names = list(REPLICATED)
    flat = _pack([grads[n] for n in names] + [loss_row], F32, 256 * LANES).reshape(-1, LANES)
    got = all_gather8(flat, "gather_small_grads")
    tot = sum_leading(got, "sum_small_grads").reshape(-1)
    parts = _unpack(tot, [grads[n].shape for n in names] + [loss_row.shape])
    return dict(zip(names, parts[:-1])), parts[-1][0]


def _row(v):
    return v.reshape(1, -1).astype(F32)


def _pad_rows(wt, rows):
    return jnp.pad(wt.astype(F32), ((0, rows - wt.shape[0]), (0, 0)))


def _ffn_fwd(xin, g, wa, wb, cw, cb, wd, tag):
    cwa, cwb = _pad_rows(cw[:, :D_FF], 8), _pad_rows(cw[:, D_FF:], 8)
    xout, xn, hpa, hpb, act = ffn_fwd(xin, _row(g), wa, wb, cwa, cwb, _row(cb[:D_FF]), _row(cb[D_FF:]), wd, tag)
    return xout, (xin, xn, hpa, hpb, act)


def _ffn_bwd(dxout, saved, g, wa, wb, cw, cb, wd, tag, zero=0.0):
    xin, xn, hpa, hpb, act = saved
    d_wd = matmul(act, dxout, ta=True, name=f"{tag}_d_wdown")
    cwa, cwb = _pad_rows(cw[:, :D_FF], 8), _pad_rows(cw[:, D_FF:], 8)
    dxin, dpa, dpb, dwa, dwb, dba, dbb, dg = ffn_bwd(dxout, xin, _row(g), hpa, hpb, wa, wb, cwa, cwb,
                                                     _row(cb[:D_FF]) + zero, _row(cb[D_FF:]), wd, tag + "_bwd")
    d_wu = jnp.concatenate([matmul(xn, dpa, ta=True, name=f"{tag}_d_wup_a"),
                            matmul(xn, dpb, ta=True, name=f"{tag}_d_wup_b")], axis=1)
    taps = lambda t: t.transpose(1, 0, 2).reshape(8, -1)
    d_cw = jnp.concatenate([taps(dwa)[:FFN_K], taps(dwb)[:FFN_K]], axis=1)
    d_cb = jnp.concatenate([taps(dba)[0], taps(dbb)[0]])
    return dxin, dg[0], d_wu, d_cw, d_cb, d_wd


def _step(x, target, w, mom, var):
    s = x.shape[0]
    full, pending, zero = _gather_weights(w)
    cos, sin = rope_tables(s)

    w_in0 = full["l0_w_in"]
    w_in0p = jnp.concatenate([w_in0, jnp.zeros((D_MODEL, H0_W - w_in0.shape[1]), BF16)], axis=1)
    wq = full["l0_w_uq"].reshape(Q_LORA, N_HEADS, QK_NOPE + QK_ROPE)
    zq = lambda n: jnp.zeros((Q_LORA, N_HEADS, n), BF16)
    w_uqp = jnp.concatenate([wq[..., :QK_NOPE], zq(LANES - QK_NOPE), wq[..., QK_NOPE:], zq(LANES - QK_ROPE)],
                            axis=-1).reshape(Q_LORA, N_HEADS * HEAD_PAD)
    w_ukv = full["l0_w_ukv"]
    w_out = full["l0_w_out"]
    w_out_u = w_out[:CONV_WIDTH]
    wo = w_out[CONV_WIDTH:].reshape(N_HEADS, V_DIM, D_MODEL)
    w_out_a = jnp.concatenate([jnp.zeros_like(wo), wo], axis=1).reshape(N_HEADS * LANES, D_MODEL)
    conv_w = _pad_rows(full["l0_conv_w"], CONV_HALO)

    xn0 = rms_fwd(x, _row(w["l0_mix_norm"]) + zero, "l0_mix_rms")
    h0 = matmul(xn0, w_in0p, name="l0_in_proj")
    qn_g, kvn_g = _row(w["l0_q_norm"]), _row(w["l0_kv_norm"])
    u0, cq, ckv, kr = mixpre_fwd(h0, qn_g, kvn_g, cos, sin)
    cb, lg, lb = _row(w["l0_conv_b"]), _row(w["l0_conv_ln_g"]), _row(w["l0_conv_ln_b"])
    u = convln_fwd(u0, conv_w, cb, lg, lb)
    q = q_up_rope(cq, w_uqp, cos, sin)
    kv = matmul(ckv, w_ukv, out_dtype=BF16, name="l0_kv_up")
    o, lse = attn_fwd(q, kv, kr)
    x1 = matmul(u, w_out_u, res=x, name="l0_out_conv")
    x1 = matmul(o, w_out_a, res=x1, name="l0_out_attn")
    full.update(_finish_gather(pending, x1, w))
    w_up0a, w_up0b = full["l0_w_up"][:, :D_FF], full["l0_w_up"][:, D_FF:]
    w_up1a, w_up1b = full["l1_w_up"][:, :D_FF], full["l1_w_up"][:, D_FF:]

    x2, ffn0 = _ffn_fwd(x1, w["l0_ffn_norm"], w_up0a, w_up0b, full["l0_ffn_conv_w"], w["l0_ffn_conv_b"],
                        full["l0_w_down"], "l0_ffn")

    g_, p_, c_ = SSM_GROUPS, SSM_STATE, SSM_GROUP
    s5_in = (w["l1_log_dt"].reshape(g_, 1), w["l1_a_re"], w["l1_a_im"],
             w["l1_b_re"].reshape(g_, p_ * c_), w["l1_b_im"].reshape(g_, p_ * c_))
    lam_r, lam_i, bb_r, bb_i = s5_params_fwd(*s5_in)
    lam_rf, lam_if = lam_r.reshape(1, NS), lam_i.reshape(1, NS)

    def b_blocks(bb):
        t = bb.reshape(NQ, 8, p_, c_).transpose(0, 1, 3, 2)
        return _block_diag(t).astype(BF16)

    def c_blocks(cm):
        t = cm.reshape(NQ, 8, c_, p_).transpose(0, 1, 3, 2)
        return _block_diag(t).astype(BF16)

    bre, bim = b_blocks(bb_r), b_blocks(bb_i)
    cre, cim = c_blocks(w["l1_c_re"]), c_blocks(w["l1_c_im"])
    dskip = _row(w["l1_d"])
    xn2 = rms_fwd(x2, _row(w["l1_mix_norm"]), "l1_mix_rms")
    u1 = matmul(xn2, full["l1_w_in"], name="l1_in_proj")
    xs_r, xs_i, y1, yg = s5_scan_fwd(u1, lam_rf, lam_if, bre, bim, cre, cim, dskip)
    x3, z = glu_proj_res(yg, full["l1_w_glu"], _row(w["l1_b_glu"]), x2)

    x4, ffn1 = _ffn_fwd(x3, w["l1_ffn_norm"], w_up1a, w_up1b, full["l1_ffn_conv_w"], w["l1_ffn_conv_b"],
                        full["l1_w_down"], "l1_ffn")
    loss_part, dx4, dgf = loss_head(x4, _row(w["final_norm"]), target)

    gr = {"final_norm": dgf[0]}

    dx3, gr["l1_ffn_norm"], gr["l1_w_up"], gr["l1_ffn_conv_w"], gr["l1_ffn_conv_b"], gr["l1_w_down"] = _ffn_bwd(
        dx4, ffn1, w["l1_ffn_norm"], w_up1a, w_up1b, full["l1_ffn_conv_w"], w["l1_ffn_conv_b"], full["l1_w_down"],
        "l1_ffn")

    dz, dbga, dbgb = glu_bwd(z, dx3)
    gr["l1_b_glu"] = jnp.concatenate([dbga[0], dbgb[0]])
    dyg = matmul(dz, full["l1_w_glu"], tb=True, name="l1_d_yg")
    gr["l1_w_glu"] = matmul(yg, dz, ta=True, name="l1_d_wglu")
    a_r, a_i, dy1 = s5_scan_bwd(dyg, y1, lam_rf, lam_if, cre, cim)
    du1, dlr, dli, dbr, dbi, dcr, dci, dd = s5_grads(u1, dy1, xs_r, xs_i, a_r, a_i, bre, bim, dskip)
    gr["l1_d"] = dd[0]

    def b_unblock(d):
        return _block_diag_t(d, c_, p_).transpose(0, 1, 3, 2).reshape(g_, p_ * c_)

    def c_unblock(d):
        return _block_diag_t(d, p_, c_).transpose(0, 1, 3, 2).reshape(g_, c_, p_)

    gr["l1_c_re"], gr["l1_c_im"] = c_unblock(dcr), c_unblock(dci)
    dld, dar, dai, dbre, dbim = s5_params_bwd(*s5_in, dlr[0].reshape(g_, p_), dli[0].reshape(g_, p_),
                                              b_unblock(dbr), b_unblock(dbi))
    gr["l1_log_dt"], gr["l1_a_re"], gr["l1_a_im"] = dld.reshape(g_), dar, dai
    gr["l1_b_re"], gr["l1_b_im"] = dbre.reshape(g_, p_, c_), dbim.reshape(g_, p_, c_)
    dxn2 = matmul(du1, full["l1_w_in"], tb=True, name="l1_d_xn")
    gr["l1_w_in"] = matmul(xn2, du1, ta=True, name="l1_d_win")
    dx2, dg = rms_bwd(x2, _row(w["l1_mix_norm"]), dxn2, dx3, "l1_mix_rms_bwd")
    gr["l1_mix_norm"] = dg[0]
    red_a, zero_a = _reduce_begin(gr, ("l1_w_up", "l1_w_down", "l1_w_glu", "l1_w_in"), "a")

    dx1, gr["l0_ffn_norm"], gr["l0_w_up"], gr["l0_ffn_conv_w"], gr["l0_ffn_conv_b"], gr["l0_w_down"] = _ffn_bwd(
        dx2, ffn0, w["l0_ffn_norm"], w_up0a, w_up0b, full["l0_ffn_conv_w"], w["l0_ffn_conv_b"], full["l0_w_down"],
        "l0_ffn", zero_a)
    red_b, zero_b = _reduce_begin(gr, ("l0_w_up", "l0_w_down"), "b")

    du = matmul(dx1, w_out_u + zero_b.astype(BF16), tb=True, out_dtype=BF16, name="l0_d_u")
    do = matmul(dx1, w_out_a, tb=True, out_dtype=BF16, name="l0_d_o")
    d_wout_u = matmul(u, dx1, ta=True, name="l0_d_wout_u")
    d_wout_a = matmul(o, dx1, ta=True, name="l0_d_wout_a")
    gr["l0_w_out"] = jnp.concatenate(
        [d_wout_u, d_wout_a.reshape(N_HEADS, LANES, D_MODEL)[:, LANES - V_DIM:].reshape(N_HEADS * V_DIM, D_MODEL)])
    dq, dkv, dkr = attn_bwd(q, kv, kr, o, do, lse)
    dqraw = qrope_bwd(dq, cos, sin)
    dcq = matmul(dqraw, w_uqp, tb=True, name="l0_d_cq")
    d_wuqp = matmul(cq, dqraw, ta=True, name="l0_d_wuq").reshape(Q_LORA, N_HEADS, HEAD_PAD)
    gr["l0_w_uq"] = jnp.concatenate([d_wuqp[..., :QK_NOPE], d_wuqp[..., LANES:LANES + QK_ROPE]],
                                    axis=-1).reshape(Q_LORA, -1)
    dckv = matmul(dkv, w_ukv, tb=True, name="l0_d_ckv")
    gr["l0_w_ukv"] = matmul(ckv, dkv, ta=True, name="l0_d_wukv")
    du1c, dlg, dlb, dcb = convln_bwd1(u0, conv_w, cb, lg, lb, du)
    gr["l0_conv_ln_g"], gr["l0_conv_ln_b"], gr["l0_conv_b"] = dlg[0], dlb[0], dcb[0]
    du0, dcw = convln_bwd2(u0, conv_w, du1c)
    gr["l0_conv_w"] = dcw[:CONV_K]
    dh0, dqn, dkvn = mixpre_bwd(h0, qn_g, kvn_g, cos, sin, du0, dcq, dckv, dkr)
    gr["l0_q_norm"], gr["l0_kv_norm"] = dqn[0], dkvn[0]
    dxn0 = matmul(dh0, w_in0p, tb=True, name="l0_d_xn")
    gr["l0_w_in"] = matmul(xn0, dh0, ta=True, name="l0_d_win")[:, :w_in0.shape[1]]
    grad_x, dg = rms_bwd(x, _row(w["l0_mix_norm"]), dxn0, dx1, "l0_mix_rms_bwd")
    gr["l0_mix_norm"] = dg[0]

    rest = [n for n, _ in SHARDED if n not in red_a[-2] + red_b[-2]]
    red_c, _ = _reduce_begin(gr, rest, "c")
    g_sh = {**_reduce_end(red_a, grad_x, "a"), **_reduce_end(red_b, grad_x, "b"), **_reduce_end(red_c, grad_x, "c")}
    g_rep, loss = _reduce_replicated(gr, loss_part[0])
    grad, delta, new_m, new_v = {}, {}, {}, {}
    for n, _ in SHARDED:
        shp = w[n].shape
        two_d = (lambda t: t.reshape(shp[0], -1))
        grad[n] = g_sh[n]
        delta[n], new_m[n], new_v[n] = adamw(two_d(w[n]), two_d(g_sh[n]), two_d(mom[n]), two_d(var[n]), f"adamw_{n}")
    names = list(REPLICATED)
    pk = lambda d: _pack([d[n] for n in names], F32, 256 * LANES).reshape(-1, LANES)
    dl, nm, nv = adamw(pk(w), pk(g_rep), pk(mom), pk(var), "adamw_small")
    shapes = [w[n].shape for n in names]
    for n, d_, m_, v_ in zip(names, _unpack(dl.reshape(-1), shapes), _unpack(nm.reshape(-1), shapes),
                             _unpack(nv.reshape(-1), shapes)):
        grad[n], delta[n], new_m[n], new_v[n] = g_rep[n], d_, m_, v_

    order = ["l0_mix_norm", "l0_w_in", "l0_conv_w", "l0_conv_b", "l0_conv_ln_g", "l0_conv_ln_b", "l0_q_norm",
             "l0_kv_norm", "l0_w_uq", "l0_w_ukv", "l0_w_out", "l0_ffn_norm", "l0_w_up", "l0_ffn_conv_w",
             "l0_ffn_conv_b", "l0_w_down", "l1_mix_norm", "l1_w_in", "l1_log_dt", "l1_a_re", "l1_a_im", "l1_b_re",
             "l1_b_im", "l1_c_re", "l1_c_im", "l1_d", "l1_w_glu", "l1_b_glu", "l1_ffn_norm", "l1_w_up",
             "l1_ffn_conv_w", "l1_ffn_conv_b", "l1_w_down", "final_norm"]
    return (loss, grad_x[None], *[grad[n] for n in order], *[delta[n] for n in order],
            *[new_m[n] for n in order], *[new_v[n] for n in order])
```

```python
import functools
import math

import jax
import jax.numpy as jnp
import numpy as np
from jax import lax
from jax.experimental import pallas as pl
from jax.experimental.pallas import tpu as pltpu

F32 = jnp.float32
BF16 = jnp.bfloat16
MESH = pl.DeviceIdType.MESH

D_MODEL = 1024
EPS = 1e-6
LN_EPS = 1e-5
CONV_WIDTH = 512
CONV_K = 31
N_HEADS = 8
QK_NOPE = 64
QK_ROPE = 32
V_DIM = 64
Q_LORA = 256
KV_LORA = 128
ROPE_BASE = 10000.0
ATT_SCALE = (QK_NOPE + QK_ROPE) ** -0.5
SSM_WIDTH = 512
SSM_GROUP = 16
SSM_GROUPS = 32
SSM_STATE = 64
D_FF = 2816
FFN_K = 3
ADAM_LR = 0.001
ADAM_B1 = 0.9
ADAM_B2 = 0.999
ADAM_EPS = 1e-08
ADAM_WD = 0.01
ADAM_STEP = 10

N_CHIPS = 4
LANES = 128
HEAD_PAD = 256
CONV_HALO = 32
FFN_HALO = 16
VMEM_LIMIT = 56 * 1024 * 1024

ROW_TILE = 512
FFN_ROW_TILE = 1024
FFN_COL_TILE = 256
FFN_ROW_CHUNK = 64
CONV_ROW_CHUNK = 32
FFN_BWD_PARTS = 4
ATT_TILE = 1024
SCAN_TILE = 256
SCAN_UNROLL = 4


def _cparams(*sem):
    return pltpu.CompilerParams(dimension_semantics=tuple(sem), vmem_limit_bytes=VMEM_LIMIT)


def _pick(n, cands):
    for c in cands:
        if n % c == 0:
            return c
    return n


def matmul(a, b, *, ta=False, tb=False, res=None, bias=None, out_dtype=None, name):
    if out_dtype is None:
        out_dtype = BF16 if ta else F32
    if ta:
        kdim, m = a.shape
    else:
        m, kdim = a.shape
    if tb:
        n, k2 = b.shape
    else:
        k2, n = b.shape
    assert kdim == k2, (a.shape, b.shape, ta, tb)
    tn = _pick(n, (1408, 1024, 768, 512, 384, 256, 128))
    if ta:
        tm = _pick(m, (1408, 1024, 512, 256, 128))
        tk = _pick(kdim, (2048, 1024, 512, 256, 128))
    else:
        tm = _pick(m, (1024, 512, 256, 128))
        tk = kdim
        if kdim > 1024:
            tn = _pick(n, (512, 256, 128))
        if tm * tn > 1024 * 1024 and out_dtype == F32:
            tm = _pick(m, (512, 256, 128))
    nk = kdim // tk
    has_res, has_bias = res is not None, bias is not None
    dims = (((0,) if ta else (1,), (1,) if tb else (0,)), ((), ()))

    def body(*refs):
        a_ref, b_ref = refs[0], refs[1]
        pos = 2
        res_ref = bias_ref = None
        if has_res:
            res_ref = refs[pos]
            pos += 1
        if has_bias:
            bias_ref = refs[pos]
            pos += 1
        o_ref = refs[pos]

        def finish(r):
            if has_bias:
                r = r + bias_ref[...]
            if has_res:
                r = r + res_ref[...].astype(F32)
            o_ref[...] = r.astype(o_ref.dtype)

        prod = lax.dot_general(a_ref[...].astype(BF16), b_ref[...].astype(BF16), dims, preferred_element_type=F32)
        if nk == 1:
            finish(prod)
            return
        acc_ref = refs[pos + 1]
        k = pl.program_id(2)

        @pl.when(k == 0)
        def _():
            acc_ref[...] = prod

        @pl.when(k > 0)
        def _():
            acc_ref[...] += prod

        @pl.when(k == nk - 1)
        def _():
            finish(acc_ref[...])

    a_spec = pl.BlockSpec((tk, tm), lambda i, j, k: (k, i)) if ta else pl.BlockSpec((tm, tk), lambda i, j, k: (i, k))
    b_spec = pl.BlockSpec((tn, tk), lambda i, j, k: (j, k)) if tb else pl.BlockSpec((tk, tn), lambda i, j, k: (k, j))
    in_specs = [a_spec, b_spec]
    args = [a, b]
    if has_res:
        in_specs.append(pl.BlockSpec((tm, tn), lambda i, j, k: (i, j)))
        args.append(res)
    if has_bias:
        in_specs.append(pl.BlockSpec((1, tn), lambda i, j, k: (0, j)))
        args.append(bias)
    return pl.pallas_call(
        body, name=name, grid=(m // tm, n // tn, nk),
        in_specs=in_specs, out_specs=pl.BlockSpec((tm, tn), lambda i, j, k: (i, j)),
        out_shape=jax.ShapeDtypeStruct((m, n), out_dtype),
        scratch_shapes=[pltpu.VMEM((tm, tn), F32)] if nk > 1 else [],
        compiler_params=_cparams("parallel", "parallel", "arbitrary"),
    )(*args)


def rowcall(body, *, rows, ts, ins, outs, name, scratch=()):
    nt = rows // ts
    in_specs, args = [], []
    for arr, kind in ins:
        if kind == "row":
            in_specs.append(pl.BlockSpec((ts, arr.shape[1]), lambda i: (i, 0)))
        elif kind == "rev":
            in_specs.append(pl.BlockSpec((ts, arr.shape[1]), lambda i: (nt - 1 - i, 0)))
        elif kind == "full":
            nd = arr.ndim
            in_specs.append(pl.BlockSpec(arr.shape, lambda i, nd=nd: (0,) * nd))
        elif kind.startswith("prev:"):
            h = int(kind[5:])
            r = ts // h
            in_specs.append(pl.BlockSpec((h, arr.shape[1]), lambda i, r=r: (jnp.maximum(i * r - 1, 0), 0)))
        elif kind.startswith("next:"):
            h = int(kind[5:])
            r = ts // h
            last = rows // h - 1
            in_specs.append(pl.BlockSpec((h, arr.shape[1]), lambda i, r=r, last=last: (jnp.minimum((i + 1) * r, last), 0)))
        elif kind.startswith("revprev:"):
            h = int(kind[8:])
            r = ts // h
            in_specs.append(pl.BlockSpec((h, arr.shape[1]), lambda i, r=r: (jnp.maximum((nt - 1 - i) * r - 1, 0), 0)))
        else:
            raise ValueError(kind)
        args.append(arr)
    out_specs, out_shapes = [], []
    for shape, dtype, kind in outs:
        if kind == "row":
            out_specs.append(pl.BlockSpec((ts, shape[1]), lambda i: (i, 0)))
        elif kind == "rev":
            out_specs.append(pl.BlockSpec((ts, shape[1]), lambda i: (nt - 1 - i, 0)))
        else:
            nd = len(shape)
            out_specs.append(pl.BlockSpec(tuple(shape), lambda i, nd=nd: (0,) * nd))
        out_shapes.append(jax.ShapeDtypeStruct(tuple(shape), dtype))
    return pl.pallas_call(
        functools.partial(body, nt), name=name, grid=(nt,),
        in_specs=in_specs, out_specs=tuple(out_specs), out_shape=tuple(out_shapes),
        scratch_shapes=list(scratch),
        compiler_params=_cparams("arbitrary"),
    )(*args)


def _rms(x, g):
    return x * lax.rsqrt(jnp.mean(x * x, axis=-1, keepdims=True) + EPS) * g


def _layer_norm(x, g, b):
    mu = jnp.mean(x, axis=-1, keepdims=True)
    xc = x - mu
    var = jnp.mean(xc * xc, axis=-1, keepdims=True)
    return xc * lax.rsqrt(var + LN_EPS) * g + b


def _sigmoid(x):
    return 1.0 / (1.0 + jnp.exp(-x))


def _silu(x):
    return x * _sigmoid(x)


def _gelu(x):
    return 0.5 * x * (1.0 + jnp.tanh(math.sqrt(2.0 / math.pi) * (x + 0.044715 * (x * x * x))))


def _acc(ref, i, val):
    s = jnp.sum(val, axis=0, keepdims=True)

    @pl.when(i == 0)
    def _():
        ref[...] = jnp.zeros_like(ref)

    ref[...] += jnp.broadcast_to(s, ref.shape)


def rms_fwd(x, g, name):
    s, c = x.shape

    def body(nt, x_ref, g_ref, o_ref):
        o_ref[...] = _rms(x_ref[...], g_ref[...]).astype(BF16)

    return rowcall(body, rows=s, ts=min(ROW_TILE, s), ins=[(x, "row"), (g, "full")],
                   outs=[((s, c), BF16, "row")], name=name)[0]


def rms_bwd(x, g, dxn, dres, name):
    s, c = x.shape

    def body(nt, x_ref, g_ref, d_ref, r_ref, dx_ref, dg_ref):
        i = pl.program_id(0)
        _, vjp = jax.vjp(_rms, x_ref[...], g_ref[...])
        dx, dg = vjp(d_ref[...].astype(F32))
        dx_ref[...] = dx + r_ref[...]
        _acc(dg_ref, i, dg)

    return rowcall(body, rows=s, ts=min(ROW_TILE, s),
                   ins=[(x, "row"), (g, "full"), (dxn, "row"), (dres, "row")],
                   outs=[((s, c), F32, "row"), ((8, c), F32, "acc")], name=name)


def _partner(t):
    lane = lax.broadcasted_iota(jnp.int32, t.shape, 1)
    half = QK_ROPE // 2
    return jnp.where(lane % QK_ROPE < half, pltpu.roll(t, LANES - half, 1), pltpu.roll(t, half, 1))


def _rope(t, cos, sin):
    return t * cos + _partner(t) * sin


def _rope_t(d, cos, sin):
    return d * cos + _partner(d * sin)


def rope_tables(s):
    half = QK_ROPE // 2
    inv = ROPE_BASE ** (-jnp.arange(half, dtype=F32) / half)
    ang = jnp.arange(s).astype(F32)[:, None] * inv[None, :]
    cos, sin = jnp.cos(ang), jnp.sin(ang)
    z = jnp.zeros((s, LANES - QK_ROPE), F32)
    return jnp.concatenate([cos, cos, z], axis=1), jnp.concatenate([-sin, sin, z], axis=1)


H0_A, H0_G, H0_Q, H0_KV, H0_KR, H0_W = 0, 512, 1024, 1280, 1408, 1536


def _mixpre_fn(a, g, q, kv, qn, kvn):
    return a * _sigmoid(g), _rms(q, qn), _rms(kv, kvn)


def _h0_parts(h_ref):
    return (h_ref[:, H0_A:H0_G], h_ref[:, H0_G:H0_Q], h_ref[:, H0_Q:H0_KV], h_ref[:, H0_KV:H0_KR])


def mixpre_fwd(h0, qn, kvn, cos, sin):
    s = h0.shape[0]

    def body(nt, h_ref, qn_ref, kvn_ref, cos_ref, sin_ref, u0_ref, cq_ref, ckv_ref, kr_ref):
        u0, cq, ckv = _mixpre_fn(*_h0_parts(h_ref), qn_ref[...], kvn_ref[...])
        u0_ref[...] = u0
        cq_ref[...] = cq.astype(BF16)
        ckv_ref[...] = ckv.astype(BF16)
        kr_ref[...] = _rope(h_ref[:, H0_KR:H0_W], cos_ref[...], sin_ref[...]).astype(BF16)

    return rowcall(body, rows=s, ts=min(ROW_TILE, s),
                   ins=[(h0, "row"), (qn, "full"), (kvn, "full"), (cos, "row"), (sin, "row")],
                   outs=[((s, CONV_WIDTH), F32, "row"), ((s, Q_LORA), BF16, "row"),
                         ((s, KV_LORA), BF16, "row"), ((s, LANES), BF16, "row")], name="mixpre_fwd")


def mixpre_bwd(h0, qn, kvn, cos, sin, du0, dcq, dckv, dkr):
    s = h0.shape[0]

    def body(nt, h_ref, qn_ref, kvn_ref, cos_ref, sin_ref, du0_ref, dcq_ref, dckv_ref, dkr_ref,
             dh_ref, dqn_ref, dkvn_ref):
        i = pl.program_id(0)
        _, vjp = jax.vjp(_mixpre_fn, *_h0_parts(h_ref), qn_ref[...], kvn_ref[...])
        da, dg, dq, dkv, dqn, dkvn = vjp((du0_ref[...], dcq_ref[...], dckv_ref[...]))
        dh_ref[:, H0_A:H0_G] = da.astype(BF16)
        dh_ref[:, H0_G:H0_Q] = dg.astype(BF16)
        dh_ref[:, H0_Q:H0_KV] = dq.astype(BF16)
        dh_ref[:, H0_KV:H0_KR] = dkv.astype(BF16)
        dkr = dkr_ref[:, :LANES]
        for h in range(1, N_HEADS):
            dkr = dkr + dkr_ref[:, h * LANES:(h + 1) * LANES]
        dh_ref[:, H0_KR:H0_W] = _rope_t(dkr, cos_ref[...], sin_ref[...]).astype(BF16)
        _acc(dqn_ref, i, dqn)
        _acc(dkvn_ref, i, dkvn)

    return rowcall(body, rows=s, ts=min(ROW_TILE, s),
                   ins=[(h0, "row"), (qn, "full"), (kvn, "full"), (cos, "row"), (sin, "row"),
                        (du0, "row"), (dcq, "row"), (dckv, "row"), (dkr, "row")],
                   outs=[((s, H0_W), BF16, "row"), ((8, Q_LORA), F32, "acc"), ((8, KV_LORA), F32, "acc")],
                   name="mixpre_bwd")


def _conv_taps(ext_ref, w_ref, ts, first, ntaps, flip=False):
    acc = None
    for k in range(ntaps):
        term = w_ref[pl.ds(ntaps - 1 - k if flip else k, 1), :] * ext_ref[pl.ds(first + k, ts), :]
        acc = term if acc is None else acc + term
    return acc


def _ln_silu(u1, g, b):
    return _silu(_layer_norm(u1, g, b))


SUBLANES = 8


def _fill_shifted(sh_ref, parts, rows):
    pos = 0
    for p in parts:
        sh_ref[0, pl.ds(pos, p.shape[0]), :] = p
        pos += p.shape[0]
    sh_ref[0, pl.ds(rows, SUBLANES), :] = jnp.zeros((SUBLANES, sh_ref.shape[2]), F32)
    for r in range(1, SUBLANES):
        sh_ref[r, pl.ds(0, rows), :] = sh_ref[0, pl.ds(r, rows), :]


def _window(sh_ref, off, n):
    r = off % SUBLANES
    return sh_ref[r, pl.ds(off - r, n), :]


def _taps_aligned(sh_ref, w_ref, n, first, ntaps, flip=False):
    acc = None
    for k in range(ntaps):
        term = w_ref[pl.ds(ntaps - 1 - k if flip else k, 1), :] * _window(sh_ref, first + k, n)
        acc = term if acc is None else acc + term
    return acc


def _conv_scratch(ts, c):
    return pltpu.VMEM((SUBLANES, ts + CONV_HALO + SUBLANES, c), F32)


def convln_fwd(u0, w, b, lg, lb):
    s, c = u0.shape
    ts = min(ROW_TILE, s)
    rc = min(CONV_ROW_CHUNK, ts)
    first = CONV_HALO - (CONV_K - 1)

    def body(nt, cur_ref, prev_ref, w_ref, b_ref, lg_ref, lb_ref, o_ref, sh_ref):
        i = pl.program_id(0)
        _fill_shifted(sh_ref, [jnp.where(i > 0, prev_ref[...], 0.0), cur_ref[...]], ts + CONV_HALO)
        for r0 in range(0, ts, rc):
            u1 = _taps_aligned(sh_ref, w_ref, rc, first + r0, CONV_K) + b_ref[...]
            o_ref[pl.ds(r0, rc), :] = _ln_silu(u1, lg_ref[...], lb_ref[...]).astype(BF16)

    return rowcall(body, rows=s, ts=ts,
                   ins=[(u0, "row"), (u0, f"prev:{CONV_HALO}"), (w, "full"), (b, "full"), (lg, "full"), (lb, "full")],
                   outs=[((s, c), BF16, "row")], name="convln_fwd", scratch=[_conv_scratch(ts, c)])[0]


def convln_bwd1(u0, w, b, lg, lb, du):
    s, c = u0.shape
    ts = min(ROW_TILE, s)
    rc = min(CONV_ROW_CHUNK, ts)
    first = CONV_HALO - (CONV_K - 1)

    def body(nt, cur_ref, prev_ref, w_ref, b_ref, lg_ref, lb_ref, du_ref, du1_ref, dlg_ref, dlb_ref, dcb_ref, sh_ref):
        i = pl.program_id(0)
        _fill_shifted(sh_ref, [jnp.where(i > 0, prev_ref[...], 0.0), cur_ref[...]], ts + CONV_HALO)
        sums = [jnp.zeros((1, c), F32)] * 3
        for r0 in range(0, ts, rc):
            u1 = _taps_aligned(sh_ref, w_ref, rc, first + r0, CONV_K) + b_ref[...]
            _, vjp = jax.vjp(_ln_silu, u1, lg_ref[...], lb_ref[...])
            du1, dlg, dlb = vjp(du_ref[pl.ds(r0, rc), :].astype(F32))
            du1_ref[pl.ds(r0, rc), :] = du1
            parts = (dlg, dlb, jnp.sum(du1, axis=0, keepdims=True))
            sums = [a + jnp.sum(p, axis=0, keepdims=True) for a, p in zip(sums, parts)]
        _acc(dlg_ref, i, sums[0])
        _acc(dlb_ref, i, sums[1])
        _acc(dcb_ref, i, sums[2])

    return rowcall(body, rows=s, ts=ts,
                   ins=[(u0, "row"), (u0, f"prev:{CONV_HALO}"), (w, "full"), (b, "full"), (lg, "full"), (lb, "full"),
                        (du, "row")],
                   outs=[((s, c), F32, "row"), ((8, c), F32, "acc"), ((8, c), F32, "acc"), ((8, c), F32, "acc")],
                   name="convln_bwd1", scratch=[_conv_scratch(ts, c)])


def convln_bwd2(u0, w, du1):
    s, c = u0.shape
    ts = min(ROW_TILE, s)
    rc = min(CONV_ROW_CHUNK, ts)
    first = CONV_HALO - (CONV_K - 1)

    def body(nt, cur_ref, prev_ref, d_ref, dnext_ref, w_ref, du0_ref, dw_ref, sh_ref, dsh_ref):
        i = pl.program_id(0)
        _fill_shifted(sh_ref, [jnp.where(i > 0, prev_ref[...], 0.0), cur_ref[...]], ts + CONV_HALO)
        _fill_shifted(dsh_ref, [d_ref[...], jnp.where(i < nt - 1, dnext_ref[...], 0.0)], ts + CONV_HALO)
        for r0 in range(0, ts, rc):
            du0_ref[pl.ds(r0, rc), :] = _taps_aligned(dsh_ref, w_ref, rc, r0, CONV_K, flip=True)

        @pl.when(i == 0)
        def _():
            dw_ref[...] = jnp.zeros_like(dw_ref)

        for k in range(CONV_K):
            part = jnp.zeros((SUBLANES, c), F32)
            for r0 in range(0, ts, rc):
                prod = d_ref[pl.ds(r0, rc), :] * _window(sh_ref, first + k + r0, rc)
                for a in range(0, rc, SUBLANES):
                    part = part + prod[a:a + SUBLANES]
            dw_ref[pl.ds(k, 1), :] += jnp.sum(part, axis=0, keepdims=True)

    return rowcall(body, rows=s, ts=ts,
                   ins=[(u0, "row"), (u0, f"prev:{CONV_HALO}"), (du1, "row"), (du1, f"next:{CONV_HALO}"), (w, "full")],
                   outs=[((s, c), F32, "row"), ((CONV_HALO, c), F32, "acc")], name="convln_bwd2",
                   scratch=[_conv_scratch(ts, c), _conv_scratch(ts, c)])


def q_up_rope(cq, w_uqp, cos, sin):
    s, kdim = cq.shape
    n = w_uqp.shape[1]
    ts = min(ROW_TILE, s)

    def body(nt, cq_ref, w_ref, cos_ref, sin_ref, o_ref):
        cos_v, sin_v = cos_ref[...] * ATT_SCALE, sin_ref[...] * ATT_SCALE
        for h in range(N_HEADS):
            cols = slice(h * HEAD_PAD, (h + 1) * HEAD_PAD)
            qh = jnp.dot(cq_ref[...], w_ref[:, cols], preferred_element_type=F32)
            o_ref[:, h * HEAD_PAD:h * HEAD_PAD + LANES] = (qh[:, :LANES] * ATT_SCALE).astype(BF16)
            o_ref[:, h * HEAD_PAD + LANES:(h + 1) * HEAD_PAD] = _rope(qh[:, LANES:], cos_v, sin_v).astype(BF16)

    return rowcall(body, rows=s, ts=ts, ins=[(cq, "row"), (w_uqp, "full"), (cos, "row"), (sin, "row")],
                   outs=[((s, n), BF16, "row")], name="l0_q_up_rope")[0]


def qrope_bwd(dq, cos, sin):
    s = dq.shape[0]

    def body(nt, d_ref, cos_ref, sin_ref, o_ref):
        cos_v, sin_v = cos_ref[...] * ATT_SCALE, sin_ref[...] * ATT_SCALE
        for h in range(N_HEADS):
            nope = d_ref[:, h * HEAD_PAD:h * HEAD_PAD + LANES] * ATT_SCALE
            o_ref[:, h * HEAD_PAD:h * HEAD_PAD + LANES] = nope.astype(BF16)
            r = d_ref[:, h * HEAD_PAD + LANES:(h + 1) * HEAD_PAD].astype(F32)
            o_ref[:, h * HEAD_PAD + LANES:(h + 1) * HEAD_PAD] = _rope_t(r, cos_v, sin_v).astype(BF16)

    return rowcall(body, rows=s, ts=min(ROW_TILE, s), ins=[(dq, "row"), (cos, "row"), (sin, "row")],
                   outs=[((s, N_HEADS * HEAD_PAD), BF16, "row")], name="qrope_bwd")[0]


_NT = (((1,), (1,)), ((), ()))
_TN = (((0,), (0,)), ((), ()))


def _scores(q, kvr, diagonal):
    s = lax.dot_general(q, kvr, _NT, preferred_element_type=F32)
    if not diagonal:
        return s
    row = lax.broadcasted_iota(jnp.int32, s.shape, 0)
    col = lax.broadcasted_iota(jnp.int32, s.shape, 1)
    return jnp.where(col <= row, s, -jnp.inf)


def _on_causal_pairs(pair, k_blk, fn):
    @pl.when(k_blk < 2 * pair)
    def _():
        fn(0, False)
        fn(1, False)

    @pl.when(k_blk == 2 * pair)
    def _():
        fn(0, True)
        fn(1, False)

    @pl.when(k_blk == 2 * pair + 1)
    def _():
        fn(1, True)


def attn_fwd(q, kv, kr):
    s = q.shape[0]
    t = min(ATT_TILE, s // 2)
    n = s // t
    np_ = n // 2

    def body(q_ref, kv_ref, kr_ref, o_ref, lse_ref, m_ref, l_ref, acc_ref):
        i, j = pl.program_id(1), pl.program_id(2)

        @pl.when(j == 0)
        def _():
            m_ref[...] = jnp.full_like(m_ref, -jnp.inf)
            l_ref[...] = jnp.zeros_like(l_ref)
            acc_ref[...] = jnp.zeros_like(acc_ref)

        def block(sub, diagonal):
            kvv = kv_ref[...]
            kvr = jnp.concatenate([kvv, kr_ref[...]], axis=1)
            sc = _scores(q_ref[pl.ds(sub * t, t), :], kvr, diagonal)
            m_prev = m_ref[sub]
            m_new = jnp.maximum(m_prev, jnp.max(sc, axis=-1, keepdims=True))
            alpha = jnp.exp(m_prev - m_new)
            p = jnp.exp(sc - m_new)
            l_ref[sub] = alpha * l_ref[sub] + jnp.sum(p, axis=-1, keepdims=True)
            acc_ref[sub] = alpha * acc_ref[sub] + jnp.dot(p.astype(BF16), kvv, preferred_element_type=F32)
            m_ref[sub] = m_new

        _on_causal_pairs(i, j, block)

        @pl.when(j == 2 * i + 1)
        def _():
            for sub in range(2):
                l = l_ref[sub]
                o_ref[pl.ds(sub * t, t), :] = (acc_ref[sub] / l).astype(BF16)
                lse_ref[pl.ds(sub * t, t), :] = jnp.broadcast_to(m_ref[sub] + jnp.log(l), (t, LANES))

    kj = lambda h, i, j: (jnp.minimum(j, 2 * i + 1), h)
    return pl.pallas_call(
        body, name="attn_fwd", grid=(N_HEADS, np_, n),
        in_specs=[pl.BlockSpec((2 * t, HEAD_PAD), lambda h, i, j: (i, h)),
                  pl.BlockSpec((t, LANES), kj),
                  pl.BlockSpec((t, LANES), lambda h, i, j: (jnp.minimum(j, 2 * i + 1), 0))],
        out_specs=(pl.BlockSpec((2 * t, LANES), lambda h, i, j: (i, h)),
                   pl.BlockSpec((2 * t, LANES), lambda h, i, j: (i, h))),
        out_shape=(jax.ShapeDtypeStruct((s, N_HEADS * LANES), BF16),
                   jax.ShapeDtypeStruct((s, N_HEADS * LANES), F32)),
        scratch_shapes=[pltpu.VMEM((2, t, 1), F32), pltpu.VMEM((2, t, 1), F32), pltpu.VMEM((2, t, LANES), F32)],
        compiler_params=_cparams("parallel", "parallel", "arbitrary"),
    )(q, kv, kr)


def attn_bwd(q, kv, kr, o, do, lse):
    s = q.shape[0]
    t = min(ATT_TILE, s // 2)
    n = s // t
    np_ = n // 2

    def body(q_ref, kv_ref, kr_ref, o_ref, do_ref, lse_ref, dq_ref, dkv_ref, dkr_ref):
        j, i = pl.program_id(1), pl.program_id(2)

        @pl.when((i == 0) & (j == 0))
        def _():
            dq_ref[...] = jnp.zeros_like(dq_ref)

        @pl.when(i == 0)
        def _():
            dkv_ref[...] = jnp.zeros_like(dkv_ref)
            dkr_ref[...] = jnp.zeros_like(dkr_ref)

        def block(sub, diagonal):
            sl = pl.ds(sub * t, t)
            qv, dov, kvv = q_ref[sl, :], do_ref[sl, :], kv_ref[...]
            kvr = jnp.concatenate([kvv, kr_ref[...]], axis=1)
            p = jnp.exp(_scores(qv, kvr, diagonal) - lse_ref[sl, :1])
            dp = lax.dot_general(dov, kvv, _NT, preferred_element_type=F32)
            delta = jnp.sum(dov.astype(F32) * o_ref[sl, :].astype(F32), axis=-1, keepdims=True)
            ds = (p * (dp - delta)).astype(BF16)
            dk = lax.dot_general(ds, qv, _TN, preferred_element_type=F32)
            dkv_ref[...] += lax.dot_general(p.astype(BF16), dov, _TN, preferred_element_type=F32) + dk[:, :LANES]
            dkr_ref[...] += dk[:, LANES:]
            rows = pl.ds(pl.multiple_of((2 * i + sub) * t, t), t)
            dq_ref[rows, :] += jnp.dot(ds, kvr, preferred_element_type=F32)

        _on_causal_pairs(i, j, block)

    qi = lambda h, j, i: (jnp.maximum(i, lax.div(j, 2)), h)
    kj = lambda h, j, i: (j, h)
    return pl.pallas_call(
        body, name="attn_bwd", grid=(N_HEADS, n, np_),
        in_specs=[pl.BlockSpec((2 * t, HEAD_PAD), qi), pl.BlockSpec((t, LANES), kj),
                  pl.BlockSpec((t, LANES), lambda h, j, i: (j, 0)),
                  pl.BlockSpec((2 * t, LANES), qi), pl.BlockSpec((2 * t, LANES), qi), pl.BlockSpec((2 * t, LANES), qi)],
        out_specs=(pl.BlockSpec((s, HEAD_PAD), lambda h, j, i: (0, h)),
                   pl.BlockSpec((t, LANES), kj), pl.BlockSpec((t, LANES), kj)),
        out_shape=(jax.ShapeDtypeStruct((s, N_HEADS * HEAD_PAD), F32),
                   jax.ShapeDtypeStruct((s, N_HEADS * LANES), F32), jax.ShapeDtypeStruct((s, N_HEADS * LANES), F32)),
        compiler_params=_cparams("parallel", "arbitrary", "arbitrary"),
    )(q, kv, kr, o, do, lse)


def ffn_fwd(x, g, wa, wb, cwa, cwb, ba, bb, wd, name):
    s, d = x.shape
    f = wa.shape[1]
    ts, tf = min(FFN_ROW_TILE, s), FFN_COL_TILE
    hal = FFN_HALO
    nj = f // tf
    first = hal - (FFN_K - 1)
    rc = min(FFN_ROW_CHUNK, ts)

    def body(x_ref, xp_ref, g_ref, wa_ref, wb_ref, cwa_ref, cwb_ref, ba_ref, bb_ref, wd_ref,
             xo_ref, xn_ref, hpa_ref, hpb_ref, act_ref, xe_ref, ea_ref, eb_ref):
        i, j = pl.program_id(0), pl.program_id(1)

        @pl.when(j == 0)
        def _():
            xn = _rms(x_ref[...], g_ref[...]).astype(BF16)
            xn_ref[...] = xn
            xe_ref[pl.ds(hal, ts), :] = xn
            xe_ref[pl.ds(0, hal), :] = jnp.where(i > 0, _rms(xp_ref[...], g_ref[...]), 0.0).astype(BF16)
            xo_ref[...] = x_ref[...]

        halves = ((0, ts // 2), (ts // 2, ts))
        for lo, hi in halves:
            e0, e1 = (0 if lo == 0 else hal + lo), hal + hi
            xe = xe_ref[pl.ds(e0, e1 - e0), :]
            ea_ref[pl.ds(e0, e1 - e0), :] = jnp.dot(xe, wa_ref[...], preferred_element_type=F32)
            eb_ref[pl.ds(e0, e1 - e0), :] = jnp.dot(xe, wb_ref[...], preferred_element_type=F32)
        for lo, hi in halves:
            hpa_ref[pl.ds(lo, hi - lo), :] = ea_ref[pl.ds(hal + lo, hi - lo), :].astype(BF16)
            hpb_ref[pl.ds(lo, hi - lo), :] = eb_ref[pl.ds(hal + lo, hi - lo), :].astype(BF16)
            for r0 in range(lo, hi, rc):
                ha = _conv_taps(ea_ref, cwa_ref, rc, first + r0, FFN_K) + ba_ref[...]
                hb = _conv_taps(eb_ref, cwb_ref, rc, first + r0, FFN_K) + bb_ref[...]
                act_ref[pl.ds(r0, rc), :] = (_silu(ha) * hb).astype(BF16)
            xo_ref[pl.ds(lo, hi - lo), :] += jnp.dot(act_ref[pl.ds(lo, hi - lo), :], wd_ref[...],
                                                     preferred_element_type=F32)

    r = ts // hal
    row = pl.BlockSpec((ts, d), lambda i, j: (i, 0))
    prev = pl.BlockSpec((hal, d), lambda i, j: (jnp.maximum(i * r - 1, 0), 0))
    gsp = pl.BlockSpec((1, d), lambda i, j: (0, 0))
    wup = pl.BlockSpec((d, tf), lambda i, j: (0, j))
    cwsp = pl.BlockSpec((8, tf), lambda i, j: (0, j))
    bsp = pl.BlockSpec((1, tf), lambda i, j: (0, j))
    wdn = pl.BlockSpec((tf, d), lambda i, j: (j, 0))
    hid = pl.BlockSpec((ts, tf), lambda i, j: (i, j))
    return pl.pallas_call(
        body, name=name, grid=(s // ts, nj),
        in_specs=[row, prev, gsp, wup, wup, cwsp, cwsp, bsp, bsp, wdn],
        out_specs=(row, row, hid, hid, hid),
        out_shape=(jax.ShapeDtypeStruct((s, d), F32), jax.ShapeDtypeStruct((s, d), BF16),
                   jax.ShapeDtypeStruct((s, f), BF16), jax.ShapeDtypeStruct((s, f), BF16),
                   jax.ShapeDtypeStruct((s, f), BF16)),
        scratch_shapes=[pltpu.VMEM((ts + hal, d), BF16), pltpu.VMEM((ts + hal, tf), F32),
                        pltpu.VMEM((ts + hal, tf), F32)],
        compiler_params=_cparams("parallel", "arbitrary"),
    )(x, x, g, wa, wb, cwa, cwb, ba, bb, wd)


def ffn_bwd(dy, x, g, hpa, hpb, wa, wb, cwa, cwb, ba, bb, wd, name):
    s, d = dy.shape
    f = hpa.shape[1]
    ts, tf = min(FFN_ROW_TILE, s), FFN_COL_TILE
    hal = FFN_HALO
    nt, nj = s // ts, f // tf
    te = ts + hal
    first = hal - (FFN_K - 1)
    rc = min(FFN_ROW_CHUNK, ts)

    def body(dy_ref, dyn_ref, x_ref, g_ref, a_ref, ap_ref, an_ref, b_ref, bp_ref, bn_ref, wa_ref, wb_ref,
             cwa_ref, cwb_ref, ba_ref, bb_ref, wd_ref,
             dx_ref, dpa_ref, dpb_ref, dwa_ref, dwb_ref, dba_ref, dbb_ref, dg_ref,
             dye_ref, ea_ref, eb_ref, dact_ref, da_ref, db_ref, dxn_ref):
        i, j = pl.program_id(0), pl.program_id(1)
        last = i == nt - 1

        @pl.when(j == 0)
        def _():
            dye_ref[pl.ds(0, ts), :] = dy_ref[...].astype(BF16)
            dye_ref[pl.ds(ts, hal), :] = jnp.where(last, 0.0, dyn_ref[...]).astype(BF16)
            dxn_ref[...] = jnp.zeros_like(dxn_ref)

        @pl.when((i == 0) & (j == 0))
        def _():
            for r in (dwa_ref, dwb_ref, dba_ref, dbb_ref, dg_ref):
                r[...] = jnp.zeros_like(r)

        halves = tuple((k * ts // FFN_BWD_PARTS, (k + 1) * ts // FFN_BWD_PARTS) for k in range(FFN_BWD_PARTS))
        for lo, hi in halves:
            n = hi - lo + (hal if hi == ts else 0)
            dact_ref[pl.ds(lo, n), :] = lax.dot_general(dye_ref[pl.ds(lo, n), :], wd_ref[...], _NT,
                                                        preferred_element_type=F32)
        for cur, prev, nxt, ext in ((a_ref, ap_ref, an_ref, ea_ref), (b_ref, bp_ref, bn_ref, eb_ref)):
            ext[pl.ds(0, hal), :] = jnp.where(i > 0, prev[...].astype(F32), 0.0)
            ext[pl.ds(hal, ts), :] = cur[...].astype(F32)
            ext[pl.ds(hal + ts, hal), :] = jnp.where(last, 0.0, nxt[...].astype(F32))
        zero = jnp.zeros((1, tf), F32)
        sums = {"ba": zero, "bb": zero, **{("a", k): zero for k in range(FFN_K)}, **{("b", k): zero for k in range(FFN_K)}}
        for r0 in list(range(0, ts, rc)) + [ts]:
            n = rc if r0 < ts else hal
            win_a = [ea_ref[pl.ds(first + r0 + k, n), :] for k in range(FFN_K)]
            win_b = [eb_ref[pl.ds(first + r0 + k, n), :] for k in range(FFN_K)]
            ha = sum(cwa_ref[pl.ds(k, 1), :] * win_a[k] for k in range(FFN_K)) + ba_ref[...]
            hb = sum(cwb_ref[pl.ds(k, 1), :] * win_b[k] for k in range(FFN_K)) + bb_ref[...]
            sig = _sigmoid(ha)
            gs = dact_ref[pl.ds(r0, n), :] * sig
            dha = gs * hb * (1.0 + ha * (1.0 - sig))
            dhb = gs * ha
            da_ref[pl.ds(r0, n), :] = dha
            db_ref[pl.ds(r0, n), :] = dhb
            if r0 < ts:
                sums["ba"] = sums["ba"] + jnp.sum(dha, axis=0, keepdims=True)
                sums["bb"] = sums["bb"] + jnp.sum(dhb, axis=0, keepdims=True)
                for k in range(FFN_K):
                    sums["a", k] = sums["a", k] + jnp.sum(dha * win_a[k], axis=0, keepdims=True)
                    sums["b", k] = sums["b", k] + jnp.sum(dhb * win_b[k], axis=0, keepdims=True)
        for lo, hi in halves:
            for r0 in range(lo, hi, rc):
                dpa_ref[pl.ds(r0, rc), :] = _conv_taps(da_ref, cwa_ref, rc, r0, FFN_K, flip=True).astype(BF16)
                dpb_ref[pl.ds(r0, rc), :] = _conv_taps(db_ref, cwb_ref, rc, r0, FFN_K, flip=True).astype(BF16)
            rows = pl.ds(lo, hi - lo)
            dxn_ref[rows, :] += (lax.dot_general(dpa_ref[rows, :], wa_ref[...], _NT, preferred_element_type=F32)
                                 + lax.dot_general(dpb_ref[rows, :], wb_ref[...], _NT, preferred_element_type=F32))
        dba_ref[j] += jnp.broadcast_to(sums["ba"], (8, tf))
        dbb_ref[j] += jnp.broadcast_to(sums["bb"], (8, tf))
        row = lax.broadcasted_iota(jnp.int32, (8, tf), 0)
        dwa_ref[j] += sum(jnp.where(row == k, sums["a", k], 0.0) for k in range(FFN_K))
        dwb_ref[j] += sum(jnp.where(row == k, sums["b", k], 0.0) for k in range(FFN_K))

        @pl.when(j == nj - 1)
        def _():
            _, vjp = jax.vjp(_rms, x_ref[...], g_ref[...])
            dx, dg = vjp(dxn_ref[...])
            dx_ref[...] = dx + dy_ref[...]
            dg_ref[...] += jnp.broadcast_to(jnp.sum(dg, axis=0, keepdims=True), dg_ref.shape)

    r = ts // hal
    lastblk = s // hal - 1
    row = pl.BlockSpec((ts, d), lambda i, j: (i, 0))
    gsp = pl.BlockSpec((1, d), lambda i, j: (0, 0))
    dgsp = pl.BlockSpec((8, d), lambda i, j: (0, 0))
    rown = pl.BlockSpec((hal, d), lambda i, j: (jnp.minimum((i + 1) * r, lastblk), 0))
    cur = pl.BlockSpec((ts, tf), lambda i, j: (i, j))
    prev = pl.BlockSpec((hal, tf), lambda i, j: (jnp.maximum(i * r - 1, 0), j))
    nxt = pl.BlockSpec((hal, tf), lambda i, j: (jnp.minimum((i + 1) * r, lastblk), j))
    wup = pl.BlockSpec((d, tf), lambda i, j: (0, j))
    cwsp = pl.BlockSpec((8, tf), lambda i, j: (0, j))
    bsp = pl.BlockSpec((1, tf), lambda i, j: (0, j))
    wdn = pl.BlockSpec((tf, d), lambda i, j: (j, 0))
    accsp = pl.BlockSpec((nj, 8, tf), lambda i, j: (0, 0, 0))
    accshape = jax.ShapeDtypeStruct((nj, 8, tf), F32)
    return pl.pallas_call(
        body, name=name, grid=(nt, nj),
        in_specs=[row, rown, row, gsp, cur, prev, nxt, cur, prev, nxt, wup, wup, cwsp, cwsp, bsp, bsp, wdn],
        out_specs=(row, cur, cur, accsp, accsp, accsp, accsp, dgsp),
        out_shape=(jax.ShapeDtypeStruct((s, d), F32), jax.ShapeDtypeStruct((s, f), BF16),
                   jax.ShapeDtypeStruct((s, f), BF16), accshape, accshape, accshape, accshape,
                   jax.ShapeDtypeStruct((8, d), F32)),
        scratch_shapes=[pltpu.VMEM((te, d), BF16), pltpu.VMEM((ts + 2 * hal, tf), F32),
                        pltpu.VMEM((ts + 2 * hal, tf), F32), pltpu.VMEM((te, tf), F32),
                        pltpu.VMEM((te, tf), F32), pltpu.VMEM((te, tf), F32), pltpu.VMEM((ts, d), F32)],
        compiler_params=_cparams("arbitrary", "arbitrary"),
    )(dy, dy, x, g, hpa, hpa, hpa, hpb, hpb, hpb, wa, wb, cwa, cwb, ba, bb, wd)


NQ = 4
SQ = SSM_STATE * 8
NS = SSM_GROUPS * SSM_STATE


def _s5_disc(log_dt, a_re, a_im, b_re, b_im, expand):
    dt = jnp.exp(log_dt)
    mag = jnp.exp(a_re * dt)
    lb_re, lb_im = mag * jnp.cos(a_im * dt), mag * jnp.sin(a_im * dt)
    den = a_re * a_re + a_im * a_im
    nr, ni = lb_re - 1.0, lb_im
    f_re = (nr * a_re + ni * a_im) / den
    f_im = (ni * a_re - nr * a_im) / den
    fe_re = jnp.dot(f_re, expand, precision=lax.Precision.HIGHEST, preferred_element_type=F32)
    fe_im = jnp.dot(f_im, expand, precision=lax.Precision.HIGHEST, preferred_element_type=F32)
    return lb_re, lb_im, fe_re * b_re - fe_im * b_im, fe_re * b_im + fe_im * b_re


def _expand_matrix():
    e = np.zeros((SSM_STATE, SSM_STATE * SSM_GROUP), np.float32)
    for p in range(SSM_STATE):
        e[p, p * SSM_GROUP:(p + 1) * SSM_GROUP] = 1.0
    return jnp.asarray(e)


def s5_params_fwd(log_dt, a_re, a_im, b_re, b_im):
    expand = _expand_matrix()

    def body(ld_ref, ar_ref, ai_ref, br_ref, bi_ref, e_ref, lr_ref, li_ref, bbr_ref, bbi_ref):
        lr, li, bbr, bbi = _s5_disc(ld_ref[...], ar_ref[...], ai_ref[...], br_ref[...], bi_ref[...], e_ref[...])
        lr_ref[...] = lr
        li_ref[...] = li
        bbr_ref[...] = bbr
        bbi_ref[...] = bbi

    g, p, pc = SSM_GROUPS, SSM_STATE, SSM_STATE * SSM_GROUP
    return pl.pallas_call(
        body, name="s5_params_fwd",
        out_shape=(jax.ShapeDtypeStruct((g, p), F32), jax.ShapeDtypeStruct((g, p), F32),
                   jax.ShapeDtypeStruct((g, pc), F32), jax.ShapeDtypeStruct((g, pc), F32)),
    )(log_dt, a_re, a_im, b_re, b_im, expand)


def s5_params_bwd(log_dt, a_re, a_im, b_re, b_im, dlr, dli, dbbr, dbbi):
    expand = _expand_matrix()

    def body(ld_ref, ar_ref, ai_ref, br_ref, bi_ref, e_ref, dlr_ref, dli_ref, dbbr_ref, dbbi_ref,
             dld_ref, dar_ref, dai_ref, dbr_ref, dbi_ref):
        e = e_ref[...]
        f = lambda ld, ar, ai, br, bi: _s5_disc(ld, ar, ai, br, bi, e)
        _, vjp = jax.vjp(f, ld_ref[...], ar_ref[...], ai_ref[...], br_ref[...], bi_ref[...])
        dld, dar, dai, dbr, dbi = vjp((dlr_ref[...], dli_ref[...], dbbr_ref[...], dbbi_ref[...]))
        dld_ref[...] = dld
        dar_ref[...] = dar
        dai_ref[...] = dai
        dbr_ref[...] = dbr
        dbi_ref[...] = dbi

    g, p, pc = SSM_GROUPS, SSM_STATE, SSM_STATE * SSM_GROUP
    return pl.pallas_call(
        body, name="s5_params_bwd",
        out_shape=(jax.ShapeDtypeStruct((g, 1), F32), jax.ShapeDtypeStruct((g, p), F32),
                   jax.ShapeDtypeStruct((g, p), F32), jax.ShapeDtypeStruct((g, pc), F32),
                   jax.ShapeDtypeStruct((g, pc), F32)),
    )(log_dt, a_re, a_im, b_re, b_im, expand, dlr, dli, dbbr, dbbi)


def _cmul(ar, ai, br, bi):
    return ar * br - ai * bi, ar * bi + ai * br


def _power_rows(lr, li, conj_rev):
    row = lax.broadcasted_iota(jnp.int32, (8, NS), 0)
    tr = jnp.zeros((8, NS), F32)
    ti = jnp.zeros((8, NS), F32)
    pr, pi = lr, li
    for r in range(8):
        dst = 7 - r if conj_rev else r
        tr = jnp.where(row == dst, pr, tr)
        ti = jnp.where(row == dst, -pi if conj_rev else pi, ti)
        if r < 7:
            pr, pi = _cmul(pr, pi, lr, li)
    return tr, ti


def _scan8(xr, xi, tr_ref, ti_ref, cr, ci, reverse):
    row = lax.broadcasted_iota(jnp.int32, xr.shape, 0)
    for d in (1, 2, 4):
        if reverse:
            sr, si = pltpu.roll(xr, 8 - d, 0), pltpu.roll(xi, 8 - d, 0)
            keep = row < 8 - d
            pw = 8 - d
        else:
            sr, si = pltpu.roll(xr, d, 0), pltpu.roll(xi, d, 0)
            keep = row >= d
            pw = d - 1
        mr, mi = _cmul(tr_ref[pl.ds(pw, 1), :], ti_ref[pl.ds(pw, 1), :], sr, si)
        xr = xr + jnp.where(keep, mr, 0.0)
        xi = xi + jnp.where(keep, mi, 0.0)
    mr, mi = _cmul(tr_ref[...], ti_ref[...], cr, ci)
    return xr + mr, xi + mi


def _row_of(x, r):
    row = lax.broadcasted_iota(jnp.int32, x.shape, 0)
    return jnp.sum(jnp.where(row == r, x, 0.0), axis=0, keepdims=True)


def s5_scan_fwd(u, lam_r, lam_i, bre, bim, cre, cim, dskip):
    s = u.shape[0]
    tt = min(SCAN_TILE, s)
    nb = tt // 8

    def body(nt, u_ref, lr_ref, li_ref, bre_ref, bim_ref, cre_ref, cim_ref, d_ref,
             xr_ref, xi_ref, y_ref, yg_ref, tr_ref, ti_ref, cr_ref, ci_ref):
        i = pl.program_id(0)

        @pl.when(i == 0)
        def _():
            tr, ti = _power_rows(lr_ref[...], li_ref[...], False)
            tr_ref[...] = tr
            ti_ref[...] = ti
            cr_ref[...] = jnp.zeros_like(cr_ref)
            ci_ref[...] = jnp.zeros_like(ci_ref)

        uv = u_ref[...]
        ub = uv.astype(BF16)
        for q in range(NQ):
            uq = ub[:, q * LANES:(q + 1) * LANES]
            xr_ref[:, q * SQ:(q + 1) * SQ] = jnp.dot(uq, bre_ref[q], preferred_element_type=F32)
            xi_ref[:, q * SQ:(q + 1) * SQ] = jnp.dot(uq, bim_ref[q], preferred_element_type=F32)

        def step(b, carry):
            cr, ci = carry
            rows = pl.ds(pl.multiple_of(b * 8, 8), 8)
            xr, xi = _scan8(xr_ref[rows, :], xi_ref[rows, :], tr_ref, ti_ref, cr, ci, False)
            xr_ref[rows, :] = xr
            xi_ref[rows, :] = xi
            return _row_of(xr, 7), _row_of(xi, 7)

        cr, ci = lax.fori_loop(0, nb, step, (cr_ref[...], ci_ref[...]), unroll=min(SCAN_UNROLL, nb))
        cr_ref[...] = cr
        ci_ref[...] = ci
        y = d_ref[...] * uv
        for q in range(NQ):
            yq = (jnp.dot(xr_ref[:, q * SQ:(q + 1) * SQ].astype(BF16), cre_ref[q], preferred_element_type=F32)
                  - jnp.dot(xi_ref[:, q * SQ:(q + 1) * SQ].astype(BF16), cim_ref[q], preferred_element_type=F32))
            y_ref[:, q * LANES:(q + 1) * LANES] = yq + y[:, q * LANES:(q + 1) * LANES]
        yg_ref[...] = _gelu(y_ref[...]).astype(BF16)

    return rowcall(body, rows=s, ts=tt,
                   ins=[(u, "row"), (lam_r, "full"), (lam_i, "full"), (bre, "full"), (bim, "full"),
                        (cre, "full"), (cim, "full"), (dskip, "full")],
                   outs=[((s, NS), F32, "row"), ((s, NS), F32, "row"), ((s, SSM_WIDTH), F32, "row"),
                         ((s, SSM_WIDTH), BF16, "row")], name="s5_scan_fwd",
                   scratch=[pltpu.VMEM((8, NS), F32), pltpu.VMEM((8, NS), F32),
                            pltpu.VMEM((1, NS), F32), pltpu.VMEM((1, NS), F32)])


def s5_scan_bwd(dyg, y, lam_r, lam_i, cre, cim):
    s = y.shape[0]
    tt = min(SCAN_TILE, s)
    nb = tt // 8

    def body(nt, dyg_ref, y_ref, lr_ref, li_ref, cre_ref, cim_ref,
             ar_ref, ai_ref, dy_ref, tr_ref, ti_ref, cr_ref, ci_ref):
        i = pl.program_id(0)

        @pl.when(i == 0)
        def _():
            tr, ti = _power_rows(lr_ref[...], li_ref[...], True)
            tr_ref[...] = tr
            ti_ref[...] = ti
            cr_ref[...] = jnp.zeros_like(cr_ref)
            ci_ref[...] = jnp.zeros_like(ci_ref)

        _, vjp = jax.vjp(_gelu, y_ref[...])
        dy = vjp(dyg_ref[...])[0]
        dyb = dy.astype(BF16)
        dy_ref[...] = dyb
        for q in range(NQ):
            dq = dyb[:, q * LANES:(q + 1) * LANES]
            ar_ref[:, q * SQ:(q + 1) * SQ] = lax.dot_general(dq, cre_ref[q], _NT, preferred_element_type=F32)
            ai_ref[:, q * SQ:(q + 1) * SQ] = -lax.dot_general(dq, cim_ref[q], _NT, preferred_element_type=F32)

        def step(b, carry):
            cr, ci = carry
            rows = pl.ds(pl.multiple_of((nb - 1 - b) * 8, 8), 8)
            xr, xi = _scan8(ar_ref[rows, :], ai_ref[rows, :], tr_ref, ti_ref, cr, ci, True)
            ar_ref[rows, :] = xr
            ai_ref[rows, :] = xi
            return _row_of(xr, 0), _row_of(xi, 0)

        cr, ci = lax.fori_loop(0, nb, step, (cr_ref[...], ci_ref[...]), unroll=min(SCAN_UNROLL, nb))
        cr_ref[...] = cr
        ci_ref[...] = ci

    return rowcall(body, rows=s, ts=tt,
                   ins=[(dyg, "rev"), (y, "rev"), (lam_r, "full"), (lam_i, "full"), (cre, "full"), (cim, "full")],
                   outs=[((s, NS), F32, "rev"), ((s, NS), F32, "rev"), ((s, SSM_WIDTH), BF16, "rev")],
                   name="s5_scan_bwd",
                   scratch=[pltpu.VMEM((8, NS), F32), pltpu.VMEM((8, NS), F32),
                            pltpu.VMEM((1, NS), F32), pltpu.VMEM((1, NS), F32)])


def s5_grads(u, dy, xr, xi, ar, ai, bre, bim, dskip):
    s = u.shape[0]
    tt = min(SCAN_TILE, s)

    def body(nt, u_ref, dy_ref, xr_ref, xrp_ref, xi_ref, xip_ref, ar_ref, ai_ref, bre_ref, bim_ref, d_ref,
             du_ref, dlr_ref, dli_ref, dbr_ref, dbi_ref, dcr_ref, dci_ref, dd_ref, er_ref, ei_ref):
        i = pl.program_id(0)

        @pl.when(i == 0)
        def _():
            for r in (dbr_ref, dbi_ref, dcr_ref, dci_ref):
                r[...] = jnp.zeros_like(r)

        uv, dyb = u_ref[...], dy_ref[...]
        dyf = dyb.astype(F32)
        av_r, av_i, xv_r, xv_i = ar_ref[...], ai_ref[...], xr_ref[...], xi_ref[...]
        er_ref[pl.ds(0, 8), :] = jnp.where(i > 0, xrp_ref[...], 0.0)
        ei_ref[pl.ds(0, 8), :] = jnp.where(i > 0, xip_ref[...], 0.0)
        er_ref[pl.ds(8, tt), :] = xv_r
        ei_ref[pl.ds(8, tt), :] = xv_i
        sr, si = er_ref[pl.ds(7, tt), :], ei_ref[pl.ds(7, tt), :]
        _acc(dlr_ref, i, av_r * sr + av_i * si)
        _acc(dli_ref, i, av_i * sr - av_r * si)
        _acc(dd_ref, i, dyf * uv)
        ub = uv.astype(BF16)
        ab_r, ab_i = av_r.astype(BF16), av_i.astype(BF16)
        xb_r, xb_i = xv_r.astype(BF16), xv_i.astype(BF16)
        du = d_ref[...] * dyf
        for q in range(NQ):
            cs, ss = slice(q * LANES, (q + 1) * LANES), slice(q * SQ, (q + 1) * SQ)
            dbr_ref[q] += lax.dot_general(ub[:, cs], ab_r[:, ss], _TN, preferred_element_type=F32)
            dbi_ref[q] += lax.dot_general(ub[:, cs], ab_i[:, ss], _TN, preferred_element_type=F32)
            dcr_ref[q] += lax.dot_general(xb_r[:, ss], dyb[:, cs], _TN, preferred_element_type=F32)
            dci_ref[q] -= lax.dot_general(xb_i[:, ss], dyb[:, cs], _TN, preferred_element_type=F32)
            du_ref[:, cs] = (du[:, cs]
                             + lax.dot_general(ab_r[:, ss], bre_ref[q], _NT, preferred_element_type=F32)
                             + lax.dot_general(ab_i[:, ss], bim_ref[q], _NT, preferred_element_type=F32))

    return rowcall(body, rows=s, ts=tt,
                   ins=[(u, "row"), (dy, "row"), (xr, "row"), (xr, "prev:8"), (xi, "row"), (xi, "prev:8"),
                        (ar, "row"), (ai, "row"), (bre, "full"), (bim, "full"), (dskip, "full")],
                   outs=[((s, SSM_WIDTH), F32, "row"), ((8, NS), F32, "acc"), ((8, NS), F32, "acc"),
                         ((NQ, LANES, SQ), F32, "acc"), ((NQ, LANES, SQ), F32, "acc"),
                         ((NQ, SQ, LANES), F32, "acc"), ((NQ, SQ, LANES), F32, "acc"),
                         ((8, SSM_WIDTH), F32, "acc")], name="s5_grads",
                   scratch=[pltpu.VMEM((tt + 8, NS), F32), pltpu.VMEM((tt + 8, NS), F32)])


def _glu_fn(za, zb):
    return za * _sigmoid(zb)


def glu_proj_res(yg, w_glu, b_glu, xres):
    s = yg.shape[0]

    def body(nt, y_ref, w_ref, b_ref, x_ref, o_ref, z_ref):
        yv = y_ref[...]
        za = jnp.dot(yv, w_ref[:, :D_MODEL], preferred_element_type=F32) + b_ref[:, :D_MODEL]
        zb = jnp.dot(yv, w_ref[:, D_MODEL:], preferred_element_type=F32) + b_ref[:, D_MODEL:]
        z_ref[:, :D_MODEL] = za.astype(BF16)
        z_ref[:, D_MODEL:] = zb.astype(BF16)
        o_ref[...] = x_ref[...] + _glu_fn(za, zb)

    return rowcall(body, rows=s, ts=min(ROW_TILE, s),
                   ins=[(yg, "row"), (w_glu, "full"), (b_glu, "full"), (xres, "row")],
                   outs=[((s, D_MODEL), F32, "row"), ((s, 2 * D_MODEL), BF16, "row")], name="l1_glu_proj_res")


def glu_bwd(z, dout):
    s, c = z.shape

    def body(nt, z_ref, d_ref, dz_ref, dba_ref, dbb_ref):
        i = pl.program_id(0)
        _, vjp = jax.vjp(_glu_fn, z_ref[:, :D_MODEL].astype(F32), z_ref[:, D_MODEL:].astype(F32))
        dza, dzb = vjp(d_ref[...])
        dz_ref[:, :D_MODEL] = dza.astype(BF16)
        dz_ref[:, D_MODEL:] = dzb.astype(BF16)
        _acc(dba_ref, i, dza)
        _acc(dbb_ref, i, dzb)

    return rowcall(body, rows=s, ts=min(ROW_TILE, s), ins=[(z, "row"), (dout, "row")],
                   outs=[((s, c), BF16, "row"), ((8, D_MODEL), F32, "acc"), ((8, D_MODEL), F32, "acc")],
                   name="glu_bwd")


def loss_head(x, g, target):
    s, c = x.shape

    def body(nt, x_ref, g_ref, t_ref, loss_ref, dx_ref, dg_ref):
        i = pl.program_id(0)
        y, vjp = jax.vjp(_rms, x_ref[...], g_ref[...])
        err = y - t_ref[...]
        dx, dg = vjp(err * (1.0 / c))
        dx_ref[...] = dx
        _acc(dg_ref, i, dg)
        part = jnp.sum(jnp.sum(err * err, axis=-1, keepdims=True), axis=0, keepdims=True) * (0.5 / c)

        @pl.when(i == 0)
        def _():
            loss_ref[...] = jnp.zeros_like(loss_ref)

        loss_ref[...] += jnp.broadcast_to(part, loss_ref.shape)

    return rowcall(body, rows=s, ts=min(ROW_TILE, s), ins=[(x, "row"), (g, "full"), (target, "row")],
                   outs=[((8, LANES), F32, "acc"), ((s, c), F32, "row"), ((8, c), F32, "acc")], name="loss_head")


def _tile_rows(r, cands=(512, 256, 128, 64, 32, 16, 8)):
    return _pick(r, cands)


def add_to_bf16(a, b, name):
    n, r, c = a.shape
    tr = _tile_rows(r)

    def body(a_ref, b_ref, o_ref):
        o_ref[...] = (a_ref[...].astype(F32) + b_ref[...].astype(F32)).astype(BF16)

    spec = pl.BlockSpec((1, tr, c), lambda j, i: (j, i, 0))
    return pl.pallas_call(body, name=name, grid=(n, r // tr), in_specs=[spec, spec], out_specs=spec,
                          out_shape=jax.ShapeDtypeStruct((n, r, c), BF16),
                          compiler_params=_cparams("parallel", "parallel"))(a, b)


def sum_leading(a, name):
    n, r, c = a.shape
    tr = _tile_rows(r)

    def body(a_ref, o_ref):
        acc = a_ref[0].astype(F32)
        for k in range(1, n):
            acc = acc + a_ref[k].astype(F32)
        o_ref[...] = acc

    return pl.pallas_call(body, name=name, grid=(r // tr,),
                          in_specs=[pl.BlockSpec((n, tr, c), lambda i: (0, i, 0))],
                          out_specs=pl.BlockSpec((tr, c), lambda i: (i, 0)),
                          out_shape=jax.ShapeDtypeStruct((r, c), F32),
                          compiler_params=_cparams("parallel"))(a)


def adamw(w, g, m, v, name):
    r, c = w.shape
    tr = _tile_rows(r, (256, 128, 64, 32, 16, 8))
    c1 = 1.0 - ADAM_B1 ** ADAM_STEP
    c2 = 1.0 - ADAM_B2 ** ADAM_STEP

    def body(w_ref, g_ref, m_ref, v_ref, d_ref, nm_ref, nv_ref):
        gv = g_ref[...]
        mn = ADAM_B1 * m_ref[...] + (1.0 - ADAM_B1) * gv
        vn = ADAM_B2 * v_ref[...] + (1.0 - ADAM_B2) * (gv * gv)
        d_ref[...] = -ADAM_LR * ((mn / c1) / (jnp.sqrt(vn / c2) + ADAM_EPS) + ADAM_WD * w_ref[...])
        nm_ref[...] = mn
        nv_ref[...] = vn

    spec = pl.BlockSpec((tr, c), lambda i: (i, 0))
    shp = jax.ShapeDtypeStruct((r, c), F32)
    return pl.pallas_call(body, name=name, grid=(r // tr,), in_specs=[spec] * 4, out_specs=(spec,) * 3,
                          out_shape=(shp,) * 3, compiler_params=_cparams("parallel"))(w, g, m, v)


_ANY = pl.BlockSpec(memory_space=pl.ANY)


def all_gather8(block, name):
    r, c = block.shape

    def body(x_ref, out_ref, send_sems, recv_sems, local_sem):
        x, y, cc = lax.axis_index("x"), lax.axis_index("y"), lax.axis_index("c")
        me, sibling = (x, y, cc), (x, y, 1 - cc)
        chips = [(1 - x, y), (x, 1 - y), (1 - x, 1 - y)]

        def slot(px, py, pc):
            return out_ref.at[4 * px + 2 * py + pc]

        def copy(k, blk, to, src=None):
            return pltpu.make_async_remote_copy(
                src_ref=slot(*blk) if src is None else src, dst_ref=slot(*blk),
                send_sem=send_sems.at[k], recv_sem=recv_sems.at[k], device_id=to, device_id_type=MESH)

        mine = pltpu.make_async_copy(x_ref, slot(*me), local_sem)
        mine.start()
        first = [copy(0, me, sibling, src=x_ref)]
        first += [copy(1 + j, me, (*chip, cc), src=x_ref) for j, chip in enumerate(chips)]
        for cp in first:
            cp.start()
        passed = [copy(4 + j, (*chip, cc), sibling) for j, chip in enumerate(chips)]
        for j, chip in enumerate(chips):
            copy(1 + j, (*chip, cc), me).wait_recv()
            passed[j].start()
        copy(0, sibling, me).wait_recv()
        for j, chip in enumerate(chips):
            copy(4 + j, (*chip, 1 - cc), me).wait_recv()
        for cp in first + passed:
            cp.wait_send()
        mine.wait()

    return pl.pallas_call(
        body, name=name, in_specs=[_ANY], out_specs=_ANY,
        out_shape=jax.ShapeDtypeStruct((8, r, c), block.dtype),
        scratch_shapes=[pltpu.SemaphoreType.DMA((7,)), pltpu.SemaphoreType.DMA((7,)), pltpu.SemaphoreType.DMA],
    )(block)


_HBM = pl.BlockSpec(memory_space=pltpu.HBM)
_SEM = pl.BlockSpec(memory_space=pltpu.SEMAPHORE)
_DATAFLOW = pltpu.SideEffectType.DATAFLOW_SIDE_EFFECTING
N_REMOTE = 6


def _remote_peers(x, y, cc, sibling=False):
    peers = [(1 - x, y, cc), (x, 1 - y, cc), (1 - x, 1 - y, cc),
             (1 - x, y, 1 - cc), (x, 1 - y, 1 - cc), (1 - x, 1 - y, 1 - cc)]
    return peers + [(x, y, 1 - cc)] if sibling else peers


def gather_start(block, after, name, sibling=False):
    r, c = block.shape
    n_peers = N_REMOTE + int(sibling)

    def body(x_ref, land_ref, after_ref, send_sems, recv_sems, x_thru, land_thru, token):
        x, y, cc = lax.axis_index("x"), lax.axis_index("y"), lax.axis_index("c")
        for k, peer in enumerate(_remote_peers(x, y, cc, sibling)):
            pltpu.make_async_remote_copy(src_ref=x_ref, dst_ref=land_ref.at[4 * x + 2 * y + cc],
                                         send_sem=send_sems.at[k], recv_sem=recv_sems.at[k],
                                         device_id=peer, device_id_type=MESH).start()
        token[...] = jnp.zeros_like(token)

    land = pltpu.with_memory_space_constraint(lax.empty((8, r, c), block.dtype), pltpu.HBM)
    return pl.pallas_call(
        body, name=name,
        out_shape=(pltpu.SemaphoreType.DMA((n_peers,)), pltpu.SemaphoreType.DMA((n_peers,)),
                   pltpu.HBM((r, c), block.dtype), pltpu.HBM((8, r, c), block.dtype),
                   jax.ShapeDtypeStruct((8, LANES), F32)),
        in_specs=(_HBM, _HBM, _ANY), out_specs=(_SEM, _SEM, _HBM, _HBM, pl.BlockSpec(memory_space=pltpu.VMEM)),
        input_output_aliases={0: 2, 1: 3},
        compiler_params=pltpu.CompilerParams(has_side_effects=_DATAFLOW),
    )(pltpu.with_memory_space_constraint(block, pltpu.HBM), land, after)


def gather_wait(send_sems, recv_sems, block_thru, land_thru, after, name, sibling=False):
    def body(x_ref, land_ref, send_sems, recv_sems, after_ref, x_dead, got_ref):
        x, y, cc = lax.axis_index("x"), lax.axis_index("y"), lax.axis_index("c")
        for k, (px, py, pc) in enumerate(_remote_peers(x, y, cc, sibling)):
            cp = pltpu.make_async_remote_copy(src_ref=x_ref, dst_ref=land_ref.at[4 * px + 2 * py + pc],
                                              send_sem=send_sems.at[k], recv_sem=recv_sems.at[k],
                                              device_id=(px, py, pc), device_id_type=MESH)
            cp.wait_send()
            cp.wait_recv()

    return pl.pallas_call(
        body, name=name,
        out_shape=(pltpu.HBM(block_thru.shape, block_thru.dtype), pltpu.HBM(land_thru.shape, land_thru.dtype)),
        in_specs=(_HBM, _HBM, _SEM, _SEM, _ANY), out_specs=(_HBM, _HBM), input_output_aliases={0: 0, 1: 1},
        compiler_params=pltpu.CompilerParams(has_side_effects=_DATAFLOW),
    )(block_thru, land_thru, send_sems, recv_sems, after)[1]


def _chip_peers(x, y):
    return [(1 - x, y), (x, 1 - y), (1 - x, 1 - y)]


def exchange_start(parts, name):
    def body(p_ref, land_ref, send_sems, recv_sems, p_thru, land_thru, token):
        x, y, cc = lax.axis_index("x"), lax.axis_index("y"), lax.axis_index("c")
        for k, (px, py) in enumerate(_chip_peers(x, y)):
            pltpu.make_async_remote_copy(src_ref=p_ref.at[2 * px + py], dst_ref=land_ref.at[2 * x + y],
                                         send_sem=send_sems.at[k], recv_sem=recv_sems.at[k],
                                         device_id=(px, py, cc), device_id_type=MESH).start()
        token[...] = jnp.zeros_like(token)

    land = pltpu.with_memory_space_constraint(lax.empty(parts.shape, parts.dtype), pltpu.HBM)
    return pl.pallas_call(
        body, name=name,
        out_shape=(pltpu.SemaphoreType.DMA((3,)), pltpu.SemaphoreType.DMA((3,)),
                   pltpu.HBM(parts.shape, parts.dtype), pltpu.HBM(parts.shape, parts.dtype),
                   jax.ShapeDtypeStruct((8, LANES), F32)),
        in_specs=(_HBM, _HBM), out_specs=(_SEM, _SEM, _HBM, _HBM, pl.BlockSpec(memory_space=pltpu.VMEM)),
        input_output_aliases={0: 2, 1: 3},
        compiler_params=pltpu.CompilerParams(has_side_effects=_DATAFLOW),
    )(pltpu.with_memory_space_constraint(parts, pltpu.HBM), land)


def exchange_wait(send_sems, recv_sems, parts_thru, land_thru, after, name):
    def body(p_ref, land_ref, send_sems, recv_sems, after_ref, p_dead, got_ref):
        x, y, cc = lax.axis_index("x"), lax.axis_index("y"), lax.axis_index("c")
        for k, (px, py) in enumerate(_chip_peers(x, y)):
            cp = pltpu.make_async_remote_copy(src_ref=p_ref.at[2 * px + py], dst_ref=land_ref.at[2 * px + py],
                                              send_sem=send_sems.at[k], recv_sem=recv_sems.at[k],
                                              device_id=(px, py, cc), device_id_type=MESH)
            cp.wait_send()
            cp.wait_recv()

    return pl.pallas_call(
        body, name=name,
        out_shape=(pltpu.HBM(parts_thru.shape, parts_thru.dtype), pltpu.HBM(land_thru.shape, land_thru.dtype)),
        in_specs=(_HBM, _HBM, _SEM, _SEM, _ANY), out_specs=(_HBM, _HBM), input_output_aliases={0: 0, 1: 1},
        compiler_params=pltpu.CompilerParams(has_side_effects=_DATAFLOW),
    )(parts_thru, land_thru, send_sems, recv_sems, after)


def sibling_swap(block, name):
    def body(x_ref, out_ref, send_sem, recv_sem):
        x, y, cc = lax.axis_index("x"), lax.axis_index("y"), lax.axis_index("c")
        cp = pltpu.make_async_remote_copy(src_ref=x_ref, dst_ref=out_ref, send_sem=send_sem, recv_sem=recv_sem,
                                          device_id=(x, y, 1 - cc), device_id_type=MESH)
        cp.start()
        cp.wait()

    return pl.pallas_call(
        body, name=name, in_specs=[_ANY], out_specs=_ANY,
        out_shape=jax.ShapeDtypeStruct(block.shape, block.dtype),
        scratch_shapes=[pltpu.SemaphoreType.DMA, pltpu.SemaphoreType.DMA],
    )(block)


PACK_COLS = 1024
SHARDED = (("l0_w_in", 1), ("l0_w_uq", 1), ("l0_w_ukv", 1), ("l0_w_out", 0), ("l0_w_up", 1), ("l0_w_down", 0),
           ("l1_w_in", 0), ("l1_w_glu", 1), ("l1_w_up", 1), ("l1_w_down", 0),
           ("l0_conv_w", 1), ("l0_ffn_conv_w", 1), ("l1_ffn_conv_w", 1))
REPLICATED = ("l0_mix_norm", "l0_conv_b", "l0_conv_ln_g", "l0_conv_ln_b", "l0_q_norm", "l0_kv_norm", "l0_ffn_norm",
              "l0_ffn_conv_b", "l1_mix_norm", "l1_log_dt", "l1_a_re", "l1_a_im", "l1_b_re", "l1_b_im", "l1_c_re",
              "l1_c_im", "l1_d", "l1_b_glu", "l1_ffn_norm", "l1_ffn_conv_b", "final_norm")


def _pack(arrs, dtype, mult):
    flat = jnp.concatenate([a.reshape(-1).astype(dtype) for a in arrs])
    n = flat.shape[0]
    total = -(-n // mult) * mult
    return jnp.pad(flat, (0, total - n))


def _unpack(flat, shapes):
    out, pos = [], 0
    for shp in shapes:
        n = int(np.prod(shp))
        out.append(flat[pos:pos + n].reshape(shp))
        pos += n
    return out


PACK_ROW_ALIGN = 16


def _pack_rows(arrs, dtype, row_mult):
    parts = []
    for a in arrs:
        n = int(np.prod(a.shape))
        rows = -(-n // PACK_COLS)
        if n % PACK_COLS == 0:
            r = a.astype(dtype).reshape(rows, PACK_COLS)
        else:
            r = jnp.pad(a.reshape(-1).astype(dtype), (0, rows * PACK_COLS - n)).reshape(rows, PACK_COLS)
        parts.append(jnp.pad(r, ((0, (-rows) % PACK_ROW_ALIGN), (0, 0))))
    p = jnp.concatenate(parts)
    return jnp.pad(p, ((0, (-p.shape[0]) % row_mult), (0, 0)))


def _unpack_rows(pack, shapes):
    out, r0 = [], 0
    for shp in shapes:
        n = int(np.prod(shp))
        rows = -(-n // PACK_COLS)
        piece = lax.optimization_barrier(pack[r0:r0 + rows])
        out.append(piece.reshape(shp) if n % PACK_COLS == 0 else piece.reshape(-1)[:n].reshape(shp))
        r0 += rows + (-rows) % PACK_ROW_ALIGN
    return out


def _shard(full, axis, j):
    n = full.shape[axis] // N_CHIPS
    return lax.slice_in_dim(full, j * n, (j + 1) * n, axis=axis)


def _block_diag(t):
    q, g, a, b = t.shape
    eye = jnp.eye(g, dtype=t.dtype)
    return jnp.einsum("qgab,gh->qgahb", t, eye).reshape(q, g * a, g * b)


def _block_diag_t(d, a, b):
    q = d.shape[0]
    d5 = d.reshape(q, 8, a, 8, b)
    eye = jnp.eye(8, dtype=d.dtype)
    return jnp.einsum("qgahb,gh->qgab", d5, eye)


def kernel(x, l0_mix_norm, l0_w_in, l0_conv_w, l0_conv_b, l0_conv_ln_g, l0_conv_ln_b, l0_q_norm, l0_kv_norm, l0_w_uq, l0_w_ukv, l0_w_out, l0_ffn_norm, l0_w_up, l0_ffn_conv_w, l0_ffn_conv_b, l0_w_down, l1_mix_norm, l1_w_in, l1_log_dt, l1_a_re, l1_a_im, l1_b_re, l1_b_im, l1_c_re, l1_c_im, l1_d, l1_w_glu, l1_b_glu, l1_ffn_norm, l1_w_up, l1_ffn_conv_w, l1_ffn_conv_b, l1_w_down, final_norm, loss_target, m_l0_mix_norm, m_l0_w_in, m_l0_conv_w, m_l0_conv_b, m_l0_conv_ln_g, m_l0_conv_ln_b, m_l0_q_norm, m_l0_kv_norm, m_l0_w_uq, m_l0_w_ukv, m_l0_w_out, m_l0_ffn_norm, m_l0_w_up, m_l0_ffn_conv_w, m_l0_ffn_conv_b, m_l0_w_down, m_l1_mix_norm, m_l1_w_in, m_l1_log_dt, m_l1_a_re, m_l1_a_im, m_l1_b_re, m_l1_b_im, m_l1_c_re, m_l1_c_im, m_l1_d, m_l1_w_glu, m_l1_b_glu, m_l1_ffn_norm, m_l1_w_up, m_l1_ffn_conv_w, m_l1_ffn_conv_b, m_l1_w_down, m_final_norm, v_l0_mix_norm, v_l0_w_in, v_l0_conv_w, v_l0_conv_b, v_l0_conv_ln_g, v_l0_conv_ln_b, v_l0_q_norm, v_l0_kv_norm, v_l0_w_uq, v_l0_w_ukv, v_l0_w_out, v_l0_ffn_norm, v_l0_w_up, v_l0_ffn_conv_w, v_l0_ffn_conv_b, v_l0_w_down, v_l1_mix_norm, v_l1_w_in, v_l1_log_dt, v_l1_a_re, v_l1_a_im, v_l1_b_re, v_l1_b_im, v_l1_c_re, v_l1_c_im, v_l1_d, v_l1_w_glu, v_l1_b_glu, v_l1_ffn_norm, v_l1_w_up, v_l1_ffn_conv_w, v_l1_ffn_conv_b, v_l1_w_down, v_final_norm):
    a = dict(locals())
    w = {n: a[n] for n in [s for s, _ in SHARDED] + list(REPLICATED)}
    mom = {n: a["m_" + n] for n in w}
    var = {n: a["v_" + n] for n in w}
    return _step(a["x"][0], a["loss_target"][0], w, mom, var)


FIRST_WEIGHTS = ("l0_w_in", "l0_w_uq", "l0_w_ukv", "l0_w_out")
LATER_WEIGHTS = ("l0_w_up", "l0_w_down", "l1_w_in", "l1_w_glu", "l1_w_up", "l1_w_down")


def _assemble(got, names, w):
    got = got.reshape(N_CHIPS, -1, PACK_COLS)
    shapes = [w[n].shape for n in names]
    per_chip = [_unpack_rows(got[j], shapes) for j in range(N_CHIPS)]
    axes = dict(SHARDED)
    return {n: jnp.concatenate([per_chip[j][k] for j in range(N_CHIPS)], axis=axes[n]) for k, n in enumerate(names)}


def _gather_weights(w):
    cc = lax.axis_index("c")
    small = [n for n, _ in SHARDED[10:]]
    full = {}
    for names, dtype, mult in ((FIRST_WEIGHTS, BF16, 2 * 256), (small, F32, 2 * PACK_ROW_ALIGN)):
        pack = _pack_rows([w[n] for n in names], dtype, mult)
        half = lax.dynamic_index_in_dim(pack.reshape(2, -1, PACK_COLS), cc, axis=0, keepdims=False)
        got = all_gather8(half, "gather_" + ("first_matrices" if dtype == BF16 else "conv_weights"))
        full.update(_assemble(got, names, w))
    pack = _pack_rows([w[n] for n in LATER_WEIGHTS], BF16, 2 * 256).reshape(2, -1, PACK_COLS)
    half = lax.dynamic_index_in_dim(pack, cc, axis=0, keepdims=False)
    send_sems, recv_sems, half_thru, land_thru, token = gather_start(half, got, "gather_later_start")
    return full, (send_sems, recv_sems, half_thru, land_thru, pack), token[0, 0]


def _finish_gather(pending, after, w):
    send_sems, recv_sems, half_thru, land_thru, pack = pending
    got = gather_wait(send_sems, recv_sems, half_thru, land_thru, after, "gather_later_wait")
    chip = 2 * lax.axis_index("x") + lax.axis_index("y")
    got = lax.dynamic_update_slice(got, pack, (2 * chip, 0, 0))
    return _assemble(got, LATER_WEIGHTS, w)


def _reduce_begin(grads, names, tag):
    cc = lax.axis_index("c")
    axes = dict(SHARDED)
    packs = [_pack_rows([_shard(grads[n], axes[n], j) for n in names], BF16, 2 * 256) for j in range(N_CHIPS)]
    g = jnp.stack(packs).reshape(N_CHIPS, 2, -1, PACK_COLS)
    keep = lax.dynamic_index_in_dim(g, cc, axis=1, keepdims=False)
    give = lax.dynamic_index_in_dim(g, 1 - cc, axis=1, keepdims=False)
    got = sibling_swap(give, f"grad_swap_halves_{tag}")
    parts = add_to_bf16(keep, got, f"grad_add_sibling_{tag}")
    send_sems, recv_sems, parts_thru, land_thru, token = exchange_start(parts, f"grad_exchange_start_{tag}")
    shapes = [_shard(grads[n], axes[n], 0).shape for n in names]
    return (send_sems, recv_sems, parts_thru, land_thru, list(names), shapes), token[0, 0]


def _reduce_end(state, after, tag):
    send_sems, recv_sems, parts_thru, land_thru, names, shapes = state
    cc = lax.axis_index("c")
    chip = 2 * lax.axis_index("x") + lax.axis_index("y")
    parts, landed = exchange_wait(send_sems, recv_sems, parts_thru, land_thru, after, f"grad_exchange_wait_{tag}")
    own = lax.dynamic_index_in_dim(parts, chip, axis=0, keepdims=True)
    landed = lax.dynamic_update_slice(landed, own, (chip, 0, 0))
    mine = sum_leading(landed, f"grad_sum_chips_{tag}")
    theirs = sibling_swap(mine, f"grad_swap_sums_{tag}")
    lo = jnp.where(cc == 0, mine, theirs)
    hi = jnp.where(cc == 0, theirs, mine)
    return dict(zip(names, _unpack_rows(jnp.concatenate([lo, hi]), shapes)))


def _reduce_replicated(grads, loss_row, finish_sharded):
    names = list(REPLICATED)
    flat = _pack([grads[n] for n in names] + [loss_row], F32, 256 * LANES).reshape(-1, LANES)
    send_sems, recv_sems, flat_thru, land_thru, _ = gather_start(flat, loss_row, "small_grads_start", sibling=True)
    sharded = finish_sharded()
    got = gather_wait(send_sems, recv_sems, flat_thru, land_thru, next(iter(sharded.values())), "small_grads_wait",
                      sibling=True)
    me = 4 * lax.axis_index("x") + 2 * lax.axis_index("y") + lax.axis_index("c")
    got = lax.dynamic_update_slice(got, flat[None], (me, 0, 0))
    tot = sum_leading(got, "sum_small_grads").reshape(-1)
    parts = _unpack(tot, [grads[n].shape for n in names] + [loss_row.shape])
    return dict(zip(names, parts[:-1])), parts[-1][0], sharded


def _row(v):
    return v.reshape(1, -1).astype(F32)


def _pad_rows(wt, rows):
    return jnp.pad(wt.astype(F32), ((0, rows - wt.shape[0]), (0, 0)))


def _ffn_fwd(xin, g, wa, wb, cw, cb, wd, tag):
    cwa, cwb = _pad_rows(cw[:, :D_FF], 8), _pad_rows(cw[:, D_FF:], 8)
    xout, xn, hpa, hpb, act = ffn_fwd(xin, _row(g), wa, wb, cwa, cwb, _row(cb[:D_FF]), _row(cb[D_FF:]), wd, tag)
    return xout, (xin, xn, hpa, hpb, act)


def _ffn_bwd(dxout, saved, g, wa, wb, cw, cb, wd, tag, zero=0.0):
    xin, xn, hpa, hpb, act = saved
    d_wd = matmul(act, dxout, ta=True, name=f"{tag}_d_wdown")
    cwa, cwb = _pad_rows(cw[:, :D_FF], 8), _pad_rows(cw[:, D_FF:], 8)
    dxin, dpa, dpb, dwa, dwb, dba, dbb, dg = ffn_bwd(dxout, xin, _row(g), hpa, hpb, wa, wb, cwa, cwb,
                                                     _row(cb[:D_FF]) + zero, _row(cb[D_FF:]), wd, tag + "_bwd")
    d_wu = jnp.concatenate([matmul(xn, dpa, ta=True, name=f"{tag}_d_wup_a"),
                            matmul(xn, dpb, ta=True, name=f"{tag}_d_wup_b")], axis=1)
    taps = lambda t: t.transpose(1, 0, 2).reshape(8, -1)
    d_cw = jnp.concatenate([taps(dwa)[:FFN_K], taps(dwb)[:FFN_K]], axis=1)
    d_cb = jnp.concatenate([taps(dba)[0], taps(dbb)[0]])
    return dxin, dg[0], d_wu, d_cw, d_cb, d_wd


def _step(x, target, w, mom, var):
    s = x.shape[0]
    full, pending, zero = _gather_weights(w)
    cos, sin = rope_tables(s)

    w_in0 = full["l0_w_in"]
    w_in0p = jnp.concatenate([w_in0, jnp.zeros((D_MODEL, H0_W - w_in0.shape[1]), BF16)], axis=1)
    wq = full["l0_w_uq"].reshape(Q_LORA, N_HEADS, QK_NOPE + QK_ROPE)
    zq = lambda n: jnp.zeros((Q_LORA, N_HEADS, n), BF16)
    w_uqp = jnp.concatenate([wq[..., :QK_NOPE], zq(LANES - QK_NOPE), wq[..., QK_NOPE:], zq(LANES - QK_ROPE)],
                            axis=-1).reshape(Q_LORA, N_HEADS * HEAD_PAD)
    w_ukv = full["l0_w_ukv"]
    w_out = full["l0_w_out"]
    w_out_u = w_out[:CONV_WIDTH]
    wo = w_out[CONV_WIDTH:].reshape(N_HEADS, V_DIM, D_MODEL)
    w_out_a = jnp.concatenate([jnp.zeros_like(wo), wo], axis=1).reshape(N_HEADS * LANES, D_MODEL)
    conv_w = _pad_rows(full["l0_conv_w"], CONV_HALO)

    xn0 = rms_fwd(x, _row(w["l0_mix_norm"]) + zero, "l0_mix_rms")
    h0 = matmul(xn0, w_in0p, name="l0_in_proj")
    qn_g, kvn_g = _row(w["l0_q_norm"]), _row(w["l0_kv_norm"])
    u0, cq, ckv, kr = mixpre_fwd(h0, qn_g, kvn_g, cos, sin)
    cb, lg, lb = _row(w["l0_conv_b"]), _row(w["l0_conv_ln_g"]), _row(w["l0_conv_ln_b"])
    u = convln_fwd(u0, conv_w, cb, lg, lb)
    q = q_up_rope(cq, w_uqp, cos, sin)
    kv = matmul(ckv, w_ukv, out_dtype=BF16, name="l0_kv_up")
    o, lse = attn_fwd(q, kv, kr)
    x1 = matmul(u, w_out_u, res=x, name="l0_out_conv")
    x1 = matmul(o, w_out_a, res=x1, name="l0_out_attn")
    full.update(_finish_gather(pending, x1, w))
    w_up0a, w_up0b = full["l0_w_up"][:, :D_FF], full["l0_w_up"][:, D_FF:]
    w_up1a, w_up1b = full["l1_w_up"][:, :D_FF], full["l1_w_up"][:, D_FF:]

    x2, ffn0 = _ffn_fwd(x1, w["l0_ffn_norm"], w_up0a, w_up0b, full["l0_ffn_conv_w"], w["l0_ffn_conv_b"],
                        full["l0_w_down"], "l0_ffn")

    g_, p_, c_ = SSM_GROUPS, SSM_STATE, SSM_GROUP
    s5_in = (w["l1_log_dt"].reshape(g_, 1), w["l1_a_re"], w["l1_a_im"],
             w["l1_b_re"].reshape(g_, p_ * c_), w["l1_b_im"].reshape(g_, p_ * c_))
    lam_r, lam_i, bb_r, bb_i = s5_params_fwd(*s5_in)
    lam_rf, lam_if = lam_r.reshape(1, NS), lam_i.reshape(1, NS)

    def b_blocks(bb):
        t = bb.reshape(NQ, 8, p_, c_).transpose(0, 1, 3, 2)
        return _block_diag(t).astype(BF16)

    def c_blocks(cm):
        t = cm.reshape(NQ, 8, c_, p_).transpose(0, 1, 3, 2)
        return _block_diag(t).astype(BF16)

    bre, bim = b_blocks(bb_r), b_blocks(bb_i)
    cre, cim = c_blocks(w["l1_c_re"]), c_blocks(w["l1_c_im"])
    dskip = _row(w["l1_d"])
    xn2 = rms_fwd(x2, _row(w["l1_mix_norm"]), "l1_mix_rms")
    u1 = matmul(xn2, full["l1_w_in"], name="l1_in_proj")
    xs_r, xs_i, y1, yg = s5_scan_fwd(u1, lam_rf, lam_if, bre, bim, cre, cim, dskip)
    x3, z = glu_proj_res(yg, full["l1_w_glu"], _row(w["l1_b_glu"]), x2)

    x4, ffn1 = _ffn_fwd(x3, w["l1_ffn_norm"], w_up1a, w_up1b, full["l1_ffn_conv_w"], w["l1_ffn_conv_b"],
                        full["l1_w_down"], "l1_ffn")
    loss_part, dx4, dgf = loss_head(x4, _row(w["final_norm"]), target)

    gr = {"final_norm": dgf[0]}

    dx3, gr["l1_ffn_norm"], gr["l1_w_up"], gr["l1_ffn_conv_w"], gr["l1_ffn_conv_b"], gr["l1_w_down"] = _ffn_bwd(
        dx4, ffn1, w["l1_ffn_norm"], w_up1a, w_up1b, full["l1_ffn_conv_w"], w["l1_ffn_conv_b"], full["l1_w_down"],
        "l1_ffn")

    dz, dbga, dbgb = glu_bwd(z, dx3)
    gr["l1_b_glu"] = jnp.concatenate([dbga[0], dbgb[0]])
    dyg = matmul(dz, full["l1_w_glu"], tb=True, name="l1_d_yg")
    gr["l1_w_glu"] = matmul(yg, dz, ta=True, name="l1_d_wglu")
    a_r, a_i, dy1 = s5_scan_bwd(dyg, y1, lam_rf, lam_if, cre, cim)
    du1, dlr, dli, dbr, dbi, dcr, dci, dd = s5_grads(u1, dy1, xs_r, xs_i, a_r, a_i, bre, bim, dskip)
    gr["l1_d"] = dd[0]

    def b_unblock(d):
        return _block_diag_t(d, c_, p_).transpose(0, 1, 3, 2).reshape(g_, p_ * c_)

    def c_unblock(d):
        return _block_diag_t(d, p_, c_).transpose(0, 1, 3, 2).reshape(g_, c_, p_)

    gr["l1_c_re"], gr["l1_c_im"] = c_unblock(dcr), c_unblock(dci)
    dld, dar, dai, dbre, dbim = s5_params_bwd(*s5_in, dlr[0].reshape(g_, p_), dli[0].reshape(g_, p_),
                                              b_unblock(dbr), b_unblock(dbi))
    gr["l1_log_dt"], gr["l1_a_re"], gr["l1_a_im"] = dld.reshape(g_), dar, dai
    gr["l1_b_re"], gr["l1_b_im"] = dbre.reshape(g_, p_, c_), dbim.reshape(g_, p_, c_)
    dxn2 = matmul(du1, full["l1_w_in"], tb=True, name="l1_d_xn")
    gr["l1_w_in"] = matmul(xn2, du1, ta=True, name="l1_d_win")
    dx2, dg = rms_bwd(x2, _row(w["l1_mix_norm"]), dxn2, dx3, "l1_mix_rms_bwd")
    gr["l1_mix_norm"] = dg[0]
    red_a, zero_a = _reduce_begin(gr, ("l1_w_up", "l1_w_down", "l1_w_glu", "l1_w_in"), "a")

    dx1, gr["l0_ffn_norm"], gr["l0_w_up"], gr["l0_ffn_conv_w"], gr["l0_ffn_conv_b"], gr["l0_w_down"] = _ffn_bwd(
        dx2, ffn0, w["l0_ffn_norm"], w_up0a, w_up0b, full["l0_ffn_conv_w"], w["l0_ffn_conv_b"], full["l0_w_down"],
        "l0_ffn", zero_a)
    red_b, zero_b = _reduce_begin(gr, ("l0_w_up", "l0_w_down"), "b")

    du = matmul(dx1, w_out_u + zero_b.astype(BF16), tb=True, out_dtype=BF16, name="l0_d_u")
    do = matmul(dx1, w_out_a, tb=True, out_dtype=BF16, name="l0_d_o")
    d_wout_u = matmul(u, dx1, ta=True, name="l0_d_wout_u")
    d_wout_a = matmul(o, dx1, ta=True, name="l0_d_wout_a")
    gr["l0_w_out"] = jnp.concatenate(
        [d_wout_u, d_wout_a.reshape(N_HEADS, LANES, D_MODEL)[:, LANES - V_DIM:].reshape(N_HEADS * V_DIM, D_MODEL)])
    dq, dkv, dkr = attn_bwd(q, kv, kr, o, do, lse)
    dqraw = qrope_bwd(dq, cos, sin)
    dcq = matmul(dqraw, w_uqp, tb=True, name="l0_d_cq")
    d_wuqp = matmul(cq, dqraw, ta=True, name="l0_d_wuq").reshape(Q_LORA, N_HEADS, HEAD_PAD)
    gr["l0_w_uq"] = jnp.concatenate([d_wuqp[..., :QK_NOPE], d_wuqp[..., LANES:LANES + QK_ROPE]],
                                    axis=-1).reshape(Q_LORA, -1)
    dckv = matmul(dkv, w_ukv, tb=True, name="l0_d_ckv")
    gr["l0_w_ukv"] = matmul(ckv, dkv, ta=True, name="l0_d_wukv")
    du1c, dlg, dlb, dcb = convln_bwd1(u0, conv_w, cb, lg, lb, du)
    gr["l0_conv_ln_g"], gr["l0_conv_ln_b"], gr["l0_conv_b"] = dlg[0], dlb[0], dcb[0]
    du0, dcw = convln_bwd2(u0, conv_w, du1c)
    gr["l0_conv_w"] = dcw[:CONV_K]
    dh0, dqn, dkvn = mixpre_bwd(h0, qn_g, kvn_g, cos, sin, du0, dcq, dckv, dkr)
    gr["l0_q_norm"], gr["l0_kv_norm"] = dqn[0], dkvn[0]
    dxn0 = matmul(dh0, w_in0p, tb=True, name="l0_d_xn")
    gr["l0_w_in"] = matmul(xn0, dh0, ta=True, name="l0_d_win")[:, :w_in0.shape[1]]
    grad_x, dg = rms_bwd(x, _row(w["l0_mix_norm"]), dxn0, dx1, "l0_mix_rms_bwd")
    gr["l0_mix_norm"] = dg[0]

    rest = [n for n, _ in SHARDED if n not in red_a[-2] + red_b[-2]]
    red_c, _ = _reduce_begin(gr, rest, "c")
    finish = lambda: {**_reduce_end(red_a, grad_x, "a"), **_reduce_end(red_b, grad_x, "b"),
                      **_reduce_end(red_c, grad_x, "c")}
    g_rep, loss, g_sh = _reduce_replicated(gr, loss_part[0], finish)
    grad, delta, new_m, new_v = {}, {}, {}, {}
    for n, _ in SHARDED:
        shp = w[n].shape
        two_d = (lambda t: t.reshape(shp[0], -1))
        grad[n] = g_sh[n]
        delta[n], new_m[n], new_v[n] = adamw(two_d(w[n]), two_d(g_sh[n]), two_d(mom[n]), two_d(var[n]), f"adamw_{n}")
    names = list(REPLICATED)
    pk = lambda d: _pack([d[n] for n in names], F32, 256 * LANES).reshape(-1, LANES)
    dl, nm, nv = adamw(pk(w), pk(g_rep), pk(mom), pk(var), "adamw_small")
    shapes = [w[n].shape for n in names]
    for n, d_, m_, v_ in zip(names, _unpack(dl.reshape(-1), shapes), _unpack(nm.reshape(-1), shapes),
                             _unpack(nv.reshape(-1), shapes)):
        grad[n], delta[n], new_m[n], new_v[n] = g_rep[n], d_, m_, v_

    order = ["l0_mix_norm", "l0_w_in", "l0_conv_w", "l0_conv_b", "l0_conv_ln_g", "l0_conv_ln_b", "l0_q_norm",
             "l0_kv_norm", "l0_w_uq", "l0_w_ukv", "l0_w_out", "l0_ffn_norm", "l0_w_up", "l0_ffn_conv_w",
             "l0_ffn_conv_b", "l0_w_down", "l1_mix_norm", "l1_w_in", "l1_log_dt", "l1_a_re", "l1_a_im", "l1_b_re",
             "l1_b_im", "l1_c_re", "l1_c_im", "l1_d", "l1_w_glu", "l1_b_glu", "l1_ffn_norm", "l1_w_up",
             "l1_ffn_conv_w", "l1_ffn_conv_b", "l1_w_down", "final_norm"]
    return (loss, grad_x[None], *[grad[n] for n in order], *[delta[n] for n in order],
            *[new_m[n] for n in order], *[new_v[n] for n in order])
```
